```python
import functools
import jax, jax.numpy as jnp
from jax import lax
import numpy as np

D_MODEL = 2048
BATCH = 8
SEQ = 2048
DEPTH = 2

GRID_W = 64
CTX_LEN = 256
HEAD_DIM = 128
N_MIX_HEADS = D_MODEL // HEAD_DIM
ATT_Q_HEADS = N_MIX_HEADS // 2
ATT_KV_HEADS = ATT_Q_HEADS // 4
RET_HEADS = N_MIX_HEADS // 4
RET_DIM = HEAD_DIM
GLA_HEADS = N_MIX_HEADS // 4
GLA_DK = HEAD_DIM // 2
GLA_DV = HEAD_DIM
GLA_GATE_RANK = 16
GLA_TAU = 16.0
D_FF = 5632
ROPE_THETA = 10000.0
Q_BLOCK = 128
RET_CHUNK = 128
GLA_CHUNK = 64
N_MOD = 6
EPS = 1e-6
ATT_W = ATT_Q_HEADS * HEAD_DIM
ATT_KV_W = ATT_KV_HEADS * HEAD_DIM
RET_W = RET_HEADS * RET_DIM
GLA_K_W = GLA_HEADS * GLA_DK
GLA_V_W = GLA_HEADS * GLA_DV
IN_SPLITS = (ATT_W, ATT_KV_W, ATT_KV_W, RET_W, RET_W, RET_W, RET_W, GLA_K_W, GLA_K_W, GLA_V_W, GLA_V_W, 2 * GLA_GATE_RANK)
N_IN = ATT_W + 2 * ATT_KV_W + 4 * RET_W + 2 * GLA_K_W + 2 * GLA_V_W + 2 * GLA_GATE_RANK

kernel_name = "hymba_style_diffusion_hybrid_attn_retnet_gla"


def rms_norm(x, g):
    xf = x.astype(jnp.float32)
    y = xf * lax.rsqrt(jnp.mean(xf * xf, axis=-1, keepdims=True) + EPS)
    return (y * g.astype(jnp.float32)).astype(x.dtype)


def modulate(h, shift, scale):
    return h * (1 + scale) + shift


def split_heads(t, n_heads):
    B, L, _ = t.shape
    return t.reshape(B, L, n_heads, -1).transpose(0, 2, 1, 3)


def merge_heads(t):
    B, H, L, d = t.shape
    return t.transpose(0, 2, 1, 3).reshape(B, L, H * d)


def axial_rope(n_tokens):
    rows = n_tokens // GRID_W
    row = jnp.repeat(jnp.arange(rows, dtype=jnp.float32), GRID_W)
    col = jnp.tile(jnp.arange(GRID_W, dtype=jnp.float32), rows)
    n_freq = HEAD_DIM // 4
    inv_freq = ROPE_THETA ** (-jnp.arange(n_freq, dtype=jnp.float32) / n_freq)
    ang = jnp.concatenate([row[:, None] * inv_freq, col[:, None] * inv_freq], axis=-1)
    return jnp.cos(ang), jnp.sin(ang)


def apply_rope(t, cos, sin):
    half = t.shape[-1] // 2
    t1, t2 = t[..., :half], t[..., half:]
    cos = cos.astype(t.dtype)
    sin = sin.astype(t.dtype)
    return jnp.concatenate([t1 * cos - t2 * sin, t1 * sin + t2 * cos], axis=-1)


def project_inputs(h, w_in, q_norm_g, k_norm_g, gla_gate_up, gla_gate_b, rope):
    B, L, _ = h.shape
    f32 = jnp.float32
    z = h @ w_in
    cuts = np.cumsum(IN_SPLITS)[:-1].tolist()
    aq, ak, av, rq, rk, rv, rg, gq, gk, gv, gr, ga = jnp.split(z, cuts, axis=-1)
    aq = rms_norm(split_heads(aq, ATT_Q_HEADS), q_norm_g)
    ak = rms_norm(split_heads(ak, ATT_KV_HEADS), k_norm_g)
    rq = split_heads(rq, RET_HEADS)
    rk = split_heads(rk, RET_HEADS) * (RET_DIM ** -0.5)
    if rope is not None:
        cos, sin = rope
        aq, ak, rq, rk = (apply_rope(t, cos, sin) for t in (aq, ak, rq, rk))
    logit = jnp.einsum('blnr,nrk->nblk', ga.reshape(B, L, 2, GLA_GATE_RANK), gla_gate_up) + gla_gate_b[:, None, None, :]
    log_a = jax.nn.log_sigmoid(logit.astype(f32)) / GLA_TAU
    return dict(
        aq=aq, ak=ak, av=split_heads(av, ATT_KV_HEADS),
        rq=rq.astype(f32), rk=rk.astype(f32), rv=split_heads(rv, RET_HEADS).astype(f32), rg=rg,
        gq=(split_heads(gq, GLA_HEADS) * (GLA_DK ** -0.5)).astype(f32),
        gk=split_heads(gk, GLA_HEADS).astype(f32), gv=split_heads(gv, GLA_HEADS).astype(f32), gr=gr,
        la_f=split_heads(log_a[0], GLA_HEADS), la_b=split_heads(log_a[1], GLA_HEADS))


def softmax_attention(q, k, v):
    B, Hq, Lq, d = q.shape
    Hkv = k.shape[1]
    G = Hq // Hkv
    nb = Lq // Q_BLOCK
    qb = q.reshape(B, Hkv, G, nb, Q_BLOCK, d).transpose(3, 0, 1, 2, 4, 5)
    scale = d ** -0.5

    def block(qi):
        s = jnp.einsum('bkgqd,bksd->bkgqs', qi, k).astype(jnp.float32) * scale
        p = jax.nn.softmax(s, axis=-1).astype(v.dtype)
        return jnp.einsum('bkgqs,bksd->bkgqd', p, v)

    o = lax.map(block, qb)
    return o.transpose(1, 2, 3, 0, 4, 5).reshape(B, Hq, Lq, d)


def retention_scan(log_g, q, k, v, s0):
    B, H, L, _ = q.shape
    C = RET_CHUNK
    n = L // C
    f32 = jnp.float32
    lg = log_g.astype(f32)[:, None]
    idx = jnp.arange(C, dtype=f32)
    rel = idx[:, None] - idx[None, :]
    causal = rel >= 0
    d_in = jnp.where(causal, jnp.exp(lg[:, :, None] * jnp.where(causal, rel, 0.0)), 0.0)
    q_dec = jnp.exp(lg * (idx + 1))[..., None]
    k_dec = jnp.exp(lg * (C - 1 - idx))[..., None]
    c_dec = jnp.exp(lg * C)[..., None]

    def chunks(t):
        return jnp.moveaxis(t.reshape(B, H, n, C, t.shape[-1]), 2, 0)

    def step(S, inp):
        qc, kc, vc = inp
        att = jnp.einsum('bhid,bhjd->bhij', qc, kc) * d_in
        o = jnp.einsum('bhij,bhjv->bhiv', att, vc) + jnp.einsum('bhid,bhdv->bhiv', qc * q_dec, S)
        S = c_dec * S + jnp.einsum('bhjd,bhjv->bhdv', kc * k_dec, vc)
        return S, o

    S, o = lax.scan(step, s0, (chunks(q), chunks(k), chunks(v)))
    return jnp.moveaxis(o, 0, 2).reshape(B, H, L, -1), S


def gla_scan(q, k, v, log_a, s0):
    B, H, L, _ = q.shape
    C = GLA_CHUNK
    n = L // C
    causal = jnp.tril(jnp.ones((C, C), dtype=bool))[:, :, None]

    def chunks(t):
        return jnp.moveaxis(t.reshape(B, H, n, C, t.shape[-1]), 2, 0)

    def step(S, inp):
        qc, kc, vc, ac = inp
        b = jnp.cumsum(ac, axis=2)
        diff = b[:, :, :, None, :] - b[:, :, None, :, :]
        dec = jnp.where(causal, jnp.exp(jnp.where(causal, diff, 0.0)), 0.0)
        att = jnp.einsum('bhid,bhjd,bhijd->bhij', qc, kc, dec)
        o = jnp.einsum('bhij,bhjv->bhiv', att, vc) + jnp.einsum('bhid,bhdv->bhiv', qc * jnp.exp(b), S)
        b_end = b[:, :, -1:, :]
        S = jnp.exp(b_end[:, :, 0, :, None]) * S + jnp.einsum('bhjd,bhjv->bhdv', kc * jnp.exp(b_end - b), vc)
        return S, o

    S, o = lax.scan(step, s0, (chunks(q), chunks(k), chunks(v), chunks(log_a)))
    return jnp.moveaxis(o, 0, 2).reshape(B, H, L, -1), S


def run_bidirectional(scan_f, scan_b, ctx_f, ctx_b, lat_f, lat_b, s0):
    flip = lambda ts: tuple(jnp.flip(t, axis=2) for t in ts)
    o_cf, s_cf = scan_f(*ctx_f, s0)
    o_cb, s_cb = scan_b(*flip(ctx_b), s0)
    o_lf, _ = scan_f(*lat_f, s_cf)
    o_lb, _ = scan_b(*flip(lat_b), s_cb)
    return o_cf + jnp.flip(o_cb, axis=2), o_lf + jnp.flip(o_lb, axis=2)


def mixer_output(o_att, o_ret, g_ret, o_gla, g_gla, ret_norm_g, gla_norm_g, w_out):
    dt = g_ret.dtype
    att = merge_heads(o_att).astype(dt)
    ret = merge_heads(rms_norm(o_ret, ret_norm_g)).astype(dt) * jax.nn.silu(g_ret)
    gla = merge_heads(rms_norm(o_gla, gla_norm_g)).astype(dt) * jax.nn.silu(g_gla)
    return jnp.concatenate([att, ret, gla], axis=-1) @ w_out


def dwconv3(u, w, b):
    up = jnp.pad(u, ((0, 0), (1, 1), (0, 0)))
    return up[:, :-2] * w[0] + up[:, 1:-1] * w[1] + up[:, 2:] * w[2] + b


def conv_glu(h, w_up, conv_w, conv_b, w_down):
    a, v = jnp.split(h @ w_up, 2, axis=-1)
    return (jax.nn.silu(dwconv3(a, conv_w, conv_b)) * v) @ w_down


def _fwd_setup_inputs(seed: int = 0) -> dict:
    key = jax.random.key(seed)
    ks = jax.random.split(key, 22)
    f32 = jnp.float32

    def nrm(k, shape, scale):
        return jax.random.normal(k, shape, f32) * scale

    def gain(k, shape):
        return 1.0 + 0.02 * jax.random.normal(k, shape, f32)

    decay0 = jnp.log(1.0 - 2.0 ** (-5.0 - jnp.arange(RET_HEADS, dtype=f32)))
    return {
        "x": nrm(ks[0], (BATCH, SEQ, D_MODEL), 1.0),
        "c": nrm(ks[1], (BATCH, D_MODEL), 1.0),
        "ctx": nrm(ks[2], (BATCH, CTX_LEN, D_MODEL), 1.0),
        "c_ctx": nrm(ks[3], (D_MODEL,), 1.0),
        "ada_w": nrm(ks[4], (DEPTH, D_MODEL, N_MOD * D_MODEL), 0.5 * D_MODEL ** -0.5),
        "ada_b": nrm(ks[5], (DEPTH, N_MOD * D_MODEL), 0.01),
        "norm1_g": gain(ks[6], (DEPTH, D_MODEL)),
        "w_in": nrm(ks[7], (DEPTH, D_MODEL, N_IN), D_MODEL ** -0.5),
        "q_norm_g": gain(ks[8], (DEPTH, HEAD_DIM)),
        "k_norm_g": gain(ks[9], (DEPTH, HEAD_DIM)),
        "ret_log_decay": decay0 * (1.0 + 0.05 * jax.random.normal(ks[10], (DEPTH, 2, RET_HEADS), f32)),
        "ret_norm_g": gain(ks[11], (DEPTH, RET_DIM)),
        "gla_gate_up": nrm(ks[12], (DEPTH, 2, GLA_GATE_RANK, GLA_K_W), GLA_GATE_RANK ** -0.5),
        "gla_gate_b": nrm(ks[13], (DEPTH, 2, GLA_K_W), 0.1),
        "gla_norm_g": gain(ks[14], (DEPTH, GLA_DV)),
        "w_out": nrm(ks[15], (DEPTH, D_MODEL, D_MODEL), D_MODEL ** -0.5),
        "norm2_g": gain(ks[16], (DEPTH, D_MODEL)),
        "w_up": nrm(ks[17], (DEPTH, D_MODEL, 2 * D_FF), D_MODEL ** -0.5),
        "conv_w": nrm(ks[18], (DEPTH, 3, D_FF), 3 ** -0.5),
        "conv_b": nrm(ks[19], (DEPTH, D_FF), 0.01),
        "w_down": nrm(ks[20], (DEPTH, D_FF, D_MODEL), D_FF ** -0.5),
        "final_norm_g": gain(ks[21], (D_MODEL,)),
    }


def _fwd_reference(x, c, ctx, c_ctx, ada_w, ada_b, norm1_g, w_in, q_norm_g, k_norm_g, ret_log_decay, ret_norm_g, gla_gate_up, gla_gate_b, gla_norm_g, w_out, norm2_g, w_up, conv_w, conv_b, w_down, final_norm_g):
    B, L, D = x.shape
    f32 = jnp.float32
    rope = axial_rope(L)
    s0_ret = jnp.zeros((B, RET_HEADS, RET_DIM, RET_DIM), f32)
    s0_gla = jnp.zeros((B, GLA_HEADS, GLA_DK, GLA_DV), f32)
    xc = ctx
    for l in range(DEPTH):
        last = l == DEPTH - 1
        mod = (jax.nn.silu(c) @ ada_w[l] + ada_b[l]).reshape(B, N_MOD, 1, D)
        mod_c = (jax.nn.silu(c_ctx) @ ada_w[l] + ada_b[l]).reshape(N_MOD, 1, 1, D)

        h = modulate(rms_norm(x, norm1_g[l]), mod[:, 0], mod[:, 1])
        hc = modulate(rms_norm(xc, norm1_g[l]), mod_c[0], mod_c[1])
        lat = project_inputs(h, w_in[l], q_norm_g[l], k_norm_g[l], gla_gate_up[l], gla_gate_b[l], rope)
        cx = project_inputs(hc, w_in[l], q_norm_g[l], k_norm_g[l], gla_gate_up[l], gla_gate_b[l], None)

        k_all = jnp.concatenate([cx['ak'], lat['ak']], axis=2)
        v_all = jnp.concatenate([cx['av'], lat['av']], axis=2)
        att_lat = softmax_attention(lat['aq'], k_all, v_all)

        ret_ctx, ret_lat = run_bidirectional(
            functools.partial(retention_scan, ret_log_decay[l, 0]),
            functools.partial(retention_scan, ret_log_decay[l, 1]),
            (cx['rq'], cx['rk'], cx['rv']), (cx['rq'], cx['rk'], cx['rv']),
            (lat['rq'], lat['rk'], lat['rv']), (lat['rq'], lat['rk'], lat['rv']), s0_ret)

        gla_ctx, gla_lat = run_bidirectional(
            gla_scan, gla_scan,
            (cx['gq'], cx['gk'], cx['gv'], cx['la_f']), (cx['gq'], cx['gk'], cx['gv'], cx['la_b']),
            (lat['gq'], lat['gk'], lat['gv'], lat['la_f']), (lat['gq'], lat['gk'], lat['gv'], lat['la_b']), s0_gla)

        y = mixer_output(att_lat, ret_lat, lat['rg'], gla_lat, lat['gr'], ret_norm_g[l], gla_norm_g[l], w_out[l])
        x = x + mod[:, 2] * y

        h2 = modulate(rms_norm(x, norm2_g[l]), mod[:, 3], mod[:, 4])
        x = x + mod[:, 5] * conv_glu(h2, w_up[l], conv_w[l], conv_b[l], w_down[l])

        if not last:
            att_ctx = softmax_attention(cx['aq'], cx['ak'], cx['av'])
            yc = mixer_output(att_ctx, ret_ctx, cx['rg'], gla_ctx, cx['gr'], ret_norm_g[l], gla_norm_g[l], w_out[l])
            xc = xc + mod_c[2] * yc
            hc2 = modulate(rms_norm(xc, norm2_g[l]), mod_c[3], mod_c[4])
            xc = xc + mod_c[5] * conv_glu(hc2, w_up[l], conv_w[l], conv_b[l], w_down[l])
    return rms_norm(x, final_norm_g)


import jax as _jax
import jax.numpy as _jnp

TWIN_FORMAT = 'train_step'
FWD_PARAMS = ['x', 'c', 'ctx', 'c_ctx', 'ada_w', 'ada_b', 'norm1_g', 'w_in', 'q_norm_g', 'k_norm_g', 'ret_log_decay', 'ret_norm_g', 'gla_gate_up', 'gla_gate_b', 'gla_norm_g', 'w_out', 'norm2_g', 'w_up', 'conv_w', 'conv_b', 'w_down', 'final_norm_g']
TWIN_WEIGHTS = ['c_ctx', 'ada_w', 'ada_b', 'norm1_g', 'w_in', 'q_norm_g', 'k_norm_g', 'ret_log_decay', 'ret_norm_g', 'gla_gate_up', 'gla_gate_b', 'gla_norm_g', 'w_out', 'norm2_g', 'w_up', 'conv_w', 'conv_b', 'w_down', 'final_norm_g']
TWIN_DIFF_INPUT = 'x'
TWIN_INPUTS = ['x', 'c', 'ctx', 'c_ctx', 'ada_w', 'ada_b', 'norm1_g', 'w_in', 'q_norm_g', 'k_norm_g', 'ret_log_decay', 'ret_norm_g', 'gla_gate_up', 'gla_gate_b', 'gla_norm_g', 'w_out', 'norm2_g', 'w_up', 'conv_w', 'conv_b', 'w_down', 'final_norm_g', 'loss_target', 'm_c_ctx', 'm_ada_w', 'm_ada_b', 'm_norm1_g', 'm_w_in', 'm_q_norm_g', 'm_k_norm_g', 'm_ret_log_decay', 'm_ret_norm_g', 'm_gla_gate_up', 'm_gla_gate_b', 'm_gla_norm_g', 'm_w_out', 'm_norm2_g', 'm_w_up', 'm_conv_w', 'm_conv_b', 'm_w_down', 'm_final_norm_g', 'v_c_ctx', 'v_ada_w', 'v_ada_b', 'v_norm1_g', 'v_w_in', 'v_q_norm_g', 'v_k_norm_g', 'v_ret_log_decay', 'v_ret_norm_g', 'v_gla_gate_up', 'v_gla_gate_b', 'v_gla_norm_g', 'v_w_out', 'v_norm2_g', 'v_w_up', 'v_conv_w', 'v_conv_b', 'v_w_down', 'v_final_norm_g']
TWIN_OUTPUTS = ['loss', 'grad_x', 'grad_c_ctx', 'grad_ada_w', 'grad_ada_b', 'grad_norm1_g', 'grad_w_in', 'grad_q_norm_g', 'grad_k_norm_g', 'grad_ret_log_decay', 'grad_ret_norm_g', 'grad_gla_gate_up', 'grad_gla_gate_b', 'grad_gla_norm_g', 'grad_w_out', 'grad_norm2_g', 'grad_w_up', 'grad_conv_w', 'grad_conv_b', 'grad_w_down', 'grad_final_norm_g', 'delta_c_ctx', 'delta_ada_w', 'delta_ada_b', 'delta_norm1_g', 'delta_w_in', 'delta_q_norm_g', 'delta_k_norm_g', 'delta_ret_log_decay', 'delta_ret_norm_g', 'delta_gla_gate_up', 'delta_gla_gate_b', 'delta_gla_norm_g', 'delta_w_out', 'delta_norm2_g', 'delta_w_up', 'delta_conv_w', 'delta_conv_b', 'delta_w_down', 'delta_final_norm_g', 'new_m_c_ctx', 'new_m_ada_w', 'new_m_ada_b', 'new_m_norm1_g', 'new_m_w_in', 'new_m_q_norm_g', 'new_m_k_norm_g', 'new_m_ret_log_decay', 'new_m_ret_norm_g', 'new_m_gla_gate_up', 'new_m_gla_gate_b', 'new_m_gla_norm_g', 'new_m_w_out', 'new_m_norm2_g', 'new_m_w_up', 'new_m_conv_w', 'new_m_conv_b', 'new_m_w_down', 'new_m_final_norm_g', 'new_v_c_ctx', 'new_v_ada_w', 'new_v_ada_b', 'new_v_norm1_g', 'new_v_w_in', 'new_v_q_norm_g', 'new_v_k_norm_g', 'new_v_ret_log_decay', 'new_v_ret_norm_g', 'new_v_gla_gate_up', 'new_v_gla_gate_b', 'new_v_gla_norm_g', 'new_v_w_out', 'new_v_norm2_g', 'new_v_w_up', 'new_v_conv_w', 'new_v_conv_b', 'new_v_w_down', 'new_v_final_norm_g']
TWIN_LEAF_KINDS = {'loss': 'loss', 'grad_x': 'grad_x', 'grad_c_ctx': 'grad_w', 'grad_ada_w': 'grad_w', 'grad_ada_b': 'grad_w', 'grad_norm1_g': 'grad_w', 'grad_w_in': 'grad_w', 'grad_q_norm_g': 'grad_w', 'grad_k_norm_g': 'grad_w', 'grad_ret_log_decay': 'grad_w', 'grad_ret_norm_g': 'grad_w', 'grad_gla_gate_up': 'grad_w', 'grad_gla_gate_b': 'grad_w', 'grad_gla_norm_g': 'grad_w', 'grad_w_out': 'grad_w', 'grad_norm2_g': 'grad_w', 'grad_w_up': 'grad_w', 'grad_conv_w': 'grad_w', 'grad_conv_b': 'grad_w', 'grad_w_down': 'grad_w', 'grad_final_norm_g': 'grad_w', 'delta_c_ctx': 'delta_w', 'delta_ada_w': 'delta_w', 'delta_ada_b': 'delta_w', 'delta_norm1_g': 'delta_w', 'delta_w_in': 'delta_w', 'delta_q_norm_g': 'delta_w', 'delta_k_norm_g': 'delta_w', 'delta_ret_log_decay': 'delta_w', 'delta_ret_norm_g': 'delta_w', 'delta_gla_gate_up': 'delta_w', 'delta_gla_gate_b': 'delta_w', 'delta_gla_norm_g': 'delta_w', 'delta_w_out': 'delta_w', 'delta_norm2_g': 'delta_w', 'delta_w_up': 'delta_w', 'delta_conv_w': 'delta_w', 'delta_conv_b': 'delta_w', 'delta_w_down': 'delta_w', 'delta_final_norm_g': 'delta_w', 'new_m_c_ctx': 'new_m', 'new_m_ada_w': 'new_m', 'new_m_ada_b': 'new_m', 'new_m_norm1_g': 'new_m', 'new_m_w_in': 'new_m', 'new_m_q_norm_g': 'new_m', 'new_m_k_norm_g': 'new_m', 'new_m_ret_log_decay': 'new_m', 'new_m_ret_norm_g': 'new_m', 'new_m_gla_gate_up': 'new_m', 'new_m_gla_gate_b': 'new_m', 'new_m_gla_norm_g': 'new_m', 'new_m_w_out': 'new_m', 'new_m_norm2_g': 'new_m', 'new_m_w_up': 'new_m', 'new_m_conv_w': 'new_m', 'new_m_conv_b': 'new_m', 'new_m_w_down': 'new_m', 'new_m_final_norm_g': 'new_m', 'new_v_c_ctx': 'new_v', 'new_v_ada_w': 'new_v', 'new_v_ada_b': 'new_v', 'new_v_norm1_g': 'new_v', 'new_v_w_in': 'new_v', 'new_v_q_norm_g': 'new_v', 'new_v_k_norm_g': 'new_v', 'new_v_ret_log_decay': 'new_v', 'new_v_ret_norm_g': 'new_v', 'new_v_gla_gate_up': 'new_v', 'new_v_gla_gate_b': 'new_v', 'new_v_gla_norm_g': 'new_v', 'new_v_w_out': 'new_v', 'new_v_norm2_g': 'new_v', 'new_v_w_up': 'new_v', 'new_v_conv_w': 'new_v', 'new_v_conv_b': 'new_v', 'new_v_w_down': 'new_v', 'new_v_final_norm_g': 'new_v'}


def _forward(args):
    return _fwd_reference(*[args[k] for k in FWD_PARAMS])


def _output_shape():
    out = _jax.eval_shape(lambda: _forward(_fwd_setup_inputs(0)))
    return out.shape, out.dtype

N_MICROBATCH = 1
ADAM_LR = 0.001
ADAM_B1 = 0.9
ADAM_B2 = 0.999
ADAM_EPS = 1e-08
ADAM_WD = 0.01
ADAM_STEP = 10
PER_EXAMPLE_BATCH_AXIS = {'x': 0, 'c': 0, 'ctx': 0, 'loss_target': 0}
SHARED_INPUTS = []
_WEIGHT_DTYPES = {'c_ctx': _jnp.float32, 'ada_w': _jnp.float32, 'ada_b': _jnp.float32, 'norm1_g': _jnp.float32, 'w_in': _jnp.float32, 'q_norm_g': _jnp.float32, 'k_norm_g': _jnp.float32, 'ret_log_decay': _jnp.float32, 'ret_norm_g': _jnp.float32, 'gla_gate_up': _jnp.float32, 'gla_gate_b': _jnp.float32, 'gla_norm_g': _jnp.float32, 'w_out': _jnp.float32, 'norm2_g': _jnp.float32, 'w_up': _jnp.float32, 'conv_w': _jnp.float32, 'conv_b': _jnp.float32, 'w_down': _jnp.float32, 'final_norm_g': _jnp.float32}
MOMENT_SCALE = {'c_ctx': 6.218124e-03, 'ada_w': 1.796564e-02, 'ada_b': 3.006311e-02, 'norm1_g': 1.707212e-02, 'w_in': 1.230315e-02, 'q_norm_g': 4.508784e-03, 'k_norm_g': 4.783609e-03, 'ret_log_decay': 6.283116e+00, 'ret_norm_g': 2.680673e-02, 'gla_gate_up': 2.138619e-03, 'gla_gate_b': 5.991974e-03, 'gla_norm_g': 2.465758e-02, 'w_out': 9.809050e-03, 'norm2_g': 1.861540e-02, 'w_up': 8.211066e-03, 'conv_w': 8.489564e-03, 'conv_b': 7.451853e-03, 'w_down': 1.340465e-02, 'final_norm_g': 8.007033e+00}


def _to_microbatches(a, axis):
    t = _jnp.moveaxis(a, axis, 0)
    t = t.reshape((N_MICROBATCH, t.shape[0] // N_MICROBATCH) + t.shape[1:])
    return _jnp.moveaxis(t, 1, axis + 1)


def setup_inputs(seed: int = 0) -> dict:
    inp = _fwd_setup_inputs(seed)
    key = _jax.random.fold_in(_jax.random.key(seed), 7919)
    shape, _ = _output_shape()
    out = dict(inp)
    out["loss_target"] = _jax.random.normal(_jax.random.fold_in(key, 0), shape, _jnp.float32)
    for i, name in enumerate(TWIN_WEIGHTS):
        w = inp[name].astype(_jnp.float32)
        if MOMENT_SCALE is None:
            s = _jnp.sqrt(_jnp.mean(_jnp.square(w)) + 1e-30)
        else:
            s = MOMENT_SCALE[name]
        km, kv = _jax.random.split(_jax.random.fold_in(key, i + 1))
        out[name] = w
        out["m_" + name] = s * _jax.random.normal(km, w.shape, _jnp.float32)
        out["v_" + name] = (s * s) * _jax.random.uniform(kv, w.shape, _jnp.float32, 0.5, 1.5)
    if N_MICROBATCH > 1:
        for name, axis in PER_EXAMPLE_BATCH_AXIS.items():
            out[name] = _to_microbatches(out[name], axis)
    return {'x': out['x'], 'c': out['c'], 'ctx': out['ctx'], 'c_ctx': out['c_ctx'], 'ada_w': out['ada_w'], 'ada_b': out['ada_b'], 'norm1_g': out['norm1_g'], 'w_in': out['w_in'], 'q_norm_g': out['q_norm_g'], 'k_norm_g': out['k_norm_g'], 'ret_log_decay': out['ret_log_decay'], 'ret_norm_g': out['ret_norm_g'], 'gla_gate_up': out['gla_gate_up'], 'gla_gate_b': out['gla_gate_b'], 'gla_norm_g': out['gla_norm_g'], 'w_out': out['w_out'], 'norm2_g': out['norm2_g'], 'w_up': out['w_up'], 'conv_w': out['conv_w'], 'conv_b': out['conv_b'], 'w_down': out['w_down'], 'final_norm_g': out['final_norm_g'], 'loss_target': out['loss_target'], 'm_c_ctx': out['m_c_ctx'], 'm_ada_w': out['m_ada_w'], 'm_ada_b': out['m_ada_b'], 'm_norm1_g': out['m_norm1_g'], 'm_w_in': out['m_w_in'], 'm_q_norm_g': out['m_q_norm_g'], 'm_k_norm_g': out['m_k_norm_g'], 'm_ret_log_decay': out['m_ret_log_decay'], 'm_ret_norm_g': out['m_ret_norm_g'], 'm_gla_gate_up': out['m_gla_gate_up'], 'm_gla_gate_b': out['m_gla_gate_b'], 'm_gla_norm_g': out['m_gla_norm_g'], 'm_w_out': out['m_w_out'], 'm_norm2_g': out['m_norm2_g'], 'm_w_up': out['m_w_up'], 'm_conv_w': out['m_conv_w'], 'm_conv_b': out['m_conv_b'], 'm_w_down': out['m_w_down'], 'm_final_norm_g': out['m_final_norm_g'], 'v_c_ctx': out['v_c_ctx'], 'v_ada_w': out['v_ada_w'], 'v_ada_b': out['v_ada_b'], 'v_norm1_g': out['v_norm1_g'], 'v_w_in': out['v_w_in'], 'v_q_norm_g': out['v_q_norm_g'], 'v_k_norm_g': out['v_k_norm_g'], 'v_ret_log_decay': out['v_ret_log_decay'], 'v_ret_norm_g': out['v_ret_norm_g'], 'v_gla_gate_up': out['v_gla_gate_up'], 'v_gla_gate_b': out['v_gla_gate_b'], 'v_gla_norm_g': out['v_gla_norm_g'], 'v_w_out': out['v_w_out'], 'v_norm2_g': out['v_norm2_g'], 'v_w_up': out['v_w_up'], 'v_conv_w': out['v_conv_w'], 'v_conv_b': out['v_conv_b'], 'v_w_down': out['v_w_down'], 'v_final_norm_g': out['v_final_norm_g']}


def _loss(weights, diff, rest, loss_target):
    with _jax.named_scope("forward"):
        args = {**rest, TWIN_DIFF_INPUT: diff, **{k: w.astype(_WEIGHT_DTYPES[k]) for k, w in weights.items()}}
        y = _forward(args)
    with _jax.named_scope("loss_head"):
        err = _jnp.square(y.astype(_jnp.float32) - loss_target)
        return 0.5 * _jnp.sum(_jnp.mean(err, axis=-1)) if err.ndim else 0.5 * err


def _adamw(w, g, m, v):
    m = ADAM_B1 * m + (1.0 - ADAM_B1) * g
    v = ADAM_B2 * v + (1.0 - ADAM_B2) * _jnp.square(g)
    m_hat = m / (1.0 - ADAM_B1 ** ADAM_STEP)
    v_hat = v / (1.0 - ADAM_B2 ** ADAM_STEP)
    delta = -ADAM_LR * (m_hat / (_jnp.sqrt(v_hat) + ADAM_EPS) + ADAM_WD * w)
    return delta, m, v


def reference(x, c, ctx, c_ctx, ada_w, ada_b, norm1_g, w_in, q_norm_g, k_norm_g, ret_log_decay, ret_norm_g, gla_gate_up, gla_gate_b, gla_norm_g, w_out, norm2_g, w_up, conv_w, conv_b, w_down, final_norm_g, loss_target, m_c_ctx, m_ada_w, m_ada_b, m_norm1_g, m_w_in, m_q_norm_g, m_k_norm_g, m_ret_log_decay, m_ret_norm_g, m_gla_gate_up, m_gla_gate_b, m_gla_norm_g, m_w_out, m_norm2_g, m_w_up, m_conv_w, m_conv_b, m_w_down, m_final_norm_g, v_c_ctx, v_ada_w, v_ada_b, v_norm1_g, v_w_in, v_q_norm_g, v_k_norm_g, v_ret_log_decay, v_ret_norm_g, v_gla_gate_up, v_gla_gate_b, v_gla_norm_g, v_w_out, v_norm2_g, v_w_up, v_conv_w, v_conv_b, v_w_down, v_final_norm_g):
    given = dict(x=x, c=c, ctx=ctx, c_ctx=c_ctx, ada_w=ada_w, ada_b=ada_b, norm1_g=norm1_g, w_in=w_in, q_norm_g=q_norm_g, k_norm_g=k_norm_g, ret_log_decay=ret_log_decay, ret_norm_g=ret_norm_g, gla_gate_up=gla_gate_up, gla_gate_b=gla_gate_b, gla_norm_g=gla_norm_g, w_out=w_out, norm2_g=norm2_g, w_up=w_up, conv_w=conv_w, conv_b=conv_b, w_down=w_down, final_norm_g=final_norm_g, loss_target=loss_target, m_c_ctx=m_c_ctx, m_ada_w=m_ada_w, m_ada_b=m_ada_b, m_norm1_g=m_norm1_g, m_w_in=m_w_in, m_q_norm_g=m_q_norm_g, m_k_norm_g=m_k_norm_g, m_ret_log_decay=m_ret_log_decay, m_ret_norm_g=m_ret_norm_g, m_gla_gate_up=m_gla_gate_up, m_gla_gate_b=m_gla_gate_b, m_gla_norm_g=m_gla_norm_g, m_w_out=m_w_out, m_norm2_g=m_norm2_g, m_w_up=m_w_up, m_conv_w=m_conv_w, m_conv_b=m_conv_b, m_w_down=m_w_down, m_final_norm_g=m_final_norm_g, v_c_ctx=v_c_ctx, v_ada_w=v_ada_w, v_ada_b=v_ada_b, v_norm1_g=v_norm1_g, v_w_in=v_w_in, v_q_norm_g=v_q_norm_g, v_k_norm_g=v_k_norm_g, v_ret_log_decay=v_ret_log_decay, v_ret_norm_g=v_ret_norm_g, v_gla_gate_up=v_gla_gate_up, v_gla_gate_b=v_gla_gate_b, v_gla_norm_g=v_gla_norm_g, v_w_out=v_w_out, v_norm2_g=v_norm2_g, v_w_up=v_w_up, v_conv_w=v_conv_w, v_conv_b=v_conv_b, v_w_down=v_w_down, v_final_norm_g=v_final_norm_g)
    weights = {n: given[n] for n in TWIN_WEIGHTS}
    shared = {n: given[n] for n in SHARED_INPUTS}
    per_example = {n: given[n] for n in ['x', 'c', 'ctx']}
    grad_fn = _jax.value_and_grad(_loss, argnums=(0, 1))

    def one_microbatch(ex, loss_target):
        ex = dict(ex)
        diff = ex.pop(TWIN_DIFF_INPUT)
        return grad_fn(weights, diff, {**shared, **ex}, loss_target)

    if N_MICROBATCH == 1:
        loss, (grad_w, grad_x) = one_microbatch(per_example, given["loss_target"])
    else:
        def body(carry, xs):
            loss_sum, grad_sum = carry
            l_k, (gw_k, gx_k) = one_microbatch(xs[0], xs[1])
            with _jax.named_scope("update"):
                return (loss_sum + l_k, _jax.tree.map(_jnp.add, grad_sum, gw_k)), gx_k

        init = (_jnp.zeros((), _jnp.float32), _jax.tree.map(_jnp.zeros_like, weights))
        (loss, grad_w), grad_x = _jax.lax.scan(body, init, (per_example, given["loss_target"]))
    with _jax.named_scope("update"):
        delta_w, new_m, new_v = {}, {}, {}
        for n in TWIN_WEIGHTS:
            delta_w[n], new_m[n], new_v[n] = _adamw(weights[n], grad_w[n], given["m_" + n], given["v_" + n])
    return (loss, grad_x, *[grad_w[n] for n in TWIN_WEIGHTS], *[delta_w[n] for n in TWIN_WEIGHTS],
            *[new_m[n] for n in TWIN_WEIGHTS], *[new_v[n] for n in TWIN_WEIGHTS])
```

```python
import functools
from typing import NamedTuple

import numpy as np
import jax
import jax.numpy as jnp
from jax import lax
from jax.experimental import pallas as pl
from jax.experimental.pallas import tpu as pltpu

F32 = jnp.float32
BF16 = jnp.bfloat16

D_MODEL = 2048
HEAD_DIM = 128
ATT_Q_HEADS = 8
ATT_KV_HEADS = 2
ATT_GROUP = ATT_Q_HEADS // ATT_KV_HEADS
RET_HEADS = 4
GLA_HEADS = 4
GLA_DK = 64
GLA_DV = 128
GLA_RANK = 16
GLA_TAU = 16.0
RET_CHUNK = 128
GLA_CHUNK = 64
GRID_W = 64
ROPE_THETA = 10000.0
N_MOD = 6
EPS = 1e-6
N_MAIN = 5120
N_GATE = 2 * GLA_RANK
LANES = 128
ROW_TILE = 256
FFN_COL_TILE = 256
VMEM_LIMIT = 56 * 1024 * 1024

ADAM_LR = 0.001
ADAM_B1 = 0.9
ADAM_B2 = 0.999
ADAM_EPS = 1e-08
ADAM_WD = 0.01
ADAM_STEP = 10

Z_AQ, Z_AK, Z_AV = 0, 1024, 1280
Z_RQ, Z_RK, Z_RV, Z_RG = 1536, 2048, 2560, 3072
Z_GQ, Z_GK, Z_GV, Z_GR = 3584, 3840, 4096, 4608
P_AQ, P_AK, P_GQ, P_RQ, P_RK, P_LA = 0, 1024, 1280, 1536, 2048, 2560
P_W = 3072


def _params(sem=None):
    return pltpu.CompilerParams(dimension_semantics=sem, vmem_limit_bytes=VMEM_LIMIT)


def _pick(n, cands):
    for c in cands:
        if n % c == 0:
            return c
    return n


_NN = (((1,), (0,)), ((), ()))
_NT = (((1,), (1,)), ((), ()))
_TN = (((0,), (0,)), ((), ()))


def _dg(a, b, dims):
    return lax.dot_general(a.astype(BF16), b.astype(BF16), dims, preferred_element_type=F32)


@jax.custom_vjp
def bdot(a, b):
    return _dg(a, b, _NN)


def _bdot_fwd(a, b):
    return _dg(a, b, _NN), (a, b)


def _bdot_bwd(res, ct):
    a, b = res
    return _dg(ct, b, _NT), _dg(a, ct, _TN)


bdot.defvjp(_bdot_fwd, _bdot_bwd)


@jax.custom_vjp
def bdot_nt(a, b):
    return _dg(a, b, _NT)


def _bdot_nt_fwd(a, b):
    return _dg(a, b, _NT), (a, b)


def _bdot_nt_bwd(res, ct):
    a, b = res
    return _dg(ct, b, _NN), _dg(ct, a, _TN)


bdot_nt.defvjp(_bdot_nt_fwd, _bdot_nt_bwd)


@jax.custom_vjp
def bdot_tn(a, b):
    return _dg(a, b, _TN)


def _bdot_tn_fwd(a, b):
    return _dg(a, b, _TN), (a, b)


def _bdot_tn_bwd(res, ct):
    a, b = res
    return _dg(b, ct, _NT), _dg(a, ct, _NN)


bdot_tn.defvjp(_bdot_tn_fwd, _bdot_tn_bwd)


def _split3(x):
    x1 = x.astype(BF16)
    r1 = x - x1.astype(F32)
    x2 = r1.astype(BF16)
    x3 = (r1 - x2.astype(F32)).astype(BF16)
    return x1, x2, x3


def _mask_dot(mask_bf16, x, dims):
    x1, x2, x3 = _split3(x)
    f = lambda t: lax.dot_general(mask_bf16, t, dims, preferred_element_type=F32)
    return f(x1) + f(x2) + f(x3)


@jax.custom_vjp
def mask_cumsum(mask, x):
    return _mask_dot(mask.astype(BF16), x, _NN)


def _mask_cumsum_fwd(mask, x):
    return mask_cumsum(mask, x), mask


def _mask_cumsum_bwd(mask, ct):
    return jnp.zeros_like(mask), _mask_dot(mask.astype(BF16), ct, _TN)


mask_cumsum.defvjp(_mask_cumsum_fwd, _mask_cumsum_bwd)


def _roll(x, shift, axis):
    return pltpu.roll(x, shift % x.shape[axis], axis)


@functools.partial(jax.custom_vjp, nondiff_argnums=(1, 2))
def roll(x, shift, axis):
    return _roll(x, shift, axis)


def _roll_fwd(x, shift, axis):
    return _roll(x, shift, axis), None


def _roll_bwd(shift, axis, _, ct):
    return (_roll(ct, -shift, axis),)


roll.defvjp(_roll_fwd, _roll_bwd)


def rms(x):
    return x * lax.rsqrt(jnp.mean(x * x, axis=-1, keepdims=True) + EPS)


def silu(x):
    return x * (1.0 / (1.0 + jnp.exp(-x)))


def log_sigmoid(x):
    return jnp.minimum(x, 0.0) - jnp.log(1.0 + jnp.exp(-jnp.abs(x)))


def rope(t, cos, sin):
    return t * cos + roll(t, HEAD_DIM // 2, 1) * sin


def _heads(x, n, width=HEAD_DIM):
    return [x[:, h * width:(h + 1) * width] for h in range(n)]


class Row(NamedTuple):
    arr: jax.Array
    width: int
    idx: int = 0
    diff: bool = True


class Par(NamedTuple):
    arr: jax.Array
    grouped: bool = False
    diff: bool = True


def _row_specs(rows, pars, tm, n_lat_tiles):
    def grp(i):
        return jnp.minimum(i // n_lat_tiles, 1)

    specs = [pl.BlockSpec((tm, r.width), functools.partial(lambda i, k: (i, k), k=r.idx)) for r in rows]
    for p in pars:
        blk = (1,) + p.arr.shape[1:]
        if p.grouped:
            specs.append(pl.BlockSpec(blk, lambda i: (grp(i), 0, 0)))
        else:
            specs.append(pl.BlockSpec(blk, lambda i: (0, 0, 0)))
    return specs


def row_map(name, fn, rows, pars, outs, n_rows, n_lat):
    tm = ROW_TILE
    nr, npar = len(rows), len(pars)

    def body(*refs):
        vals = [r[...] for r in refs[:nr]] + [p[0] for p in refs[nr:nr + npar]]
        res = fn(*vals)
        for o, v in zip(refs[nr + npar:], res):
            o[...] = v.astype(o.dtype)

    return pl.pallas_call(
        body, name=name, grid=(n_rows // tm,),
        in_specs=_row_specs(rows, pars, tm, n_lat // tm),
        out_specs=[pl.BlockSpec((tm, w), lambda i: (i, 0)) for w, _ in outs],
        out_shape=[jax.ShapeDtypeStruct((n_rows, w), dt) for w, dt in outs],
        compiler_params=_params(("arbitrary",)),
    )(*[r.arr for r in rows], *[p.arr for p in pars])


def row_vjp(name, fn, rows, pars, cts, n_rows, n_lat, add_to_first=None):
    tm = ROW_TILE
    nr, npar, nc = len(rows), len(pars), len(cts)
    n_lat_tiles = n_lat // tm
    args = list(rows) + list(pars)
    diff_pos = [k for k, a in enumerate(args) if a.diff]
    n_add = 0 if add_to_first is None else 1

    def body(*refs):
        i = pl.program_id(0)
        vals = [r[...] for r in refs[:nr]] + [p[0] for p in refs[nr:nr + npar]]
        ct_vals = tuple(c[...] for c in refs[nr + npar:nr + npar + nc])
        out_refs = refs[nr + npar + nc + n_add:]

        def g(*dv):
            full = list(vals)
            for k, v in zip(diff_pos, dv):
                full[k] = v
            return tuple(fn(*full))

        _, vjp = jax.vjp(g, *[vals[k] for k in diff_pos])
        grads = vjp(ct_vals)
        for n, (k, o, gr) in enumerate(zip(diff_pos, out_refs, grads)):
            if k < nr:
                o[...] = gr + refs[nr + npar + nc][...] if (n == 0 and n_add) else gr
            else:
                first = (i == 0) | (i == n_lat_tiles) if args[k].grouped else (i == 0)

                @pl.when(first)
                def _():
                    o[0] = gr

                @pl.when(jnp.logical_not(first))
                def _():
                    o[0] += gr

    def grp(i):
        return jnp.minimum(i // n_lat_tiles, 1)

    out_specs, out_shape = [], []
    for k in diff_pos:
        a = args[k]
        if k < nr:
            out_specs.append(pl.BlockSpec((tm, a.width), lambda i: (i, 0)))
            out_shape.append(jax.ShapeDtypeStruct((n_rows, a.width), F32))
        else:
            blk = (1,) + a.arr.shape[1:]
            out_specs.append(pl.BlockSpec(blk, (lambda i: (grp(i), 0, 0)) if a.grouped else (lambda i: (0, 0, 0))))
            out_shape.append(jax.ShapeDtypeStruct(a.arr.shape, F32))
    extra = list(cts) + ([add_to_first] if n_add else [])
    ct_specs = [pl.BlockSpec((tm, c.shape[1]), lambda i: (i, 0)) for c in extra]
    return pl.pallas_call(
        body, name=name, grid=(n_rows // tm,),
        in_specs=_row_specs(rows, pars, tm, n_lat_tiles) + ct_specs,
        out_specs=out_specs, out_shape=out_shape,
        compiler_params=_params(("arbitrary",)),
    )(*[r.arr for r in rows], *[p.arr for p in pars], *extra)


def matmul(name, a, b, *, ta=False, tb=False, add=None, out_dtype=F32):
    m, k = (a.shape[1], a.shape[0]) if ta else a.shape
    n = b.shape[0] if tb else b.shape[1]
    assert (b.shape[1] if tb else b.shape[0]) == k, (a.shape, b.shape, ta, tb)
    tm = _pick(m, (1024, 768, 512, 256, 128))
    tn = _pick(n, (1024, 768, 512, 256, 128))
    tk = _pick(k, (512, 256, 128))
    nk = k // tk
    dims = (((0 if ta else 1,), (1 if tb else 0,)), ((), ()))

    def body(a_ref, b_ref, *rest):
        o_ref, acc = rest[-2:]
        kk = pl.program_id(2)

        @pl.when(kk == 0)
        def _():
            acc[...] = jnp.zeros_like(acc)

        acc[...] += lax.dot_general(a_ref[...].astype(BF16), b_ref[...].astype(BF16), dims,
                                    preferred_element_type=F32)

        @pl.when(kk == nk - 1)
        def _():
            r = acc[...]
            if add is not None:
                r = r + rest[0][...]
            o_ref[...] = r.astype(o_ref.dtype)

    a_spec = pl.BlockSpec((tk, tm), lambda i, j, kk: (kk, i)) if ta else pl.BlockSpec((tm, tk), lambda i, j, kk: (i, kk))
    b_spec = pl.BlockSpec((tn, tk), lambda i, j, kk: (j, kk)) if tb else pl.BlockSpec((tk, tn), lambda i, j, kk: (kk, j))
    o_spec = pl.BlockSpec((tm, tn), lambda i, j, kk: (i, j))
    ins = [a, b] + ([add] if add is not None else [])
    return pl.pallas_call(
        body, name=name, grid=(m // tm, n // tn, nk),
        in_specs=[a_spec, b_spec] + ([o_spec] if add is not None else []),
        out_specs=o_spec, out_shape=jax.ShapeDtypeStruct((m, n), out_dtype),
        scratch_shapes=[pltpu.VMEM((tm, tn), F32)],
        compiler_params=_params(("parallel", "parallel", "arbitrary")),
    )(*ins)


def normmod_tile(x, g, shift, scale):
    return (rms(x) * g * (1.0 + scale) + shift,)


def resid_tile(x, y, gate):
    return (x + gate * y,)


def prep_tile(z_qk, z_rq, z_rk, z_gq, zg, cos, sin, qg, kg, gate_up, gate_b):
    out = []
    for h, t in enumerate(_heads(z_qk, ATT_Q_HEADS + ATT_KV_HEADS)):
        out.append(rope(rms(t) * (qg if h < ATT_Q_HEADS else kg), cos, sin))
    gq = z_gq * (GLA_DK ** -0.5)
    rq = [rope(t, cos, sin) for t in _heads(z_rq, RET_HEADS)]
    rk = [rope(t * (HEAD_DIM ** -0.5), cos, sin) for t in _heads(z_rk, RET_HEADS)]
    la = [log_sigmoid(bdot(zg, gate_up[d * LANES:(d + 1) * LANES]) + gate_b[d:d + 1]) * (1.0 / GLA_TAU) for d in range(2)]
    return (jnp.concatenate(out + [gq] + rq + rk + la, axis=1),)


def post_tile(o_att, o_ret_f, o_ret_b, o_gla_f, o_gla_b, rg, gr, ret_g, gla_g):
    ret = jnp.concatenate([rms(t) * ret_g for t in _heads(o_ret_f + o_ret_b, RET_HEADS)], axis=1) * silu(rg)
    gla = jnp.concatenate([rms(t) * gla_g for t in _heads(o_gla_f + o_gla_b, GLA_HEADS)], axis=1) * silu(gr)
    return (jnp.concatenate([o_att, ret, gla], axis=1),)


def _convglu_tile(n_lat, u, cw, cb):
    tc = FFN_COL_TILE
    a, v = u[:, :tc], u[:, tc:]
    t = a.shape[0]
    row = lax.broadcasted_iota(jnp.int32, (t, 1), 0)
    has_prev = ((row != 0) & (row != n_lat)).astype(F32)
    has_next = ((row != n_lat - 1) & (row != t - 1)).astype(F32)
    conv = roll(a, 1, 0) * has_prev * cw[0:1] + a * cw[1:2] + roll(a, -1, 0) * has_next * cw[2:3] + cb
    return silu(conv) * v


def convglu(name, u, cw, cb, n_lat):
    t, f2 = u.shape
    f, tc = f2 // 2, FFN_COL_TILE

    def body(u_ref, cw_ref, cb_ref, o_ref):
        o_ref[...] = _convglu_tile(n_lat, u_ref[...], cw_ref[...], cb_ref[...]).astype(o_ref.dtype)

    return pl.pallas_call(
        body, name=name, grid=(f // tc,),
        in_specs=[pl.BlockSpec((t, 2 * tc), lambda j: (0, j)), pl.BlockSpec((3, tc), lambda j: (0, j)),
                  pl.BlockSpec((1, tc), lambda j: (0, j))],
        out_specs=pl.BlockSpec((t, tc), lambda j: (0, j)),
        out_shape=jax.ShapeDtypeStruct((t, f), BF16),
        compiler_params=_params(("parallel",)),
    )(u, cw, cb)


def convglu_bwd(name, u, cw, cb, dg, n_lat):
    t, f2 = u.shape
    f, tc = f2 // 2, FFN_COL_TILE

    def body(u_ref, cw_ref, cb_ref, dg_ref, du_ref, dcw_ref, dcb_ref):
        _, vjp = jax.vjp(functools.partial(_convglu_tile, n_lat), u_ref[...], cw_ref[...], cb_ref[...])
        du_ref[...], dcw_ref[...], dcb_ref[...] = vjp(dg_ref[...])

    return pl.pallas_call(
        body, name=name, grid=(f // tc,),
        in_specs=[pl.BlockSpec((t, 2 * tc), lambda j: (0, j)), pl.BlockSpec((3, tc), lambda j: (0, j)),
                  pl.BlockSpec((1, tc), lambda j: (0, j)), pl.BlockSpec((t, tc), lambda j: (0, j))],
        out_specs=[pl.BlockSpec((t, 2 * tc), lambda j: (0, j)), pl.BlockSpec((3, tc), lambda j: (0, j)),
                   pl.BlockSpec((1, tc), lambda j: (0, j))],
        out_shape=[jax.ShapeDtypeStruct((t, f2), F32), jax.ShapeDtypeStruct((3, f), F32),
                   jax.ShapeDtypeStruct((1, f), F32)],
        compiler_params=_params(("parallel",)),
    )(u, cw, cb, dg)


def final_loss(x, target, g, n_lat):
    tm = ROW_TILE
    d = x.shape[1]

    def body(x_ref, t_ref, g_ref, loss_ref, dx_ref, dg_ref):
        i = pl.program_id(0)
        tgt = t_ref[...]

        def f(xv, gv):
            e = rms(xv) * gv - tgt
            s = jnp.sum(jnp.sum(e * e, axis=1, keepdims=True), axis=0, keepdims=True)
            return s * (0.5 / d)

        val, vjp = jax.vjp(f, x_ref[...], g_ref[...])
        dx, dgv = vjp(jnp.ones((1, 1), F32))
        dx_ref[...] = dx

        @pl.when(i == 0)
        def _():
            dg_ref[...] = dgv
            loss_ref[...] = jnp.broadcast_to(val, loss_ref.shape)

        @pl.when(i != 0)
        def _():
            dg_ref[...] += dgv
            loss_ref[...] += jnp.broadcast_to(val, loss_ref.shape)

    return pl.pallas_call(
        body, name="final_loss", grid=(n_lat // tm,),
        in_specs=[pl.BlockSpec((tm, d), lambda i: (i, 0)), pl.BlockSpec((tm, d), lambda i: (i, 0)),
                  pl.BlockSpec((1, d), lambda i: (0, 0))],
        out_specs=[pl.BlockSpec((1, LANES), lambda i: (0, 0)), pl.BlockSpec((tm, d), lambda i: (i, 0)),
                   pl.BlockSpec((1, d), lambda i: (0, 0))],
        out_shape=[jax.ShapeDtypeStruct((1, LANES), F32), jax.ShapeDtypeStruct((n_lat, d), F32),
                   jax.ShapeDtypeStruct((1, d), F32)],
        compiler_params=_params(("arbitrary",)),
    )(x, target, g)


ATT_SCALE = HEAD_DIM ** -0.5
_AK_BLK = P_AK // HEAD_DIM
_AV_BLK = Z_AV // HEAD_DIM


def _att_tile(n_rows, n_lat):
    return _pick(n_lat, (256, 128)) if (n_rows - n_lat) % 256 == 0 else 128


def attn_fwd(p, z, n_lat):
    t = p.shape[0]
    tq = tk = _att_tile(t, n_lat)
    nq, nkv, nlat = t // tq, t // tk, n_lat // tq

    def body(q_ref, k_ref, v_ref, o_ref, lse_ref, m_s, l_s, acc_s):
        i, j = pl.program_id(1), pl.program_id(2)

        @pl.when(j == 0)
        def _():
            m_s[...] = jnp.full_like(m_s, -jnp.inf)
            l_s[...] = jnp.zeros_like(l_s)
            acc_s[...] = jnp.zeros_like(acc_s)

        @pl.when((i < nlat) | (j >= nlat))
        def _():
            s = _dg(q_ref[...], k_ref[...], _NT) * ATT_SCALE
            m_new = jnp.maximum(m_s[...], jnp.max(s, axis=1, keepdims=True))
            alpha = jnp.exp(m_s[...] - m_new)
            pr = jnp.exp(s - m_new)
            l_s[...] = alpha * l_s[...] + jnp.sum(pr, axis=1, keepdims=True)
            acc_s[...] = alpha * acc_s[...] + _dg(pr, v_ref[...], _NN)
            m_s[...] = m_new

        @pl.when(j == nkv - 1)
        def _():
            o_ref[...] = acc_s[...] / l_s[...]
            lse_ref[0] = m_s[...] + jnp.log(l_s[...])

    return pl.pallas_call(
        body, name="attn_fwd", grid=(ATT_Q_HEADS, nq, nkv),
        in_specs=[pl.BlockSpec((tq, HEAD_DIM), lambda h, i, j: (i, h)),
                  pl.BlockSpec((tk, HEAD_DIM), lambda h, i, j: (j, _AK_BLK + h // ATT_GROUP)),
                  pl.BlockSpec((tk, HEAD_DIM), lambda h, i, j: (j, _AV_BLK + h // ATT_GROUP))],
        out_specs=[pl.BlockSpec((tq, HEAD_DIM), lambda h, i, j: (i, h)),
                   pl.BlockSpec((1, tq, 1), lambda h, i, j: (h, i, 0))],
        out_shape=[jax.ShapeDtypeStruct((t, ATT_Q_HEADS * HEAD_DIM), F32),
                   jax.ShapeDtypeStruct((ATT_Q_HEADS, t, 1), F32)],
        scratch_shapes=[pltpu.VMEM((tq, 1), F32), pltpu.VMEM((tq, 1), F32), pltpu.VMEM((tq, HEAD_DIM), F32)],
        compiler_params=_params(("parallel", "parallel", "arbitrary")),
    )(p, p, z)


def attn_bwd(p, z, o, lse, do, n_lat):
    t = p.shape[0]
    tq = tk = _att_tile(t, n_lat)
    nq, nkv, nlat = t // tq, t // tk, n_lat // tq

    def dq_body(q_ref, k_ref, v_ref, o_ref, do_ref, lse_ref, dq_ref, delta_ref, acc_s, dl_s):
        i, j = pl.program_id(1), pl.program_id(2)

        @pl.when(j == 0)
        def _():
            acc_s[...] = jnp.zeros_like(acc_s)
            dl_s[...] = jnp.sum(o_ref[...] * do_ref[...], axis=1, keepdims=True)

        @pl.when((i < nlat) | (j >= nlat))
        def _():
            s = _dg(q_ref[...], k_ref[...], _NT) * ATT_SCALE
            pr = jnp.exp(s - lse_ref[0])
            dp = _dg(do_ref[...], v_ref[...], _NT)
            ds = pr * (dp - dl_s[...]) * ATT_SCALE
            acc_s[...] += _dg(ds, k_ref[...], _NN)

        @pl.when(j == nkv - 1)
        def _():
            dq_ref[...] = acc_s[...]
            delta_ref[0] = dl_s[...]

    q_spec = pl.BlockSpec((tq, HEAD_DIM), lambda h, i, j: (i, h))
    row_spec = pl.BlockSpec((1, tq, 1), lambda h, i, j: (h, i, 0))
    dq, delta = pl.pallas_call(
        dq_body, name="attn_bwd_dq", grid=(ATT_Q_HEADS, nq, nkv),
        in_specs=[q_spec,
                  pl.BlockSpec((tk, HEAD_DIM), lambda h, i, j: (j, _AK_BLK + h // ATT_GROUP)),
                  pl.BlockSpec((tk, HEAD_DIM), lambda h, i, j: (j, _AV_BLK + h // ATT_GROUP)),
                  q_spec, q_spec, row_spec],
        out_specs=[q_spec, row_spec],
        out_shape=[jax.ShapeDtypeStruct((t, ATT_Q_HEADS * HEAD_DIM), F32),
                   jax.ShapeDtypeStruct((ATT_Q_HEADS, t, 1), F32)],
        scratch_shapes=[pltpu.VMEM((tq, HEAD_DIM), F32), pltpu.VMEM((tq, 1), F32)],
        compiler_params=_params(("parallel", "parallel", "arbitrary")),
    )(p, p, z, o, do, lse)

    n_inner = ATT_GROUP * nq

    def dkv_body(q_ref, k_ref, v_ref, do_ref, lse_ref, delta_ref, dk_ref, dv_ref, dk_s, dv_s):
        j, r = pl.program_id(1), pl.program_id(2)
        i = r % nq

        @pl.when(r == 0)
        def _():
            dk_s[...] = jnp.zeros_like(dk_s)
            dv_s[...] = jnp.zeros_like(dv_s)

        @pl.when((i < nlat) | (j >= nlat))
        def _():
            s = _dg(q_ref[...], k_ref[...], _NT) * ATT_SCALE
            pr = jnp.exp(s - lse_ref[0])
            dv_s[...] += _dg(pr, do_ref[...], _TN)
            dp = _dg(do_ref[...], v_ref[...], _NT)
            ds = pr * (dp - delta_ref[0]) * ATT_SCALE
            dk_s[...] += _dg(ds, q_ref[...], _TN)

        @pl.when(r == n_inner - 1)
        def _():
            dk_ref[...] = dk_s[...]
            dv_ref[...] = dv_s[...]

    qh_spec = pl.BlockSpec((tq, HEAD_DIM), lambda kv, j, r: (r % nq, kv * ATT_GROUP + r // nq))
    rowh_spec = pl.BlockSpec((1, tq, 1), lambda kv, j, r: (kv * ATT_GROUP + r // nq, r % nq, 0))
    kv_out = pl.BlockSpec((tk, HEAD_DIM), lambda kv, j, r: (j, kv))
    dk, dv = pl.pallas_call(
        dkv_body, name="attn_bwd_dkv", grid=(ATT_KV_HEADS, nkv, n_inner),
        in_specs=[qh_spec,
                  pl.BlockSpec((tk, HEAD_DIM), lambda kv, j, r: (j, _AK_BLK + kv)),
                  pl.BlockSpec((tk, HEAD_DIM), lambda kv, j, r: (j, _AV_BLK + kv)),
                  qh_spec, rowh_spec, rowh_spec],
        out_specs=[kv_out, kv_out],
        out_shape=[jax.ShapeDtypeStruct((t, ATT_KV_HEADS * HEAD_DIM), F32)] * 2,
        scratch_shapes=[pltpu.VMEM((tk, HEAD_DIM), F32), pltpu.VMEM((tk, HEAD_DIM), F32)],
        compiler_params=_params(("parallel", "parallel", "arbitrary")),
    )(p, p, z, do, lse, delta)
    return dq, dk, dv


_RQ_BLK = P_RQ // HEAD_DIM
_RK_BLK = P_RK // HEAD_DIM
_RV_BLK = Z_RV // HEAD_DIM


def _scan_chunk(direction, step, n_chunks, n_lat_chunks):
    return jnp.where(direction == 0, (step + n_lat_chunks) % n_chunks, n_chunks - 1 - step)


def _ret_geometry(direction):
    c = RET_CHUNK
    i = lax.broadcasted_iota(jnp.int32, (c, c), 0)
    j = lax.broadcasted_iota(jnp.int32, (c, c), 1)
    rel = jnp.where(direction == 0, i - j, j - i).astype(F32)
    r = lax.broadcasted_iota(jnp.int32, (c, 1), 0)
    pos = jnp.where(direction == 0, r, c - 1 - r).astype(F32)
    return rel, pos


def ret_chunk(q, k, v, s, lg, rel, pos):
    c = RET_CHUNK
    causal = rel >= 0
    d_in = jnp.where(causal, jnp.exp(lg * jnp.where(causal, rel, 0.0)), 0.0)
    q_dec = jnp.exp(lg * (pos + 1.0))
    k_dec = jnp.exp(lg * (c - 1.0 - pos))
    c_dec = jnp.exp(lg * c)
    att = bdot_nt(q, k) * d_in
    o = bdot(att, v) + bdot(q * q_dec, s)
    s_new = c_dec * s + bdot_tn(k * k_dec, v)
    return o, s_new


def ret_fwd(p, z, lg, n_lat):
    t = p.shape[0]
    c = RET_CHUNK
    nc, nlc = t // c, n_lat // c

    def body(q_ref, k_ref, v_ref, lg_ref, o_ref, ssave_ref, s_s):
        d, n = pl.program_id(0), pl.program_id(2)

        @pl.when(n == 0)
        def _():
            s_s[...] = jnp.zeros_like(s_s)

        rel, pos = _ret_geometry(d)
        ssave_ref[0, 0, 0] = s_s[...]
        o, s_new = ret_chunk(q_ref[...], k_ref[...], v_ref[...], s_s[...], lg_ref[0, 0], rel, pos)
        o_ref[...] = o
        s_s[...] = s_new

    def blk(base):
        return pl.BlockSpec((c, HEAD_DIM), lambda d, h, n: (_scan_chunk(d, n, nc, nlc), base + h))

    return pl.pallas_call(
        body, name="ret_fwd", grid=(2, RET_HEADS, nc),
        in_specs=[blk(_RQ_BLK), blk(_RK_BLK), blk(_RV_BLK), pl.BlockSpec((1, 1, 1, 1), lambda d, h, n: (d, h, 0, 0))],
        out_specs=[pl.BlockSpec((c, HEAD_DIM), lambda d, h, n: (_scan_chunk(d, n, nc, nlc), d * RET_HEADS + h)),
                   pl.BlockSpec((1, 1, 1, HEAD_DIM, HEAD_DIM), lambda d, h, n: (d, h, n, 0, 0))],
        out_shape=[jax.ShapeDtypeStruct((t, 2 * RET_HEADS * HEAD_DIM), F32),
                   jax.ShapeDtypeStruct((2, RET_HEADS, nc, HEAD_DIM, HEAD_DIM), F32)],
        scratch_shapes=[pltpu.VMEM((HEAD_DIM, HEAD_DIM), F32)],
        compiler_params=_params(("parallel", "parallel", "arbitrary")),
    )(p, p, z, lg)


def ret_bwd(p, z, lg, states, do, n_lat):
    t = p.shape[0]
    c = RET_CHUNK
    nc, nlc = t // c, n_lat // c

    def body(q_ref, k_ref, v_ref, lg_ref, s_ref, do_ref, dq_ref, dk_ref, dv_ref, dlg_ref, ds_s):
        d, n = pl.program_id(0), pl.program_id(2)

        @pl.when(n == 0)
        def _():
            ds_s[...] = jnp.zeros_like(ds_s)
            dlg_ref[...] = jnp.zeros_like(dlg_ref)

        rel, pos = _ret_geometry(d)
        f = functools.partial(ret_chunk, rel=rel, pos=pos)
        _, vjp = jax.vjp(f, q_ref[...], k_ref[...], v_ref[...], s_ref[0, 0, 0], lg_ref[0, 0])
        dq, dk, dv, ds, dlg = vjp((do_ref[...], ds_s[...]))
        dq_ref[...], dk_ref[...], dv_ref[...] = dq, dk, dv
        ds_s[...] = ds
        dlg_ref[0, 0] += dlg

    def chunk_of(d, n):
        return _scan_chunk(d, nc - 1 - n, nc, nlc)

    def blk(base):
        return pl.BlockSpec((c, HEAD_DIM), lambda d, h, n: (chunk_of(d, n), base + h))

    out_blk = pl.BlockSpec((c, HEAD_DIM), lambda d, h, n: (chunk_of(d, n), d * RET_HEADS + h))
    grad_shape = jax.ShapeDtypeStruct((t, 2 * RET_HEADS * HEAD_DIM), F32)
    return pl.pallas_call(
        body, name="ret_bwd", grid=(2, RET_HEADS, nc),
        in_specs=[blk(_RQ_BLK), blk(_RK_BLK), blk(_RV_BLK), pl.BlockSpec((1, 1, 1, 1), lambda d, h, n: (d, h, 0, 0)),
                  pl.BlockSpec((1, 1, 1, HEAD_DIM, HEAD_DIM), lambda d, h, n: (d, h, nc - 1 - n, 0, 0)),
                  pl.BlockSpec((c, HEAD_DIM), lambda d, h, n: (chunk_of(d, n), h))],
        out_specs=[out_blk, out_blk, out_blk, pl.BlockSpec((1, 1, 1, 1), lambda d, h, n: (d, h, 0, 0))],
        out_shape=[grad_shape, grad_shape, grad_shape, jax.ShapeDtypeStruct((2, RET_HEADS, 1, 1), F32)],
        scratch_shapes=[pltpu.VMEM((HEAD_DIM, HEAD_DIM), F32)],
        compiler_params=_params(("parallel", "parallel", "arbitrary")),
    )(p, p, z, lg, states, do)


_GQ_BLK = P_GQ // (GLA_HEADS * GLA_DK)
_GK_BLK = Z_GK // (GLA_HEADS * GLA_DK)
_GV_BLK = Z_GV // (GLA_HEADS * GLA_DV)
_LA_BLK = P_LA // (GLA_HEADS * GLA_DK)


def _gla_mask(direction):
    c = GLA_CHUNK
    i = lax.broadcasted_iota(jnp.int32, (c, c), 0)
    j = lax.broadcasted_iota(jnp.int32, (c, c), 1)
    return (jnp.where(direction == 0, i - j, j - i) >= 0).astype(F32)


def gla_chunk(q, k, v, la, st, mask):
    b = mask_cumsum(mask, la)
    btot = jnp.sum(la, axis=0, keepdims=True)
    half = 0.5 * btot
    qt, kt = q * jnp.exp(b - half), k * jnp.exp(half - b)
    qs, ke = q * jnp.exp(b), k * jnp.exp(btot - b)
    outs, upd = [], []
    for h in range(GLA_HEADS):
        ks = slice(h * GLA_DK, (h + 1) * GLA_DK)
        vh = v[:, h * GLA_DV:(h + 1) * GLA_DV]
        att = bdot_nt(qt[:, ks], kt[:, ks]) * mask
        outs.append(bdot(att, vh) + bdot_nt(qs[:, ks], st[:, ks]))
        upd.append(bdot_tn(vh, ke[:, ks]))
    st_new = st * jnp.exp(btot) + jnp.concatenate(upd, axis=1)
    return jnp.concatenate(outs, axis=1), st_new


def gla_fwd(p, z, n_lat):
    t = p.shape[0]
    c = GLA_CHUNK
    nc, nlc = t // c, n_lat // c
    kw, vw = GLA_HEADS * GLA_DK, GLA_HEADS * GLA_DV

    def body(q_ref, k_ref, v_ref, la_ref, o_ref, ssave_ref, s_s):
        d, n = pl.program_id(0), pl.program_id(1)

        @pl.when(n == 0)
        def _():
            s_s[...] = jnp.zeros_like(s_s)

        ssave_ref[0, 0] = s_s[...]
        o, s_new = gla_chunk(q_ref[...], k_ref[...], v_ref[...], la_ref[...], s_s[...], _gla_mask(d))
        o_ref[...] = o
        s_s[...] = s_new

    def chunk_of(d, n):
        return _scan_chunk(d, n, nc, nlc)

    return pl.pallas_call(
        body, name="gla_fwd", grid=(2, nc),
        in_specs=[pl.BlockSpec((c, kw), lambda d, n: (chunk_of(d, n), _GQ_BLK)),
                  pl.BlockSpec((c, kw), lambda d, n: (chunk_of(d, n), _GK_BLK)),
                  pl.BlockSpec((c, vw), lambda d, n: (chunk_of(d, n), _GV_BLK)),
                  pl.BlockSpec((c, kw), lambda d, n: (chunk_of(d, n), _LA_BLK + d))],
        out_specs=[pl.BlockSpec((c, vw), lambda d, n: (chunk_of(d, n), d)),
                   pl.BlockSpec((1, 1, GLA_DV, kw), lambda d, n: (d, n, 0, 0))],
        out_shape=[jax.ShapeDtypeStruct((t, 2 * vw), F32), jax.ShapeDtypeStruct((2, nc, GLA_DV, kw), F32)],
        scratch_shapes=[pltpu.VMEM((GLA_DV, kw), F32)],
        compiler_params=_params(("parallel", "arbitrary")),
    )(p, z, z, p)


def gla_bwd(p, z, states, do, n_lat):
    t = p.shape[0]
    c = GLA_CHUNK
    nc, nlc = t // c, n_lat // c
    kw, vw = GLA_HEADS * GLA_DK, GLA_HEADS * GLA_DV

    def body(q_ref, k_ref, v_ref, la_ref, s_ref, do_ref, dq_ref, dk_ref, dv_ref, dla_ref, ds_s):
        d, n = pl.program_id(0), pl.program_id(1)

        @pl.when(n == 0)
        def _():
            ds_s[...] = jnp.zeros_like(ds_s)

        f = functools.partial(gla_chunk, mask=_gla_mask(d))
        _, vjp = jax.vjp(f, q_ref[...], k_ref[...], v_ref[...], la_ref[...], s_ref[0, 0])
        dq_ref[...], dk_ref[...], dv_ref[...], dla_ref[...], ds_s[...] = vjp((do_ref[...], ds_s[...]))

    def chunk_of(d, n):
        return _scan_chunk(d, nc - 1 - n, nc, nlc)

    k_out = pl.BlockSpec((c, kw), lambda d, n: (chunk_of(d, n), d))
    return pl.pallas_call(
        body, name="gla_bwd", grid=(2, nc),
        in_specs=[pl.BlockSpec((c, kw), lambda d, n: (chunk_of(d, n), _GQ_BLK)),
                  pl.BlockSpec((c, kw), lambda d, n: (chunk_of(d, n), _GK_BLK)),
                  pl.BlockSpec((c, vw), lambda d, n: (chunk_of(d, n), _GV_BLK)),
                  pl.BlockSpec((c, kw), lambda d, n: (chunk_of(d, n), _LA_BLK + d)),
                  pl.BlockSpec((1, 1, GLA_DV, kw), lambda d, n: (d, nc - 1 - n, 0, 0)),
                  pl.BlockSpec((c, vw), lambda d, n: (chunk_of(d, n), 0))],
        out_specs=[k_out, k_out, pl.BlockSpec((c, vw), lambda d, n: (chunk_of(d, n), d)), k_out],
        out_shape=[jax.ShapeDtypeStruct((t, 2 * kw), F32), jax.ShapeDtypeStruct((t, 2 * kw), F32),
                   jax.ShapeDtypeStruct((t, 2 * vw), F32), jax.ShapeDtypeStruct((t, 2 * kw), F32)],
        scratch_shapes=[pltpu.VMEM((GLA_DV, kw), F32)],
        compiler_params=_params(("parallel", "arbitrary")),
    )(p, z, z, p, states, do)


def _adam_tile(w, g, m, v):
    m = ADAM_B1 * m + (1.0 - ADAM_B1) * g
    v = ADAM_B2 * v + (1.0 - ADAM_B2) * (g * g)
    m_hat = m / (1.0 - ADAM_B1 ** ADAM_STEP)
    v_hat = v / (1.0 - ADAM_B2 ** ADAM_STEP)
    delta = -ADAM_LR * (m_hat / (jnp.sqrt(v_hat) + ADAM_EPS) + ADAM_WD * w)
    return delta, m, v


def adamw(name, w, g, m, v):
    shape = w.shape
    cols = shape[-1] if w.ndim > 1 and shape[-1] >= LANES else int(np.prod(shape))
    rows = int(np.prod(shape)) // cols
    tr = rows
    for cand in (512, 256, 128, 64, 32, 16, 8):
        if rows % cand == 0 and cand * cols * 4 <= (1 << 20):
            tr = cand
            break
    flat = [a.reshape(rows, cols) for a in (w, g, m, v)]

    def body(w_ref, g_ref, m_ref, v_ref, d_ref, mo_ref, vo_ref):
        d_ref[...], mo_ref[...], vo_ref[...] = _adam_tile(w_ref[...], g_ref[...], m_ref[...], v_ref[...])

    spec = pl.BlockSpec((tr, cols), lambda i: (i, 0))
    outs = pl.pallas_call(
        body, name=name, grid=(rows // tr,),
        in_specs=[spec] * 4, out_specs=[spec] * 3,
        out_shape=[jax.ShapeDtypeStruct((rows, cols), F32)] * 3,
        compiler_params=_params(("parallel",)),
    )(*flat)
    return tuple(o.reshape(shape) for o in outs)


def add_n(name, terms):
    rows, cols = terms[0].shape
    tr = rows
    for cand in (512, 256, 128, 64, 32, 16, 8):
        if rows % cand == 0 and cand * cols * 4 <= (1 << 20):
            tr = cand
            break

    def body(*refs):
        acc = refs[0][...]
        for r in refs[1:-1]:
            acc = acc + r[...]
        refs[-1][...] = acc

    spec = pl.BlockSpec((tr, cols), lambda i: (i, 0))
    return pl.pallas_call(
        body, name=name, grid=(rows // tr,), in_specs=[spec] * len(terms), out_specs=spec,
        out_shape=jax.ShapeDtypeStruct((rows, cols), F32), compiler_params=_params(("parallel",)),
    )(*terms)


MESH = pl.DeviceIdType.MESH
_HBM = pl.BlockSpec(memory_space=pltpu.HBM)
N_CHIPS = 4
N_DEV = 8


def _place():
    x, y, c = lax.axis_index("x"), lax.axis_index("y"), lax.axis_index("c")
    chips = [(1 - x, y), (x, 1 - y), (1 - x, 1 - y)]
    return x, y, c, chips


def _remote(src, dst, send_sem, recv_sem, to):
    return pltpu.make_async_remote_copy(src_ref=src, dst_ref=dst, send_sem=send_sem, recv_sem=recv_sem,
                                        device_id=to, device_id_type=MESH)


def all_gather_small(name, v):
    m_per, n = v.shape

    def body(x_ref, out_ref, send_sems, recv_sems, local_sem):
        x, y, c, chips = _place()
        me, sibling = (x, y, c), (x, y, 1 - c)

        def rows(px, py, pc):
            return out_ref.at[pl.ds((4 * px + 2 * py + pc) * m_per, m_per), :]

        def copy(k, block, to, src=None):
            return _remote(rows(*block) if src is None else src, rows(*block), send_sems.at[k], recv_sems.at[k], to)

        mine = pltpu.make_async_copy(x_ref, rows(*me), local_sem)
        mine.start()
        first = [copy(0, me, sibling, src=x_ref)]
        first += [copy(1 + j, me, (*chip, c), src=x_ref) for j, chip in enumerate(chips)]
        for cp in first:
            cp.start()
        passed = [copy(4 + j, (*chip, c), sibling) for j, chip in enumerate(chips)]
        for j, chip in enumerate(chips):
            copy(1 + j, (*chip, c), me).wait_recv()
            passed[j].start()
        copy(0, sibling, me).wait_recv()
        for j, chip in enumerate(chips):
            copy(4 + j, (*chip, 1 - c), me).wait_recv()
        for cp in first + passed:
            cp.wait_send()
        mine.wait()

    return pl.pallas_call(
        body, name=name,
        out_shape=jax.ShapeDtypeStruct((N_DEV * m_per, n), v.dtype),
        in_specs=[pl.BlockSpec(memory_space=pltpu.VMEM)],
        out_specs=pl.BlockSpec(memory_space=pltpu.VMEM),
        scratch_shapes=[pltpu.SemaphoreType.DMA((7,)), pltpu.SemaphoreType.DMA((7,)), pltpu.SemaphoreType.DMA],
        compiler_params=pltpu.CompilerParams(vmem_limit_bytes=VMEM_LIMIT),
    )(v)


def all_gather_shards(name, shards):
    nt = len(shards)

    def body(*refs):
        x_refs, out_refs = refs[:nt], refs[nt:2 * nt]
        send_sems, recv_sems, local_sems = refs[2 * nt:]
        x, y, c, chips = _place()
        q = 2 * x + y
        sibling = (x, y, 1 - c)
        sends, local = [], []
        for t in range(nt):
            half = x_refs[t].shape[0] // 2

            def part(qq, hh, t=t, half=half):
                return out_refs[t].at[qq, pl.ds(hh * half, half), :]

            cp = pltpu.make_async_copy(x_refs[t], out_refs[t].at[q], local_sems.at[t])
            cp.start()
            local.append(cp)
            for j, chip in enumerate(chips):
                cp = _remote(x_refs[t].at[pl.ds(c * half, half), :], part(q, c), send_sems.at[6 * t + j],
                             recv_sems.at[6 * t + j], (*chip, c))
                cp.start()
                sends.append(cp)
        for t in range(nt):
            half = x_refs[t].shape[0] // 2

            def part(qq, hh, t=t, half=half):
                return out_refs[t].at[qq, pl.ds(hh * half, half), :]

            for j, chip in enumerate(chips):
                qj = 2 * chip[0] + chip[1]
                _remote(part(qj, c), part(qj, c), send_sems.at[6 * t + j], recv_sems.at[6 * t + j], (*chip, c)).wait_recv()
                cp = _remote(part(qj, c), part(qj, c), send_sems.at[6 * t + 3 + j], recv_sems.at[6 * t + 3 + j], sibling)
                cp.start()
                sends.append(cp)
        for t in range(nt):
            half = x_refs[t].shape[0] // 2
            for j, chip in enumerate(chips):
                qj = 2 * chip[0] + chip[1]
                dst = out_refs[t].at[qj, pl.ds((1 - c) * half, half), :]
                _remote(dst, dst, send_sems.at[6 * t + 3 + j], recv_sems.at[6 * t + 3 + j], sibling).wait_recv()
        for cp in sends:
            cp.wait_send()
        for cp in local:
            cp.wait()

    return pl.pallas_call(
        body, name=name,
        out_shape=[jax.ShapeDtypeStruct((N_CHIPS,) + s.shape, s.dtype) for s in shards],
        in_specs=[_HBM] * nt, out_specs=[_HBM] * nt,
        scratch_shapes=[pltpu.SemaphoreType.DMA((6 * nt,)), pltpu.SemaphoreType.DMA((6 * nt,)),
                        pltpu.SemaphoreType.DMA((nt,))],
    )(*shards)


def exchange_sibling_halves(name, grads):
    nt = len(grads)

    def body(*refs):
        g_refs, out_refs = refs[:nt], refs[nt:2 * nt]
        send_sems, recv_sems = refs[2 * nt:]
        x, y, c, _ = _place()
        sibling = (x, y, 1 - c)
        cps = []
        for t in range(nt):
            half = g_refs[t].shape[1] // 2
            cp = _remote(g_refs[t].at[:, pl.ds((1 - c) * half, half), :], out_refs[t], send_sems.at[t], recv_sems.at[t], sibling)
            cp.start()
            cps.append(cp)
        for cp in cps:
            cp.wait()

    return pl.pallas_call(
        body, name=name,
        out_shape=[jax.ShapeDtypeStruct((g.shape[0], g.shape[1] // 2, g.shape[2]), g.dtype) for g in grads],
        in_specs=[_HBM] * nt, out_specs=[_HBM] * nt,
        scratch_shapes=[pltpu.SemaphoreType.DMA((nt,)), pltpu.SemaphoreType.DMA((nt,))],
    )(*grads)


def scatter_to_chips(name, sums):
    nt = len(sums)

    def body(*refs):
        s_refs, out_refs = refs[:nt], refs[nt:2 * nt]
        send_sems, recv_sems = refs[2 * nt:]
        x, y, c, chips = _place()
        cps = []
        for t in range(nt):
            for j, chip in enumerate(chips):
                qj = 2 * chip[0] + chip[1]
                cp = _remote(s_refs[t].at[qj], out_refs[t].at[j], send_sems.at[3 * t + j], recv_sems.at[3 * t + j], (*chip, c))
                cp.start()
                cps.append(cp)
        for cp in cps:
            cp.wait()

    return pl.pallas_call(
        body, name=name,
        out_shape=[jax.ShapeDtypeStruct((3,) + s.shape[1:], s.dtype) for s in sums],
        in_specs=[_HBM] * nt, out_specs=[_HBM] * nt,
        scratch_shapes=[pltpu.SemaphoreType.DMA((3 * nt,)), pltpu.SemaphoreType.DMA((3 * nt,))],
    )(*sums)


def join_sibling_halves(name, halves):
    nt = len(halves)

    def body(*refs):
        h_refs, out_refs = refs[:nt], refs[nt:2 * nt]
        send_sems, recv_sems, local_sems = refs[2 * nt:]
        x, y, c, _ = _place()
        sibling = (x, y, 1 - c)
        cps, local = [], []
        for t in range(nt):
            half = h_refs[t].shape[0]
            dst = out_refs[t].at[pl.ds(c * half, half), :]
            lc = pltpu.make_async_copy(h_refs[t], dst, local_sems.at[t])
            lc.start()
            local.append(lc)
            cp = _remote(h_refs[t], dst, send_sems.at[t], recv_sems.at[t], sibling)
            cp.start()
            cps.append(cp)
        for t, cp in enumerate(cps):
            half = h_refs[t].shape[0]
            other = out_refs[t].at[pl.ds((1 - c) * half, half), :]
            _remote(other, other, send_sems.at[t], recv_sems.at[t], sibling).wait_recv()
            cp.wait_send()
        for lc in local:
            lc.wait()

    return pl.pallas_call(
        body, name=name,
        out_shape=[jax.ShapeDtypeStruct((2 * h.shape[0], h.shape[1]), h.dtype) for h in halves],
        in_specs=[_HBM] * nt, out_specs=[_HBM] * nt,
        scratch_shapes=[pltpu.SemaphoreType.DMA((nt,)), pltpu.SemaphoreType.DMA((nt,)), pltpu.SemaphoreType.DMA((nt,))],
    )(*halves)


def sum_device_blocks(name, g):
    n = g.shape[1]

    def body(g_ref, o_ref):
        acc = g_ref[0:8, :]
        for d in range(1, N_DEV):
            acc = acc + g_ref[8 * d:8 * (d + 1), :]
        o_ref[...] = acc

    return pl.pallas_call(body, name=name, out_shape=jax.ShapeDtypeStruct((8, n), F32),
                          compiler_params=pltpu.CompilerParams(vmem_limit_bytes=VMEM_LIMIT))(g)


class LayerWeights(NamedTuple):
    w_main: jax.Array
    w_gate: jax.Array
    w_out: jax.Array
    w_up: jax.Array
    w_down: jax.Array
    norm1_g: jax.Array
    q_g: jax.Array
    k_g: jax.Array
    lg: jax.Array
    ret_g: jax.Array
    gate_up: jax.Array
    gate_b: jax.Array
    gla_g: jax.Array
    norm2_g: jax.Array
    conv_w: jax.Array
    conv_b: jax.Array


def _mod(mods, k):
    return mods[:, k:k + 1, :]


def _prep_args(z, zg, cos, sin, w):
    rows = [Row(z, Z_AV, 0), Row(z, 512, Z_RQ // 512), Row(z, 512, Z_RK // 512), Row(z, 256, Z_GQ // 256),
            Row(zg, LANES, 0), Row(cos, HEAD_DIM, 0, False), Row(sin, HEAD_DIM, 0, False)]
    return rows, [Par(w.q_g), Par(w.k_g), Par(w.gate_up), Par(w.gate_b)]


def _post_args(o_att, o_ret, o_gla, z, w):
    rows = [Row(o_att, 1024), Row(o_ret, 512, 0), Row(o_ret, 512, 1, False), Row(o_gla, 512, 0), Row(o_gla, 512, 1, False),
            Row(z, 512, Z_RG // 512), Row(z, 512, Z_GR // 512)]
    return rows, [Par(w.ret_g), Par(w.gla_g)]


def layer_fwd(l, xs, mods, w, cos, sin, n_lat):
    t, d = xs.shape
    tag = f"l{l}_"
    nm1 = [Par(w.norm1_g), Par(_mod(mods, 0), True), Par(_mod(mods, 1), True)]
    (h,) = row_map(tag + "norm1", normmod_tile, [Row(xs, d)], nm1, [(d, BF16)], t, n_lat)
    z = matmul(tag + "in_proj", h, w.w_main)
    zg = matmul(tag + "gate_proj", h, w.w_gate)
    rows, pars = _prep_args(z, zg, cos, sin, w)
    (p,) = row_map(tag + "prep", prep_tile, rows, pars, [(P_W, F32)], t, n_lat)
    o_att, lse = attn_fwd(p, z, n_lat)
    o_ret, s_ret = ret_fwd(p, z, w.lg, n_lat)
    o_gla, s_gla = gla_fwd(p, z, n_lat)
    rows, pars = _post_args(o_att, o_ret, o_gla, z, w)
    (m,) = row_map(tag + "post", post_tile, rows, pars, [(d, BF16)], t, n_lat)
    y = matmul(tag + "out_proj", m, w.w_out)
    (x1,) = row_map(tag + "resid1", resid_tile, [Row(xs, d), Row(y, d)], [Par(_mod(mods, 2), True)], [(d, F32)], t, n_lat)
    nm2 = [Par(w.norm2_g), Par(_mod(mods, 3), True), Par(_mod(mods, 4), True)]
    (h2,) = row_map(tag + "norm2", normmod_tile, [Row(x1, d)], nm2, [(d, BF16)], t, n_lat)
    u = matmul(tag + "up_proj", h2, w.w_up)
    g = convglu(tag + "convglu", u, w.conv_w, w.conv_b, n_lat)
    yd = matmul(tag + "down_proj", g, w.w_down)
    (x2,) = row_map(tag + "resid2", resid_tile, [Row(x1, d), Row(yd, d)], [Par(_mod(mods, 5), True)], [(d, F32)], t, n_lat)
    saved = dict(xs=xs, h=h, z=z, zg=zg, p=p, o_att=o_att, lse=lse, o_ret=o_ret, s_ret=s_ret, o_gla=o_gla, s_gla=s_gla,
                 m=m, y=y, x1=x1, h2=h2, u=u, g=g, yd=yd)
    return x2, saved


def _sum_dirs(a):
    w = a.shape[1] // 2
    return a[:, :w] + a[:, w:]


def layer_bwd(l, dx2, s, mods, w, cos, sin, n_lat):
    t, d = dx2.shape
    tag = f"l{l}_b_"
    dyd, dgate5 = row_vjp(tag + "resid2", resid_tile, [Row(s["x1"], d, 0, False), Row(s["yd"], d)],
                          [Par(_mod(mods, 5), True)], [dx2], t, n_lat)
    dg = matmul(tag + "down_dx", dyd, w.w_down, tb=True)
    dw_down = matmul(tag + "down_dw", s["g"], dyd, ta=True)
    du, dcw, dcb = convglu_bwd(tag + "convglu", s["u"], w.conv_w, w.conv_b, dg, n_lat)
    dh2 = matmul(tag + "up_dx", du, w.w_up, tb=True)
    dw_up = matmul(tag + "up_dw", s["h2"], du, ta=True)
    nm2 = [Par(w.norm2_g), Par(_mod(mods, 3), True), Par(_mod(mods, 4), True)]
    dx1, dg2, dshift3, dscale4 = row_vjp(tag + "norm2", normmod_tile, [Row(s["x1"], d)], nm2, [dh2], t, n_lat, add_to_first=dx2)
    dy, dgate2 = row_vjp(tag + "resid1", resid_tile, [Row(s["xs"], d, 0, False), Row(s["y"], d)],
                         [Par(_mod(mods, 2), True)], [dx1], t, n_lat)
    dm = matmul(tag + "out_dx", dy, w.w_out, tb=True)
    dw_out = matmul(tag + "out_dw", s["m"], dy, ta=True)
    rows, pars = _post_args(s["o_att"], s["o_ret"], s["o_gla"], s["z"], w)
    do_att, do_ret, do_gla, d_rg, d_gr, d_ret_g, d_gla_g = row_vjp(tag + "post", post_tile, rows, pars, [dm], t, n_lat)
    dq_a, dk_a, dv_a = attn_bwd(s["p"], s["z"], s["o_att"], s["lse"], do_att, n_lat)
    dq_r, dk_r, dv_r, dlg = ret_bwd(s["p"], s["z"], w.lg, s["s_ret"], do_ret, n_lat)
    dq_g, dk_g, dv_g, dla = gla_bwd(s["p"], s["z"], s["s_gla"], do_gla, n_lat)
    dp = jnp.concatenate([dq_a, dk_a, _sum_dirs(dq_g), _sum_dirs(dq_r), _sum_dirs(dk_r), dla], axis=1)
    rows, pars = _prep_args(s["z"], s["zg"], cos, sin, w)
    d_zqk, d_zrq, d_zrk, d_zgq, dzg, d_qg, d_kg, d_up, d_gb = row_vjp(tag + "prep", prep_tile, rows, pars, [dp], t, n_lat)
    dz = jnp.concatenate([d_zqk, dv_a, d_zrq, d_zrk, _sum_dirs(dv_r), d_rg, d_zgq, _sum_dirs(dk_g), _sum_dirs(dv_g), d_gr], axis=1)
    dh_gate = matmul(tag + "gate_dx", dzg, w.w_gate, tb=True)
    dh = matmul(tag + "in_dx", dz, w.w_main, tb=True, add=dh_gate)
    dw_main = matmul(tag + "in_dw", s["h"], dz, ta=True)
    dw_gate = matmul(tag + "gate_dw", s["h"], dzg, ta=True)
    nm1 = [Par(w.norm1_g), Par(_mod(mods, 0), True), Par(_mod(mods, 1), True)]
    dx, dg1, dshift0, dscale1 = row_vjp(tag + "norm1", normmod_tile, [Row(s["xs"], d)], nm1, [dh], t, n_lat, add_to_first=dx1)
    dmods = jnp.concatenate([dshift0, dscale1, dgate2, dshift3, dscale4, dgate5], axis=1)
    grads = dict(w_main=dw_main, w_gate=dw_gate, w_out=dw_out, w_up=dw_up, w_down=dw_down, norm1_g=dg1, q_g=d_qg, k_g=d_kg,
                 lg=dlg, ret_g=d_ret_g, gate_up=d_up, gate_b=d_gb, gla_g=d_gla_g, norm2_g=dg2, conv_w=dcw, conv_b=dcb)
    return dx, dmods, grads


def rope_tables(n_lat, n_ctx):
    rows = n_lat // GRID_W
    row = jnp.repeat(jnp.arange(rows, dtype=F32), GRID_W)
    col = jnp.tile(jnp.arange(GRID_W, dtype=F32), rows)
    n_freq = HEAD_DIM // 4
    inv_freq = ROPE_THETA ** (-jnp.arange(n_freq, dtype=F32) / n_freq)
    ang = jnp.concatenate([row[:, None] * inv_freq, col[:, None] * inv_freq], axis=-1)
    cos, sin = jnp.cos(ang), jnp.sin(ang)
    cos = jnp.concatenate([jnp.concatenate([cos, cos], axis=1), jnp.ones((n_ctx, HEAD_DIM), F32)], axis=0)
    sin = jnp.concatenate([jnp.concatenate([-sin, sin], axis=1), jnp.zeros((n_ctx, HEAD_DIM), F32)], axis=0)
    return cos, sin


def local_step(xs, target, mods, weights, final_g, n_lat):
    t, d = xs.shape
    cos, sin = rope_tables(n_lat, t - n_lat)
    saved = []
    h = xs
    for l, w in enumerate(weights):
        h, s = layer_fwd(l, h, mods[l], w, cos, sin, n_lat)
        saved.append(s)
    loss, dlat, dgf = final_loss(h, target, final_g, n_lat)
    dx = jnp.concatenate([dlat, jnp.zeros((t - n_lat, d), F32)], axis=0)
    dmods, grads = [None] * len(weights), [None] * len(weights)
    for l in reversed(range(len(weights))):
        dx, dmods[l], grads[l] = layer_bwd(l, dx, saved[l], mods[l], weights[l], cos, sin, n_lat)
    return loss, dx, dmods, grads, dgf


WEIGHT_NAMES = ("c_ctx", "ada_w", "ada_b", "norm1_g", "w_in", "q_norm_g", "k_norm_g", "ret_log_decay", "ret_norm_g",
                "gla_gate_up", "gla_gate_b", "gla_norm_g", "w_out", "norm2_g", "w_up", "conv_w", "conv_b", "w_down", "final_norm_g")
PACK_QUANTUM = 8 * LANES


def _pack(arrays):
    flat = jnp.concatenate([a.reshape(-1).astype(F32) for a in arrays])
    n = -(-flat.shape[0] // PACK_QUANTUM) * PACK_QUANTUM
    return jnp.pad(flat, (0, n - flat.shape[0])).reshape(8, n // 8)


def _unpack(flat2d, shapes):
    out, at = [], 0
    for s in shapes:
        size = int(np.prod(s))
        out.append(flat2d[:, at:at + size].reshape((flat2d.shape[0],) + tuple(s)))
        at += size
    return out


def _per_device(gathered):
    return gathered.reshape(N_DEV, -1)


def _from_chips(per_device, axis):
    chips = per_device[0::2]
    moved = jnp.moveaxis(chips, 0, axis)
    shape = moved.shape
    return moved.reshape(shape[:axis] + (shape[axis] * shape[axis + 1],) + shape[axis + 2:])


def kernel(x, c, ctx, c_ctx, ada_w, ada_b, norm1_g, w_in, q_norm_g, k_norm_g, ret_log_decay, ret_norm_g, gla_gate_up, gla_gate_b, gla_norm_g, w_out, norm2_g, w_up, conv_w, conv_b, w_down, final_norm_g, loss_target, m_c_ctx, m_ada_w, m_ada_b, m_norm1_g, m_w_in, m_q_norm_g, m_k_norm_g, m_ret_log_decay, m_ret_norm_g, m_gla_gate_up, m_gla_gate_b, m_gla_norm_g, m_w_out, m_norm2_g, m_w_up, m_conv_w, m_conv_b, m_w_down, m_final_norm_g, v_c_ctx, v_ada_w, v_ada_b, v_norm1_g, v_w_in, v_q_norm_g, v_k_norm_g, v_ret_log_decay, v_ret_norm_g, v_gla_gate_up, v_gla_gate_b, v_gla_norm_g, v_w_out, v_norm2_g, v_w_up, v_conv_w, v_conv_b, v_w_down, v_final_norm_g):
    weights = dict(zip(WEIGHT_NAMES, (c_ctx, ada_w, ada_b, norm1_g, w_in, q_norm_g, k_norm_g, ret_log_decay, ret_norm_g,
                                      gla_gate_up, gla_gate_b, gla_norm_g, w_out, norm2_g, w_up, conv_w, conv_b, w_down, final_norm_g)))
    mom_m = dict(zip(WEIGHT_NAMES, (m_c_ctx, m_ada_w, m_ada_b, m_norm1_g, m_w_in, m_q_norm_g, m_k_norm_g, m_ret_log_decay, m_ret_norm_g,
                                    m_gla_gate_up, m_gla_gate_b, m_gla_norm_g, m_w_out, m_norm2_g, m_w_up, m_conv_w, m_conv_b, m_w_down, m_final_norm_g)))
    mom_v = dict(zip(WEIGHT_NAMES, (v_c_ctx, v_ada_w, v_ada_b, v_norm1_g, v_w_in, v_q_norm_g, v_k_norm_g, v_ret_log_decay, v_ret_norm_g,
                                    v_gla_gate_up, v_gla_gate_b, v_gla_norm_g, v_w_out, v_norm2_g, v_w_up, v_conv_w, v_conv_b, v_w_down, v_final_norm_g)))
    depth, d = norm1_g.shape
    assert d == D_MODEL and x.shape[0] == 1
    n_lat, n_ctx, f = x.shape[1], ctx.shape[1], conv_b.shape[1]
    assert n_lat % ROW_TILE == 0 and n_ctx % ROW_TILE == 0 and f % FFN_COL_TILE == 0 and f % N_CHIPS == 0
    tc = FFN_COL_TILE
    n_in = w_in.shape[2]
    n_ada = ada_w.shape[2]
    xi, yi, ci = lax.axis_index("x"), lax.axis_index("y"), lax.axis_index("c")
    chip = 2 * xi + yi
    dev = 2 * chip + ci

    shards = [w_in.astype(BF16).reshape(depth * d, n_in), w_out.astype(BF16).reshape(-1, d),
              w_up.astype(BF16).reshape(depth * d, -1), w_down.astype(BF16).reshape(-1, d)]
    g_in, g_out, g_up, g_down = all_gather_shards("gather_weights", shards)
    w_in_full = g_in.reshape(N_CHIPS, depth, d, n_in).transpose(1, 2, 0, 3).reshape(depth, d, N_CHIPS * n_in)
    w_out_full = g_out.reshape(N_CHIPS, depth, d // N_CHIPS, d).transpose(1, 0, 2, 3).reshape(depth, d, d)
    w_up_full = g_up.reshape(N_CHIPS, depth, d, 2 * f // N_CHIPS).transpose(1, 2, 0, 3).reshape(depth, d, 2, f // tc, tc)
    w_up_full = w_up_full.transpose(0, 1, 3, 2, 4).reshape(depth, d, 2 * f)
    w_down_full = g_down.reshape(N_CHIPS, depth, f // N_CHIPS, d).transpose(1, 0, 2, 3).reshape(depth, f, d)
    w_main = w_in_full[:, :, :N_MAIN]
    w_gate = jnp.pad(w_in_full[:, :, N_MAIN:], ((0, 0), (0, 0), (0, LANES - N_GATE)))

    small_shapes = [c.shape[1:], conv_w.shape, gla_gate_up.shape, gla_gate_b.shape]
    got = _per_device(all_gather_small("gather_small", _pack([c, conv_w, gla_gate_up, gla_gate_b])))
    c_all, conv_w_sh, gate_up_sh, gate_b_sh = _unpack(got, small_shapes)
    conv_w_full = _from_chips(conv_w_sh, 2)
    gate_up_full = _from_chips(gate_up_sh, 3)
    gate_b_full = _from_chips(gate_b_sh, 2)

    act = jnp.zeros((16, d), F32).at[0:N_DEV].set(jax.nn.silu(c_all)).at[N_DEV].set(jax.nn.silu(c_ctx))
    mod_sh = jnp.stack([matmul(f"ada_fwd{l}", act, ada_w[l]) for l in range(depth)])
    got = _per_device(all_gather_small("gather_mods", _pack([mod_sh])))
    (mod_sh_all,) = _unpack(got, [mod_sh.shape])
    mod_full = _from_chips(mod_sh_all, 2) + ada_b[:, None, :]
    mod_mine = lax.dynamic_index_in_dim(mod_full, dev, axis=1, keepdims=False)
    mods = [jnp.stack([mod_mine[l].reshape(N_MOD, d), mod_full[l, N_DEV].reshape(N_MOD, d)]) for l in range(depth)]

    layer_w = []
    for l in range(depth):
        up = jnp.zeros((2, LANES, GLA_HEADS * GLA_DK), F32)
        up = up.at[0, 0:GLA_RANK].set(gate_up_full[l, 0]).at[1, GLA_RANK:2 * GLA_RANK].set(gate_up_full[l, 1])
        layer_w.append(LayerWeights(
            w_main=w_main[l], w_gate=w_gate[l], w_out=w_out_full[l], w_up=w_up_full[l], w_down=w_down_full[l],
            norm1_g=norm1_g[l].reshape(1, 1, d), q_g=q_norm_g[l].reshape(1, 1, HEAD_DIM), k_g=k_norm_g[l].reshape(1, 1, HEAD_DIM),
            lg=ret_log_decay[l].reshape(2, RET_HEADS, 1, 1), ret_g=ret_norm_g[l].reshape(1, 1, HEAD_DIM),
            gate_up=up.reshape(1, 2 * LANES, -1), gate_b=gate_b_full[l].reshape(1, 2, -1), gla_g=gla_norm_g[l].reshape(1, 1, HEAD_DIM),
            norm2_g=norm2_g[l].reshape(1, 1, d), conv_w=conv_w_full[l], conv_b=conv_b[l].reshape(1, f)))
    xs = jnp.concatenate([x[0], ctx[0]], axis=0)
    loss, dx, dmods, grads, dgf = local_step(xs, loss_target[0], mods, layer_w, final_norm_g.reshape(1, d), n_lat)

    def gate_up_grad(g):
        return jnp.stack([g[0, 0:GLA_RANK], g[0, LANES + GLA_RANK:LANES + 2 * GLA_RANK]])

    per_layer = [[dmods[l][0], dmods[l][1], grads[l]["norm1_g"], grads[l]["norm2_g"], grads[l]["q_g"], grads[l]["k_g"],
                  grads[l]["ret_g"], grads[l]["gla_g"], grads[l]["lg"], gate_up_grad(grads[l]["gate_up"]), grads[l]["gate_b"],
                  grads[l]["conv_w"], grads[l]["conv_b"]] for l in range(depth)]
    layer_shapes = [(N_MOD * d,), (N_MOD * d,), (d,), (d,), (HEAD_DIM,), (HEAD_DIM,), (HEAD_DIM,), (HEAD_DIM,), (2, RET_HEADS),
                    (2, GLA_RANK, GLA_HEADS * GLA_DK), (2, GLA_HEADS * GLA_DK), (3, f), (f,)]
    packed = _pack([a for lay in per_layer for a in lay] + [dgf, loss[0, 0:1]])
    gathered = all_gather_small("gather_small_grads", packed)
    every = _unpack(_per_device(gathered), layer_shapes * depth + [(d,), (1,)])
    total = _unpack(sum_device_blocks("sum_small_grads", gathered).reshape(1, -1), layer_shapes * depth + [(d,), (1,)])
    nl = len(layer_shapes)

    def tot(l, k):
        return total[l * nl + k][0]

    out = {"norm1_g": jnp.stack([tot(l, 2) for l in range(depth)]), "norm2_g": jnp.stack([tot(l, 3) for l in range(depth)]),
           "q_norm_g": jnp.stack([tot(l, 4) for l in range(depth)]), "k_norm_g": jnp.stack([tot(l, 5) for l in range(depth)]),
           "ret_norm_g": jnp.stack([tot(l, 6) for l in range(depth)]), "gla_norm_g": jnp.stack([tot(l, 7) for l in range(depth)]),
           "ret_log_decay": jnp.stack([tot(l, 8) for l in range(depth)]),
           "gla_gate_up": lax.dynamic_slice_in_dim(jnp.stack([tot(l, 9) for l in range(depth)]), chip * gla_gate_up.shape[3], gla_gate_up.shape[3], axis=3),
           "gla_gate_b": lax.dynamic_slice_in_dim(jnp.stack([tot(l, 10) for l in range(depth)]), chip * gla_gate_b.shape[2], gla_gate_b.shape[2], axis=2),
           "conv_w": lax.dynamic_slice_in_dim(jnp.stack([tot(l, 11) for l in range(depth)]), chip * conv_w.shape[2], conv_w.shape[2], axis=2),
           "conv_b": jnp.stack([tot(l, 12) for l in range(depth)]),
           "final_norm_g": total[depth * nl][0],
           "ada_b": jnp.stack([tot(l, 0) + tot(l, 1) for l in range(depth)])}
    loss_total = total[depth * nl + 1][0, 0]

    dmod_all = jnp.zeros((depth, 16, N_MOD * d), F32)
    for l in range(depth):
        dmod_all = dmod_all.at[l, 0:N_DEV].set(every[l * nl][:, :]).at[l, N_DEV].set(tot(l, 1))
    dmod_cols = lax.dynamic_slice_in_dim(dmod_all, chip * n_ada, n_ada, axis=2)
    out["ada_w"] = jnp.stack([matmul(f"ada_dw{l}", act, dmod_cols[l], ta=True) for l in range(depth)])
    dact = matmul("ada_dx0", dmod_cols[0], ada_w[0], tb=True)
    for l in range(1, depth):
        dact = matmul(f"ada_dx{l}", dmod_cols[l], ada_w[l], tb=True, add=dact)
    got = _per_device(all_gather_small("gather_dcctx", _pack([dact[N_DEV]])))[0::2, :d]
    dsilu = got[0] + got[1] + got[2] + got[3]
    sig = jax.nn.sigmoid(c_ctx)
    out["c_ctx"] = dsilu * (sig + c_ctx * sig * (1.0 - sig))

    dw_in = jnp.stack([jnp.concatenate([grads[l]["w_main"], grads[l]["w_gate"][:, :N_GATE]], axis=1) for l in range(depth)])
    dw_up = jnp.stack([grads[l]["w_up"].reshape(d, f // tc, 2, tc).transpose(0, 2, 1, 3).reshape(d, 2 * f) for l in range(depth)])
    dw_out = jnp.stack([grads[l]["w_out"] for l in range(depth)])
    dw_down = jnp.stack([grads[l]["w_down"] for l in range(depth)])
    pieces = [dw_in.reshape(depth, d, N_CHIPS, n_in).transpose(2, 0, 1, 3).reshape(N_CHIPS, depth * d, n_in),
              dw_out.reshape(depth, N_CHIPS, d // N_CHIPS, d).transpose(1, 0, 2, 3).reshape(N_CHIPS, -1, d),
              dw_up.reshape(depth, d, N_CHIPS, 2 * f // N_CHIPS).transpose(2, 0, 1, 3).reshape(N_CHIPS, depth * d, -1),
              dw_down.reshape(depth, N_CHIPS, f // N_CHIPS, d).transpose(1, 0, 2, 3).reshape(N_CHIPS, -1, d)]
    from_sibling = exchange_sibling_halves("rs_sibling", pieces)
    chip_sums = []
    for k, (pc, sib) in enumerate(zip(pieces, from_sibling)):
        half = pc.shape[1] // 2
        own = lax.dynamic_slice_in_dim(pc, ci * half, half, axis=1)
        chip_sums.append(add_n(f"rs_add_sibling{k}", [own.reshape(-1, pc.shape[2]), sib.reshape(-1, pc.shape[2])]).reshape(sib.shape))
    from_chips = scatter_to_chips("rs_chips", chip_sums)
    halves = []
    for k, (cs, got) in enumerate(zip(chip_sums, from_chips)):
        own = lax.dynamic_index_in_dim(cs, chip, axis=0, keepdims=False)
        halves.append(add_n(f"rs_add_chips{k}", [own, got[0], got[1], got[2]]))
    full = join_sibling_halves("rs_join", halves)
    out["w_in"] = full[0].reshape(w_in.shape)
    out["w_out"] = full[1].reshape(w_out.shape)
    out["w_up"] = full[2].reshape(w_up.shape)
    out["w_down"] = full[3].reshape(w_down.shape)

    deltas, new_m, new_v = {}, {}, {}
    for name in WEIGHT_NAMES:
        out[name] = out[name].reshape(weights[name].shape)
        deltas[name], new_m[name], new_v[name] = adamw("adamw_" + name, weights[name], out[name], mom_m[name], mom_v[name])
    grad_x = dx[:n_lat].reshape(x.shape)
    return (loss_total, grad_x, *[out[n] for n in WEIGHT_NAMES], *[deltas[n] for n in WEIGHT_NAMES],
            *[new_m[n] for n in WEIGHT_NAMES], *[new_v[n] for n in WEIGHT_NAMES])
```

```python
import functools
from typing import NamedTuple

import numpy as np
import jax
import jax.numpy as jnp
from jax import lax
from jax.experimental import pallas as pl
from jax.experimental.pallas import tpu as pltpu

F32 = jnp.float32
BF16 = jnp.bfloat16

D_MODEL = 2048
HEAD_DIM = 128
ATT_Q_HEADS = 8
ATT_KV_HEADS = 2
ATT_GROUP = ATT_Q_HEADS // ATT_KV_HEADS
RET_HEADS = 4
GLA_HEADS = 4
GLA_DK = 64
GLA_DV = 128
GLA_RANK = 16
GLA_TAU = 16.0
RET_CHUNK = 128
GLA_CHUNK = 64
GRID_W = 64
ROPE_THETA = 10000.0
N_MOD = 6
EPS = 1e-6
N_MAIN = 5120
N_GATE = 2 * GLA_RANK
LANES = 128
ROW_TILE = 256
FFN_COL_TILE = 256
VMEM_LIMIT = 56 * 1024 * 1024

ADAM_LR = 0.001
ADAM_B1 = 0.9
ADAM_B2 = 0.999
ADAM_EPS = 1e-08
ADAM_WD = 0.01
ADAM_STEP = 10

Z_AQ, Z_AK, Z_AV = 0, 1024, 1280
Z_RQ, Z_RK, Z_RV, Z_RG = 1536, 2048, 2560, 3072
Z_GQ, Z_GK, Z_GV, Z_GR = 3584, 3840, 4096, 4608
P_AQ, P_AK, P_GQ, P_RQ, P_RK, P_LA = 0, 1024, 1280, 1536, 2048, 2560
P_W = 3072


def _params(sem=None):
    return pltpu.CompilerParams(dimension_semantics=sem, vmem_limit_bytes=VMEM_LIMIT)


def _pick(n, cands):
    for c in cands:
        if n % c == 0:
            return c
    return n


_NN = (((1,), (0,)), ((), ()))
_NT = (((1,), (1,)), ((), ()))
_TN = (((0,), (0,)), ((), ()))


def _dg(a, b, dims):
    return lax.dot_general(a.astype(BF16), b.astype(BF16), dims, preferred_element_type=F32)


@jax.custom_vjp
def bdot(a, b):
    return _dg(a, b, _NN)


def _bdot_fwd(a, b):
    return _dg(a, b, _NN), (a, b)


def _bdot_bwd(res, ct):
    a, b = res
    return _dg(ct, b, _NT), _dg(a, ct, _TN)


bdot.defvjp(_bdot_fwd, _bdot_bwd)


@jax.custom_vjp
def bdot_nt(a, b):
    return _dg(a, b, _NT)


def _bdot_nt_fwd(a, b):
    return _dg(a, b, _NT), (a, b)


def _bdot_nt_bwd(res, ct):
    a, b = res
    return _dg(ct, b, _NN), _dg(ct, a, _TN)


bdot_nt.defvjp(_bdot_nt_fwd, _bdot_nt_bwd)


@jax.custom_vjp
def bdot_tn(a, b):
    return _dg(a, b, _TN)


def _bdot_tn_fwd(a, b):
    return _dg(a, b, _TN), (a, b)


def _bdot_tn_bwd(res, ct):
    a, b = res
    return _dg(b, ct, _NT), _dg(a, ct, _NN)


bdot_tn.defvjp(_bdot_tn_fwd, _bdot_tn_bwd)


def _split3(x):
    x1 = x.astype(BF16)
    r1 = x - x1.astype(F32)
    x2 = r1.astype(BF16)
    x3 = (r1 - x2.astype(F32)).astype(BF16)
    return x1, x2, x3


def _mask_dot(mask_bf16, x, dims):
    x1, x2, x3 = _split3(x)
    f = lambda t: lax.dot_general(mask_bf16, t, dims, preferred_element_type=F32)
    return f(x1) + f(x2) + f(x3)


@jax.custom_vjp
def mask_cumsum(mask, x):
    return _mask_dot(mask.astype(BF16), x, _NN)


def _mask_cumsum_fwd(mask, x):
    return mask_cumsum(mask, x), mask


def _mask_cumsum_bwd(mask, ct):
    return jnp.zeros_like(mask), _mask_dot(mask.astype(BF16), ct, _TN)


mask_cumsum.defvjp(_mask_cumsum_fwd, _mask_cumsum_bwd)


def _roll(x, shift, axis):
    return pltpu.roll(x, shift % x.shape[axis], axis)


@functools.partial(jax.custom_vjp, nondiff_argnums=(1, 2))
def roll(x, shift, axis):
    return _roll(x, shift, axis)


def _roll_fwd(x, shift, axis):
    return _roll(x, shift, axis), None


def _roll_bwd(shift, axis, _, ct):
    return (_roll(ct, -shift, axis),)


roll.defvjp(_roll_fwd, _roll_bwd)


def rms(x):
    return x * lax.rsqrt(jnp.mean(x * x, axis=-1, keepdims=True) + EPS)


def silu(x):
    return x * (1.0 / (1.0 + jnp.exp(-x)))


def log_sigmoid(x):
    return jnp.minimum(x, 0.0) - jnp.log(1.0 + jnp.exp(-jnp.abs(x)))


def rope(t, cos, sin):
    return t * cos + roll(t, HEAD_DIM // 2, 1) * sin


def _heads(x, n, width=HEAD_DIM):
    return [x[:, h * width:(h + 1) * width] for h in range(n)]


class Row(NamedTuple):
    arr: jax.Array
    width: int
    idx: int = 0
    diff: bool = True


class Par(NamedTuple):
    arr: jax.Array
    grouped: bool = False
    diff: bool = True


def _row_specs(rows, pars, tm, n_lat_tiles):
    def grp(i):
        return jnp.minimum(i // n_lat_tiles, 1)

    specs = [pl.BlockSpec((tm, r.width), functools.partial(lambda i, k: (i, k), k=r.idx)) for r in rows]
    for p in pars:
        blk = (1,) + p.arr.shape[1:]
        if p.grouped:
            specs.append(pl.BlockSpec(blk, lambda i: (grp(i), 0, 0)))
        else:
            specs.append(pl.BlockSpec(blk, lambda i: (0, 0, 0)))
    return specs


def row_map(name, fn, rows, pars, outs, n_rows, n_lat):
    tm = ROW_TILE
    nr, npar = len(rows), len(pars)

    def body(*refs):
        vals = [r[...] for r in refs[:nr]] + [p[0] for p in refs[nr:nr + npar]]
        res = fn(*vals)
        for o, v in zip(refs[nr + npar:], res):
            o[...] = v.astype(o.dtype)

    return pl.pallas_call(
        body, name=name, grid=(n_rows // tm,),
        in_specs=_row_specs(rows, pars, tm, n_lat // tm),
        out_specs=[pl.BlockSpec((tm, w), lambda i: (i, 0)) for w, _ in outs],
        out_shape=[jax.ShapeDtypeStruct((n_rows, w), dt) for w, dt in outs],
        compiler_params=_params(("arbitrary",)),
    )(*[r.arr for r in rows], *[p.arr for p in pars])


def row_vjp(name, fn, rows, pars, cts, n_rows, n_lat, add_to_first=None):
    tm = ROW_TILE
    nr, npar, nc = len(rows), len(pars), len(cts)
    n_lat_tiles = n_lat // tm
    args = list(rows) + list(pars)
    diff_pos = [k for k, a in enumerate(args) if a.diff]
    n_add = 0 if add_to_first is None else 1

    def body(*refs):
        i = pl.program_id(0)
        vals = [r[...] for r in refs[:nr]] + [p[0] for p in refs[nr:nr + npar]]
        ct_vals = tuple(c[...] for c in refs[nr + npar:nr + npar + nc])
        out_refs = refs[nr + npar + nc + n_add:]

        def g(*dv):
            full = list(vals)
            for k, v in zip(diff_pos, dv):
                full[k] = v
            return tuple(fn(*full))

        _, vjp = jax.vjp(g, *[vals[k] for k in diff_pos])
        grads = vjp(ct_vals)
        for n, (k, o, gr) in enumerate(zip(diff_pos, out_refs, grads)):
            if k < nr:
                o[...] = gr + refs[nr + npar + nc][...] if (n == 0 and n_add) else gr
            else:
                first = (i == 0) | (i == n_lat_tiles) if args[k].grouped else (i == 0)

                @pl.when(first)
                def _():
                    o[0] = gr

                @pl.when(jnp.logical_not(first))
                def _():
                    o[0] += gr

    def grp(i):
        return jnp.minimum(i // n_lat_tiles, 1)

    out_specs, out_shape = [], []
    for k in diff_pos:
        a = args[k]
        if k < nr:
            out_specs.append(pl.BlockSpec((tm, a.width), lambda i: (i, 0)))
            out_shape.append(jax.ShapeDtypeStruct((n_rows, a.width), F32))
        else:
            blk = (1,) + a.arr.shape[1:]
            out_specs.append(pl.BlockSpec(blk, (lambda i: (grp(i), 0, 0)) if a.grouped else (lambda i: (0, 0, 0))))
            out_shape.append(jax.ShapeDtypeStruct(a.arr.shape, F32))
    extra = list(cts) + ([add_to_first] if n_add else [])
    ct_specs = [pl.BlockSpec((tm, c.shape[1]), lambda i: (i, 0)) for c in extra]
    return pl.pallas_call(
        body, name=name, grid=(n_rows // tm,),
        in_specs=_row_specs(rows, pars, tm, n_lat_tiles) + ct_specs,
        out_specs=out_specs, out_shape=out_shape,
        compiler_params=_params(("arbitrary",)),
    )(*[r.arr for r in rows], *[p.arr for p in pars], *extra)


class BView(NamedTuple):
    n: int
    k: int
    tn: int
    tk: int
    index_map: object
    lead: int = 1


def matmul(name, a, b, *, ta=False, tb=False, add=None, out_dtype=F32, view=None):
    m = a.shape[1] if ta else a.shape[0]
    if view is None:
        k = a.shape[0] if ta else a.shape[1]
        n = b.shape[0] if tb else b.shape[1]
        assert (b.shape[1] if tb else b.shape[0]) == k, (a.shape, b.shape, ta, tb)
        tn = _pick(n, (1024, 768, 512, 256, 128))
        tk = _pick(k, (512, 256, 128))
    else:
        n, k, tn, tk = view.n, view.k, view.tn, view.tk
    tm = _pick(m, (1024, 768, 512, 256, 128))
    nk = k // tk
    dims = (((0 if ta else 1,), (1 if tb else 0,)), ((), ()))

    def body(a_ref, b_ref, *rest):
        o_ref, acc = rest[-2:]
        kk = pl.program_id(2)

        @pl.when(kk == 0)
        def _():
            acc[...] = jnp.zeros_like(acc)

        acc[...] += lax.dot_general(a_ref[...].astype(BF16), b_ref[...].astype(BF16), dims,
                                    preferred_element_type=F32)

        @pl.when(kk == nk - 1)
        def _():
            r = acc[...]
            if add is not None:
                r = r + rest[0][...]
            o_ref[...] = r.astype(o_ref.dtype)

    if ta:
        a_spec = pl.BlockSpec((tk, tm), lambda i, j, kk: (kk, i))
    else:
        a_spec = pl.BlockSpec((tm, tk), lambda i, j, kk: (i, kk))
    b_tile = (tn, tk) if tb else (tk, tn)
    if view is not None:
        b_spec = pl.BlockSpec((None,) * view.lead + b_tile, view.index_map)
    elif tb:
        b_spec = pl.BlockSpec(b_tile, lambda i, j, kk: (j, kk))
    else:
        b_spec = pl.BlockSpec(b_tile, lambda i, j, kk: (kk, j))
    o_spec = pl.BlockSpec((tm, tn), lambda i, j, kk: (i, j))
    ins = [a, b] + ([add] if add is not None else [])
    return pl.pallas_call(
        body, name=name, grid=(m // tm, n // tn, nk),
        in_specs=[a_spec, b_spec] + ([o_spec] if add is not None else []),
        out_specs=o_spec, out_shape=jax.ShapeDtypeStruct((m, n), out_dtype),
        scratch_shapes=[pltpu.VMEM((tm, tn), F32)],
        compiler_params=_params(("parallel", "parallel", "arbitrary")),
    )(*ins)


def normmod_tile(x, g, shift, scale):
    return (rms(x) * g * (1.0 + scale) + shift,)


def resid_tile(x, y, gate):
    return (x + gate * y,)


def prep_tile(z_qk, z_rq, z_rk, z_gq, zg, cos, sin, qg, kg, gate_up, gate_b):
    out = []
    for h, t in enumerate(_heads(z_qk, ATT_Q_HEADS + ATT_KV_HEADS)):
        out.append(rope(rms(t) * (qg if h < ATT_Q_HEADS else kg), cos, sin))
    gq = z_gq * (GLA_DK ** -0.5)
    rq = [rope(t, cos, sin) for t in _heads(z_rq, RET_HEADS)]
    rk = [rope(t * (HEAD_DIM ** -0.5), cos, sin) for t in _heads(z_rk, RET_HEADS)]
    la = [log_sigmoid(bdot(zg, gate_up[d * LANES:(d + 1) * LANES]) + gate_b[d:d + 1]) * (1.0 / GLA_TAU) for d in range(2)]
    return (jnp.concatenate(out + [gq] + rq + rk + la, axis=1),)


def post_tile(o_att, o_ret_f, o_ret_b, o_gla_f, o_gla_b, rg, gr, ret_g, gla_g):
    ret = jnp.concatenate([rms(t) * ret_g for t in _heads(o_ret_f + o_ret_b, RET_HEADS)], axis=1) * silu(rg)
    gla = jnp.concatenate([rms(t) * gla_g for t in _heads(o_gla_f + o_gla_b, GLA_HEADS)], axis=1) * silu(gr)
    return (jnp.concatenate([o_att, ret, gla], axis=1),)


def _convglu_tile(n_lat, a, v, cw, cb):
    t = a.shape[0]
    row = lax.broadcasted_iota(jnp.int32, (t, 1), 0)
    has_prev = ((row != 0) & (row != n_lat)).astype(F32)
    has_next = ((row != n_lat - 1) & (row != t - 1)).astype(F32)
    conv = roll(a, 1, 0) * has_prev * cw[0:1] + a * cw[1:2] + roll(a, -1, 0) * has_next * cw[2:3] + cb
    return silu(conv) * v


def convglu(name, u, cw, cb, n_lat):
    t, f2 = u.shape
    f, tc = f2 // 2, FFN_COL_TILE
    nb = f // tc

    def body(a_ref, v_ref, cw_ref, cb_ref, o_ref):
        o_ref[...] = _convglu_tile(n_lat, a_ref[...], v_ref[...], cw_ref[...], cb_ref[...]).astype(o_ref.dtype)

    return pl.pallas_call(
        body, name=name, grid=(nb,),
        in_specs=[pl.BlockSpec((t, tc), lambda j: (0, j)), pl.BlockSpec((t, tc), lambda j: (0, nb + j)),
                  pl.BlockSpec((3, tc), lambda j: (0, j)), pl.BlockSpec((1, tc), lambda j: (0, j))],
        out_specs=pl.BlockSpec((t, tc), lambda j: (0, j)),
        out_shape=jax.ShapeDtypeStruct((t, f), BF16),
        compiler_params=_params(("parallel",)),
    )(u, u, cw, cb)


def convglu_bwd(name, u, cw, cb, dg, n_lat):
    t, f2 = u.shape
    f, tc = f2 // 2, FFN_COL_TILE
    nb = f // tc

    def body(a_ref, v_ref, cw_ref, cb_ref, dg_ref, da_ref, dv_ref, dcw_ref, dcb_ref):
        _, vjp = jax.vjp(functools.partial(_convglu_tile, n_lat), a_ref[...], v_ref[...], cw_ref[...], cb_ref[...])
        da_ref[...], dv_ref[...], dcw_ref[...], dcb_ref[...] = vjp(dg_ref[...])

    col = pl.BlockSpec((t, tc), lambda j: (0, j))
    return pl.pallas_call(
        body, name=name, grid=(nb,),
        in_specs=[col, pl.BlockSpec((t, tc), lambda j: (0, nb + j)), pl.BlockSpec((3, tc), lambda j: (0, j)),
                  pl.BlockSpec((1, tc), lambda j: (0, j)), col],
        out_specs=[col, col, pl.BlockSpec((3, tc), lambda j: (0, j)), pl.BlockSpec((1, tc), lambda j: (0, j))],
        out_shape=[jax.ShapeDtypeStruct((t, f), F32), jax.ShapeDtypeStruct((t, f), F32),
                   jax.ShapeDtypeStruct((3, f), F32), jax.ShapeDtypeStruct((1, f), F32)],
        compiler_params=_params(("parallel",)),
    )(u, u, cw, cb, dg)


def final_loss(x, target, g, n_lat):
    tm = ROW_TILE
    d = x.shape[1]

    def body(x_ref, t_ref, g_ref, loss_ref, dx_ref, dg_ref):
        i = pl.program_id(0)
        tgt = t_ref[...]

        def f(xv, gv):
            e = rms(xv) * gv - tgt
            s = jnp.sum(jnp.sum(e * e, axis=1, keepdims=True), axis=0, keepdims=True)
            return s * (0.5 / d)

        val, vjp = jax.vjp(f, x_ref[...], g_ref[...])
        dx, dgv = vjp(jnp.ones((1, 1), F32))
        dx_ref[...] = dx

        @pl.when(i == 0)
        def _():
            dg_ref[...] = dgv
            loss_ref[...] = jnp.broadcast_to(val, loss_ref.shape)

        @pl.when(i != 0)
        def _():
            dg_ref[...] += dgv
            loss_ref[...] += jnp.broadcast_to(val, loss_ref.shape)

    return pl.pallas_call(
        body, name="final_loss", grid=(n_lat // tm,),
        in_specs=[pl.BlockSpec((tm, d), lambda i: (i, 0)), pl.BlockSpec((tm, d), lambda i: (i, 0)),
                  pl.BlockSpec((1, d), lambda i: (0, 0))],
        out_specs=[pl.BlockSpec((1, LANES), lambda i: (0, 0)), pl.BlockSpec((tm, d), lambda i: (i, 0)),
                   pl.BlockSpec((1, d), lambda i: (0, 0))],
        out_shape=[jax.ShapeDtypeStruct((1, LANES), F32), jax.ShapeDtypeStruct((n_lat, d), F32),
                   jax.ShapeDtypeStruct((1, d), F32)],
        compiler_params=_params(("arbitrary",)),
    )(x, target, g)


ATT_SCALE = HEAD_DIM ** -0.5
_AK_BLK = P_AK // HEAD_DIM
_AV_BLK = Z_AV // HEAD_DIM


def _att_specs(t, tq):
    gw = ATT_GROUP * HEAD_DIM
    q_spec = pl.BlockSpec((tq, gw), lambda kv, i: (i, kv))
    k_spec = pl.BlockSpec((t, HEAD_DIM), lambda kv, i: (0, _AK_BLK + kv))
    v_spec = pl.BlockSpec((t, HEAD_DIM), lambda kv, i: (0, _AV_BLK + kv))
    row_spec = pl.BlockSpec((ATT_GROUP, tq, 1), lambda kv, i: (kv, i, 0))
    return q_spec, k_spec, v_spec, row_spec


def _att_mask(i, t, tq, n_lat):
    col = lax.broadcasted_iota(jnp.int32, (1, t), 1)
    return jnp.where((i >= n_lat // tq) & (col < n_lat), -jnp.inf, 0.0).astype(F32)


def attn_fwd(p, z, n_lat):
    t = p.shape[0]
    tq = ROW_TILE

    def body(q_ref, k_ref, v_ref, o_ref, lse_ref):
        mask = _att_mask(pl.program_id(1), t, tq, n_lat)
        k, v = k_ref[...].astype(BF16), v_ref[...].astype(BF16)
        for g in range(ATT_GROUP):
            cols = slice(g * HEAD_DIM, (g + 1) * HEAD_DIM)
            s = _dg(q_ref[:, cols], k, _NT) * ATT_SCALE + mask
            m = jnp.max(s, axis=1, keepdims=True)
            pr = jnp.exp(s - m)
            l = jnp.sum(pr, axis=1, keepdims=True)
            o_ref[:, cols] = _dg(pr, v, _NN) / l
            lse_ref[g] = m + jnp.log(l)

    q_spec, k_spec, v_spec, row_spec = _att_specs(t, tq)
    return pl.pallas_call(
        body, name="attn_fwd", grid=(ATT_KV_HEADS, t // tq),
        in_specs=[q_spec, k_spec, v_spec], out_specs=[q_spec, row_spec],
        out_shape=[jax.ShapeDtypeStruct((t, ATT_Q_HEADS * HEAD_DIM), F32),
                   jax.ShapeDtypeStruct((ATT_Q_HEADS, t, 1), F32)],
        compiler_params=_params(("parallel", "parallel")),
    )(p, p, z)


def attn_bwd(p, z, o, lse, do, n_lat):
    t = p.shape[0]
    tq = ROW_TILE

    def body(q_ref, k_ref, v_ref, o_ref, do_ref, lse_ref, dq_ref, dk_ref, dv_ref):
        i = pl.program_id(1)

        @pl.when(i == 0)
        def _():
            dk_ref[...] = jnp.zeros_like(dk_ref)
            dv_ref[...] = jnp.zeros_like(dv_ref)

        mask = _att_mask(i, t, tq, n_lat)
        k, v = k_ref[...].astype(BF16), v_ref[...].astype(BF16)
        dk, dv = dk_ref[...], dv_ref[...]
        for g in range(ATT_GROUP):
            cols = slice(g * HEAD_DIM, (g + 1) * HEAD_DIM)
            q, do_g = q_ref[:, cols].astype(BF16), do_ref[:, cols]
            pr = jnp.exp(_dg(q, k, _NT) * ATT_SCALE + mask - lse_ref[g])
            delta = jnp.sum(o_ref[:, cols] * do_g, axis=1, keepdims=True)
            ds = pr * (_dg(do_g, v, _NT) - delta) * ATT_SCALE
            dq_ref[:, cols] = _dg(ds, k, _NN)
            dk = dk + _dg(ds, q, _TN)
            dv = dv + _dg(pr, do_g, _TN)
        dk_ref[...], dv_ref[...] = dk, dv

    q_spec, k_spec, v_spec, row_spec = _att_specs(t, tq)
    kv_out = pl.BlockSpec((t, HEAD_DIM), lambda kv, i: (0, kv))
    return pl.pallas_call(
        body, name="attn_bwd", grid=(ATT_KV_HEADS, t // tq),
        in_specs=[q_spec, k_spec, v_spec, q_spec, q_spec, row_spec],
        out_specs=[q_spec, kv_out, kv_out],
        out_shape=[jax.ShapeDtypeStruct((t, ATT_Q_HEADS * HEAD_DIM), F32),
                   jax.ShapeDtypeStruct((t, ATT_KV_HEADS * HEAD_DIM), F32),
                   jax.ShapeDtypeStruct((t, ATT_KV_HEADS * HEAD_DIM), F32)],
        compiler_params=_params(("parallel", "arbitrary")),
    )(p, p, z, o, do, lse)


_RQ_BLK = P_RQ // HEAD_DIM
_RK_BLK = P_RK // HEAD_DIM
_RV_BLK = Z_RV // HEAD_DIM


def _scan_chunk(direction, step, n_chunks, n_lat_chunks):
    return jnp.where(direction == 0, (step + n_lat_chunks) % n_chunks, n_chunks - 1 - step)


def _ret_geometry(direction):
    c = RET_CHUNK
    i = lax.broadcasted_iota(jnp.int32, (c, c), 0)
    j = lax.broadcasted_iota(jnp.int32, (c, c), 1)
    rel = jnp.where(direction == 0, i - j, j - i).astype(F32)
    r = lax.broadcasted_iota(jnp.int32, (c, 1), 0)
    pos = jnp.where(direction == 0, r, c - 1 - r).astype(F32)
    return rel, pos


def ret_chunk(q, k, v, s, lg, rel, pos):
    c = RET_CHUNK
    causal = rel >= 0
    d_in = jnp.where(causal, jnp.exp(lg * jnp.where(causal, rel, 0.0)), 0.0)
    q_dec = jnp.exp(lg * (pos + 1.0))
    k_dec = jnp.exp(lg * (c - 1.0 - pos))
    c_dec = jnp.exp(lg * c)
    att = bdot_nt(q, k) * d_in
    o = bdot(att, v) + bdot(q * q_dec, s)
    s_new = c_dec * s + bdot_tn(k * k_dec, v)
    return o, s_new


def ret_fwd(p, z, lg, n_lat):
    t = p.shape[0]
    c = RET_CHUNK
    nc, nlc = t // c, n_lat // c

    def body(q_ref, k_ref, v_ref, lg_ref, o_ref, ssave_ref, s_s):
        d, n = pl.program_id(0), pl.program_id(2)

        @pl.when(n == 0)
        def _():
            s_s[...] = jnp.zeros_like(s_s)

        rel, pos = _ret_geometry(d)
        ssave_ref[0, 0, 0] = s_s[...]
        o, s_new = ret_chunk(q_ref[...], k_ref[...], v_ref[...], s_s[...], lg_ref[0, 0], rel, pos)
        o_ref[...] = o
        s_s[...] = s_new

    def blk(base):
        return pl.BlockSpec((c, HEAD_DIM), lambda d, h, n: (_scan_chunk(d, n, nc, nlc), base + h))

    return pl.pallas_call(
        body, name="ret_fwd", grid=(2, RET_HEADS, nc),
        in_specs=[blk(_RQ_BLK), blk(_RK_BLK), blk(_RV_BLK), pl.BlockSpec((1, 1, 1, 1), lambda d, h, n: (d, h, 0, 0))],
        out_specs=[pl.BlockSpec((c, HEAD_DIM), lambda d, h, n: (_scan_chunk(d, n, nc, nlc), d * RET_HEADS + h)),
                   pl.BlockSpec((1, 1, 1, HEAD_DIM, HEAD_DIM), lambda d, h, n: (d, h, n, 0, 0))],
        out_shape=[jax.ShapeDtypeStruct((t, 2 * RET_HEADS * HEAD_DIM), F32),
                   jax.ShapeDtypeStruct((2, RET_HEADS, nc, HEAD_DIM, HEAD_DIM), F32)],
        scratch_shapes=[pltpu.VMEM((HEAD_DIM, HEAD_DIM), F32)],
        compiler_params=_params(("parallel", "parallel", "arbitrary")),
    )(p, p, z, lg)


def ret_bwd(p, z, lg, states, do, n_lat):
    t = p.shape[0]
    c = RET_CHUNK
    nc, nlc = t // c, n_lat // c

    def body(q_ref, k_ref, v_ref, lg_ref, s_ref, do_ref, dq_ref, dk_ref, dv_ref, dlg_ref, ds_s):
        d, n = pl.program_id(0), pl.program_id(2)

        @pl.when(n == 0)
        def _():
            ds_s[...] = jnp.zeros_like(ds_s)
            dlg_ref[...] = jnp.zeros_like(dlg_ref)

        rel, pos = _ret_geometry(d)
        f = functools.partial(ret_chunk, rel=rel, pos=pos)
        _, vjp = jax.vjp(f, q_ref[...], k_ref[...], v_ref[...], s_ref[0, 0, 0], lg_ref[0, 0])
        dq, dk, dv, ds, dlg = vjp((do_ref[...], ds_s[...]))
        dq_ref[...], dk_ref[...], dv_ref[...] = dq, dk, dv
        ds_s[...] = ds
        dlg_ref[0, 0] += dlg

    def chunk_of(d, n):
        return _scan_chunk(d, nc - 1 - n, nc, nlc)

    def blk(base):
        return pl.BlockSpec((c, HEAD_DIM), lambda d, h, n: (chunk_of(d, n), base + h))

    out_blk = pl.BlockSpec((c, HEAD_DIM), lambda d, h, n: (chunk_of(d, n), d * RET_HEADS + h))
    grad_shape = jax.ShapeDtypeStruct((t, 2 * RET_HEADS * HEAD_DIM), F32)
    return pl.pallas_call(
        body, name="ret_bwd", grid=(2, RET_HEADS, nc),
        in_specs=[blk(_RQ_BLK), blk(_RK_BLK), blk(_RV_BLK), pl.BlockSpec((1, 1, 1, 1), lambda d, h, n: (d, h, 0, 0)),
                  pl.BlockSpec((1, 1, 1, HEAD_DIM, HEAD_DIM), lambda d, h, n: (d, h, nc - 1 - n, 0, 0)),
                  pl.BlockSpec((c, HEAD_DIM), lambda d, h, n: (chunk_of(d, n), h))],
        out_specs=[out_blk, out_blk, out_blk, pl.BlockSpec((1, 1, 1, 1), lambda d, h, n: (d, h, 0, 0))],
        out_shape=[grad_shape, grad_shape, grad_shape, jax.ShapeDtypeStruct((2, RET_HEADS, 1, 1), F32)],
        scratch_shapes=[pltpu.VMEM((HEAD_DIM, HEAD_DIM), F32)],
        compiler_params=_params(("parallel", "parallel", "arbitrary")),
    )(p, p, z, lg, states, do)


_GQ_BLK = P_GQ // (GLA_HEADS * GLA_DK)
_GK_BLK = Z_GK // (GLA_HEADS * GLA_DK)
_GV_BLK = Z_GV // (GLA_HEADS * GLA_DV)
_LA_BLK = P_LA // (GLA_HEADS * GLA_DK)


def _gla_mask(direction):
    c = GLA_CHUNK
    i = lax.broadcasted_iota(jnp.int32, (c, c), 0)
    j = lax.broadcasted_iota(jnp.int32, (c, c), 1)
    return (jnp.where(direction == 0, i - j, j - i) >= 0).astype(F32)


def gla_chunk(q, k, v, la, st, mask):
    b = mask_cumsum(mask, la)
    btot = jnp.sum(la, axis=0, keepdims=True)
    half = 0.5 * btot
    qt, kt = q * jnp.exp(b - half), k * jnp.exp(half - b)
    qs, ke = q * jnp.exp(b), k * jnp.exp(btot - b)
    outs, upd = [], []
    for h in range(GLA_HEADS):
        ks = slice(h * GLA_DK, (h + 1) * GLA_DK)
        vh = v[:, h * GLA_DV:(h + 1) * GLA_DV]
        att = bdot_nt(qt[:, ks], kt[:, ks]) * mask
        outs.append(bdot(att, vh) + bdot_nt(qs[:, ks], st[:, ks]))
        upd.append(bdot_tn(vh, ke[:, ks]))
    st_new = st * jnp.exp(btot) + jnp.concatenate(upd, axis=1)
    return jnp.concatenate(outs, axis=1), st_new


def gla_fwd(p, z, n_lat):
    t = p.shape[0]
    c = GLA_CHUNK
    nc, nlc = t // c, n_lat // c
    kw, vw = GLA_HEADS * GLA_DK, GLA_HEADS * GLA_DV

    def body(q_ref, k_ref, v_ref, la_ref, o_ref, ssave_ref, s_s):
        d, n = pl.program_id(0), pl.program_id(1)

        @pl.when(n == 0)
        def _():
            s_s[...] = jnp.zeros_like(s_s)

        ssave_ref[0, 0] = s_s[...]
        o, s_new = gla_chunk(q_ref[...], k_ref[...], v_ref[...], la_ref[...], s_s[...], _gla_mask(d))
        o_ref[...] = o
        s_s[...] = s_new

    def chunk_of(d, n):
        return _scan_chunk(d, n, nc, nlc)

    return pl.pallas_call(
        body, name="gla_fwd", grid=(2, nc),
        in_specs=[pl.BlockSpec((c, kw), lambda d, n: (chunk_of(d, n), _GQ_BLK)),
                  pl.BlockSpec((c, kw), lambda d, n: (chunk_of(d, n), _GK_BLK)),
                  pl.BlockSpec((c, vw), lambda d, n: (chunk_of(d, n), _GV_BLK)),
                  pl.BlockSpec((c, kw), lambda d, n: (chunk_of(d, n), _LA_BLK + d))],
        out_specs=[pl.BlockSpec((c, vw), lambda d, n: (chunk_of(d, n), d)),
                   pl.BlockSpec((1, 1, GLA_DV, kw), lambda d, n: (d, n, 0, 0))],
        out_shape=[jax.ShapeDtypeStruct((t, 2 * vw), F32), jax.ShapeDtypeStruct((2, nc, GLA_DV, kw), F32)],
        scratch_shapes=[pltpu.VMEM((GLA_DV, kw), F32)],
        compiler_params=_params(("parallel", "arbitrary")),
    )(p, z, z, p)


def gla_bwd(p, z, states, do, n_lat):
    t = p.shape[0]
    c = GLA_CHUNK
    nc, nlc = t // c, n_lat // c
    kw, vw = GLA_HEADS * GLA_DK, GLA_HEADS * GLA_DV

    def body(q_ref, k_ref, v_ref, la_ref, s_ref, do_ref, dq_ref, dk_ref, dv_ref, dla_ref, ds_s):
        d, n = pl.program_id(0), pl.program_id(1)

        @pl.when(n == 0)
        def _():
            ds_s[...] = jnp.zeros_like(ds_s)

        f = functools.partial(gla_chunk, mask=_gla_mask(d))
        _, vjp = jax.vjp(f, q_ref[...], k_ref[...], v_ref[...], la_ref[...], s_ref[0, 0])
        dq_ref[...], dk_ref[...], dv_ref[...], dla_ref[...], ds_s[...] = vjp((do_ref[...], ds_s[...]))

    def chunk_of(d, n):
        return _scan_chunk(d, nc - 1 - n, nc, nlc)

    k_out = pl.BlockSpec((c, kw), lambda d, n: (chunk_of(d, n), d))
    return pl.pallas_call(
        body, name="gla_bwd", grid=(2, nc),
        in_specs=[pl.BlockSpec((c, kw), lambda d, n: (chunk_of(d, n), _GQ_BLK)),
                  pl.BlockSpec((c, kw), lambda d, n: (chunk_of(d, n), _GK_BLK)),
                  pl.BlockSpec((c, vw), lambda d, n: (chunk_of(d, n), _GV_BLK)),
                  pl.BlockSpec((c, kw), lambda d, n: (chunk_of(d, n), _LA_BLK + d)),
                  pl.BlockSpec((1, 1, GLA_DV, kw), lambda d, n: (d, nc - 1 - n, 0, 0)),
                  pl.BlockSpec((c, vw), lambda d, n: (chunk_of(d, n), 0))],
        out_specs=[k_out, k_out, pl.BlockSpec((c, vw), lambda d, n: (chunk_of(d, n), d)), k_out],
        out_shape=[jax.ShapeDtypeStruct((t, 2 * kw), F32), jax.ShapeDtypeStruct((t, 2 * kw), F32),
                   jax.ShapeDtypeStruct((t, 2 * vw), F32), jax.ShapeDtypeStruct((t, 2 * kw), F32)],
        scratch_shapes=[pltpu.VMEM((GLA_DV, kw), F32)],
        compiler_params=_params(("parallel", "arbitrary")),
    )(p, z, z, p, states, do)


def _adam_tile(w, g, m, v):
    m = ADAM_B1 * m + (1.0 - ADAM_B1) * g
    v = ADAM_B2 * v + (1.0 - ADAM_B2) * (g * g)
    m_hat = m / (1.0 - ADAM_B1 ** ADAM_STEP)
    v_hat = v / (1.0 - ADAM_B2 ** ADAM_STEP)
    delta = -ADAM_LR * (m_hat / (jnp.sqrt(v_hat) + ADAM_EPS) + ADAM_WD * w)
    return delta, m, v


def adamw(name, w, g, m, v):
    shape = w.shape
    cols = shape[-1] if w.ndim > 1 and shape[-1] >= LANES else int(np.prod(shape))
    rows = int(np.prod(shape)) // cols
    tr = rows
    for cand in (512, 256, 128, 64, 32, 16, 8):
        if rows % cand == 0 and cand * cols * 4 <= (1 << 20):
            tr = cand
            break
    flat = [a.reshape(rows, cols) for a in (w, g, m, v)]

    def body(w_ref, g_ref, m_ref, v_ref, d_ref, mo_ref, vo_ref):
        d_ref[...], mo_ref[...], vo_ref[...] = _adam_tile(w_ref[...], g_ref[...], m_ref[...], v_ref[...])

    spec = pl.BlockSpec((tr, cols), lambda i: (i, 0))
    outs = pl.pallas_call(
        body, name=name, grid=(rows // tr,),
        in_specs=[spec] * 4, out_specs=[spec] * 3,
        out_shape=[jax.ShapeDtypeStruct((rows, cols), F32)] * 3,
        compiler_params=_params(("parallel",)),
    )(*flat)
    return tuple(o.reshape(shape) for o in outs)


MESH = pl.DeviceIdType.MESH
_HBM = pl.BlockSpec(memory_space=pltpu.HBM)
N_CHIPS = 4
N_DEV = 8


def _place():
    x, y, c = lax.axis_index("x"), lax.axis_index("y"), lax.axis_index("c")
    chips = [(1 - x, y), (x, 1 - y), (1 - x, 1 - y)]
    return x, y, c, chips


def _remote(src, dst, send_sem, recv_sem, to):
    return pltpu.make_async_remote_copy(src_ref=src, dst_ref=dst, send_sem=send_sem, recv_sem=recv_sem,
                                        device_id=to, device_id_type=MESH)


def all_gather_small(name, v):
    m_per, n = v.shape

    def body(x_ref, out_ref, send_sems, recv_sems, local_sem):
        x, y, c, chips = _place()
        me, sibling = (x, y, c), (x, y, 1 - c)

        def rows(px, py, pc):
            return out_ref.at[pl.ds((4 * px + 2 * py + pc) * m_per, m_per), :]

        def copy(k, block, to, src=None):
            return _remote(rows(*block) if src is None else src, rows(*block), send_sems.at[k], recv_sems.at[k], to)

        mine = pltpu.make_async_copy(x_ref, rows(*me), local_sem)
        mine.start()
        first = [copy(0, me, sibling, src=x_ref)]
        first += [copy(1 + j, me, (*chip, c), src=x_ref) for j, chip in enumerate(chips)]
        for cp in first:
            cp.start()
        passed = [copy(4 + j, (*chip, c), sibling) for j, chip in enumerate(chips)]
        for j, chip in enumerate(chips):
            copy(1 + j, (*chip, c), me).wait_recv()
            passed[j].start()
        copy(0, sibling, me).wait_recv()
        for j, chip in enumerate(chips):
            copy(4 + j, (*chip, 1 - c), me).wait_recv()
        for cp in first + passed:
            cp.wait_send()
        mine.wait()

    return pl.pallas_call(
        body, name=name,
        out_shape=jax.ShapeDtypeStruct((N_DEV * m_per, n), v.dtype),
        in_specs=[pl.BlockSpec(memory_space=pltpu.VMEM)],
        out_specs=pl.BlockSpec(memory_space=pltpu.VMEM),
        scratch_shapes=[pltpu.SemaphoreType.DMA((7,)), pltpu.SemaphoreType.DMA((7,)), pltpu.SemaphoreType.DMA],
        compiler_params=pltpu.CompilerParams(vmem_limit_bytes=VMEM_LIMIT),
    )(v)


def all_gather_shards(name, shards):
    nt = len(shards)

    def body(*refs):
        x_refs, out_refs = refs[:nt], refs[nt:2 * nt]
        send_sems, recv_sems = refs[2 * nt:]
        x, y, c, chips = _place()
        q = 2 * x + y
        sibling = (x, y, 1 - c)
        sends = []
        for t in range(nt):
            half = x_refs[t].shape[0] // 2

            def part(qq, hh, t=t, half=half):
                return out_refs[t].at[qq, pl.ds(hh * half, half), :]

            for j, chip in enumerate(chips):
                cp = _remote(x_refs[t].at[pl.ds(c * half, half), :], part(q, c), send_sems.at[6 * t + j],
                             recv_sems.at[6 * t + j], (*chip, c))
                cp.start()
                sends.append(cp)
        for t in range(nt):
            half = x_refs[t].shape[0] // 2

            def part(qq, hh, t=t, half=half):
                return out_refs[t].at[qq, pl.ds(hh * half, half), :]

            for j, chip in enumerate(chips):
                qj = 2 * chip[0] + chip[1]
                _remote(part(qj, c), part(qj, c), send_sems.at[6 * t + j], recv_sems.at[6 * t + j], (*chip, c)).wait_recv()
                cp = _remote(part(qj, c), part(qj, c), send_sems.at[6 * t + 3 + j], recv_sems.at[6 * t + 3 + j], sibling)
                cp.start()
                sends.append(cp)
        for t in range(nt):
            half = x_refs[t].shape[0] // 2
            for j, chip in enumerate(chips):
                qj = 2 * chip[0] + chip[1]
                dst = out_refs[t].at[qj, pl.ds((1 - c) * half, half), :]
                _remote(dst, dst, send_sems.at[6 * t + 3 + j], recv_sems.at[6 * t + 3 + j], sibling).wait_recv()
        for cp in sends:
            cp.wait_send()

    return pl.pallas_call(
        body, name=name,
        out_shape=[jax.ShapeDtypeStruct((N_CHIPS,) + s.shape, s.dtype) for s in shards],
        in_specs=[_HBM] * nt, out_specs=[_HBM] * nt,
        scratch_shapes=[pltpu.SemaphoreType.DMA((6 * nt,)), pltpu.SemaphoreType.DMA((6 * nt,))],
    )(*shards)


def exchange_sibling_halves(name, grads):
    nt = len(grads)

    def body(*refs):
        g_refs, out_refs = refs[:nt], refs[nt:2 * nt]
        send_sems, recv_sems = refs[2 * nt:]
        x, y, c, _ = _place()
        sibling = (x, y, 1 - c)
        cps = []
        for t in range(nt):
            half = g_refs[t].shape[1] // 2
            cp = _remote(g_refs[t].at[:, pl.ds((1 - c) * half, half), :], out_refs[t], send_sems.at[t], recv_sems.at[t], sibling)
            cp.start()
            cps.append(cp)
        for cp in cps:
            cp.wait()

    return pl.pallas_call(
        body, name=name,
        out_shape=[jax.ShapeDtypeStruct((g.shape[0], g.shape[1] // 2, g.shape[2]), g.dtype) for g in grads],
        in_specs=[_HBM] * nt, out_specs=[_HBM] * nt,
        scratch_shapes=[pltpu.SemaphoreType.DMA((nt,)), pltpu.SemaphoreType.DMA((nt,))],
    )(*grads)


def scatter_to_chips(name, sums):
    nt = len(sums)

    def body(*refs):
        s_refs, out_refs = refs[:nt], refs[nt:2 * nt]
        send_sems, recv_sems = refs[2 * nt:]
        x, y, c, chips = _place()
        cps = []
        for t in range(nt):
            for j, chip in enumerate(chips):
                qj = 2 * chip[0] + chip[1]
                cp = _remote(s_refs[t].at[qj], out_refs[t].at[j], send_sems.at[3 * t + j], recv_sems.at[3 * t + j], (*chip, c))
                cp.start()
                cps.append(cp)
        for cp in cps:
            cp.wait()

    return pl.pallas_call(
        body, name=name,
        out_shape=[jax.ShapeDtypeStruct((3,) + s.shape[1:], s.dtype) for s in sums],
        in_specs=[_HBM] * nt, out_specs=[_HBM] * nt,
        scratch_shapes=[pltpu.SemaphoreType.DMA((3 * nt,)), pltpu.SemaphoreType.DMA((3 * nt,))],
    )(*sums)


def join_sibling_halves(name, halves):
    nt = len(halves)

    def body(*refs):
        out_refs = refs[nt:2 * nt]
        send_sems, recv_sems = refs[2 * nt:]
        x, y, c, _ = _place()
        sibling = (x, y, 1 - c)
        cps = []
        for t in range(nt):
            half = out_refs[t].shape[0] // 2
            mine = out_refs[t].at[pl.ds(c * half, half), :]
            cp = _remote(mine, mine, send_sems.at[t], recv_sems.at[t], sibling)
            cp.start()
            cps.append(cp)
        for t, cp in enumerate(cps):
            half = out_refs[t].shape[0] // 2
            other = out_refs[t].at[pl.ds((1 - c) * half, half), :]
            _remote(other, other, send_sems.at[t], recv_sems.at[t], sibling).wait_recv()
            cp.wait_send()

    return pl.pallas_call(
        body, name=name,
        out_shape=[jax.ShapeDtypeStruct(h.shape, h.dtype) for h in halves],
        in_specs=[_HBM] * nt, out_specs=[_HBM] * nt,
        input_output_aliases={t: t for t in range(nt)},
        scratch_shapes=[pltpu.SemaphoreType.DMA((nt,)), pltpu.SemaphoreType.DMA((nt,))],
    )(*halves)


def _rows_tile(rows, cols):
    for cand in (512, 256, 128, 64, 32, 16):
        if rows % cand == 0 and cand * cols * 4 <= (1 << 20):
            return cand
    return rows


def add_sibling_half(name, pieces, from_sibling, core):
    n, h, cols = from_sibling.shape
    tr = _rows_tile(h, cols)
    nb = h // tr

    def body(c_ref, a_ref, b_ref, o_ref):
        o_ref[...] = (a_ref[...].astype(F32) + b_ref[...].astype(F32)).astype(o_ref.dtype)

    blk = pl.BlockSpec((1, tr, cols), lambda q, i, c_ref: (q, i, 0))
    return pl.pallas_call(
        body, name=name,
        grid_spec=pltpu.PrefetchScalarGridSpec(
            num_scalar_prefetch=1, grid=(n, nb),
            in_specs=[pl.BlockSpec((1, tr, cols), lambda q, i, c_ref: (q, c_ref[0] * nb + i, 0)), blk], out_specs=blk),
        out_shape=jax.ShapeDtypeStruct((n, h, cols), BF16),
        compiler_params=_params(("parallel", "parallel")),
    )(core.reshape(1).astype(jnp.int32), pieces, from_sibling)


def add_chip_sums(name, chip_sums, from_chips, chip, core):
    _, h, cols = chip_sums.shape
    tr = _rows_tile(h, cols)
    nb = h // tr

    def body(s_ref, own_ref, r0_ref, r1_ref, r2_ref, o_ref):
        acc = own_ref[0].astype(F32) + r0_ref[0].astype(F32)
        o_ref[...] = acc + r1_ref[0].astype(F32) + r2_ref[0].astype(F32)

    def got(j):
        return pl.BlockSpec((1, tr, cols), lambda i, s_ref: (j, i, 0))

    return pl.pallas_call(
        body, name=name,
        grid_spec=pltpu.PrefetchScalarGridSpec(
            num_scalar_prefetch=1, grid=(nb,),
            in_specs=[pl.BlockSpec((1, tr, cols), lambda i, s_ref: (s_ref[0], i, 0)), got(0), got(1), got(2)],
            out_specs=pl.BlockSpec((tr, cols), lambda i, s_ref: (s_ref[1] * nb + i, 0))),
        out_shape=jax.ShapeDtypeStruct((2 * h, cols), F32),
        compiler_params=_params(("parallel",)),
    )(jnp.stack([chip, core]).astype(jnp.int32), chip_sums, from_chips, from_chips, from_chips)


def sum_device_blocks(name, g):
    n = g.shape[1]

    def body(g_ref, o_ref):
        acc = g_ref[0:8, :]
        for d in range(1, N_DEV):
            acc = acc + g_ref[8 * d:8 * (d + 1), :]
        o_ref[...] = acc

    return pl.pallas_call(body, name=name, out_shape=jax.ShapeDtypeStruct((8, n), F32),
                          compiler_params=pltpu.CompilerParams(vmem_limit_bytes=VMEM_LIMIT))(g)


class LayerWeights(NamedTuple):
    w_main: jax.Array
    w_gate: jax.Array
    g_out: jax.Array
    g_up: jax.Array
    g_down: jax.Array
    norm1_g: jax.Array
    q_g: jax.Array
    k_g: jax.Array
    lg: jax.Array
    ret_g: jax.Array
    gate_up: jax.Array
    gate_b: jax.Array
    gla_g: jax.Array
    norm2_g: jax.Array
    conv_w: jax.Array
    conv_b: jax.Array


def _mod(mods, k):
    return mods[:, k:k + 1, :]


def out_view(l, tb):
    rows = D_MODEL // N_CHIPS
    if tb:
        return BView(n=D_MODEL, k=D_MODEL, tn=rows, tk=512, index_map=lambda i, j, kk: (j, l, kk))
    return BView(n=D_MODEL, k=D_MODEL, tn=1024, tk=rows, index_map=lambda i, j, kk: (kk, l, j))


def down_view(l, f, tb):
    rows = f // N_CHIPS
    if tb:
        return BView(n=f, k=D_MODEL, tn=rows, tk=512, index_map=lambda i, j, kk: (j, l, kk))
    return BView(n=D_MODEL, k=f, tn=1024, tk=rows, index_map=lambda i, j, kk: (kk, l, j))


def up_view(l, f, part=None):
    cols = 2 * f // N_CHIPS
    tc = _pick(cols, (1408, 1024, 512, 256))
    nbc = cols // tc
    if part is None:
        nkb = D_MODEL // 512
        return BView(n=2 * f, k=D_MODEL, tn=tc, tk=512, index_map=lambda i, j, kk: (j // nbc, l * nkb + kk, j % nbc))
    nnb = D_MODEL // 1024
    return BView(n=D_MODEL, k=f, tn=1024, tk=tc, index_map=lambda i, j, kk: (2 * part + kk // nbc, l * nnb + j, kk % nbc))


def ada_view(l, n_ada, tb):
    if tb:
        return BView(n=D_MODEL, k=n_ada, tn=1024, tk=512, index_map=lambda i, j, kk: (l, j, kk))
    return BView(n=n_ada, k=D_MODEL, tn=1024, tk=512, index_map=lambda i, j, kk: (l, kk, j))


def _prep_args(z, zg, cos, sin, w):
    rows = [Row(z, Z_AV, 0), Row(z, 512, Z_RQ // 512), Row(z, 512, Z_RK // 512), Row(z, 256, Z_GQ // 256),
            Row(zg, LANES, 0), Row(cos, HEAD_DIM, 0, False), Row(sin, HEAD_DIM, 0, False)]
    return rows, [Par(w.q_g), Par(w.k_g), Par(w.gate_up), Par(w.gate_b)]


def _post_args(o_att, o_ret, o_gla, z, w):
    rows = [Row(o_att, 1024), Row(o_ret, 512, 0), Row(o_ret, 512, 1, False), Row(o_gla, 512, 0), Row(o_gla, 512, 1, False),
            Row(z, 512, Z_RG // 512), Row(z, 512, Z_GR // 512)]
    return rows, [Par(w.ret_g), Par(w.gla_g)]


def layer_fwd(l, xs, mods, w, cos, sin, n_lat):
    t, d = xs.shape
    tag = f"l{l}_"
    nm1 = [Par(w.norm1_g), Par(_mod(mods, 0), True), Par(_mod(mods, 1), True)]
    (h,) = row_map(tag + "norm1", normmod_tile, [Row(xs, d)], nm1, [(d, BF16)], t, n_lat)
    z = matmul(tag + "in_proj", h, w.w_main)
    zg = matmul(tag + "gate_proj", h, w.w_gate)
    rows, pars = _prep_args(z, zg, cos, sin, w)
    (p,) = row_map(tag + "prep", prep_tile, rows, pars, [(P_W, F32)], t, n_lat)
    o_att, lse = attn_fwd(p, z, n_lat)
    o_ret, s_ret = ret_fwd(p, z, w.lg, n_lat)
    o_gla, s_gla = gla_fwd(p, z, n_lat)
    rows, pars = _post_args(o_att, o_ret, o_gla, z, w)
    (m,) = row_map(tag + "post", post_tile, rows, pars, [(d, BF16)], t, n_lat)
    y = matmul(tag + "out_proj", m, w.g_out, view=out_view(l, False))
    (x1,) = row_map(tag + "resid1", resid_tile, [Row(xs, d), Row(y, d)], [Par(_mod(mods, 2), True)], [(d, F32)], t, n_lat)
    nm2 = [Par(w.norm2_g), Par(_mod(mods, 3), True), Par(_mod(mods, 4), True)]
    (h2,) = row_map(tag + "norm2", normmod_tile, [Row(x1, d)], nm2, [(d, BF16)], t, n_lat)
    f = w.conv_b.shape[1]
    u = matmul(tag + "up_proj", h2, w.g_up, view=up_view(l, f))
    g = convglu(tag + "convglu", u, w.conv_w, w.conv_b, n_lat)
    yd = matmul(tag + "down_proj", g, w.g_down, view=down_view(l, f, False))
    (x2,) = row_map(tag + "resid2", resid_tile, [Row(x1, d), Row(yd, d)], [Par(_mod(mods, 5), True)], [(d, F32)], t, n_lat)
    saved = dict(xs=xs, h=h, z=z, zg=zg, p=p, o_att=o_att, lse=lse, o_ret=o_ret, s_ret=s_ret, o_gla=o_gla, s_gla=s_gla,
                 m=m, y=y, x1=x1, h2=h2, u=u, g=g, yd=yd)
    return x2, saved


def _sum_dirs(a):
    w = a.shape[1] // 2
    return a[:, :w] + a[:, w:]


def layer_bwd(l, dx2, s, mods, w, cos, sin, n_lat):
    t, d = dx2.shape
    tag = f"l{l}_b_"
    dyd, dgate5 = row_vjp(tag + "resid2", resid_tile, [Row(s["x1"], d, 0, False), Row(s["yd"], d)],
                          [Par(_mod(mods, 5), True)], [dx2], t, n_lat)
    f = w.conv_b.shape[1]
    dg = matmul(tag + "down_dx", dyd, w.g_down, tb=True, view=down_view(l, f, True))
    dw_down = matmul(tag + "down_dw", s["g"], dyd, ta=True, out_dtype=BF16)
    da, dv, dcw, dcb = convglu_bwd(tag + "convglu", s["u"], w.conv_w, w.conv_b, dg, n_lat)
    dh2 = matmul(tag + "up_dx_gate", da, w.g_up, tb=True, view=up_view(l, f, 0))
    dh2 = matmul(tag + "up_dx_value", dv, w.g_up, tb=True, view=up_view(l, f, 1), add=dh2)
    dw_up = (matmul(tag + "up_dw_gate", s["h2"], da, ta=True, out_dtype=BF16),
             matmul(tag + "up_dw_value", s["h2"], dv, ta=True, out_dtype=BF16))
    nm2 = [Par(w.norm2_g), Par(_mod(mods, 3), True), Par(_mod(mods, 4), True)]
    dx1, dg2, dshift3, dscale4 = row_vjp(tag + "norm2", normmod_tile, [Row(s["x1"], d)], nm2, [dh2], t, n_lat, add_to_first=dx2)
    dy, dgate2 = row_vjp(tag + "resid1", resid_tile, [Row(s["xs"], d, 0, False), Row(s["y"], d)],
                         [Par(_mod(mods, 2), True)], [dx1], t, n_lat)
    dm = matmul(tag + "out_dx", dy, w.g_out, tb=True, view=out_view(l, True))
    dw_out = matmul(tag + "out_dw", s["m"], dy, ta=True, out_dtype=BF16)
    rows, pars = _post_args(s["o_att"], s["o_ret"], s["o_gla"], s["z"], w)
    do_att, do_ret, do_gla, d_rg, d_gr, d_ret_g, d_gla_g = row_vjp(tag + "post", post_tile, rows, pars, [dm], t, n_lat)
    dq_a, dk_a, dv_a = attn_bwd(s["p"], s["z"], s["o_att"], s["lse"], do_att, n_lat)
    dq_r, dk_r, dv_r, dlg = ret_bwd(s["p"], s["z"], w.lg, s["s_ret"], do_ret, n_lat)
    dq_g, dk_g, dv_g, dla = gla_bwd(s["p"], s["z"], s["s_gla"], do_gla, n_lat)
    dp = jnp.concatenate([dq_a, dk_a, _sum_dirs(dq_g), _sum_dirs(dq_r), _sum_dirs(dk_r), dla], axis=1)
    rows, pars = _prep_args(s["z"], s["zg"], cos, sin, w)
    d_zqk, d_zrq, d_zrk, d_zgq, dzg, d_qg, d_kg, d_up, d_gb = row_vjp(tag + "prep", prep_tile, rows, pars, [dp], t, n_lat)
    dz = jnp.concatenate([d_zqk, dv_a, d_zrq, d_zrk, _sum_dirs(dv_r), d_rg, d_zgq, _sum_dirs(dk_g), _sum_dirs(dv_g), d_gr], axis=1)
    dh_gate = matmul(tag + "gate_dx", dzg, w.w_gate, tb=True)
    dh = matmul(tag + "in_dx", dz, w.w_main, tb=True, add=dh_gate)
    dw_main = matmul(tag + "in_dw", s["h"], dz, ta=True, out_dtype=BF16)
    dw_gate = matmul(tag + "gate_dw", s["h"], dzg, ta=True, out_dtype=BF16)
    nm1 = [Par(w.norm1_g), Par(_mod(mods, 0), True), Par(_mod(mods, 1), True)]
    dx, dg1, dshift0, dscale1 = row_vjp(tag + "norm1", normmod_tile, [Row(s["xs"], d)], nm1, [dh], t, n_lat, add_to_first=dx1)
    dmods = jnp.concatenate([dshift0, dscale1, dgate2, dshift3, dscale4, dgate5], axis=1)
    grads = dict(w_main=dw_main, w_gate=dw_gate, w_out=dw_out, w_up=dw_up, w_down=dw_down, norm1_g=dg1, q_g=d_qg, k_g=d_kg,
                 lg=dlg, ret_g=d_ret_g, gate_up=d_up, gate_b=d_gb, gla_g=d_gla_g, norm2_g=dg2, conv_w=dcw, conv_b=dcb)
    return dx, dmods, grads


def rope_tables(n_lat, n_ctx):
    rows = n_lat // GRID_W
    row = jnp.repeat(jnp.arange(rows, dtype=F32), GRID_W)
    col = jnp.tile(jnp.arange(GRID_W, dtype=F32), rows)
    n_freq = HEAD_DIM // 4
    inv_freq = ROPE_THETA ** (-jnp.arange(n_freq, dtype=F32) / n_freq)
    ang = jnp.concatenate([row[:, None] * inv_freq, col[:, None] * inv_freq], axis=-1)
    cos, sin = jnp.cos(ang), jnp.sin(ang)
    cos = jnp.concatenate([jnp.concatenate([cos, cos], axis=1), jnp.ones((n_ctx, HEAD_DIM), F32)], axis=0)
    sin = jnp.concatenate([jnp.concatenate([-sin, sin], axis=1), jnp.zeros((n_ctx, HEAD_DIM), F32)], axis=0)
    return cos, sin


def local_step(xs, target, mods, weights, final_g, n_lat):
    t, d = xs.shape
    cos, sin = rope_tables(n_lat, t - n_lat)
    saved = []
    h = xs
    for l, w in enumerate(weights):
        h, s = layer_fwd(l, h, mods[l], w, cos, sin, n_lat)
        saved.append(s)
    loss, dlat, dgf = final_loss(h, target, final_g, n_lat)
    dx = jnp.concatenate([dlat, jnp.zeros((t - n_lat, d), F32)], axis=0)
    dmods, grads = [None] * len(weights), [None] * len(weights)
    for l in reversed(range(len(weights))):
        dx, dmods[l], grads[l] = layer_bwd(l, dx, saved[l], mods[l], weights[l], cos, sin, n_lat)
    return loss, dx, dmods, grads, dgf


WEIGHT_NAMES = ("c_ctx", "ada_w", "ada_b", "norm1_g", "w_in", "q_norm_g", "k_norm_g", "ret_log_decay", "ret_norm_g",
                "gla_gate_up", "gla_gate_b", "gla_norm_g", "w_out", "norm2_g", "w_up", "conv_w", "conv_b", "w_down", "final_norm_g")
PACK_QUANTUM = 8 * LANES


def _pack(arrays):
    flat = jnp.concatenate([a.reshape(-1).astype(F32) for a in arrays])
    n = -(-flat.shape[0] // PACK_QUANTUM) * PACK_QUANTUM
    return jnp.pad(flat, (0, n - flat.shape[0])).reshape(8, n // 8)


def _unpack(flat2d, shapes):
    out, at = [], 0
    for s in shapes:
        size = int(np.prod(s))
        out.append(flat2d[:, at:at + size].reshape((flat2d.shape[0],) + tuple(s)))
        at += size
    return out


def _per_device(gathered):
    return gathered.reshape(N_DEV, -1)


def _from_chips(per_device, axis):
    chips = per_device[0::2]
    moved = jnp.moveaxis(chips, 0, axis)
    shape = moved.shape
    return moved.reshape(shape[:axis] + (shape[axis] * shape[axis + 1],) + shape[axis + 2:])


def kernel(x, c, ctx, c_ctx, ada_w, ada_b, norm1_g, w_in, q_norm_g, k_norm_g, ret_log_decay, ret_norm_g, gla_gate_up, gla_gate_b, gla_norm_g, w_out, norm2_g, w_up, conv_w, conv_b, w_down, final_norm_g, loss_target, m_c_ctx, m_ada_w, m_ada_b, m_norm1_g, m_w_in, m_q_norm_g, m_k_norm_g, m_ret_log_decay, m_ret_norm_g, m_gla_gate_up, m_gla_gate_b, m_gla_norm_g, m_w_out, m_norm2_g, m_w_up, m_conv_w, m_conv_b, m_w_down, m_final_norm_g, v_c_ctx, v_ada_w, v_ada_b, v_norm1_g, v_w_in, v_q_norm_g, v_k_norm_g, v_ret_log_decay, v_ret_norm_g, v_gla_gate_up, v_gla_gate_b, v_gla_norm_g, v_w_out, v_norm2_g, v_w_up, v_conv_w, v_conv_b, v_w_down, v_final_norm_g):
    weights = dict(zip(WEIGHT_NAMES, (c_ctx, ada_w, ada_b, norm1_g, w_in, q_norm_g, k_norm_g, ret_log_decay, ret_norm_g,
                                      gla_gate_up, gla_gate_b, gla_norm_g, w_out, norm2_g, w_up, conv_w, conv_b, w_down, final_norm_g)))
    mom_m = dict(zip(WEIGHT_NAMES, (m_c_ctx, m_ada_w, m_ada_b, m_norm1_g, m_w_in, m_q_norm_g, m_k_norm_g, m_ret_log_decay, m_ret_norm_g,
                                    m_gla_gate_up, m_gla_gate_b, m_gla_norm_g, m_w_out, m_norm2_g, m_w_up, m_conv_w, m_conv_b, m_w_down, m_final_norm_g)))
    mom_v = dict(zip(WEIGHT_NAMES, (v_c_ctx, v_ada_w, v_ada_b, v_norm1_g, v_w_in, v_q_norm_g, v_k_norm_g, v_ret_log_decay, v_ret_norm_g,
                                    v_gla_gate_up, v_gla_gate_b, v_gla_norm_g, v_w_out, v_norm2_g, v_w_up, v_conv_w, v_conv_b, v_w_down, v_final_norm_g)))
    depth, d = norm1_g.shape
    assert d == D_MODEL and x.shape[0] == 1
    n_lat, n_ctx, f = x.shape[1], ctx.shape[1], conv_b.shape[1]
    assert n_lat % ROW_TILE == 0 and n_ctx % ROW_TILE == 0 and f % FFN_COL_TILE == 0 and f % N_CHIPS == 0
    n_in = w_in.shape[2]
    n_ada = ada_w.shape[2]
    xi, yi, ci = lax.axis_index("x"), lax.axis_index("y"), lax.axis_index("c")
    chip = 2 * xi + yi
    dev = 2 * chip + ci

    shards = [w_in.astype(BF16).reshape(depth * d, n_in), w_out.astype(BF16).reshape(-1, d),
              w_up.astype(BF16).reshape(depth * d, -1), w_down.astype(BF16).reshape(-1, d)]
    gathered_w = all_gather_shards("gather_weights", shards)
    g_in, g_out, g_up, g_down = [lax.dynamic_update_slice_in_dim(g, s[None], chip, axis=0) for g, s in zip(gathered_w, shards)]
    w_in_full = [jnp.concatenate([g_in[q, l * d:(l + 1) * d] for q in range(N_CHIPS)], axis=1) for l in range(depth)]
    w_main = [w[:, :N_MAIN] for w in w_in_full]
    w_gate = [jnp.pad(w[:, N_MAIN:], ((0, 0), (0, LANES - N_GATE))) for w in w_in_full]

    small_shapes = [c.shape[1:], conv_w.shape, gla_gate_up.shape, gla_gate_b.shape]
    got = _per_device(all_gather_small("gather_small", _pack([c, conv_w, gla_gate_up, gla_gate_b])))
    c_all, conv_w_sh, gate_up_sh, gate_b_sh = _unpack(got, small_shapes)
    conv_w_full = _from_chips(conv_w_sh, 2)
    gate_up_full = _from_chips(gate_up_sh, 3)
    gate_b_full = _from_chips(gate_b_sh, 2)

    act = jnp.zeros((16, d), F32).at[0:N_DEV].set(jax.nn.silu(c_all)).at[N_DEV].set(jax.nn.silu(c_ctx))
    mod_sh = jnp.stack([matmul(f"ada_fwd{l}", act, ada_w, view=ada_view(l, n_ada, False)) for l in range(depth)])
    got = _per_device(all_gather_small("gather_mods", _pack([mod_sh])))
    (mod_sh_all,) = _unpack(got, [mod_sh.shape])
    mod_full = _from_chips(mod_sh_all, 2) + ada_b[:, None, :]
    mod_mine = lax.dynamic_index_in_dim(mod_full, dev, axis=1, keepdims=False)
    mods = [jnp.stack([mod_mine[l].reshape(N_MOD, d), mod_full[l, N_DEV].reshape(N_MOD, d)]) for l in range(depth)]

    layer_w = []
    for l in range(depth):
        up = jnp.zeros((2, LANES, GLA_HEADS * GLA_DK), F32)
        up = up.at[0, 0:GLA_RANK].set(gate_up_full[l, 0]).at[1, GLA_RANK:2 * GLA_RANK].set(gate_up_full[l, 1])
        layer_w.append(LayerWeights(
            w_main=w_main[l], w_gate=w_gate[l], g_out=g_out, g_up=g_up, g_down=g_down,
            norm1_g=norm1_g[l].reshape(1, 1, d), q_g=q_norm_g[l].reshape(1, 1, HEAD_DIM), k_g=k_norm_g[l].reshape(1, 1, HEAD_DIM),
            lg=ret_log_decay[l].reshape(2, RET_HEADS, 1, 1), ret_g=ret_norm_g[l].reshape(1, 1, HEAD_DIM),
            gate_up=up.reshape(1, 2 * LANES, -1), gate_b=gate_b_full[l].reshape(1, 2, -1), gla_g=gla_norm_g[l].reshape(1, 1, HEAD_DIM),
            norm2_g=norm2_g[l].reshape(1, 1, d), conv_w=conv_w_full[l], conv_b=conv_b[l].reshape(1, f)))
    xs = jnp.concatenate([x[0], ctx[0]], axis=0)
    loss, dx, dmods, grads, dgf = local_step(xs, loss_target[0], mods, layer_w, final_norm_g.reshape(1, d), n_lat)

    def gate_up_grad(g):
        return jnp.stack([g[0, 0:GLA_RANK], g[0, LANES + GLA_RANK:LANES + 2 * GLA_RANK]])

    per_layer = [[dmods[l][0], dmods[l][1], grads[l]["norm1_g"], grads[l]["norm2_g"], grads[l]["q_g"], grads[l]["k_g"],
                  grads[l]["ret_g"], grads[l]["gla_g"], grads[l]["lg"], gate_up_grad(grads[l]["gate_up"]), grads[l]["gate_b"],
                  grads[l]["conv_w"], grads[l]["conv_b"]] for l in range(depth)]
    layer_shapes = [(N_MOD * d,), (N_MOD * d,), (d,), (d,), (HEAD_DIM,), (HEAD_DIM,), (HEAD_DIM,), (HEAD_DIM,), (2, RET_HEADS),
                    (2, GLA_RANK, GLA_HEADS * GLA_DK), (2, GLA_HEADS * GLA_DK), (3, f), (f,)]
    packed = _pack([a for lay in per_layer for a in lay] + [dgf, loss[0, 0:1]])
    gathered = all_gather_small("gather_small_grads", packed)
    every = _unpack(_per_device(gathered), layer_shapes * depth + [(d,), (1,)])
    total = _unpack(sum_device_blocks("sum_small_grads", gathered).reshape(1, -1), layer_shapes * depth + [(d,), (1,)])
    nl = len(layer_shapes)

    def tot(l, k):
        return total[l * nl + k][0]

    out = {"norm1_g": jnp.stack([tot(l, 2) for l in range(depth)]), "norm2_g": jnp.stack([tot(l, 3) for l in range(depth)]),
           "q_norm_g": jnp.stack([tot(l, 4) for l in range(depth)]), "k_norm_g": jnp.stack([tot(l, 5) for l in range(depth)]),
           "ret_norm_g": jnp.stack([tot(l, 6) for l in range(depth)]), "gla_norm_g": jnp.stack([tot(l, 7) for l in range(depth)]),
           "ret_log_decay": jnp.stack([tot(l, 8) for l in range(depth)]),
           "gla_gate_up": lax.dynamic_slice_in_dim(jnp.stack([tot(l, 9) for l in range(depth)]), chip * gla_gate_up.shape[3], gla_gate_up.shape[3], axis=3),
           "gla_gate_b": lax.dynamic_slice_in_dim(jnp.stack([tot(l, 10) for l in range(depth)]), chip * gla_gate_b.shape[2], gla_gate_b.shape[2], axis=2),
           "conv_w": lax.dynamic_slice_in_dim(jnp.stack([tot(l, 11) for l in range(depth)]), chip * conv_w.shape[2], conv_w.shape[2], axis=2),
           "conv_b": jnp.stack([tot(l, 12) for l in range(depth)]),
           "final_norm_g": total[depth * nl][0],
           "ada_b": jnp.stack([tot(l, 0) + tot(l, 1) for l in range(depth)])}
    loss_total = total[depth * nl + 1][0, 0]

    dmod_all = jnp.zeros((depth, 16, N_MOD * d), F32)
    for l in range(depth):
        dmod_all = dmod_all.at[l, 0:N_DEV].set(every[l * nl][:, :]).at[l, N_DEV].set(tot(l, 1))
    dmod_cols = lax.dynamic_slice_in_dim(dmod_all, chip * n_ada, n_ada, axis=2)
    out["ada_w"] = jnp.stack([matmul(f"ada_dw{l}", act, dmod_cols[l], ta=True) for l in range(depth)])
    dact = matmul("ada_dx0", dmod_cols[0], ada_w, tb=True, view=ada_view(0, n_ada, True))
    for l in range(1, depth):
        dact = matmul(f"ada_dx{l}", dmod_cols[l], ada_w, tb=True, view=ada_view(l, n_ada, True), add=dact)
    got = _per_device(all_gather_small("gather_dcctx", _pack([dact[N_DEV]])))[0::2, :d]
    dsilu = got[0] + got[1] + got[2] + got[3]
    sig = jax.nn.sigmoid(c_ctx)
    out["c_ctx"] = dsilu * (sig + c_ctx * sig * (1.0 - sig))

    def in_pieces(g):
        full_cols = jnp.concatenate([g["w_main"], g["w_gate"][:, :N_GATE]], axis=1)
        return jnp.stack([full_cols[:, q * n_in:(q + 1) * n_in] for q in range(N_CHIPS)])

    def up_pieces(g):
        gate, value = g["w_up"]
        return jnp.stack([gate[:, :f // 2], gate[:, f // 2:], value[:, :f // 2], value[:, f // 2:]])

    pieces = [jnp.concatenate([in_pieces(grads[l]) for l in range(depth)], axis=1),
              jnp.concatenate([grads[l]["w_out"].reshape(N_CHIPS, d // N_CHIPS, d) for l in range(depth)], axis=1),
              jnp.concatenate([up_pieces(grads[l]) for l in range(depth)], axis=1),
              jnp.concatenate([grads[l]["w_down"].reshape(N_CHIPS, f // N_CHIPS, d) for l in range(depth)], axis=1)]
    from_sibling = exchange_sibling_halves("rs_sibling", pieces)
    chip_sums = [add_sibling_half(f"rs_add_sibling{k}", pc, sib, ci) for k, (pc, sib) in enumerate(zip(pieces, from_sibling))]
    from_chips = scatter_to_chips("rs_chips", chip_sums)
    halves = [add_chip_sums(f"rs_add_chips{k}", cs, got, chip, ci) for k, (cs, got) in enumerate(zip(chip_sums, from_chips))]
    full = join_sibling_halves("rs_join", halves)
    out["w_in"] = full[0].reshape(w_in.shape)
    out["w_out"] = full[1].reshape(w_out.shape)
    out["w_up"] = full[2].reshape(w_up.shape)
    out["w_down"] = full[3].reshape(w_down.shape)

    deltas, new_m, new_v = {}, {}, {}
    for name in WEIGHT_NAMES:
        out[name] = out[name].reshape(weights[name].shape)
        deltas[name], new_m[name], new_v[name] = adamw("adamw_" + name, weights[name], out[name], mom_m[name], mom_v[name])
    grad_x = dx[:n_lat].reshape(x.shape)
    return (loss_total, grad_x, *[out[n] for n in WEIGHT_NAMES], *[deltas[n] for n in WEIGHT_NAMES],
            *[new_m[n] for n in WEIGHT_NAMES], *[new_v[n] for n in WEIGHT_NAMES])
```

```python
import functools
from typing import NamedTuple

import numpy as np
import jax
import jax.numpy as jnp
from jax import lax
from jax.experimental import pallas as pl
from jax.experimental.pallas import tpu as pltpu

F32 = jnp.float32
BF16 = jnp.bfloat16

D_MODEL = 2048
HEAD_DIM = 128
ATT_Q_HEADS = 8
ATT_KV_HEADS = 2
ATT_GROUP = ATT_Q_HEADS // ATT_KV_HEADS
RET_HEADS = 4
GLA_HEADS = 4
GLA_DK = 64
GLA_DV = 128
GLA_RANK = 16
GLA_TAU = 16.0
RET_CHUNK = 128
GLA_CHUNK = 64
GRID_W = 64
ROPE_THETA = 10000.0
N_MOD = 6
EPS = 1e-6
N_MAIN = 5120
N_GATE = 2 * GLA_RANK
LANES = 128
ROW_TILE = 256
FFN_COL_TILE = 256
VMEM_LIMIT = 56 * 1024 * 1024

ADAM_LR = 0.001
ADAM_B1 = 0.9
ADAM_B2 = 0.999
ADAM_EPS = 1e-08
ADAM_WD = 0.01
ADAM_STEP = 10

Z_AQ, Z_AK, Z_AV = 0, 1024, 1280
Z_RQ, Z_RK, Z_RV, Z_RG = 1536, 2048, 2560, 3072
Z_GQ, Z_GK, Z_GV, Z_GR = 3584, 3840, 4096, 4608
P_AQ, P_AK, P_GQ, P_RQ, P_RK, P_LA = 0, 1024, 1280, 1536, 2048, 2560
P_W = 3072


def _params(sem=None):
    return pltpu.CompilerParams(dimension_semantics=sem, vmem_limit_bytes=VMEM_LIMIT)


def _pick(n, cands):
    for c in cands:
        if n % c == 0:
            return c
    return n


_NN = (((1,), (0,)), ((), ()))
_NT = (((1,), (1,)), ((), ()))
_TN = (((0,), (0,)), ((), ()))


def _dg(a, b, dims):
    return lax.dot_general(a.astype(BF16), b.astype(BF16), dims, preferred_element_type=F32)


@jax.custom_vjp
def bdot(a, b):
    return _dg(a, b, _NN)


def _bdot_fwd(a, b):
    return _dg(a, b, _NN), (a, b)


def _bdot_bwd(res, ct):
    a, b = res
    return _dg(ct, b, _NT), _dg(a, ct, _TN)


bdot.defvjp(_bdot_fwd, _bdot_bwd)


@jax.custom_vjp
def bdot_nt(a, b):
    return _dg(a, b, _NT)


def _bdot_nt_fwd(a, b):
    return _dg(a, b, _NT), (a, b)


def _bdot_nt_bwd(res, ct):
    a, b = res
    return _dg(ct, b, _NN), _dg(ct, a, _TN)


bdot_nt.defvjp(_bdot_nt_fwd, _bdot_nt_bwd)


@jax.custom_vjp
def bdot_tn(a, b):
    return _dg(a, b, _TN)


def _bdot_tn_fwd(a, b):
    return _dg(a, b, _TN), (a, b)


def _bdot_tn_bwd(res, ct):
    a, b = res
    return _dg(b, ct, _NT), _dg(a, ct, _NN)


bdot_tn.defvjp(_bdot_tn_fwd, _bdot_tn_bwd)


def _split3(x):
    x1 = x.astype(BF16)
    r1 = x - x1.astype(F32)
    x2 = r1.astype(BF16)
    x3 = (r1 - x2.astype(F32)).astype(BF16)
    return x1, x2, x3


def _mask_dot(mask_bf16, x, dims):
    x1, x2, x3 = _split3(x)
    f = lambda t: lax.dot_general(mask_bf16, t, dims, preferred_element_type=F32)
    return f(x1) + f(x2) + f(x3)


@jax.custom_vjp
def mask_cumsum(mask, x):
    return _mask_dot(mask.astype(BF16), x, _NN)


def _mask_cumsum_fwd(mask, x):
    return mask_cumsum(mask, x), mask


def _mask_cumsum_bwd(mask, ct):
    return jnp.zeros_like(mask), _mask_dot(mask.astype(BF16), ct, _TN)


mask_cumsum.defvjp(_mask_cumsum_fwd, _mask_cumsum_bwd)


def _roll(x, shift, axis):
    return pltpu.roll(x, shift % x.shape[axis], axis)


@functools.partial(jax.custom_vjp, nondiff_argnums=(1, 2))
def roll(x, shift, axis):
    return _roll(x, shift, axis)


def _roll_fwd(x, shift, axis):
    return _roll(x, shift, axis), None


def _roll_bwd(shift, axis, _, ct):
    return (_roll(ct, -shift, axis),)


roll.defvjp(_roll_fwd, _roll_bwd)


def rms(x):
    return x * lax.rsqrt(jnp.mean(x * x, axis=-1, keepdims=True) + EPS)


def silu(x):
    return x * (1.0 / (1.0 + jnp.exp(-x)))


def log_sigmoid(x):
    return jnp.minimum(x, 0.0) - jnp.log(1.0 + jnp.exp(-jnp.abs(x)))


def rope(t, cos, sin):
    return t * cos + roll(t, HEAD_DIM // 2, 1) * sin


def _heads(x, n, width=HEAD_DIM):
    return [x[:, h * width:(h + 1) * width] for h in range(n)]


class Row(NamedTuple):
    arr: jax.Array
    width: int
    idx: int = 0
    diff: bool = True


class Par(NamedTuple):
    arr: jax.Array
    grouped: bool = False
    diff: bool = True


def _row_specs(rows, pars, tm, n_lat_tiles):
    def grp(i):
        return jnp.minimum(i // n_lat_tiles, 1)

    specs = [pl.BlockSpec((tm, r.width), functools.partial(lambda i, k: (i, k), k=r.idx)) for r in rows]
    for p in pars:
        blk = (1,) + p.arr.shape[1:]
        if p.grouped:
            specs.append(pl.BlockSpec(blk, lambda i: (grp(i), 0, 0)))
        else:
            specs.append(pl.BlockSpec(blk, lambda i: (0, 0, 0)))
    return specs


def row_map(name, fn, rows, pars, outs, n_rows, n_lat):
    tm = ROW_TILE
    nr, npar = len(rows), len(pars)

    def body(*refs):
        vals = [r[...] for r in refs[:nr]] + [p[0] for p in refs[nr:nr + npar]]
        res = fn(*vals)
        for o, v in zip(refs[nr + npar:], res):
            o[...] = v.astype(o.dtype)

    return pl.pallas_call(
        body, name=name, grid=(n_rows // tm,),
        in_specs=_row_specs(rows, pars, tm, n_lat // tm),
        out_specs=[pl.BlockSpec((tm, w), lambda i: (i, 0)) for w, _ in outs],
        out_shape=[jax.ShapeDtypeStruct((n_rows, w), dt) for w, dt in outs],
        compiler_params=_params(("arbitrary",)),
    )(*[r.arr for r in rows], *[p.arr for p in pars])


def row_vjp(name, fn, rows, pars, cts, n_rows, n_lat, add_to_first=None, row_grad_dtype=F32):
    tm = ROW_TILE
    nr, npar, nc = len(rows), len(pars), len(cts)
    n_lat_tiles = n_lat // tm
    args = list(rows) + list(pars)
    diff_pos = [k for k, a in enumerate(args) if a.diff]
    n_add = 0 if add_to_first is None else 1

    def body(*refs):
        i = pl.program_id(0)
        vals = [r[...] for r in refs[:nr]] + [p[0] for p in refs[nr:nr + npar]]
        ct_vals = tuple(c[...] for c in refs[nr + npar:nr + npar + nc])
        out_refs = refs[nr + npar + nc + n_add:]

        def g(*dv):
            full = list(vals)
            for k, v in zip(diff_pos, dv):
                full[k] = v
            return tuple(fn(*full))

        _, vjp = jax.vjp(g, *[vals[k] for k in diff_pos])
        grads = vjp(ct_vals)
        for n, (k, o, gr) in enumerate(zip(diff_pos, out_refs, grads)):
            if k < nr:
                o[...] = (gr + refs[nr + npar + nc][...] if (n == 0 and n_add) else gr).astype(o.dtype)
            else:
                first = (i == 0) | (i == n_lat_tiles) if args[k].grouped else (i == 0)

                @pl.when(first)
                def _():
                    o[0] = gr

                @pl.when(jnp.logical_not(first))
                def _():
                    o[0] += gr

    def grp(i):
        return jnp.minimum(i // n_lat_tiles, 1)

    out_specs, out_shape = [], []
    for k in diff_pos:
        a = args[k]
        if k < nr:
            out_specs.append(pl.BlockSpec((tm, a.width), lambda i: (i, 0)))
            out_shape.append(jax.ShapeDtypeStruct((n_rows, a.width), row_grad_dtype))
        else:
            blk = (1,) + a.arr.shape[1:]
            out_specs.append(pl.BlockSpec(blk, (lambda i: (grp(i), 0, 0)) if a.grouped else (lambda i: (0, 0, 0))))
            out_shape.append(jax.ShapeDtypeStruct(a.arr.shape, F32))
    extra = list(cts) + ([add_to_first] if n_add else [])
    ct_specs = [pl.BlockSpec((tm, c.shape[1]), lambda i: (i, 0)) for c in extra]
    return pl.pallas_call(
        body, name=name, grid=(n_rows // tm,),
        in_specs=_row_specs(rows, pars, tm, n_lat_tiles) + ct_specs,
        out_specs=out_specs, out_shape=out_shape,
        compiler_params=_params(("arbitrary",)),
    )(*[r.arr for r in rows], *[p.arr for p in pars], *extra)


class BView(NamedTuple):
    n: int
    k: int
    tn: int
    tk: int
    index_map: object
    lead: int = 1


MATMUL_VMEM_BUDGET = 40 * 1024 * 1024


def _matmul_tiles(m, n, k, a_bytes, b_bytes, o_bytes):
    tms = [c for c in (1152, 1024, 768, 512, 256, 128) if m % c == 0] or [m]
    tns = [c for c in (2048, 1408, 1280, 1024, 768, 512, 256, 128) if n % c == 0] or [n]
    tks = [k] + [c for c in (2816, 2304, 2048, 1408, 1024, 512, 256, 128) if k % c == 0 and c < k]
    for tk in tks:
        fits = [(tm * tn, tm, tn) for tm in tms for tn in tns
                if 2 * (tm * tk * a_bytes + tk * tn * b_bytes + tm * tn * o_bytes) + 2 * tm * tn * 4 <= MATMUL_VMEM_BUDGET]
        if fits and (max(fits)[0] >= 512 * 512 or tms == [m] or tk == tks[-1]):
            _, tm, tn = max(fits)
            return tm, tn, tk
    raise ValueError(f"no matmul tiling for {(m, n, k)}")


def matmul(name, a, b, *, ta=False, tb=False, add=None, out_dtype=F32, view=None):
    m = a.shape[1] if ta else a.shape[0]
    o_bytes = jnp.dtype(out_dtype).itemsize * (1 if add is None else 2)
    if view is None:
        k = a.shape[0] if ta else a.shape[1]
        n = b.shape[0] if tb else b.shape[1]
        assert (b.shape[1] if tb else b.shape[0]) == k, (a.shape, b.shape, ta, tb)
        tm, tn, tk = _matmul_tiles(m, n, k, a.dtype.itemsize, b.dtype.itemsize, o_bytes)
    else:
        n, k, tn, tk = view.n, view.k, view.tn, view.tk
        tm, _, _ = _matmul_tiles(m, tn, tk, a.dtype.itemsize, b.dtype.itemsize, o_bytes)
    nk = k // tk
    dims = (((0 if ta else 1,), (1 if tb else 0,)), ((), ()))

    def body(a_ref, b_ref, *rest):
        prod = lax.dot_general(a_ref[...].astype(BF16), b_ref[...].astype(BF16), dims, preferred_element_type=F32)
        if nk == 1:
            o_ref = rest[-1]
            o_ref[...] = (prod if add is None else prod + rest[0][...]).astype(o_ref.dtype)
            return
        o_ref, acc = rest[-2:]
        kk = pl.program_id(2)

        @pl.when(kk == 0)
        def _():
            acc[...] = prod

        @pl.when(kk != 0)
        def _():
            acc[...] += prod

        @pl.when(kk == nk - 1)
        def _():
            r = acc[...]
            if add is not None:
                r = r + rest[0][...]
            o_ref[...] = r.astype(o_ref.dtype)

    if ta:
        a_spec = pl.BlockSpec((tk, tm), lambda i, j, kk: (kk, i))
    else:
        a_spec = pl.BlockSpec((tm, tk), lambda i, j, kk: (i, kk))
    b_tile = (tn, tk) if tb else (tk, tn)
    if view is not None:
        b_spec = pl.BlockSpec((None,) * view.lead + b_tile, view.index_map)
    elif tb:
        b_spec = pl.BlockSpec(b_tile, lambda i, j, kk: (j, kk))
    else:
        b_spec = pl.BlockSpec(b_tile, lambda i, j, kk: (kk, j))
    o_spec = pl.BlockSpec((tm, tn), lambda i, j, kk: (i, j))
    ins = [a, b] + ([add] if add is not None else [])
    return pl.pallas_call(
        body, name=name, grid=(m // tm, n // tn, nk),
        in_specs=[a_spec, b_spec] + ([o_spec] if add is not None else []),
        out_specs=o_spec, out_shape=jax.ShapeDtypeStruct((m, n), out_dtype),
        scratch_shapes=[pltpu.VMEM((tm, tn), F32)] if nk > 1 else [],
        compiler_params=_params(("parallel", "parallel", "arbitrary")),
    )(*ins)


def normmod_tile(x, g, shift, scale):
    return (rms(x) * g * (1.0 + scale) + shift,)


def resid_tile(x, y, gate):
    return (x + gate * y,)


def prep_tile(z_qk, z_rq, z_rk, z_gq, zg, cos, sin, qg, kg, gate_up, gate_b):
    out = []
    for h, t in enumerate(_heads(z_qk, ATT_Q_HEADS + ATT_KV_HEADS)):
        out.append(rope(rms(t) * (qg if h < ATT_Q_HEADS else kg), cos, sin))
    gq = z_gq * (GLA_DK ** -0.5)
    rq = [rope(t, cos, sin) for t in _heads(z_rq, RET_HEADS)]
    rk = [rope(t * (HEAD_DIM ** -0.5), cos, sin) for t in _heads(z_rk, RET_HEADS)]
    la = [log_sigmoid(bdot(zg, gate_up[d * LANES:(d + 1) * LANES]) + gate_b[d:d + 1]) * (1.0 / GLA_TAU) for d in range(2)]
    return (jnp.concatenate(out + [gq] + rq + rk + la, axis=1),)


def post_tile(o_att, o_ret_f, o_ret_b, o_gla_f, o_gla_b, rg, gr, ret_g, gla_g):
    ret = jnp.concatenate([rms(t) * ret_g for t in _heads(o_ret_f + o_ret_b, RET_HEADS)], axis=1) * silu(rg)
    gla = jnp.concatenate([rms(t) * gla_g for t in _heads(o_gla_f + o_gla_b, GLA_HEADS)], axis=1) * silu(gr)
    return (jnp.concatenate([o_att, ret, gla], axis=1),)


def _convglu_tile(n_lat, a, v, cw, cb):
    t = a.shape[0]
    row = lax.broadcasted_iota(jnp.int32, (t, 1), 0)
    has_prev = ((row != 0) & (row != n_lat)).astype(F32)
    has_next = ((row != n_lat - 1) & (row != t - 1)).astype(F32)
    conv = roll(a, 1, 0) * has_prev * cw[0:1] + a * cw[1:2] + roll(a, -1, 0) * has_next * cw[2:3] + cb
    return silu(conv) * v


def convglu(name, u, cw, cb, n_lat):
    t, f2 = u.shape
    f, tc = f2 // 2, FFN_COL_TILE
    nb = f // tc

    def body(a_ref, v_ref, cw_ref, cb_ref, o_ref):
        o_ref[...] = _convglu_tile(n_lat, a_ref[...], v_ref[...], cw_ref[...], cb_ref[...]).astype(o_ref.dtype)

    return pl.pallas_call(
        body, name=name, grid=(nb,),
        in_specs=[pl.BlockSpec((t, tc), lambda j: (0, j)), pl.BlockSpec((t, tc), lambda j: (0, nb + j)),
                  pl.BlockSpec((3, tc), lambda j: (0, j)), pl.BlockSpec((1, tc), lambda j: (0, j))],
        out_specs=pl.BlockSpec((t, tc), lambda j: (0, j)),
        out_shape=jax.ShapeDtypeStruct((t, f), BF16),
        compiler_params=_params(("parallel",)),
    )(u, u, cw, cb)


def convglu_bwd(name, u, cw, cb, dg, n_lat):
    t, f2 = u.shape
    f, tc = f2 // 2, FFN_COL_TILE
    nb = f // tc

    def body(a_ref, v_ref, cw_ref, cb_ref, dg_ref, da_ref, dv_ref, dcw_ref, dcb_ref):
        _, vjp = jax.vjp(functools.partial(_convglu_tile, n_lat), a_ref[...], v_ref[...], cw_ref[...], cb_ref[...])
        da, dv, dcw_ref[...], dcb_ref[...] = vjp(dg_ref[...])
        da_ref[...], dv_ref[...] = da.astype(BF16), dv.astype(BF16)

    col = pl.BlockSpec((t, tc), lambda j: (0, j))
    return pl.pallas_call(
        body, name=name, grid=(nb,),
        in_specs=[col, pl.BlockSpec((t, tc), lambda j: (0, nb + j)), pl.BlockSpec((3, tc), lambda j: (0, j)),
                  pl.BlockSpec((1, tc), lambda j: (0, j)), col],
        out_specs=[col, col, pl.BlockSpec((3, tc), lambda j: (0, j)), pl.BlockSpec((1, tc), lambda j: (0, j))],
        out_shape=[jax.ShapeDtypeStruct((t, f), BF16), jax.ShapeDtypeStruct((t, f), BF16),
                   jax.ShapeDtypeStruct((3, f), F32), jax.ShapeDtypeStruct((1, f), F32)],
        compiler_params=_params(("parallel",)),
    )(u, u, cw, cb, dg)


def final_loss(x, target, g, n_lat):
    tm = ROW_TILE
    d = x.shape[1]

    def body(x_ref, t_ref, g_ref, loss_ref, dx_ref, dg_ref):
        i = pl.program_id(0)
        tgt = t_ref[...]

        def f(xv, gv):
            e = rms(xv) * gv - tgt
            s = jnp.sum(jnp.sum(e * e, axis=1, keepdims=True), axis=0, keepdims=True)
            return s * (0.5 / d)

        val, vjp = jax.vjp(f, x_ref[...], g_ref[...])
        dx, dgv = vjp(jnp.ones((1, 1), F32))
        dx_ref[...] = dx

        @pl.when(i == 0)
        def _():
            dg_ref[...] = dgv
            loss_ref[...] = jnp.broadcast_to(val, loss_ref.shape)

        @pl.when(i != 0)
        def _():
            dg_ref[...] += dgv
            loss_ref[...] += jnp.broadcast_to(val, loss_ref.shape)

    return pl.pallas_call(
        body, name="final_loss", grid=(n_lat // tm,),
        in_specs=[pl.BlockSpec((tm, d), lambda i: (i, 0)), pl.BlockSpec((tm, d), lambda i: (i, 0)),
                  pl.BlockSpec((1, d), lambda i: (0, 0))],
        out_specs=[pl.BlockSpec((1, LANES), lambda i: (0, 0)), pl.BlockSpec((tm, d), lambda i: (i, 0)),
                   pl.BlockSpec((1, d), lambda i: (0, 0))],
        out_shape=[jax.ShapeDtypeStruct((1, LANES), F32), jax.ShapeDtypeStruct((n_lat, d), F32),
                   jax.ShapeDtypeStruct((1, d), F32)],
        compiler_params=_params(("arbitrary",)),
    )(x, target, g)


ATT_SCALE = HEAD_DIM ** -0.5
_AK_BLK = P_AK // HEAD_DIM
_AV_BLK = Z_AV // HEAD_DIM


def _att_specs(t, tq):
    gw = ATT_GROUP * HEAD_DIM
    q_spec = pl.BlockSpec((tq, gw), lambda kv, i: (i, kv))
    k_spec = pl.BlockSpec((t, HEAD_DIM), lambda kv, i: (0, _AK_BLK + kv))
    v_spec = pl.BlockSpec((t, HEAD_DIM), lambda kv, i: (0, _AV_BLK + kv))
    row_spec = pl.BlockSpec((ATT_GROUP, tq, 1), lambda kv, i: (kv, i, 0))
    return q_spec, k_spec, v_spec, row_spec


def _att_mask(i, t, tq, n_lat):
    col = lax.broadcasted_iota(jnp.int32, (1, t), 1)
    return jnp.where((i >= n_lat // tq) & (col < n_lat), -jnp.inf, 0.0).astype(F32)


def attn_fwd(p, z, n_lat):
    t = p.shape[0]
    tq = ROW_TILE

    def body(q_ref, k_ref, v_ref, o_ref, lse_ref):
        mask = _att_mask(pl.program_id(1), t, tq, n_lat)
        k, v = k_ref[...].astype(BF16), v_ref[...].astype(BF16)
        for g in range(ATT_GROUP):
            cols = slice(g * HEAD_DIM, (g + 1) * HEAD_DIM)
            s = _dg(q_ref[:, cols], k, _NT) * ATT_SCALE + mask
            m = jnp.max(s, axis=1, keepdims=True)
            pr = jnp.exp(s - m)
            l = jnp.sum(pr, axis=1, keepdims=True)
            o_ref[:, cols] = _dg(pr, v, _NN) / l
            lse_ref[g] = m + jnp.log(l)

    q_spec, k_spec, v_spec, row_spec = _att_specs(t, tq)
    return pl.pallas_call(
        body, name="attn_fwd", grid=(ATT_KV_HEADS, t // tq),
        in_specs=[q_spec, k_spec, v_spec], out_specs=[q_spec, row_spec],
        out_shape=[jax.ShapeDtypeStruct((t, ATT_Q_HEADS * HEAD_DIM), F32),
                   jax.ShapeDtypeStruct((ATT_Q_HEADS, t, 1), F32)],
        compiler_params=_params(("parallel", "parallel")),
    )(p, p, z)


def attn_bwd(p, z, o, lse, do, n_lat):
    t = p.shape[0]
    tq = ROW_TILE

    def body(q_ref, k_ref, v_ref, o_ref, do_ref, lse_ref, dq_ref, dk_ref, dv_ref):
        i = pl.program_id(1)

        @pl.when(i == 0)
        def _():
            dk_ref[...] = jnp.zeros_like(dk_ref)
            dv_ref[...] = jnp.zeros_like(dv_ref)

        mask = _att_mask(i, t, tq, n_lat)
        k, v = k_ref[...].astype(BF16), v_ref[...].astype(BF16)
        dk, dv = dk_ref[...], dv_ref[...]
        for g in range(ATT_GROUP):
            cols = slice(g * HEAD_DIM, (g + 1) * HEAD_DIM)
            q, do_g = q_ref[:, cols].astype(BF16), do_ref[:, cols]
            pr = jnp.exp(_dg(q, k, _NT) * ATT_SCALE + mask - lse_ref[g])
            delta = jnp.sum(o_ref[:, cols] * do_g, axis=1, keepdims=True)
            ds = pr * (_dg(do_g, v, _NT) - delta) * ATT_SCALE
            dq_ref[:, cols] = _dg(ds, k, _NN)
            dk = dk + _dg(ds, q, _TN)
            dv = dv + _dg(pr, do_g, _TN)
        dk_ref[...], dv_ref[...] = dk, dv

    q_spec, k_spec, v_spec, row_spec = _att_specs(t, tq)
    kv_out = pl.BlockSpec((t, HEAD_DIM), lambda kv, i: (0, kv))
    return pl.pallas_call(
        body, name="attn_bwd", grid=(ATT_KV_HEADS, t // tq),
        in_specs=[q_spec, k_spec, v_spec, q_spec, q_spec, row_spec],
        out_specs=[q_spec, kv_out, kv_out],
        out_shape=[jax.ShapeDtypeStruct((t, ATT_Q_HEADS * HEAD_DIM), F32),
                   jax.ShapeDtypeStruct((t, ATT_KV_HEADS * HEAD_DIM), F32),
                   jax.ShapeDtypeStruct((t, ATT_KV_HEADS * HEAD_DIM), F32)],
        compiler_params=_params(("parallel", "arbitrary")),
    )(p, p, z, o, do, lse)


_RQ_BLK = P_RQ // HEAD_DIM
_RK_BLK = P_RK // HEAD_DIM
_RV_BLK = Z_RV // HEAD_DIM


def _scan_chunk(direction, step, n_chunks, n_lat_chunks):
    return jnp.where(direction == 0, (step + n_lat_chunks) % n_chunks, n_chunks - 1 - step)


def _ret_geometry(direction):
    c = RET_CHUNK
    i = lax.broadcasted_iota(jnp.int32, (c, c), 0)
    j = lax.broadcasted_iota(jnp.int32, (c, c), 1)
    rel = jnp.where(direction == 0, i - j, j - i).astype(F32)
    r = lax.broadcasted_iota(jnp.int32, (c, 1), 0)
    pos = jnp.where(direction == 0, r, c - 1 - r).astype(F32)
    return rel, pos


def ret_chunk(q, k, v, s, lg, rel, pos):
    c = RET_CHUNK
    causal = rel >= 0
    d_in = jnp.where(causal, jnp.exp(lg * jnp.where(causal, rel, 0.0)), 0.0)
    q_dec = jnp.exp(lg * (pos + 1.0))
    k_dec = jnp.exp(lg * (c - 1.0 - pos))
    c_dec = jnp.exp(lg * c)
    att = bdot_nt(q, k) * d_in
    o = bdot(att, v) + bdot(q * q_dec, s)
    s_new = c_dec * s + bdot_tn(k * k_dec, v)
    return o, s_new


def ret_fwd(p, z, lg, n_lat):
    t = p.shape[0]
    c = RET_CHUNK
    nc, nlc = t // c, n_lat // c

    def body(q_ref, k_ref, v_ref, lg_ref, o_ref, ssave_ref, s_s):
        d, n = pl.program_id(0), pl.program_id(2)

        @pl.when(n == 0)
        def _():
            s_s[...] = jnp.zeros_like(s_s)

        rel, pos = _ret_geometry(d)
        ssave_ref[0, 0, 0] = s_s[...]
        o, s_new = ret_chunk(q_ref[...], k_ref[...], v_ref[...], s_s[...], lg_ref[0, 0], rel, pos)
        o_ref[...] = o
        s_s[...] = s_new

    def blk(base):
        return pl.BlockSpec((c, HEAD_DIM), lambda d, h, n: (_scan_chunk(d, n, nc, nlc), base + h))

    return pl.pallas_call(
        body, name="ret_fwd", grid=(2, RET_HEADS, nc),
        in_specs=[blk(_RQ_BLK), blk(_RK_BLK), blk(_RV_BLK), pl.BlockSpec((1, 1, 1, 1), lambda d, h, n: (d, h, 0, 0))],
        out_specs=[pl.BlockSpec((c, HEAD_DIM), lambda d, h, n: (_scan_chunk(d, n, nc, nlc), d * RET_HEADS + h)),
                   pl.BlockSpec((1, 1, 1, HEAD_DIM, HEAD_DIM), lambda d, h, n: (d, h, n, 0, 0))],
        out_shape=[jax.ShapeDtypeStruct((t, 2 * RET_HEADS * HEAD_DIM), F32),
                   jax.ShapeDtypeStruct((2, RET_HEADS, nc, HEAD_DIM, HEAD_DIM), F32)],
        scratch_shapes=[pltpu.VMEM((HEAD_DIM, HEAD_DIM), F32)],
        compiler_params=_params(("parallel", "parallel", "arbitrary")),
    )(p, p, z, lg)


def ret_bwd(p, z, lg, states, do, n_lat):
    t = p.shape[0]
    c = RET_CHUNK
    nc, nlc = t // c, n_lat // c

    def body(q_ref, k_ref, v_ref, lg_ref, s_ref, do_ref, dq_ref, dk_ref, dv_ref, dlg_ref, ds_s):
        d, n = pl.program_id(0), pl.program_id(2)

        @pl.when(n == 0)
        def _():
            ds_s[...] = jnp.zeros_like(ds_s)
            dlg_ref[...] = jnp.zeros_like(dlg_ref)

        rel, pos = _ret_geometry(d)
        f = functools.partial(ret_chunk, rel=rel, pos=pos)
        _, vjp = jax.vjp(f, q_ref[...], k_ref[...], v_ref[...], s_ref[0, 0, 0], lg_ref[0, 0])
        dq, dk, dv, ds, dlg = vjp((do_ref[...], ds_s[...]))
        dq_ref[...], dk_ref[...], dv_ref[...] = dq, dk, dv
        ds_s[...] = ds
        dlg_ref[0, 0] += dlg

    def chunk_of(d, n):
        return _scan_chunk(d, nc - 1 - n, nc, nlc)

    def blk(base):
        return pl.BlockSpec((c, HEAD_DIM), lambda d, h, n: (chunk_of(d, n), base + h))

    out_blk = pl.BlockSpec((c, HEAD_DIM), lambda d, h, n: (chunk_of(d, n), d * RET_HEADS + h))
    grad_shape = jax.ShapeDtypeStruct((t, 2 * RET_HEADS * HEAD_DIM), F32)
    return pl.pallas_call(
        body, name="ret_bwd", grid=(2, RET_HEADS, nc),
        in_specs=[blk(_RQ_BLK), blk(_RK_BLK), blk(_RV_BLK), pl.BlockSpec((1, 1, 1, 1), lambda d, h, n: (d, h, 0, 0)),
                  pl.BlockSpec((1, 1, 1, HEAD_DIM, HEAD_DIM), lambda d, h, n: (d, h, nc - 1 - n, 0, 0)),
                  pl.BlockSpec((c, HEAD_DIM), lambda d, h, n: (chunk_of(d, n), h))],
        out_specs=[out_blk, out_blk, out_blk, pl.BlockSpec((1, 1, 1, 1), lambda d, h, n: (d, h, 0, 0))],
        out_shape=[grad_shape, grad_shape, grad_shape, jax.ShapeDtypeStruct((2, RET_HEADS, 1, 1), F32)],
        scratch_shapes=[pltpu.VMEM((HEAD_DIM, HEAD_DIM), F32)],
        compiler_params=_params(("parallel", "parallel", "arbitrary")),
    )(p, p, z, lg, states, do)


_GQ_BLK = P_GQ // (GLA_HEADS * GLA_DK)
_GK_BLK = Z_GK // (GLA_HEADS * GLA_DK)
_GV_BLK = Z_GV // (GLA_HEADS * GLA_DV)
_LA_BLK = P_LA // (GLA_HEADS * GLA_DK)


def _gla_mask(direction):
    c = GLA_CHUNK
    i = lax.broadcasted_iota(jnp.int32, (c, c), 0)
    j = lax.broadcasted_iota(jnp.int32, (c, c), 1)
    return (jnp.where(direction == 0, i - j, j - i) >= 0).astype(F32)


def gla_chunk(q, k, v, la, st, mask):
    b = mask_cumsum(mask, la)
    btot = jnp.sum(la, axis=0, keepdims=True)
    half = 0.5 * btot
    qt, kt = q * jnp.exp(b - half), k * jnp.exp(half - b)
    qs, ke = q * jnp.exp(b), k * jnp.exp(btot - b)
    outs, upd = [], []
    for h in range(GLA_HEADS):
        ks = slice(h * GLA_DK, (h + 1) * GLA_DK)
        vh = v[:, h * GLA_DV:(h + 1) * GLA_DV]
        att = bdot_nt(qt[:, ks], kt[:, ks]) * mask
        outs.append(bdot(att, vh) + bdot_nt(qs[:, ks], st[:, ks]))
        upd.append(bdot_tn(vh, ke[:, ks]))
    st_new = st * jnp.exp(btot) + jnp.concatenate(upd, axis=1)
    return jnp.concatenate(outs, axis=1), st_new


def gla_fwd(p, z, n_lat):
    t = p.shape[0]
    c = GLA_CHUNK
    nc, nlc = t // c, n_lat // c
    kw, vw = GLA_HEADS * GLA_DK, GLA_HEADS * GLA_DV

    def body(q_ref, k_ref, v_ref, la_ref, o_ref, ssave_ref, s_s):
        d, n = pl.program_id(0), pl.program_id(1)

        @pl.when(n == 0)
        def _():
            s_s[...] = jnp.zeros_like(s_s)

        ssave_ref[0, 0] = s_s[...]
        o, s_new = gla_chunk(q_ref[...], k_ref[...], v_ref[...], la_ref[...], s_s[...], _gla_mask(d))
        o_ref[...] = o
        s_s[...] = s_new

    def chunk_of(d, n):
        return _scan_chunk(d, n, nc, nlc)

    return pl.pallas_call(
        body, name="gla_fwd", grid=(2, nc),
        in_specs=[pl.BlockSpec((c, kw), lambda d, n: (chunk_of(d, n), _GQ_BLK)),
                  pl.BlockSpec((c, kw), lambda d, n: (chunk_of(d, n), _GK_BLK)),
                  pl.BlockSpec((c, vw), lambda d, n: (chunk_of(d, n), _GV_BLK)),
                  pl.BlockSpec((c, kw), lambda d, n: (chunk_of(d, n), _LA_BLK + d))],
        out_specs=[pl.BlockSpec((c, vw), lambda d, n: (chunk_of(d, n), d)),
                   pl.BlockSpec((1, 1, GLA_DV, kw), lambda d, n: (d, n, 0, 0))],
        out_shape=[jax.ShapeDtypeStruct((t, 2 * vw), F32), jax.ShapeDtypeStruct((2, nc, GLA_DV, kw), F32)],
        scratch_shapes=[pltpu.VMEM((GLA_DV, kw), F32)],
        compiler_params=_params(("parallel", "arbitrary")),
    )(p, z, z, p)


def gla_bwd(p, z, states, do, n_lat):
    t = p.shape[0]
    c = GLA_CHUNK
    nc, nlc = t // c, n_lat // c
    kw, vw = GLA_HEADS * GLA_DK, GLA_HEADS * GLA_DV

    def body(q_ref, k_ref, v_ref, la_ref, s_ref, do_ref, dq_ref, dk_ref, dv_ref, dla_ref, ds_s):
        d, n = pl.program_id(0), pl.program_id(1)

        @pl.when(n == 0)
        def _():
            ds_s[...] = jnp.zeros_like(ds_s)

        f = functools.partial(gla_chunk, mask=_gla_mask(d))
        _, vjp = jax.vjp(f, q_ref[...], k_ref[...], v_ref[...], la_ref[...], s_ref[0, 0])
        dq_ref[...], dk_ref[...], dv_ref[...], dla_ref[...], ds_s[...] = vjp((do_ref[...], ds_s[...]))

    def chunk_of(d, n):
        return _scan_chunk(d, nc - 1 - n, nc, nlc)

    k_out = pl.BlockSpec((c, kw), lambda d, n: (chunk_of(d, n), d))
    return pl.pallas_call(
        body, name="gla_bwd", grid=(2, nc),
        in_specs=[pl.BlockSpec((c, kw), lambda d, n: (chunk_of(d, n), _GQ_BLK)),
                  pl.BlockSpec((c, kw), lambda d, n: (chunk_of(d, n), _GK_BLK)),
                  pl.BlockSpec((c, vw), lambda d, n: (chunk_of(d, n), _GV_BLK)),
                  pl.BlockSpec((c, kw), lambda d, n: (chunk_of(d, n), _LA_BLK + d)),
                  pl.BlockSpec((1, 1, GLA_DV, kw), lambda d, n: (d, nc - 1 - n, 0, 0)),
                  pl.BlockSpec((c, vw), lambda d, n: (chunk_of(d, n), 0))],
        out_specs=[k_out, k_out, pl.BlockSpec((c, vw), lambda d, n: (chunk_of(d, n), d)), k_out],
        out_shape=[jax.ShapeDtypeStruct((t, 2 * kw), F32), jax.ShapeDtypeStruct((t, 2 * kw), F32),
                   jax.ShapeDtypeStruct((t, 2 * vw), F32), jax.ShapeDtypeStruct((t, 2 * kw), F32)],
        scratch_shapes=[pltpu.VMEM((GLA_DV, kw), F32)],
        compiler_params=_params(("parallel", "arbitrary")),
    )(p, z, z, p, states, do)


def _adam_tile(w, g, m, v):
    m = ADAM_B1 * m + (1.0 - ADAM_B1) * g
    v = ADAM_B2 * v + (1.0 - ADAM_B2) * (g * g)
    m_hat = m / (1.0 - ADAM_B1 ** ADAM_STEP)
    v_hat = v / (1.0 - ADAM_B2 ** ADAM_STEP)
    delta = -ADAM_LR * (m_hat / (jnp.sqrt(v_hat) + ADAM_EPS) + ADAM_WD * w)
    return delta, m, v


def adamw(name, w, g, m, v):
    shape = w.shape
    cols = shape[-1] if w.ndim > 1 and shape[-1] >= LANES else int(np.prod(shape))
    rows = int(np.prod(shape)) // cols
    tr = rows
    for cand in (512, 256, 128, 64, 32, 16, 8):
        if rows % cand == 0 and cand * cols * 4 <= (1 << 20):
            tr = cand
            break
    flat = [a.reshape(rows, cols) for a in (w, g, m, v)]

    def body(w_ref, g_ref, m_ref, v_ref, d_ref, mo_ref, vo_ref):
        d_ref[...], mo_ref[...], vo_ref[...] = _adam_tile(w_ref[...], g_ref[...], m_ref[...], v_ref[...])

    spec = pl.BlockSpec((tr, cols), lambda i: (i, 0))
    outs = pl.pallas_call(
        body, name=name, grid=(rows // tr,),
        in_specs=[spec] * 4, out_specs=[spec] * 3,
        out_shape=[jax.ShapeDtypeStruct((rows, cols), F32)] * 3,
        compiler_params=_params(("parallel",)),
    )(*flat)
    return tuple(o.reshape(shape) for o in outs)


MESH = pl.DeviceIdType.MESH
_HBM = pl.BlockSpec(memory_space=pltpu.HBM)
N_CHIPS = 4
N_DEV = 8


def _place():
    x, y, c = lax.axis_index("x"), lax.axis_index("y"), lax.axis_index("c")
    chips = [(1 - x, y), (x, 1 - y), (1 - x, 1 - y)]
    return x, y, c, chips


def _remote(src, dst, send_sem, recv_sem, to):
    return pltpu.make_async_remote_copy(src_ref=src, dst_ref=dst, send_sem=send_sem, recv_sem=recv_sem,
                                        device_id=to, device_id_type=MESH)


def all_gather_small(name, v):
    m_per, n = v.shape

    def body(x_ref, out_ref, send_sems, recv_sems, local_sem):
        x, y, c, chips = _place()
        me, sibling = (x, y, c), (x, y, 1 - c)

        def rows(px, py, pc):
            return out_ref.at[pl.ds((4 * px + 2 * py + pc) * m_per, m_per), :]

        def copy(k, block, to, src=None):
            return _remote(rows(*block) if src is None else src, rows(*block), send_sems.at[k], recv_sems.at[k], to)

        mine = pltpu.make_async_copy(x_ref, rows(*me), local_sem)
        mine.start()
        first = [copy(0, me, sibling, src=x_ref)]
        first += [copy(1 + j, me, (*chip, c), src=x_ref) for j, chip in enumerate(chips)]
        for cp in first:
            cp.start()
        passed = [copy(4 + j, (*chip, c), sibling) for j, chip in enumerate(chips)]
        for j, chip in enumerate(chips):
            copy(1 + j, (*chip, c), me).wait_recv()
            passed[j].start()
        copy(0, sibling, me).wait_recv()
        for j, chip in enumerate(chips):
            copy(4 + j, (*chip, 1 - c), me).wait_recv()
        for cp in first + passed:
            cp.wait_send()
        mine.wait()

    return pl.pallas_call(
        body, name=name,
        out_shape=jax.ShapeDtypeStruct((N_DEV * m_per, n), v.dtype),
        in_specs=[pl.BlockSpec(memory_space=pltpu.VMEM)],
        out_specs=pl.BlockSpec(memory_space=pltpu.VMEM),
        scratch_shapes=[pltpu.SemaphoreType.DMA((7,)), pltpu.SemaphoreType.DMA((7,)), pltpu.SemaphoreType.DMA],
        compiler_params=pltpu.CompilerParams(vmem_limit_bytes=VMEM_LIMIT),
    )(v)


def all_gather_shards(name, shards):
    nt = len(shards)

    def body(*refs):
        x_refs, out_refs = refs[:nt], refs[nt:2 * nt]
        send_sems, recv_sems = refs[2 * nt:]
        x, y, c, chips = _place()
        q = 2 * x + y
        sibling = (x, y, 1 - c)
        sends = []
        for t in range(nt):
            half = x_refs[t].shape[0] // 2

            def part(qq, hh, t=t, half=half):
                return out_refs[t].at[qq, pl.ds(hh * half, half), :]

            for j, chip in enumerate(chips):
                cp = _remote(x_refs[t].at[pl.ds(c * half, half), :], part(q, c), send_sems.at[6 * t + j],
                             recv_sems.at[6 * t + j], (*chip, c))
                cp.start()
                sends.append(cp)
        for t in range(nt):
            half = x_refs[t].shape[0] // 2

            def part(qq, hh, t=t, half=half):
                return out_refs[t].at[qq, pl.ds(hh * half, half), :]

            for j, chip in enumerate(chips):
                qj = 2 * chip[0] + chip[1]
                _remote(part(qj, c), part(qj, c), send_sems.at[6 * t + j], recv_sems.at[6 * t + j], (*chip, c)).wait_recv()
                cp = _remote(part(qj, c), part(qj, c), send_sems.at[6 * t + 3 + j], recv_sems.at[6 * t + 3 + j], sibling)
                cp.start()
                sends.append(cp)
        for t in range(nt):
            half = x_refs[t].shape[0] // 2
            for j, chip in enumerate(chips):
                qj = 2 * chip[0] + chip[1]
                dst = out_refs[t].at[qj, pl.ds((1 - c) * half, half), :]
                _remote(dst, dst, send_sems.at[6 * t + 3 + j], recv_sems.at[6 * t + 3 + j], sibling).wait_recv()
        for cp in sends:
            cp.wait_send()

    return pl.pallas_call(
        body, name=name,
        out_shape=[jax.ShapeDtypeStruct((N_CHIPS,) + s.shape, s.dtype) for s in shards],
        in_specs=[_HBM] * nt, out_specs=[_HBM] * nt,
        scratch_shapes=[pltpu.SemaphoreType.DMA((6 * nt,)), pltpu.SemaphoreType.DMA((6 * nt,))],
    )(*shards)


def exchange_sibling_halves(name, grads):
    nt = len(grads)

    def body(*refs):
        g_refs, out_refs = refs[:nt], refs[nt:2 * nt]
        send_sems, recv_sems = refs[2 * nt:]
        x, y, c, _ = _place()
        sibling = (x, y, 1 - c)
        cps = []
        for t in range(nt):
            half = g_refs[t].shape[1] // 2
            cp = _remote(g_refs[t].at[:, pl.ds((1 - c) * half, half), :], out_refs[t], send_sems.at[t], recv_sems.at[t], sibling)
            cp.start()
            cps.append(cp)
        for cp in cps:
            cp.wait()

    return pl.pallas_call(
        body, name=name,
        out_shape=[jax.ShapeDtypeStruct((g.shape[0], g.shape[1] // 2, g.shape[2]), g.dtype) for g in grads],
        in_specs=[_HBM] * nt, out_specs=[_HBM] * nt,
        scratch_shapes=[pltpu.SemaphoreType.DMA((nt,)), pltpu.SemaphoreType.DMA((nt,))],
    )(*grads)


def scatter_to_chips(name, sums):
    nt = len(sums)

    def body(*refs):
        s_refs, out_refs = refs[:nt], refs[nt:2 * nt]
        send_sems, recv_sems = refs[2 * nt:]
        x, y, c, chips = _place()
        cps = []
        for t in range(nt):
            for j, chip in enumerate(chips):
                qj = 2 * chip[0] + chip[1]
                cp = _remote(s_refs[t].at[qj], out_refs[t].at[j], send_sems.at[3 * t + j], recv_sems.at[3 * t + j], (*chip, c))
                cp.start()
                cps.append(cp)
        for cp in cps:
            cp.wait()

    return pl.pallas_call(
        body, name=name,
        out_shape=[jax.ShapeDtypeStruct((3,) + s.shape[1:], s.dtype) for s in sums],
        in_specs=[_HBM] * nt, out_specs=[_HBM] * nt,
        scratch_shapes=[pltpu.SemaphoreType.DMA((3 * nt,)), pltpu.SemaphoreType.DMA((3 * nt,))],
    )(*sums)


def join_sibling_halves(name, halves):
    nt = len(halves)

    def body(*refs):
        out_refs = refs[nt:2 * nt]
        send_sems, recv_sems = refs[2 * nt:]
        x, y, c, _ = _place()
        sibling = (x, y, 1 - c)
        cps = []
        for t in range(nt):
            half = out_refs[t].shape[0] // 2
            mine = out_refs[t].at[pl.ds(c * half, half), :]
            cp = _remote(mine, mine, send_sems.at[t], recv_sems.at[t], sibling)
            cp.start()
            cps.append(cp)
        for t, cp in enumerate(cps):
            half = out_refs[t].shape[0] // 2
            other = out_refs[t].at[pl.ds((1 - c) * half, half), :]
            _remote(other, other, send_sems.at[t], recv_sems.at[t], sibling).wait_recv()
            cp.wait_send()

    return pl.pallas_call(
        body, name=name,
        out_shape=[jax.ShapeDtypeStruct(h.shape, h.dtype) for h in halves],
        in_specs=[_HBM] * nt, out_specs=[_HBM] * nt,
        input_output_aliases={t: t for t in range(nt)},
        scratch_shapes=[pltpu.SemaphoreType.DMA((nt,)), pltpu.SemaphoreType.DMA((nt,))],
    )(*halves)


def _rows_tile(rows, cols):
    for cand in (512, 256, 128, 64, 32, 16):
        if rows % cand == 0 and cand * cols * 4 <= (1 << 20):
            return cand
    return rows


def add_sibling_half(name, pieces, from_sibling, core):
    n, h, cols = from_sibling.shape
    tr = _rows_tile(h, cols)
    nb = h // tr

    def body(c_ref, a_ref, b_ref, o_ref):
        o_ref[...] = (a_ref[...].astype(F32) + b_ref[...].astype(F32)).astype(o_ref.dtype)

    blk = pl.BlockSpec((1, tr, cols), lambda q, i, c_ref: (q, i, 0))
    return pl.pallas_call(
        body, name=name,
        grid_spec=pltpu.PrefetchScalarGridSpec(
            num_scalar_prefetch=1, grid=(n, nb),
            in_specs=[pl.BlockSpec((1, tr, cols), lambda q, i, c_ref: (q, c_ref[0] * nb + i, 0)), blk], out_specs=blk),
        out_shape=jax.ShapeDtypeStruct((n, h, cols), BF16),
        compiler_params=_params(("parallel", "parallel")),
    )(core.reshape(1).astype(jnp.int32), pieces, from_sibling)


def add_chip_sums(name, chip_sums, from_chips, chip, core):
    _, h, cols = chip_sums.shape
    tr = _rows_tile(h, cols)
    nb = h // tr

    def body(s_ref, own_ref, r0_ref, r1_ref, r2_ref, o_ref):
        acc = own_ref[0].astype(F32) + r0_ref[0].astype(F32)
        o_ref[...] = acc + r1_ref[0].astype(F32) + r2_ref[0].astype(F32)

    def got(j):
        return pl.BlockSpec((1, tr, cols), lambda i, s_ref: (j, i, 0))

    return pl.pallas_call(
        body, name=name,
        grid_spec=pltpu.PrefetchScalarGridSpec(
            num_scalar_prefetch=1, grid=(nb,),
            in_specs=[pl.BlockSpec((1, tr, cols), lambda i, s_ref: (s_ref[0], i, 0)), got(0), got(1), got(2)],
            out_specs=pl.BlockSpec((tr, cols), lambda i, s_ref: (s_ref[1] * nb + i, 0))),
        out_shape=jax.ShapeDtypeStruct((2 * h, cols), F32),
        compiler_params=_params(("parallel",)),
    )(jnp.stack([chip, core]).astype(jnp.int32), chip_sums, from_chips, from_chips, from_chips)


def sum_device_blocks(name, g):
    n = g.shape[1]

    def body(g_ref, o_ref):
        acc = g_ref[0:8, :]
        for d in range(1, N_DEV):
            acc = acc + g_ref[8 * d:8 * (d + 1), :]
        o_ref[...] = acc

    return pl.pallas_call(body, name=name, out_shape=jax.ShapeDtypeStruct((8, n), F32),
                          compiler_params=pltpu.CompilerParams(vmem_limit_bytes=VMEM_LIMIT))(g)


class LayerWeights(NamedTuple):
    w_main: jax.Array
    w_gate: jax.Array
    g_out: jax.Array
    g_up: jax.Array
    g_down: jax.Array
    norm1_g: jax.Array
    q_g: jax.Array
    k_g: jax.Array
    lg: jax.Array
    ret_g: jax.Array
    gate_up: jax.Array
    gate_b: jax.Array
    gla_g: jax.Array
    norm2_g: jax.Array
    conv_w: jax.Array
    conv_b: jax.Array


def _mod(mods, k):
    return mods[:, k:k + 1, :]


def out_view(l, tb):
    rows = D_MODEL // N_CHIPS
    if tb:
        return BView(n=D_MODEL, k=D_MODEL, tn=rows, tk=D_MODEL, index_map=lambda i, j, kk: (j, l, kk))
    return BView(n=D_MODEL, k=D_MODEL, tn=1024, tk=rows, index_map=lambda i, j, kk: (kk, l, j))


def down_view(l, f, tb):
    rows = f // N_CHIPS
    if tb:
        return BView(n=f, k=D_MODEL, tn=rows, tk=D_MODEL, index_map=lambda i, j, kk: (j, l, kk))
    return BView(n=D_MODEL, k=f, tn=1024, tk=rows, index_map=lambda i, j, kk: (kk, l, j))


def up_view(l, f, part=None):
    cols = 2 * f // N_CHIPS
    tc = _pick(cols, (1408, 1024, 512, 256))
    nbc = cols // tc
    if part is None:
        return BView(n=2 * f, k=D_MODEL, tn=tc, tk=D_MODEL, index_map=lambda i, j, kk: (j // nbc, l, j % nbc))
    nnb = D_MODEL // 1024
    return BView(n=D_MODEL, k=f, tn=1024, tk=tc, index_map=lambda i, j, kk: (2 * part + kk // nbc, l * nnb + j, kk % nbc))


def ada_view(l, n_ada, tb):
    if tb:
        return BView(n=D_MODEL, k=n_ada, tn=1024, tk=n_ada, index_map=lambda i, j, kk: (l, j, 0))
    return BView(n=n_ada, k=D_MODEL, tn=1024, tk=D_MODEL, index_map=lambda i, j, kk: (l, 0, j))


def _prep_args(z, zg, cos, sin, w):
    rows = [Row(z, Z_AV, 0), Row(z, 512, Z_RQ // 512), Row(z, 512, Z_RK // 512), Row(z, 256, Z_GQ // 256),
            Row(zg, LANES, 0), Row(cos, HEAD_DIM, 0, False), Row(sin, HEAD_DIM, 0, False)]
    return rows, [Par(w.q_g), Par(w.k_g), Par(w.gate_up), Par(w.gate_b)]


def _post_args(o_att, o_ret, o_gla, z, w):
    rows = [Row(o_att, 1024), Row(o_ret, 512, 0), Row(o_ret, 512, 1, False), Row(o_gla, 512, 0), Row(o_gla, 512, 1, False),
            Row(z, 512, Z_RG // 512), Row(z, 512, Z_GR // 512)]
    return rows, [Par(w.ret_g), Par(w.gla_g)]


def layer_fwd(l, xs, mods, w, cos, sin, n_lat):
    t, d = xs.shape
    tag = f"l{l}_"
    nm1 = [Par(w.norm1_g), Par(_mod(mods, 0), True), Par(_mod(mods, 1), True)]
    (h,) = row_map(tag + "norm1", normmod_tile, [Row(xs, d)], nm1, [(d, BF16)], t, n_lat)
    z = matmul(tag + "in_proj", h, w.w_main)
    zg = matmul(tag + "gate_proj", h, w.w_gate)
    rows, pars = _prep_args(z, zg, cos, sin, w)
    (p,) = row_map(tag + "prep", prep_tile, rows, pars, [(P_W, F32)], t, n_lat)
    o_att, lse = attn_fwd(p, z, n_lat)
    o_ret, s_ret = ret_fwd(p, z, w.lg, n_lat)
    o_gla, s_gla = gla_fwd(p, z, n_lat)
    rows, pars = _post_args(o_att, o_ret, o_gla, z, w)
    (m,) = row_map(tag + "post", post_tile, rows, pars, [(d, BF16)], t, n_lat)
    y = matmul(tag + "out_proj", m, w.g_out, view=out_view(l, False))
    (x1,) = row_map(tag + "resid1", resid_tile, [Row(xs, d), Row(y, d)], [Par(_mod(mods, 2), True)], [(d, F32)], t, n_lat)
    nm2 = [Par(w.norm2_g), Par(_mod(mods, 3), True), Par(_mod(mods, 4), True)]
    (h2,) = row_map(tag + "norm2", normmod_tile, [Row(x1, d)], nm2, [(d, BF16)], t, n_lat)
    f = w.conv_b.shape[1]
    u = matmul(tag + "up_proj", h2, w.g_up, view=up_view(l, f))
    g = convglu(tag + "convglu", u, w.conv_w, w.conv_b, n_lat)
    yd = matmul(tag + "down_proj", g, w.g_down, view=down_view(l, f, False))
    (x2,) = row_map(tag + "resid2", resid_tile, [Row(x1, d), Row(yd, d)], [Par(_mod(mods, 5), True)], [(d, F32)], t, n_lat)
    saved = dict(xs=xs, h=h, z=z, zg=zg, p=p, o_att=o_att, lse=lse, o_ret=o_ret, s_ret=s_ret, o_gla=o_gla, s_gla=s_gla,
                 m=m, y=y, x1=x1, h2=h2, u=u, g=g, yd=yd)
    return x2, saved


def _sum_dirs(a):
    w = a.shape[1] // 2
    return a[:, :w] + a[:, w:]


def layer_bwd(l, dx2, s, mods, w, cos, sin, n_lat):
    t, d = dx2.shape
    tag = f"l{l}_b_"
    dyd, dgate5 = row_vjp(tag + "resid2", resid_tile, [Row(s["x1"], d, 0, False), Row(s["yd"], d)],
                          [Par(_mod(mods, 5), True)], [dx2], t, n_lat, row_grad_dtype=BF16)
    f = w.conv_b.shape[1]
    dg = matmul(tag + "down_dx", dyd, w.g_down, tb=True, view=down_view(l, f, True))
    dw_down = matmul(tag + "down_dw", s["g"], dyd, ta=True, out_dtype=BF16)
    da, dv, dcw, dcb = convglu_bwd(tag + "convglu", s["u"], w.conv_w, w.conv_b, dg, n_lat)
    dh2 = matmul(tag + "up_dx_gate", da, w.g_up, tb=True, view=up_view(l, f, 0))
    dh2 = matmul(tag + "up_dx_value", dv, w.g_up, tb=True, view=up_view(l, f, 1), add=dh2)
    dw_up = (matmul(tag + "up_dw_gate", s["h2"], da, ta=True, out_dtype=BF16),
             matmul(tag + "up_dw_value", s["h2"], dv, ta=True, out_dtype=BF16))
    nm2 = [Par(w.norm2_g), Par(_mod(mods, 3), True), Par(_mod(mods, 4), True)]
    dx1, dg2, dshift3, dscale4 = row_vjp(tag + "norm2", normmod_tile, [Row(s["x1"], d)], nm2, [dh2], t, n_lat, add_to_first=dx2)
    dy, dgate2 = row_vjp(tag + "resid1", resid_tile, [Row(s["xs"], d, 0, False), Row(s["y"], d)],
                         [Par(_mod(mods, 2), True)], [dx1], t, n_lat, row_grad_dtype=BF16)
    dm = matmul(tag + "out_dx", dy, w.g_out, tb=True, view=out_view(l, True))
    dw_out = matmul(tag + "out_dw", s["m"], dy, ta=True, out_dtype=BF16)
    rows, pars = _post_args(s["o_att"], s["o_ret"], s["o_gla"], s["z"], w)
    do_att, do_ret, do_gla, d_rg, d_gr, d_ret_g, d_gla_g = row_vjp(tag + "post", post_tile, rows, pars, [dm], t, n_lat)
    dq_a, dk_a, dv_a = attn_bwd(s["p"], s["z"], s["o_att"], s["lse"], do_att, n_lat)
    dq_r, dk_r, dv_r, dlg = ret_bwd(s["p"], s["z"], w.lg, s["s_ret"], do_ret, n_lat)
    dq_g, dk_g, dv_g, dla = gla_bwd(s["p"], s["z"], s["s_gla"], do_gla, n_lat)
    dp = jnp.concatenate([dq_a, dk_a, _sum_dirs(dq_g), _sum_dirs(dq_r), _sum_dirs(dk_r), dla], axis=1)
    rows, pars = _prep_args(s["z"], s["zg"], cos, sin, w)
    d_zqk, d_zrq, d_zrk, d_zgq, dzg, d_qg, d_kg, d_up, d_gb = row_vjp(tag + "prep", prep_tile, rows, pars, [dp], t, n_lat)
    dz = jnp.concatenate([d_zqk, dv_a, d_zrq, d_zrk, _sum_dirs(dv_r), d_rg, d_zgq, _sum_dirs(dk_g), _sum_dirs(dv_g), d_gr], axis=1)
    dz, dzg = dz.astype(BF16), dzg.astype(BF16)
    dh_gate = matmul(tag + "gate_dx", dzg, w.w_gate, tb=True)
    dh = matmul(tag + "in_dx", dz, w.w_main, tb=True, add=dh_gate)
    dw_main = matmul(tag + "in_dw", s["h"], dz, ta=True, out_dtype=BF16)
    dw_gate = matmul(tag + "gate_dw", s["h"], dzg, ta=True, out_dtype=BF16)
    nm1 = [Par(w.norm1_g), Par(_mod(mods, 0), True), Par(_mod(mods, 1), True)]
    dx, dg1, dshift0, dscale1 = row_vjp(tag + "norm1", normmod_tile, [Row(s["xs"], d)], nm1, [dh], t, n_lat, add_to_first=dx1)
    dmods = jnp.concatenate([dshift0, dscale1, dgate2, dshift3, dscale4, dgate5], axis=1)
    grads = dict(w_main=dw_main, w_gate=dw_gate, w_out=dw_out, w_up=dw_up, w_down=dw_down, norm1_g=dg1, q_g=d_qg, k_g=d_kg,
                 lg=dlg, ret_g=d_ret_g, gate_up=d_up, gate_b=d_gb, gla_g=d_gla_g, norm2_g=dg2, conv_w=dcw, conv_b=dcb)
    return dx, dmods, grads


def rope_tables(n_lat, n_ctx):
    rows = n_lat // GRID_W
    row = jnp.repeat(jnp.arange(rows, dtype=F32), GRID_W)
    col = jnp.tile(jnp.arange(GRID_W, dtype=F32), rows)
    n_freq = HEAD_DIM // 4
    inv_freq = ROPE_THETA ** (-jnp.arange(n_freq, dtype=F32) / n_freq)
    ang = jnp.concatenate([row[:, None] * inv_freq, col[:, None] * inv_freq], axis=-1)
    cos, sin = jnp.cos(ang), jnp.sin(ang)
    cos = jnp.concatenate([jnp.concatenate([cos, cos], axis=1), jnp.ones((n_ctx, HEAD_DIM), F32)], axis=0)
    sin = jnp.concatenate([jnp.concatenate([-sin, sin], axis=1), jnp.zeros((n_ctx, HEAD_DIM), F32)], axis=0)
    return cos, sin


def local_step(xs, target, mods, weights, final_g, n_lat):
    t, d = xs.shape
    cos, sin = rope_tables(n_lat, t - n_lat)
    saved = []
    h = xs
    for l, w in enumerate(weights):
        h, s = layer_fwd(l, h, mods[l], w, cos, sin, n_lat)
        saved.append(s)
    loss, dlat, dgf = final_loss(h, target, final_g, n_lat)
    dx = jnp.concatenate([dlat, jnp.zeros((t - n_lat, d), F32)], axis=0)
    dmods, grads = [None] * len(weights), [None] * len(weights)
    for l in reversed(range(len(weights))):
        dx, dmods[l], grads[l] = layer_bwd(l, dx, saved[l], mods[l], weights[l], cos, sin, n_lat)
    return loss, dx, dmods, grads, dgf


WEIGHT_NAMES = ("c_ctx", "ada_w", "ada_b", "norm1_g", "w_in", "q_norm_g", "k_norm_g", "ret_log_decay", "ret_norm_g",
                "gla_gate_up", "gla_gate_b", "gla_norm_g", "w_out", "norm2_g", "w_up", "conv_w", "conv_b", "w_down", "final_norm_g")
PACK_QUANTUM = 8 * LANES


def _pack(arrays):
    flat = jnp.concatenate([a.reshape(-1).astype(F32) for a in arrays])
    n = -(-flat.shape[0] // PACK_QUANTUM) * PACK_QUANTUM
    return jnp.pad(flat, (0, n - flat.shape[0])).reshape(8, n // 8)


def _unpack(flat2d, shapes):
    out, at = [], 0
    for s in shapes:
        size = int(np.prod(s))
        out.append(flat2d[:, at:at + size].reshape((flat2d.shape[0],) + tuple(s)))
        at += size
    return out


def _per_device(gathered):
    return gathered.reshape(N_DEV, -1)


def _from_chips(per_device, axis):
    chips = per_device[0::2]
    moved = jnp.moveaxis(chips, 0, axis)
    shape = moved.shape
    return moved.reshape(shape[:axis] + (shape[axis] * shape[axis + 1],) + shape[axis + 2:])


def kernel(x, c, ctx, c_ctx, ada_w, ada_b, norm1_g, w_in, q_norm_g, k_norm_g, ret_log_decay, ret_norm_g, gla_gate_up, gla_gate_b, gla_norm_g, w_out, norm2_g, w_up, conv_w, conv_b, w_down, final_norm_g, loss_target, m_c_ctx, m_ada_w, m_ada_b, m_norm1_g, m_w_in, m_q_norm_g, m_k_norm_g, m_ret_log_decay, m_ret_norm_g, m_gla_gate_up, m_gla_gate_b, m_gla_norm_g, m_w_out, m_norm2_g, m_w_up, m_conv_w, m_conv_b, m_w_down, m_final_norm_g, v_c_ctx, v_ada_w, v_ada_b, v_norm1_g, v_w_in, v_q_norm_g, v_k_norm_g, v_ret_log_decay, v_ret_norm_g, v_gla_gate_up, v_gla_gate_b, v_gla_norm_g, v_w_out, v_norm2_g, v_w_up, v_conv_w, v_conv_b, v_w_down, v_final_norm_g):
    weights = dict(zip(WEIGHT_NAMES, (c_ctx, ada_w, ada_b, norm1_g, w_in, q_norm_g, k_norm_g, ret_log_decay, ret_norm_g,
                                      gla_gate_up, gla_gate_b, gla_norm_g, w_out, norm2_g, w_up, conv_w, conv_b, w_down, final_norm_g)))
    mom_m = dict(zip(WEIGHT_NAMES, (m_c_ctx, m_ada_w, m_ada_b, m_norm1_g, m_w_in, m_q_norm_g, m_k_norm_g, m_ret_log_decay, m_ret_norm_g,
                                    m_gla_gate_up, m_gla_gate_b, m_gla_norm_g, m_w_out, m_norm2_g, m_w_up, m_conv_w, m_conv_b, m_w_down, m_final_norm_g)))
    mom_v = dict(zip(WEIGHT_NAMES, (v_c_ctx, v_ada_w, v_ada_b, v_norm1_g, v_w_in, v_q_norm_g, v_k_norm_g, v_ret_log_decay, v_ret_norm_g,
                                    v_gla_gate_up, v_gla_gate_b, v_gla_norm_g, v_w_out, v_norm2_g, v_w_up, v_conv_w, v_conv_b, v_w_down, v_final_norm_g)))
    depth, d = norm1_g.shape
    assert d == D_MODEL and x.shape[0] == 1
    n_lat, n_ctx, f = x.shape[1], ctx.shape[1], conv_b.shape[1]
    assert n_lat % ROW_TILE == 0 and n_ctx % ROW_TILE == 0 and f % FFN_COL_TILE == 0 and f % N_CHIPS == 0
    n_in = w_in.shape[2]
    n_ada = ada_w.shape[2]
    xi, yi, ci = lax.axis_index("x"), lax.axis_index("y"), lax.axis_index("c")
    chip = 2 * xi + yi
    dev = 2 * chip + ci

    shards = [w_in.astype(BF16).reshape(depth * d, n_in), w_out.astype(BF16).reshape(-1, d),
              w_up.astype(BF16).reshape(depth * d, -1), w_down.astype(BF16).reshape(-1, d)]
    gathered_w = all_gather_shards("gather_weights", shards)
    g_in, g_out, g_up, g_down = [lax.dynamic_update_slice_in_dim(g, s[None], chip, axis=0) for g, s in zip(gathered_w, shards)]
    w_in_full = [jnp.concatenate([g_in[q, l * d:(l + 1) * d] for q in range(N_CHIPS)], axis=1) for l in range(depth)]
    w_main = [w[:, :N_MAIN] for w in w_in_full]
    w_gate = [jnp.pad(w[:, N_MAIN:], ((0, 0), (0, LANES - N_GATE))) for w in w_in_full]

    small_shapes = [c.shape[1:], conv_w.shape, gla_gate_up.shape, gla_gate_b.shape]
    got = _per_device(all_gather_small("gather_small", _pack([c, conv_w, gla_gate_up, gla_gate_b])))
    c_all, conv_w_sh, gate_up_sh, gate_b_sh = _unpack(got, small_shapes)
    conv_w_full = _from_chips(conv_w_sh, 2)
    gate_up_full = _from_chips(gate_up_sh, 3)
    gate_b_full = _from_chips(gate_b_sh, 2)

    act = jnp.zeros((16, d), F32).at[0:N_DEV].set(jax.nn.silu(c_all)).at[N_DEV].set(jax.nn.silu(c_ctx))
    mod_sh = jnp.stack([matmul(f"ada_fwd{l}", act, ada_w, view=ada_view(l, n_ada, False)) for l in range(depth)])
    got = _per_device(all_gather_small("gather_mods", _pack([mod_sh])))
    (mod_sh_all,) = _unpack(got, [mod_sh.shape])
    mod_full = _from_chips(mod_sh_all, 2) + ada_b[:, None, :]
    mod_mine = lax.dynamic_index_in_dim(mod_full, dev, axis=1, keepdims=False)
    mods = [jnp.stack([mod_mine[l].reshape(N_MOD, d), mod_full[l, N_DEV].reshape(N_MOD, d)]) for l in range(depth)]

    layer_w = []
    for l in range(depth):
        up = jnp.zeros((2, LANES, GLA_HEADS * GLA_DK), F32)
        up = up.at[0, 0:GLA_RANK].set(gate_up_full[l, 0]).at[1, GLA_RANK:2 * GLA_RANK].set(gate_up_full[l, 1])
        layer_w.append(LayerWeights(
            w_main=w_main[l], w_gate=w_gate[l], g_out=g_out, g_up=g_up, g_down=g_down,
            norm1_g=norm1_g[l].reshape(1, 1, d), q_g=q_norm_g[l].reshape(1, 1, HEAD_DIM), k_g=k_norm_g[l].reshape(1, 1, HEAD_DIM),
            lg=ret_log_decay[l].reshape(2, RET_HEADS, 1, 1), ret_g=ret_norm_g[l].reshape(1, 1, HEAD_DIM),
            gate_up=up.reshape(1, 2 * LANES, -1), gate_b=gate_b_full[l].reshape(1, 2, -1), gla_g=gla_norm_g[l].reshape(1, 1, HEAD_DIM),
            norm2_g=norm2_g[l].reshape(1, 1, d), conv_w=conv_w_full[l], conv_b=conv_b[l].reshape(1, f)))
    xs = jnp.concatenate([x[0], ctx[0]], axis=0)
    loss, dx, dmods, grads, dgf = local_step(xs, loss_target[0], mods, layer_w, final_norm_g.reshape(1, d), n_lat)

    def gate_up_grad(g):
        return jnp.stack([g[0, 0:GLA_RANK], g[0, LANES + GLA_RANK:LANES + 2 * GLA_RANK]])

    per_layer = [[dmods[l][0], dmods[l][1], grads[l]["norm1_g"], grads[l]["norm2_g"], grads[l]["q_g"], grads[l]["k_g"],
                  grads[l]["ret_g"], grads[l]["gla_g"], grads[l]["lg"], gate_up_grad(grads[l]["gate_up"]), grads[l]["gate_b"],
                  grads[l]["conv_w"], grads[l]["conv_b"]] for l in range(depth)]
    layer_shapes = [(N_MOD * d,), (N_MOD * d,), (d,), (d,), (HEAD_DIM,), (HEAD_DIM,), (HEAD_DIM,), (HEAD_DIM,), (2, RET_HEADS),
                    (2, GLA_RANK, GLA_HEADS * GLA_DK), (2, GLA_HEADS * GLA_DK), (3, f), (f,)]
    packed = _pack([a for lay in per_layer for a in lay] + [dgf, loss[0, 0:1]])
    gathered = all_gather_small("gather_small_grads", packed)
    every = _unpack(_per_device(gathered), layer_shapes * depth + [(d,), (1,)])
    total = _unpack(sum_device_blocks("sum_small_grads", gathered).reshape(1, -1), layer_shapes * depth + [(d,), (1,)])
    nl = len(layer_shapes)

    def tot(l, k):
        return total[l * nl + k][0]

    out = {"norm1_g": jnp.stack([tot(l, 2) for l in range(depth)]), "norm2_g": jnp.stack([tot(l, 3) for l in range(depth)]),
           "q_norm_g": jnp.stack([tot(l, 4) for l in range(depth)]), "k_norm_g": jnp.stack([tot(l, 5) for l in range(depth)]),
           "ret_norm_g": jnp.stack([tot(l, 6) for l in range(depth)]), "gla_norm_g": jnp.stack([tot(l, 7) for l in range(depth)]),
           "ret_log_decay": jnp.stack([tot(l, 8) for l in range(depth)]),
           "gla_gate_up": lax.dynamic_slice_in_dim(jnp.stack([tot(l, 9) for l in range(depth)]), chip * gla_gate_up.shape[3], gla_gate_up.shape[3], axis=3),
           "gla_gate_b": lax.dynamic_slice_in_dim(jnp.stack([tot(l, 10) for l in range(depth)]), chip * gla_gate_b.shape[2], gla_gate_b.shape[2], axis=2),
           "conv_w": lax.dynamic_slice_in_dim(jnp.stack([tot(l, 11) for l in range(depth)]), chip * conv_w.shape[2], conv_w.shape[2], axis=2),
           "conv_b": jnp.stack([tot(l, 12) for l in range(depth)]),
           "final_norm_g": total[depth * nl][0],
           "ada_b": jnp.stack([tot(l, 0) + tot(l, 1) for l in range(depth)])}
    loss_total = total[depth * nl + 1][0, 0]

    dmod_all = jnp.zeros((depth, 16, N_MOD * d), F32)
    for l in range(depth):
        dmod_all = dmod_all.at[l, 0:N_DEV].set(every[l * nl][:, :]).at[l, N_DEV].set(tot(l, 1))
    dmod_cols = lax.dynamic_slice_in_dim(dmod_all, chip * n_ada, n_ada, axis=2)
    out["ada_w"] = jnp.stack([matmul(f"ada_dw{l}", act, dmod_cols[l], ta=True) for l in range(depth)])
    dact = matmul("ada_dx0", dmod_cols[0], ada_w, tb=True, view=ada_view(0, n_ada, True))
    for l in range(1, depth):
        dact = matmul(f"ada_dx{l}", dmod_cols[l], ada_w, tb=True, view=ada_view(l, n_ada, True), add=dact)
    got = _per_device(all_gather_small("gather_dcctx", _pack([dact[N_DEV]])))[0::2, :d]
    dsilu = got[0] + got[1] + got[2] + got[3]
    sig = jax.nn.sigmoid(c_ctx)
    out["c_ctx"] = dsilu * (sig + c_ctx * sig * (1.0 - sig))

    def in_pieces(g):
        full_cols = jnp.concatenate([g["w_main"], g["w_gate"][:, :N_GATE]], axis=1)
        return jnp.stack([full_cols[:, q * n_in:(q + 1) * n_in] for q in range(N_CHIPS)])

    def up_pieces(g):
        gate, value = g["w_up"]
        return jnp.stack([gate[:, :f // 2], gate[:, f // 2:], value[:, :f // 2], value[:, f // 2:]])

    pieces = [jnp.concatenate([in_pieces(grads[l]) for l in range(depth)], axis=1),
              jnp.concatenate([grads[l]["w_out"].reshape(N_CHIPS, d // N_CHIPS, d) for l in range(depth)], axis=1),
              jnp.concatenate([up_pieces(grads[l]) for l in range(depth)], axis=1),
              jnp.concatenate([grads[l]["w_down"].reshape(N_CHIPS, f // N_CHIPS, d) for l in range(depth)], axis=1)]
    from_sibling = exchange_sibling_halves("rs_sibling", pieces)
    chip_sums = [add_sibling_half(f"rs_add_sibling{k}", pc, sib, ci) for k, (pc, sib) in enumerate(zip(pieces, from_sibling))]
    from_chips = scatter_to_chips("rs_chips", chip_sums)
    halves = [add_chip_sums(f"rs_add_chips{k}", cs, got, chip, ci) for k, (cs, got) in enumerate(zip(chip_sums, from_chips))]
    full = join_sibling_halves("rs_join", halves)
    out["w_in"] = full[0].reshape(w_in.shape)
    out["w_out"] = full[1].reshape(w_out.shape)
    out["w_up"] = full[2].reshape(w_up.shape)
    out["w_down"] = full[3].reshape(w_down.shape)

    deltas, new_m, new_v = {}, {}, {}
    for name in WEIGHT_NAMES:
        out[name] = out[name].reshape(weights[name].shape)
        deltas[name], new_m[name], new_v[name] = adamw("adamw_" + name, weights[name], out[name], mom_m[name], mom_v[name])
    grad_x = dx[:n_lat].reshape(x.shape)
    return (loss_total, grad_x, *[out[n] for n in WEIGHT_NAMES], *[deltas[n] for n in WEIGHT_NAMES],
            *[new_m[n] for n in WEIGHT_NAMES], *[new_v[n] for n in WEIGHT_NAMES])
```

```python
import functools
from typing import NamedTuple

import numpy as np
import jax
import jax.numpy as jnp
from jax import lax
from jax.experimental import pallas as pl
from jax.experimental.pallas import tpu as pltpu

F32 = jnp.float32
BF16 = jnp.bfloat16

D_MODEL = 2048
HEAD_DIM = 128
ATT_Q_HEADS = 8
ATT_KV_HEADS = 2
ATT_GROUP = ATT_Q_HEADS // ATT_KV_HEADS
RET_HEADS = 4
GLA_HEADS = 4
GLA_DK = 64
GLA_DV = 128
GLA_RANK = 16
GLA_TAU = 16.0
RET_CHUNK = 128
GLA_CHUNK = 64
GRID_W = 64
ROPE_THETA = 10000.0
N_MOD = 6
EPS = 1e-6
N_MAIN = 5120
N_GATE = 2 * GLA_RANK
LANES = 128
ROW_TILE = 256
FFN_COL_TILE = 256
VMEM_LIMIT = 56 * 1024 * 1024

ADAM_LR = 0.001
ADAM_B1 = 0.9
ADAM_B2 = 0.999
ADAM_EPS = 1e-08
ADAM_WD = 0.01
ADAM_STEP = 10

Z_AQ, Z_AK, Z_AV = 0, 1024, 1280
Z_RQ, Z_RK, Z_RV, Z_RG = 1536, 2048, 2560, 3072
Z_GQ, Z_GK, Z_GV, Z_GR = 3584, 3840, 4096, 4608
P_AQ, P_AK, P_GQ, P_RQ, P_RK, P_LA = 0, 1024, 1280, 1536, 2048, 2560
P_W = 3072


def _params(sem=None):
    return pltpu.CompilerParams(dimension_semantics=sem, vmem_limit_bytes=VMEM_LIMIT)


def _pick(n, cands):
    for c in cands:
        if n % c == 0:
            return c
    return n


_NN = (((1,), (0,)), ((), ()))
_NT = (((1,), (1,)), ((), ()))
_TN = (((0,), (0,)), ((), ()))


def _dg(a, b, dims):
    return lax.dot_general(a.astype(BF16), b.astype(BF16), dims, preferred_element_type=F32)


@jax.custom_vjp
def bdot(a, b):
    return _dg(a, b, _NN)


def _bdot_fwd(a, b):
    return _dg(a, b, _NN), (a, b)


def _bdot_bwd(res, ct):
    a, b = res
    return _dg(ct, b, _NT), _dg(a, ct, _TN)


bdot.defvjp(_bdot_fwd, _bdot_bwd)


@jax.custom_vjp
def bdot_nt(a, b):
    return _dg(a, b, _NT)


def _bdot_nt_fwd(a, b):
    return _dg(a, b, _NT), (a, b)


def _bdot_nt_bwd(res, ct):
    a, b = res
    return _dg(ct, b, _NN), _dg(ct, a, _TN)


bdot_nt.defvjp(_bdot_nt_fwd, _bdot_nt_bwd)


@jax.custom_vjp
def bdot_tn(a, b):
    return _dg(a, b, _TN)


def _bdot_tn_fwd(a, b):
    return _dg(a, b, _TN), (a, b)


def _bdot_tn_bwd(res, ct):
    a, b = res
    return _dg(b, ct, _NT), _dg(a, ct, _NN)


bdot_tn.defvjp(_bdot_tn_fwd, _bdot_tn_bwd)


def _split3(x):
    x1 = x.astype(BF16)
    r1 = x - x1.astype(F32)
    x2 = r1.astype(BF16)
    x3 = (r1 - x2.astype(F32)).astype(BF16)
    return x1, x2, x3


def _mask_dot(mask_bf16, x, dims):
    x1, x2, x3 = _split3(x)
    f = lambda t: lax.dot_general(mask_bf16, t, dims, preferred_element_type=F32)
    return f(x1) + f(x2) + f(x3)


@jax.custom_vjp
def mask_cumsum(mask, x):
    return _mask_dot(mask.astype(BF16), x, _NN)


def _mask_cumsum_fwd(mask, x):
    return mask_cumsum(mask, x), mask


def _mask_cumsum_bwd(mask, ct):
    return jnp.zeros_like(mask), _mask_dot(mask.astype(BF16), ct, _TN)


mask_cumsum.defvjp(_mask_cumsum_fwd, _mask_cumsum_bwd)


def _roll(x, shift, axis):
    return pltpu.roll(x, shift % x.shape[axis], axis)


@functools.partial(jax.custom_vjp, nondiff_argnums=(1, 2))
def roll(x, shift, axis):
    return _roll(x, shift, axis)


def _roll_fwd(x, shift, axis):
    return _roll(x, shift, axis), None


def _roll_bwd(shift, axis, _, ct):
    return (_roll(ct, -shift, axis),)


roll.defvjp(_roll_fwd, _roll_bwd)


def rms(x):
    return x * lax.rsqrt(jnp.mean(x * x, axis=-1, keepdims=True) + EPS)


def silu(x):
    return x * (1.0 / (1.0 + jnp.exp(-x)))


def log_sigmoid(x):
    return jnp.minimum(x, 0.0) - jnp.log(1.0 + jnp.exp(-jnp.abs(x)))


def rope(t, cos, sin):
    return t * cos + roll(t, HEAD_DIM // 2, 1) * sin


def _heads(x, n, width=HEAD_DIM):
    return [x[:, h * width:(h + 1) * width] for h in range(n)]


class Row(NamedTuple):
    arr: jax.Array
    width: int
    idx: int = 0
    diff: bool = True


class Par(NamedTuple):
    arr: jax.Array
    grouped: bool = False
    diff: bool = True


def _row_specs(rows, pars, tm, n_lat_tiles):
    def grp(i):
        return jnp.minimum(i // n_lat_tiles, 1)

    specs = [pl.BlockSpec((tm, r.width), functools.partial(lambda i, k: (i, k), k=r.idx)) for r in rows]
    for p in pars:
        blk = (1,) + p.arr.shape[1:]
        if p.grouped:
            specs.append(pl.BlockSpec(blk, lambda i: (grp(i), 0, 0)))
        else:
            specs.append(pl.BlockSpec(blk, lambda i: (0, 0, 0)))
    return specs


def row_map(name, fn, rows, pars, outs, n_rows, n_lat):
    tm = ROW_TILE
    nr, npar = len(rows), len(pars)

    def body(*refs):
        vals = [r[...] for r in refs[:nr]] + [p[0] for p in refs[nr:nr + npar]]
        res = fn(*vals)
        for o, v in zip(refs[nr + npar:], res):
            o[...] = v.astype(o.dtype)

    return pl.pallas_call(
        body, name=name, grid=(n_rows // tm,),
        in_specs=_row_specs(rows, pars, tm, n_lat // tm),
        out_specs=[pl.BlockSpec((tm, w), lambda i: (i, 0)) for w, _ in outs],
        out_shape=[jax.ShapeDtypeStruct((n_rows, w), dt) for w, dt in outs],
        compiler_params=_params(("arbitrary",)),
    )(*[r.arr for r in rows], *[p.arr for p in pars])


def row_vjp(name, fn, rows, pars, cts, n_rows, n_lat, add_to_first=None, row_grad_dtype=F32):
    tm = ROW_TILE
    nr, npar, nc = len(rows), len(pars), len(cts)
    n_lat_tiles = n_lat // tm
    args = list(rows) + list(pars)
    diff_pos = [k for k, a in enumerate(args) if a.diff]
    n_add = 0 if add_to_first is None else 1

    def body(*refs):
        i = pl.program_id(0)
        vals = [r[...] for r in refs[:nr]] + [p[0] for p in refs[nr:nr + npar]]
        ct_vals = tuple(c[...] for c in refs[nr + npar:nr + npar + nc])
        out_refs = refs[nr + npar + nc + n_add:]

        def g(*dv):
            full = list(vals)
            for k, v in zip(diff_pos, dv):
                full[k] = v
            return tuple(fn(*full))

        _, vjp = jax.vjp(g, *[vals[k] for k in diff_pos])
        grads = vjp(ct_vals)
        for n, (k, o, gr) in enumerate(zip(diff_pos, out_refs, grads)):
            if k < nr:
                o[...] = (gr + refs[nr + npar + nc][...] if (n == 0 and n_add) else gr).astype(o.dtype)
            else:
                first = (i == 0) | (i == n_lat_tiles) if args[k].grouped else (i == 0)

                @pl.when(first)
                def _():
                    o[0] = gr

                @pl.when(jnp.logical_not(first))
                def _():
                    o[0] += gr

    def grp(i):
        return jnp.minimum(i // n_lat_tiles, 1)

    out_specs, out_shape = [], []
    for k in diff_pos:
        a = args[k]
        if k < nr:
            out_specs.append(pl.BlockSpec((tm, a.width), lambda i: (i, 0)))
            out_shape.append(jax.ShapeDtypeStruct((n_rows, a.width), row_grad_dtype))
        else:
            blk = (1,) + a.arr.shape[1:]
            out_specs.append(pl.BlockSpec(blk, (lambda i: (grp(i), 0, 0)) if a.grouped else (lambda i: (0, 0, 0))))
            out_shape.append(jax.ShapeDtypeStruct(a.arr.shape, F32))
    extra = list(cts) + ([add_to_first] if n_add else [])
    ct_specs = [pl.BlockSpec((tm, c.shape[1]), lambda i: (i, 0)) for c in extra]
    return pl.pallas_call(
        body, name=name, grid=(n_rows // tm,),
        in_specs=_row_specs(rows, pars, tm, n_lat_tiles) + ct_specs,
        out_specs=out_specs, out_shape=out_shape,
        compiler_params=_params(("arbitrary",)),
    )(*[r.arr for r in rows], *[p.arr for p in pars], *extra)


class BView(NamedTuple):
    n: int
    k: int
    tn: int
    tk: int
    index_map: object
    lead: int = 1


MATMUL_VMEM_BUDGET = 40 * 1024 * 1024


def _matmul_tiles(m, n, k, a_bytes, b_bytes, o_bytes):
    tms = [c for c in (1152, 1024, 768, 512, 256, 128) if m % c == 0] or [m]
    tns = [c for c in (2048, 1408, 1280, 1024, 768, 512, 256, 128) if n % c == 0] or [n]
    tks = [k] + [c for c in (2816, 2304, 2048, 1408, 1024, 512, 256, 128) if k % c == 0 and c < k]
    for tk in tks:
        fits = [(tm * tn, tm, tn) for tm in tms for tn in tns
                if 2 * (tm * tk * a_bytes + tk * tn * b_bytes + tm * tn * o_bytes) + 2 * tm * tn * 4 <= MATMUL_VMEM_BUDGET]
        if fits and (max(fits)[0] >= 512 * 512 or tms == [m] or tk == tks[-1]):
            _, tm, tn = max(fits)
            return tm, tn, tk
    raise ValueError(f"no matmul tiling for {(m, n, k)}")


def matmul(name, a, b, *, ta=False, tb=False, add=None, out_dtype=F32, view=None):
    m = a.shape[1] if ta else a.shape[0]
    o_bytes = jnp.dtype(out_dtype).itemsize * (1 if add is None else 2)
    if view is None:
        k = a.shape[0] if ta else a.shape[1]
        n = b.shape[0] if tb else b.shape[1]
        assert (b.shape[1] if tb else b.shape[0]) == k, (a.shape, b.shape, ta, tb)
        tm, tn, tk = _matmul_tiles(m, n, k, a.dtype.itemsize, b.dtype.itemsize, o_bytes)
    else:
        n, k, tn, tk = view.n, view.k, view.tn, view.tk
        tm, _, _ = _matmul_tiles(m, tn, tk, a.dtype.itemsize, b.dtype.itemsize, o_bytes)
    nk = k // tk
    dims = (((0 if ta else 1,), (1 if tb else 0,)), ((), ()))

    def body(a_ref, b_ref, *rest):
        prod = lax.dot_general(a_ref[...].astype(BF16), b_ref[...].astype(BF16), dims, preferred_element_type=F32)
        if nk == 1:
            o_ref = rest[-1]
            o_ref[...] = (prod if add is None else prod + rest[0][...]).astype(o_ref.dtype)
            return
        o_ref, acc = rest[-2:]
        kk = pl.program_id(2)

        @pl.when(kk == 0)
        def _():
            acc[...] = prod

        @pl.when(kk != 0)
        def _():
            acc[...] += prod

        @pl.when(kk == nk - 1)
        def _():
            r = acc[...]
            if add is not None:
                r = r + rest[0][...]
            o_ref[...] = r.astype(o_ref.dtype)

    if ta:
        a_spec = pl.BlockSpec((tk, tm), lambda i, j, kk: (kk, i))
    else:
        a_spec = pl.BlockSpec((tm, tk), lambda i, j, kk: (i, kk))
    b_tile = (tn, tk) if tb else (tk, tn)
    if view is not None:
        b_spec = pl.BlockSpec((None,) * view.lead + b_tile, view.index_map)
    elif tb:
        b_spec = pl.BlockSpec(b_tile, lambda i, j, kk: (j, kk))
    else:
        b_spec = pl.BlockSpec(b_tile, lambda i, j, kk: (kk, j))
    o_spec = pl.BlockSpec((tm, tn), lambda i, j, kk: (i, j))
    ins = [a, b] + ([add] if add is not None else [])
    return pl.pallas_call(
        body, name=name, grid=(m // tm, n // tn, nk),
        in_specs=[a_spec, b_spec] + ([o_spec] if add is not None else []),
        out_specs=o_spec, out_shape=jax.ShapeDtypeStruct((m, n), out_dtype),
        scratch_shapes=[pltpu.VMEM((tm, tn), F32)] if nk > 1 else [],
        compiler_params=_params(("parallel", "parallel", "arbitrary")),
    )(*ins)


def normmod_tile(x, g, shift, scale):
    return (rms(x) * g * (1.0 + scale) + shift,)


def resid_tile(x, y, gate):
    return (x + gate * y,)


def prep_tile(z_qk, z_rq, z_rk, z_gq, zg, cos, sin, qg, kg, gate_up, gate_b):
    out = []
    for h, t in enumerate(_heads(z_qk, ATT_Q_HEADS + ATT_KV_HEADS)):
        out.append(rope(rms(t) * (qg if h < ATT_Q_HEADS else kg), cos, sin))
    gq = z_gq * (GLA_DK ** -0.5)
    rq = [rope(t, cos, sin) for t in _heads(z_rq, RET_HEADS)]
    rk = [rope(t * (HEAD_DIM ** -0.5), cos, sin) for t in _heads(z_rk, RET_HEADS)]
    la = [log_sigmoid(bdot(zg, gate_up[d * LANES:(d + 1) * LANES]) + gate_b[d:d + 1]) * (1.0 / GLA_TAU) for d in range(2)]
    return (jnp.concatenate(out + [gq] + rq + rk + la, axis=1),)


def post_tile(o_att, o_ret_f, o_ret_b, o_gla_f, o_gla_b, rg, gr, ret_g, gla_g):
    ret = jnp.concatenate([rms(t) * ret_g for t in _heads(o_ret_f + o_ret_b, RET_HEADS)], axis=1) * silu(rg)
    gla = jnp.concatenate([rms(t) * gla_g for t in _heads(o_gla_f + o_gla_b, GLA_HEADS)], axis=1) * silu(gr)
    return (jnp.concatenate([o_att, ret, gla], axis=1),)


def _convglu_tile(n_lat, a, v, cw, cb):
    t = a.shape[0]
    row = lax.broadcasted_iota(jnp.int32, (t, 1), 0)
    has_prev = ((row != 0) & (row != n_lat)).astype(F32)
    has_next = ((row != n_lat - 1) & (row != t - 1)).astype(F32)
    conv = roll(a, 1, 0) * has_prev * cw[0:1] + a * cw[1:2] + roll(a, -1, 0) * has_next * cw[2:3] + cb
    return silu(conv) * v


def convglu(name, u, cw, cb, n_lat):
    t, f2 = u.shape
    f, tc = f2 // 2, FFN_COL_TILE
    nb = f // tc

    def body(a_ref, v_ref, cw_ref, cb_ref, o_ref):
        o_ref[...] = _convglu_tile(n_lat, a_ref[...], v_ref[...], cw_ref[...], cb_ref[...]).astype(o_ref.dtype)

    return pl.pallas_call(
        body, name=name, grid=(nb,),
        in_specs=[pl.BlockSpec((t, tc), lambda j: (0, j)), pl.BlockSpec((t, tc), lambda j: (0, nb + j)),
                  pl.BlockSpec((3, tc), lambda j: (0, j)), pl.BlockSpec((1, tc), lambda j: (0, j))],
        out_specs=pl.BlockSpec((t, tc), lambda j: (0, j)),
        out_shape=jax.ShapeDtypeStruct((t, f), BF16),
        compiler_params=_params(("parallel",)),
    )(u, u, cw, cb)


def convglu_bwd(name, u, cw, cb, dg, n_lat):
    t, f2 = u.shape
    f, tc = f2 // 2, FFN_COL_TILE
    nb = f // tc

    def body(a_ref, v_ref, cw_ref, cb_ref, dg_ref, da_ref, dv_ref, dcw_ref, dcb_ref):
        _, vjp = jax.vjp(functools.partial(_convglu_tile, n_lat), a_ref[...], v_ref[...], cw_ref[...], cb_ref[...])
        da, dv, dcw_ref[...], dcb_ref[...] = vjp(dg_ref[...])
        da_ref[...], dv_ref[...] = da.astype(BF16), dv.astype(BF16)

    col = pl.BlockSpec((t, tc), lambda j: (0, j))
    return pl.pallas_call(
        body, name=name, grid=(nb,),
        in_specs=[col, pl.BlockSpec((t, tc), lambda j: (0, nb + j)), pl.BlockSpec((3, tc), lambda j: (0, j)),
                  pl.BlockSpec((1, tc), lambda j: (0, j)), col],
        out_specs=[col, col, pl.BlockSpec((3, tc), lambda j: (0, j)), pl.BlockSpec((1, tc), lambda j: (0, j))],
        out_shape=[jax.ShapeDtypeStruct((t, f), BF16), jax.ShapeDtypeStruct((t, f), BF16),
                   jax.ShapeDtypeStruct((3, f), F32), jax.ShapeDtypeStruct((1, f), F32)],
        compiler_params=_params(("parallel",)),
    )(u, u, cw, cb, dg)


def final_loss(x, target, g, n_lat):
    tm = ROW_TILE
    d = x.shape[1]

    def body(x_ref, t_ref, g_ref, loss_ref, dx_ref, dg_ref):
        i = pl.program_id(0)
        tgt = t_ref[...]

        def f(xv, gv):
            e = rms(xv) * gv - tgt
            s = jnp.sum(jnp.sum(e * e, axis=1, keepdims=True), axis=0, keepdims=True)
            return s * (0.5 / d)

        val, vjp = jax.vjp(f, x_ref[...], g_ref[...])
        dx, dgv = vjp(jnp.ones((1, 1), F32))
        dx_ref[...] = dx

        @pl.when(i == 0)
        def _():
            dg_ref[...] = dgv
            loss_ref[...] = jnp.broadcast_to(val, loss_ref.shape)

        @pl.when(i != 0)
        def _():
            dg_ref[...] += dgv
            loss_ref[...] += jnp.broadcast_to(val, loss_ref.shape)

    return pl.pallas_call(
        body, name="final_loss", grid=(n_lat // tm,),
        in_specs=[pl.BlockSpec((tm, d), lambda i: (i, 0)), pl.BlockSpec((tm, d), lambda i: (i, 0)),
                  pl.BlockSpec((1, d), lambda i: (0, 0))],
        out_specs=[pl.BlockSpec((1, LANES), lambda i: (0, 0)), pl.BlockSpec((tm, d), lambda i: (i, 0)),
                   pl.BlockSpec((1, d), lambda i: (0, 0))],
        out_shape=[jax.ShapeDtypeStruct((1, LANES), F32), jax.ShapeDtypeStruct((n_lat, d), F32),
                   jax.ShapeDtypeStruct((1, d), F32)],
        compiler_params=_params(("arbitrary",)),
    )(x, target, g)


ATT_SCALE = HEAD_DIM ** -0.5
_AK_BLK = P_AK // HEAD_DIM
_AV_BLK = Z_AV // HEAD_DIM


def _att_specs(t, tq):
    gw = ATT_GROUP * HEAD_DIM
    q_spec = pl.BlockSpec((tq, gw), lambda kv, i: (i, kv))
    k_spec = pl.BlockSpec((t, HEAD_DIM), lambda kv, i: (0, _AK_BLK + kv))
    v_spec = pl.BlockSpec((t, HEAD_DIM), lambda kv, i: (0, _AV_BLK + kv))
    row_spec = pl.BlockSpec((ATT_GROUP, tq, 1), lambda kv, i: (kv, i, 0))
    return q_spec, k_spec, v_spec, row_spec


def _att_mask(i, t, tq, n_lat):
    col = lax.broadcasted_iota(jnp.int32, (1, t), 1)
    return jnp.where((i >= n_lat // tq) & (col < n_lat), -jnp.inf, 0.0).astype(F32)


def attn_fwd(p, z, n_lat):
    t = p.shape[0]
    tq = ROW_TILE

    def body(q_ref, k_ref, v_ref, o_ref, lse_ref):
        mask = _att_mask(pl.program_id(1), t, tq, n_lat)
        k, v = k_ref[...].astype(BF16), v_ref[...].astype(BF16)
        for g in range(ATT_GROUP):
            cols = slice(g * HEAD_DIM, (g + 1) * HEAD_DIM)
            s = _dg(q_ref[:, cols], k, _NT) * ATT_SCALE + mask
            m = jnp.max(s, axis=1, keepdims=True)
            pr = jnp.exp(s - m)
            l = jnp.sum(pr, axis=1, keepdims=True)
            o_ref[:, cols] = _dg(pr, v, _NN) / l
            lse_ref[g] = m + jnp.log(l)

    q_spec, k_spec, v_spec, row_spec = _att_specs(t, tq)
    return pl.pallas_call(
        body, name="attn_fwd", grid=(ATT_KV_HEADS, t // tq),
        in_specs=[q_spec, k_spec, v_spec], out_specs=[q_spec, row_spec],
        out_shape=[jax.ShapeDtypeStruct((t, ATT_Q_HEADS * HEAD_DIM), F32),
                   jax.ShapeDtypeStruct((ATT_Q_HEADS, t, 1), F32)],
        compiler_params=_params(("parallel", "parallel")),
    )(p, p, z)


def attn_bwd(p, z, o, lse, do, n_lat):
    t = p.shape[0]
    tq = ROW_TILE

    def body(q_ref, k_ref, v_ref, o_ref, do_ref, lse_ref, dq_ref, dk_ref, dv_ref):
        i = pl.program_id(1)

        @pl.when(i == 0)
        def _():
            dk_ref[...] = jnp.zeros_like(dk_ref)
            dv_ref[...] = jnp.zeros_like(dv_ref)

        mask = _att_mask(i, t, tq, n_lat)
        k, v = k_ref[...].astype(BF16), v_ref[...].astype(BF16)
        dk, dv = dk_ref[...], dv_ref[...]
        for g in range(ATT_GROUP):
            cols = slice(g * HEAD_DIM, (g + 1) * HEAD_DIM)
            q, do_g = q_ref[:, cols].astype(BF16), do_ref[:, cols]
            pr = jnp.exp(_dg(q, k, _NT) * ATT_SCALE + mask - lse_ref[g])
            delta = jnp.sum(o_ref[:, cols] * do_g, axis=1, keepdims=True)
            ds = pr * (_dg(do_g, v, _NT) - delta) * ATT_SCALE
            dq_ref[:, cols] = _dg(ds, k, _NN)
            dk = dk + _dg(ds, q, _TN)
            dv = dv + _dg(pr, do_g, _TN)
        dk_ref[...], dv_ref[...] = dk, dv

    q_spec, k_spec, v_spec, row_spec = _att_specs(t, tq)
    kv_out = pl.BlockSpec((t, HEAD_DIM), lambda kv, i: (0, kv))
    return pl.pallas_call(
        body, name="attn_bwd", grid=(ATT_KV_HEADS, t // tq),
        in_specs=[q_spec, k_spec, v_spec, q_spec, q_spec, row_spec],
        out_specs=[q_spec, kv_out, kv_out],
        out_shape=[jax.ShapeDtypeStruct((t, ATT_Q_HEADS * HEAD_DIM), F32),
                   jax.ShapeDtypeStruct((t, ATT_KV_HEADS * HEAD_DIM), F32),
                   jax.ShapeDtypeStruct((t, ATT_KV_HEADS * HEAD_DIM), F32)],
        compiler_params=_params(("parallel", "arbitrary")),
    )(p, p, z, o, do, lse)


_RQ_BLK = P_RQ // HEAD_DIM
_RK_BLK = P_RK // HEAD_DIM
_RV_BLK = Z_RV // HEAD_DIM


def _scan_chunk(direction, step, n_chunks, n_lat_chunks):
    return jnp.where(direction == 0, (step + n_lat_chunks) % n_chunks, n_chunks - 1 - step)


def _ret_geometry(direction):
    c = RET_CHUNK
    i = lax.broadcasted_iota(jnp.int32, (c, c), 0)
    j = lax.broadcasted_iota(jnp.int32, (c, c), 1)
    rel = jnp.where(direction == 0, i - j, j - i).astype(F32)
    r = lax.broadcasted_iota(jnp.int32, (c, 1), 0)
    pos = jnp.where(direction == 0, r, c - 1 - r).astype(F32)
    return rel, pos


def ret_chunk(q, k, v, s, lg, rel, pos):
    c = RET_CHUNK
    causal = rel >= 0
    d_in = jnp.where(causal, jnp.exp(lg * jnp.where(causal, rel, 0.0)), 0.0)
    q_dec = jnp.exp(lg * (pos + 1.0))
    k_dec = jnp.exp(lg * (c - 1.0 - pos))
    c_dec = jnp.exp(lg * c)
    att = bdot_nt(q, k) * d_in
    o = bdot(att, v) + bdot(q * q_dec, s)
    s_new = c_dec * s + bdot_tn(k * k_dec, v)
    return o, s_new


def ret_fwd(p, z, lg, n_lat):
    t = p.shape[0]
    c = RET_CHUNK
    nc, nlc = t // c, n_lat // c

    def body(q_ref, k_ref, v_ref, lg_ref, o_ref, ssave_ref, s_s):
        d, n = pl.program_id(0), pl.program_id(2)

        @pl.when(n == 0)
        def _():
            s_s[...] = jnp.zeros_like(s_s)

        rel, pos = _ret_geometry(d)
        ssave_ref[0, 0, 0] = s_s[...]
        o, s_new = ret_chunk(q_ref[...], k_ref[...], v_ref[...], s_s[...], lg_ref[0, 0], rel, pos)
        o_ref[...] = o
        s_s[...] = s_new

    def blk(base):
        return pl.BlockSpec((c, HEAD_DIM), lambda d, h, n: (_scan_chunk(d, n, nc, nlc), base + h))

    return pl.pallas_call(
        body, name="ret_fwd", grid=(2, RET_HEADS, nc),
        in_specs=[blk(_RQ_BLK), blk(_RK_BLK), blk(_RV_BLK), pl.BlockSpec((1, 1, 1, 1), lambda d, h, n: (d, h, 0, 0))],
        out_specs=[pl.BlockSpec((c, HEAD_DIM), lambda d, h, n: (_scan_chunk(d, n, nc, nlc), d * RET_HEADS + h)),
                   pl.BlockSpec((1, 1, 1, HEAD_DIM, HEAD_DIM), lambda d, h, n: (d, h, n, 0, 0))],
        out_shape=[jax.ShapeDtypeStruct((t, 2 * RET_HEADS * HEAD_DIM), F32),
                   jax.ShapeDtypeStruct((2, RET_HEADS, nc, HEAD_DIM, HEAD_DIM), F32)],
        scratch_shapes=[pltpu.VMEM((HEAD_DIM, HEAD_DIM), F32)],
        compiler_params=_params(("parallel", "parallel", "arbitrary")),
    )(p, p, z, lg)


def ret_bwd(p, z, lg, states, do, n_lat):
    t = p.shape[0]
    c = RET_CHUNK
    nc, nlc = t // c, n_lat // c

    def body(q_ref, k_ref, v_ref, lg_ref, s_ref, do_ref, dq_ref, dk_ref, dv_ref, dlg_ref, ds_s):
        d, n = pl.program_id(0), pl.program_id(2)

        @pl.when(n == 0)
        def _():
            ds_s[...] = jnp.zeros_like(ds_s)
            dlg_ref[...] = jnp.zeros_like(dlg_ref)

        rel, pos = _ret_geometry(d)
        f = functools.partial(ret_chunk, rel=rel, pos=pos)
        _, vjp = jax.vjp(f, q_ref[...], k_ref[...], v_ref[...], s_ref[0, 0, 0], lg_ref[0, 0])
        dq, dk, dv, ds, dlg = vjp((do_ref[...], ds_s[...]))
        dq_ref[...], dk_ref[...], dv_ref[...] = dq, dk, dv
        ds_s[...] = ds
        dlg_ref[0, 0] += dlg

    def chunk_of(d, n):
        return _scan_chunk(d, nc - 1 - n, nc, nlc)

    def blk(base):
        return pl.BlockSpec((c, HEAD_DIM), lambda d, h, n: (chunk_of(d, n), base + h))

    out_blk = pl.BlockSpec((c, HEAD_DIM), lambda d, h, n: (chunk_of(d, n), d * RET_HEADS + h))
    grad_shape = jax.ShapeDtypeStruct((t, 2 * RET_HEADS * HEAD_DIM), F32)
    return pl.pallas_call(
        body, name="ret_bwd", grid=(2, RET_HEADS, nc),
        in_specs=[blk(_RQ_BLK), blk(_RK_BLK), blk(_RV_BLK), pl.BlockSpec((1, 1, 1, 1), lambda d, h, n: (d, h, 0, 0)),
                  pl.BlockSpec((1, 1, 1, HEAD_DIM, HEAD_DIM), lambda d, h, n: (d, h, nc - 1 - n, 0, 0)),
                  pl.BlockSpec((c, HEAD_DIM), lambda d, h, n: (chunk_of(d, n), h))],
        out_specs=[out_blk, out_blk, out_blk, pl.BlockSpec((1, 1, 1, 1), lambda d, h, n: (d, h, 0, 0))],
        out_shape=[grad_shape, grad_shape, grad_shape, jax.ShapeDtypeStruct((2, RET_HEADS, 1, 1), F32)],
        scratch_shapes=[pltpu.VMEM((HEAD_DIM, HEAD_DIM), F32)],
        compiler_params=_params(("parallel", "parallel", "arbitrary")),
    )(p, p, z, lg, states, do)


_GQ_BLK = P_GQ // (GLA_HEADS * GLA_DK)
_GK_BLK = Z_GK // (GLA_HEADS * GLA_DK)
_GV_BLK = Z_GV // (GLA_HEADS * GLA_DV)
_LA_BLK = P_LA // (GLA_HEADS * GLA_DK)


def _gla_mask(direction):
    c = GLA_CHUNK
    i = lax.broadcasted_iota(jnp.int32, (c, c), 0)
    j = lax.broadcasted_iota(jnp.int32, (c, c), 1)
    return (jnp.where(direction == 0, i - j, j - i) >= 0).astype(F32)


def gla_chunk(q, k, v, la, st, mask):
    b = mask_cumsum(mask, la)
    btot = jnp.sum(la, axis=0, keepdims=True)
    half = 0.5 * btot
    qt, kt = q * jnp.exp(b - half), k * jnp.exp(half - b)
    qs, ke = q * jnp.exp(b), k * jnp.exp(btot - b)
    outs, upd = [], []
    for h in range(GLA_HEADS):
        ks = slice(h * GLA_DK, (h + 1) * GLA_DK)
        vh = v[:, h * GLA_DV:(h + 1) * GLA_DV]
        att = bdot_nt(qt[:, ks], kt[:, ks]) * mask
        outs.append(bdot(att, vh) + bdot_nt(qs[:, ks], st[:, ks]))
        upd.append(bdot_tn(vh, ke[:, ks]))
    st_new = st * jnp.exp(btot) + jnp.concatenate(upd, axis=1)
    return jnp.concatenate(outs, axis=1), st_new


def gla_fwd(p, z, n_lat):
    t = p.shape[0]
    c = GLA_CHUNK
    nc, nlc = t // c, n_lat // c
    kw, vw = GLA_HEADS * GLA_DK, GLA_HEADS * GLA_DV

    def body(q_ref, k_ref, v_ref, la_ref, o_ref, ssave_ref, s_s):
        d, n = pl.program_id(0), pl.program_id(1)

        @pl.when(n == 0)
        def _():
            s_s[...] = jnp.zeros_like(s_s)

        ssave_ref[0, 0] = s_s[...]
        o, s_new = gla_chunk(q_ref[...], k_ref[...], v_ref[...], la_ref[...], s_s[...], _gla_mask(d))
        o_ref[...] = o
        s_s[...] = s_new

    def chunk_of(d, n):
        return _scan_chunk(d, n, nc, nlc)

    return pl.pallas_call(
        body, name="gla_fwd", grid=(2, nc),
        in_specs=[pl.BlockSpec((c, kw), lambda d, n: (chunk_of(d, n), _GQ_BLK)),
                  pl.BlockSpec((c, kw), lambda d, n: (chunk_of(d, n), _GK_BLK)),
                  pl.BlockSpec((c, vw), lambda d, n: (chunk_of(d, n), _GV_BLK)),
                  pl.BlockSpec((c, kw), lambda d, n: (chunk_of(d, n), _LA_BLK + d))],
        out_specs=[pl.BlockSpec((c, vw), lambda d, n: (chunk_of(d, n), d)),
                   pl.BlockSpec((1, 1, GLA_DV, kw), lambda d, n: (d, n, 0, 0))],
        out_shape=[jax.ShapeDtypeStruct((t, 2 * vw), F32), jax.ShapeDtypeStruct((2, nc, GLA_DV, kw), F32)],
        scratch_shapes=[pltpu.VMEM((GLA_DV, kw), F32)],
        compiler_params=_params(("parallel", "arbitrary")),
    )(p, z, z, p)


def gla_bwd(p, z, states, do, n_lat):
    t = p.shape[0]
    c = GLA_CHUNK
    nc, nlc = t // c, n_lat // c
    kw, vw = GLA_HEADS * GLA_DK, GLA_HEADS * GLA_DV

    def body(q_ref, k_ref, v_ref, la_ref, s_ref, do_ref, dq_ref, dk_ref, dv_ref, dla_ref, ds_s):
        d, n = pl.program_id(0), pl.program_id(1)

        @pl.when(n == 0)
        def _():
            ds_s[...] = jnp.zeros_like(ds_s)

        f = functools.partial(gla_chunk, mask=_gla_mask(d))
        _, vjp = jax.vjp(f, q_ref[...], k_ref[...], v_ref[...], la_ref[...], s_ref[0, 0])
        dq_ref[...], dk_ref[...], dv_ref[...], dla_ref[...], ds_s[...] = vjp((do_ref[...], ds_s[...]))

    def chunk_of(d, n):
        return _scan_chunk(d, nc - 1 - n, nc, nlc)

    k_out = pl.BlockSpec((c, kw), lambda d, n: (chunk_of(d, n), d))
    return pl.pallas_call(
        body, name="gla_bwd", grid=(2, nc),
        in_specs=[pl.BlockSpec((c, kw), lambda d, n: (chunk_of(d, n), _GQ_BLK)),
                  pl.BlockSpec((c, kw), lambda d, n: (chunk_of(d, n), _GK_BLK)),
                  pl.BlockSpec((c, vw), lambda d, n: (chunk_of(d, n), _GV_BLK)),
                  pl.BlockSpec((c, kw), lambda d, n: (chunk_of(d, n), _LA_BLK + d)),
                  pl.BlockSpec((1, 1, GLA_DV, kw), lambda d, n: (d, nc - 1 - n, 0, 0)),
                  pl.BlockSpec((c, vw), lambda d, n: (chunk_of(d, n), 0))],
        out_specs=[k_out, k_out, pl.BlockSpec((c, vw), lambda d, n: (chunk_of(d, n), d)), k_out],
        out_shape=[jax.ShapeDtypeStruct((t, 2 * kw), F32), jax.ShapeDtypeStruct((t, 2 * kw), F32),
                   jax.ShapeDtypeStruct((t, 2 * vw), F32), jax.ShapeDtypeStruct((t, 2 * kw), F32)],
        scratch_shapes=[pltpu.VMEM((GLA_DV, kw), F32)],
        compiler_params=_params(("parallel", "arbitrary")),
    )(p, z, z, p, states, do)


def _adam_tile(w, g, m, v):
    m = ADAM_B1 * m + (1.0 - ADAM_B1) * g
    v = ADAM_B2 * v + (1.0 - ADAM_B2) * (g * g)
    m_hat = m / (1.0 - ADAM_B1 ** ADAM_STEP)
    v_hat = v / (1.0 - ADAM_B2 ** ADAM_STEP)
    delta = -ADAM_LR * (m_hat / (jnp.sqrt(v_hat) + ADAM_EPS) + ADAM_WD * w)
    return delta, m, v


def adamw(name, w, g, m, v):
    shape = w.shape
    cols = shape[-1] if w.ndim > 1 and shape[-1] >= LANES else int(np.prod(shape))
    rows = int(np.prod(shape)) // cols
    tr = rows
    for cand in (512, 256, 128, 64, 32, 16, 8):
        if rows % cand == 0 and cand * cols * 4 <= (1 << 20):
            tr = cand
            break
    flat = [a.reshape(rows, cols) for a in (w, g, m, v)]

    def body(w_ref, g_ref, m_ref, v_ref, d_ref, mo_ref, vo_ref):
        d_ref[...], mo_ref[...], vo_ref[...] = _adam_tile(w_ref[...], g_ref[...], m_ref[...], v_ref[...])

    spec = pl.BlockSpec((tr, cols), lambda i: (i, 0))
    outs = pl.pallas_call(
        body, name=name, grid=(rows // tr,),
        in_specs=[spec] * 4, out_specs=[spec] * 3,
        out_shape=[jax.ShapeDtypeStruct((rows, cols), F32)] * 3,
        compiler_params=_params(("parallel",)),
    )(*flat)
    return tuple(o.reshape(shape) for o in outs)


MESH = pl.DeviceIdType.MESH
_HBM = pl.BlockSpec(memory_space=pltpu.HBM)
N_CHIPS = 4
N_DEV = 8


def _place():
    x, y, c = lax.axis_index("x"), lax.axis_index("y"), lax.axis_index("c")
    chips = [(1 - x, y), (x, 1 - y), (1 - x, 1 - y)]
    return x, y, c, chips


def _remote(src, dst, send_sem, recv_sem, to):
    return pltpu.make_async_remote_copy(src_ref=src, dst_ref=dst, send_sem=send_sem, recv_sem=recv_sem,
                                        device_id=to, device_id_type=MESH)


def all_gather_small(name, v):
    m_per, n = v.shape

    def body(x_ref, out_ref, send_sems, recv_sems, local_sem):
        x, y, c, chips = _place()
        me, sibling = (x, y, c), (x, y, 1 - c)

        def rows(px, py, pc):
            return out_ref.at[pl.ds((4 * px + 2 * py + pc) * m_per, m_per), :]

        def copy(k, block, to, src=None):
            return _remote(rows(*block) if src is None else src, rows(*block), send_sems.at[k], recv_sems.at[k], to)

        mine = pltpu.make_async_copy(x_ref, rows(*me), local_sem)
        mine.start()
        first = [copy(0, me, sibling, src=x_ref)]
        first += [copy(1 + j, me, (*chip, c), src=x_ref) for j, chip in enumerate(chips)]
        for cp in first:
            cp.start()
        passed = [copy(4 + j, (*chip, c), sibling) for j, chip in enumerate(chips)]
        for j, chip in enumerate(chips):
            copy(1 + j, (*chip, c), me).wait_recv()
            passed[j].start()
        copy(0, sibling, me).wait_recv()
        for j, chip in enumerate(chips):
            copy(4 + j, (*chip, 1 - c), me).wait_recv()
        for cp in first + passed:
            cp.wait_send()
        mine.wait()

    return pl.pallas_call(
        body, name=name,
        out_shape=jax.ShapeDtypeStruct((N_DEV * m_per, n), v.dtype),
        in_specs=[pl.BlockSpec(memory_space=pltpu.VMEM)],
        out_specs=pl.BlockSpec(memory_space=pltpu.VMEM),
        scratch_shapes=[pltpu.SemaphoreType.DMA((7,)), pltpu.SemaphoreType.DMA((7,)), pltpu.SemaphoreType.DMA],
        compiler_params=pltpu.CompilerParams(vmem_limit_bytes=VMEM_LIMIT),
    )(v)


_SEM = pl.BlockSpec(memory_space=pltpu.SEMAPHORE)
_SPLIT_COPY = pltpu.CompilerParams(has_side_effects=pltpu.SideEffectType.DATAFLOW_SIDE_EFFECTING)


def gather_copies(x_ref, land_ref, q, c, chips):
    half = x_ref.shape[0] // 2
    rows = pl.ds(c * half, half)
    return [(x_ref.at[rows, :], land_ref.at[q, rows, :], (*chip, c), land_ref.at[2 * chip[0] + chip[1], rows, :])
            for chip in chips]


def scatter_copies(s_ref, land_ref, q, c, chips):
    return [(s_ref.at[2 * chip[0] + chip[1]], land_ref.at[j], (*chip, c), land_ref.at[j]) for j, chip in enumerate(chips)]


def split_start(name, copies, srcs, land_shapes):
    nt = len(srcs)

    def body(*refs):
        x_refs, land_refs = refs[:nt], refs[nt:2 * nt]
        send, recv = refs[2 * nt:3 * nt], refs[3 * nt:4 * nt]
        x, y, c, chips = _place()
        for t in range(nt):
            for j, (src, dst, to, _) in enumerate(copies(x_refs[t], land_refs[t], 2 * x + y, c, chips)):
                _remote(src, dst, send[t].at[j], recv[t].at[j], to).start()
        refs[-1][...] = jnp.zeros_like(refs[-1])

    lands = [pltpu.with_memory_space_constraint(lax.empty(shape, s.dtype), pltpu.HBM) for shape, s in zip(land_shapes, srcs)]
    outs = pl.pallas_call(
        body, name=name,
        out_shape=tuple([pltpu.SemaphoreType.DMA((3,))] * (2 * nt) + [pltpu.HBM(s.shape, s.dtype) for s in srcs]
                        + [pltpu.HBM(l.shape, l.dtype) for l in lands] + [jax.ShapeDtypeStruct((8, LANES), F32)]),
        in_specs=[_HBM] * (2 * nt),
        out_specs=tuple([_SEM] * (2 * nt) + [_HBM] * (2 * nt) + [pl.BlockSpec(memory_space=pltpu.VMEM)]),
        input_output_aliases={i: 2 * nt + i for i in range(2 * nt)},
        compiler_params=_SPLIT_COPY,
    )(*[pltpu.with_memory_space_constraint(s, pltpu.HBM) for s in srcs], *lands)
    groups = [(outs[t], outs[nt + t], outs[2 * nt + t], outs[3 * nt + t]) for t in range(nt)]
    return groups, outs[-1]


def split_wait(name, copies, group, after):
    send, recv, src, land = group

    def body(x_ref, land_ref, send_sem, recv_sem, after_ref, x_out, land_out):
        x, y, c, chips = _place()
        for j, (s, _, to, arrival) in enumerate(copies(x_ref, land_ref, 2 * x + y, c, chips)):
            cp = _remote(s, arrival, send_sem.at[j], recv_sem.at[j], to)
            cp.wait_send()
            cp.wait_recv()

    return pl.pallas_call(
        body, name=name,
        out_shape=(pltpu.HBM(src.shape, src.dtype), pltpu.HBM(land.shape, land.dtype)),
        in_specs=(_HBM, _HBM, _SEM, _SEM, pl.BlockSpec(memory_space=pl.ANY)), out_specs=(_HBM, _HBM),
        input_output_aliases={0: 0, 1: 1}, compiler_params=_SPLIT_COPY,
    )(src, land, send, recv, after)


def forward_to_sibling(name, land):
    def body(_, out_ref, send_sems, recv_sems):
        x, y, c, chips = _place()
        sibling = (x, y, 1 - c)
        half = out_ref.shape[1] // 2
        cps = []
        for j, chip in enumerate(chips):
            part = out_ref.at[2 * chip[0] + chip[1], pl.ds(c * half, half), :]
            cp = _remote(part, part, send_sems.at[j], recv_sems.at[j], sibling)
            cp.start()
            cps.append(cp)
        for j, chip in enumerate(chips):
            other = out_ref.at[2 * chip[0] + chip[1], pl.ds((1 - c) * half, half), :]
            _remote(other, other, send_sems.at[j], recv_sems.at[j], sibling).wait_recv()
        for cp in cps:
            cp.wait_send()

    return pl.pallas_call(
        body, name=name, out_shape=jax.ShapeDtypeStruct(land.shape, land.dtype),
        in_specs=[_HBM], out_specs=_HBM, input_output_aliases={0: 0},
        scratch_shapes=[pltpu.SemaphoreType.DMA((3,)), pltpu.SemaphoreType.DMA((3,))],
    )(land)


def exchange_sibling_halves(name, grads):
    nt = len(grads)

    def body(*refs):
        g_refs, out_refs = refs[:nt], refs[nt:2 * nt]
        send_sems, recv_sems = refs[2 * nt:]
        x, y, c, _ = _place()
        sibling = (x, y, 1 - c)
        cps = []
        for t in range(nt):
            half = g_refs[t].shape[1] // 2
            cp = _remote(g_refs[t].at[:, pl.ds((1 - c) * half, half), :], out_refs[t], send_sems.at[t], recv_sems.at[t], sibling)
            cp.start()
            cps.append(cp)
        for cp in cps:
            cp.wait()

    return pl.pallas_call(
        body, name=name,
        out_shape=[jax.ShapeDtypeStruct((g.shape[0], g.shape[1] // 2, g.shape[2]), g.dtype) for g in grads],
        in_specs=[_HBM] * nt, out_specs=[_HBM] * nt,
        scratch_shapes=[pltpu.SemaphoreType.DMA((nt,)), pltpu.SemaphoreType.DMA((nt,))],
    )(*grads)


def join_sibling_halves(name, halves):
    nt = len(halves)

    def body(*refs):
        out_refs = refs[nt:2 * nt]
        send_sems, recv_sems = refs[2 * nt:]
        x, y, c, _ = _place()
        sibling = (x, y, 1 - c)
        cps = []
        for t in range(nt):
            half = out_refs[t].shape[0] // 2
            mine = out_refs[t].at[pl.ds(c * half, half), :]
            cp = _remote(mine, mine, send_sems.at[t], recv_sems.at[t], sibling)
            cp.start()
            cps.append(cp)
        for t, cp in enumerate(cps):
            half = out_refs[t].shape[0] // 2
            other = out_refs[t].at[pl.ds((1 - c) * half, half), :]
            _remote(other, other, send_sems.at[t], recv_sems.at[t], sibling).wait_recv()
            cp.wait_send()

    return pl.pallas_call(
        body, name=name,
        out_shape=[jax.ShapeDtypeStruct(h.shape, h.dtype) for h in halves],
        in_specs=[_HBM] * nt, out_specs=[_HBM] * nt,
        input_output_aliases={t: t for t in range(nt)},
        scratch_shapes=[pltpu.SemaphoreType.DMA((nt,)), pltpu.SemaphoreType.DMA((nt,))],
    )(*halves)


def _rows_tile(rows, cols):
    for cand in (512, 256, 128, 64, 32, 16):
        if rows % cand == 0 and cand * cols * 4 <= (1 << 20):
            return cand
    return rows


def add_sibling_half(name, pieces, from_sibling, core):
    n, h, cols = from_sibling.shape
    tr = _rows_tile(h, cols)
    nb = h // tr

    def body(c_ref, a_ref, b_ref, o_ref):
        o_ref[...] = (a_ref[...].astype(F32) + b_ref[...].astype(F32)).astype(o_ref.dtype)

    blk = pl.BlockSpec((1, tr, cols), lambda q, i, c_ref: (q, i, 0))
    return pl.pallas_call(
        body, name=name,
        grid_spec=pltpu.PrefetchScalarGridSpec(
            num_scalar_prefetch=1, grid=(n, nb),
            in_specs=[pl.BlockSpec((1, tr, cols), lambda q, i, c_ref: (q, c_ref[0] * nb + i, 0)), blk], out_specs=blk),
        out_shape=jax.ShapeDtypeStruct((n, h, cols), BF16),
        compiler_params=_params(("parallel", "parallel")),
    )(core.reshape(1).astype(jnp.int32), pieces, from_sibling)


def add_chip_sums(name, chip_sums, from_chips, chip, core):
    _, h, cols = chip_sums.shape
    tr = _rows_tile(h, cols)
    nb = h // tr

    def body(s_ref, own_ref, r0_ref, r1_ref, r2_ref, o_ref):
        acc = own_ref[0].astype(F32) + r0_ref[0].astype(F32)
        o_ref[...] = acc + r1_ref[0].astype(F32) + r2_ref[0].astype(F32)

    def got(j):
        return pl.BlockSpec((1, tr, cols), lambda i, s_ref: (j, i, 0))

    return pl.pallas_call(
        body, name=name,
        grid_spec=pltpu.PrefetchScalarGridSpec(
            num_scalar_prefetch=1, grid=(nb,),
            in_specs=[pl.BlockSpec((1, tr, cols), lambda i, s_ref: (s_ref[0], i, 0)), got(0), got(1), got(2)],
            out_specs=pl.BlockSpec((tr, cols), lambda i, s_ref: (s_ref[1] * nb + i, 0))),
        out_shape=jax.ShapeDtypeStruct((2 * h, cols), F32),
        compiler_params=_params(("parallel",)),
    )(jnp.stack([chip, core]).astype(jnp.int32), chip_sums, from_chips, from_chips, from_chips)


def sum_device_blocks(name, g):
    n = g.shape[1]

    def body(g_ref, o_ref):
        acc = g_ref[0:8, :]
        for d in range(1, N_DEV):
            acc = acc + g_ref[8 * d:8 * (d + 1), :]
        o_ref[...] = acc

    return pl.pallas_call(body, name=name, out_shape=jax.ShapeDtypeStruct((8, n), F32),
                          compiler_params=pltpu.CompilerParams(vmem_limit_bytes=VMEM_LIMIT))(g)


class LayerWeights(NamedTuple):
    norm1_g: jax.Array
    q_g: jax.Array
    k_g: jax.Array
    lg: jax.Array
    ret_g: jax.Array
    gate_up: jax.Array
    gate_b: jax.Array
    gla_g: jax.Array
    norm2_g: jax.Array
    conv_w: jax.Array
    conv_b: jax.Array


def _mod(mods, k):
    return mods[:, k:k + 1, :]


def out_view(l, tb):
    rows = D_MODEL // N_CHIPS
    if tb:
        return BView(n=D_MODEL, k=D_MODEL, tn=rows, tk=D_MODEL, index_map=lambda i, j, kk: (j, l, kk))
    return BView(n=D_MODEL, k=D_MODEL, tn=1024, tk=rows, index_map=lambda i, j, kk: (kk, l, j))


def down_view(l, f, tb):
    rows = f // N_CHIPS
    if tb:
        return BView(n=f, k=D_MODEL, tn=rows, tk=D_MODEL, index_map=lambda i, j, kk: (j, l, kk))
    return BView(n=D_MODEL, k=f, tn=1024, tk=rows, index_map=lambda i, j, kk: (kk, l, j))


def up_view(l, f, part=None):
    cols = 2 * f // N_CHIPS
    tc = _pick(cols, (1408, 1024, 512, 256))
    nbc = cols // tc
    if part is None:
        return BView(n=2 * f, k=D_MODEL, tn=tc, tk=D_MODEL, index_map=lambda i, j, kk: (j // nbc, l, j % nbc))
    nnb = D_MODEL // 1024
    return BView(n=D_MODEL, k=f, tn=1024, tk=tc, index_map=lambda i, j, kk: (2 * part + kk // nbc, l * nnb + j, kk % nbc))


def ada_view(l, n_ada, tb):
    if tb:
        return BView(n=D_MODEL, k=n_ada, tn=1024, tk=n_ada, index_map=lambda i, j, kk: (l, j, 0))
    return BView(n=n_ada, k=D_MODEL, tn=1024, tk=D_MODEL, index_map=lambda i, j, kk: (l, 0, j))


def _prep_args(z, zg, cos, sin, w):
    rows = [Row(z, Z_AV, 0), Row(z, 512, Z_RQ // 512), Row(z, 512, Z_RK // 512), Row(z, 256, Z_GQ // 256),
            Row(zg, LANES, 0), Row(cos, HEAD_DIM, 0, False), Row(sin, HEAD_DIM, 0, False)]
    return rows, [Par(w.q_g), Par(w.k_g), Par(w.gate_up), Par(w.gate_b)]


def _post_args(o_att, o_ret, o_gla, z, w):
    rows = [Row(o_att, 1024), Row(o_ret, 512, 0), Row(o_ret, 512, 1, False), Row(o_gla, 512, 0), Row(o_gla, 512, 1, False),
            Row(z, 512, Z_RG // 512), Row(z, 512, Z_GR // 512)]
    return rows, [Par(w.ret_g), Par(w.gla_g)]


def layer_fwd(l, xs, mods, w, fetch, cos, sin, n_lat):
    t, d = xs.shape
    tag = f"l{l}_"
    nm1 = [Par(w.norm1_g), Par(_mod(mods, 0), True), Par(_mod(mods, 1), True)]
    (h,) = row_map(tag + "norm1", normmod_tile, [Row(xs, d)], nm1, [(d, BF16)], t, n_lat)
    w_main, w_gate = fetch("w_in", h)
    z = matmul(tag + "in_proj", h, w_main)
    zg = matmul(tag + "gate_proj", h, w_gate)
    rows, pars = _prep_args(z, zg, cos, sin, w)
    (p,) = row_map(tag + "prep", prep_tile, rows, pars, [(P_W, F32)], t, n_lat)
    o_att, lse = attn_fwd(p, z, n_lat)
    o_ret, s_ret = ret_fwd(p, z, w.lg, n_lat)
    o_gla, s_gla = gla_fwd(p, z, n_lat)
    rows, pars = _post_args(o_att, o_ret, o_gla, z, w)
    (m,) = row_map(tag + "post", post_tile, rows, pars, [(d, BF16)], t, n_lat)
    g_out = fetch("w_out", m)
    y = matmul(tag + "out_proj", m, g_out, view=out_view(0, False))
    (x1,) = row_map(tag + "resid1", resid_tile, [Row(xs, d), Row(y, d)], [Par(_mod(mods, 2), True)], [(d, F32)], t, n_lat)
    nm2 = [Par(w.norm2_g), Par(_mod(mods, 3), True), Par(_mod(mods, 4), True)]
    (h2,) = row_map(tag + "norm2", normmod_tile, [Row(x1, d)], nm2, [(d, BF16)], t, n_lat)
    f = w.conv_b.shape[1]
    g_up = fetch("w_up", h2)
    u = matmul(tag + "up_proj", h2, g_up, view=up_view(0, f))
    g = convglu(tag + "convglu", u, w.conv_w, w.conv_b, n_lat)
    g_down = fetch("w_down", g)
    yd = matmul(tag + "down_proj", g, g_down, view=down_view(0, f, False))
    (x2,) = row_map(tag + "resid2", resid_tile, [Row(x1, d), Row(yd, d)], [Par(_mod(mods, 5), True)], [(d, F32)], t, n_lat)
    saved = dict(xs=xs, h=h, z=z, zg=zg, p=p, o_att=o_att, lse=lse, o_ret=o_ret, s_ret=s_ret, o_gla=o_gla, s_gla=s_gla,
                 m=m, y=y, x1=x1, h2=h2, u=u, g=g, yd=yd, w_main=w_main, w_gate=w_gate, g_out=g_out, g_up=g_up, g_down=g_down)
    return x2, saved


def _sum_dirs(a):
    w = a.shape[1] // 2
    return a[:, :w] + a[:, w:]


def layer_bwd(l, dx2, s, mods, w, cos, sin, n_lat):
    t, d = dx2.shape
    tag = f"l{l}_b_"
    dyd, dgate5 = row_vjp(tag + "resid2", resid_tile, [Row(s["x1"], d, 0, False), Row(s["yd"], d)],
                          [Par(_mod(mods, 5), True)], [dx2], t, n_lat, row_grad_dtype=BF16)
    f = w.conv_b.shape[1]
    dg = matmul(tag + "down_dx", dyd, s["g_down"], tb=True, view=down_view(0, f, True))
    dw_down = matmul(tag + "down_dw", s["g"], dyd, ta=True, out_dtype=BF16)
    da, dv, dcw, dcb = convglu_bwd(tag + "convglu", s["u"], w.conv_w, w.conv_b, dg, n_lat)
    dh2 = matmul(tag + "up_dx_gate", da, s["g_up"], tb=True, view=up_view(0, f, 0))
    dh2 = matmul(tag + "up_dx_value", dv, s["g_up"], tb=True, view=up_view(0, f, 1), add=dh2)
    dw_up = (matmul(tag + "up_dw_gate", s["h2"], da, ta=True, out_dtype=BF16),
             matmul(tag + "up_dw_value", s["h2"], dv, ta=True, out_dtype=BF16))
    nm2 = [Par(w.norm2_g), Par(_mod(mods, 3), True), Par(_mod(mods, 4), True)]
    dx1, dg2, dshift3, dscale4 = row_vjp(tag + "norm2", normmod_tile, [Row(s["x1"], d)], nm2, [dh2], t, n_lat, add_to_first=dx2)
    dy, dgate2 = row_vjp(tag + "resid1", resid_tile, [Row(s["xs"], d, 0, False), Row(s["y"], d)],
                         [Par(_mod(mods, 2), True)], [dx1], t, n_lat, row_grad_dtype=BF16)
    dm = matmul(tag + "out_dx", dy, s["g_out"], tb=True, view=out_view(0, True))
    dw_out = matmul(tag + "out_dw", s["m"], dy, ta=True, out_dtype=BF16)
    rows, pars = _post_args(s["o_att"], s["o_ret"], s["o_gla"], s["z"], w)
    do_att, do_ret, do_gla, d_rg, d_gr, d_ret_g, d_gla_g = row_vjp(tag + "post", post_tile, rows, pars, [dm], t, n_lat)
    dq_a, dk_a, dv_a = attn_bwd(s["p"], s["z"], s["o_att"], s["lse"], do_att, n_lat)
    dq_r, dk_r, dv_r, dlg = ret_bwd(s["p"], s["z"], w.lg, s["s_ret"], do_ret, n_lat)
    dq_g, dk_g, dv_g, dla = gla_bwd(s["p"], s["z"], s["s_gla"], do_gla, n_lat)
    dp = jnp.concatenate([dq_a, dk_a, _sum_dirs(dq_g), _sum_dirs(dq_r), _sum_dirs(dk_r), dla], axis=1)
    rows, pars = _prep_args(s["z"], s["zg"], cos, sin, w)
    d_zqk, d_zrq, d_zrk, d_zgq, dzg, d_qg, d_kg, d_up, d_gb = row_vjp(tag + "prep", prep_tile, rows, pars, [dp], t, n_lat)
    dz = jnp.concatenate([d_zqk, dv_a, d_zrq, d_zrk, _sum_dirs(dv_r), d_rg, d_zgq, _sum_dirs(dk_g), _sum_dirs(dv_g), d_gr], axis=1)
    dz, dzg = dz.astype(BF16), dzg.astype(BF16)
    dh_gate = matmul(tag + "gate_dx", dzg, s["w_gate"], tb=True)
    dh = matmul(tag + "in_dx", dz, s["w_main"], tb=True, add=dh_gate)
    dw_main = matmul(tag + "in_dw", s["h"], dz, ta=True, out_dtype=BF16)
    dw_gate = matmul(tag + "gate_dw", s["h"], dzg, ta=True, out_dtype=BF16)
    nm1 = [Par(w.norm1_g), Par(_mod(mods, 0), True), Par(_mod(mods, 1), True)]
    dx, dg1, dshift0, dscale1 = row_vjp(tag + "norm1", normmod_tile, [Row(s["xs"], d)], nm1, [dh], t, n_lat, add_to_first=dx1)
    dmods = jnp.concatenate([dshift0, dscale1, dgate2, dshift3, dscale4, dgate5], axis=1)
    grads = dict(w_main=dw_main, w_gate=dw_gate, w_out=dw_out, w_up=dw_up, w_down=dw_down, norm1_g=dg1, q_g=d_qg, k_g=d_kg,
                 lg=dlg, ret_g=d_ret_g, gate_up=d_up, gate_b=d_gb, gla_g=d_gla_g, norm2_g=dg2, conv_w=dcw, conv_b=dcb)
    return dx, dmods, grads


def rope_tables(n_lat, n_ctx):
    rows = n_lat // GRID_W
    row = jnp.repeat(jnp.arange(rows, dtype=F32), GRID_W)
    col = jnp.tile(jnp.arange(GRID_W, dtype=F32), rows)
    n_freq = HEAD_DIM // 4
    inv_freq = ROPE_THETA ** (-jnp.arange(n_freq, dtype=F32) / n_freq)
    ang = jnp.concatenate([row[:, None] * inv_freq, col[:, None] * inv_freq], axis=-1)
    cos, sin = jnp.cos(ang), jnp.sin(ang)
    cos = jnp.concatenate([jnp.concatenate([cos, cos], axis=1), jnp.ones((n_ctx, HEAD_DIM), F32)], axis=0)
    sin = jnp.concatenate([jnp.concatenate([-sin, sin], axis=1), jnp.zeros((n_ctx, HEAD_DIM), F32)], axis=0)
    return cos, sin


def local_step(xs, target, mods, weights, fetch, final_g, n_lat, layer_done):
    t, d = xs.shape
    cos, sin = rope_tables(n_lat, t - n_lat)
    saved = []
    h = xs
    for l, w in enumerate(weights):
        h, s = layer_fwd(l, h, mods[l], w, functools.partial(fetch, l), cos, sin, n_lat)
        saved.append(s)
    loss, dlat, dgf = final_loss(h, target, final_g, n_lat)
    dx = jnp.concatenate([dlat, jnp.zeros((t - n_lat, d), F32)], axis=0)
    dmods, grads = [None] * len(weights), [None] * len(weights)
    for l in reversed(range(len(weights))):
        dx, dmods[l], grads[l] = layer_bwd(l, dx, saved[l], mods[l], weights[l], cos, sin, n_lat)
        dx = dx + layer_done(l, grads[l])[0, 0]
    return loss, dx, dmods, grads, dgf


WEIGHT_NAMES = ("c_ctx", "ada_w", "ada_b", "norm1_g", "w_in", "q_norm_g", "k_norm_g", "ret_log_decay", "ret_norm_g",
                "gla_gate_up", "gla_gate_b", "gla_norm_g", "w_out", "norm2_g", "w_up", "conv_w", "conv_b", "w_down", "final_norm_g")
PACK_QUANTUM = 8 * LANES


def _pack(arrays):
    flat = jnp.concatenate([a.reshape(-1).astype(F32) for a in arrays])
    n = -(-flat.shape[0] // PACK_QUANTUM) * PACK_QUANTUM
    return jnp.pad(flat, (0, n - flat.shape[0])).reshape(8, n // 8)


def _unpack(flat2d, shapes):
    out, at = [], 0
    for s in shapes:
        size = int(np.prod(s))
        out.append(flat2d[:, at:at + size].reshape((flat2d.shape[0],) + tuple(s)))
        at += size
    return out


def _per_device(gathered):
    return gathered.reshape(N_DEV, -1)


def _from_chips(per_device, axis):
    chips = per_device[0::2]
    moved = jnp.moveaxis(chips, 0, axis)
    shape = moved.shape
    return moved.reshape(shape[:axis] + (shape[axis] * shape[axis + 1],) + shape[axis + 2:])


def kernel(x, c, ctx, c_ctx, ada_w, ada_b, norm1_g, w_in, q_norm_g, k_norm_g, ret_log_decay, ret_norm_g, gla_gate_up, gla_gate_b, gla_norm_g, w_out, norm2_g, w_up, conv_w, conv_b, w_down, final_norm_g, loss_target, m_c_ctx, m_ada_w, m_ada_b, m_norm1_g, m_w_in, m_q_norm_g, m_k_norm_g, m_ret_log_decay, m_ret_norm_g, m_gla_gate_up, m_gla_gate_b, m_gla_norm_g, m_w_out, m_norm2_g, m_w_up, m_conv_w, m_conv_b, m_w_down, m_final_norm_g, v_c_ctx, v_ada_w, v_ada_b, v_norm1_g, v_w_in, v_q_norm_g, v_k_norm_g, v_ret_log_decay, v_ret_norm_g, v_gla_gate_up, v_gla_gate_b, v_gla_norm_g, v_w_out, v_norm2_g, v_w_up, v_conv_w, v_conv_b, v_w_down, v_final_norm_g):
    weights = dict(zip(WEIGHT_NAMES, (c_ctx, ada_w, ada_b, norm1_g, w_in, q_norm_g, k_norm_g, ret_log_decay, ret_norm_g,
                                      gla_gate_up, gla_gate_b, gla_norm_g, w_out, norm2_g, w_up, conv_w, conv_b, w_down, final_norm_g)))
    mom_m = dict(zip(WEIGHT_NAMES, (m_c_ctx, m_ada_w, m_ada_b, m_norm1_g, m_w_in, m_q_norm_g, m_k_norm_g, m_ret_log_decay, m_ret_norm_g,
                                    m_gla_gate_up, m_gla_gate_b, m_gla_norm_g, m_w_out, m_norm2_g, m_w_up, m_conv_w, m_conv_b, m_w_down, m_final_norm_g)))
    mom_v = dict(zip(WEIGHT_NAMES, (v_c_ctx, v_ada_w, v_ada_b, v_norm1_g, v_w_in, v_q_norm_g, v_k_norm_g, v_ret_log_decay, v_ret_norm_g,
                                    v_gla_gate_up, v_gla_gate_b, v_gla_norm_g, v_w_out, v_norm2_g, v_w_up, v_conv_w, v_conv_b, v_w_down, v_final_norm_g)))
    depth, d = norm1_g.shape
    assert d == D_MODEL and x.shape[0] == 1
    n_lat, n_ctx, f = x.shape[1], ctx.shape[1], conv_b.shape[1]
    assert n_lat % ROW_TILE == 0 and n_ctx % ROW_TILE == 0 and f % FFN_COL_TILE == 0 and f % N_CHIPS == 0
    n_in = w_in.shape[2]
    n_ada = ada_w.shape[2]
    xi, yi, ci = lax.axis_index("x"), lax.axis_index("y"), lax.axis_index("c")
    chip = 2 * xi + yi
    dev = 2 * chip + ci

    big = ("w_in", "w_out", "w_up", "w_down")
    order = [(l, name) for l in range(depth) for name in big]
    shards = [weights[name][l].astype(BF16) for l, name in order]
    in_flight, token = split_start("gather_start", gather_copies, shards, [(N_CHIPS,) + s.shape for s in shards])

    def fetch(l, name, after):
        tag = f"{name}{l}"
        own, land = split_wait("gather_wait_" + tag, gather_copies, in_flight[order.index((l, name))], after)
        land = forward_to_sibling("gather_pass_" + tag, land)
        land = lax.dynamic_update_slice_in_dim(land, own[None], chip, axis=0)
        if name != "w_in":
            return land
        cols = jnp.concatenate([land[q] for q in range(N_CHIPS)], axis=1)
        return cols[:, :N_MAIN], jnp.pad(cols[:, N_MAIN:], ((0, 0), (0, LANES - N_GATE)))

    small_shapes = [c.shape[1:], conv_w.shape, gla_gate_up.shape, gla_gate_b.shape]
    got = _per_device(all_gather_small("gather_small", _pack([c, conv_w, gla_gate_up, gla_gate_b])))
    c_all, conv_w_sh, gate_up_sh, gate_b_sh = _unpack(got, small_shapes)
    conv_w_full = _from_chips(conv_w_sh, 2)
    gate_up_full = _from_chips(gate_up_sh, 3)
    gate_b_full = _from_chips(gate_b_sh, 2)

    act = jnp.zeros((16, d), F32).at[0:N_DEV].set(jax.nn.silu(c_all)).at[N_DEV].set(jax.nn.silu(c_ctx))
    mod_sh = jnp.stack([matmul(f"ada_fwd{l}", act, ada_w, view=ada_view(l, n_ada, False)) for l in range(depth)])
    got = _per_device(all_gather_small("gather_mods", _pack([mod_sh])))
    (mod_sh_all,) = _unpack(got, [mod_sh.shape])
    mod_full = _from_chips(mod_sh_all, 2) + ada_b[:, None, :]
    mod_mine = lax.dynamic_index_in_dim(mod_full, dev, axis=1, keepdims=False)
    mods = [jnp.stack([mod_mine[l].reshape(N_MOD, d), mod_full[l, N_DEV].reshape(N_MOD, d)]) for l in range(depth)]

    layer_w = []
    for l in range(depth):
        up = jnp.zeros((2, LANES, GLA_HEADS * GLA_DK), F32)
        up = up.at[0, 0:GLA_RANK].set(gate_up_full[l, 0]).at[1, GLA_RANK:2 * GLA_RANK].set(gate_up_full[l, 1])
        layer_w.append(LayerWeights(
            norm1_g=norm1_g[l].reshape(1, 1, d), q_g=q_norm_g[l].reshape(1, 1, HEAD_DIM), k_g=k_norm_g[l].reshape(1, 1, HEAD_DIM),
            lg=ret_log_decay[l].reshape(2, RET_HEADS, 1, 1), ret_g=ret_norm_g[l].reshape(1, 1, HEAD_DIM),
            gate_up=up.reshape(1, 2 * LANES, -1), gate_b=gate_b_full[l].reshape(1, 2, -1), gla_g=gla_norm_g[l].reshape(1, 1, HEAD_DIM),
            norm2_g=norm2_g[l].reshape(1, 1, d), conv_w=conv_w_full[l], conv_b=conv_b[l].reshape(1, f)))

    def in_pieces(g):
        full_cols = jnp.concatenate([g["w_main"], g["w_gate"][:, :N_GATE]], axis=1)
        return jnp.stack([full_cols[:, q * n_in:(q + 1) * n_in] for q in range(N_CHIPS)])

    def up_pieces(g):
        gate, value = g["w_up"]
        return jnp.stack([gate[:, :f // 2], gate[:, f // 2:], value[:, :f // 2], value[:, f // 2:]])

    reducing = {}

    def layer_done(l, g):
        pieces = [in_pieces(g), g["w_out"].reshape(N_CHIPS, d // N_CHIPS, d), up_pieces(g),
                  g["w_down"].reshape(N_CHIPS, f // N_CHIPS, d)]
        from_sibling = exchange_sibling_halves(f"rs_sibling{l}", pieces)
        sums = [add_sibling_half(f"rs_add_sibling{l}_{k}", pc, sib, ci) for k, (pc, sib) in enumerate(zip(pieces, from_sibling))]
        reducing[l], started = split_start(f"rs_start{l}", scatter_copies, sums, [(3,) + s.shape[1:] for s in sums])
        return started

    xs = jnp.concatenate([x[0], ctx[0]], axis=0) + token[0, 0]
    loss, dx, dmods, grads, dgf = local_step(xs, loss_target[0], mods, layer_w, fetch, final_norm_g.reshape(1, d), n_lat, layer_done)

    def gate_up_grad(g):
        return jnp.stack([g[0, 0:GLA_RANK], g[0, LANES + GLA_RANK:LANES + 2 * GLA_RANK]])

    per_layer = [[dmods[l][0], dmods[l][1], grads[l]["norm1_g"], grads[l]["norm2_g"], grads[l]["q_g"], grads[l]["k_g"],
                  grads[l]["ret_g"], grads[l]["gla_g"], grads[l]["lg"], gate_up_grad(grads[l]["gate_up"]), grads[l]["gate_b"],
                  grads[l]["conv_w"], grads[l]["conv_b"]] for l in range(depth)]
    layer_shapes = [(N_MOD * d,), (N_MOD * d,), (d,), (d,), (HEAD_DIM,), (HEAD_DIM,), (HEAD_DIM,), (HEAD_DIM,), (2, RET_HEADS),
                    (2, GLA_RANK, GLA_HEADS * GLA_DK), (2, GLA_HEADS * GLA_DK), (3, f), (f,)]
    packed = _pack([a for lay in per_layer for a in lay] + [dgf, loss[0, 0:1]])
    gathered = all_gather_small("gather_small_grads", packed)
    every = _unpack(_per_device(gathered), layer_shapes * depth + [(d,), (1,)])
    total = _unpack(sum_device_blocks("sum_small_grads", gathered).reshape(1, -1), layer_shapes * depth + [(d,), (1,)])
    nl = len(layer_shapes)

    def tot(l, k):
        return total[l * nl + k][0]

    out = {"norm1_g": jnp.stack([tot(l, 2) for l in range(depth)]), "norm2_g": jnp.stack([tot(l, 3) for l in range(depth)]),
           "q_norm_g": jnp.stack([tot(l, 4) for l in range(depth)]), "k_norm_g": jnp.stack([tot(l, 5) for l in range(depth)]),
           "ret_norm_g": jnp.stack([tot(l, 6) for l in range(depth)]), "gla_norm_g": jnp.stack([tot(l, 7) for l in range(depth)]),
           "ret_log_decay": jnp.stack([tot(l, 8) for l in range(depth)]),
           "gla_gate_up": lax.dynamic_slice_in_dim(jnp.stack([tot(l, 9) for l in range(depth)]), chip * gla_gate_up.shape[3], gla_gate_up.shape[3], axis=3),
           "gla_gate_b": lax.dynamic_slice_in_dim(jnp.stack([tot(l, 10) for l in range(depth)]), chip * gla_gate_b.shape[2], gla_gate_b.shape[2], axis=2),
           "conv_w": lax.dynamic_slice_in_dim(jnp.stack([tot(l, 11) for l in range(depth)]), chip * conv_w.shape[2], conv_w.shape[2], axis=2),
           "conv_b": jnp.stack([tot(l, 12) for l in range(depth)]),
           "final_norm_g": total[depth * nl][0],
           "ada_b": jnp.stack([tot(l, 0) + tot(l, 1) for l in range(depth)])}
    loss_total = total[depth * nl + 1][0, 0]

    dmod_all = jnp.zeros((depth, 16, N_MOD * d), F32)
    for l in range(depth):
        dmod_all = dmod_all.at[l, 0:N_DEV].set(every[l * nl][:, :]).at[l, N_DEV].set(tot(l, 1))
    dmod_cols = lax.dynamic_slice_in_dim(dmod_all, chip * n_ada, n_ada, axis=2)
    out["ada_w"] = jnp.stack([matmul(f"ada_dw{l}", act, dmod_cols[l], ta=True) for l in range(depth)])
    dact = matmul("ada_dx0", dmod_cols[0], ada_w, tb=True, view=ada_view(0, n_ada, True))
    for l in range(1, depth):
        dact = matmul(f"ada_dx{l}", dmod_cols[l], ada_w, tb=True, view=ada_view(l, n_ada, True), add=dact)
    got = _per_device(all_gather_small("gather_dcctx", _pack([dact[N_DEV]])))[0::2, :d]
    dsilu = got[0] + got[1] + got[2] + got[3]
    sig = jax.nn.sigmoid(c_ctx)
    out["c_ctx"] = dsilu * (sig + c_ctx * sig * (1.0 - sig))

    halves = []
    for l in reversed(range(depth)):
        for k, group in enumerate(reducing[l]):
            sums, got = split_wait(f"rs_wait{l}_{k}", scatter_copies, group, dx)
            halves.append(add_chip_sums(f"rs_add_chips{l}_{k}", sums, got, chip, ci))
    full = join_sibling_halves("rs_join", halves)
    for k, name in enumerate(big):
        out[name] = jnp.stack([full[(depth - 1 - l) * len(big) + k] for l in range(depth)])

    deltas, new_m, new_v = {}, {}, {}
    for name in WEIGHT_NAMES:
        out[name] = out[name].reshape(weights[name].shape)
        deltas[name], new_m[name], new_v[name] = adamw("adamw_" + name, weights[name], out[name], mom_m[name], mom_v[name])
    grad_x = dx[:n_lat].reshape(x.shape)
    return (loss_total, grad_x, *[out[n] for n in WEIGHT_NAMES], *[deltas[n] for n in WEIGHT_NAMES],
            *[new_m[n] for n in WEIGHT_NAMES], *[new_v[n] for n in WEIGHT_NAMES])
```

```python
import functools
from typing import NamedTuple

import numpy as np
import jax
import jax.numpy as jnp
from jax import lax
from jax.experimental import pallas as pl
from jax.experimental.pallas import tpu as pltpu

F32 = jnp.float32
BF16 = jnp.bfloat16

D_MODEL = 2048
HEAD_DIM = 128
ATT_Q_HEADS = 8
ATT_KV_HEADS = 2
ATT_GROUP = ATT_Q_HEADS // ATT_KV_HEADS
RET_HEADS = 4
GLA_HEADS = 4
GLA_DK = 64
GLA_DV = 128
GLA_RANK = 16
GLA_TAU = 16.0
RET_CHUNK = 128
GLA_CHUNK = 64
GRID_W = 64
ROPE_THETA = 10000.0
N_MOD = 6
EPS = 1e-6
N_MAIN = 5120
N_GATE = 2 * GLA_RANK
LANES = 128
ROW_TILE = 256
FFN_COL_TILE = 256
VMEM_LIMIT = 56 * 1024 * 1024

ADAM_LR = 0.001
ADAM_B1 = 0.9
ADAM_B2 = 0.999
ADAM_EPS = 1e-08
ADAM_WD = 0.01
ADAM_STEP = 10

Z_AQ, Z_AK, Z_AV = 0, 1024, 1280
Z_RQ, Z_RK, Z_RV, Z_RG = 1536, 2048, 2560, 3072
Z_GQ, Z_GK, Z_GV, Z_GR = 3584, 3840, 4096, 4608
P_AQ, P_AK, P_GQ, P_RQ, P_RK, P_LA = 0, 1024, 1280, 1536, 2048, 2560
P_W = 3072


def _params(sem=None):
    return pltpu.CompilerParams(dimension_semantics=sem, vmem_limit_bytes=VMEM_LIMIT)


def _pick(n, cands):
    for c in cands:
        if n % c == 0:
            return c
    return n


_NN = (((1,), (0,)), ((), ()))
_NT = (((1,), (1,)), ((), ()))
_TN = (((0,), (0,)), ((), ()))


def _dg(a, b, dims):
    return lax.dot_general(a.astype(BF16), b.astype(BF16), dims, preferred_element_type=F32)


@jax.custom_vjp
def bdot(a, b):
    return _dg(a, b, _NN)


def _bdot_fwd(a, b):
    return _dg(a, b, _NN), (a, b)


def _bdot_bwd(res, ct):
    a, b = res
    return _dg(ct, b, _NT), _dg(a, ct, _TN)


bdot.defvjp(_bdot_fwd, _bdot_bwd)


@jax.custom_vjp
def bdot_nt(a, b):
    return _dg(a, b, _NT)


def _bdot_nt_fwd(a, b):
    return _dg(a, b, _NT), (a, b)


def _bdot_nt_bwd(res, ct):
    a, b = res
    return _dg(ct, b, _NN), _dg(ct, a, _TN)


bdot_nt.defvjp(_bdot_nt_fwd, _bdot_nt_bwd)


@jax.custom_vjp
def bdot_tn(a, b):
    return _dg(a, b, _TN)


def _bdot_tn_fwd(a, b):
    return _dg(a, b, _TN), (a, b)


def _bdot_tn_bwd(res, ct):
    a, b = res
    return _dg(b, ct, _NT), _dg(a, ct, _NN)


bdot_tn.defvjp(_bdot_tn_fwd, _bdot_tn_bwd)


def _split3(x):
    x1 = x.astype(BF16)
    r1 = x - x1.astype(F32)
    x2 = r1.astype(BF16)
    x3 = (r1 - x2.astype(F32)).astype(BF16)
    return x1, x2, x3


def _mask_dot(mask_bf16, x, dims):
    x1, x2, x3 = _split3(x)
    f = lambda t: lax.dot_general(mask_bf16, t, dims, preferred_element_type=F32)
    return f(x1) + f(x2) + f(x3)


@jax.custom_vjp
def mask_cumsum(mask, x):
    return _mask_dot(mask.astype(BF16), x, _NN)


def _mask_cumsum_fwd(mask, x):
    return mask_cumsum(mask, x), mask


def _mask_cumsum_bwd(mask, ct):
    return jnp.zeros_like(mask), _mask_dot(mask.astype(BF16), ct, _TN)


mask_cumsum.defvjp(_mask_cumsum_fwd, _mask_cumsum_bwd)


def _roll(x, shift, axis):
    return pltpu.roll(x, shift % x.shape[axis], axis)


@functools.partial(jax.custom_vjp, nondiff_argnums=(1, 2))
def roll(x, shift, axis):
    return _roll(x, shift, axis)


def _roll_fwd(x, shift, axis):
    return _roll(x, shift, axis), None


def _roll_bwd(shift, axis, _, ct):
    return (_roll(ct, -shift, axis),)


roll.defvjp(_roll_fwd, _roll_bwd)


def rms(x):
    return x * lax.rsqrt(jnp.mean(x * x, axis=-1, keepdims=True) + EPS)


def silu(x):
    return x * (1.0 / (1.0 + jnp.exp(-x)))


def log_sigmoid(x):
    return jnp.minimum(x, 0.0) - jnp.log(1.0 + jnp.exp(-jnp.abs(x)))


def rope(t, cos, sin):
    return t * cos + roll(t, HEAD_DIM // 2, 1) * sin


def _heads(x, n, width=HEAD_DIM):
    return [x[:, h * width:(h + 1) * width] for h in range(n)]


class Row(NamedTuple):
    arr: jax.Array
    width: int
    idx: int = 0
    diff: bool = True


class Par(NamedTuple):
    arr: jax.Array
    grouped: bool = False
    diff: bool = True


def _row_specs(rows, pars, tm, n_lat_tiles):
    def grp(i):
        return jnp.minimum(i // n_lat_tiles, 1)

    specs = [pl.BlockSpec((tm, r.width), functools.partial(lambda i, k: (i, k), k=r.idx)) for r in rows]
    for p in pars:
        blk = (1,) + p.arr.shape[1:]
        if p.grouped:
            specs.append(pl.BlockSpec(blk, lambda i: (grp(i), 0, 0)))
        else:
            specs.append(pl.BlockSpec(blk, lambda i: (0, 0, 0)))
    return specs


def row_map(name, fn, rows, pars, outs, n_rows, n_lat):
    tm = ROW_TILE
    nr, npar = len(rows), len(pars)

    def body(*refs):
        vals = [r[...] for r in refs[:nr]] + [p[0] for p in refs[nr:nr + npar]]
        res = fn(*vals)
        for o, v in zip(refs[nr + npar:], res):
            o[...] = v.astype(o.dtype)

    return pl.pallas_call(
        body, name=name, grid=(n_rows // tm,),
        in_specs=_row_specs(rows, pars, tm, n_lat // tm),
        out_specs=[pl.BlockSpec((tm, w), lambda i: (i, 0)) for w, _ in outs],
        out_shape=[jax.ShapeDtypeStruct((n_rows, w), dt) for w, dt in outs],
        compiler_params=_params(("arbitrary",)),
    )(*[r.arr for r in rows], *[p.arr for p in pars])


def row_vjp(name, fn, rows, pars, cts, n_rows, n_lat, add_to_first=None, row_grad_dtype=F32, after=None):
    tm = ROW_TILE
    nr, npar, nc = len(rows), len(pars), len(cts)
    n_lat_tiles = n_lat // tm
    args = list(rows) + list(pars)
    diff_pos = [k for k, a in enumerate(args) if a.diff]
    n_add = 0 if add_to_first is None else 1
    n_after = 0 if after is None else 1

    def body(*refs):
        i = pl.program_id(0)
        vals = [r[...] for r in refs[:nr]] + [p[0] for p in refs[nr:nr + npar]]
        ct_vals = tuple(c[...] for c in refs[nr + npar:nr + npar + nc])
        out_refs = refs[nr + npar + nc + n_add + n_after:]

        def g(*dv):
            full = list(vals)
            for k, v in zip(diff_pos, dv):
                full[k] = v
            return tuple(fn(*full))

        _, vjp = jax.vjp(g, *[vals[k] for k in diff_pos])
        grads = vjp(ct_vals)
        for n, (k, o, gr) in enumerate(zip(diff_pos, out_refs, grads)):
            if k < nr:
                o[...] = (gr + refs[nr + npar + nc][...] if (n == 0 and n_add) else gr).astype(o.dtype)
            else:
                first = (i == 0) | (i == n_lat_tiles) if args[k].grouped else (i == 0)

                @pl.when(first)
                def _():
                    o[0] = gr

                @pl.when(jnp.logical_not(first))
                def _():
                    o[0] += gr

    def grp(i):
        return jnp.minimum(i // n_lat_tiles, 1)

    out_specs, out_shape = [], []
    for k in diff_pos:
        a = args[k]
        if k < nr:
            out_specs.append(pl.BlockSpec((tm, a.width), lambda i: (i, 0)))
            out_shape.append(jax.ShapeDtypeStruct((n_rows, a.width), row_grad_dtype))
        else:
            blk = (1,) + a.arr.shape[1:]
            out_specs.append(pl.BlockSpec(blk, (lambda i: (grp(i), 0, 0)) if a.grouped else (lambda i: (0, 0, 0))))
            out_shape.append(jax.ShapeDtypeStruct(a.arr.shape, F32))
    extra = list(cts) + ([add_to_first] if n_add else [])
    ct_specs = [pl.BlockSpec((tm, c.shape[1]), lambda i: (i, 0)) for c in extra]
    if n_after:
        extra.append(after)
        ct_specs.append(pl.BlockSpec(memory_space=pl.ANY))
    return pl.pallas_call(
        body, name=name, grid=(n_rows // tm,),
        in_specs=_row_specs(rows, pars, tm, n_lat_tiles) + ct_specs,
        out_specs=out_specs, out_shape=out_shape,
        compiler_params=_params(("arbitrary",)),
    )(*[r.arr for r in rows], *[p.arr for p in pars], *extra)


class BView(NamedTuple):
    n: int
    k: int
    tn: int
    tk: int
    index_map: object
    lead: int = 1


MATMUL_VMEM_BUDGET = 40 * 1024 * 1024


def _matmul_tiles(m, n, k, a_bytes, b_bytes, o_bytes):
    tms = [c for c in (1152, 1024, 768, 512, 256, 128) if m % c == 0] or [m]
    tns = [c for c in (2048, 1408, 1280, 1024, 768, 512, 256, 128) if n % c == 0] or [n]
    tks = [k] + [c for c in (2816, 2304, 2048, 1408, 1024, 512, 256, 128) if k % c == 0 and c < k]
    for tk in tks:
        fits = [(tm * tn, tm, tn) for tm in tms for tn in tns
                if 2 * (tm * tk * a_bytes + tk * tn * b_bytes + tm * tn * o_bytes) + 2 * tm * tn * 4 <= MATMUL_VMEM_BUDGET]
        if fits and (max(fits)[0] >= 512 * 512 or tms == [m] or tk == tks[-1]):
            _, tm, tn = max(fits)
            return tm, tn, tk
    raise ValueError(f"no matmul tiling for {(m, n, k)}")


def matmul(name, a, b, *, ta=False, tb=False, add=None, out_dtype=F32, view=None):
    m = a.shape[1] if ta else a.shape[0]
    o_bytes = jnp.dtype(out_dtype).itemsize * (1 if add is None else 2)
    if view is None:
        k = a.shape[0] if ta else a.shape[1]
        n = b.shape[0] if tb else b.shape[1]
        assert (b.shape[1] if tb else b.shape[0]) == k, (a.shape, b.shape, ta, tb)
        tm, tn, tk = _matmul_tiles(m, n, k, a.dtype.itemsize, b.dtype.itemsize, o_bytes)
    else:
        n, k, tn, tk = view.n, view.k, view.tn, view.tk
        tm, _, _ = _matmul_tiles(m, tn, tk, a.dtype.itemsize, b.dtype.itemsize, o_bytes)
    nk = k // tk
    dims = (((0 if ta else 1,), (1 if tb else 0,)), ((), ()))

    def body(a_ref, b_ref, *rest):
        prod = lax.dot_general(a_ref[...].astype(BF16), b_ref[...].astype(BF16), dims, preferred_element_type=F32)
        if nk == 1:
            o_ref = rest[-1]
            o_ref[...] = (prod if add is None else prod + rest[0][...]).astype(o_ref.dtype)
            return
        o_ref, acc = rest[-2:]
        kk = pl.program_id(2)

        @pl.when(kk == 0)
        def _():
            acc[...] = prod

        @pl.when(kk != 0)
        def _():
            acc[...] += prod

        @pl.when(kk == nk - 1)
        def _():
            r = acc[...]
            if add is not None:
                r = r + rest[0][...]
            o_ref[...] = r.astype(o_ref.dtype)

    if ta:
        a_spec = pl.BlockSpec((tk, tm), lambda i, j, kk: (kk, i))
    else:
        a_spec = pl.BlockSpec((tm, tk), lambda i, j, kk: (i, kk))
    b_tile = (tn, tk) if tb else (tk, tn)
    if view is not None:
        b_spec = pl.BlockSpec((None,) * view.lead + b_tile, view.index_map)
    elif tb:
        b_spec = pl.BlockSpec(b_tile, lambda i, j, kk: (j, kk))
    else:
        b_spec = pl.BlockSpec(b_tile, lambda i, j, kk: (kk, j))
    o_spec = pl.BlockSpec((tm, tn), lambda i, j, kk: (i, j))
    ins = [a, b] + ([add] if add is not None else [])
    return pl.pallas_call(
        body, name=name, grid=(m // tm, n // tn, nk),
        in_specs=[a_spec, b_spec] + ([o_spec] if add is not None else []),
        out_specs=o_spec, out_shape=jax.ShapeDtypeStruct((m, n), out_dtype),
        scratch_shapes=[pltpu.VMEM((tm, tn), F32)] if nk > 1 else [],
        compiler_params=_params(("parallel", "parallel", "arbitrary")),
    )(*ins)


def normmod_tile(x, g, shift, scale):
    return (rms(x) * g * (1.0 + scale) + shift,)


def resid_tile(x, y, gate):
    return (x + gate * y,)


def prep_tile(z_qk, z_rq, z_rk, z_gq, zg, cos, sin, qg, kg, gate_up, gate_b):
    out = []
    for h, t in enumerate(_heads(z_qk, ATT_Q_HEADS + ATT_KV_HEADS)):
        out.append(rope(rms(t) * (qg if h < ATT_Q_HEADS else kg), cos, sin))
    gq = z_gq * (GLA_DK ** -0.5)
    rq = [rope(t, cos, sin) for t in _heads(z_rq, RET_HEADS)]
    rk = [rope(t * (HEAD_DIM ** -0.5), cos, sin) for t in _heads(z_rk, RET_HEADS)]
    la = [log_sigmoid(bdot(zg, gate_up[d * LANES:(d + 1) * LANES]) + gate_b[d:d + 1]) * (1.0 / GLA_TAU) for d in range(2)]
    return (jnp.concatenate(out + [gq] + rq + rk + la, axis=1),)


def post_tile(o_att, o_ret_f, o_ret_b, o_gla_f, o_gla_b, rg, gr, ret_g, gla_g):
    ret = jnp.concatenate([rms(t) * ret_g for t in _heads(o_ret_f + o_ret_b, RET_HEADS)], axis=1) * silu(rg)
    gla = jnp.concatenate([rms(t) * gla_g for t in _heads(o_gla_f + o_gla_b, GLA_HEADS)], axis=1) * silu(gr)
    return (jnp.concatenate([o_att, ret, gla], axis=1),)


def _convglu_tile(n_lat, a, v, cw, cb):
    t = a.shape[0]
    row = lax.broadcasted_iota(jnp.int32, (t, 1), 0)
    has_prev = ((row != 0) & (row != n_lat)).astype(F32)
    has_next = ((row != n_lat - 1) & (row != t - 1)).astype(F32)
    conv = roll(a, 1, 0) * has_prev * cw[0:1] + a * cw[1:2] + roll(a, -1, 0) * has_next * cw[2:3] + cb
    return silu(conv) * v


def convglu(name, u, cw, cb, n_lat):
    t, f2 = u.shape
    f, tc = f2 // 2, FFN_COL_TILE
    nb = f // tc

    def body(a_ref, v_ref, cw_ref, cb_ref, o_ref):
        o_ref[...] = _convglu_tile(n_lat, a_ref[...], v_ref[...], cw_ref[...], cb_ref[...]).astype(o_ref.dtype)

    return pl.pallas_call(
        body, name=name, grid=(nb,),
        in_specs=[pl.BlockSpec((t, tc), lambda j: (0, j)), pl.BlockSpec((t, tc), lambda j: (0, nb + j)),
                  pl.BlockSpec((3, tc), lambda j: (0, j)), pl.BlockSpec((1, tc), lambda j: (0, j))],
        out_specs=pl.BlockSpec((t, tc), lambda j: (0, j)),
        out_shape=jax.ShapeDtypeStruct((t, f), BF16),
        compiler_params=_params(("parallel",)),
    )(u, u, cw, cb)


def convglu_bwd(name, u, cw, cb, dg, n_lat):
    t, f2 = u.shape
    f, tc = f2 // 2, FFN_COL_TILE
    nb = f // tc

    def body(a_ref, v_ref, cw_ref, cb_ref, dg_ref, da_ref, dv_ref, dcw_ref, dcb_ref):
        _, vjp = jax.vjp(functools.partial(_convglu_tile, n_lat), a_ref[...], v_ref[...], cw_ref[...], cb_ref[...])
        da, dv, dcw_ref[...], dcb_ref[...] = vjp(dg_ref[...])
        da_ref[...], dv_ref[...] = da.astype(BF16), dv.astype(BF16)

    col = pl.BlockSpec((t, tc), lambda j: (0, j))
    return pl.pallas_call(
        body, name=name, grid=(nb,),
        in_specs=[col, pl.BlockSpec((t, tc), lambda j: (0, nb + j)), pl.BlockSpec((3, tc), lambda j: (0, j)),
                  pl.BlockSpec((1, tc), lambda j: (0, j)), col],
        out_specs=[col, col, pl.BlockSpec((3, tc), lambda j: (0, j)), pl.BlockSpec((1, tc), lambda j: (0, j))],
        out_shape=[jax.ShapeDtypeStruct((t, f), BF16), jax.ShapeDtypeStruct((t, f), BF16),
                   jax.ShapeDtypeStruct((3, f), F32), jax.ShapeDtypeStruct((1, f), F32)],
        compiler_params=_params(("parallel",)),
    )(u, u, cw, cb, dg)


def final_loss(x, target, g, n_lat):
    tm = ROW_TILE
    d = x.shape[1]

    def body(x_ref, t_ref, g_ref, loss_ref, dx_ref, dg_ref):
        i = pl.program_id(0)
        tgt = t_ref[...]

        def f(xv, gv):
            e = rms(xv) * gv - tgt
            s = jnp.sum(jnp.sum(e * e, axis=1, keepdims=True), axis=0, keepdims=True)
            return s * (0.5 / d)

        val, vjp = jax.vjp(f, x_ref[...], g_ref[...])
        dx, dgv = vjp(jnp.ones((1, 1), F32))
        dx_ref[...] = dx

        @pl.when(i == 0)
        def _():
            dg_ref[...] = dgv
            loss_ref[...] = jnp.broadcast_to(val, loss_ref.shape)

        @pl.when(i != 0)
        def _():
            dg_ref[...] += dgv
            loss_ref[...] += jnp.broadcast_to(val, loss_ref.shape)

    return pl.pallas_call(
        body, name="final_loss", grid=(n_lat // tm,),
        in_specs=[pl.BlockSpec((tm, d), lambda i: (i, 0)), pl.BlockSpec((tm, d), lambda i: (i, 0)),
                  pl.BlockSpec((1, d), lambda i: (0, 0))],
        out_specs=[pl.BlockSpec((1, LANES), lambda i: (0, 0)), pl.BlockSpec((tm, d), lambda i: (i, 0)),
                   pl.BlockSpec((1, d), lambda i: (0, 0))],
        out_shape=[jax.ShapeDtypeStruct((1, LANES), F32), jax.ShapeDtypeStruct((n_lat, d), F32),
                   jax.ShapeDtypeStruct((1, d), F32)],
        compiler_params=_params(("arbitrary",)),
    )(x, target, g)


ATT_SCALE = HEAD_DIM ** -0.5
_AK_BLK = P_AK // HEAD_DIM
_AV_BLK = Z_AV // HEAD_DIM


def _att_specs(t, tq):
    gw = ATT_GROUP * HEAD_DIM
    q_spec = pl.BlockSpec((tq, gw), lambda kv, i: (i, kv))
    k_spec = pl.BlockSpec((t, HEAD_DIM), lambda kv, i: (0, _AK_BLK + kv))
    v_spec = pl.BlockSpec((t, HEAD_DIM), lambda kv, i: (0, _AV_BLK + kv))
    row_spec = pl.BlockSpec((ATT_GROUP, tq, 1), lambda kv, i: (kv, i, 0))
    return q_spec, k_spec, v_spec, row_spec


def _att_mask(i, t, tq, n_lat):
    col = lax.broadcasted_iota(jnp.int32, (1, t), 1)
    return jnp.where((i >= n_lat // tq) & (col < n_lat), -jnp.inf, 0.0).astype(F32)


def attn_fwd(p, z, n_lat):
    t = p.shape[0]
    tq = ROW_TILE

    def body(q_ref, k_ref, v_ref, o_ref, lse_ref):
        mask = _att_mask(pl.program_id(1), t, tq, n_lat)
        k, v = k_ref[...].astype(BF16), v_ref[...].astype(BF16)
        for g in range(ATT_GROUP):
            cols = slice(g * HEAD_DIM, (g + 1) * HEAD_DIM)
            s = _dg(q_ref[:, cols], k, _NT) * ATT_SCALE + mask
            m = jnp.max(s, axis=1, keepdims=True)
            pr = jnp.exp(s - m)
            l = jnp.sum(pr, axis=1, keepdims=True)
            o_ref[:, cols] = _dg(pr, v, _NN) / l
            lse_ref[g] = m + jnp.log(l)

    q_spec, k_spec, v_spec, row_spec = _att_specs(t, tq)
    return pl.pallas_call(
        body, name="attn_fwd", grid=(ATT_KV_HEADS, t // tq),
        in_specs=[q_spec, k_spec, v_spec], out_specs=[q_spec, row_spec],
        out_shape=[jax.ShapeDtypeStruct((t, ATT_Q_HEADS * HEAD_DIM), F32),
                   jax.ShapeDtypeStruct((ATT_Q_HEADS, t, 1), F32)],
        compiler_params=_params(("parallel", "parallel")),
    )(p, p, z)


def attn_bwd(p, z, o, lse, do, n_lat):
    t = p.shape[0]
    tq = ROW_TILE

    def body(q_ref, k_ref, v_ref, o_ref, do_ref, lse_ref, dq_ref, dk_ref, dv_ref):
        i = pl.program_id(1)

        @pl.when(i == 0)
        def _():
            dk_ref[...] = jnp.zeros_like(dk_ref)
            dv_ref[...] = jnp.zeros_like(dv_ref)

        mask = _att_mask(i, t, tq, n_lat)
        k, v = k_ref[...].astype(BF16), v_ref[...].astype(BF16)
        dk, dv = dk_ref[...], dv_ref[...]
        for g in range(ATT_GROUP):
            cols = slice(g * HEAD_DIM, (g + 1) * HEAD_DIM)
            q, do_g = q_ref[:, cols].astype(BF16), do_ref[:, cols]
            pr = jnp.exp(_dg(q, k, _NT) * ATT_SCALE + mask - lse_ref[g])
            delta = jnp.sum(o_ref[:, cols] * do_g, axis=1, keepdims=True)
            ds = pr * (_dg(do_g, v, _NT) - delta) * ATT_SCALE
            dq_ref[:, cols] = _dg(ds, k, _NN)
            dk = dk + _dg(ds, q, _TN)
            dv = dv + _dg(pr, do_g, _TN)
        dk_ref[...], dv_ref[...] = dk, dv

    q_spec, k_spec, v_spec, row_spec = _att_specs(t, tq)
    kv_out = pl.BlockSpec((t, HEAD_DIM), lambda kv, i: (0, kv))
    return pl.pallas_call(
        body, name="attn_bwd", grid=(ATT_KV_HEADS, t // tq),
        in_specs=[q_spec, k_spec, v_spec, q_spec, q_spec, row_spec],
        out_specs=[q_spec, kv_out, kv_out],
        out_shape=[jax.ShapeDtypeStruct((t, ATT_Q_HEADS * HEAD_DIM), F32),
                   jax.ShapeDtypeStruct((t, ATT_KV_HEADS * HEAD_DIM), F32),
                   jax.ShapeDtypeStruct((t, ATT_KV_HEADS * HEAD_DIM), F32)],
        compiler_params=_params(("parallel", "arbitrary")),
    )(p, p, z, o, do, lse)


_RQ_BLK = P_RQ // HEAD_DIM
_RK_BLK = P_RK // HEAD_DIM
_RV_BLK = Z_RV // HEAD_DIM


def _scan_chunk(direction, step, n_chunks, n_lat_chunks):
    return jnp.where(direction == 0, (step + n_lat_chunks) % n_chunks, n_chunks - 1 - step)


def _ret_geometry(direction):
    c = RET_CHUNK
    i = lax.broadcasted_iota(jnp.int32, (c, c), 0)
    j = lax.broadcasted_iota(jnp.int32, (c, c), 1)
    rel = jnp.where(direction == 0, i - j, j - i).astype(F32)
    r = lax.broadcasted_iota(jnp.int32, (c, 1), 0)
    pos = jnp.where(direction == 0, r, c - 1 - r).astype(F32)
    return rel, pos


def ret_chunk(q, k, v, s, lg, rel, pos):
    c = RET_CHUNK
    causal = rel >= 0
    d_in = jnp.where(causal, jnp.exp(lg * jnp.where(causal, rel, 0.0)), 0.0)
    q_dec = jnp.exp(lg * (pos + 1.0))
    k_dec = jnp.exp(lg * (c - 1.0 - pos))
    c_dec = jnp.exp(lg * c)
    att = bdot_nt(q, k) * d_in
    o = bdot(att, v) + bdot(q * q_dec, s)
    s_new = c_dec * s + bdot_tn(k * k_dec, v)
    return o, s_new


def ret_fwd(p, z, lg, n_lat):
    t = p.shape[0]
    c = RET_CHUNK
    nc, nlc = t // c, n_lat // c

    def body(q_ref, k_ref, v_ref, lg_ref, o_ref, ssave_ref, s_s):
        d, n = pl.program_id(0), pl.program_id(1)

        @pl.when(n == 0)
        def _():
            s_s[...] = jnp.zeros_like(s_s)

        rel, pos = _ret_geometry(d)
        for h in range(RET_HEADS):
            cols = slice(h * HEAD_DIM, (h + 1) * HEAD_DIM)
            ssave_ref[0, h, 0] = s_s[h]
            o, s_new = ret_chunk(q_ref[:, cols], k_ref[:, cols], v_ref[:, cols], s_s[h], lg_ref[0, h], rel, pos)
            o_ref[:, cols] = o
            s_s[h] = s_new

    w = RET_HEADS * HEAD_DIM

    def blk(base):
        return pl.BlockSpec((c, w), lambda d, n: (_scan_chunk(d, n, nc, nlc), base // RET_HEADS))

    return pl.pallas_call(
        body, name="ret_fwd", grid=(2, nc),
        in_specs=[blk(_RQ_BLK), blk(_RK_BLK), blk(_RV_BLK), pl.BlockSpec((1, RET_HEADS, 1, 1), lambda d, n: (d, 0, 0, 0))],
        out_specs=[pl.BlockSpec((c, w), lambda d, n: (_scan_chunk(d, n, nc, nlc), d)),
                   pl.BlockSpec((1, RET_HEADS, 1, HEAD_DIM, HEAD_DIM), lambda d, n: (d, 0, n, 0, 0))],
        out_shape=[jax.ShapeDtypeStruct((t, 2 * w), F32),
                   jax.ShapeDtypeStruct((2, RET_HEADS, nc, HEAD_DIM, HEAD_DIM), F32)],
        scratch_shapes=[pltpu.VMEM((RET_HEADS, HEAD_DIM, HEAD_DIM), F32)],
        compiler_params=_params(("parallel", "arbitrary")),
    )(p, p, z, lg)


def ret_bwd(p, z, lg, states, do, n_lat):
    t = p.shape[0]
    c = RET_CHUNK
    nc, nlc = t // c, n_lat // c

    def body(q_ref, k_ref, v_ref, lg_ref, s_ref, do_ref, dq_ref, dk_ref, dv_ref, dlg_ref, ds_s):
        d, n = pl.program_id(0), pl.program_id(1)

        @pl.when(n == 0)
        def _():
            ds_s[...] = jnp.zeros_like(ds_s)
            dlg_ref[...] = jnp.zeros_like(dlg_ref)

        rel, pos = _ret_geometry(d)
        f = functools.partial(ret_chunk, rel=rel, pos=pos)
        for h in range(RET_HEADS):
            cols = slice(h * HEAD_DIM, (h + 1) * HEAD_DIM)
            _, vjp = jax.vjp(f, q_ref[:, cols], k_ref[:, cols], v_ref[:, cols], s_ref[0, h, 0], lg_ref[0, h])
            dq, dk, dv, ds, dlg = vjp((do_ref[:, cols], ds_s[h]))
            dq_ref[:, cols], dk_ref[:, cols], dv_ref[:, cols] = dq, dk, dv
            ds_s[h] = ds
            dlg_ref[0, h] += dlg

    def chunk_of(d, n):
        return _scan_chunk(d, nc - 1 - n, nc, nlc)

    w = RET_HEADS * HEAD_DIM

    def blk(base):
        return pl.BlockSpec((c, w), lambda d, n: (chunk_of(d, n), base // RET_HEADS))

    out_blk = pl.BlockSpec((c, w), lambda d, n: (chunk_of(d, n), d))
    lg_blk = pl.BlockSpec((1, RET_HEADS, 1, 1), lambda d, n: (d, 0, 0, 0))
    grad_shape = jax.ShapeDtypeStruct((t, 2 * w), F32)
    return pl.pallas_call(
        body, name="ret_bwd", grid=(2, nc),
        in_specs=[blk(_RQ_BLK), blk(_RK_BLK), blk(_RV_BLK), lg_blk,
                  pl.BlockSpec((1, RET_HEADS, 1, HEAD_DIM, HEAD_DIM), lambda d, n: (d, 0, nc - 1 - n, 0, 0)),
                  pl.BlockSpec((c, w), lambda d, n: (chunk_of(d, n), 0))],
        out_specs=[out_blk, out_blk, out_blk, lg_blk],
        out_shape=[grad_shape, grad_shape, grad_shape, jax.ShapeDtypeStruct((2, RET_HEADS, 1, 1), F32)],
        scratch_shapes=[pltpu.VMEM((RET_HEADS, HEAD_DIM, HEAD_DIM), F32)],
        compiler_params=_params(("parallel", "arbitrary")),
    )(p, p, z, lg, states, do)


_GQ_BLK = P_GQ // (GLA_HEADS * GLA_DK)
_GK_BLK = Z_GK // (GLA_HEADS * GLA_DK)
_GV_BLK = Z_GV // (GLA_HEADS * GLA_DV)
_LA_BLK = P_LA // (GLA_HEADS * GLA_DK)


def _gla_mask(direction):
    c = GLA_CHUNK
    i = lax.broadcasted_iota(jnp.int32, (c, c), 0)
    j = lax.broadcasted_iota(jnp.int32, (c, c), 1)
    return (jnp.where(direction == 0, i - j, j - i) >= 0).astype(F32)


def gla_chunk(q, k, v, la, st, mask):
    b = mask_cumsum(mask, la)
    btot = jnp.sum(la, axis=0, keepdims=True)
    half = 0.5 * btot
    qt, kt = q * jnp.exp(b - half), k * jnp.exp(half - b)
    qs, ke = q * jnp.exp(b), k * jnp.exp(btot - b)
    outs, upd = [], []
    for h in range(GLA_HEADS):
        ks = slice(h * GLA_DK, (h + 1) * GLA_DK)
        vh = v[:, h * GLA_DV:(h + 1) * GLA_DV]
        att = bdot_nt(qt[:, ks], kt[:, ks]) * mask
        outs.append(bdot(att, vh) + bdot_nt(qs[:, ks], st[:, ks]))
        upd.append(bdot_tn(vh, ke[:, ks]))
    st_new = st * jnp.exp(btot) + jnp.concatenate(upd, axis=1)
    return jnp.concatenate(outs, axis=1), st_new


def gla_fwd(p, z, n_lat):
    t = p.shape[0]
    c = GLA_CHUNK
    nc, nlc = t // c, n_lat // c
    kw, vw = GLA_HEADS * GLA_DK, GLA_HEADS * GLA_DV

    def body(q_ref, k_ref, v_ref, la_ref, o_ref, ssave_ref, s_s):
        d, n = pl.program_id(0), pl.program_id(1)

        @pl.when(n == 0)
        def _():
            s_s[...] = jnp.zeros_like(s_s)

        ssave_ref[0, 0] = s_s[...]
        o, s_new = gla_chunk(q_ref[...], k_ref[...], v_ref[...], la_ref[...], s_s[...], _gla_mask(d))
        o_ref[...] = o
        s_s[...] = s_new

    def chunk_of(d, n):
        return _scan_chunk(d, n, nc, nlc)

    return pl.pallas_call(
        body, name="gla_fwd", grid=(2, nc),
        in_specs=[pl.BlockSpec((c, kw), lambda d, n: (chunk_of(d, n), _GQ_BLK)),
                  pl.BlockSpec((c, kw), lambda d, n: (chunk_of(d, n), _GK_BLK)),
                  pl.BlockSpec((c, vw), lambda d, n: (chunk_of(d, n), _GV_BLK)),
                  pl.BlockSpec((c, kw), lambda d, n: (chunk_of(d, n), _LA_BLK + d))],
        out_specs=[pl.BlockSpec((c, vw), lambda d, n: (chunk_of(d, n), d)),
                   pl.BlockSpec((1, 1, GLA_DV, kw), lambda d, n: (d, n, 0, 0))],
        out_shape=[jax.ShapeDtypeStruct((t, 2 * vw), F32), jax.ShapeDtypeStruct((2, nc, GLA_DV, kw), F32)],
        scratch_shapes=[pltpu.VMEM((GLA_DV, kw), F32)],
        compiler_params=_params(("parallel", "arbitrary")),
    )(p, z, z, p)


def gla_bwd(p, z, states, do, n_lat):
    t = p.shape[0]
    c = GLA_CHUNK
    nc, nlc = t // c, n_lat // c
    kw, vw = GLA_HEADS * GLA_DK, GLA_HEADS * GLA_DV

    def body(q_ref, k_ref, v_ref, la_ref, s_ref, do_ref, dq_ref, dk_ref, dv_ref, dla_ref, ds_s):
        d, n = pl.program_id(0), pl.program_id(1)

        @pl.when(n == 0)
        def _():
            ds_s[...] = jnp.zeros_like(ds_s)

        f = functools.partial(gla_chunk, mask=_gla_mask(d))
        _, vjp = jax.vjp(f, q_ref[...], k_ref[...], v_ref[...], la_ref[...], s_ref[0, 0])
        dq_ref[...], dk_ref[...], dv_ref[...], dla_ref[...], ds_s[...] = vjp((do_ref[...], ds_s[...]))

    def chunk_of(d, n):
        return _scan_chunk(d, nc - 1 - n, nc, nlc)

    k_out = pl.BlockSpec((c, kw), lambda d, n: (chunk_of(d, n), d))
    return pl.pallas_call(
        body, name="gla_bwd", grid=(2, nc),
        in_specs=[pl.BlockSpec((c, kw), lambda d, n: (chunk_of(d, n), _GQ_BLK)),
                  pl.BlockSpec((c, kw), lambda d, n: (chunk_of(d, n), _GK_BLK)),
                  pl.BlockSpec((c, vw), lambda d, n: (chunk_of(d, n), _GV_BLK)),
                  pl.BlockSpec((c, kw), lambda d, n: (chunk_of(d, n), _LA_BLK + d)),
                  pl.BlockSpec((1, 1, GLA_DV, kw), lambda d, n: (d, nc - 1 - n, 0, 0)),
                  pl.BlockSpec((c, vw), lambda d, n: (chunk_of(d, n), 0))],
        out_specs=[k_out, k_out, pl.BlockSpec((c, vw), lambda d, n: (chunk_of(d, n), d)), k_out],
        out_shape=[jax.ShapeDtypeStruct((t, 2 * kw), F32), jax.ShapeDtypeStruct((t, 2 * kw), F32),
                   jax.ShapeDtypeStruct((t, 2 * vw), F32), jax.ShapeDtypeStruct((t, 2 * kw), F32)],
        scratch_shapes=[pltpu.VMEM((GLA_DV, kw), F32)],
        compiler_params=_params(("parallel", "arbitrary")),
    )(p, z, z, p, states, do)


def _adam_tile(w, g, m, v):
    m = ADAM_B1 * m + (1.0 - ADAM_B1) * g
    v = ADAM_B2 * v + (1.0 - ADAM_B2) * (g * g)
    m_hat = m / (1.0 - ADAM_B1 ** ADAM_STEP)
    v_hat = v / (1.0 - ADAM_B2 ** ADAM_STEP)
    delta = -ADAM_LR * (m_hat / (jnp.sqrt(v_hat) + ADAM_EPS) + ADAM_WD * w)
    return delta, m, v


def adamw(name, w, g, m, v):
    shape = w.shape
    cols = shape[-1] if w.ndim > 1 and shape[-1] >= LANES else int(np.prod(shape))
    rows = int(np.prod(shape)) // cols
    tr = rows
    for cand in (512, 256, 128, 64, 32, 16, 8):
        if rows % cand == 0 and cand * cols * 4 <= (1 << 20):
            tr = cand
            break
    flat = [a.reshape(rows, cols) for a in (w, g, m, v)]

    def body(w_ref, g_ref, m_ref, v_ref, d_ref, mo_ref, vo_ref):
        d_ref[...], mo_ref[...], vo_ref[...] = _adam_tile(w_ref[...], g_ref[...], m_ref[...], v_ref[...])

    spec = pl.BlockSpec((tr, cols), lambda i: (i, 0))
    outs = pl.pallas_call(
        body, name=name, grid=(rows // tr,),
        in_specs=[spec] * 4, out_specs=[spec] * 3,
        out_shape=[jax.ShapeDtypeStruct((rows, cols), F32)] * 3,
        compiler_params=_params(("parallel",)),
    )(*flat)
    return tuple(o.reshape(shape) for o in outs)


MESH = pl.DeviceIdType.MESH
_HBM = pl.BlockSpec(memory_space=pltpu.HBM)
N_CHIPS = 4
N_DEV = 8


def _place():
    x, y, c = lax.axis_index("x"), lax.axis_index("y"), lax.axis_index("c")
    chips = [(1 - x, y), (x, 1 - y), (1 - x, 1 - y)]
    return x, y, c, chips


def _remote(src, dst, send_sem, recv_sem, to):
    return pltpu.make_async_remote_copy(src_ref=src, dst_ref=dst, send_sem=send_sem, recv_sem=recv_sem,
                                        device_id=to, device_id_type=MESH)


def all_gather_small(name, v):
    m_per, n = v.shape

    def body(x_ref, out_ref, send_sems, recv_sems, local_sem):
        x, y, c, chips = _place()
        me, sibling = (x, y, c), (x, y, 1 - c)

        def rows(px, py, pc):
            return out_ref.at[pl.ds((4 * px + 2 * py + pc) * m_per, m_per), :]

        def copy(k, block, to, src=None):
            return _remote(rows(*block) if src is None else src, rows(*block), send_sems.at[k], recv_sems.at[k], to)

        mine = pltpu.make_async_copy(x_ref, rows(*me), local_sem)
        mine.start()
        first = [copy(0, me, sibling, src=x_ref)]
        first += [copy(1 + j, me, (*chip, c), src=x_ref) for j, chip in enumerate(chips)]
        for cp in first:
            cp.start()
        passed = [copy(4 + j, (*chip, c), sibling) for j, chip in enumerate(chips)]
        for j, chip in enumerate(chips):
            copy(1 + j, (*chip, c), me).wait_recv()
            passed[j].start()
        copy(0, sibling, me).wait_recv()
        for j, chip in enumerate(chips):
            copy(4 + j, (*chip, 1 - c), me).wait_recv()
        for cp in first + passed:
            cp.wait_send()
        mine.wait()

    return pl.pallas_call(
        body, name=name,
        out_shape=jax.ShapeDtypeStruct((N_DEV * m_per, n), v.dtype),
        in_specs=[pl.BlockSpec(memory_space=pltpu.VMEM)],
        out_specs=pl.BlockSpec(memory_space=pltpu.VMEM),
        scratch_shapes=[pltpu.SemaphoreType.DMA((7,)), pltpu.SemaphoreType.DMA((7,)), pltpu.SemaphoreType.DMA],
        compiler_params=pltpu.CompilerParams(vmem_limit_bytes=VMEM_LIMIT),
    )(v)


_SEM = pl.BlockSpec(memory_space=pltpu.SEMAPHORE)
_SPLIT_COPY = pltpu.CompilerParams(has_side_effects=pltpu.SideEffectType.DATAFLOW_SIDE_EFFECTING)


def gather_copies(x_ref, land_ref, q, c, chips):
    half = x_ref.shape[0] // 2
    rows = pl.ds(c * half, half)
    return [(x_ref.at[rows, :], land_ref.at[q, rows, :], (*chip, c), land_ref.at[2 * chip[0] + chip[1], rows, :])
            for chip in chips]


def scatter_copies(s_ref, land_ref, q, c, chips):
    return [(s_ref.at[2 * chip[0] + chip[1]], land_ref.at[j], (*chip, c), land_ref.at[j]) for j, chip in enumerate(chips)]


def split_start(name, copies, srcs, land_shapes):
    nt = len(srcs)

    def body(*refs):
        x_refs, land_refs = refs[:nt], refs[nt:2 * nt]
        send, recv = refs[2 * nt:3 * nt], refs[3 * nt:4 * nt]
        x, y, c, chips = _place()
        for t in range(nt):
            for j, (src, dst, to, _) in enumerate(copies(x_refs[t], land_refs[t], 2 * x + y, c, chips)):
                _remote(src, dst, send[t].at[j], recv[t].at[j], to).start()
        refs[-1][...] = jnp.zeros_like(refs[-1])

    lands = [pltpu.with_memory_space_constraint(lax.empty(shape, s.dtype), pltpu.HBM) for shape, s in zip(land_shapes, srcs)]
    outs = pl.pallas_call(
        body, name=name,
        out_shape=tuple([pltpu.SemaphoreType.DMA((3,))] * (2 * nt) + [pltpu.HBM(s.shape, s.dtype) for s in srcs]
                        + [pltpu.HBM(l.shape, l.dtype) for l in lands] + [jax.ShapeDtypeStruct((8, LANES), F32)]),
        in_specs=[_HBM] * (2 * nt),
        out_specs=tuple([_SEM] * (2 * nt) + [_HBM] * (2 * nt) + [pl.BlockSpec(memory_space=pltpu.VMEM)]),
        input_output_aliases={i: 2 * nt + i for i in range(2 * nt)},
        compiler_params=_SPLIT_COPY,
    )(*[pltpu.with_memory_space_constraint(s, pltpu.HBM) for s in srcs], *lands)
    groups = [(outs[t], outs[nt + t], outs[2 * nt + t], outs[3 * nt + t]) for t in range(nt)]
    return groups, outs[-1]


def split_wait(name, copies, group, after):
    send, recv, src, land = group

    def body(x_ref, land_ref, send_sem, recv_sem, after_ref, x_out, land_out):
        x, y, c, chips = _place()
        for j, (s, _, to, arrival) in enumerate(copies(x_ref, land_ref, 2 * x + y, c, chips)):
            cp = _remote(s, arrival, send_sem.at[j], recv_sem.at[j], to)
            cp.wait_send()
            cp.wait_recv()

    return pl.pallas_call(
        body, name=name,
        out_shape=(pltpu.HBM(src.shape, src.dtype), pltpu.HBM(land.shape, land.dtype)),
        in_specs=(_HBM, _HBM, _SEM, _SEM, pl.BlockSpec(memory_space=pl.ANY)), out_specs=(_HBM, _HBM),
        input_output_aliases={0: 0, 1: 1}, compiler_params=_SPLIT_COPY,
    )(src, land, send, recv, after)


def forward_to_sibling(name, land):
    def body(_, out_ref, send_sems, recv_sems):
        x, y, c, chips = _place()
        sibling = (x, y, 1 - c)
        half = out_ref.shape[1] // 2
        cps = []
        for j, chip in enumerate(chips):
            part = out_ref.at[2 * chip[0] + chip[1], pl.ds(c * half, half), :]
            cp = _remote(part, part, send_sems.at[j], recv_sems.at[j], sibling)
            cp.start()
            cps.append(cp)
        for j, chip in enumerate(chips):
            other = out_ref.at[2 * chip[0] + chip[1], pl.ds((1 - c) * half, half), :]
            _remote(other, other, send_sems.at[j], recv_sems.at[j], sibling).wait_recv()
        for cp in cps:
            cp.wait_send()

    return pl.pallas_call(
        body, name=name, out_shape=jax.ShapeDtypeStruct(land.shape, land.dtype),
        in_specs=[_HBM], out_specs=_HBM, input_output_aliases={0: 0},
        scratch_shapes=[pltpu.SemaphoreType.DMA((3,)), pltpu.SemaphoreType.DMA((3,))],
    )(land)


def exchange_sibling_halves(name, grads):
    nt = len(grads)

    def body(*refs):
        g_refs, out_refs = refs[:nt], refs[nt:2 * nt]
        send_sems, recv_sems = refs[2 * nt:]
        x, y, c, _ = _place()
        sibling = (x, y, 1 - c)
        cps = []
        for t in range(nt):
            half = g_refs[t].shape[1] // 2
            cp = _remote(g_refs[t].at[:, pl.ds((1 - c) * half, half), :], out_refs[t], send_sems.at[t], recv_sems.at[t], sibling)
            cp.start()
            cps.append(cp)
        for cp in cps:
            cp.wait()

    return pl.pallas_call(
        body, name=name,
        out_shape=[jax.ShapeDtypeStruct((g.shape[0], g.shape[1] // 2, g.shape[2]), g.dtype) for g in grads],
        in_specs=[_HBM] * nt, out_specs=[_HBM] * nt,
        scratch_shapes=[pltpu.SemaphoreType.DMA((nt,)), pltpu.SemaphoreType.DMA((nt,))],
    )(*grads)


def join_sibling_halves(name, halves):
    nt = len(halves)

    def body(*refs):
        out_refs = refs[nt:2 * nt]
        send_sems, recv_sems = refs[2 * nt:]
        x, y, c, _ = _place()
        sibling = (x, y, 1 - c)
        cps = []
        for t in range(nt):
            half = out_refs[t].shape[0] // 2
            mine = out_refs[t].at[pl.ds(c * half, half), :]
            cp = _remote(mine, mine, send_sems.at[t], recv_sems.at[t], sibling)
            cp.start()
            cps.append(cp)
        for t, cp in enumerate(cps):
            half = out_refs[t].shape[0] // 2
            other = out_refs[t].at[pl.ds((1 - c) * half, half), :]
            _remote(other, other, send_sems.at[t], recv_sems.at[t], sibling).wait_recv()
            cp.wait_send()

    return pl.pallas_call(
        body, name=name,
        out_shape=[jax.ShapeDtypeStruct(h.shape, h.dtype) for h in halves],
        in_specs=[_HBM] * nt, out_specs=[_HBM] * nt,
        input_output_aliases={t: t for t in range(nt)},
        scratch_shapes=[pltpu.SemaphoreType.DMA((nt,)), pltpu.SemaphoreType.DMA((nt,))],
    )(*halves)


def _rows_tile(rows, cols):
    for cand in (512, 256, 128, 64, 32, 16):
        if rows % cand == 0 and cand * cols * 4 <= (1 << 20):
            return cand
    return rows


def add_sibling_half(name, pieces, from_sibling, core):
    n, h, cols = from_sibling.shape
    tr = _rows_tile(h, cols)
    nb = h // tr

    def body(c_ref, a_ref, b_ref, o_ref):
        o_ref[...] = (a_ref[...].astype(F32) + b_ref[...].astype(F32)).astype(o_ref.dtype)

    blk = pl.BlockSpec((1, tr, cols), lambda q, i, c_ref: (q, i, 0))
    return pl.pallas_call(
        body, name=name,
        grid_spec=pltpu.PrefetchScalarGridSpec(
            num_scalar_prefetch=1, grid=(n, nb),
            in_specs=[pl.BlockSpec((1, tr, cols), lambda q, i, c_ref: (q, c_ref[0] * nb + i, 0)), blk], out_specs=blk),
        out_shape=jax.ShapeDtypeStruct((n, h, cols), BF16),
        compiler_params=_params(("parallel", "parallel")),
    )(core.reshape(1).astype(jnp.int32), pieces, from_sibling)


def add_chip_sums(name, chip_sums, from_chips, chip, core):
    _, h, cols = chip_sums.shape
    tr = _rows_tile(h, cols)
    nb = h // tr

    def body(s_ref, own_ref, r0_ref, r1_ref, r2_ref, o_ref):
        acc = own_ref[0].astype(F32) + r0_ref[0].astype(F32)
        o_ref[...] = acc + r1_ref[0].astype(F32) + r2_ref[0].astype(F32)

    def got(j):
        return pl.BlockSpec((1, tr, cols), lambda i, s_ref: (j, i, 0))

    return pl.pallas_call(
        body, name=name,
        grid_spec=pltpu.PrefetchScalarGridSpec(
            num_scalar_prefetch=1, grid=(nb,),
            in_specs=[pl.BlockSpec((1, tr, cols), lambda i, s_ref: (s_ref[0], i, 0)), got(0), got(1), got(2)],
            out_specs=pl.BlockSpec((tr, cols), lambda i, s_ref: (s_ref[1] * nb + i, 0))),
        out_shape=jax.ShapeDtypeStruct((2 * h, cols), F32),
        compiler_params=_params(("parallel",)),
    )(jnp.stack([chip, core]).astype(jnp.int32), chip_sums, from_chips, from_chips, from_chips)


def sum_device_blocks(name, g):
    n = g.shape[1]

    def body(g_ref, o_ref):
        acc = g_ref[0:8, :]
        for d in range(1, N_DEV):
            acc = acc + g_ref[8 * d:8 * (d + 1), :]
        o_ref[...] = acc

    return pl.pallas_call(body, name=name, out_shape=jax.ShapeDtypeStruct((8, n), F32),
                          compiler_params=pltpu.CompilerParams(vmem_limit_bytes=VMEM_LIMIT))(g)


class LayerWeights(NamedTuple):
    norm1_g: jax.Array
    q_g: jax.Array
    k_g: jax.Array
    lg: jax.Array
    ret_g: jax.Array
    gate_up: jax.Array
    gate_b: jax.Array
    gla_g: jax.Array
    norm2_g: jax.Array
    conv_w: jax.Array
    conv_b: jax.Array


def _mod(mods, k):
    return mods[:, k:k + 1, :]


def out_view(l, tb):
    rows = D_MODEL // N_CHIPS
    if tb:
        return BView(n=D_MODEL, k=D_MODEL, tn=rows, tk=D_MODEL, index_map=lambda i, j, kk: (j, l, kk))
    return BView(n=D_MODEL, k=D_MODEL, tn=1024, tk=rows, index_map=lambda i, j, kk: (kk, l, j))


def down_view(l, f, tb):
    rows = f // N_CHIPS
    if tb:
        return BView(n=f, k=D_MODEL, tn=rows, tk=D_MODEL, index_map=lambda i, j, kk: (j, l, kk))
    return BView(n=D_MODEL, k=f, tn=1024, tk=rows, index_map=lambda i, j, kk: (kk, l, j))


def up_view(l, f, part=None):
    cols = 2 * f // N_CHIPS
    tc = _pick(cols, (1408, 1024, 512, 256))
    nbc = cols // tc
    if part is None:
        return BView(n=2 * f, k=D_MODEL, tn=tc, tk=D_MODEL, index_map=lambda i, j, kk: (j // nbc, l, j % nbc))
    nnb = D_MODEL // 1024
    return BView(n=D_MODEL, k=f, tn=1024, tk=tc, index_map=lambda i, j, kk: (2 * part + kk // nbc, l * nnb + j, kk % nbc))


def ada_view(l, n_ada, tb):
    if tb:
        return BView(n=D_MODEL, k=n_ada, tn=1024, tk=n_ada, index_map=lambda i, j, kk: (l, j, 0))
    return BView(n=n_ada, k=D_MODEL, tn=1024, tk=D_MODEL, index_map=lambda i, j, kk: (l, 0, j))


def _prep_args(z, zg, cos, sin, w):
    rows = [Row(z, Z_AV, 0), Row(z, 512, Z_RQ // 512), Row(z, 512, Z_RK // 512), Row(z, 256, Z_GQ // 256),
            Row(zg, LANES, 0), Row(cos, HEAD_DIM, 0, False), Row(sin, HEAD_DIM, 0, False)]
    return rows, [Par(w.q_g), Par(w.k_g), Par(w.gate_up), Par(w.gate_b)]


def _post_args(o_att, o_ret, o_gla, z, w):
    rows = [Row(o_att, 1024), Row(o_ret, 512, 0), Row(o_ret, 512, 1, False), Row(o_gla, 512, 0), Row(o_gla, 512, 1, False),
            Row(z, 512, Z_RG // 512), Row(z, 512, Z_GR // 512)]
    return rows, [Par(w.ret_g), Par(w.gla_g)]


def layer_fwd(l, xs, mods, w, fetch, cos, sin, n_lat):
    t, d = xs.shape
    tag = f"l{l}_"
    nm1 = [Par(w.norm1_g), Par(_mod(mods, 0), True), Par(_mod(mods, 1), True)]
    (h,) = row_map(tag + "norm1", normmod_tile, [Row(xs, d)], nm1, [(d, BF16)], t, n_lat)
    w_main, w_gate = fetch("w_in", h)
    z = matmul(tag + "in_proj", h, w_main)
    zg = matmul(tag + "gate_proj", h, w_gate)
    rows, pars = _prep_args(z, zg, cos, sin, w)
    (p,) = row_map(tag + "prep", prep_tile, rows, pars, [(P_W, F32)], t, n_lat)
    o_att, lse = attn_fwd(p, z, n_lat)
    o_ret, s_ret = ret_fwd(p, z, w.lg, n_lat)
    o_gla, s_gla = gla_fwd(p, z, n_lat)
    rows, pars = _post_args(o_att, o_ret, o_gla, z, w)
    (m,) = row_map(tag + "post", post_tile, rows, pars, [(d, BF16)], t, n_lat)
    g_out = fetch("w_out", m)
    y = matmul(tag + "out_proj", m, g_out, view=out_view(0, False))
    (x1,) = row_map(tag + "resid1", resid_tile, [Row(xs, d), Row(y, d)], [Par(_mod(mods, 2), True)], [(d, F32)], t, n_lat)
    nm2 = [Par(w.norm2_g), Par(_mod(mods, 3), True), Par(_mod(mods, 4), True)]
    (h2,) = row_map(tag + "norm2", normmod_tile, [Row(x1, d)], nm2, [(d, BF16)], t, n_lat)
    f = w.conv_b.shape[1]
    g_up = fetch("w_up", h2)
    u = matmul(tag + "up_proj", h2, g_up, view=up_view(0, f))
    g = convglu(tag + "convglu", u, w.conv_w, w.conv_b, n_lat)
    g_down = fetch("w_down", g)
    yd = matmul(tag + "down_proj", g, g_down, view=down_view(0, f, False))
    (x2,) = row_map(tag + "resid2", resid_tile, [Row(x1, d), Row(yd, d)], [Par(_mod(mods, 5), True)], [(d, F32)], t, n_lat)
    saved = dict(xs=xs, h=h, z=z, zg=zg, p=p, o_att=o_att, lse=lse, o_ret=o_ret, s_ret=s_ret, o_gla=o_gla, s_gla=s_gla,
                 m=m, y=y, x1=x1, h2=h2, u=u, g=g, yd=yd, w_main=w_main, w_gate=w_gate, g_out=g_out, g_up=g_up, g_down=g_down)
    return x2, saved


def _sum_dirs(a):
    w = a.shape[1] // 2
    return a[:, :w] + a[:, w:]


def layer_bwd(l, dx2, s, mods, w, cos, sin, n_lat, grad_ready):
    t, d = dx2.shape
    tag = f"l{l}_b_"
    dyd, dgate5 = row_vjp(tag + "resid2", resid_tile, [Row(s["x1"], d, 0, False), Row(s["yd"], d)],
                          [Par(_mod(mods, 5), True)], [dx2], t, n_lat, row_grad_dtype=BF16)
    f = w.conv_b.shape[1]
    dg = matmul(tag + "down_dx", dyd, s["g_down"], tb=True, view=down_view(0, f, True))
    dw_down = matmul(tag + "down_dw", s["g"], dyd, ta=True, out_dtype=BF16)
    da, dv, dcw, dcb = convglu_bwd(tag + "convglu", s["u"], w.conv_w, w.conv_b, dg, n_lat)
    dh2 = matmul(tag + "up_dx_gate", da, s["g_up"], tb=True, view=up_view(0, f, 0))
    dh2 = matmul(tag + "up_dx_value", dv, s["g_up"], tb=True, view=up_view(0, f, 1), add=dh2)
    dw_up = (matmul(tag + "up_dw_gate", s["h2"], da, ta=True, out_dtype=BF16),
             matmul(tag + "up_dw_value", s["h2"], dv, ta=True, out_dtype=BF16))
    started = grad_ready("ffn", dict(w_up=dw_up, w_down=dw_down))
    nm2 = [Par(w.norm2_g), Par(_mod(mods, 3), True), Par(_mod(mods, 4), True)]
    dx1, dg2, dshift3, dscale4 = row_vjp(tag + "norm2", normmod_tile, [Row(s["x1"], d)], nm2, [dh2], t, n_lat,
                                         add_to_first=dx2, after=started)
    dy, dgate2 = row_vjp(tag + "resid1", resid_tile, [Row(s["xs"], d, 0, False), Row(s["y"], d)],
                         [Par(_mod(mods, 2), True)], [dx1], t, n_lat, row_grad_dtype=BF16)
    dm = matmul(tag + "out_dx", dy, s["g_out"], tb=True, view=out_view(0, True))
    dw_out = matmul(tag + "out_dw", s["m"], dy, ta=True, out_dtype=BF16)
    rows, pars = _post_args(s["o_att"], s["o_ret"], s["o_gla"], s["z"], w)
    started = grad_ready("w_out", dict(w_out=dw_out))
    do_att, do_ret, do_gla, d_rg, d_gr, d_ret_g, d_gla_g = row_vjp(tag + "post", post_tile, rows, pars, [dm], t, n_lat, after=started)
    dq_a, dk_a, dv_a = attn_bwd(s["p"], s["z"], s["o_att"], s["lse"], do_att, n_lat)
    dq_r, dk_r, dv_r, dlg = ret_bwd(s["p"], s["z"], w.lg, s["s_ret"], do_ret, n_lat)
    dq_g, dk_g, dv_g, dla = gla_bwd(s["p"], s["z"], s["s_gla"], do_gla, n_lat)
    dp = jnp.concatenate([dq_a, dk_a, _sum_dirs(dq_g), _sum_dirs(dq_r), _sum_dirs(dk_r), dla], axis=1)
    rows, pars = _prep_args(s["z"], s["zg"], cos, sin, w)
    d_zqk, d_zrq, d_zrk, d_zgq, dzg, d_qg, d_kg, d_up, d_gb = row_vjp(tag + "prep", prep_tile, rows, pars, [dp], t, n_lat)
    dz = jnp.concatenate([d_zqk, dv_a, d_zrq, d_zrk, _sum_dirs(dv_r), d_rg, d_zgq, _sum_dirs(dk_g), _sum_dirs(dv_g), d_gr], axis=1)
    dz, dzg = dz.astype(BF16), dzg.astype(BF16)
    dh_gate = matmul(tag + "gate_dx", dzg, s["w_gate"], tb=True)
    dh = matmul(tag + "in_dx", dz, s["w_main"], tb=True, add=dh_gate)
    dw_main = matmul(tag + "in_dw", s["h"], dz, ta=True, out_dtype=BF16)
    dw_gate = matmul(tag + "gate_dw", s["h"], dzg, ta=True, out_dtype=BF16)
    started = grad_ready("w_in", dict(w_main=dw_main, w_gate=dw_gate))
    nm1 = [Par(w.norm1_g), Par(_mod(mods, 0), True), Par(_mod(mods, 1), True)]
    dx, dg1, dshift0, dscale1 = row_vjp(tag + "norm1", normmod_tile, [Row(s["xs"], d)], nm1, [dh], t, n_lat,
                                        add_to_first=dx1, after=started)
    dmods = jnp.concatenate([dshift0, dscale1, dgate2, dshift3, dscale4, dgate5], axis=1)
    grads = dict(w_main=dw_main, w_gate=dw_gate, w_out=dw_out, w_up=dw_up, w_down=dw_down, norm1_g=dg1, q_g=d_qg, k_g=d_kg,
                 lg=dlg, ret_g=d_ret_g, gate_up=d_up, gate_b=d_gb, gla_g=d_gla_g, norm2_g=dg2, conv_w=dcw, conv_b=dcb)
    return dx, dmods, grads


def rope_tables(n_lat, n_ctx):
    rows = n_lat // GRID_W
    row = jnp.repeat(jnp.arange(rows, dtype=F32), GRID_W)
    col = jnp.tile(jnp.arange(GRID_W, dtype=F32), rows)
    n_freq = HEAD_DIM // 4
    inv_freq = ROPE_THETA ** (-jnp.arange(n_freq, dtype=F32) / n_freq)
    ang = jnp.concatenate([row[:, None] * inv_freq, col[:, None] * inv_freq], axis=-1)
    cos, sin = jnp.cos(ang), jnp.sin(ang)
    cos = jnp.concatenate([jnp.concatenate([cos, cos], axis=1), jnp.ones((n_ctx, HEAD_DIM), F32)], axis=0)
    sin = jnp.concatenate([jnp.concatenate([-sin, sin], axis=1), jnp.zeros((n_ctx, HEAD_DIM), F32)], axis=0)
    return cos, sin


def local_step(xs, target, mods, weights, fetch, final_g, n_lat, grad_ready):
    t, d = xs.shape
    cos, sin = rope_tables(n_lat, t - n_lat)
    saved = []
    h = xs
    for l, w in enumerate(weights):
        h, s = layer_fwd(l, h, mods[l], w, functools.partial(fetch, l), cos, sin, n_lat)
        saved.append(s)
    loss, dlat, dgf = final_loss(h, target, final_g, n_lat)
    dx = jnp.concatenate([dlat, jnp.zeros((t - n_lat, d), F32)], axis=0)
    dmods, grads = [None] * len(weights), [None] * len(weights)
    for l in reversed(range(len(weights))):
        dx, dmods[l], grads[l] = layer_bwd(l, dx, saved[l], mods[l], weights[l], cos, sin, n_lat, functools.partial(grad_ready, l))
    return loss, dx, dmods, grads, dgf


WEIGHT_NAMES = ("c_ctx", "ada_w", "ada_b", "norm1_g", "w_in", "q_norm_g", "k_norm_g", "ret_log_decay", "ret_norm_g",
                "gla_gate_up", "gla_gate_b", "gla_norm_g", "w_out", "norm2_g", "w_up", "conv_w", "conv_b", "w_down", "final_norm_g")
PACK_QUANTUM = 8 * LANES


def _pack(arrays):
    flat = jnp.concatenate([a.reshape(-1).astype(F32) for a in arrays])
    n = -(-flat.shape[0] // PACK_QUANTUM) * PACK_QUANTUM
    return jnp.pad(flat, (0, n - flat.shape[0])).reshape(8, n // 8)


def _unpack(flat2d, shapes):
    out, at = [], 0
    for s in shapes:
        size = int(np.prod(s))
        out.append(flat2d[:, at:at + size].reshape((flat2d.shape[0],) + tuple(s)))
        at += size
    return out


def _per_device(gathered):
    return gathered.reshape(N_DEV, -1)


def _from_chips(per_device, axis):
    chips = per_device[0::2]
    moved = jnp.moveaxis(chips, 0, axis)
    shape = moved.shape
    return moved.reshape(shape[:axis] + (shape[axis] * shape[axis + 1],) + shape[axis + 2:])


def kernel(x, c, ctx, c_ctx, ada_w, ada_b, norm1_g, w_in, q_norm_g, k_norm_g, ret_log_decay, ret_norm_g, gla_gate_up, gla_gate_b, gla_norm_g, w_out, norm2_g, w_up, conv_w, conv_b, w_down, final_norm_g, loss_target, m_c_ctx, m_ada_w, m_ada_b, m_norm1_g, m_w_in, m_q_norm_g, m_k_norm_g, m_ret_log_decay, m_ret_norm_g, m_gla_gate_up, m_gla_gate_b, m_gla_norm_g, m_w_out, m_norm2_g, m_w_up, m_conv_w, m_conv_b, m_w_down, m_final_norm_g, v_c_ctx, v_ada_w, v_ada_b, v_norm1_g, v_w_in, v_q_norm_g, v_k_norm_g, v_ret_log_decay, v_ret_norm_g, v_gla_gate_up, v_gla_gate_b, v_gla_norm_g, v_w_out, v_norm2_g, v_w_up, v_conv_w, v_conv_b, v_w_down, v_final_norm_g):
    weights = dict(zip(WEIGHT_NAMES, (c_ctx, ada_w, ada_b, norm1_g, w_in, q_norm_g, k_norm_g, ret_log_decay, ret_norm_g,
                                      gla_gate_up, gla_gate_b, gla_norm_g, w_out, norm2_g, w_up, conv_w, conv_b, w_down, final_norm_g)))
    mom_m = dict(zip(WEIGHT_NAMES, (m_c_ctx, m_ada_w, m_ada_b, m_norm1_g, m_w_in, m_q_norm_g, m_k_norm_g, m_ret_log_decay, m_ret_norm_g,
                                    m_gla_gate_up, m_gla_gate_b, m_gla_norm_g, m_w_out, m_norm2_g, m_w_up, m_conv_w, m_conv_b, m_w_down, m_final_norm_g)))
    mom_v = dict(zip(WEIGHT_NAMES, (v_c_ctx, v_ada_w, v_ada_b, v_norm1_g, v_w_in, v_q_norm_g, v_k_norm_g, v_ret_log_decay, v_ret_norm_g,
                                    v_gla_gate_up, v_gla_gate_b, v_gla_norm_g, v_w_out, v_norm2_g, v_w_up, v_conv_w, v_conv_b, v_w_down, v_final_norm_g)))
    depth, d = norm1_g.shape
    assert d == D_MODEL and x.shape[0] == 1
    n_lat, n_ctx, f = x.shape[1], ctx.shape[1], conv_b.shape[1]
    assert n_lat % ROW_TILE == 0 and n_ctx % ROW_TILE == 0 and f % FFN_COL_TILE == 0 and f % N_CHIPS == 0
    n_in = w_in.shape[2]
    n_ada = ada_w.shape[2]
    xi, yi, ci = lax.axis_index("x"), lax.axis_index("y"), lax.axis_index("c")
    chip = 2 * xi + yi
    dev = 2 * chip + ci

    big = ("w_in", "w_out", "w_up", "w_down")
    order = [(l, name) for l in range(depth) for name in big]
    shards = [weights[name][l].astype(BF16) for l, name in order]
    in_flight, token = split_start("gather_start", gather_copies, shards, [(N_CHIPS,) + s.shape for s in shards])

    def fetch(l, name, after):
        tag = f"{name}{l}"
        own, land = split_wait("gather_wait_" + tag, gather_copies, in_flight[order.index((l, name))], after)
        land = forward_to_sibling("gather_pass_" + tag, land)
        land = lax.dynamic_update_slice_in_dim(land, own[None], chip, axis=0)
        if name != "w_in":
            return land
        cols = jnp.concatenate([land[q] for q in range(N_CHIPS)], axis=1)
        return cols[:, :N_MAIN], jnp.pad(cols[:, N_MAIN:], ((0, 0), (0, LANES - N_GATE)))

    small_shapes = [c.shape[1:], conv_w.shape, gla_gate_up.shape, gla_gate_b.shape]
    got = _per_device(all_gather_small("gather_small", _pack([c, conv_w, gla_gate_up, gla_gate_b])))
    c_all, conv_w_sh, gate_up_sh, gate_b_sh = _unpack(got, small_shapes)
    conv_w_full = _from_chips(conv_w_sh, 2)
    gate_up_full = _from_chips(gate_up_sh, 3)
    gate_b_full = _from_chips(gate_b_sh, 2)

    act = jnp.zeros((16, d), F32).at[0:N_DEV].set(jax.nn.silu(c_all)).at[N_DEV].set(jax.nn.silu(c_ctx))
    mod_sh = jnp.stack([matmul(f"ada_fwd{l}", act, ada_w, view=ada_view(l, n_ada, False)) for l in range(depth)])
    got = _per_device(all_gather_small("gather_mods", _pack([mod_sh])))
    (mod_sh_all,) = _unpack(got, [mod_sh.shape])
    mod_full = _from_chips(mod_sh_all, 2) + ada_b[:, None, :]
    mod_mine = lax.dynamic_index_in_dim(mod_full, dev, axis=1, keepdims=False)
    mods = [jnp.stack([mod_mine[l].reshape(N_MOD, d), mod_full[l, N_DEV].reshape(N_MOD, d)]) for l in range(depth)]

    layer_w = []
    for l in range(depth):
        up = jnp.zeros((2, LANES, GLA_HEADS * GLA_DK), F32)
        up = up.at[0, 0:GLA_RANK].set(gate_up_full[l, 0]).at[1, GLA_RANK:2 * GLA_RANK].set(gate_up_full[l, 1])
        layer_w.append(LayerWeights(
            norm1_g=norm1_g[l].reshape(1, 1, d), q_g=q_norm_g[l].reshape(1, 1, HEAD_DIM), k_g=k_norm_g[l].reshape(1, 1, HEAD_DIM),
            lg=ret_log_decay[l].reshape(2, RET_HEADS, 1, 1), ret_g=ret_norm_g[l].reshape(1, 1, HEAD_DIM),
            gate_up=up.reshape(1, 2 * LANES, -1), gate_b=gate_b_full[l].reshape(1, 2, -1), gla_g=gla_norm_g[l].reshape(1, 1, HEAD_DIM),
            norm2_g=norm2_g[l].reshape(1, 1, d), conv_w=conv_w_full[l], conv_b=conv_b[l].reshape(1, f)))

    def pieces_of(name, g):
        if name == "w_in":
            full_cols = jnp.concatenate([g["w_main"], g["w_gate"][:, :N_GATE]], axis=1)
            return jnp.stack([full_cols[:, q * n_in:(q + 1) * n_in] for q in range(N_CHIPS)])
        if name == "w_up":
            gate, value = g["w_up"]
            return jnp.stack([gate[:, :f // 2], gate[:, f // 2:], value[:, :f // 2], value[:, f // 2:]])
        return g[name].reshape(N_CHIPS, -1, d)

    groups = {"ffn": ("w_up", "w_down"), "w_out": ("w_out",), "w_in": ("w_in",)}
    reducing = {}

    def grad_ready(l, group, g):
        pieces = [pieces_of(name, g) for name in groups[group]]
        from_sibling = exchange_sibling_halves(f"rs_sibling_{group}{l}", pieces)
        sums = [add_sibling_half(f"rs_add_sibling_{name}{l}", pc, sib, ci) for name, pc, sib in zip(groups[group], pieces, from_sibling)]
        in_flight_sums, started = split_start(f"rs_start_{group}{l}", scatter_copies, sums, [(3,) + s.shape[1:] for s in sums])
        reducing.update({(l, name): grp for name, grp in zip(groups[group], in_flight_sums)})
        return started

    xs = jnp.concatenate([x[0], ctx[0]], axis=0) + token[0, 0]
    loss, dx, dmods, grads, dgf = local_step(xs, loss_target[0], mods, layer_w, fetch, final_norm_g.reshape(1, d), n_lat, grad_ready)

    def gate_up_grad(g):
        return jnp.stack([g[0, 0:GLA_RANK], g[0, LANES + GLA_RANK:LANES + 2 * GLA_RANK]])

    per_layer = [[dmods[l][0], dmods[l][1], grads[l]["norm1_g"], grads[l]["norm2_g"], grads[l]["q_g"], grads[l]["k_g"],
                  grads[l]["ret_g"], grads[l]["gla_g"], grads[l]["lg"], gate_up_grad(grads[l]["gate_up"]), grads[l]["gate_b"],
                  grads[l]["conv_w"], grads[l]["conv_b"]] for l in range(depth)]
    layer_shapes = [(N_MOD * d,), (N_MOD * d,), (d,), (d,), (HEAD_DIM,), (HEAD_DIM,), (HEAD_DIM,), (HEAD_DIM,), (2, RET_HEADS),
                    (2, GLA_RANK, GLA_HEADS * GLA_DK), (2, GLA_HEADS * GLA_DK), (3, f), (f,)]
    packed = _pack([a for lay in per_layer for a in lay] + [dgf, loss[0, 0:1]])
    gathered = all_gather_small("gather_small_grads", packed)
    every = _unpack(_per_device(gathered), layer_shapes * depth + [(d,), (1,)])
    total = _unpack(sum_device_blocks("sum_small_grads", gathered).reshape(1, -1), layer_shapes * depth + [(d,), (1,)])
    nl = len(layer_shapes)

    def tot(l, k):
        return total[l * nl + k][0]

    out = {"norm1_g": jnp.stack([tot(l, 2) for l in range(depth)]), "norm2_g": jnp.stack([tot(l, 3) for l in range(depth)]),
           "q_norm_g": jnp.stack([tot(l, 4) for l in range(depth)]), "k_norm_g": jnp.stack([tot(l, 5) for l in range(depth)]),
           "ret_norm_g": jnp.stack([tot(l, 6) for l in range(depth)]), "gla_norm_g": jnp.stack([tot(l, 7) for l in range(depth)]),
           "ret_log_decay": jnp.stack([tot(l, 8) for l in range(depth)]),
           "gla_gate_up": lax.dynamic_slice_in_dim(jnp.stack([tot(l, 9) for l in range(depth)]), chip * gla_gate_up.shape[3], gla_gate_up.shape[3], axis=3),
           "gla_gate_b": lax.dynamic_slice_in_dim(jnp.stack([tot(l, 10) for l in range(depth)]), chip * gla_gate_b.shape[2], gla_gate_b.shape[2], axis=2),
           "conv_w": lax.dynamic_slice_in_dim(jnp.stack([tot(l, 11) for l in range(depth)]), chip * conv_w.shape[2], conv_w.shape[2], axis=2),
           "conv_b": jnp.stack([tot(l, 12) for l in range(depth)]),
           "final_norm_g": total[depth * nl][0],
           "ada_b": jnp.stack([tot(l, 0) + tot(l, 1) for l in range(depth)])}
    loss_total = total[depth * nl + 1][0, 0]

    dmod_all = jnp.zeros((depth, 16, N_MOD * d), F32)
    for l in range(depth):
        dmod_all = dmod_all.at[l, 0:N_DEV].set(every[l * nl][:, :]).at[l, N_DEV].set(tot(l, 1))
    dmod_cols = lax.dynamic_slice_in_dim(dmod_all, chip * n_ada, n_ada, axis=2)
    out["ada_w"] = jnp.stack([matmul(f"ada_dw{l}", act, dmod_cols[l], ta=True) for l in range(depth)])
    dact = matmul("ada_dx0", dmod_cols[0], ada_w, tb=True, view=ada_view(0, n_ada, True))
    for l in range(1, depth):
        dact = matmul(f"ada_dx{l}", dmod_cols[l], ada_w, tb=True, view=ada_view(l, n_ada, True), add=dact)
    got = _per_device(all_gather_small("gather_dcctx", _pack([dact[N_DEV]])))[0::2, :d]
    dsilu = got[0] + got[1] + got[2] + got[3]
    sig = jax.nn.sigmoid(c_ctx)
    out["c_ctx"] = dsilu * (sig + c_ctx * sig * (1.0 - sig))

    deltas, new_m, new_v = {}, {}, {}

    def update(name):
        out[name] = out[name].reshape(weights[name].shape)
        deltas[name], new_m[name], new_v[name] = adamw("adamw_" + name, weights[name], out[name], mom_m[name], mom_v[name])

    for name in WEIGHT_NAMES:
        if name not in big:
            update(name)
    behind = new_v["ada_w"]
    for name in ("w_down", "w_up", "w_out", "w_in"):
        halves = []
        for l in range(depth):
            sums, got = split_wait(f"rs_wait_{name}{l}", scatter_copies, reducing[(l, name)], behind)
            halves.append(add_chip_sums(f"rs_add_chips_{name}{l}", sums, got, chip, ci))
        out[name] = jnp.stack(join_sibling_halves("rs_join_" + name, halves))
        update(name)
        behind = new_v[name]
    grad_x = dx[:n_lat].reshape(x.shape)
    return (loss_total, grad_x, *[out[n] for n in WEIGHT_NAMES], *[deltas[n] for n in WEIGHT_NAMES],
            *[new_m[n] for n in WEIGHT_NAMES], *[new_v[n] for n in WEIGHT_NAMES])
```

```python
import functools
from typing import NamedTuple

import numpy as np
import jax
import jax.numpy as jnp
from jax import lax
from jax.experimental import pallas as pl
from jax.experimental.pallas import tpu as pltpu

F32 = jnp.float32
BF16 = jnp.bfloat16

D_MODEL = 2048
HEAD_DIM = 128
ATT_Q_HEADS = 8
ATT_KV_HEADS = 2
ATT_GROUP = ATT_Q_HEADS // ATT_KV_HEADS
RET_HEADS = 4
GLA_HEADS = 4
GLA_DK = 64
GLA_DV = 128
GLA_RANK = 16
GLA_TAU = 16.0
RET_CHUNK = 128
GLA_CHUNK = 64
GRID_W = 64
ROPE_THETA = 10000.0
N_MOD = 6
EPS = 1e-6
N_MAIN = 5120
N_GATE = 2 * GLA_RANK
LANES = 128
ROW_TILE = 256
FFN_COL_TILE = 256
VMEM_LIMIT = 56 * 1024 * 1024

ADAM_LR = 0.001
ADAM_B1 = 0.9
ADAM_B2 = 0.999
ADAM_EPS = 1e-08
ADAM_WD = 0.01
ADAM_STEP = 10

Z_AQ, Z_AK, Z_AV = 0, 1024, 1280
Z_RQ, Z_RK, Z_RV, Z_RG = 1536, 2048, 2560, 3072
Z_GQ, Z_GK, Z_GV, Z_GR = 3584, 3840, 4096, 4608
P_AQ, P_AK, P_GQ, P_RQ, P_RK, P_LA = 0, 1024, 1280, 1536, 2048, 2560
P_W = 3072


def _params(sem=None):
    return pltpu.CompilerParams(dimension_semantics=sem, vmem_limit_bytes=VMEM_LIMIT)


def _pick(n, cands):
    for c in cands:
        if n % c == 0:
            return c
    return n


_NN = (((1,), (0,)), ((), ()))
_NT = (((1,), (1,)), ((), ()))
_TN = (((0,), (0,)), ((), ()))


def _dg(a, b, dims):
    return lax.dot_general(a.astype(BF16), b.astype(BF16), dims, preferred_element_type=F32)


@jax.custom_vjp
def bdot(a, b):
    return _dg(a, b, _NN)


def _bdot_fwd(a, b):
    return _dg(a, b, _NN), (a, b)


def _bdot_bwd(res, ct):
    a, b = res
    return _dg(ct, b, _NT), _dg(a, ct, _TN)


bdot.defvjp(_bdot_fwd, _bdot_bwd)


@jax.custom_vjp
def bdot_nt(a, b):
    return _dg(a, b, _NT)


def _bdot_nt_fwd(a, b):
    return _dg(a, b, _NT), (a, b)


def _bdot_nt_bwd(res, ct):
    a, b = res
    return _dg(ct, b, _NN), _dg(ct, a, _TN)


bdot_nt.defvjp(_bdot_nt_fwd, _bdot_nt_bwd)


@jax.custom_vjp
def bdot_tn(a, b):
    return _dg(a, b, _TN)


def _bdot_tn_fwd(a, b):
    return _dg(a, b, _TN), (a, b)


def _bdot_tn_bwd(res, ct):
    a, b = res
    return _dg(b, ct, _NT), _dg(a, ct, _NN)


bdot_tn.defvjp(_bdot_tn_fwd, _bdot_tn_bwd)


def _split3(x):
    x1 = x.astype(BF16)
    r1 = x - x1.astype(F32)
    x2 = r1.astype(BF16)
    x3 = (r1 - x2.astype(F32)).astype(BF16)
    return x1, x2, x3


def _mask_dot(mask_bf16, x, dims):
    x1, x2, x3 = _split3(x)
    f = lambda t: lax.dot_general(mask_bf16, t, dims, preferred_element_type=F32)
    return f(x1) + f(x2) + f(x3)


@jax.custom_vjp
def mask_cumsum(mask, x):
    return _mask_dot(mask.astype(BF16), x, _NN)


def _mask_cumsum_fwd(mask, x):
    return mask_cumsum(mask, x), mask


def _mask_cumsum_bwd(mask, ct):
    return jnp.zeros_like(mask), _mask_dot(mask.astype(BF16), ct, _TN)


mask_cumsum.defvjp(_mask_cumsum_fwd, _mask_cumsum_bwd)


def _roll(x, shift, axis):
    return pltpu.roll(x, shift % x.shape[axis], axis)


@functools.partial(jax.custom_vjp, nondiff_argnums=(1, 2))
def roll(x, shift, axis):
    return _roll(x, shift, axis)


def _roll_fwd(x, shift, axis):
    return _roll(x, shift, axis), None


def _roll_bwd(shift, axis, _, ct):
    return (_roll(ct, -shift, axis),)


roll.defvjp(_roll_fwd, _roll_bwd)


def rms(x):
    return x * lax.rsqrt(jnp.mean(x * x, axis=-1, keepdims=True) + EPS)


def silu(x):
    return x * (1.0 / (1.0 + jnp.exp(-x)))


def log_sigmoid(x):
    return jnp.minimum(x, 0.0) - jnp.log(1.0 + jnp.exp(-jnp.abs(x)))


def rope(t, cos, sin):
    return t * cos + roll(t, HEAD_DIM // 2, 1) * sin


def _heads(x, n, width=HEAD_DIM):
    return [x[:, h * width:(h + 1) * width] for h in range(n)]


class Row(NamedTuple):
    arr: jax.Array
    width: int
    idx: int = 0
    diff: bool = True


class Par(NamedTuple):
    arr: jax.Array
    grouped: bool = False
    diff: bool = True


def _row_specs(rows, pars, tm, n_lat_tiles):
    def grp(i):
        return jnp.minimum(i // n_lat_tiles, 1)

    specs = [pl.BlockSpec((tm, r.width), functools.partial(lambda i, k: (i, k), k=r.idx)) for r in rows]
    for p in pars:
        blk = (1,) + p.arr.shape[1:]
        if p.grouped:
            specs.append(pl.BlockSpec(blk, lambda i: (grp(i), 0, 0)))
        else:
            specs.append(pl.BlockSpec(blk, lambda i: (0, 0, 0)))
    return specs


def row_map(name, fn, rows, pars, outs, n_rows, n_lat):
    tm = ROW_TILE
    nr, npar = len(rows), len(pars)

    def body(*refs):
        vals = [r[...] for r in refs[:nr]] + [p[0] for p in refs[nr:nr + npar]]
        res = fn(*vals)
        for o, v in zip(refs[nr + npar:], res):
            o[...] = v.astype(o.dtype)

    return pl.pallas_call(
        body, name=name, grid=(n_rows // tm,),
        in_specs=_row_specs(rows, pars, tm, n_lat // tm),
        out_specs=[pl.BlockSpec((tm, w), lambda i: (i, 0)) for w, _ in outs],
        out_shape=[jax.ShapeDtypeStruct((n_rows, w), dt) for w, dt in outs],
        compiler_params=_params(("arbitrary",)),
    )(*[r.arr for r in rows], *[p.arr for p in pars])


def row_vjp(name, fn, rows, pars, cts, n_rows, n_lat, add_to_first=None, row_grad_dtype=F32, after=None):
    tm = ROW_TILE
    nr, npar, nc = len(rows), len(pars), len(cts)
    n_lat_tiles = n_lat // tm
    args = list(rows) + list(pars)
    diff_pos = [k for k, a in enumerate(args) if a.diff]
    n_add = 0 if add_to_first is None else 1
    n_after = 0 if after is None else 1

    def body(*refs):
        i = pl.program_id(0)
        vals = [r[...] for r in refs[:nr]] + [p[0] for p in refs[nr:nr + npar]]
        ct_vals = tuple(c[...] for c in refs[nr + npar:nr + npar + nc])
        out_refs = refs[nr + npar + nc + n_add + n_after:]

        def g(*dv):
            full = list(vals)
            for k, v in zip(diff_pos, dv):
                full[k] = v
            return tuple(fn(*full))

        _, vjp = jax.vjp(g, *[vals[k] for k in diff_pos])
        grads = vjp(ct_vals)
        for n, (k, o, gr) in enumerate(zip(diff_pos, out_refs, grads)):
            if k < nr:
                o[...] = (gr + refs[nr + npar + nc][...] if (n == 0 and n_add) else gr).astype(o.dtype)
            else:
                first = (i == 0) | (i == n_lat_tiles) if args[k].grouped else (i == 0)

                @pl.when(first)
                def _():
                    o[0] = gr

                @pl.when(jnp.logical_not(first))
                def _():
                    o[0] += gr

    def grp(i):
        return jnp.minimum(i // n_lat_tiles, 1)

    out_specs, out_shape = [], []
    for k in diff_pos:
        a = args[k]
        if k < nr:
            out_specs.append(pl.BlockSpec((tm, a.width), lambda i: (i, 0)))
            out_shape.append(jax.ShapeDtypeStruct((n_rows, a.width), row_grad_dtype))
        else:
            blk = (1,) + a.arr.shape[1:]
            out_specs.append(pl.BlockSpec(blk, (lambda i: (grp(i), 0, 0)) if a.grouped else (lambda i: (0, 0, 0))))
            out_shape.append(jax.ShapeDtypeStruct(a.arr.shape, F32))
    extra = list(cts) + ([add_to_first] if n_add else [])
    ct_specs = [pl.BlockSpec((tm, c.shape[1]), lambda i: (i, 0)) for c in extra]
    if n_after:
        extra.append(after)
        ct_specs.append(pl.BlockSpec(memory_space=pl.ANY))
    return pl.pallas_call(
        body, name=name, grid=(n_rows // tm,),
        in_specs=_row_specs(rows, pars, tm, n_lat_tiles) + ct_specs,
        out_specs=out_specs, out_shape=out_shape,
        compiler_params=_params(("arbitrary",)),
    )(*[r.arr for r in rows], *[p.arr for p in pars], *extra)


class BView(NamedTuple):
    n: int
    k: int
    tn: int
    tk: int
    index_map: object
    lead: int = 1


MATMUL_VMEM_BUDGET = 40 * 1024 * 1024


def _matmul_tiles(m, n, k, a_bytes, b_bytes, o_bytes):
    tms = [c for c in (1152, 1024, 768, 512, 256, 128) if m % c == 0] or [m]
    tns = [c for c in (2048, 1408, 1280, 1024, 768, 512, 256, 128) if n % c == 0] or [n]
    tks = [k] + [c for c in (2816, 2304, 2048, 1408, 1024, 512, 256, 128) if k % c == 0 and c < k]
    for tk in tks:
        fits = [(tm * tn, tm, tn) for tm in tms for tn in tns
                if 2 * (tm * tk * a_bytes + tk * tn * b_bytes + tm * tn * o_bytes) + 2 * tm * tn * 4 <= MATMUL_VMEM_BUDGET]
        if fits and (max(fits)[0] >= 512 * 512 or tms == [m] or tk == tks[-1]):
            _, tm, tn = max(fits)
            return tm, tn, tk
    raise ValueError(f"no matmul tiling for {(m, n, k)}")


def matmul(name, a, b, *, ta=False, tb=False, add=None, out_dtype=F32, view=None):
    m = a.shape[1] if ta else a.shape[0]
    o_bytes = jnp.dtype(out_dtype).itemsize * (1 if add is None else 2)
    if view is None:
        k = a.shape[0] if ta else a.shape[1]
        n = b.shape[0] if tb else b.shape[1]
        assert (b.shape[1] if tb else b.shape[0]) == k, (a.shape, b.shape, ta, tb)
        tm, tn, tk = _matmul_tiles(m, n, k, a.dtype.itemsize, b.dtype.itemsize, o_bytes)
    else:
        n, k, tn, tk = view.n, view.k, view.tn, view.tk
        tm, _, _ = _matmul_tiles(m, tn, tk, a.dtype.itemsize, b.dtype.itemsize, o_bytes)
    nk = k // tk
    dims = (((0 if ta else 1,), (1 if tb else 0,)), ((), ()))

    def body(a_ref, b_ref, *rest):
        prod = lax.dot_general(a_ref[...].astype(BF16), b_ref[...].astype(BF16), dims, preferred_element_type=F32)
        if nk == 1:
            o_ref = rest[-1]
            o_ref[...] = (prod if add is None else prod + rest[0][...]).astype(o_ref.dtype)
            return
        o_ref, acc = rest[-2:]
        kk = pl.program_id(2)

        @pl.when(kk == 0)
        def _():
            acc[...] = prod

        @pl.when(kk != 0)
        def _():
            acc[...] += prod

        @pl.when(kk == nk - 1)
        def _():
            r = acc[...]
            if add is not None:
                r = r + rest[0][...]
            o_ref[...] = r.astype(o_ref.dtype)

    if ta:
        a_spec = pl.BlockSpec((tk, tm), lambda i, j, kk: (kk, i))
    else:
        a_spec = pl.BlockSpec((tm, tk), lambda i, j, kk: (i, kk))
    b_tile = (tn, tk) if tb else (tk, tn)
    if view is not None:
        b_spec = pl.BlockSpec((None,) * view.lead + b_tile, view.index_map)
    elif tb:
        b_spec = pl.BlockSpec(b_tile, lambda i, j, kk: (j, kk))
    else:
        b_spec = pl.BlockSpec(b_tile, lambda i, j, kk: (kk, j))
    o_spec = pl.BlockSpec((tm, tn), lambda i, j, kk: (i, j))
    ins = [a, b] + ([add] if add is not None else [])
    return pl.pallas_call(
        body, name=name, grid=(m // tm, n // tn, nk),
        in_specs=[a_spec, b_spec] + ([o_spec] if add is not None else []),
        out_specs=o_spec, out_shape=jax.ShapeDtypeStruct((m, n), out_dtype),
        scratch_shapes=[pltpu.VMEM((tm, tn), F32)] if nk > 1 else [],
        compiler_params=_params(("parallel", "parallel", "arbitrary")),
    )(*ins)


def normmod_tile(x, g, shift, scale):
    return (rms(x) * g * (1.0 + scale) + shift,)


def resid_tile(x, y, gate):
    return (x + gate * y,)


def prep_tile(z_qk, z_rq, z_rk, z_gq, zg, cos, sin, qg, kg, gate_up, gate_b):
    out = []
    for h, t in enumerate(_heads(z_qk, ATT_Q_HEADS + ATT_KV_HEADS)):
        out.append(rope(rms(t) * (qg if h < ATT_Q_HEADS else kg), cos, sin))
    gq = z_gq * (GLA_DK ** -0.5)
    rq = [rope(t, cos, sin) for t in _heads(z_rq, RET_HEADS)]
    rk = [rope(t * (HEAD_DIM ** -0.5), cos, sin) for t in _heads(z_rk, RET_HEADS)]
    la = [log_sigmoid(bdot(zg, gate_up[d * LANES:(d + 1) * LANES]) + gate_b[d:d + 1]) * (1.0 / GLA_TAU) for d in range(2)]
    return (jnp.concatenate(out + [gq] + rq + rk + la, axis=1),)


def post_tile(o_att, o_ret_f, o_ret_b, o_gla_f, o_gla_b, rg, gr, ret_g, gla_g):
    ret = jnp.concatenate([rms(t) * ret_g for t in _heads(o_ret_f + o_ret_b, RET_HEADS)], axis=1) * silu(rg)
    gla = jnp.concatenate([rms(t) * gla_g for t in _heads(o_gla_f + o_gla_b, GLA_HEADS)], axis=1) * silu(gr)
    return (jnp.concatenate([o_att, ret, gla], axis=1),)


def _convglu_tile(n_lat, a, v, cw, cb):
    t = a.shape[0]
    row = lax.broadcasted_iota(jnp.int32, (t, 1), 0)
    has_prev = ((row != 0) & (row != n_lat)).astype(F32)
    has_next = ((row != n_lat - 1) & (row != t - 1)).astype(F32)
    conv = roll(a, 1, 0) * has_prev * cw[0:1] + a * cw[1:2] + roll(a, -1, 0) * has_next * cw[2:3] + cb
    return silu(conv) * v


def convglu(name, u, cw, cb, n_lat):
    t, f2 = u.shape
    f, tc = f2 // 2, FFN_COL_TILE
    nb = f // tc

    def body(a_ref, v_ref, cw_ref, cb_ref, o_ref):
        o_ref[...] = _convglu_tile(n_lat, a_ref[...], v_ref[...], cw_ref[...], cb_ref[...]).astype(o_ref.dtype)

    return pl.pallas_call(
        body, name=name, grid=(nb,),
        in_specs=[pl.BlockSpec((t, tc), lambda j: (0, j)), pl.BlockSpec((t, tc), lambda j: (0, nb + j)),
                  pl.BlockSpec((3, tc), lambda j: (0, j)), pl.BlockSpec((1, tc), lambda j: (0, j))],
        out_specs=pl.BlockSpec((t, tc), lambda j: (0, j)),
        out_shape=jax.ShapeDtypeStruct((t, f), BF16),
        compiler_params=_params(("parallel",)),
    )(u, u, cw, cb)


def convglu_bwd(name, u, cw, cb, dg, n_lat):
    t, f2 = u.shape
    f, tc = f2 // 2, FFN_COL_TILE
    nb = f // tc

    def body(a_ref, v_ref, cw_ref, cb_ref, dg_ref, da_ref, dv_ref, dcw_ref, dcb_ref):
        _, vjp = jax.vjp(functools.partial(_convglu_tile, n_lat), a_ref[...], v_ref[...], cw_ref[...], cb_ref[...])
        da, dv, dcw_ref[...], dcb_ref[...] = vjp(dg_ref[...])
        da_ref[...], dv_ref[...] = da.astype(BF16), dv.astype(BF16)

    col = pl.BlockSpec((t, tc), lambda j: (0, j))
    return pl.pallas_call(
        body, name=name, grid=(nb,),
        in_specs=[col, pl.BlockSpec((t, tc), lambda j: (0, nb + j)), pl.BlockSpec((3, tc), lambda j: (0, j)),
                  pl.BlockSpec((1, tc), lambda j: (0, j)), col],
        out_specs=[col, col, pl.BlockSpec((3, tc), lambda j: (0, j)), pl.BlockSpec((1, tc), lambda j: (0, j))],
        out_shape=[jax.ShapeDtypeStruct((t, f), BF16), jax.ShapeDtypeStruct((t, f), BF16),
                   jax.ShapeDtypeStruct((3, f), F32), jax.ShapeDtypeStruct((1, f), F32)],
        compiler_params=_params(("parallel",)),
    )(u, u, cw, cb, dg)


def final_loss(x, target, g, n_lat):
    tm = ROW_TILE
    d = x.shape[1]

    def body(x_ref, t_ref, g_ref, loss_ref, dx_ref, dg_ref):
        i = pl.program_id(0)
        tgt = t_ref[...]

        def f(xv, gv):
            e = rms(xv) * gv - tgt
            s = jnp.sum(jnp.sum(e * e, axis=1, keepdims=True), axis=0, keepdims=True)
            return s * (0.5 / d)

        val, vjp = jax.vjp(f, x_ref[...], g_ref[...])
        dx, dgv = vjp(jnp.ones((1, 1), F32))
        dx_ref[...] = dx

        @pl.when(i == 0)
        def _():
            dg_ref[...] = dgv
            loss_ref[...] = jnp.broadcast_to(val, loss_ref.shape)

        @pl.when(i != 0)
        def _():
            dg_ref[...] += dgv
            loss_ref[...] += jnp.broadcast_to(val, loss_ref.shape)

    return pl.pallas_call(
        body, name="final_loss", grid=(n_lat // tm,),
        in_specs=[pl.BlockSpec((tm, d), lambda i: (i, 0)), pl.BlockSpec((tm, d), lambda i: (i, 0)),
                  pl.BlockSpec((1, d), lambda i: (0, 0))],
        out_specs=[pl.BlockSpec((1, LANES), lambda i: (0, 0)), pl.BlockSpec((tm, d), lambda i: (i, 0)),
                   pl.BlockSpec((1, d), lambda i: (0, 0))],
        out_shape=[jax.ShapeDtypeStruct((1, LANES), F32), jax.ShapeDtypeStruct((n_lat, d), F32),
                   jax.ShapeDtypeStruct((1, d), F32)],
        compiler_params=_params(("arbitrary",)),
    )(x, target, g)


ATT_SCALE = HEAD_DIM ** -0.5
_AK_BLK = P_AK // HEAD_DIM
_AV_BLK = Z_AV // HEAD_DIM


def _att_specs(t, tq):
    gw = ATT_GROUP * HEAD_DIM
    q_spec = pl.BlockSpec((tq, gw), lambda kv, i: (i, kv))
    k_spec = pl.BlockSpec((t, HEAD_DIM), lambda kv, i: (0, _AK_BLK + kv))
    v_spec = pl.BlockSpec((t, HEAD_DIM), lambda kv, i: (0, _AV_BLK + kv))
    row_spec = pl.BlockSpec((ATT_GROUP, tq, 1), lambda kv, i: (kv, i, 0))
    return q_spec, k_spec, v_spec, row_spec


def _att_mask(i, t, tq, n_lat):
    col = lax.broadcasted_iota(jnp.int32, (1, t), 1)
    return jnp.where((i >= n_lat // tq) & (col < n_lat), -jnp.inf, 0.0).astype(F32)


def attn_fwd(p, z, n_lat):
    t = p.shape[0]
    tq = ROW_TILE

    def body(q_ref, k_ref, v_ref, o_ref, lse_ref):
        mask = _att_mask(pl.program_id(1), t, tq, n_lat)
        k, v = k_ref[...].astype(BF16), v_ref[...].astype(BF16)
        for g in range(ATT_GROUP):
            cols = slice(g * HEAD_DIM, (g + 1) * HEAD_DIM)
            s = _dg(q_ref[:, cols], k, _NT) * ATT_SCALE + mask
            m = jnp.max(s, axis=1, keepdims=True)
            pr = jnp.exp(s - m)
            l = jnp.sum(pr, axis=1, keepdims=True)
            o_ref[:, cols] = _dg(pr, v, _NN) / l
            lse_ref[g] = m + jnp.log(l)

    q_spec, k_spec, v_spec, row_spec = _att_specs(t, tq)
    return pl.pallas_call(
        body, name="attn_fwd", grid=(ATT_KV_HEADS, t // tq),
        in_specs=[q_spec, k_spec, v_spec], out_specs=[q_spec, row_spec],
        out_shape=[jax.ShapeDtypeStruct((t, ATT_Q_HEADS * HEAD_DIM), F32),
                   jax.ShapeDtypeStruct((ATT_Q_HEADS, t, 1), F32)],
        compiler_params=_params(("parallel", "parallel")),
    )(p, p, z)


def attn_bwd(p, z, o, lse, do, n_lat):
    t = p.shape[0]
    tq = ROW_TILE

    def body(q_ref, k_ref, v_ref, o_ref, do_ref, lse_ref, dq_ref, dk_ref, dv_ref):
        i = pl.program_id(1)

        @pl.when(i == 0)
        def _():
            dk_ref[...] = jnp.zeros_like(dk_ref)
            dv_ref[...] = jnp.zeros_like(dv_ref)

        mask = _att_mask(i, t, tq, n_lat)
        k, v = k_ref[...].astype(BF16), v_ref[...].astype(BF16)
        dk, dv = dk_ref[...], dv_ref[...]
        for g in range(ATT_GROUP):
            cols = slice(g * HEAD_DIM, (g + 1) * HEAD_DIM)
            q, do_g = q_ref[:, cols].astype(BF16), do_ref[:, cols]
            pr = jnp.exp(_dg(q, k, _NT) * ATT_SCALE + mask - lse_ref[g])
            delta = jnp.sum(o_ref[:, cols] * do_g, axis=1, keepdims=True)
            ds = pr * (_dg(do_g, v, _NT) - delta) * ATT_SCALE
            dq_ref[:, cols] = _dg(ds, k, _NN)
            dk = dk + _dg(ds, q, _TN)
            dv = dv + _dg(pr, do_g, _TN)
        dk_ref[...], dv_ref[...] = dk, dv

    q_spec, k_spec, v_spec, row_spec = _att_specs(t, tq)
    kv_out = pl.BlockSpec((t, HEAD_DIM), lambda kv, i: (0, kv))
    return pl.pallas_call(
        body, name="attn_bwd", grid=(ATT_KV_HEADS, t // tq),
        in_specs=[q_spec, k_spec, v_spec, q_spec, q_spec, row_spec],
        out_specs=[q_spec, kv_out, kv_out],
        out_shape=[jax.ShapeDtypeStruct((t, ATT_Q_HEADS * HEAD_DIM), F32),
                   jax.ShapeDtypeStruct((t, ATT_KV_HEADS * HEAD_DIM), F32),
                   jax.ShapeDtypeStruct((t, ATT_KV_HEADS * HEAD_DIM), F32)],
        compiler_params=_params(("parallel", "arbitrary")),
    )(p, p, z, o, do, lse)


_RQ_BLK = P_RQ // HEAD_DIM
_RK_BLK = P_RK // HEAD_DIM
_RV_BLK = Z_RV // HEAD_DIM


def _scan_chunk(direction, step, n_chunks, n_lat_chunks):
    return jnp.where(direction == 0, (step + n_lat_chunks) % n_chunks, n_chunks - 1 - step)


def _ret_geometry(direction):
    c = RET_CHUNK
    i = lax.broadcasted_iota(jnp.int32, (c, c), 0)
    j = lax.broadcasted_iota(jnp.int32, (c, c), 1)
    rel = jnp.where(direction == 0, i - j, j - i).astype(F32)
    r = lax.broadcasted_iota(jnp.int32, (c, 1), 0)
    pos = jnp.where(direction == 0, r, c - 1 - r).astype(F32)
    return rel, pos


def ret_chunk(q, k, v, s, lg, rel, pos):
    c = RET_CHUNK
    causal = rel >= 0
    d_in = jnp.where(causal, jnp.exp(lg * jnp.where(causal, rel, 0.0)), 0.0)
    q_dec = jnp.exp(lg * (pos + 1.0))
    k_dec = jnp.exp(lg * (c - 1.0 - pos))
    c_dec = jnp.exp(lg * c)
    att = bdot_nt(q, k) * d_in
    o = bdot(att, v) + bdot(q * q_dec, s)
    s_new = c_dec * s + bdot_tn(k * k_dec, v)
    return o, s_new


def ret_fwd(p, z, lg, n_lat):
    t = p.shape[0]
    c = RET_CHUNK
    nc, nlc = t // c, n_lat // c

    def body(q_ref, k_ref, v_ref, lg_ref, o_ref, ssave_ref, s_s):
        d, n = pl.program_id(0), pl.program_id(1)

        @pl.when(n == 0)
        def _():
            s_s[...] = jnp.zeros_like(s_s)

        rel, pos = _ret_geometry(d)
        for h in range(RET_HEADS):
            cols = slice(h * HEAD_DIM, (h + 1) * HEAD_DIM)
            ssave_ref[0, h, 0] = s_s[h]
            o, s_new = ret_chunk(q_ref[:, cols], k_ref[:, cols], v_ref[:, cols], s_s[h], lg_ref[0, h], rel, pos)
            o_ref[:, cols] = o
            s_s[h] = s_new

    w = RET_HEADS * HEAD_DIM

    def blk(base):
        return pl.BlockSpec((c, w), lambda d, n: (_scan_chunk(d, n, nc, nlc), base // RET_HEADS))

    return pl.pallas_call(
        body, name="ret_fwd", grid=(2, nc),
        in_specs=[blk(_RQ_BLK), blk(_RK_BLK), blk(_RV_BLK), pl.BlockSpec((1, RET_HEADS, 1, 1), lambda d, n: (d, 0, 0, 0))],
        out_specs=[pl.BlockSpec((c, w), lambda d, n: (_scan_chunk(d, n, nc, nlc), d)),
                   pl.BlockSpec((1, RET_HEADS, 1, HEAD_DIM, HEAD_DIM), lambda d, n: (d, 0, n, 0, 0))],
        out_shape=[jax.ShapeDtypeStruct((t, 2 * w), F32),
                   jax.ShapeDtypeStruct((2, RET_HEADS, nc, HEAD_DIM, HEAD_DIM), F32)],
        scratch_shapes=[pltpu.VMEM((RET_HEADS, HEAD_DIM, HEAD_DIM), F32)],
        compiler_params=_params(("parallel", "arbitrary")),
    )(p, p, z, lg)


def ret_bwd(p, z, lg, states, do, n_lat):
    t = p.shape[0]
    c = RET_CHUNK
    nc, nlc = t // c, n_lat // c

    def body(q_ref, k_ref, v_ref, lg_ref, s_ref, do_ref, dq_ref, dk_ref, dv_ref, dlg_ref, ds_s):
        d, n = pl.program_id(0), pl.program_id(1)

        @pl.when(n == 0)
        def _():
            ds_s[...] = jnp.zeros_like(ds_s)
            dlg_ref[...] = jnp.zeros_like(dlg_ref)

        rel, pos = _ret_geometry(d)
        f = functools.partial(ret_chunk, rel=rel, pos=pos)
        for h in range(RET_HEADS):
            cols = slice(h * HEAD_DIM, (h + 1) * HEAD_DIM)
            _, vjp = jax.vjp(f, q_ref[:, cols], k_ref[:, cols], v_ref[:, cols], s_ref[0, h, 0], lg_ref[0, h])
            dq, dk, dv, ds, dlg = vjp((do_ref[:, cols], ds_s[h]))
            dq_ref[:, cols], dk_ref[:, cols], dv_ref[:, cols] = dq, dk, dv
            ds_s[h] = ds
            dlg_ref[0, h] += dlg

    def chunk_of(d, n):
        return _scan_chunk(d, nc - 1 - n, nc, nlc)

    w = RET_HEADS * HEAD_DIM

    def blk(base):
        return pl.BlockSpec((c, w), lambda d, n: (chunk_of(d, n), base // RET_HEADS))

    out_blk = pl.BlockSpec((c, w), lambda d, n: (chunk_of(d, n), d))
    lg_blk = pl.BlockSpec((1, RET_HEADS, 1, 1), lambda d, n: (d, 0, 0, 0))
    grad_shape = jax.ShapeDtypeStruct((t, 2 * w), F32)
    return pl.pallas_call(
        body, name="ret_bwd", grid=(2, nc),
        in_specs=[blk(_RQ_BLK), blk(_RK_BLK), blk(_RV_BLK), lg_blk,
                  pl.BlockSpec((1, RET_HEADS, 1, HEAD_DIM, HEAD_DIM), lambda d, n: (d, 0, nc - 1 - n, 0, 0)),
                  pl.BlockSpec((c, w), lambda d, n: (chunk_of(d, n), 0))],
        out_specs=[out_blk, out_blk, out_blk, lg_blk],
        out_shape=[grad_shape, grad_shape, grad_shape, jax.ShapeDtypeStruct((2, RET_HEADS, 1, 1), F32)],
        scratch_shapes=[pltpu.VMEM((RET_HEADS, HEAD_DIM, HEAD_DIM), F32)],
        compiler_params=_params(("parallel", "arbitrary")),
    )(p, p, z, lg, states, do)


_GQ_BLK = P_GQ // (GLA_HEADS * GLA_DK)
_GK_BLK = Z_GK // (GLA_HEADS * GLA_DK)
_GV_BLK = Z_GV // (GLA_HEADS * GLA_DV)
_LA_BLK = P_LA // (GLA_HEADS * GLA_DK)


def _gla_mask(direction):
    c = GLA_CHUNK
    i = lax.broadcasted_iota(jnp.int32, (c, c), 0)
    j = lax.broadcasted_iota(jnp.int32, (c, c), 1)
    return (jnp.where(direction == 0, i - j, j - i) >= 0).astype(F32)


def gla_chunk(q, k, v, la, st, mask):
    b = mask_cumsum(mask, la)
    btot = jnp.sum(la, axis=0, keepdims=True)
    half = 0.5 * btot
    qt, kt = q * jnp.exp(b - half), k * jnp.exp(half - b)
    qs, ke = q * jnp.exp(b), k * jnp.exp(btot - b)
    outs, upd = [], []
    for h in range(GLA_HEADS):
        ks = slice(h * GLA_DK, (h + 1) * GLA_DK)
        vh = v[:, h * GLA_DV:(h + 1) * GLA_DV]
        att = bdot_nt(qt[:, ks], kt[:, ks]) * mask
        outs.append(bdot(att, vh) + bdot_nt(qs[:, ks], st[:, ks]))
        upd.append(bdot_tn(vh, ke[:, ks]))
    st_new = st * jnp.exp(btot) + jnp.concatenate(upd, axis=1)
    return jnp.concatenate(outs, axis=1), st_new


def gla_fwd(p, z, n_lat):
    t = p.shape[0]
    c = GLA_CHUNK
    nc, nlc = t // c, n_lat // c
    kw, vw = GLA_HEADS * GLA_DK, GLA_HEADS * GLA_DV

    def body(q_ref, k_ref, v_ref, la_ref, o_ref, ssave_ref, s_s):
        d, n = pl.program_id(0), pl.program_id(1)

        @pl.when(n == 0)
        def _():
            s_s[...] = jnp.zeros_like(s_s)

        ssave_ref[0, 0] = s_s[...]
        o, s_new = gla_chunk(q_ref[...], k_ref[...], v_ref[...], la_ref[...], s_s[...], _gla_mask(d))
        o_ref[...] = o
        s_s[...] = s_new

    def chunk_of(d, n):
        return _scan_chunk(d, n, nc, nlc)

    return pl.pallas_call(
        body, name="gla_fwd", grid=(2, nc),
        in_specs=[pl.BlockSpec((c, kw), lambda d, n: (chunk_of(d, n), _GQ_BLK)),
                  pl.BlockSpec((c, kw), lambda d, n: (chunk_of(d, n), _GK_BLK)),
                  pl.BlockSpec((c, vw), lambda d, n: (chunk_of(d, n), _GV_BLK)),
                  pl.BlockSpec((c, kw), lambda d, n: (chunk_of(d, n), _LA_BLK + d))],
        out_specs=[pl.BlockSpec((c, vw), lambda d, n: (chunk_of(d, n), d)),
                   pl.BlockSpec((1, 1, GLA_DV, kw), lambda d, n: (d, n, 0, 0))],
        out_shape=[jax.ShapeDtypeStruct((t, 2 * vw), F32), jax.ShapeDtypeStruct((2, nc, GLA_DV, kw), F32)],
        scratch_shapes=[pltpu.VMEM((GLA_DV, kw), F32)],
        compiler_params=_params(("parallel", "arbitrary")),
    )(p, z, z, p)


def gla_bwd(p, z, states, do, n_lat):
    t = p.shape[0]
    c = GLA_CHUNK
    nc, nlc = t // c, n_lat // c
    kw, vw = GLA_HEADS * GLA_DK, GLA_HEADS * GLA_DV

    def body(q_ref, k_ref, v_ref, la_ref, s_ref, do_ref, dq_ref, dk_ref, dv_ref, dla_ref, ds_s):
        d, n = pl.program_id(0), pl.program_id(1)

        @pl.when(n == 0)
        def _():
            ds_s[...] = jnp.zeros_like(ds_s)

        f = functools.partial(gla_chunk, mask=_gla_mask(d))
        _, vjp = jax.vjp(f, q_ref[...], k_ref[...], v_ref[...], la_ref[...], s_ref[0, 0])
        dq_ref[...], dk_ref[...], dv_ref[...], dla_ref[...], ds_s[...] = vjp((do_ref[...], ds_s[...]))

    def chunk_of(d, n):
        return _scan_chunk(d, nc - 1 - n, nc, nlc)

    k_out = pl.BlockSpec((c, kw), lambda d, n: (chunk_of(d, n), d))
    return pl.pallas_call(
        body, name="gla_bwd", grid=(2, nc),
        in_specs=[pl.BlockSpec((c, kw), lambda d, n: (chunk_of(d, n), _GQ_BLK)),
                  pl.BlockSpec((c, kw), lambda d, n: (chunk_of(d, n), _GK_BLK)),
                  pl.BlockSpec((c, vw), lambda d, n: (chunk_of(d, n), _GV_BLK)),
                  pl.BlockSpec((c, kw), lambda d, n: (chunk_of(d, n), _LA_BLK + d)),
                  pl.BlockSpec((1, 1, GLA_DV, kw), lambda d, n: (d, nc - 1 - n, 0, 0)),
                  pl.BlockSpec((c, vw), lambda d, n: (chunk_of(d, n), 0))],
        out_specs=[k_out, k_out, pl.BlockSpec((c, vw), lambda d, n: (chunk_of(d, n), d)), k_out],
        out_shape=[jax.ShapeDtypeStruct((t, 2 * kw), F32), jax.ShapeDtypeStruct((t, 2 * kw), F32),
                   jax.ShapeDtypeStruct((t, 2 * vw), F32), jax.ShapeDtypeStruct((t, 2 * kw), F32)],
        scratch_shapes=[pltpu.VMEM((GLA_DV, kw), F32)],
        compiler_params=_params(("parallel", "arbitrary")),
    )(p, z, z, p, states, do)


def _adam_tile(w, g, m, v):
    m = ADAM_B1 * m + (1.0 - ADAM_B1) * g
    v = ADAM_B2 * v + (1.0 - ADAM_B2) * (g * g)
    m_hat = m / (1.0 - ADAM_B1 ** ADAM_STEP)
    v_hat = v / (1.0 - ADAM_B2 ** ADAM_STEP)
    delta = -ADAM_LR * (m_hat / (jnp.sqrt(v_hat) + ADAM_EPS) + ADAM_WD * w)
    return delta, m, v


def adamw(name, w, g, m, v):
    shape = w.shape
    cols = shape[-1] if w.ndim > 1 and shape[-1] >= LANES else int(np.prod(shape))
    rows = int(np.prod(shape)) // cols
    tr = rows
    for cand in (512, 256, 128, 64, 32, 16, 8):
        if rows % cand == 0 and cand * cols * 4 <= (1 << 20):
            tr = cand
            break
    flat = [a.reshape(rows, cols) for a in (w, g, m, v)]

    def body(w_ref, g_ref, m_ref, v_ref, d_ref, mo_ref, vo_ref):
        d_ref[...], mo_ref[...], vo_ref[...] = _adam_tile(w_ref[...], g_ref[...], m_ref[...], v_ref[...])

    spec = pl.BlockSpec((tr, cols), lambda i: (i, 0))
    outs = pl.pallas_call(
        body, name=name, grid=(rows // tr,),
        in_specs=[spec] * 4, out_specs=[spec] * 3,
        out_shape=[jax.ShapeDtypeStruct((rows, cols), F32)] * 3,
        compiler_params=_params(("parallel",)),
    )(*flat)
    return tuple(o.reshape(shape) for o in outs)


MESH = pl.DeviceIdType.MESH
_HBM = pl.BlockSpec(memory_space=pltpu.HBM)
N_CHIPS = 4
N_DEV = 8


def _place():
    x, y, c = lax.axis_index("x"), lax.axis_index("y"), lax.axis_index("c")
    chips = [(1 - x, y), (x, 1 - y), (1 - x, 1 - y)]
    return x, y, c, chips


def _remote(src, dst, send_sem, recv_sem, to):
    return pltpu.make_async_remote_copy(src_ref=src, dst_ref=dst, send_sem=send_sem, recv_sem=recv_sem,
                                        device_id=to, device_id_type=MESH)


def all_gather_small(name, v):
    m_per, n = v.shape

    def body(x_ref, out_ref, send_sems, recv_sems, local_sem):
        x, y, c, chips = _place()
        me, sibling = (x, y, c), (x, y, 1 - c)

        def rows(px, py, pc):
            return out_ref.at[pl.ds((4 * px + 2 * py + pc) * m_per, m_per), :]

        def copy(k, block, to, src=None):
            return _remote(rows(*block) if src is None else src, rows(*block), send_sems.at[k], recv_sems.at[k], to)

        mine = pltpu.make_async_copy(x_ref, rows(*me), local_sem)
        mine.start()
        first = [copy(0, me, sibling, src=x_ref)]
        first += [copy(1 + j, me, (*chip, c), src=x_ref) for j, chip in enumerate(chips)]
        for cp in first:
            cp.start()
        passed = [copy(4 + j, (*chip, c), sibling) for j, chip in enumerate(chips)]
        for j, chip in enumerate(chips):
            copy(1 + j, (*chip, c), me).wait_recv()
            passed[j].start()
        copy(0, sibling, me).wait_recv()
        for j, chip in enumerate(chips):
            copy(4 + j, (*chip, 1 - c), me).wait_recv()
        for cp in first + passed:
            cp.wait_send()
        mine.wait()

    return pl.pallas_call(
        body, name=name,
        out_shape=jax.ShapeDtypeStruct((N_DEV * m_per, n), v.dtype),
        in_specs=[pl.BlockSpec(memory_space=pltpu.VMEM)],
        out_specs=pl.BlockSpec(memory_space=pltpu.VMEM),
        scratch_shapes=[pltpu.SemaphoreType.DMA((7,)), pltpu.SemaphoreType.DMA((7,)), pltpu.SemaphoreType.DMA],
        compiler_params=pltpu.CompilerParams(vmem_limit_bytes=VMEM_LIMIT),
    )(v)


_SEM = pl.BlockSpec(memory_space=pltpu.SEMAPHORE)
_SPLIT_COPY = pltpu.CompilerParams(has_side_effects=pltpu.SideEffectType.DATAFLOW_SIDE_EFFECTING)


class CopyPlan(NamedTuple):
    copies: object
    n: int
    in_place: bool = False


def _gather_copies(x_ref, land_ref, x, y, c, chips):
    half = x_ref.shape[0] // 2
    rows = pl.ds(c * half, half)
    return [(x_ref.at[rows, :], land_ref.at[2 * x + y, rows, :], (*chip, c), land_ref.at[2 * chip[0] + chip[1], rows, :])
            for chip in chips]


def _pass_copies(land_ref, _, x, y, c, chips):
    half = land_ref.shape[1] // 2
    mine, other = pl.ds(c * half, half), pl.ds((1 - c) * half, half)
    return [(land_ref.at[2 * chip[0] + chip[1], mine, :], land_ref.at[2 * chip[0] + chip[1], mine, :], (x, y, 1 - c),
             land_ref.at[2 * chip[0] + chip[1], other, :]) for chip in chips]


def _sibling_half_copies(p_ref, land_ref, x, y, c, chips):
    half = p_ref.shape[1] // 2
    return [(p_ref.at[:, pl.ds((1 - c) * half, half), :], land_ref, (x, y, 1 - c), land_ref)]


def _scatter_copies(s_ref, land_ref, x, y, c, chips):
    return [(s_ref.at[2 * chip[0] + chip[1]], land_ref.at[j], (*chip, c), land_ref.at[j]) for j, chip in enumerate(chips)]


def _join_copies(buf_ref, _, x, y, c, chips):
    half = buf_ref.shape[0] // 2
    mine = buf_ref.at[pl.ds(c * half, half), :]
    return [(mine, mine, (x, y, 1 - c), buf_ref.at[pl.ds((1 - c) * half, half), :])]


GATHER = CopyPlan(_gather_copies, 3)
PASS_ON = CopyPlan(_pass_copies, 3, in_place=True)
SIBLING_HALF = CopyPlan(_sibling_half_copies, 1)
SCATTER = CopyPlan(_scatter_copies, 3)
JOIN = CopyPlan(_join_copies, 1, in_place=True)


def split_start(name, plan, srcs, land_shapes=None):
    nt = len(srcs)
    arrays = [pltpu.with_memory_space_constraint(s, pltpu.HBM) for s in srcs]
    if not plan.in_place:
        arrays += [pltpu.with_memory_space_constraint(lax.empty(shape, s.dtype), pltpu.HBM) for shape, s in zip(land_shapes, srcs)]
    na = len(arrays)

    def body(*refs):
        x_refs = refs[:nt]
        land_refs = x_refs if plan.in_place else refs[nt:na]
        send, recv = refs[na:na + nt], refs[na + nt:na + 2 * nt]
        x, y, c, chips = _place()
        for t in range(nt):
            for j, (src, dst, to, _) in enumerate(plan.copies(x_refs[t], land_refs[t], x, y, c, chips)):
                _remote(src, dst, send[t].at[j], recv[t].at[j], to).start()
        refs[-1][...] = jnp.zeros_like(refs[-1])

    outs = pl.pallas_call(
        body, name=name,
        out_shape=tuple([pltpu.SemaphoreType.DMA((plan.n,))] * (2 * nt) + [pltpu.HBM(a.shape, a.dtype) for a in arrays]
                        + [jax.ShapeDtypeStruct((8, LANES), F32)]),
        in_specs=[_HBM] * na,
        out_specs=tuple([_SEM] * (2 * nt) + [_HBM] * na + [pl.BlockSpec(memory_space=pltpu.VMEM)]),
        input_output_aliases={i: 2 * nt + i for i in range(na)},
        compiler_params=_SPLIT_COPY,
    )(*arrays)
    groups = [(outs[t], outs[nt + t]) + tuple(outs[2 * nt + t + k * nt] for k in range(na // nt)) for t in range(nt)]
    return groups, outs[-1]


def split_wait(name, plan, group, after):
    send, recv, *arrays = group
    na = len(arrays)

    def body(*refs):
        x_ref, land_ref = refs[0], refs[na - 1]
        send_sem, recv_sem = refs[na], refs[na + 1]
        x, y, c, chips = _place()
        for j, (s, _, to, arrival) in enumerate(plan.copies(x_ref, land_ref, x, y, c, chips)):
            cp = _remote(s, arrival, send_sem.at[j], recv_sem.at[j], to)
            cp.wait_send()
            cp.wait_recv()

    return pl.pallas_call(
        body, name=name,
        out_shape=tuple(pltpu.HBM(a.shape, a.dtype) for a in arrays),
        in_specs=tuple([_HBM] * na + [_SEM, _SEM, pl.BlockSpec(memory_space=pl.ANY)]), out_specs=tuple([_HBM] * na),
        input_output_aliases={i: i for i in range(na)}, compiler_params=_SPLIT_COPY,
    )(*arrays, send, recv, after)


def _rows_tile(rows, cols):
    for cand in (512, 256, 128, 64, 32, 16):
        if rows % cand == 0 and cand * cols * 4 <= (1 << 21):
            return cand
    return rows


def add_sibling_half(name, pieces, from_sibling, core):
    n, h, cols = from_sibling.shape
    tr = _rows_tile(h, cols)
    nb = h // tr

    def body(c_ref, a_ref, b_ref, o_ref):
        o_ref[...] = (a_ref[...].astype(F32) + b_ref[...].astype(F32)).astype(o_ref.dtype)

    blk = pl.BlockSpec((1, tr, cols), lambda q, i, c_ref: (q, i, 0))
    return pl.pallas_call(
        body, name=name,
        grid_spec=pltpu.PrefetchScalarGridSpec(
            num_scalar_prefetch=1, grid=(n, nb),
            in_specs=[pl.BlockSpec((1, tr, cols), lambda q, i, c_ref: (q, c_ref[0] * nb + i, 0)), blk], out_specs=blk),
        out_shape=jax.ShapeDtypeStruct((n, h, cols), BF16),
        compiler_params=_params(("parallel", "parallel")),
    )(core.reshape(1).astype(jnp.int32), pieces, from_sibling)


def add_chip_sums(name, chip_sums, from_chips, chip, core):
    _, h, cols = chip_sums.shape
    tr = _rows_tile(h, cols)
    nb = h // tr

    def body(s_ref, own_ref, r0_ref, r1_ref, r2_ref, o_ref):
        acc = own_ref[0].astype(F32) + r0_ref[0].astype(F32)
        o_ref[...] = acc + r1_ref[0].astype(F32) + r2_ref[0].astype(F32)

    def got(j):
        return pl.BlockSpec((1, tr, cols), lambda i, s_ref: (j, i, 0))

    return pl.pallas_call(
        body, name=name,
        grid_spec=pltpu.PrefetchScalarGridSpec(
            num_scalar_prefetch=1, grid=(nb,),
            in_specs=[pl.BlockSpec((1, tr, cols), lambda i, s_ref: (s_ref[0], i, 0)), got(0), got(1), got(2)],
            out_specs=pl.BlockSpec((tr, cols), lambda i, s_ref: (s_ref[1] * nb + i, 0))),
        out_shape=jax.ShapeDtypeStruct((2 * h, cols), F32),
        compiler_params=_params(("parallel",)),
    )(jnp.stack([chip, core]).astype(jnp.int32), chip_sums, from_chips, from_chips, from_chips)


def sum_device_blocks(name, g):
    n = g.shape[1]

    def body(g_ref, o_ref):
        acc = g_ref[0:8, :]
        for d in range(1, N_DEV):
            acc = acc + g_ref[8 * d:8 * (d + 1), :]
        o_ref[...] = acc

    return pl.pallas_call(body, name=name, out_shape=jax.ShapeDtypeStruct((8, n), F32),
                          compiler_params=pltpu.CompilerParams(vmem_limit_bytes=VMEM_LIMIT))(g)


class LayerWeights(NamedTuple):
    norm1_g: jax.Array
    q_g: jax.Array
    k_g: jax.Array
    lg: jax.Array
    ret_g: jax.Array
    gate_up: jax.Array
    gate_b: jax.Array
    gla_g: jax.Array
    norm2_g: jax.Array
    conv_w: jax.Array
    conv_b: jax.Array


def _mod(mods, k):
    return mods[:, k:k + 1, :]


def out_view(l, tb):
    rows = D_MODEL // N_CHIPS
    if tb:
        return BView(n=D_MODEL, k=D_MODEL, tn=rows, tk=D_MODEL, index_map=lambda i, j, kk: (j, l, kk))
    return BView(n=D_MODEL, k=D_MODEL, tn=1024, tk=rows, index_map=lambda i, j, kk: (kk, l, j))


def down_view(l, f, tb):
    rows = f // N_CHIPS
    if tb:
        return BView(n=f, k=D_MODEL, tn=rows, tk=D_MODEL, index_map=lambda i, j, kk: (j, l, kk))
    return BView(n=D_MODEL, k=f, tn=1024, tk=rows, index_map=lambda i, j, kk: (kk, l, j))


def up_view(l, f, part=None):
    cols = 2 * f // N_CHIPS
    tc = _pick(cols, (1408, 1024, 512, 256))
    nbc = cols // tc
    if part is None:
        return BView(n=2 * f, k=D_MODEL, tn=tc, tk=D_MODEL, index_map=lambda i, j, kk: (j // nbc, l, j % nbc))
    nnb = D_MODEL // 1024
    return BView(n=D_MODEL, k=f, tn=1024, tk=tc, index_map=lambda i, j, kk: (2 * part + kk // nbc, l * nnb + j, kk % nbc))


def ada_view(l, n_ada, tb):
    if tb:
        return BView(n=D_MODEL, k=n_ada, tn=1024, tk=n_ada, index_map=lambda i, j, kk: (l, j, 0))
    return BView(n=n_ada, k=D_MODEL, tn=1024, tk=D_MODEL, index_map=lambda i, j, kk: (l, 0, j))


def _prep_args(z, zg, cos, sin, w):
    rows = [Row(z, Z_AV, 0), Row(z, 512, Z_RQ // 512), Row(z, 512, Z_RK // 512), Row(z, 256, Z_GQ // 256),
            Row(zg, LANES, 0), Row(cos, HEAD_DIM, 0, False), Row(sin, HEAD_DIM, 0, False)]
    return rows, [Par(w.q_g), Par(w.k_g), Par(w.gate_up), Par(w.gate_b)]


def _post_args(o_att, o_ret, o_gla, z, w):
    rows = [Row(o_att, 1024), Row(o_ret, 512, 0), Row(o_ret, 512, 1, False), Row(o_gla, 512, 0), Row(o_gla, 512, 1, False),
            Row(z, 512, Z_RG // 512), Row(z, 512, Z_GR // 512)]
    return rows, [Par(w.ret_g), Par(w.gla_g)]


def layer_fwd(l, xs, mods, w, fetch, cos, sin, n_lat):
    t, d = xs.shape
    tag = f"l{l}_"
    nm1 = [Par(w.norm1_g), Par(_mod(mods, 0), True), Par(_mod(mods, 1), True)]
    (h,) = row_map(tag + "norm1", normmod_tile, [Row(xs, d)], nm1, [(d, BF16)], t, n_lat)
    w_main, w_gate = fetch("w_in", h)
    z = matmul(tag + "in_proj", h, w_main)
    zg = matmul(tag + "gate_proj", h, w_gate)
    rows, pars = _prep_args(z, zg, cos, sin, w)
    (p,) = row_map(tag + "prep", prep_tile, rows, pars, [(P_W, F32)], t, n_lat)
    o_att, lse = attn_fwd(p, z, n_lat)
    o_ret, s_ret = ret_fwd(p, z, w.lg, n_lat)
    o_gla, s_gla = gla_fwd(p, z, n_lat)
    rows, pars = _post_args(o_att, o_ret, o_gla, z, w)
    (m,) = row_map(tag + "post", post_tile, rows, pars, [(d, BF16)], t, n_lat)
    g_out = fetch("w_out", m)
    y = matmul(tag + "out_proj", m, g_out, view=out_view(0, False))
    (x1,) = row_map(tag + "resid1", resid_tile, [Row(xs, d), Row(y, d)], [Par(_mod(mods, 2), True)], [(d, F32)], t, n_lat)
    nm2 = [Par(w.norm2_g), Par(_mod(mods, 3), True), Par(_mod(mods, 4), True)]
    (h2,) = row_map(tag + "norm2", normmod_tile, [Row(x1, d)], nm2, [(d, BF16)], t, n_lat)
    f = w.conv_b.shape[1]
    g_up = fetch("w_up", h2)
    u = matmul(tag + "up_proj", h2, g_up, view=up_view(0, f))
    g = convglu(tag + "convglu", u, w.conv_w, w.conv_b, n_lat)
    g_down = fetch("w_down", g)
    yd = matmul(tag + "down_proj", g, g_down, view=down_view(0, f, False))
    (x2,) = row_map(tag + "resid2", resid_tile, [Row(x1, d), Row(yd, d)], [Par(_mod(mods, 5), True)], [(d, F32)], t, n_lat)
    saved = dict(xs=xs, h=h, z=z, zg=zg, p=p, o_att=o_att, lse=lse, o_ret=o_ret, s_ret=s_ret, o_gla=o_gla, s_gla=s_gla,
                 m=m, y=y, x1=x1, h2=h2, u=u, g=g, yd=yd, w_main=w_main, w_gate=w_gate, g_out=g_out, g_up=g_up, g_down=g_down)
    return x2, saved


def _sum_dirs(a):
    w = a.shape[1] // 2
    return a[:, :w] + a[:, w:]


def layer_bwd(l, dx2, s, mods, w, cos, sin, n_lat, grad_ready):
    t, d = dx2.shape
    tag = f"l{l}_b_"
    dyd, dgate5 = row_vjp(tag + "resid2", resid_tile, [Row(s["x1"], d, 0, False), Row(s["yd"], d)],
                          [Par(_mod(mods, 5), True)], [dx2], t, n_lat, row_grad_dtype=BF16)
    f = w.conv_b.shape[1]
    dg = matmul(tag + "down_dx", dyd, s["g_down"], tb=True, view=down_view(0, f, True))
    dw_down = matmul(tag + "down_dw", s["g"], dyd, ta=True, out_dtype=BF16)
    da, dv, dcw, dcb = convglu_bwd(tag + "convglu", s["u"], w.conv_w, w.conv_b, dg, n_lat)
    dh2 = matmul(tag + "up_dx_gate", da, s["g_up"], tb=True, view=up_view(0, f, 0))
    dh2 = matmul(tag + "up_dx_value", dv, s["g_up"], tb=True, view=up_view(0, f, 1), add=dh2)
    dw_up = (matmul(tag + "up_dw_gate", s["h2"], da, ta=True, out_dtype=BF16),
             matmul(tag + "up_dw_value", s["h2"], dv, ta=True, out_dtype=BF16))
    started = grad_ready("ffn", dict(w_up=dw_up, w_down=dw_down))
    nm2 = [Par(w.norm2_g), Par(_mod(mods, 3), True), Par(_mod(mods, 4), True)]
    dx1, dg2, dshift3, dscale4 = row_vjp(tag + "norm2", normmod_tile, [Row(s["x1"], d)], nm2, [dh2], t, n_lat,
                                         add_to_first=dx2, after=started)
    dy, dgate2 = row_vjp(tag + "resid1", resid_tile, [Row(s["xs"], d, 0, False), Row(s["y"], d)],
                         [Par(_mod(mods, 2), True)], [dx1], t, n_lat, row_grad_dtype=BF16)
    dm = matmul(tag + "out_dx", dy, s["g_out"], tb=True, view=out_view(0, True))
    dw_out = matmul(tag + "out_dw", s["m"], dy, ta=True, out_dtype=BF16)
    rows, pars = _post_args(s["o_att"], s["o_ret"], s["o_gla"], s["z"], w)
    started = grad_ready("w_out", dict(w_out=dw_out))
    do_att, do_ret, do_gla, d_rg, d_gr, d_ret_g, d_gla_g = row_vjp(tag + "post", post_tile, rows, pars, [dm], t, n_lat, after=started)
    dq_a, dk_a, dv_a = attn_bwd(s["p"], s["z"], s["o_att"], s["lse"], do_att, n_lat)
    dq_r, dk_r, dv_r, dlg = ret_bwd(s["p"], s["z"], w.lg, s["s_ret"], do_ret, n_lat)
    dq_g, dk_g, dv_g, dla = gla_bwd(s["p"], s["z"], s["s_gla"], do_gla, n_lat)
    dp = jnp.concatenate([dq_a, dk_a, _sum_dirs(dq_g), _sum_dirs(dq_r), _sum_dirs(dk_r), dla], axis=1)
    rows, pars = _prep_args(s["z"], s["zg"], cos, sin, w)
    d_zqk, d_zrq, d_zrk, d_zgq, dzg, d_qg, d_kg, d_up, d_gb = row_vjp(tag + "prep", prep_tile, rows, pars, [dp], t, n_lat)
    dz = jnp.concatenate([d_zqk, dv_a, d_zrq, d_zrk, _sum_dirs(dv_r), d_rg, d_zgq, _sum_dirs(dk_g), _sum_dirs(dv_g), d_gr], axis=1)
    dz, dzg = dz.astype(BF16), dzg.astype(BF16)
    dh_gate = matmul(tag + "gate_dx", dzg, s["w_gate"], tb=True)
    dh = matmul(tag + "in_dx", dz, s["w_main"], tb=True, add=dh_gate)
    dw_main = matmul(tag + "in_dw", s["h"], dz, ta=True, out_dtype=BF16)
    dw_gate = matmul(tag + "gate_dw", s["h"], dzg, ta=True, out_dtype=BF16)
    started = grad_ready("w_in", dict(w_main=dw_main, w_gate=dw_gate))
    nm1 = [Par(w.norm1_g), Par(_mod(mods, 0), True), Par(_mod(mods, 1), True)]
    dx, dg1, dshift0, dscale1 = row_vjp(tag + "norm1", normmod_tile, [Row(s["xs"], d)], nm1, [dh], t, n_lat,
                                        add_to_first=dx1, after=started)
    dmods = jnp.concatenate([dshift0, dscale1, dgate2, dshift3, dscale4, dgate5], axis=1)
    grads = dict(w_main=dw_main, w_gate=dw_gate, w_out=dw_out, w_up=dw_up, w_down=dw_down, norm1_g=dg1, q_g=d_qg, k_g=d_kg,
                 lg=dlg, ret_g=d_ret_g, gate_up=d_up, gate_b=d_gb, gla_g=d_gla_g, norm2_g=dg2, conv_w=dcw, conv_b=dcb)
    return dx, dmods, grads


def rope_tables(n_lat, n_ctx):
    rows = n_lat // GRID_W
    row = jnp.repeat(jnp.arange(rows, dtype=F32), GRID_W)
    col = jnp.tile(jnp.arange(GRID_W, dtype=F32), rows)
    n_freq = HEAD_DIM // 4
    inv_freq = ROPE_THETA ** (-jnp.arange(n_freq, dtype=F32) / n_freq)
    ang = jnp.concatenate([row[:, None] * inv_freq, col[:, None] * inv_freq], axis=-1)
    cos, sin = jnp.cos(ang), jnp.sin(ang)
    cos = jnp.concatenate([jnp.concatenate([cos, cos], axis=1), jnp.ones((n_ctx, HEAD_DIM), F32)], axis=0)
    sin = jnp.concatenate([jnp.concatenate([-sin, sin], axis=1), jnp.zeros((n_ctx, HEAD_DIM), F32)], axis=0)
    return cos, sin


def local_step(xs, target, mods, weights, fetch, final_g, n_lat, grad_ready):
    t, d = xs.shape
    cos, sin = rope_tables(n_lat, t - n_lat)
    saved = []
    h = xs
    for l, w in enumerate(weights):
        h, s = layer_fwd(l, h, mods[l], w, functools.partial(fetch, l), cos, sin, n_lat)
        saved.append(s)
    loss, dlat, dgf = final_loss(h, target, final_g, n_lat)
    dx = jnp.concatenate([dlat, jnp.zeros((t - n_lat, d), F32)], axis=0)
    dmods, grads = [None] * len(weights), [None] * len(weights)
    for l in reversed(range(len(weights))):
        dx, dmods[l], grads[l] = layer_bwd(l, dx, saved[l], mods[l], weights[l], cos, sin, n_lat, functools.partial(grad_ready, l))
    return loss, dx, dmods, grads, dgf


WEIGHT_NAMES = ("c_ctx", "ada_w", "ada_b", "norm1_g", "w_in", "q_norm_g", "k_norm_g", "ret_log_decay", "ret_norm_g",
                "gla_gate_up", "gla_gate_b", "gla_norm_g", "w_out", "norm2_g", "w_up", "conv_w", "conv_b", "w_down", "final_norm_g")
PACK_QUANTUM = 8 * LANES


def _pack(arrays):
    flat = jnp.concatenate([a.reshape(-1).astype(F32) for a in arrays])
    n = -(-flat.shape[0] // PACK_QUANTUM) * PACK_QUANTUM
    return jnp.pad(flat, (0, n - flat.shape[0])).reshape(8, n // 8)


def _unpack(flat2d, shapes):
    out, at = [], 0
    for s in shapes:
        size = int(np.prod(s))
        out.append(flat2d[:, at:at + size].reshape((flat2d.shape[0],) + tuple(s)))
        at += size
    return out


def _per_device(gathered):
    return gathered.reshape(N_DEV, -1)


def _from_chips(per_device, axis):
    chips = per_device[0::2]
    moved = jnp.moveaxis(chips, 0, axis)
    shape = moved.shape
    return moved.reshape(shape[:axis] + (shape[axis] * shape[axis + 1],) + shape[axis + 2:])


def kernel(x, c, ctx, c_ctx, ada_w, ada_b, norm1_g, w_in, q_norm_g, k_norm_g, ret_log_decay, ret_norm_g, gla_gate_up, gla_gate_b, gla_norm_g, w_out, norm2_g, w_up, conv_w, conv_b, w_down, final_norm_g, loss_target, m_c_ctx, m_ada_w, m_ada_b, m_norm1_g, m_w_in, m_q_norm_g, m_k_norm_g, m_ret_log_decay, m_ret_norm_g, m_gla_gate_up, m_gla_gate_b, m_gla_norm_g, m_w_out, m_norm2_g, m_w_up, m_conv_w, m_conv_b, m_w_down, m_final_norm_g, v_c_ctx, v_ada_w, v_ada_b, v_norm1_g, v_w_in, v_q_norm_g, v_k_norm_g, v_ret_log_decay, v_ret_norm_g, v_gla_gate_up, v_gla_gate_b, v_gla_norm_g, v_w_out, v_norm2_g, v_w_up, v_conv_w, v_conv_b, v_w_down, v_final_norm_g):
    weights = dict(zip(WEIGHT_NAMES, (c_ctx, ada_w, ada_b, norm1_g, w_in, q_norm_g, k_norm_g, ret_log_decay, ret_norm_g,
                                      gla_gate_up, gla_gate_b, gla_norm_g, w_out, norm2_g, w_up, conv_w, conv_b, w_down, final_norm_g)))
    mom_m = dict(zip(WEIGHT_NAMES, (m_c_ctx, m_ada_w, m_ada_b, m_norm1_g, m_w_in, m_q_norm_g, m_k_norm_g, m_ret_log_decay, m_ret_norm_g,
                                    m_gla_gate_up, m_gla_gate_b, m_gla_norm_g, m_w_out, m_norm2_g, m_w_up, m_conv_w, m_conv_b, m_w_down, m_final_norm_g)))
    mom_v = dict(zip(WEIGHT_NAMES, (v_c_ctx, v_ada_w, v_ada_b, v_norm1_g, v_w_in, v_q_norm_g, v_k_norm_g, v_ret_log_decay, v_ret_norm_g,
                                    v_gla_gate_up, v_gla_gate_b, v_gla_norm_g, v_w_out, v_norm2_g, v_w_up, v_conv_w, v_conv_b, v_w_down, v_final_norm_g)))
    depth, d = norm1_g.shape
    assert d == D_MODEL and x.shape[0] == 1
    n_lat, n_ctx, f = x.shape[1], ctx.shape[1], conv_b.shape[1]
    assert n_lat % ROW_TILE == 0 and n_ctx % ROW_TILE == 0 and f % FFN_COL_TILE == 0 and f % N_CHIPS == 0
    n_in = w_in.shape[2]
    n_ada = ada_w.shape[2]
    xi, yi, ci = lax.axis_index("x"), lax.axis_index("y"), lax.axis_index("c")
    chip = 2 * xi + yi
    dev = 2 * chip + ci

    big = ("w_in", "w_out", "w_up", "w_down")
    order = [(l, name) for l in range(depth) for name in big]
    shards = [weights[name][l].astype(BF16) for l, name in order]
    in_flight, token = split_start("gather_start", GATHER, shards, [(N_CHIPS,) + s.shape for s in shards])
    passing = {}

    def pass_on(k, after):
        tag = "{1}{0}".format(*order[k])
        own, land = split_wait("gather_wait_" + tag, GATHER, in_flight[k], after)
        (moving,), _ = split_start("gather_pass_" + tag, PASS_ON, [land])
        passing[k] = (own, moving)

    def fetch(l, name, after):
        k = order.index((l, name))
        if k == 0:
            pass_on(0, after)
        own, moving = passing.pop(k)
        (land,) = split_wait(f"gather_pass_wait_{name}{l}", PASS_ON, moving, after)
        if k + 1 < len(order):
            pass_on(k + 1, after)
        land = lax.dynamic_update_slice_in_dim(land, own[None], chip, axis=0)
        if name != "w_in":
            return land
        cols = jnp.concatenate([land[q] for q in range(N_CHIPS)], axis=1)
        return cols[:, :N_MAIN], jnp.pad(cols[:, N_MAIN:], ((0, 0), (0, LANES - N_GATE)))

    small_shapes = [c.shape[1:], conv_w.shape, gla_gate_up.shape, gla_gate_b.shape]
    got = _per_device(all_gather_small("gather_small", _pack([c, conv_w, gla_gate_up, gla_gate_b])))
    c_all, conv_w_sh, gate_up_sh, gate_b_sh = _unpack(got, small_shapes)
    conv_w_full = _from_chips(conv_w_sh, 2)
    gate_up_full = _from_chips(gate_up_sh, 3)
    gate_b_full = _from_chips(gate_b_sh, 2)

    act = jnp.zeros((16, d), F32).at[0:N_DEV].set(jax.nn.silu(c_all)).at[N_DEV].set(jax.nn.silu(c_ctx))
    mod_sh = jnp.stack([matmul(f"ada_fwd{l}", act, ada_w, view=ada_view(l, n_ada, False)) for l in range(depth)])
    got = _per_device(all_gather_small("gather_mods", _pack([mod_sh])))
    (mod_sh_all,) = _unpack(got, [mod_sh.shape])
    mod_full = _from_chips(mod_sh_all, 2) + ada_b[:, None, :]
    mod_mine = lax.dynamic_index_in_dim(mod_full, dev, axis=1, keepdims=False)
    mods = [jnp.stack([mod_mine[l].reshape(N_MOD, d), mod_full[l, N_DEV].reshape(N_MOD, d)]) for l in range(depth)]

    layer_w = []
    for l in range(depth):
        up = jnp.zeros((2, LANES, GLA_HEADS * GLA_DK), F32)
        up = up.at[0, 0:GLA_RANK].set(gate_up_full[l, 0]).at[1, GLA_RANK:2 * GLA_RANK].set(gate_up_full[l, 1])
        layer_w.append(LayerWeights(
            norm1_g=norm1_g[l].reshape(1, 1, d), q_g=q_norm_g[l].reshape(1, 1, HEAD_DIM), k_g=k_norm_g[l].reshape(1, 1, HEAD_DIM),
            lg=ret_log_decay[l].reshape(2, RET_HEADS, 1, 1), ret_g=ret_norm_g[l].reshape(1, 1, HEAD_DIM),
            gate_up=up.reshape(1, 2 * LANES, -1), gate_b=gate_b_full[l].reshape(1, 2, -1), gla_g=gla_norm_g[l].reshape(1, 1, HEAD_DIM),
            norm2_g=norm2_g[l].reshape(1, 1, d), conv_w=conv_w_full[l], conv_b=conv_b[l].reshape(1, f)))

    def pieces_of(name, g):
        if name == "w_in":
            full_cols = jnp.concatenate([g["w_main"], g["w_gate"][:, :N_GATE]], axis=1)
            return jnp.stack([full_cols[:, q * n_in:(q + 1) * n_in] for q in range(N_CHIPS)])
        if name == "w_up":
            gate, value = g["w_up"]
            return jnp.stack([gate[:, :f // 2], gate[:, f // 2:], value[:, :f // 2], value[:, f // 2:]])
        return g[name].reshape(N_CHIPS, -1, d)

    groups = {"ffn": ("w_up", "w_down"), "w_out": ("w_out",), "w_in": ("w_in",)}
    reducing = {}
    to_sibling = []

    def sibling_arrived(after):
        if not to_sibling:
            return None
        l, group, in_flight_halves = to_sibling.pop()
        sums = []
        for name, halves in zip(groups[group], in_flight_halves):
            pieces, from_sibling = split_wait(f"rs_sibling_wait_{name}{l}", SIBLING_HALF, halves, after)
            sums.append(add_sibling_half(f"rs_add_sibling_{name}{l}", pieces, from_sibling, ci))
        in_flight_sums, started = split_start(f"rs_start_{group}{l}", SCATTER, sums, [(3,) + s.shape[1:] for s in sums])
        reducing.update({(l, name): grp for name, grp in zip(groups[group], in_flight_sums)})
        return started

    def grad_ready(l, group, g):
        pieces = [pieces_of(name, g) for name in groups[group]]
        before = sibling_arrived(pieces[0])
        in_flight_halves, started = split_start(f"rs_sibling_{group}{l}", SIBLING_HALF, pieces,
                                                [(N_CHIPS, pc.shape[1] // 2, pc.shape[2]) for pc in pieces])
        to_sibling.append((l, group, in_flight_halves))
        return started if before is None else started + before

    xs = jnp.concatenate([x[0], ctx[0]], axis=0) + token[0, 0]
    loss, dx, dmods, grads, dgf = local_step(xs, loss_target[0], mods, layer_w, fetch, final_norm_g.reshape(1, d), n_lat, grad_ready)
    sibling_arrived(dx)

    def gate_up_grad(g):
        return jnp.stack([g[0, 0:GLA_RANK], g[0, LANES + GLA_RANK:LANES + 2 * GLA_RANK]])

    per_layer = [[dmods[l][0], dmods[l][1], grads[l]["norm1_g"], grads[l]["norm2_g"], grads[l]["q_g"], grads[l]["k_g"],
                  grads[l]["ret_g"], grads[l]["gla_g"], grads[l]["lg"], gate_up_grad(grads[l]["gate_up"]), grads[l]["gate_b"],
                  grads[l]["conv_w"], grads[l]["conv_b"]] for l in range(depth)]
    layer_shapes = [(N_MOD * d,), (N_MOD * d,), (d,), (d,), (HEAD_DIM,), (HEAD_DIM,), (HEAD_DIM,), (HEAD_DIM,), (2, RET_HEADS),
                    (2, GLA_RANK, GLA_HEADS * GLA_DK), (2, GLA_HEADS * GLA_DK), (3, f), (f,)]
    packed = _pack([a for lay in per_layer for a in lay] + [dgf, loss[0, 0:1]])
    gathered = all_gather_small("gather_small_grads", packed)
    every = _unpack(_per_device(gathered), layer_shapes * depth + [(d,), (1,)])
    total = _unpack(sum_device_blocks("sum_small_grads", gathered).reshape(1, -1), layer_shapes * depth + [(d,), (1,)])
    nl = len(layer_shapes)

    def tot(l, k):
        return total[l * nl + k][0]

    out = {"norm1_g": jnp.stack([tot(l, 2) for l in range(depth)]), "norm2_g": jnp.stack([tot(l, 3) for l in range(depth)]),
           "q_norm_g": jnp.stack([tot(l, 4) for l in range(depth)]), "k_norm_g": jnp.stack([tot(l, 5) for l in range(depth)]),
           "ret_norm_g": jnp.stack([tot(l, 6) for l in range(depth)]), "gla_norm_g": jnp.stack([tot(l, 7) for l in range(depth)]),
           "ret_log_decay": jnp.stack([tot(l, 8) for l in range(depth)]),
           "gla_gate_up": lax.dynamic_slice_in_dim(jnp.stack([tot(l, 9) for l in range(depth)]), chip * gla_gate_up.shape[3], gla_gate_up.shape[3], axis=3),
           "gla_gate_b": lax.dynamic_slice_in_dim(jnp.stack([tot(l, 10) for l in range(depth)]), chip * gla_gate_b.shape[2], gla_gate_b.shape[2], axis=2),
           "conv_w": lax.dynamic_slice_in_dim(jnp.stack([tot(l, 11) for l in range(depth)]), chip * conv_w.shape[2], conv_w.shape[2], axis=2),
           "conv_b": jnp.stack([tot(l, 12) for l in range(depth)]),
           "final_norm_g": total[depth * nl][0],
           "ada_b": jnp.stack([tot(l, 0) + tot(l, 1) for l in range(depth)])}
    loss_total = total[depth * nl + 1][0, 0]

    dmod_all = jnp.zeros((depth, 16, N_MOD * d), F32)
    for l in range(depth):
        dmod_all = dmod_all.at[l, 0:N_DEV].set(every[l * nl][:, :]).at[l, N_DEV].set(tot(l, 1))
    dmod_cols = lax.dynamic_slice_in_dim(dmod_all, chip * n_ada, n_ada, axis=2)
    out["ada_w"] = jnp.stack([matmul(f"ada_dw{l}", act, dmod_cols[l], ta=True) for l in range(depth)])
    dact = matmul("ada_dx0", dmod_cols[0], ada_w, tb=True, view=ada_view(0, n_ada, True))
    for l in range(1, depth):
        dact = matmul(f"ada_dx{l}", dmod_cols[l], ada_w, tb=True, view=ada_view(l, n_ada, True), add=dact)
    got = _per_device(all_gather_small("gather_dcctx", _pack([dact[N_DEV]])))[0::2, :d]
    dsilu = got[0] + got[1] + got[2] + got[3]
    sig = jax.nn.sigmoid(c_ctx)
    out["c_ctx"] = dsilu * (sig + c_ctx * sig * (1.0 - sig))

    deltas, new_m, new_v = {}, {}, {}

    def update(name):
        out[name] = out[name].reshape(weights[name].shape)
        deltas[name], new_m[name], new_v[name] = adamw("adamw_" + name, weights[name], out[name], mom_m[name], mom_v[name])

    for name in WEIGHT_NAMES:
        if name not in big:
            update(name)
    behind = new_v["ada_w"]
    joining = []

    def joined(after):
        name, in_flight_halves = joining.pop()
        out[name] = jnp.stack([split_wait(f"rs_join_wait_{name}{l}", JOIN, grp, after)[0] for l, grp in enumerate(in_flight_halves)])
        update(name)
        return new_v[name]

    for name in ("w_down", "w_up", "w_out", "w_in"):
        halves = []
        for l in range(depth):
            sums, got = split_wait(f"rs_wait_{name}{l}", SCATTER, reducing[(l, name)], behind)
            halves.append(add_chip_sums(f"rs_add_chips_{name}{l}", sums, got, chip, ci))
        in_flight_halves, _ = split_start("rs_join_" + name, JOIN, halves)
        if joining:
            behind = joined(behind)
        joining.append((name, in_flight_halves))
    joined(behind)
    grad_x = dx[:n_lat].reshape(x.shape)
    return (loss_total, grad_x, *[out[n] for n in WEIGHT_NAMES], *[deltas[n] for n in WEIGHT_NAMES],
            *[new_m[n] for n in WEIGHT_NAMES], *[new_v[n] for n in WEIGHT_NAMES])
```

```python
import functools
from typing import NamedTuple

import numpy as np
import jax
import jax.numpy as jnp
from jax import lax
from jax.experimental import pallas as pl
from jax.experimental.pallas import tpu as pltpu

F32 = jnp.float32
BF16 = jnp.bfloat16

D_MODEL = 2048
HEAD_DIM = 128
ATT_Q_HEADS = 8
ATT_KV_HEADS = 2
ATT_GROUP = ATT_Q_HEADS // ATT_KV_HEADS
RET_HEADS = 4
GLA_HEADS = 4
GLA_DK = 64
GLA_DV = 128
GLA_RANK = 16
GLA_TAU = 16.0
RET_CHUNK = 128
GLA_CHUNK = 64
GRID_W = 64
ROPE_THETA = 10000.0
N_MOD = 6
EPS = 1e-6
N_MAIN = 5120
N_GATE = 2 * GLA_RANK
LANES = 128
ROW_TILE = 256
FFN_COL_TILE = 256
VMEM_LIMIT = 56 * 1024 * 1024

ADAM_LR = 0.001
ADAM_B1 = 0.9
ADAM_B2 = 0.999
ADAM_EPS = 1e-08
ADAM_WD = 0.01
ADAM_STEP = 10

Z_AQ, Z_AK, Z_AV = 0, 1024, 1280
Z_RQ, Z_RK, Z_RV, Z_RG = 1536, 2048, 2560, 3072
Z_GQ, Z_GK, Z_GV, Z_GR = 3584, 3840, 4096, 4608
P_AQ, P_AK, P_GQ, P_RQ, P_RK, P_LA = 0, 1024, 1280, 1536, 2048, 2560
P_W = 3072


def _params(sem=None):
    return pltpu.CompilerParams(dimension_semantics=sem, vmem_limit_bytes=VMEM_LIMIT)


def _pick(n, cands):
    for c in cands:
        if n % c == 0:
            return c
    return n


_NN = (((1,), (0,)), ((), ()))
_NT = (((1,), (1,)), ((), ()))
_TN = (((0,), (0,)), ((), ()))


def _dg(a, b, dims):
    return lax.dot_general(a.astype(BF16), b.astype(BF16), dims, preferred_element_type=F32)


@jax.custom_vjp
def bdot(a, b):
    return _dg(a, b, _NN)


def _bdot_fwd(a, b):
    return _dg(a, b, _NN), (a, b)


def _bdot_bwd(res, ct):
    a, b = res
    return _dg(ct, b, _NT), _dg(a, ct, _TN)


bdot.defvjp(_bdot_fwd, _bdot_bwd)


@jax.custom_vjp
def bdot_nt(a, b):
    return _dg(a, b, _NT)


def _bdot_nt_fwd(a, b):
    return _dg(a, b, _NT), (a, b)


def _bdot_nt_bwd(res, ct):
    a, b = res
    return _dg(ct, b, _NN), _dg(ct, a, _TN)


bdot_nt.defvjp(_bdot_nt_fwd, _bdot_nt_bwd)


@jax.custom_vjp
def bdot_tn(a, b):
    return _dg(a, b, _TN)


def _bdot_tn_fwd(a, b):
    return _dg(a, b, _TN), (a, b)


def _bdot_tn_bwd(res, ct):
    a, b = res
    return _dg(b, ct, _NT), _dg(a, ct, _NN)


bdot_tn.defvjp(_bdot_tn_fwd, _bdot_tn_bwd)


def _split3(x):
    x1 = x.astype(BF16)
    r1 = x - x1.astype(F32)
    x2 = r1.astype(BF16)
    x3 = (r1 - x2.astype(F32)).astype(BF16)
    return x1, x2, x3


def _mask_dot(mask_bf16, x, dims):
    x1, x2, x3 = _split3(x)
    f = lambda t: lax.dot_general(mask_bf16, t, dims, preferred_element_type=F32)
    return f(x1) + f(x2) + f(x3)


@jax.custom_vjp
def mask_cumsum(mask, x):
    return _mask_dot(mask.astype(BF16), x, _NN)


def _mask_cumsum_fwd(mask, x):
    return mask_cumsum(mask, x), mask


def _mask_cumsum_bwd(mask, ct):
    return jnp.zeros_like(mask), _mask_dot(mask.astype(BF16), ct, _TN)


mask_cumsum.defvjp(_mask_cumsum_fwd, _mask_cumsum_bwd)


def _roll(x, shift, axis):
    return pltpu.roll(x, shift % x.shape[axis], axis)


@functools.partial(jax.custom_vjp, nondiff_argnums=(1, 2))
def roll(x, shift, axis):
    return _roll(x, shift, axis)


def _roll_fwd(x, shift, axis):
    return _roll(x, shift, axis), None


def _roll_bwd(shift, axis, _, ct):
    return (_roll(ct, -shift, axis),)


roll.defvjp(_roll_fwd, _roll_bwd)


def rms(x):
    return x * lax.rsqrt(jnp.mean(x * x, axis=-1, keepdims=True) + EPS)


def silu(x):
    return x * (1.0 / (1.0 + jnp.exp(-x)))


def log_sigmoid(x):
    return jnp.minimum(x, 0.0) - jnp.log(1.0 + jnp.exp(-jnp.abs(x)))


def rope(t, cos, sin):
    return t * cos + roll(t, HEAD_DIM // 2, 1) * sin


def _heads(x, n, width=HEAD_DIM):
    return [x[:, h * width:(h + 1) * width] for h in range(n)]


class Row(NamedTuple):
    arr: jax.Array
    width: int
    idx: int = 0
    diff: bool = True


class Par(NamedTuple):
    arr: jax.Array
    grouped: bool = False
    diff: bool = True


def _row_specs(rows, pars, tm, n_lat_tiles):
    def grp(i):
        return jnp.minimum(i // n_lat_tiles, 1)

    specs = [pl.BlockSpec((tm, r.width), functools.partial(lambda i, k: (i, k), k=r.idx)) for r in rows]
    for p in pars:
        blk = (1,) + p.arr.shape[1:]
        if p.grouped:
            specs.append(pl.BlockSpec(blk, lambda i: (grp(i), 0, 0)))
        else:
            specs.append(pl.BlockSpec(blk, lambda i: (0, 0, 0)))
    return specs


def row_map(name, fn, rows, pars, outs, n_rows, n_lat):
    tm = ROW_TILE
    nr, npar = len(rows), len(pars)

    def body(*refs):
        vals = [r[...] for r in refs[:nr]] + [p[0] for p in refs[nr:nr + npar]]
        res = fn(*vals)
        for o, v in zip(refs[nr + npar:], res):
            o[...] = v.astype(o.dtype)

    return pl.pallas_call(
        body, name=name, grid=(n_rows // tm,),
        in_specs=_row_specs(rows, pars, tm, n_lat // tm),
        out_specs=[pl.BlockSpec((tm, w), lambda i: (i, 0)) for w, _ in outs],
        out_shape=[jax.ShapeDtypeStruct((n_rows, w), dt) for w, dt in outs],
        compiler_params=_params(("arbitrary",)),
    )(*[r.arr for r in rows], *[p.arr for p in pars])


def row_vjp(name, fn, rows, pars, cts, n_rows, n_lat, add_to_first=None, row_grad_dtype=F32, after=None):
    tm = ROW_TILE
    nr, npar, nc = len(rows), len(pars), len(cts)
    n_lat_tiles = n_lat // tm
    args = list(rows) + list(pars)
    diff_pos = [k for k, a in enumerate(args) if a.diff]
    n_add = 0 if add_to_first is None else 1
    n_after = 0 if after is None else 1

    def body(*refs):
        i = pl.program_id(0)
        vals = [r[...] for r in refs[:nr]] + [p[0] for p in refs[nr:nr + npar]]
        ct_vals = tuple(c[...] for c in refs[nr + npar:nr + npar + nc])
        out_refs = refs[nr + npar + nc + n_add + n_after:]

        def g(*dv):
            full = list(vals)
            for k, v in zip(diff_pos, dv):
                full[k] = v
            return tuple(fn(*full))

        _, vjp = jax.vjp(g, *[vals[k] for k in diff_pos])
        grads = vjp(ct_vals)
        for n, (k, o, gr) in enumerate(zip(diff_pos, out_refs, grads)):
            if k < nr:
                o[...] = (gr + refs[nr + npar + nc][...] if (n == 0 and n_add) else gr).astype(o.dtype)
            else:
                first = (i == 0) | (i == n_lat_tiles) if args[k].grouped else (i == 0)

                @pl.when(first)
                def _():
                    o[0] = gr

                @pl.when(jnp.logical_not(first))
                def _():
                    o[0] += gr

    def grp(i):
        return jnp.minimum(i // n_lat_tiles, 1)

    out_specs, out_shape = [], []
    for k in diff_pos:
        a = args[k]
        if k < nr:
            out_specs.append(pl.BlockSpec((tm, a.width), lambda i: (i, 0)))
            out_shape.append(jax.ShapeDtypeStruct((n_rows, a.width), row_grad_dtype))
        else:
            blk = (1,) + a.arr.shape[1:]
            out_specs.append(pl.BlockSpec(blk, (lambda i: (grp(i), 0, 0)) if a.grouped else (lambda i: (0, 0, 0))))
            out_shape.append(jax.ShapeDtypeStruct(a.arr.shape, F32))
    extra = list(cts) + ([add_to_first] if n_add else [])
    ct_specs = [pl.BlockSpec((tm, c.shape[1]), lambda i: (i, 0)) for c in extra]
    if n_after:
        extra.append(after)
        ct_specs.append(pl.BlockSpec(memory_space=pl.ANY))
    return pl.pallas_call(
        body, name=name, grid=(n_rows // tm,),
        in_specs=_row_specs(rows, pars, tm, n_lat_tiles) + ct_specs,
        out_specs=out_specs, out_shape=out_shape,
        compiler_params=_params(("arbitrary",)),
    )(*[r.arr for r in rows], *[p.arr for p in pars], *extra)


class BView(NamedTuple):
    n: int
    k: int
    tn: int
    tk: int
    index_map: object
    lead: int = 1


MATMUL_VMEM_BUDGET = 40 * 1024 * 1024


def _matmul_tiles(m, n, k, a_bytes, b_bytes, o_bytes):
    tms = [c for c in (1152, 1024, 768, 512, 256, 128) if m % c == 0] or [m]
    tns = [c for c in (2048, 1408, 1280, 1024, 768, 512, 256, 128) if n % c == 0] or [n]
    tks = [k] + [c for c in (2816, 2304, 2048, 1408, 1024, 512, 256, 128) if k % c == 0 and c < k]
    for tk in tks:
        fits = [(tm * tn, tm, tn) for tm in tms for tn in tns
                if 2 * (tm * tk * a_bytes + tk * tn * b_bytes + tm * tn * o_bytes) + 2 * tm * tn * 4 <= MATMUL_VMEM_BUDGET]
        if fits and (max(fits)[0] >= 512 * 512 or tms == [m] or tk == tks[-1]):
            _, tm, tn = max(fits)
            return tm, tn, tk
    raise ValueError(f"no matmul tiling for {(m, n, k)}")


def matmul(name, a, b, *, ta=False, tb=False, add=None, out_dtype=F32, view=None):
    m = a.shape[1] if ta else a.shape[0]
    o_bytes = jnp.dtype(out_dtype).itemsize * (1 if add is None else 2)
    if view is None:
        k = a.shape[0] if ta else a.shape[1]
        n = b.shape[0] if tb else b.shape[1]
        assert (b.shape[1] if tb else b.shape[0]) == k, (a.shape, b.shape, ta, tb)
        tm, tn, tk = _matmul_tiles(m, n, k, a.dtype.itemsize, b.dtype.itemsize, o_bytes)
    else:
        n, k, tn, tk = view.n, view.k, view.tn, view.tk
        tm, _, _ = _matmul_tiles(m, tn, tk, a.dtype.itemsize, b.dtype.itemsize, o_bytes)
    nk = k // tk
    dims = (((0 if ta else 1,), (1 if tb else 0,)), ((), ()))

    def body(a_ref, b_ref, *rest):
        prod = lax.dot_general(a_ref[...].astype(BF16), b_ref[...].astype(BF16), dims, preferred_element_type=F32)
        if nk == 1:
            o_ref = rest[-1]
            o_ref[...] = (prod if add is None else prod + rest[0][...]).astype(o_ref.dtype)
            return
        o_ref, acc = rest[-2:]
        kk = pl.program_id(2)

        @pl.when(kk == 0)
        def _():
            acc[...] = prod

        @pl.when(kk != 0)
        def _():
            acc[...] += prod

        @pl.when(kk == nk - 1)
        def _():
            r = acc[...]
            if add is not None:
                r = r + rest[0][...]
            o_ref[...] = r.astype(o_ref.dtype)

    if ta:
        a_spec = pl.BlockSpec((tk, tm), lambda i, j, kk: (kk, i))
    else:
        a_spec = pl.BlockSpec((tm, tk), lambda i, j, kk: (i, kk))
    b_tile = (tn, tk) if tb else (tk, tn)
    if view is not None:
        b_spec = pl.BlockSpec((None,) * view.lead + b_tile, view.index_map)
    elif tb:
        b_spec = pl.BlockSpec(b_tile, lambda i, j, kk: (j, kk))
    else:
        b_spec = pl.BlockSpec(b_tile, lambda i, j, kk: (kk, j))
    o_spec = pl.BlockSpec((tm, tn), lambda i, j, kk: (i, j))
    ins = [a, b] + ([add] if add is not None else [])
    return pl.pallas_call(
        body, name=name, grid=(m // tm, n // tn, nk),
        in_specs=[a_spec, b_spec] + ([o_spec] if add is not None else []),
        out_specs=o_spec, out_shape=jax.ShapeDtypeStruct((m, n), out_dtype),
        scratch_shapes=[pltpu.VMEM((tm, tn), F32)] if nk > 1 else [],
        compiler_params=_params(("parallel", "parallel", "arbitrary")),
    )(*ins)


def normmod_tile(x, g, shift, scale):
    return (rms(x) * g * (1.0 + scale) + shift,)


def resid_tile(x, y, gate):
    return (x + gate * y,)


def prep_tile(z_qk, z_rq, z_rk, z_gq, zg, cos, sin, qg, kg, gate_up, gate_b):
    out = []
    for h, t in enumerate(_heads(z_qk, ATT_Q_HEADS + ATT_KV_HEADS)):
        out.append(rope(rms(t) * (qg if h < ATT_Q_HEADS else kg), cos, sin))
    gq = z_gq * (GLA_DK ** -0.5)
    rq = [rope(t, cos, sin) for t in _heads(z_rq, RET_HEADS)]
    rk = [rope(t * (HEAD_DIM ** -0.5), cos, sin) for t in _heads(z_rk, RET_HEADS)]
    la = [log_sigmoid(bdot(zg, gate_up[d * LANES:(d + 1) * LANES]) + gate_b[d:d + 1]) * (1.0 / GLA_TAU) for d in range(2)]
    return (jnp.concatenate(out + [gq] + rq + rk + la, axis=1),)


def post_tile(o_att, o_ret_f, o_ret_b, o_gla_f, o_gla_b, rg, gr, ret_g, gla_g):
    ret = jnp.concatenate([rms(t) * ret_g for t in _heads(o_ret_f + o_ret_b, RET_HEADS)], axis=1) * silu(rg)
    gla = jnp.concatenate([rms(t) * gla_g for t in _heads(o_gla_f + o_gla_b, GLA_HEADS)], axis=1) * silu(gr)
    return (jnp.concatenate([o_att, ret, gla], axis=1),)


def _convglu_tile(n_lat, a, v, cw, cb):
    t = a.shape[0]
    row = lax.broadcasted_iota(jnp.int32, (t, 1), 0)
    has_prev = ((row != 0) & (row != n_lat)).astype(F32)
    has_next = ((row != n_lat - 1) & (row != t - 1)).astype(F32)
    conv = roll(a, 1, 0) * has_prev * cw[0:1] + a * cw[1:2] + roll(a, -1, 0) * has_next * cw[2:3] + cb
    return silu(conv) * v


def convglu(name, u, cw, cb, n_lat):
    t, f2 = u.shape
    f, tc = f2 // 2, FFN_COL_TILE
    nb = f // tc

    def body(a_ref, v_ref, cw_ref, cb_ref, o_ref):
        o_ref[...] = _convglu_tile(n_lat, a_ref[...], v_ref[...], cw_ref[...], cb_ref[...]).astype(o_ref.dtype)

    return pl.pallas_call(
        body, name=name, grid=(nb,),
        in_specs=[pl.BlockSpec((t, tc), lambda j: (0, j)), pl.BlockSpec((t, tc), lambda j: (0, nb + j)),
                  pl.BlockSpec((3, tc), lambda j: (0, j)), pl.BlockSpec((1, tc), lambda j: (0, j))],
        out_specs=pl.BlockSpec((t, tc), lambda j: (0, j)),
        out_shape=jax.ShapeDtypeStruct((t, f), BF16),
        compiler_params=_params(("parallel",)),
    )(u, u, cw, cb)


def convglu_bwd(name, u, cw, cb, dg, n_lat):
    t, f2 = u.shape
    f, tc = f2 // 2, FFN_COL_TILE
    nb = f // tc

    def body(a_ref, v_ref, cw_ref, cb_ref, dg_ref, da_ref, dv_ref, dcw_ref, dcb_ref):
        _, vjp = jax.vjp(functools.partial(_convglu_tile, n_lat), a_ref[...], v_ref[...], cw_ref[...], cb_ref[...])
        da, dv, dcw_ref[...], dcb_ref[...] = vjp(dg_ref[...])
        da_ref[...], dv_ref[...] = da.astype(BF16), dv.astype(BF16)

    col = pl.BlockSpec((t, tc), lambda j: (0, j))
    return pl.pallas_call(
        body, name=name, grid=(nb,),
        in_specs=[col, pl.BlockSpec((t, tc), lambda j: (0, nb + j)), pl.BlockSpec((3, tc), lambda j: (0, j)),
                  pl.BlockSpec((1, tc), lambda j: (0, j)), col],
        out_specs=[col, col, pl.BlockSpec((3, tc), lambda j: (0, j)), pl.BlockSpec((1, tc), lambda j: (0, j))],
        out_shape=[jax.ShapeDtypeStruct((t, f), BF16), jax.ShapeDtypeStruct((t, f), BF16),
                   jax.ShapeDtypeStruct((3, f), F32), jax.ShapeDtypeStruct((1, f), F32)],
        compiler_params=_params(("parallel",)),
    )(u, u, cw, cb, dg)


def final_loss(x, target, g, n_lat):
    tm = ROW_TILE
    d = x.shape[1]

    def body(x_ref, t_ref, g_ref, loss_ref, dx_ref, dg_ref):
        i = pl.program_id(0)
        tgt = t_ref[...]

        def f(xv, gv):
            e = rms(xv) * gv - tgt
            s = jnp.sum(jnp.sum(e * e, axis=1, keepdims=True), axis=0, keepdims=True)
            return s * (0.5 / d)

        val, vjp = jax.vjp(f, x_ref[...], g_ref[...])
        dx, dgv = vjp(jnp.ones((1, 1), F32))
        dx_ref[...] = dx

        @pl.when(i == 0)
        def _():
            dg_ref[...] = dgv
            loss_ref[...] = jnp.broadcast_to(val, loss_ref.shape)

        @pl.when(i != 0)
        def _():
            dg_ref[...] += dgv
            loss_ref[...] += jnp.broadcast_to(val, loss_ref.shape)

    return pl.pallas_call(
        body, name="final_loss", grid=(n_lat // tm,),
        in_specs=[pl.BlockSpec((tm, d), lambda i: (i, 0)), pl.BlockSpec((tm, d), lambda i: (i, 0)),
                  pl.BlockSpec((1, d), lambda i: (0, 0))],
        out_specs=[pl.BlockSpec((1, LANES), lambda i: (0, 0)), pl.BlockSpec((tm, d), lambda i: (i, 0)),
                   pl.BlockSpec((1, d), lambda i: (0, 0))],
        out_shape=[jax.ShapeDtypeStruct((1, LANES), F32), jax.ShapeDtypeStruct((n_lat, d), F32),
                   jax.ShapeDtypeStruct((1, d), F32)],
        compiler_params=_params(("arbitrary",)),
    )(x, target, g)


ATT_SCALE = HEAD_DIM ** -0.5
_AK_BLK = P_AK // HEAD_DIM
_AV_BLK = Z_AV // HEAD_DIM


def _att_specs(t, tq):
    gw = ATT_GROUP * HEAD_DIM
    q_spec = pl.BlockSpec((tq, gw), lambda kv, i: (i, kv))
    k_spec = pl.BlockSpec((t, HEAD_DIM), lambda kv, i: (0, _AK_BLK + kv))
    v_spec = pl.BlockSpec((t, HEAD_DIM), lambda kv, i: (0, _AV_BLK + kv))
    row_spec = pl.BlockSpec((ATT_GROUP, tq, 1), lambda kv, i: (kv, i, 0))
    return q_spec, k_spec, v_spec, row_spec


def _att_mask(i, t, tq, n_lat):
    col = lax.broadcasted_iota(jnp.int32, (1, t), 1)
    return jnp.where((i >= n_lat // tq) & (col < n_lat), -jnp.inf, 0.0).astype(F32)


def attn_fwd(p, z, n_lat):
    t = p.shape[0]
    tq = ROW_TILE

    def body(q_ref, k_ref, v_ref, o_ref, lse_ref):
        mask = _att_mask(pl.program_id(1), t, tq, n_lat)
        k, v = k_ref[...].astype(BF16), v_ref[...].astype(BF16)
        for g in range(ATT_GROUP):
            cols = slice(g * HEAD_DIM, (g + 1) * HEAD_DIM)
            s = _dg(q_ref[:, cols], k, _NT) * ATT_SCALE + mask
            m = jnp.max(s, axis=1, keepdims=True)
            pr = jnp.exp(s - m)
            l = jnp.sum(pr, axis=1, keepdims=True)
            o_ref[:, cols] = _dg(pr, v, _NN) / l
            lse_ref[g] = m + jnp.log(l)

    q_spec, k_spec, v_spec, row_spec = _att_specs(t, tq)
    return pl.pallas_call(
        body, name="attn_fwd", grid=(ATT_KV_HEADS, t // tq),
        in_specs=[q_spec, k_spec, v_spec], out_specs=[q_spec, row_spec],
        out_shape=[jax.ShapeDtypeStruct((t, ATT_Q_HEADS * HEAD_DIM), F32),
                   jax.ShapeDtypeStruct((ATT_Q_HEADS, t, 1), F32)],
        compiler_params=_params(("parallel", "parallel")),
    )(p, p, z)


def attn_bwd(p, z, o, lse, do, n_lat):
    t = p.shape[0]
    tq = ROW_TILE

    def body(q_ref, k_ref, v_ref, o_ref, do_ref, lse_ref, dq_ref, dk_ref, dv_ref):
        i = pl.program_id(1)

        @pl.when(i == 0)
        def _():
            dk_ref[...] = jnp.zeros_like(dk_ref)
            dv_ref[...] = jnp.zeros_like(dv_ref)

        mask = _att_mask(i, t, tq, n_lat)
        k, v = k_ref[...].astype(BF16), v_ref[...].astype(BF16)
        dk, dv = dk_ref[...], dv_ref[...]
        for g in range(ATT_GROUP):
            cols = slice(g * HEAD_DIM, (g + 1) * HEAD_DIM)
            q, do_g = q_ref[:, cols].astype(BF16), do_ref[:, cols]
            pr = jnp.exp(_dg(q, k, _NT) * ATT_SCALE + mask - lse_ref[g])
            delta = jnp.sum(o_ref[:, cols] * do_g, axis=1, keepdims=True)
            ds = pr * (_dg(do_g, v, _NT) - delta) * ATT_SCALE
            dq_ref[:, cols] = _dg(ds, k, _NN)
            dk = dk + _dg(ds, q, _TN)
            dv = dv + _dg(pr, do_g, _TN)
        dk_ref[...], dv_ref[...] = dk, dv

    q_spec, k_spec, v_spec, row_spec = _att_specs(t, tq)
    kv_out = pl.BlockSpec((t, HEAD_DIM), lambda kv, i: (0, kv))
    return pl.pallas_call(
        body, name="attn_bwd", grid=(ATT_KV_HEADS, t // tq),
        in_specs=[q_spec, k_spec, v_spec, q_spec, q_spec, row_spec],
        out_specs=[q_spec, kv_out, kv_out],
        out_shape=[jax.ShapeDtypeStruct((t, ATT_Q_HEADS * HEAD_DIM), F32),
                   jax.ShapeDtypeStruct((t, ATT_KV_HEADS * HEAD_DIM), F32),
                   jax.ShapeDtypeStruct((t, ATT_KV_HEADS * HEAD_DIM), F32)],
        compiler_params=_params(("parallel", "arbitrary")),
    )(p, p, z, o, do, lse)


_RQ_BLK = P_RQ // HEAD_DIM
_RK_BLK = P_RK // HEAD_DIM
_RV_BLK = Z_RV // HEAD_DIM


def _scan_chunk(direction, step, n_chunks, n_lat_chunks):
    return jnp.where(direction == 0, (step + n_lat_chunks) % n_chunks, n_chunks - 1 - step)


def _ret_geometry(direction):
    c = RET_CHUNK
    i = lax.broadcasted_iota(jnp.int32, (c, c), 0)
    j = lax.broadcasted_iota(jnp.int32, (c, c), 1)
    rel = jnp.where(direction == 0, i - j, j - i).astype(F32)
    r = lax.broadcasted_iota(jnp.int32, (c, 1), 0)
    pos = jnp.where(direction == 0, r, c - 1 - r).astype(F32)
    return rel, pos


def ret_chunk(q, k, v, s, lg, rel, pos):
    c = RET_CHUNK
    causal = rel >= 0
    d_in = jnp.where(causal, jnp.exp(lg * jnp.where(causal, rel, 0.0)), 0.0)
    q_dec = jnp.exp(lg * (pos + 1.0))
    k_dec = jnp.exp(lg * (c - 1.0 - pos))
    c_dec = jnp.exp(lg * c)
    att = bdot_nt(q, k) * d_in
    o = bdot(att, v) + bdot(q * q_dec, s)
    s_new = c_dec * s + bdot_tn(k * k_dec, v)
    return o, s_new


def ret_fwd(p, z, lg, n_lat):
    t = p.shape[0]
    c = RET_CHUNK
    nc, nlc = t // c, n_lat // c

    def body(q_ref, k_ref, v_ref, lg_ref, o_ref, ssave_ref, s_s):
        d, n = pl.program_id(0), pl.program_id(1)

        @pl.when(n == 0)
        def _():
            s_s[...] = jnp.zeros_like(s_s)

        rel, pos = _ret_geometry(d)
        for h in range(RET_HEADS):
            cols = slice(h * HEAD_DIM, (h + 1) * HEAD_DIM)
            ssave_ref[0, h, 0] = s_s[h]
            o, s_new = ret_chunk(q_ref[:, cols], k_ref[:, cols], v_ref[:, cols], s_s[h], lg_ref[0, h], rel, pos)
            o_ref[:, cols] = o
            s_s[h] = s_new

    w = RET_HEADS * HEAD_DIM

    def blk(base):
        return pl.BlockSpec((c, w), lambda d, n: (_scan_chunk(d, n, nc, nlc), base // RET_HEADS))

    return pl.pallas_call(
        body, name="ret_fwd", grid=(2, nc),
        in_specs=[blk(_RQ_BLK), blk(_RK_BLK), blk(_RV_BLK), pl.BlockSpec((1, RET_HEADS, 1, 1), lambda d, n: (d, 0, 0, 0))],
        out_specs=[pl.BlockSpec((c, w), lambda d, n: (_scan_chunk(d, n, nc, nlc), d)),
                   pl.BlockSpec((1, RET_HEADS, 1, HEAD_DIM, HEAD_DIM), lambda d, n: (d, 0, n, 0, 0))],
        out_shape=[jax.ShapeDtypeStruct((t, 2 * w), F32),
                   jax.ShapeDtypeStruct((2, RET_HEADS, nc, HEAD_DIM, HEAD_DIM), F32)],
        scratch_shapes=[pltpu.VMEM((RET_HEADS, HEAD_DIM, HEAD_DIM), F32)],
        compiler_params=_params(("parallel", "arbitrary")),
    )(p, p, z, lg)


def ret_bwd(p, z, lg, states, do, n_lat):
    t = p.shape[0]
    c = RET_CHUNK
    nc, nlc = t // c, n_lat // c

    def body(q_ref, k_ref, v_ref, lg_ref, s_ref, do_ref, dq_ref, dk_ref, dv_ref, dlg_ref, ds_s):
        d, n = pl.program_id(0), pl.program_id(1)

        @pl.when(n == 0)
        def _():
            ds_s[...] = jnp.zeros_like(ds_s)
            dlg_ref[...] = jnp.zeros_like(dlg_ref)

        rel, pos = _ret_geometry(d)
        f = functools.partial(ret_chunk, rel=rel, pos=pos)
        for h in range(RET_HEADS):
            cols = slice(h * HEAD_DIM, (h + 1) * HEAD_DIM)
            _, vjp = jax.vjp(f, q_ref[:, cols], k_ref[:, cols], v_ref[:, cols], s_ref[0, h, 0], lg_ref[0, h])
            dq, dk, dv, ds, dlg = vjp((do_ref[:, cols], ds_s[h]))
            dq_ref[:, cols], dk_ref[:, cols], dv_ref[:, cols] = dq, dk, dv
            ds_s[h] = ds
            dlg_ref[0, h] += dlg

    def chunk_of(d, n):
        return _scan_chunk(d, nc - 1 - n, nc, nlc)

    w = RET_HEADS * HEAD_DIM

    def blk(base):
        return pl.BlockSpec((c, w), lambda d, n: (chunk_of(d, n), base // RET_HEADS))

    out_blk = pl.BlockSpec((c, w), lambda d, n: (chunk_of(d, n), d))
    lg_blk = pl.BlockSpec((1, RET_HEADS, 1, 1), lambda d, n: (d, 0, 0, 0))
    grad_shape = jax.ShapeDtypeStruct((t, 2 * w), F32)
    return pl.pallas_call(
        body, name="ret_bwd", grid=(2, nc),
        in_specs=[blk(_RQ_BLK), blk(_RK_BLK), blk(_RV_BLK), lg_blk,
                  pl.BlockSpec((1, RET_HEADS, 1, HEAD_DIM, HEAD_DIM), lambda d, n: (d, 0, nc - 1 - n, 0, 0)),
                  pl.BlockSpec((c, w), lambda d, n: (chunk_of(d, n), 0))],
        out_specs=[out_blk, out_blk, out_blk, lg_blk],
        out_shape=[grad_shape, grad_shape, grad_shape, jax.ShapeDtypeStruct((2, RET_HEADS, 1, 1), F32)],
        scratch_shapes=[pltpu.VMEM((RET_HEADS, HEAD_DIM, HEAD_DIM), F32)],
        compiler_params=_params(("parallel", "arbitrary")),
    )(p, p, z, lg, states, do)


_GQ_BLK = P_GQ // (GLA_HEADS * GLA_DK)
_GK_BLK = Z_GK // (GLA_HEADS * GLA_DK)
_GV_BLK = Z_GV // (GLA_HEADS * GLA_DV)
_LA_BLK = P_LA // (GLA_HEADS * GLA_DK)


def _gla_mask(direction):
    c = GLA_CHUNK
    i = lax.broadcasted_iota(jnp.int32, (c, c), 0)
    j = lax.broadcasted_iota(jnp.int32, (c, c), 1)
    return (jnp.where(direction == 0, i - j, j - i) >= 0).astype(F32)


def gla_chunk(q, k, v, la, st, mask):
    b = mask_cumsum(mask, la)
    btot = jnp.sum(la, axis=0, keepdims=True)
    half = 0.5 * btot
    qt, kt = q * jnp.exp(b - half), k * jnp.exp(half - b)
    qs, ke = q * jnp.exp(b), k * jnp.exp(btot - b)
    outs, upd = [], []
    for h in range(GLA_HEADS):
        ks = slice(h * GLA_DK, (h + 1) * GLA_DK)
        vh = v[:, h * GLA_DV:(h + 1) * GLA_DV]
        att = bdot_nt(qt[:, ks], kt[:, ks]) * mask
        outs.append(bdot(att, vh) + bdot_nt(qs[:, ks], st[:, ks]))
        upd.append(bdot_tn(vh, ke[:, ks]))
    st_new = st * jnp.exp(btot) + jnp.concatenate(upd, axis=1)
    return jnp.concatenate(outs, axis=1), st_new


def gla_fwd(p, z, n_lat):
    t = p.shape[0]
    c = GLA_CHUNK
    nc, nlc = t // c, n_lat // c
    kw, vw = GLA_HEADS * GLA_DK, GLA_HEADS * GLA_DV

    def body(q_ref, k_ref, v_ref, la_ref, o_ref, ssave_ref, s_s):
        d, n = pl.program_id(0), pl.program_id(1)

        @pl.when(n == 0)
        def _():
            s_s[...] = jnp.zeros_like(s_s)

        ssave_ref[0, 0] = s_s[...]
        o, s_new = gla_chunk(q_ref[...], k_ref[...], v_ref[...], la_ref[...], s_s[...], _gla_mask(d))
        o_ref[...] = o
        s_s[...] = s_new

    def chunk_of(d, n):
        return _scan_chunk(d, n, nc, nlc)

    return pl.pallas_call(
        body, name="gla_fwd", grid=(2, nc),
        in_specs=[pl.BlockSpec((c, kw), lambda d, n: (chunk_of(d, n), _GQ_BLK)),
                  pl.BlockSpec((c, kw), lambda d, n: (chunk_of(d, n), _GK_BLK)),
                  pl.BlockSpec((c, vw), lambda d, n: (chunk_of(d, n), _GV_BLK)),
                  pl.BlockSpec((c, kw), lambda d, n: (chunk_of(d, n), _LA_BLK + d))],
        out_specs=[pl.BlockSpec((c, vw), lambda d, n: (chunk_of(d, n), d)),
                   pl.BlockSpec((1, 1, GLA_DV, kw), lambda d, n: (d, n, 0, 0))],
        out_shape=[jax.ShapeDtypeStruct((t, 2 * vw), F32), jax.ShapeDtypeStruct((2, nc, GLA_DV, kw), F32)],
        scratch_shapes=[pltpu.VMEM((GLA_DV, kw), F32)],
        compiler_params=_params(("parallel", "arbitrary")),
    )(p, z, z, p)


def gla_bwd(p, z, states, do, n_lat):
    t = p.shape[0]
    c = GLA_CHUNK
    nc, nlc = t // c, n_lat // c
    kw, vw = GLA_HEADS * GLA_DK, GLA_HEADS * GLA_DV

    def body(q_ref, k_ref, v_ref, la_ref, s_ref, do_ref, dq_ref, dk_ref, dv_ref, dla_ref, ds_s):
        d, n = pl.program_id(0), pl.program_id(1)

        @pl.when(n == 0)
        def _():
            ds_s[...] = jnp.zeros_like(ds_s)

        f = functools.partial(gla_chunk, mask=_gla_mask(d))
        _, vjp = jax.vjp(f, q_ref[...], k_ref[...], v_ref[...], la_ref[...], s_ref[0, 0])
        dq_ref[...], dk_ref[...], dv_ref[...], dla_ref[...], ds_s[...] = vjp((do_ref[...], ds_s[...]))

    def chunk_of(d, n):
        return _scan_chunk(d, nc - 1 - n, nc, nlc)

    k_out = pl.BlockSpec((c, kw), lambda d, n: (chunk_of(d, n), d))
    return pl.pallas_call(
        body, name="gla_bwd", grid=(2, nc),
        in_specs=[pl.BlockSpec((c, kw), lambda d, n: (chunk_of(d, n), _GQ_BLK)),
                  pl.BlockSpec((c, kw), lambda d, n: (chunk_of(d, n), _GK_BLK)),
                  pl.BlockSpec((c, vw), lambda d, n: (chunk_of(d, n), _GV_BLK)),
                  pl.BlockSpec((c, kw), lambda d, n: (chunk_of(d, n), _LA_BLK + d)),
                  pl.BlockSpec((1, 1, GLA_DV, kw), lambda d, n: (d, nc - 1 - n, 0, 0)),
                  pl.BlockSpec((c, vw), lambda d, n: (chunk_of(d, n), 0))],
        out_specs=[k_out, k_out, pl.BlockSpec((c, vw), lambda d, n: (chunk_of(d, n), d)), k_out],
        out_shape=[jax.ShapeDtypeStruct((t, 2 * kw), F32), jax.ShapeDtypeStruct((t, 2 * kw), F32),
                   jax.ShapeDtypeStruct((t, 2 * vw), F32), jax.ShapeDtypeStruct((t, 2 * kw), F32)],
        scratch_shapes=[pltpu.VMEM((GLA_DV, kw), F32)],
        compiler_params=_params(("parallel", "arbitrary")),
    )(p, z, z, p, states, do)


def _adam_tile(w, g, m, v):
    m = ADAM_B1 * m + (1.0 - ADAM_B1) * g
    v = ADAM_B2 * v + (1.0 - ADAM_B2) * (g * g)
    m_hat = m / (1.0 - ADAM_B1 ** ADAM_STEP)
    v_hat = v / (1.0 - ADAM_B2 ** ADAM_STEP)
    delta = -ADAM_LR * (m_hat / (jnp.sqrt(v_hat) + ADAM_EPS) + ADAM_WD * w)
    return delta, m, v


def adamw(name, w, g, m, v):
    shape = w.shape
    cols = shape[-1] if w.ndim > 1 and shape[-1] >= LANES else int(np.prod(shape))
    rows = int(np.prod(shape)) // cols
    tr = rows
    for cand in (512, 256, 128, 64, 32, 16, 8):
        if rows % cand == 0 and cand * cols * 4 <= (1 << 20):
            tr = cand
            break
    flat = [a.reshape(rows, cols) for a in (w, g, m, v)]

    def body(w_ref, g_ref, m_ref, v_ref, d_ref, mo_ref, vo_ref):
        d_ref[...], mo_ref[...], vo_ref[...] = _adam_tile(w_ref[...], g_ref[...], m_ref[...], v_ref[...])

    spec = pl.BlockSpec((tr, cols), lambda i: (i, 0))
    outs = pl.pallas_call(
        body, name=name, grid=(rows // tr,),
        in_specs=[spec] * 4, out_specs=[spec] * 3,
        out_shape=[jax.ShapeDtypeStruct((rows, cols), F32)] * 3,
        compiler_params=_params(("parallel",)),
    )(*flat)
    return tuple(o.reshape(shape) for o in outs)


MESH = pl.DeviceIdType.MESH
_HBM = pl.BlockSpec(memory_space=pltpu.HBM)
N_CHIPS = 4
N_DEV = 8


def _place():
    x, y, c = lax.axis_index("x"), lax.axis_index("y"), lax.axis_index("c")
    chips = [(1 - x, y), (x, 1 - y), (1 - x, 1 - y)]
    return x, y, c, chips


def _remote(src, dst, send_sem, recv_sem, to):
    return pltpu.make_async_remote_copy(src_ref=src, dst_ref=dst, send_sem=send_sem, recv_sem=recv_sem,
                                        device_id=to, device_id_type=MESH)


def all_gather_small(name, v):
    m_per, n = v.shape

    def body(x_ref, out_ref, send_sems, recv_sems, local_sem):
        x, y, c, chips = _place()
        me, sibling = (x, y, c), (x, y, 1 - c)

        def rows(px, py, pc):
            return out_ref.at[pl.ds((4 * px + 2 * py + pc) * m_per, m_per), :]

        def copy(k, block, to, src=None):
            return _remote(rows(*block) if src is None else src, rows(*block), send_sems.at[k], recv_sems.at[k], to)

        mine = pltpu.make_async_copy(x_ref, rows(*me), local_sem)
        mine.start()
        first = [copy(0, me, sibling, src=x_ref)]
        first += [copy(1 + j, me, (*chip, c), src=x_ref) for j, chip in enumerate(chips)]
        for cp in first:
            cp.start()
        passed = [copy(4 + j, (*chip, c), sibling) for j, chip in enumerate(chips)]
        for j, chip in enumerate(chips):
            copy(1 + j, (*chip, c), me).wait_recv()
            passed[j].start()
        copy(0, sibling, me).wait_recv()
        for j, chip in enumerate(chips):
            copy(4 + j, (*chip, 1 - c), me).wait_recv()
        for cp in first + passed:
            cp.wait_send()
        mine.wait()

    return pl.pallas_call(
        body, name=name,
        out_shape=jax.ShapeDtypeStruct((N_DEV * m_per, n), v.dtype),
        in_specs=[pl.BlockSpec(memory_space=pltpu.VMEM)],
        out_specs=pl.BlockSpec(memory_space=pltpu.VMEM),
        scratch_shapes=[pltpu.SemaphoreType.DMA((7,)), pltpu.SemaphoreType.DMA((7,)), pltpu.SemaphoreType.DMA],
        compiler_params=pltpu.CompilerParams(vmem_limit_bytes=VMEM_LIMIT),
    )(v)


_SEM = pl.BlockSpec(memory_space=pltpu.SEMAPHORE)
_SPLIT_COPY = pltpu.CompilerParams(has_side_effects=pltpu.SideEffectType.DATAFLOW_SIDE_EFFECTING)


class CopyPlan(NamedTuple):
    copies: object
    n: int
    in_place: bool = False


def _gather_copies(x_ref, land_ref, x, y, c, chips):
    half = x_ref.shape[0] // 2
    rows = pl.ds(c * half, half)
    return [(x_ref.at[rows, :], land_ref.at[2 * x + y, rows, :], (*chip, c), land_ref.at[2 * chip[0] + chip[1], rows, :])
            for chip in chips]


def _pass_copies(land_ref, _, x, y, c, chips):
    half = land_ref.shape[1] // 2
    mine, other = pl.ds(c * half, half), pl.ds((1 - c) * half, half)
    return [(land_ref.at[2 * chip[0] + chip[1], mine, :], land_ref.at[2 * chip[0] + chip[1], mine, :], (x, y, 1 - c),
             land_ref.at[2 * chip[0] + chip[1], other, :]) for chip in chips]


def _sibling_half_copies(p_ref, land_ref, x, y, c, chips):
    half = p_ref.shape[1] // 2
    return [(p_ref.at[:, pl.ds((1 - c) * half, half), :], land_ref, (x, y, 1 - c), land_ref)]


def _scatter_copies(s_ref, land_ref, x, y, c, chips):
    return [(s_ref.at[2 * chip[0] + chip[1]], land_ref.at[j], (*chip, c), land_ref.at[j]) for j, chip in enumerate(chips)]


def _join_copies(buf_ref, _, x, y, c, chips):
    half = buf_ref.shape[0] // 2
    mine = buf_ref.at[pl.ds(c * half, half), :]
    return [(mine, mine, (x, y, 1 - c), buf_ref.at[pl.ds((1 - c) * half, half), :])]


GATHER = CopyPlan(_gather_copies, 3)
PASS_ON = CopyPlan(_pass_copies, 3, in_place=True)
SIBLING_HALF = CopyPlan(_sibling_half_copies, 1)
SCATTER = CopyPlan(_scatter_copies, 3)
JOIN = CopyPlan(_join_copies, 1, in_place=True)


def split_start(name, plan, srcs, land_shapes=None, after=None):
    nt = len(srcs)
    arrays = [pltpu.with_memory_space_constraint(s, pltpu.HBM) for s in srcs]
    if not plan.in_place:
        arrays += [pltpu.with_memory_space_constraint(lax.empty(shape, s.dtype), pltpu.HBM) for shape, s in zip(land_shapes, srcs)]
    na = len(arrays)
    behind = [] if after is None else [after]
    n_in = na + len(behind)

    def body(*refs):
        x_refs = refs[:nt]
        land_refs = x_refs if plan.in_place else refs[nt:na]
        send, recv = refs[n_in:n_in + nt], refs[n_in + nt:n_in + 2 * nt]
        x, y, c, chips = _place()
        for t in range(nt):
            for j, (src, dst, to, _) in enumerate(plan.copies(x_refs[t], land_refs[t], x, y, c, chips)):
                _remote(src, dst, send[t].at[j], recv[t].at[j], to).start()
        refs[-1][...] = jnp.zeros_like(refs[-1])

    outs = pl.pallas_call(
        body, name=name,
        out_shape=tuple([pltpu.SemaphoreType.DMA((plan.n,))] * (2 * nt) + [pltpu.HBM(a.shape, a.dtype) for a in arrays]
                        + [jax.ShapeDtypeStruct((8, LANES), F32)]),
        in_specs=[_HBM] * na + [pl.BlockSpec(memory_space=pl.ANY)] * len(behind),
        out_specs=tuple([_SEM] * (2 * nt) + [_HBM] * na + [pl.BlockSpec(memory_space=pltpu.VMEM)]),
        input_output_aliases={i: 2 * nt + i for i in range(na)},
        compiler_params=_SPLIT_COPY,
    )(*arrays, *behind)
    groups = [(outs[t], outs[nt + t]) + tuple(outs[2 * nt + t + k * nt] for k in range(na // nt)) for t in range(nt)]
    return groups, outs[-1]


def split_wait(name, plan, group, after):
    send, recv, *arrays = group
    na = len(arrays)

    def body(*refs):
        x_ref, land_ref = refs[0], refs[na - 1]
        send_sem, recv_sem = refs[na], refs[na + 1]
        x, y, c, chips = _place()
        for j, (s, _, to, arrival) in enumerate(plan.copies(x_ref, land_ref, x, y, c, chips)):
            cp = _remote(s, arrival, send_sem.at[j], recv_sem.at[j], to)
            cp.wait_send()
            cp.wait_recv()

    return pl.pallas_call(
        body, name=name,
        out_shape=tuple(pltpu.HBM(a.shape, a.dtype) for a in arrays),
        in_specs=tuple([_HBM] * na + [_SEM, _SEM, pl.BlockSpec(memory_space=pl.ANY)]), out_specs=tuple([_HBM] * na),
        input_output_aliases={i: i for i in range(na)}, compiler_params=_SPLIT_COPY,
    )(*arrays, send, recv, after)


def _rows_tile(rows, cols):
    for cand in (512, 256, 128, 64, 32, 16):
        if rows % cand == 0 and cand * cols * 4 <= (1 << 21):
            return cand
    return rows


def add_sibling_half(name, pieces, from_sibling, core):
    n, h, cols = from_sibling.shape
    tr = _rows_tile(h, cols)
    nb = h // tr

    def body(c_ref, a_ref, b_ref, o_ref):
        o_ref[...] = (a_ref[...].astype(F32) + b_ref[...].astype(F32)).astype(o_ref.dtype)

    blk = pl.BlockSpec((1, tr, cols), lambda q, i, c_ref: (q, i, 0))
    return pl.pallas_call(
        body, name=name,
        grid_spec=pltpu.PrefetchScalarGridSpec(
            num_scalar_prefetch=1, grid=(n, nb),
            in_specs=[pl.BlockSpec((1, tr, cols), lambda q, i, c_ref: (q, c_ref[0] * nb + i, 0)), blk], out_specs=blk),
        out_shape=jax.ShapeDtypeStruct((n, h, cols), BF16),
        compiler_params=_params(("parallel", "parallel")),
    )(core.reshape(1).astype(jnp.int32), pieces, from_sibling)


def add_chip_sums(name, chip_sums, from_chips, chip, core):
    _, h, cols = chip_sums.shape
    tr = _rows_tile(h, cols)
    nb = h // tr

    def body(s_ref, own_ref, r0_ref, r1_ref, r2_ref, o_ref):
        acc = own_ref[0].astype(F32) + r0_ref[0].astype(F32)
        o_ref[...] = acc + r1_ref[0].astype(F32) + r2_ref[0].astype(F32)

    def got(j):
        return pl.BlockSpec((1, tr, cols), lambda i, s_ref: (j, i, 0))

    return pl.pallas_call(
        body, name=name,
        grid_spec=pltpu.PrefetchScalarGridSpec(
            num_scalar_prefetch=1, grid=(nb,),
            in_specs=[pl.BlockSpec((1, tr, cols), lambda i, s_ref: (s_ref[0], i, 0)), got(0), got(1), got(2)],
            out_specs=pl.BlockSpec((tr, cols), lambda i, s_ref: (s_ref[1] * nb + i, 0))),
        out_shape=jax.ShapeDtypeStruct((2 * h, cols), F32),
        compiler_params=_params(("parallel",)),
    )(jnp.stack([chip, core]).astype(jnp.int32), chip_sums, from_chips, from_chips, from_chips)


def sum_device_blocks(name, g):
    n = g.shape[1]

    def body(g_ref, o_ref):
        acc = g_ref[0:8, :]
        for d in range(1, N_DEV):
            acc = acc + g_ref[8 * d:8 * (d + 1), :]
        o_ref[...] = acc

    return pl.pallas_call(body, name=name, out_shape=jax.ShapeDtypeStruct((8, n), F32),
                          compiler_params=pltpu.CompilerParams(vmem_limit_bytes=VMEM_LIMIT))(g)


class LayerWeights(NamedTuple):
    norm1_g: jax.Array
    q_g: jax.Array
    k_g: jax.Array
    lg: jax.Array
    ret_g: jax.Array
    gate_up: jax.Array
    gate_b: jax.Array
    gla_g: jax.Array
    norm2_g: jax.Array
    conv_w: jax.Array
    conv_b: jax.Array


def _mod(mods, k):
    return mods[:, k:k + 1, :]


def out_view(l, tb):
    rows = D_MODEL // N_CHIPS
    if tb:
        return BView(n=D_MODEL, k=D_MODEL, tn=rows, tk=D_MODEL, index_map=lambda i, j, kk: (j, l, kk))
    return BView(n=D_MODEL, k=D_MODEL, tn=1024, tk=rows, index_map=lambda i, j, kk: (kk, l, j))


def down_view(l, f, tb):
    rows = f // N_CHIPS
    if tb:
        return BView(n=f, k=D_MODEL, tn=rows, tk=D_MODEL, index_map=lambda i, j, kk: (j, l, kk))
    return BView(n=D_MODEL, k=f, tn=1024, tk=rows, index_map=lambda i, j, kk: (kk, l, j))


def up_view(l, f, part=None):
    cols = 2 * f // N_CHIPS
    tc = _pick(cols, (1408, 1024, 512, 256))
    nbc = cols // tc
    if part is None:
        return BView(n=2 * f, k=D_MODEL, tn=tc, tk=D_MODEL, index_map=lambda i, j, kk: (j // nbc, l, j % nbc))
    nnb = D_MODEL // 1024
    return BView(n=D_MODEL, k=f, tn=1024, tk=tc, index_map=lambda i, j, kk: (2 * part + kk // nbc, l * nnb + j, kk % nbc))


def ada_view(l, n_ada, tb):
    if tb:
        return BView(n=D_MODEL, k=n_ada, tn=1024, tk=n_ada, index_map=lambda i, j, kk: (l, j, 0))
    return BView(n=n_ada, k=D_MODEL, tn=1024, tk=D_MODEL, index_map=lambda i, j, kk: (l, 0, j))


def _prep_args(z, zg, cos, sin, w):
    rows = [Row(z, Z_AV, 0), Row(z, 512, Z_RQ // 512), Row(z, 512, Z_RK // 512), Row(z, 256, Z_GQ // 256),
            Row(zg, LANES, 0), Row(cos, HEAD_DIM, 0, False), Row(sin, HEAD_DIM, 0, False)]
    return rows, [Par(w.q_g), Par(w.k_g), Par(w.gate_up), Par(w.gate_b)]


def _post_args(o_att, o_ret, o_gla, z, w):
    rows = [Row(o_att, 1024), Row(o_ret, 512, 0), Row(o_ret, 512, 1, False), Row(o_gla, 512, 0), Row(o_gla, 512, 1, False),
            Row(z, 512, Z_RG // 512), Row(z, 512, Z_GR // 512)]
    return rows, [Par(w.ret_g), Par(w.gla_g)]


def layer_fwd(l, xs, mods, w, fetch, cos, sin, n_lat):
    t, d = xs.shape
    tag = f"l{l}_"
    nm1 = [Par(w.norm1_g), Par(_mod(mods, 0), True), Par(_mod(mods, 1), True)]
    (h,) = row_map(tag + "norm1", normmod_tile, [Row(xs, d)], nm1, [(d, BF16)], t, n_lat)
    w_main, w_gate = fetch("w_in", h)
    z = matmul(tag + "in_proj", h, w_main)
    zg = matmul(tag + "gate_proj", h, w_gate)
    rows, pars = _prep_args(z, zg, cos, sin, w)
    (p,) = row_map(tag + "prep", prep_tile, rows, pars, [(P_W, F32)], t, n_lat)
    o_att, lse = attn_fwd(p, z, n_lat)
    o_ret, s_ret = ret_fwd(p, z, w.lg, n_lat)
    o_gla, s_gla = gla_fwd(p, z, n_lat)
    rows, pars = _post_args(o_att, o_ret, o_gla, z, w)
    (m,) = row_map(tag + "post", post_tile, rows, pars, [(d, BF16)], t, n_lat)
    g_out = fetch("w_out", m)
    y = matmul(tag + "out_proj", m, g_out, view=out_view(0, False))
    (x1,) = row_map(tag + "resid1", resid_tile, [Row(xs, d), Row(y, d)], [Par(_mod(mods, 2), True)], [(d, F32)], t, n_lat)
    nm2 = [Par(w.norm2_g), Par(_mod(mods, 3), True), Par(_mod(mods, 4), True)]
    (h2,) = row_map(tag + "norm2", normmod_tile, [Row(x1, d)], nm2, [(d, BF16)], t, n_lat)
    f = w.conv_b.shape[1]
    g_up = fetch("w_up", h2)
    u = matmul(tag + "up_proj", h2, g_up, view=up_view(0, f))
    g = convglu(tag + "convglu", u, w.conv_w, w.conv_b, n_lat)
    g_down = fetch("w_down", g)
    yd = matmul(tag + "down_proj", g, g_down, view=down_view(0, f, False))
    (x2,) = row_map(tag + "resid2", resid_tile, [Row(x1, d), Row(yd, d)], [Par(_mod(mods, 5), True)], [(d, F32)], t, n_lat)
    saved = dict(xs=xs, h=h, z=z, zg=zg, p=p, o_att=o_att, lse=lse, o_ret=o_ret, s_ret=s_ret, o_gla=o_gla, s_gla=s_gla,
                 m=m, y=y, x1=x1, h2=h2, u=u, g=g, yd=yd, w_main=w_main, w_gate=w_gate, g_out=g_out, g_up=g_up, g_down=g_down)
    return x2, saved


def _sum_dirs(a):
    w = a.shape[1] // 2
    return a[:, :w] + a[:, w:]


def layer_bwd(l, dx2, s, mods, w, cos, sin, n_lat, grad_ready):
    t, d = dx2.shape
    tag = f"l{l}_b_"
    dyd, dgate5 = row_vjp(tag + "resid2", resid_tile, [Row(s["x1"], d, 0, False), Row(s["yd"], d)],
                          [Par(_mod(mods, 5), True)], [dx2], t, n_lat, row_grad_dtype=BF16)
    f = w.conv_b.shape[1]
    dg = matmul(tag + "down_dx", dyd, s["g_down"], tb=True, view=down_view(0, f, True))
    dw_down = matmul(tag + "down_dw", s["g"], dyd, ta=True, out_dtype=BF16)
    da, dv, dcw, dcb = convglu_bwd(tag + "convglu", s["u"], w.conv_w, w.conv_b, dg, n_lat)
    dh2 = matmul(tag + "up_dx_gate", da, s["g_up"], tb=True, view=up_view(0, f, 0))
    dh2 = matmul(tag + "up_dx_value", dv, s["g_up"], tb=True, view=up_view(0, f, 1), add=dh2)
    dw_up = (matmul(tag + "up_dw_gate", s["h2"], da, ta=True, out_dtype=BF16),
             matmul(tag + "up_dw_value", s["h2"], dv, ta=True, out_dtype=BF16))
    started = grad_ready("ffn", dict(w_up=dw_up, w_down=dw_down))
    nm2 = [Par(w.norm2_g), Par(_mod(mods, 3), True), Par(_mod(mods, 4), True)]
    dx1, dg2, dshift3, dscale4 = row_vjp(tag + "norm2", normmod_tile, [Row(s["x1"], d)], nm2, [dh2], t, n_lat,
                                         add_to_first=dx2, after=started)
    dy, dgate2 = row_vjp(tag + "resid1", resid_tile, [Row(s["xs"], d, 0, False), Row(s["y"], d)],
                         [Par(_mod(mods, 2), True)], [dx1], t, n_lat, row_grad_dtype=BF16)
    dm = matmul(tag + "out_dx", dy, s["g_out"], tb=True, view=out_view(0, True))
    dw_out = matmul(tag + "out_dw", s["m"], dy, ta=True, out_dtype=BF16)
    rows, pars = _post_args(s["o_att"], s["o_ret"], s["o_gla"], s["z"], w)
    started = grad_ready("w_out", dict(w_out=dw_out))
    do_att, do_ret, do_gla, d_rg, d_gr, d_ret_g, d_gla_g = row_vjp(tag + "post", post_tile, rows, pars, [dm], t, n_lat, after=started)
    dq_a, dk_a, dv_a = attn_bwd(s["p"], s["z"], s["o_att"], s["lse"], do_att, n_lat)
    dq_r, dk_r, dv_r, dlg = ret_bwd(s["p"], s["z"], w.lg, s["s_ret"], do_ret, n_lat)
    dq_g, dk_g, dv_g, dla = gla_bwd(s["p"], s["z"], s["s_gla"], do_gla, n_lat)
    dp = jnp.concatenate([dq_a, dk_a, _sum_dirs(dq_g), _sum_dirs(dq_r), _sum_dirs(dk_r), dla], axis=1)
    rows, pars = _prep_args(s["z"], s["zg"], cos, sin, w)
    d_zqk, d_zrq, d_zrk, d_zgq, dzg, d_qg, d_kg, d_up, d_gb = row_vjp(tag + "prep", prep_tile, rows, pars, [dp], t, n_lat)
    dz = jnp.concatenate([d_zqk, dv_a, d_zrq, d_zrk, _sum_dirs(dv_r), d_rg, d_zgq, _sum_dirs(dk_g), _sum_dirs(dv_g), d_gr], axis=1)
    dz, dzg = dz.astype(BF16), dzg.astype(BF16)
    dh_gate = matmul(tag + "gate_dx", dzg, s["w_gate"], tb=True)
    dh = matmul(tag + "in_dx", dz, s["w_main"], tb=True, add=dh_gate)
    dw_main = matmul(tag + "in_dw", s["h"], dz, ta=True, out_dtype=BF16)
    dw_gate = matmul(tag + "gate_dw", s["h"], dzg, ta=True, out_dtype=BF16)
    started = grad_ready("w_in", dict(w_main=dw_main, w_gate=dw_gate))
    nm1 = [Par(w.norm1_g), Par(_mod(mods, 0), True), Par(_mod(mods, 1), True)]
    dx, dg1, dshift0, dscale1 = row_vjp(tag + "norm1", normmod_tile, [Row(s["xs"], d)], nm1, [dh], t, n_lat,
                                        add_to_first=dx1, after=started)
    dmods = jnp.concatenate([dshift0, dscale1, dgate2, dshift3, dscale4, dgate5], axis=1)
    grads = dict(w_main=dw_main, w_gate=dw_gate, w_out=dw_out, w_up=dw_up, w_down=dw_down, norm1_g=dg1, q_g=d_qg, k_g=d_kg,
                 lg=dlg, ret_g=d_ret_g, gate_up=d_up, gate_b=d_gb, gla_g=d_gla_g, norm2_g=dg2, conv_w=dcw, conv_b=dcb)
    return dx, dmods, grads


def rope_tables(n_lat, n_ctx):
    rows = n_lat // GRID_W
    row = jnp.repeat(jnp.arange(rows, dtype=F32), GRID_W)
    col = jnp.tile(jnp.arange(GRID_W, dtype=F32), rows)
    n_freq = HEAD_DIM // 4
    inv_freq = ROPE_THETA ** (-jnp.arange(n_freq, dtype=F32) / n_freq)
    ang = jnp.concatenate([row[:, None] * inv_freq, col[:, None] * inv_freq], axis=-1)
    cos, sin = jnp.cos(ang), jnp.sin(ang)
    cos = jnp.concatenate([jnp.concatenate([cos, cos], axis=1), jnp.ones((n_ctx, HEAD_DIM), F32)], axis=0)
    sin = jnp.concatenate([jnp.concatenate([-sin, sin], axis=1), jnp.zeros((n_ctx, HEAD_DIM), F32)], axis=0)
    return cos, sin


def local_step(xs, target, mods, weights, fetch, final_g, n_lat, grad_ready):
    t, d = xs.shape
    cos, sin = rope_tables(n_lat, t - n_lat)
    saved = []
    h = xs
    for l, w in enumerate(weights):
        h, s = layer_fwd(l, h, mods[l], w, functools.partial(fetch, l), cos, sin, n_lat)
        saved.append(s)
    loss, dlat, dgf = final_loss(h, target, final_g, n_lat)
    dx = jnp.concatenate([dlat, jnp.zeros((t - n_lat, d), F32)], axis=0)
    dmods, grads = [None] * len(weights), [None] * len(weights)
    for l in reversed(range(len(weights))):
        dx, dmods[l], grads[l] = layer_bwd(l, dx, saved[l], mods[l], weights[l], cos, sin, n_lat, functools.partial(grad_ready, l))
    return loss, dx, dmods, grads, dgf


WEIGHT_NAMES = ("c_ctx", "ada_w", "ada_b", "norm1_g", "w_in", "q_norm_g", "k_norm_g", "ret_log_decay", "ret_norm_g",
                "gla_gate_up", "gla_gate_b", "gla_norm_g", "w_out", "norm2_g", "w_up", "conv_w", "conv_b", "w_down", "final_norm_g")
PACK_QUANTUM = 8 * LANES


def _pack(arrays):
    flat = jnp.concatenate([a.reshape(-1).astype(F32) for a in arrays])
    n = -(-flat.shape[0] // PACK_QUANTUM) * PACK_QUANTUM
    return jnp.pad(flat, (0, n - flat.shape[0])).reshape(8, n // 8)


def _unpack(flat2d, shapes):
    out, at = [], 0
    for s in shapes:
        size = int(np.prod(s))
        out.append(flat2d[:, at:at + size].reshape((flat2d.shape[0],) + tuple(s)))
        at += size
    return out


def _per_device(gathered):
    return gathered.reshape(N_DEV, -1)


def _from_chips(per_device, axis):
    chips = per_device[0::2]
    moved = jnp.moveaxis(chips, 0, axis)
    shape = moved.shape
    return moved.reshape(shape[:axis] + (shape[axis] * shape[axis + 1],) + shape[axis + 2:])


def kernel(x, c, ctx, c_ctx, ada_w, ada_b, norm1_g, w_in, q_norm_g, k_norm_g, ret_log_decay, ret_norm_g, gla_gate_up, gla_gate_b, gla_norm_g, w_out, norm2_g, w_up, conv_w, conv_b, w_down, final_norm_g, loss_target, m_c_ctx, m_ada_w, m_ada_b, m_norm1_g, m_w_in, m_q_norm_g, m_k_norm_g, m_ret_log_decay, m_ret_norm_g, m_gla_gate_up, m_gla_gate_b, m_gla_norm_g, m_w_out, m_norm2_g, m_w_up, m_conv_w, m_conv_b, m_w_down, m_final_norm_g, v_c_ctx, v_ada_w, v_ada_b, v_norm1_g, v_w_in, v_q_norm_g, v_k_norm_g, v_ret_log_decay, v_ret_norm_g, v_gla_gate_up, v_gla_gate_b, v_gla_norm_g, v_w_out, v_norm2_g, v_w_up, v_conv_w, v_conv_b, v_w_down, v_final_norm_g):
    weights = dict(zip(WEIGHT_NAMES, (c_ctx, ada_w, ada_b, norm1_g, w_in, q_norm_g, k_norm_g, ret_log_decay, ret_norm_g,
                                      gla_gate_up, gla_gate_b, gla_norm_g, w_out, norm2_g, w_up, conv_w, conv_b, w_down, final_norm_g)))
    mom_m = dict(zip(WEIGHT_NAMES, (m_c_ctx, m_ada_w, m_ada_b, m_norm1_g, m_w_in, m_q_norm_g, m_k_norm_g, m_ret_log_decay, m_ret_norm_g,
                                    m_gla_gate_up, m_gla_gate_b, m_gla_norm_g, m_w_out, m_norm2_g, m_w_up, m_conv_w, m_conv_b, m_w_down, m_final_norm_g)))
    mom_v = dict(zip(WEIGHT_NAMES, (v_c_ctx, v_ada_w, v_ada_b, v_norm1_g, v_w_in, v_q_norm_g, v_k_norm_g, v_ret_log_decay, v_ret_norm_g,
                                    v_gla_gate_up, v_gla_gate_b, v_gla_norm_g, v_w_out, v_norm2_g, v_w_up, v_conv_w, v_conv_b, v_w_down, v_final_norm_g)))
    depth, d = norm1_g.shape
    assert d == D_MODEL and x.shape[0] == 1
    n_lat, n_ctx, f = x.shape[1], ctx.shape[1], conv_b.shape[1]
    assert n_lat % ROW_TILE == 0 and n_ctx % ROW_TILE == 0 and f % FFN_COL_TILE == 0 and f % N_CHIPS == 0
    n_in = w_in.shape[2]
    n_ada = ada_w.shape[2]
    xi, yi, ci = lax.axis_index("x"), lax.axis_index("y"), lax.axis_index("c")
    chip = 2 * xi + yi
    dev = 2 * chip + ci

    big = ("w_in", "w_out", "w_up", "w_down")
    order = [(l, name) for l in range(depth) for name in big]
    shards = [weights[name][l].astype(BF16) for l, name in order]
    passing = {}

    def pass_on(k, after):
        tag = "{1}{0}".format(*order[k])
        own, land = split_wait("gather_wait_" + tag, GATHER, in_flight[k], after)
        (moving,), _ = split_start("gather_pass_" + tag, PASS_ON, [land])
        passing[k] = (own, moving)

    def fetch(l, name, after):
        k = order.index((l, name))
        if k == 0:
            pass_on(0, after)
        own, moving = passing.pop(k)
        (land,) = split_wait(f"gather_pass_wait_{name}{l}", PASS_ON, moving, after)
        if k + 1 < len(order):
            pass_on(k + 1, after)
        land = lax.dynamic_update_slice_in_dim(land, own[None], chip, axis=0)
        if name != "w_in":
            return land
        cols = jnp.concatenate([land[q] for q in range(N_CHIPS)], axis=1)
        return cols[:, :N_MAIN], jnp.pad(cols[:, N_MAIN:], ((0, 0), (0, LANES - N_GATE)))

    small_shapes = [c.shape[1:], conv_w.shape, gla_gate_up.shape, gla_gate_b.shape]
    got = _per_device(all_gather_small("gather_small", _pack([c, conv_w, gla_gate_up, gla_gate_b])))
    c_all, conv_w_sh, gate_up_sh, gate_b_sh = _unpack(got, small_shapes)
    conv_w_full = _from_chips(conv_w_sh, 2)
    gate_up_full = _from_chips(gate_up_sh, 3)
    gate_b_full = _from_chips(gate_b_sh, 2)

    act = jnp.zeros((16, d), F32).at[0:N_DEV].set(jax.nn.silu(c_all)).at[N_DEV].set(jax.nn.silu(c_ctx))
    mod_sh = jnp.stack([matmul(f"ada_fwd{l}", act, ada_w, view=ada_view(l, n_ada, False)) for l in range(depth)])
    got = _per_device(all_gather_small("gather_mods", _pack([mod_sh])))
    (mod_sh_all,) = _unpack(got, [mod_sh.shape])
    mod_full = _from_chips(mod_sh_all, 2) + ada_b[:, None, :]
    mod_mine = lax.dynamic_index_in_dim(mod_full, dev, axis=1, keepdims=False)
    mods = [jnp.stack([mod_mine[l].reshape(N_MOD, d), mod_full[l, N_DEV].reshape(N_MOD, d)]) for l in range(depth)]
    in_flight, token = split_start("gather_start", GATHER, shards, [(N_CHIPS,) + s.shape for s in shards], after=mod_full)

    layer_w = []
    for l in range(depth):
        up = jnp.zeros((2, LANES, GLA_HEADS * GLA_DK), F32)
        up = up.at[0, 0:GLA_RANK].set(gate_up_full[l, 0]).at[1, GLA_RANK:2 * GLA_RANK].set(gate_up_full[l, 1])
        layer_w.append(LayerWeights(
            norm1_g=norm1_g[l].reshape(1, 1, d), q_g=q_norm_g[l].reshape(1, 1, HEAD_DIM), k_g=k_norm_g[l].reshape(1, 1, HEAD_DIM),
            lg=ret_log_decay[l].reshape(2, RET_HEADS, 1, 1), ret_g=ret_norm_g[l].reshape(1, 1, HEAD_DIM),
            gate_up=up.reshape(1, 2 * LANES, -1), gate_b=gate_b_full[l].reshape(1, 2, -1), gla_g=gla_norm_g[l].reshape(1, 1, HEAD_DIM),
            norm2_g=norm2_g[l].reshape(1, 1, d), conv_w=conv_w_full[l], conv_b=conv_b[l].reshape(1, f)))

    def pieces_of(name, g):
        if name == "w_in":
            full_cols = jnp.concatenate([g["w_main"], g["w_gate"][:, :N_GATE]], axis=1)
            return jnp.stack([full_cols[:, q * n_in:(q + 1) * n_in] for q in range(N_CHIPS)])
        if name == "w_up":
            gate, value = g["w_up"]
            return jnp.stack([gate[:, :f // 2], gate[:, f // 2:], value[:, :f // 2], value[:, f // 2:]])
        return g[name].reshape(N_CHIPS, -1, d)

    groups = {"ffn": ("w_up", "w_down"), "w_out": ("w_out",), "w_in": ("w_in",)}
    reducing = {}
    to_sibling = []

    def sibling_arrived(after):
        started = None
        while to_sibling:
            l, group, in_flight_halves = to_sibling.pop(0)
            sums = []
            for name, halves in zip(groups[group], in_flight_halves):
                pieces, from_sibling = split_wait(f"rs_sibling_wait_{name}{l}", SIBLING_HALF, halves, after)
                sums.append(add_sibling_half(f"rs_add_sibling_{name}{l}", pieces, from_sibling, ci))
            in_flight_sums, token = split_start(f"rs_start_{group}{l}", SCATTER, sums, [(3,) + s.shape[1:] for s in sums])
            reducing.update({(l, name): grp for name, grp in zip(groups[group], in_flight_sums)})
            started = token if started is None else started + token
        return started

    def grad_ready(l, group, g):
        pieces = [pieces_of(name, g) for name in groups[group]]
        before = None if (l, group) == (0, "w_in") else sibling_arrived(pieces[0])
        in_flight_halves, started = split_start(f"rs_sibling_{group}{l}", SIBLING_HALF, pieces,
                                                [(N_CHIPS, pc.shape[1] // 2, pc.shape[2]) for pc in pieces])
        to_sibling.append((l, group, in_flight_halves))
        return started if before is None else started + before

    xs = jnp.concatenate([x[0], ctx[0]], axis=0) + token[0, 0]
    loss, dx, dmods, grads, dgf = local_step(xs, loss_target[0], mods, layer_w, fetch, final_norm_g.reshape(1, d), n_lat, grad_ready)

    def gate_up_grad(g):
        return jnp.stack([g[0, 0:GLA_RANK], g[0, LANES + GLA_RANK:LANES + 2 * GLA_RANK]])

    per_layer = [[dmods[l][0], dmods[l][1], grads[l]["norm1_g"], grads[l]["norm2_g"], grads[l]["q_g"], grads[l]["k_g"],
                  grads[l]["ret_g"], grads[l]["gla_g"], grads[l]["lg"], gate_up_grad(grads[l]["gate_up"]), grads[l]["gate_b"],
                  grads[l]["conv_w"], grads[l]["conv_b"]] for l in range(depth)]
    layer_shapes = [(N_MOD * d,), (N_MOD * d,), (d,), (d,), (HEAD_DIM,), (HEAD_DIM,), (HEAD_DIM,), (HEAD_DIM,), (2, RET_HEADS),
                    (2, GLA_RANK, GLA_HEADS * GLA_DK), (2, GLA_HEADS * GLA_DK), (3, f), (f,)]
    packed = _pack([a for lay in per_layer for a in lay] + [dgf, loss[0, 0:1]])
    gathered = all_gather_small("gather_small_grads", packed)
    every = _unpack(_per_device(gathered), layer_shapes * depth + [(d,), (1,)])
    total = _unpack(sum_device_blocks("sum_small_grads", gathered).reshape(1, -1), layer_shapes * depth + [(d,), (1,)])
    nl = len(layer_shapes)

    def tot(l, k):
        return total[l * nl + k][0]

    out = {"norm1_g": jnp.stack([tot(l, 2) for l in range(depth)]), "norm2_g": jnp.stack([tot(l, 3) for l in range(depth)]),
           "q_norm_g": jnp.stack([tot(l, 4) for l in range(depth)]), "k_norm_g": jnp.stack([tot(l, 5) for l in range(depth)]),
           "ret_norm_g": jnp.stack([tot(l, 6) for l in range(depth)]), "gla_norm_g": jnp.stack([tot(l, 7) for l in range(depth)]),
           "ret_log_decay": jnp.stack([tot(l, 8) for l in range(depth)]),
           "gla_gate_up": lax.dynamic_slice_in_dim(jnp.stack([tot(l, 9) for l in range(depth)]), chip * gla_gate_up.shape[3], gla_gate_up.shape[3], axis=3),
           "gla_gate_b": lax.dynamic_slice_in_dim(jnp.stack([tot(l, 10) for l in range(depth)]), chip * gla_gate_b.shape[2], gla_gate_b.shape[2], axis=2),
           "conv_w": lax.dynamic_slice_in_dim(jnp.stack([tot(l, 11) for l in range(depth)]), chip * conv_w.shape[2], conv_w.shape[2], axis=2),
           "conv_b": jnp.stack([tot(l, 12) for l in range(depth)]),
           "final_norm_g": total[depth * nl][0],
           "ada_b": jnp.stack([tot(l, 0) + tot(l, 1) for l in range(depth)])}
    loss_total = total[depth * nl + 1][0, 0]

    dmod_all = jnp.zeros((depth, 16, N_MOD * d), F32)
    for l in range(depth):
        dmod_all = dmod_all.at[l, 0:N_DEV].set(every[l * nl][:, :]).at[l, N_DEV].set(tot(l, 1))
    dmod_cols = lax.dynamic_slice_in_dim(dmod_all, chip * n_ada, n_ada, axis=2)
    out["ada_w"] = jnp.stack([matmul(f"ada_dw{l}", act, dmod_cols[l], ta=True) for l in range(depth)])
    dact = matmul("ada_dx0", dmod_cols[0], ada_w, tb=True, view=ada_view(0, n_ada, True))
    for l in range(1, depth):
        dact = matmul(f"ada_dx{l}", dmod_cols[l], ada_w, tb=True, view=ada_view(l, n_ada, True), add=dact)
    got = _per_device(all_gather_small("gather_dcctx", _pack([dact[N_DEV]])))
    sibling_arrived(got)
    got = got[0::2, :d]
    dsilu = got[0] + got[1] + got[2] + got[3]
    sig = jax.nn.sigmoid(c_ctx)
    out["c_ctx"] = dsilu * (sig + c_ctx * sig * (1.0 - sig))

    deltas, new_m, new_v = {}, {}, {}

    def update(name):
        out[name] = out[name].reshape(weights[name].shape)
        deltas[name], new_m[name], new_v[name] = adamw("adamw_" + name, weights[name], out[name], mom_m[name], mom_v[name])

    for name in WEIGHT_NAMES:
        if name not in big:
            update(name)
    behind = new_v["ada_w"]
    joining = []

    def joined(after):
        name, in_flight_halves = joining.pop()
        out[name] = jnp.stack([split_wait(f"rs_join_wait_{name}{l}", JOIN, grp, after)[0] for l, grp in enumerate(in_flight_halves)])
        update(name)
        return new_v[name]

    for name in ("w_down", "w_up", "w_out", "w_in"):
        halves = []
        for l in range(depth):
            sums, got = split_wait(f"rs_wait_{name}{l}", SCATTER, reducing[(l, name)], behind)
            halves.append(add_chip_sums(f"rs_add_chips_{name}{l}", sums, got, chip, ci))
        in_flight_halves, _ = split_start("rs_join_" + name, JOIN, halves)
        if joining:
            behind = joined(behind)
        joining.append((name, in_flight_halves))
    joined(behind)
    grad_x = dx[:n_lat].reshape(x.shape)
    return (loss_total, grad_x, *[out[n] for n in WEIGHT_NAMES], *[deltas[n] for n in WEIGHT_NAMES],
            *[new_m[n] for n in WEIGHT_NAMES], *[new_v[n] for n in WEIGHT_NAMES])
```

```python
import functools
from typing import NamedTuple

import numpy as np
import jax
import jax.numpy as jnp
from jax import lax
from jax.experimental import pallas as pl
from jax.experimental.pallas import tpu as pltpu

F32 = jnp.float32
BF16 = jnp.bfloat16

D_MODEL = 2048
HEAD_DIM = 128
ATT_Q_HEADS = 8
ATT_KV_HEADS = 2
ATT_GROUP = ATT_Q_HEADS // ATT_KV_HEADS
RET_HEADS = 4
GLA_HEADS = 4
GLA_DK = 64
GLA_DV = 128
GLA_RANK = 16
GLA_TAU = 16.0
RET_CHUNK = 256
GLA_CHUNK = 128
GRID_W = 64
ROPE_THETA = 10000.0
N_MOD = 6
EPS = 1e-6
N_MAIN = 5120
N_GATE = 2 * GLA_RANK
LANES = 128
ROW_TILE = 256
FFN_COL_TILE = 256
VMEM_LIMIT = 56 * 1024 * 1024

ADAM_LR = 0.001
ADAM_B1 = 0.9
ADAM_B2 = 0.999
ADAM_EPS = 1e-08
ADAM_WD = 0.01
ADAM_STEP = 10

Z_AQ, Z_AK, Z_AV = 0, 1024, 1280
Z_RQ, Z_RK, Z_RV, Z_RG = 1536, 2048, 2560, 3072
Z_GQ, Z_GK, Z_GV, Z_GR = 3584, 3840, 4096, 4608
P_AQ, P_AK, P_GQ, P_RQ, P_RK, P_LA = 0, 1024, 1280, 1536, 2048, 2560
P_W = 3072


def _params(sem=None):
    return pltpu.CompilerParams(dimension_semantics=sem, vmem_limit_bytes=VMEM_LIMIT)


def _pick(n, cands):
    for c in cands:
        if n % c == 0:
            return c
    return n


_NN = (((1,), (0,)), ((), ()))
_NT = (((1,), (1,)), ((), ()))
_TN = (((0,), (0,)), ((), ()))


def _dg(a, b, dims):
    return lax.dot_general(a.astype(BF16), b.astype(BF16), dims, preferred_element_type=F32)


@jax.custom_vjp
def bdot(a, b):
    return _dg(a, b, _NN)


def _bdot_fwd(a, b):
    return _dg(a, b, _NN), (a, b)


def _bdot_bwd(res, ct):
    a, b = res
    return _dg(ct, b, _NT), _dg(a, ct, _TN)


bdot.defvjp(_bdot_fwd, _bdot_bwd)


@jax.custom_vjp
def bdot_nt(a, b):
    return _dg(a, b, _NT)


def _bdot_nt_fwd(a, b):
    return _dg(a, b, _NT), (a, b)


def _bdot_nt_bwd(res, ct):
    a, b = res
    return _dg(ct, b, _NN), _dg(ct, a, _TN)


bdot_nt.defvjp(_bdot_nt_fwd, _bdot_nt_bwd)


@jax.custom_vjp
def bdot_tn(a, b):
    return _dg(a, b, _TN)


def _bdot_tn_fwd(a, b):
    return _dg(a, b, _TN), (a, b)


def _bdot_tn_bwd(res, ct):
    a, b = res
    return _dg(b, ct, _NT), _dg(a, ct, _NN)


bdot_tn.defvjp(_bdot_tn_fwd, _bdot_tn_bwd)


def _split3(x):
    x1 = x.astype(BF16)
    r1 = x - x1.astype(F32)
    x2 = r1.astype(BF16)
    x3 = (r1 - x2.astype(F32)).astype(BF16)
    return x1, x2, x3


def _mask_dot(mask_bf16, x, dims):
    x1, x2, x3 = _split3(x)
    f = lambda t: lax.dot_general(mask_bf16, t, dims, preferred_element_type=F32)
    return f(x1) + f(x2) + f(x3)


@jax.custom_vjp
def mask_cumsum(mask, x):
    return _mask_dot(mask.astype(BF16), x, _NN)


def _mask_cumsum_fwd(mask, x):
    return mask_cumsum(mask, x), mask


def _mask_cumsum_bwd(mask, ct):
    return jnp.zeros_like(mask), _mask_dot(mask.astype(BF16), ct, _TN)


mask_cumsum.defvjp(_mask_cumsum_fwd, _mask_cumsum_bwd)


def _roll(x, shift, axis):
    return pltpu.roll(x, shift % x.shape[axis], axis)


@functools.partial(jax.custom_vjp, nondiff_argnums=(1, 2))
def roll(x, shift, axis):
    return _roll(x, shift, axis)


def _roll_fwd(x, shift, axis):
    return _roll(x, shift, axis), None


def _roll_bwd(shift, axis, _, ct):
    return (_roll(ct, -shift, axis),)


roll.defvjp(_roll_fwd, _roll_bwd)


def rms(x):
    return x * lax.rsqrt(jnp.mean(x * x, axis=-1, keepdims=True) + EPS)


def silu(x):
    return x * (1.0 / (1.0 + jnp.exp(-x)))


def log_sigmoid(x):
    return jnp.minimum(x, 0.0) - jnp.log(1.0 + jnp.exp(-jnp.abs(x)))


def rope(t, cos, sin):
    return t * cos + roll(t, HEAD_DIM // 2, 1) * sin


def _heads(x, n, width=HEAD_DIM):
    return [x[:, h * width:(h + 1) * width] for h in range(n)]


class Row(NamedTuple):
    arr: jax.Array
    width: int
    idx: int = 0
    diff: bool = True


class Par(NamedTuple):
    arr: jax.Array
    grouped: bool = False
    diff: bool = True


def _row_specs(rows, pars, tm, n_lat_tiles):
    def grp(i):
        return jnp.minimum(i // n_lat_tiles, 1)

    specs = [pl.BlockSpec((tm, r.width), functools.partial(lambda i, k: (i, k), k=r.idx)) for r in rows]
    for p in pars:
        blk = (1,) + p.arr.shape[1:]
        if p.grouped:
            specs.append(pl.BlockSpec(blk, lambda i: (grp(i), 0, 0)))
        else:
            specs.append(pl.BlockSpec(blk, lambda i: (0, 0, 0)))
    return specs


def row_map(name, fn, rows, pars, outs, n_rows, n_lat):
    tm = ROW_TILE
    nr, npar = len(rows), len(pars)

    def body(*refs):
        vals = [r[...] for r in refs[:nr]] + [p[0] for p in refs[nr:nr + npar]]
        res = fn(*vals)
        for o, v in zip(refs[nr + npar:], res):
            o[...] = v.astype(o.dtype)

    return pl.pallas_call(
        body, name=name, grid=(n_rows // tm,),
        in_specs=_row_specs(rows, pars, tm, n_lat // tm),
        out_specs=[pl.BlockSpec((tm, w), lambda i: (i, 0)) for w, _ in outs],
        out_shape=[jax.ShapeDtypeStruct((n_rows, w), dt) for w, dt in outs],
        compiler_params=_params(("arbitrary",)),
    )(*[r.arr for r in rows], *[p.arr for p in pars])


def row_vjp(name, fn, rows, pars, cts, n_rows, n_lat, add_to_first=None, row_grad_dtype=F32, after=None):
    tm = ROW_TILE
    nr, npar, nc = len(rows), len(pars), len(cts)
    n_lat_tiles = n_lat // tm
    args = list(rows) + list(pars)
    diff_pos = [k for k, a in enumerate(args) if a.diff]
    n_add = 0 if add_to_first is None else 1
    n_after = 0 if after is None else 1

    def body(*refs):
        i = pl.program_id(0)
        vals = [r[...] for r in refs[:nr]] + [p[0] for p in refs[nr:nr + npar]]
        ct_vals = tuple(c[...] for c in refs[nr + npar:nr + npar + nc])
        out_refs = refs[nr + npar + nc + n_add + n_after:]

        def g(*dv):
            full = list(vals)
            for k, v in zip(diff_pos, dv):
                full[k] = v
            return tuple(fn(*full))

        _, vjp = jax.vjp(g, *[vals[k] for k in diff_pos])
        grads = vjp(ct_vals)
        for n, (k, o, gr) in enumerate(zip(diff_pos, out_refs, grads)):
            if k < nr:
                o[...] = (gr + refs[nr + npar + nc][...] if (n == 0 and n_add) else gr).astype(o.dtype)
            else:
                first = (i == 0) | (i == n_lat_tiles) if args[k].grouped else (i == 0)

                @pl.when(first)
                def _():
                    o[0] = gr

                @pl.when(jnp.logical_not(first))
                def _():
                    o[0] += gr

    def grp(i):
        return jnp.minimum(i // n_lat_tiles, 1)

    out_specs, out_shape = [], []
    for k in diff_pos:
        a = args[k]
        if k < nr:
            out_specs.append(pl.BlockSpec((tm, a.width), lambda i: (i, 0)))
            out_shape.append(jax.ShapeDtypeStruct((n_rows, a.width), row_grad_dtype))
        else:
            blk = (1,) + a.arr.shape[1:]
            out_specs.append(pl.BlockSpec(blk, (lambda i: (grp(i), 0, 0)) if a.grouped else (lambda i: (0, 0, 0))))
            out_shape.append(jax.ShapeDtypeStruct(a.arr.shape, F32))
    extra = list(cts) + ([add_to_first] if n_add else [])
    ct_specs = [pl.BlockSpec((tm, c.shape[1]), lambda i: (i, 0)) for c in extra]
    if n_after:
        extra.append(after)
        ct_specs.append(pl.BlockSpec(memory_space=pl.ANY))
    return pl.pallas_call(
        body, name=name, grid=(n_rows // tm,),
        in_specs=_row_specs(rows, pars, tm, n_lat_tiles) + ct_specs,
        out_specs=out_specs, out_shape=out_shape,
        compiler_params=_params(("arbitrary",)),
    )(*[r.arr for r in rows], *[p.arr for p in pars], *extra)


class BView(NamedTuple):
    n: int
    k: int
    tn: int
    tk: int
    index_map: object
    lead: int = 1


MATMUL_VMEM_BUDGET = 40 * 1024 * 1024


def _matmul_tiles(m, n, k, a_bytes, b_bytes, o_bytes):
    tms = [c for c in (1152, 1024, 768, 512, 256, 128) if m % c == 0] or [m]
    tns = [c for c in (2048, 1408, 1280, 1024, 768, 512, 256, 128) if n % c == 0] or [n]
    tks = [k] + [c for c in (2816, 2304, 2048, 1408, 1024, 512, 256, 128) if k % c == 0 and c < k]
    for tk in tks:
        fits = [(tm * tn, tm, tn) for tm in tms for tn in tns
                if 2 * (tm * tk * a_bytes + tk * tn * b_bytes + tm * tn * o_bytes) + 2 * tm * tn * 4 <= MATMUL_VMEM_BUDGET]
        if fits and (max(fits)[0] >= 512 * 512 or tms == [m] or tk == tks[-1]):
            _, tm, tn = max(fits)
            return tm, tn, tk
    raise ValueError(f"no matmul tiling for {(m, n, k)}")


class OView(NamedTuple):
    shape: tuple
    index_map: object
    tn: int = None
    into: object = None


def matmul(name, a, b, *, ta=False, tb=False, add=None, out_dtype=F32, view=None, o_view=None):
    m = a.shape[1] if ta else a.shape[0]
    o_bytes = jnp.dtype(out_dtype).itemsize * (1 if add is None else 2)
    if view is None:
        k = a.shape[0] if ta else a.shape[1]
        n = b.shape[0] if tb else b.shape[1]
        assert (b.shape[1] if tb else b.shape[0]) == k, (a.shape, b.shape, ta, tb)
        if o_view is not None and o_view.tn is not None:
            tn = o_view.tn
            tm, _, tk = _matmul_tiles(m, tn, k, a.dtype.itemsize, b.dtype.itemsize, o_bytes)
        else:
            tm, tn, tk = _matmul_tiles(m, n, k, a.dtype.itemsize, b.dtype.itemsize, o_bytes)
    else:
        n, k, tn, tk = view.n, view.k, view.tn, view.tk
        tm, _, _ = _matmul_tiles(m, tn, tk, a.dtype.itemsize, b.dtype.itemsize, o_bytes)
    nk = k // tk
    dims = (((0 if ta else 1,), (1 if tb else 0,)), ((), ()))

    def body(a_ref, b_ref, *rest):
        prod = lax.dot_general(a_ref[...].astype(BF16), b_ref[...].astype(BF16), dims, preferred_element_type=F32)
        if nk == 1:
            o_ref = rest[-1]
            o_ref[...] = (prod if add is None else prod + rest[0][...]).astype(o_ref.dtype)
            return
        o_ref, acc = rest[-2:]
        kk = pl.program_id(2)

        @pl.when(kk == 0)
        def _():
            acc[...] = prod

        @pl.when(kk != 0)
        def _():
            acc[...] += prod

        @pl.when(kk == nk - 1)
        def _():
            r = acc[...]
            if add is not None:
                r = r + rest[0][...]
            o_ref[...] = r.astype(o_ref.dtype)

    if ta:
        a_spec = pl.BlockSpec((tk, tm), lambda i, j, kk: (kk, i))
    else:
        a_spec = pl.BlockSpec((tm, tk), lambda i, j, kk: (i, kk))
    b_tile = (tn, tk) if tb else (tk, tn)
    if view is not None:
        b_spec = pl.BlockSpec((None,) * view.lead + b_tile, view.index_map)
    elif tb:
        b_spec = pl.BlockSpec(b_tile, lambda i, j, kk: (j, kk))
    else:
        b_spec = pl.BlockSpec(b_tile, lambda i, j, kk: (kk, j))
    o_spec = pl.BlockSpec((tm, tn), lambda i, j, kk: (i, j))
    ins = [a, b] + ([add] if add is not None else [])
    in_specs = [a_spec, b_spec] + ([o_spec] if add is not None else [])
    out_shape, aliases = jax.ShapeDtypeStruct((m, n), out_dtype), {}
    if o_view is not None:
        assert add is None
        o_spec = pl.BlockSpec((None, tm, tn), o_view.index_map)
        out_shape = jax.ShapeDtypeStruct(o_view.shape, out_dtype)
        if o_view.into is not None:
            aliases = {len(ins): 0}
            ins.append(o_view.into)
            in_specs.append(pl.BlockSpec(memory_space=pl.ANY))
    return pl.pallas_call(
        body, name=name, grid=(m // tm, n // tn, nk),
        in_specs=in_specs, out_specs=o_spec, out_shape=out_shape, input_output_aliases=aliases,
        scratch_shapes=[pltpu.VMEM((tm, tn), F32)] if nk > 1 else [],
        compiler_params=_params(("parallel", "parallel", "arbitrary")),
    )(*ins)


def normmod_tile(x, g, shift, scale):
    return (rms(x) * g * (1.0 + scale) + shift,)


def resid_tile(x, y, gate):
    return (x + gate * y,)


def prep_tile(z_qk, z_rq, z_rk, z_gq, zg, cos, sin, qg, kg, gate_up, gate_b):
    out = []
    for h, t in enumerate(_heads(z_qk, ATT_Q_HEADS + ATT_KV_HEADS)):
        out.append(rope(rms(t) * (qg if h < ATT_Q_HEADS else kg), cos, sin))
    gq = z_gq * (GLA_DK ** -0.5)
    rq = [rope(t, cos, sin) for t in _heads(z_rq, RET_HEADS)]
    rk = [rope(t * (HEAD_DIM ** -0.5), cos, sin) for t in _heads(z_rk, RET_HEADS)]
    la = [log_sigmoid(bdot(zg, gate_up[d * LANES:(d + 1) * LANES]) + gate_b[d:d + 1]) * (1.0 / GLA_TAU) for d in range(2)]
    return (jnp.concatenate(out + [gq] + rq + rk + la, axis=1),)


def post_tile(o_att, o_ret_f, o_ret_b, o_gla_f, o_gla_b, rg, gr, ret_g, gla_g):
    ret = jnp.concatenate([rms(t) * ret_g for t in _heads(o_ret_f + o_ret_b, RET_HEADS)], axis=1) * silu(rg)
    gla = jnp.concatenate([rms(t) * gla_g for t in _heads(o_gla_f + o_gla_b, GLA_HEADS)], axis=1) * silu(gr)
    return (jnp.concatenate([o_att, ret, gla], axis=1),)


def _convglu_tile(n_lat, a, v, cw, cb):
    t = a.shape[0]
    row = lax.broadcasted_iota(jnp.int32, (t, 1), 0)
    has_prev = ((row != 0) & (row != n_lat)).astype(F32)
    has_next = ((row != n_lat - 1) & (row != t - 1)).astype(F32)
    conv = roll(a, 1, 0) * has_prev * cw[0:1] + a * cw[1:2] + roll(a, -1, 0) * has_next * cw[2:3] + cb
    return silu(conv) * v


def convglu(name, u, cw, cb, n_lat):
    t, f2 = u.shape
    f, tc = f2 // 2, FFN_COL_TILE
    nb = f // tc

    def body(a_ref, v_ref, cw_ref, cb_ref, o_ref):
        o_ref[...] = _convglu_tile(n_lat, a_ref[...], v_ref[...], cw_ref[...], cb_ref[...]).astype(o_ref.dtype)

    return pl.pallas_call(
        body, name=name, grid=(nb,),
        in_specs=[pl.BlockSpec((t, tc), lambda j: (0, j)), pl.BlockSpec((t, tc), lambda j: (0, nb + j)),
                  pl.BlockSpec((3, tc), lambda j: (0, j)), pl.BlockSpec((1, tc), lambda j: (0, j))],
        out_specs=pl.BlockSpec((t, tc), lambda j: (0, j)),
        out_shape=jax.ShapeDtypeStruct((t, f), BF16),
        compiler_params=_params(("parallel",)),
    )(u, u, cw, cb)


def convglu_bwd(name, u, cw, cb, dg, n_lat):
    t, f2 = u.shape
    f, tc = f2 // 2, FFN_COL_TILE
    nb = f // tc

    def body(a_ref, v_ref, cw_ref, cb_ref, dg_ref, da_ref, dv_ref, dcw_ref, dcb_ref):
        _, vjp = jax.vjp(functools.partial(_convglu_tile, n_lat), a_ref[...], v_ref[...], cw_ref[...], cb_ref[...])
        da, dv, dcw_ref[...], dcb_ref[...] = vjp(dg_ref[...])
        da_ref[...], dv_ref[...] = da.astype(BF16), dv.astype(BF16)

    col = pl.BlockSpec((t, tc), lambda j: (0, j))
    return pl.pallas_call(
        body, name=name, grid=(nb,),
        in_specs=[col, pl.BlockSpec((t, tc), lambda j: (0, nb + j)), pl.BlockSpec((3, tc), lambda j: (0, j)),
                  pl.BlockSpec((1, tc), lambda j: (0, j)), col],
        out_specs=[col, col, pl.BlockSpec((3, tc), lambda j: (0, j)), pl.BlockSpec((1, tc), lambda j: (0, j))],
        out_shape=[jax.ShapeDtypeStruct((t, f), BF16), jax.ShapeDtypeStruct((t, f), BF16),
                   jax.ShapeDtypeStruct((3, f), F32), jax.ShapeDtypeStruct((1, f), F32)],
        compiler_params=_params(("parallel",)),
    )(u, u, cw, cb, dg)


def final_loss(x, target, g, n_lat):
    tm = ROW_TILE
    d = x.shape[1]

    def body(x_ref, t_ref, g_ref, loss_ref, dx_ref, dg_ref):
        i = pl.program_id(0)
        tgt = t_ref[...]

        def f(xv, gv):
            e = rms(xv) * gv - tgt
            s = jnp.sum(jnp.sum(e * e, axis=1, keepdims=True), axis=0, keepdims=True)
            return s * (0.5 / d)

        val, vjp = jax.vjp(f, x_ref[...], g_ref[...])
        dx, dgv = vjp(jnp.ones((1, 1), F32))
        dx_ref[...] = dx

        @pl.when(i == 0)
        def _():
            dg_ref[...] = dgv
            loss_ref[...] = jnp.broadcast_to(val, loss_ref.shape)

        @pl.when(i != 0)
        def _():
            dg_ref[...] += dgv
            loss_ref[...] += jnp.broadcast_to(val, loss_ref.shape)

    return pl.pallas_call(
        body, name="final_loss", grid=(n_lat // tm,),
        in_specs=[pl.BlockSpec((tm, d), lambda i: (i, 0)), pl.BlockSpec((tm, d), lambda i: (i, 0)),
                  pl.BlockSpec((1, d), lambda i: (0, 0))],
        out_specs=[pl.BlockSpec((1, LANES), lambda i: (0, 0)), pl.BlockSpec((tm, d), lambda i: (i, 0)),
                   pl.BlockSpec((1, d), lambda i: (0, 0))],
        out_shape=[jax.ShapeDtypeStruct((1, LANES), F32), jax.ShapeDtypeStruct((n_lat, d), F32),
                   jax.ShapeDtypeStruct((1, d), F32)],
        compiler_params=_params(("arbitrary",)),
    )(x, target, g)


ATT_SCALE = HEAD_DIM ** -0.5
_AK_BLK = P_AK // HEAD_DIM
_AV_BLK = Z_AV // HEAD_DIM


def _att_specs(t, tq):
    gw = ATT_GROUP * HEAD_DIM
    q_spec = pl.BlockSpec((tq, gw), lambda kv, i: (i, kv))
    k_spec = pl.BlockSpec((t, HEAD_DIM), lambda kv, i: (0, _AK_BLK + kv))
    v_spec = pl.BlockSpec((t, HEAD_DIM), lambda kv, i: (0, _AV_BLK + kv))
    row_spec = pl.BlockSpec((ATT_GROUP, tq, 1), lambda kv, i: (kv, i, 0))
    return q_spec, k_spec, v_spec, row_spec


def _att_mask(i, t, tq, n_lat):
    col = lax.broadcasted_iota(jnp.int32, (1, t), 1)
    return jnp.where((i >= n_lat // tq) & (col < n_lat), -jnp.inf, 0.0).astype(F32)


def attn_fwd(p, z, n_lat):
    t = p.shape[0]
    tq = ROW_TILE

    def body(q_ref, k_ref, v_ref, o_ref, lse_ref):
        mask = _att_mask(pl.program_id(1), t, tq, n_lat)
        k, v = k_ref[...].astype(BF16), v_ref[...].astype(BF16)
        for g in range(ATT_GROUP):
            cols = slice(g * HEAD_DIM, (g + 1) * HEAD_DIM)
            s = _dg(q_ref[:, cols], k, _NT) * ATT_SCALE + mask
            m = jnp.max(s, axis=1, keepdims=True)
            pr = jnp.exp(s - m)
            l = jnp.sum(pr, axis=1, keepdims=True)
            o_ref[:, cols] = _dg(pr, v, _NN) / l
            lse_ref[g] = m + jnp.log(l)

    q_spec, k_spec, v_spec, row_spec = _att_specs(t, tq)
    return pl.pallas_call(
        body, name="attn_fwd", grid=(ATT_KV_HEADS, t // tq),
        in_specs=[q_spec, k_spec, v_spec], out_specs=[q_spec, row_spec],
        out_shape=[jax.ShapeDtypeStruct((t, ATT_Q_HEADS * HEAD_DIM), F32),
                   jax.ShapeDtypeStruct((ATT_Q_HEADS, t, 1), F32)],
        compiler_params=_params(("parallel", "parallel")),
    )(p, p, z)


def attn_bwd(p, z, o, lse, do, n_lat):
    t = p.shape[0]
    tq = ROW_TILE

    def body(q_ref, k_ref, v_ref, o_ref, do_ref, lse_ref, dq_ref, dk_ref, dv_ref):
        i = pl.program_id(1)

        @pl.when(i == 0)
        def _():
            dk_ref[...] = jnp.zeros_like(dk_ref)
            dv_ref[...] = jnp.zeros_like(dv_ref)

        mask = _att_mask(i, t, tq, n_lat)
        k, v = k_ref[...].astype(BF16), v_ref[...].astype(BF16)
        dk, dv = dk_ref[...], dv_ref[...]
        for g in range(ATT_GROUP):
            cols = slice(g * HEAD_DIM, (g + 1) * HEAD_DIM)
            q, do_g = q_ref[:, cols].astype(BF16), do_ref[:, cols]
            pr = jnp.exp(_dg(q, k, _NT) * ATT_SCALE + mask - lse_ref[g])
            delta = jnp.sum(o_ref[:, cols] * do_g, axis=1, keepdims=True)
            ds = pr * (_dg(do_g, v, _NT) - delta) * ATT_SCALE
            dq_ref[:, cols] = _dg(ds, k, _NN)
            dk = dk + _dg(ds, q, _TN)
            dv = dv + _dg(pr, do_g, _TN)
        dk_ref[...], dv_ref[...] = dk, dv

    q_spec, k_spec, v_spec, row_spec = _att_specs(t, tq)
    kv_out = pl.BlockSpec((t, HEAD_DIM), lambda kv, i: (0, kv))
    return pl.pallas_call(
        body, name="attn_bwd", grid=(ATT_KV_HEADS, t // tq),
        in_specs=[q_spec, k_spec, v_spec, q_spec, q_spec, row_spec],
        out_specs=[q_spec, kv_out, kv_out],
        out_shape=[jax.ShapeDtypeStruct((t, ATT_Q_HEADS * HEAD_DIM), F32),
                   jax.ShapeDtypeStruct((t, ATT_KV_HEADS * HEAD_DIM), F32),
                   jax.ShapeDtypeStruct((t, ATT_KV_HEADS * HEAD_DIM), F32)],
        compiler_params=_params(("parallel", "arbitrary")),
    )(p, p, z, o, do, lse)


_RQ_BLK = P_RQ // HEAD_DIM
_RK_BLK = P_RK // HEAD_DIM
_RV_BLK = Z_RV // HEAD_DIM


def _scan_chunk(direction, step, n_chunks, n_lat_chunks):
    return jnp.where(direction == 0, (step + n_lat_chunks) % n_chunks, n_chunks - 1 - step)


def _ret_geometry(direction):
    c = RET_CHUNK
    i = lax.broadcasted_iota(jnp.int32, (c, c), 0)
    j = lax.broadcasted_iota(jnp.int32, (c, c), 1)
    rel = jnp.where(direction == 0, i - j, j - i).astype(F32)
    r = lax.broadcasted_iota(jnp.int32, (c, 1), 0)
    pos = jnp.where(direction == 0, r, c - 1 - r).astype(F32)
    return rel, pos


def ret_chunk(q, k, v, s, lg, rel, pos):
    c = RET_CHUNK
    causal = rel >= 0
    d_in = jnp.where(causal, jnp.exp(lg * jnp.where(causal, rel, 0.0)), 0.0)
    q_dec = jnp.exp(lg * (pos + 1.0))
    k_dec = jnp.exp(lg * (c - 1.0 - pos))
    c_dec = jnp.exp(lg * c)
    att = bdot_nt(q, k) * d_in
    o = bdot(att, v) + bdot(q * q_dec, s)
    s_new = c_dec * s + bdot_tn(k * k_dec, v)
    return o, s_new


def ret_fwd(p, z, lg, n_lat):
    t = p.shape[0]
    c = RET_CHUNK
    nc, nlc = t // c, n_lat // c

    def body(q_ref, k_ref, v_ref, lg_ref, o_ref, ssave_ref, s_s):
        d, n = pl.program_id(0), pl.program_id(1)

        @pl.when(n == 0)
        def _():
            s_s[...] = jnp.zeros_like(s_s)

        rel, pos = _ret_geometry(d)
        for h in range(RET_HEADS):
            cols = slice(h * HEAD_DIM, (h + 1) * HEAD_DIM)
            ssave_ref[0, h, 0] = s_s[h]
            o, s_new = ret_chunk(q_ref[:, cols], k_ref[:, cols], v_ref[:, cols], s_s[h], lg_ref[0, h], rel, pos)
            o_ref[:, cols] = o
            s_s[h] = s_new

    w = RET_HEADS * HEAD_DIM

    def blk(base):
        return pl.BlockSpec((c, w), lambda d, n: (_scan_chunk(d, n, nc, nlc), base // RET_HEADS))

    return pl.pallas_call(
        body, name="ret_fwd", grid=(2, nc),
        in_specs=[blk(_RQ_BLK), blk(_RK_BLK), blk(_RV_BLK), pl.BlockSpec((1, RET_HEADS, 1, 1), lambda d, n: (d, 0, 0, 0))],
        out_specs=[pl.BlockSpec((c, w), lambda d, n: (_scan_chunk(d, n, nc, nlc), d)),
                   pl.BlockSpec((1, RET_HEADS, 1, HEAD_DIM, HEAD_DIM), lambda d, n: (d, 0, n, 0, 0))],
        out_shape=[jax.ShapeDtypeStruct((t, 2 * w), F32),
                   jax.ShapeDtypeStruct((2, RET_HEADS, nc, HEAD_DIM, HEAD_DIM), F32)],
        scratch_shapes=[pltpu.VMEM((RET_HEADS, HEAD_DIM, HEAD_DIM), F32)],
        compiler_params=_params(("parallel", "arbitrary")),
    )(p, p, z, lg)


def ret_bwd(p, z, lg, states, do, n_lat):
    t = p.shape[0]
    c = RET_CHUNK
    nc, nlc = t // c, n_lat // c

    def body(q_ref, k_ref, v_ref, lg_ref, s_ref, do_ref, dq_ref, dk_ref, dv_ref, dlg_ref, ds_s):
        d, n = pl.program_id(0), pl.program_id(1)

        @pl.when(n == 0)
        def _():
            ds_s[...] = jnp.zeros_like(ds_s)
            dlg_ref[...] = jnp.zeros_like(dlg_ref)

        rel, pos = _ret_geometry(d)
        f = functools.partial(ret_chunk, rel=rel, pos=pos)
        for h in range(RET_HEADS):
            cols = slice(h * HEAD_DIM, (h + 1) * HEAD_DIM)
            _, vjp = jax.vjp(f, q_ref[:, cols], k_ref[:, cols], v_ref[:, cols], s_ref[0, h, 0], lg_ref[0, h])
            dq, dk, dv, ds, dlg = vjp((do_ref[:, cols], ds_s[h]))
            dq_ref[:, cols], dk_ref[:, cols], dv_ref[:, cols] = dq, dk, dv
            ds_s[h] = ds
            dlg_ref[0, h] += dlg

    def chunk_of(d, n):
        return _scan_chunk(d, nc - 1 - n, nc, nlc)

    w = RET_HEADS * HEAD_DIM

    def blk(base):
        return pl.BlockSpec((c, w), lambda d, n: (chunk_of(d, n), base // RET_HEADS))

    out_blk = pl.BlockSpec((c, w), lambda d, n: (chunk_of(d, n), d))
    lg_blk = pl.BlockSpec((1, RET_HEADS, 1, 1), lambda d, n: (d, 0, 0, 0))
    grad_shape = jax.ShapeDtypeStruct((t, 2 * w), F32)
    return pl.pallas_call(
        body, name="ret_bwd", grid=(2, nc),
        in_specs=[blk(_RQ_BLK), blk(_RK_BLK), blk(_RV_BLK), lg_blk,
                  pl.BlockSpec((1, RET_HEADS, 1, HEAD_DIM, HEAD_DIM), lambda d, n: (d, 0, nc - 1 - n, 0, 0)),
                  pl.BlockSpec((c, w), lambda d, n: (chunk_of(d, n), 0))],
        out_specs=[out_blk, out_blk, out_blk, lg_blk],
        out_shape=[grad_shape, grad_shape, grad_shape, jax.ShapeDtypeStruct((2, RET_HEADS, 1, 1), F32)],
        scratch_shapes=[pltpu.VMEM((RET_HEADS, HEAD_DIM, HEAD_DIM), F32)],
        compiler_params=_params(("parallel", "arbitrary")),
    )(p, p, z, lg, states, do)


_GQ_BLK = P_GQ // (GLA_HEADS * GLA_DK)
_GK_BLK = Z_GK // (GLA_HEADS * GLA_DK)
_GV_BLK = Z_GV // (GLA_HEADS * GLA_DV)
_LA_BLK = P_LA // (GLA_HEADS * GLA_DK)


def _gla_mask(direction):
    c = GLA_CHUNK
    i = lax.broadcasted_iota(jnp.int32, (c, c), 0)
    j = lax.broadcasted_iota(jnp.int32, (c, c), 1)
    return (jnp.where(direction == 0, i - j, j - i) >= 0).astype(F32)


def gla_chunk(q, k, v, la, st, mask):
    b = mask_cumsum(mask, la)
    btot = jnp.sum(la, axis=0, keepdims=True)
    half = 0.5 * btot
    qt, kt = q * jnp.exp(b - half), k * jnp.exp(half - b)
    qs, ke = q * jnp.exp(b), k * jnp.exp(btot - b)
    outs, upd = [], []
    for h in range(GLA_HEADS):
        ks = slice(h * GLA_DK, (h + 1) * GLA_DK)
        vh = v[:, h * GLA_DV:(h + 1) * GLA_DV]
        att = bdot_nt(qt[:, ks], kt[:, ks]) * mask
        outs.append(bdot(att, vh) + bdot_nt(qs[:, ks], st[:, ks]))
        upd.append(bdot_tn(vh, ke[:, ks]))
    st_new = st * jnp.exp(btot) + jnp.concatenate(upd, axis=1)
    return jnp.concatenate(outs, axis=1), st_new


def gla_fwd(p, z, n_lat):
    t = p.shape[0]
    c = GLA_CHUNK
    nc, nlc = t // c, n_lat // c
    kw, vw = GLA_HEADS * GLA_DK, GLA_HEADS * GLA_DV

    def body(q_ref, k_ref, v_ref, la_ref, o_ref, ssave_ref, s_s):
        d, n = pl.program_id(0), pl.program_id(1)

        @pl.when(n == 0)
        def _():
            s_s[...] = jnp.zeros_like(s_s)

        ssave_ref[0, 0] = s_s[...]
        o, s_new = gla_chunk(q_ref[...], k_ref[...], v_ref[...], la_ref[...], s_s[...], _gla_mask(d))
        o_ref[...] = o
        s_s[...] = s_new

    def chunk_of(d, n):
        return _scan_chunk(d, n, nc, nlc)

    return pl.pallas_call(
        body, name="gla_fwd", grid=(2, nc),
        in_specs=[pl.BlockSpec((c, kw), lambda d, n: (chunk_of(d, n), _GQ_BLK)),
                  pl.BlockSpec((c, kw), lambda d, n: (chunk_of(d, n), _GK_BLK)),
                  pl.BlockSpec((c, vw), lambda d, n: (chunk_of(d, n), _GV_BLK)),
                  pl.BlockSpec((c, kw), lambda d, n: (chunk_of(d, n), _LA_BLK + d))],
        out_specs=[pl.BlockSpec((c, vw), lambda d, n: (chunk_of(d, n), d)),
                   pl.BlockSpec((1, 1, GLA_DV, kw), lambda d, n: (d, n, 0, 0))],
        out_shape=[jax.ShapeDtypeStruct((t, 2 * vw), F32), jax.ShapeDtypeStruct((2, nc, GLA_DV, kw), F32)],
        scratch_shapes=[pltpu.VMEM((GLA_DV, kw), F32)],
        compiler_params=_params(("parallel", "arbitrary")),
    )(p, z, z, p)


def gla_bwd(p, z, states, do, n_lat):
    t = p.shape[0]
    c = GLA_CHUNK
    nc, nlc = t // c, n_lat // c
    kw, vw = GLA_HEADS * GLA_DK, GLA_HEADS * GLA_DV

    def body(q_ref, k_ref, v_ref, la_ref, s_ref, do_ref, dq_ref, dk_ref, dv_ref, dla_ref, ds_s):
        d, n = pl.program_id(0), pl.program_id(1)

        @pl.when(n == 0)
        def _():
            ds_s[...] = jnp.zeros_like(ds_s)

        f = functools.partial(gla_chunk, mask=_gla_mask(d))
        _, vjp = jax.vjp(f, q_ref[...], k_ref[...], v_ref[...], la_ref[...], s_ref[0, 0])
        dq_ref[...], dk_ref[...], dv_ref[...], dla_ref[...], ds_s[...] = vjp((do_ref[...], ds_s[...]))

    def chunk_of(d, n):
        return _scan_chunk(d, nc - 1 - n, nc, nlc)

    k_out = pl.BlockSpec((c, kw), lambda d, n: (chunk_of(d, n), d))
    return pl.pallas_call(
        body, name="gla_bwd", grid=(2, nc),
        in_specs=[pl.BlockSpec((c, kw), lambda d, n: (chunk_of(d, n), _GQ_BLK)),
                  pl.BlockSpec((c, kw), lambda d, n: (chunk_of(d, n), _GK_BLK)),
                  pl.BlockSpec((c, vw), lambda d, n: (chunk_of(d, n), _GV_BLK)),
                  pl.BlockSpec((c, kw), lambda d, n: (chunk_of(d, n), _LA_BLK + d)),
                  pl.BlockSpec((1, 1, GLA_DV, kw), lambda d, n: (d, nc - 1 - n, 0, 0)),
                  pl.BlockSpec((c, vw), lambda d, n: (chunk_of(d, n), 0))],
        out_specs=[k_out, k_out, pl.BlockSpec((c, vw), lambda d, n: (chunk_of(d, n), d)), k_out],
        out_shape=[jax.ShapeDtypeStruct((t, 2 * kw), F32), jax.ShapeDtypeStruct((t, 2 * kw), F32),
                   jax.ShapeDtypeStruct((t, 2 * vw), F32), jax.ShapeDtypeStruct((t, 2 * kw), F32)],
        scratch_shapes=[pltpu.VMEM((GLA_DV, kw), F32)],
        compiler_params=_params(("parallel", "arbitrary")),
    )(p, z, z, p, states, do)


def _adam_tile(w, g, m, v):
    m = ADAM_B1 * m + (1.0 - ADAM_B1) * g
    v = ADAM_B2 * v + (1.0 - ADAM_B2) * (g * g)
    m_hat = m / (1.0 - ADAM_B1 ** ADAM_STEP)
    v_hat = v / (1.0 - ADAM_B2 ** ADAM_STEP)
    delta = -ADAM_LR * (m_hat / (jnp.sqrt(v_hat) + ADAM_EPS) + ADAM_WD * w)
    return delta, m, v


def adamw(name, w, g, m, v):
    shape = w.shape
    cols = shape[-1] if w.ndim > 1 and shape[-1] >= LANES else int(np.prod(shape))
    rows = int(np.prod(shape)) // cols
    tr = rows
    for cand in (512, 256, 128, 64, 32, 16, 8):
        if rows % cand == 0 and cand * cols * 4 <= (1 << 20):
            tr = cand
            break
    flat = [a.reshape(rows, cols) for a in (w, g, m, v)]

    def body(w_ref, g_ref, m_ref, v_ref, d_ref, mo_ref, vo_ref):
        d_ref[...], mo_ref[...], vo_ref[...] = _adam_tile(w_ref[...], g_ref[...], m_ref[...], v_ref[...])

    spec = pl.BlockSpec((tr, cols), lambda i: (i, 0))
    outs = pl.pallas_call(
        body, name=name, grid=(rows // tr,),
        in_specs=[spec] * 4, out_specs=[spec] * 3,
        out_shape=[jax.ShapeDtypeStruct((rows, cols), F32)] * 3,
        compiler_params=_params(("parallel",)),
    )(*flat)
    return tuple(o.reshape(shape) for o in outs)


def adamw_layers(name, w, grads, m, v):
    depth, rows, cols = w.shape
    tr = _rows_tile(rows, cols)
    nb = rows // tr

    def body(w_ref, m_ref, v_ref, *rest):
        g_refs, (g_ref, d_ref, mo_ref, vo_ref) = rest[:depth], rest[depth:]
        l = pl.program_id(0)
        for k in range(depth):
            @pl.when(l == k)
            def _():
                g = g_refs[k][...]
                g_ref[...] = g
                d_ref[...], mo_ref[...], vo_ref[...] = _adam_tile(w_ref[...], g, m_ref[...], v_ref[...])

    def layer_grad(k):
        return pl.BlockSpec((tr, cols), lambda l, i: (jnp.where(l < k, 0, jnp.where(l == k, i, nb - 1)), 0))

    spec = pl.BlockSpec((None, tr, cols), lambda l, i: (l, i, 0))
    g_all, delta, new_m, new_v = pl.pallas_call(
        body, name=name, grid=(depth, nb),
        in_specs=[spec] * 3 + [layer_grad(k) for k in range(depth)], out_specs=[spec] * 4,
        out_shape=[jax.ShapeDtypeStruct(w.shape, F32)] * 4,
        compiler_params=_params(("arbitrary", "arbitrary")),
    )(w, m, v, *grads)
    return (delta, new_m, new_v), g_all


MESH = pl.DeviceIdType.MESH
_HBM = pl.BlockSpec(memory_space=pltpu.HBM)
N_CHIPS = 4
N_DEV = 8


def _place():
    x, y, c = lax.axis_index("x"), lax.axis_index("y"), lax.axis_index("c")
    chips = [(1 - x, y), (x, 1 - y), (1 - x, 1 - y)]
    return x, y, c, chips


def _remote(src, dst, send_sem, recv_sem, to):
    return pltpu.make_async_remote_copy(src_ref=src, dst_ref=dst, send_sem=send_sem, recv_sem=recv_sem,
                                        device_id=to, device_id_type=MESH)


def all_gather_small(name, v):
    m_per, n = v.shape

    def body(x_ref, out_ref, send_sems, recv_sems, local_sem):
        x, y, c, chips = _place()
        me, sibling = (x, y, c), (x, y, 1 - c)

        def rows(px, py, pc):
            return out_ref.at[pl.ds((4 * px + 2 * py + pc) * m_per, m_per), :]

        def copy(k, block, to, src=None):
            return _remote(rows(*block) if src is None else src, rows(*block), send_sems.at[k], recv_sems.at[k], to)

        mine = pltpu.make_async_copy(x_ref, rows(*me), local_sem)
        mine.start()
        first = [copy(0, me, sibling, src=x_ref)]
        first += [copy(1 + j, me, (*chip, c), src=x_ref) for j, chip in enumerate(chips)]
        for cp in first:
            cp.start()
        passed = [copy(4 + j, (*chip, c), sibling) for j, chip in enumerate(chips)]
        for j, chip in enumerate(chips):
            copy(1 + j, (*chip, c), me).wait_recv()
            passed[j].start()
        copy(0, sibling, me).wait_recv()
        for j, chip in enumerate(chips):
            copy(4 + j, (*chip, 1 - c), me).wait_recv()
        for cp in first + passed:
            cp.wait_send()
        mine.wait()

    return pl.pallas_call(
        body, name=name,
        out_shape=jax.ShapeDtypeStruct((N_DEV * m_per, n), v.dtype),
        in_specs=[pl.BlockSpec(memory_space=pltpu.VMEM)],
        out_specs=pl.BlockSpec(memory_space=pltpu.VMEM),
        scratch_shapes=[pltpu.SemaphoreType.DMA((7,)), pltpu.SemaphoreType.DMA((7,)), pltpu.SemaphoreType.DMA],
        compiler_params=pltpu.CompilerParams(vmem_limit_bytes=VMEM_LIMIT),
    )(v)


_SEM = pl.BlockSpec(memory_space=pltpu.SEMAPHORE)
_SPLIT_COPY = pltpu.CompilerParams(has_side_effects=pltpu.SideEffectType.DATAFLOW_SIDE_EFFECTING)


class CopyPlan(NamedTuple):
    copies: object
    n: int
    in_place: bool = False


def _gather_copies(x_ref, land_ref, x, y, c, chips):
    half = x_ref.shape[0] // 2
    rows = pl.ds(c * half, half)
    return [(x_ref.at[rows, :], land_ref.at[2 * x + y, rows, :], (*chip, c), land_ref.at[2 * chip[0] + chip[1], rows, :])
            for chip in chips]


def _pass_copies(land_ref, _, x, y, c, chips):
    half = land_ref.shape[1] // 2
    mine, other = pl.ds(c * half, half), pl.ds((1 - c) * half, half)
    return [(land_ref.at[2 * chip[0] + chip[1], mine, :], land_ref.at[2 * chip[0] + chip[1], mine, :], (x, y, 1 - c),
             land_ref.at[2 * chip[0] + chip[1], other, :]) for chip in chips]


def _sibling_half_copies(p_ref, land_ref, x, y, c, chips):
    half = p_ref.shape[1] // 2
    return [(p_ref.at[:, pl.ds((1 - c) * half, half), :], land_ref, (x, y, 1 - c), land_ref)]


def _scatter_copies(s_ref, land_ref, x, y, c, chips):
    return [(s_ref.at[2 * chip[0] + chip[1]], land_ref.at[j], (*chip, c), land_ref.at[j]) for j, chip in enumerate(chips)]


def _join_copies(buf_ref, _, x, y, c, chips):
    half = buf_ref.shape[0] // 2
    mine = buf_ref.at[pl.ds(c * half, half), :]
    return [(mine, mine, (x, y, 1 - c), buf_ref.at[pl.ds((1 - c) * half, half), :])]


GATHER = CopyPlan(_gather_copies, 3)
PASS_ON = CopyPlan(_pass_copies, 3, in_place=True)
SIBLING_HALF = CopyPlan(_sibling_half_copies, 1)
SCATTER = CopyPlan(_scatter_copies, 3)
JOIN = CopyPlan(_join_copies, 1, in_place=True)


def split_start(name, plan, srcs, land_shapes=None, after=None):
    nt = len(srcs)
    arrays = [pltpu.with_memory_space_constraint(s, pltpu.HBM) for s in srcs]
    if not plan.in_place:
        arrays += [pltpu.with_memory_space_constraint(lax.empty(shape, s.dtype), pltpu.HBM) for shape, s in zip(land_shapes, srcs)]
    na = len(arrays)
    behind = [] if after is None else [after]
    n_in = na + len(behind)

    def body(*refs):
        x_refs = refs[:nt]
        land_refs = x_refs if plan.in_place else refs[nt:na]
        send, recv = refs[n_in:n_in + nt], refs[n_in + nt:n_in + 2 * nt]
        x, y, c, chips = _place()
        for t in range(nt):
            for j, (src, dst, to, _) in enumerate(plan.copies(x_refs[t], land_refs[t], x, y, c, chips)):
                _remote(src, dst, send[t].at[j], recv[t].at[j], to).start()
        refs[-1][...] = jnp.zeros_like(refs[-1])

    outs = pl.pallas_call(
        body, name=name,
        out_shape=tuple([pltpu.SemaphoreType.DMA((plan.n,))] * (2 * nt) + [pltpu.HBM(a.shape, a.dtype) for a in arrays]
                        + [jax.ShapeDtypeStruct((8, LANES), F32)]),
        in_specs=[_HBM] * na + [pl.BlockSpec(memory_space=pl.ANY)] * len(behind),
        out_specs=tuple([_SEM] * (2 * nt) + [_HBM] * na + [pl.BlockSpec(memory_space=pltpu.VMEM)]),
        input_output_aliases={i: 2 * nt + i for i in range(na)},
        compiler_params=_SPLIT_COPY,
    )(*arrays, *behind)
    groups = [(outs[t], outs[nt + t]) + tuple(outs[2 * nt + t + k * nt] for k in range(na // nt)) for t in range(nt)]
    return groups, outs[-1]


def split_wait(name, plan, group, after):
    send, recv, *arrays = group
    na = len(arrays)

    def body(*refs):
        x_ref, land_ref = refs[0], refs[na - 1]
        send_sem, recv_sem = refs[na], refs[na + 1]
        x, y, c, chips = _place()
        for j, (s, _, to, arrival) in enumerate(plan.copies(x_ref, land_ref, x, y, c, chips)):
            cp = _remote(s, arrival, send_sem.at[j], recv_sem.at[j], to)
            cp.wait_send()
            cp.wait_recv()

    return pl.pallas_call(
        body, name=name,
        out_shape=tuple(pltpu.HBM(a.shape, a.dtype) for a in arrays),
        in_specs=tuple([_HBM] * na + [_SEM, _SEM, pl.BlockSpec(memory_space=pl.ANY)]), out_specs=tuple([_HBM] * na),
        input_output_aliases={i: i for i in range(na)}, compiler_params=_SPLIT_COPY,
    )(*arrays, send, recv, after)


def _rows_tile(rows, cols):
    for cand in (512, 256, 128, 64, 32, 16):
        if rows % cand == 0 and cand * cols * 4 <= (1 << 21):
            return cand
    return rows


def add_sibling_half(name, pieces, from_sibling, core):
    n, h, cols = from_sibling.shape
    tr = _rows_tile(h, cols)
    nb = h // tr

    def body(c_ref, a_ref, b_ref, o_ref):
        o_ref[...] = (a_ref[...].astype(F32) + b_ref[...].astype(F32)).astype(o_ref.dtype)

    blk = pl.BlockSpec((1, tr, cols), lambda q, i, c_ref: (q, i, 0))
    return pl.pallas_call(
        body, name=name,
        grid_spec=pltpu.PrefetchScalarGridSpec(
            num_scalar_prefetch=1, grid=(n, nb),
            in_specs=[pl.BlockSpec((1, tr, cols), lambda q, i, c_ref: (q, c_ref[0] * nb + i, 0)), blk], out_specs=blk),
        out_shape=jax.ShapeDtypeStruct((n, h, cols), BF16),
        compiler_params=_params(("parallel", "parallel")),
    )(core.reshape(1).astype(jnp.int32), pieces, from_sibling)


def add_chip_sums(name, chip_sums, from_chips, chip, core):
    _, h, cols = chip_sums.shape
    tr = _rows_tile(h, cols)
    nb = h // tr

    def body(s_ref, own_ref, r0_ref, r1_ref, r2_ref, o_ref):
        acc = own_ref[0].astype(F32) + r0_ref[0].astype(F32)
        o_ref[...] = acc + r1_ref[0].astype(F32) + r2_ref[0].astype(F32)

    def got(j):
        return pl.BlockSpec((1, tr, cols), lambda i, s_ref: (j, i, 0))

    return pl.pallas_call(
        body, name=name,
        grid_spec=pltpu.PrefetchScalarGridSpec(
            num_scalar_prefetch=1, grid=(nb,),
            in_specs=[pl.BlockSpec((1, tr, cols), lambda i, s_ref: (s_ref[0], i, 0)), got(0), got(1), got(2)],
            out_specs=pl.BlockSpec((tr, cols), lambda i, s_ref: (s_ref[1] * nb + i, 0))),
        out_shape=jax.ShapeDtypeStruct((2 * h, cols), F32),
        compiler_params=_params(("parallel",)),
    )(jnp.stack([chip, core]).astype(jnp.int32), chip_sums, from_chips, from_chips, from_chips)


def sum_device_blocks(name, g):
    n = g.shape[1]

    def body(g_ref, o_ref):
        acc = g_ref[0:8, :]
        for d in range(1, N_DEV):
            acc = acc + g_ref[8 * d:8 * (d + 1), :]
        o_ref[...] = acc

    return pl.pallas_call(body, name=name, out_shape=jax.ShapeDtypeStruct((8, n), F32),
                          compiler_params=pltpu.CompilerParams(vmem_limit_bytes=VMEM_LIMIT))(g)


class LayerWeights(NamedTuple):
    norm1_g: jax.Array
    q_g: jax.Array
    k_g: jax.Array
    lg: jax.Array
    ret_g: jax.Array
    gate_up: jax.Array
    gate_b: jax.Array
    gla_g: jax.Array
    norm2_g: jax.Array
    conv_w: jax.Array
    conv_b: jax.Array


def _mod(mods, k):
    return mods[:, k:k + 1, :]


def out_view(l, tb):
    rows = D_MODEL // N_CHIPS
    if tb:
        return BView(n=D_MODEL, k=D_MODEL, tn=rows, tk=D_MODEL, index_map=lambda i, j, kk: (j, l, kk))
    return BView(n=D_MODEL, k=D_MODEL, tn=1024, tk=rows, index_map=lambda i, j, kk: (kk, l, j))


def down_view(l, f, tb):
    rows = f // N_CHIPS
    if tb:
        return BView(n=f, k=D_MODEL, tn=rows, tk=D_MODEL, index_map=lambda i, j, kk: (j, l, kk))
    return BView(n=D_MODEL, k=f, tn=1024, tk=rows, index_map=lambda i, j, kk: (kk, l, j))


def up_view(l, f, part=None):
    cols = 2 * f // N_CHIPS
    tc = _pick(cols, (1408, 1024, 512, 256))
    nbc = cols // tc
    if part is None:
        return BView(n=2 * f, k=D_MODEL, tn=tc, tk=D_MODEL, index_map=lambda i, j, kk: (j // nbc, l, j % nbc))
    nnb = D_MODEL // 1024
    return BView(n=D_MODEL, k=f, tn=1024, tk=tc, index_map=lambda i, j, kk: (2 * part + kk // nbc, l * nnb + j, kk % nbc))


def up_grad_view(f, part, into):
    cols = f // 2
    tn = _pick(cols, (1408, 1024, 512, 256))
    nbc = cols // tn
    return OView((N_CHIPS, D_MODEL, cols), lambda i, j, kk: (2 * part + j // nbc, i, j % nbc), tn, into)


def ada_view(l, n_ada, tb):
    if tb:
        return BView(n=D_MODEL, k=n_ada, tn=1024, tk=n_ada, index_map=lambda i, j, kk: (l, j, 0))
    return BView(n=n_ada, k=D_MODEL, tn=1024, tk=D_MODEL, index_map=lambda i, j, kk: (l, 0, j))


def _prep_args(z, zg, cos, sin, w):
    rows = [Row(z, Z_AV, 0), Row(z, 512, Z_RQ // 512), Row(z, 512, Z_RK // 512), Row(z, 256, Z_GQ // 256),
            Row(zg, LANES, 0), Row(cos, HEAD_DIM, 0, False), Row(sin, HEAD_DIM, 0, False)]
    return rows, [Par(w.q_g), Par(w.k_g), Par(w.gate_up), Par(w.gate_b)]


def _post_args(o_att, o_ret, o_gla, z, w):
    rows = [Row(o_att, 1024), Row(o_ret, 512, 0), Row(o_ret, 512, 1, False), Row(o_gla, 512, 0), Row(o_gla, 512, 1, False),
            Row(z, 512, Z_RG // 512), Row(z, 512, Z_GR // 512)]
    return rows, [Par(w.ret_g), Par(w.gla_g)]


def layer_fwd(l, xs, mods, w, fetch, cos, sin, n_lat):
    t, d = xs.shape
    tag = f"l{l}_"
    nm1 = [Par(w.norm1_g), Par(_mod(mods, 0), True), Par(_mod(mods, 1), True)]
    (h,) = row_map(tag + "norm1", normmod_tile, [Row(xs, d)], nm1, [(d, BF16)], t, n_lat)
    w_main, w_gate = fetch("w_in", h)
    z = matmul(tag + "in_proj", h, w_main)
    zg = matmul(tag + "gate_proj", h, w_gate)
    rows, pars = _prep_args(z, zg, cos, sin, w)
    (p,) = row_map(tag + "prep", prep_tile, rows, pars, [(P_W, F32)], t, n_lat)
    o_att, lse = attn_fwd(p, z, n_lat)
    o_ret, s_ret = ret_fwd(p, z, w.lg, n_lat)
    o_gla, s_gla = gla_fwd(p, z, n_lat)
    rows, pars = _post_args(o_att, o_ret, o_gla, z, w)
    (m,) = row_map(tag + "post", post_tile, rows, pars, [(d, BF16)], t, n_lat)
    g_out = fetch("w_out", m)
    y = matmul(tag + "out_proj", m, g_out, view=out_view(0, False))
    (x1,) = row_map(tag + "resid1", resid_tile, [Row(xs, d), Row(y, d)], [Par(_mod(mods, 2), True)], [(d, F32)], t, n_lat)
    nm2 = [Par(w.norm2_g), Par(_mod(mods, 3), True), Par(_mod(mods, 4), True)]
    (h2,) = row_map(tag + "norm2", normmod_tile, [Row(x1, d)], nm2, [(d, BF16)], t, n_lat)
    f = w.conv_b.shape[1]
    g_up = fetch("w_up", h2)
    u = matmul(tag + "up_proj", h2, g_up, view=up_view(0, f))
    g = convglu(tag + "convglu", u, w.conv_w, w.conv_b, n_lat)
    g_down = fetch("w_down", g)
    yd = matmul(tag + "down_proj", g, g_down, view=down_view(0, f, False))
    (x2,) = row_map(tag + "resid2", resid_tile, [Row(x1, d), Row(yd, d)], [Par(_mod(mods, 5), True)], [(d, F32)], t, n_lat)
    saved = dict(xs=xs, h=h, z=z, zg=zg, p=p, o_att=o_att, lse=lse, o_ret=o_ret, s_ret=s_ret, o_gla=o_gla, s_gla=s_gla,
                 m=m, y=y, x1=x1, h2=h2, u=u, g=g, yd=yd, w_main=w_main, w_gate=w_gate, g_out=g_out, g_up=g_up, g_down=g_down)
    return x2, saved


def _sum_dirs(a):
    w = a.shape[1] // 2
    return a[:, :w] + a[:, w:]


def layer_bwd(l, dx2, s, mods, w, cos, sin, n_lat, grad_ready):
    t, d = dx2.shape
    tag = f"l{l}_b_"
    dyd, dgate5 = row_vjp(tag + "resid2", resid_tile, [Row(s["x1"], d, 0, False), Row(s["yd"], d)],
                          [Par(_mod(mods, 5), True)], [dx2], t, n_lat, row_grad_dtype=BF16)
    f = w.conv_b.shape[1]
    dg = matmul(tag + "down_dx", dyd, s["g_down"], tb=True, view=down_view(0, f, True))
    dw_down = matmul(tag + "down_dw", s["g"], dyd, ta=True, out_dtype=BF16)
    da, dv, dcw, dcb = convglu_bwd(tag + "convglu", s["u"], w.conv_w, w.conv_b, dg, n_lat)
    dh2 = matmul(tag + "up_dx_gate", da, s["g_up"], tb=True, view=up_view(0, f, 0))
    dh2 = matmul(tag + "up_dx_value", dv, s["g_up"], tb=True, view=up_view(0, f, 1), add=dh2)
    dw_up = matmul(tag + "up_dw_gate", s["h2"], da, ta=True, out_dtype=BF16, o_view=up_grad_view(f, 0, None))
    dw_up = matmul(tag + "up_dw_value", s["h2"], dv, ta=True, out_dtype=BF16, o_view=up_grad_view(f, 1, dw_up))
    started = grad_ready("ffn", dict(w_up=dw_up, w_down=dw_down))
    nm2 = [Par(w.norm2_g), Par(_mod(mods, 3), True), Par(_mod(mods, 4), True)]
    dx1, dg2, dshift3, dscale4 = row_vjp(tag + "norm2", normmod_tile, [Row(s["x1"], d)], nm2, [dh2], t, n_lat,
                                         add_to_first=dx2, after=started)
    dy, dgate2 = row_vjp(tag + "resid1", resid_tile, [Row(s["xs"], d, 0, False), Row(s["y"], d)],
                         [Par(_mod(mods, 2), True)], [dx1], t, n_lat, row_grad_dtype=BF16)
    dm = matmul(tag + "out_dx", dy, s["g_out"], tb=True, view=out_view(0, True))
    dw_out = matmul(tag + "out_dw", s["m"], dy, ta=True, out_dtype=BF16)
    rows, pars = _post_args(s["o_att"], s["o_ret"], s["o_gla"], s["z"], w)
    started = grad_ready("w_out", dict(w_out=dw_out))
    do_att, do_ret, do_gla, d_rg, d_gr, d_ret_g, d_gla_g = row_vjp(tag + "post", post_tile, rows, pars, [dm], t, n_lat, after=started)
    dq_a, dk_a, dv_a = attn_bwd(s["p"], s["z"], s["o_att"], s["lse"], do_att, n_lat)
    dq_r, dk_r, dv_r, dlg = ret_bwd(s["p"], s["z"], w.lg, s["s_ret"], do_ret, n_lat)
    dq_g, dk_g, dv_g, dla = gla_bwd(s["p"], s["z"], s["s_gla"], do_gla, n_lat)
    dp = jnp.concatenate([dq_a, dk_a, _sum_dirs(dq_g), _sum_dirs(dq_r), _sum_dirs(dk_r), dla], axis=1)
    rows, pars = _prep_args(s["z"], s["zg"], cos, sin, w)
    d_zqk, d_zrq, d_zrk, d_zgq, dzg, d_qg, d_kg, d_up, d_gb = row_vjp(tag + "prep", prep_tile, rows, pars, [dp], t, n_lat)
    dz = jnp.concatenate([d_zqk, dv_a, d_zrq, d_zrk, _sum_dirs(dv_r), d_rg, d_zgq, _sum_dirs(dk_g), _sum_dirs(dv_g), d_gr], axis=1)
    dz, dzg = dz.astype(BF16), dzg.astype(BF16)
    dh_gate = matmul(tag + "gate_dx", dzg, s["w_gate"], tb=True)
    dh = matmul(tag + "in_dx", dz, s["w_main"], tb=True, add=dh_gate)
    dw_main = matmul(tag + "in_dw", s["h"], dz, ta=True, out_dtype=BF16)
    dw_gate = matmul(tag + "gate_dw", s["h"], dzg, ta=True, out_dtype=BF16)
    started = grad_ready("w_in", dict(w_main=dw_main, w_gate=dw_gate))
    nm1 = [Par(w.norm1_g), Par(_mod(mods, 0), True), Par(_mod(mods, 1), True)]
    dx, dg1, dshift0, dscale1 = row_vjp(tag + "norm1", normmod_tile, [Row(s["xs"], d)], nm1, [dh], t, n_lat,
                                        add_to_first=dx1, after=started)
    dmods = jnp.concatenate([dshift0, dscale1, dgate2, dshift3, dscale4, dgate5], axis=1)
    grads = dict(w_main=dw_main, w_gate=dw_gate, w_out=dw_out, w_up=dw_up, w_down=dw_down, norm1_g=dg1, q_g=d_qg, k_g=d_kg,
                 lg=dlg, ret_g=d_ret_g, gate_up=d_up, gate_b=d_gb, gla_g=d_gla_g, norm2_g=dg2, conv_w=dcw, conv_b=dcb)
    return dx, dmods, grads


def rope_tables(n_lat, n_ctx):
    rows = n_lat // GRID_W
    row = jnp.repeat(jnp.arange(rows, dtype=F32), GRID_W)
    col = jnp.tile(jnp.arange(GRID_W, dtype=F32), rows)
    n_freq = HEAD_DIM // 4
    inv_freq = ROPE_THETA ** (-jnp.arange(n_freq, dtype=F32) / n_freq)
    ang = jnp.concatenate([row[:, None] * inv_freq, col[:, None] * inv_freq], axis=-1)
    cos, sin = jnp.cos(ang), jnp.sin(ang)
    cos = jnp.concatenate([jnp.concatenate([cos, cos], axis=1), jnp.ones((n_ctx, HEAD_DIM), F32)], axis=0)
    sin = jnp.concatenate([jnp.concatenate([-sin, sin], axis=1), jnp.zeros((n_ctx, HEAD_DIM), F32)], axis=0)
    return cos, sin


def local_step(xs, target, mods, weights, fetch, final_g, n_lat, grad_ready):
    t, d = xs.shape
    cos, sin = rope_tables(n_lat, t - n_lat)
    saved = []
    h = xs
    for l, w in enumerate(weights):
        h, s = layer_fwd(l, h, mods[l], w, functools.partial(fetch, l), cos, sin, n_lat)
        saved.append(s)
    loss, dlat, dgf = final_loss(h, target, final_g, n_lat)
    dx = jnp.concatenate([dlat, jnp.zeros((t - n_lat, d), F32)], axis=0)
    dmods, grads = [None] * len(weights), [None] * len(weights)
    for l in reversed(range(len(weights))):
        dx, dmods[l], grads[l] = layer_bwd(l, dx, saved[l], mods[l], weights[l], cos, sin, n_lat, functools.partial(grad_ready, l))
    return loss, dx, dmods, grads, dgf


WEIGHT_NAMES = ("c_ctx", "ada_w", "ada_b", "norm1_g", "w_in", "q_norm_g", "k_norm_g", "ret_log_decay", "ret_norm_g",
                "gla_gate_up", "gla_gate_b", "gla_norm_g", "w_out", "norm2_g", "w_up", "conv_w", "conv_b", "w_down", "final_norm_g")
PACK_QUANTUM = 8 * LANES


def _pack(arrays):
    flat = jnp.concatenate([a.reshape(-1).astype(F32) for a in arrays])
    n = -(-flat.shape[0] // PACK_QUANTUM) * PACK_QUANTUM
    return jnp.pad(flat, (0, n - flat.shape[0])).reshape(8, n // 8)


def _unpack(flat2d, shapes):
    out, at = [], 0
    for s in shapes:
        size = int(np.prod(s))
        out.append(flat2d[:, at:at + size].reshape((flat2d.shape[0],) + tuple(s)))
        at += size
    return out


def _per_device(gathered):
    return gathered.reshape(N_DEV, -1)


def _from_chips(per_device, axis):
    chips = per_device[0::2]
    moved = jnp.moveaxis(chips, 0, axis)
    shape = moved.shape
    return moved.reshape(shape[:axis] + (shape[axis] * shape[axis + 1],) + shape[axis + 2:])


def kernel(x, c, ctx, c_ctx, ada_w, ada_b, norm1_g, w_in, q_norm_g, k_norm_g, ret_log_decay, ret_norm_g, gla_gate_up, gla_gate_b, gla_norm_g, w_out, norm2_g, w_up, conv_w, conv_b, w_down, final_norm_g, loss_target, m_c_ctx, m_ada_w, m_ada_b, m_norm1_g, m_w_in, m_q_norm_g, m_k_norm_g, m_ret_log_decay, m_ret_norm_g, m_gla_gate_up, m_gla_gate_b, m_gla_norm_g, m_w_out, m_norm2_g, m_w_up, m_conv_w, m_conv_b, m_w_down, m_final_norm_g, v_c_ctx, v_ada_w, v_ada_b, v_norm1_g, v_w_in, v_q_norm_g, v_k_norm_g, v_ret_log_decay, v_ret_norm_g, v_gla_gate_up, v_gla_gate_b, v_gla_norm_g, v_w_out, v_norm2_g, v_w_up, v_conv_w, v_conv_b, v_w_down, v_final_norm_g):
    weights = dict(zip(WEIGHT_NAMES, (c_ctx, ada_w, ada_b, norm1_g, w_in, q_norm_g, k_norm_g, ret_log_decay, ret_norm_g,
                                      gla_gate_up, gla_gate_b, gla_norm_g, w_out, norm2_g, w_up, conv_w, conv_b, w_down, final_norm_g)))
    mom_m = dict(zip(WEIGHT_NAMES, (m_c_ctx, m_ada_w, m_ada_b, m_norm1_g, m_w_in, m_q_norm_g, m_k_norm_g, m_ret_log_decay, m_ret_norm_g,
                                    m_gla_gate_up, m_gla_gate_b, m_gla_norm_g, m_w_out, m_norm2_g, m_w_up, m_conv_w, m_conv_b, m_w_down, m_final_norm_g)))
    mom_v = dict(zip(WEIGHT_NAMES, (v_c_ctx, v_ada_w, v_ada_b, v_norm1_g, v_w_in, v_q_norm_g, v_k_norm_g, v_ret_log_decay, v_ret_norm_g,
                                    v_gla_gate_up, v_gla_gate_b, v_gla_norm_g, v_w_out, v_norm2_g, v_w_up, v_conv_w, v_conv_b, v_w_down, v_final_norm_g)))
    depth, d = norm1_g.shape
    assert d == D_MODEL and x.shape[0] == 1
    n_lat, n_ctx, f = x.shape[1], ctx.shape[1], conv_b.shape[1]
    assert n_lat % ROW_TILE == 0 and n_ctx % ROW_TILE == 0 and f % FFN_COL_TILE == 0 and f % N_CHIPS == 0
    n_in = w_in.shape[2]
    n_ada = ada_w.shape[2]
    xi, yi, ci = lax.axis_index("x"), lax.axis_index("y"), lax.axis_index("c")
    chip = 2 * xi + yi
    dev = 2 * chip + ci

    big = ("w_in", "w_out", "w_up", "w_down")
    order = [(l, name) for l in range(depth) for name in big]
    shards = [weights[name][l].astype(BF16) for l, name in order]
    passing = {}

    def pass_on(k, after):
        tag = "{1}{0}".format(*order[k])
        own, land = split_wait("gather_wait_" + tag, GATHER, in_flight[k], after)
        (moving,), _ = split_start("gather_pass_" + tag, PASS_ON, [land])
        passing[k] = (own, moving)

    def fetch(l, name, after):
        k = order.index((l, name))
        if k == 0:
            pass_on(0, after)
        own, moving = passing.pop(k)
        (land,) = split_wait(f"gather_pass_wait_{name}{l}", PASS_ON, moving, after)
        if k + 1 < len(order):
            pass_on(k + 1, after)
        land = lax.dynamic_update_slice_in_dim(land, own[None], chip, axis=0)
        if name != "w_in":
            return land
        cols = jnp.concatenate([land[q] for q in range(N_CHIPS)], axis=1)
        return cols[:, :N_MAIN], jnp.pad(cols[:, N_MAIN:], ((0, 0), (0, LANES - N_GATE)))

    small_shapes = [c.shape[1:], conv_w.shape, gla_gate_up.shape, gla_gate_b.shape]
    got = _per_device(all_gather_small("gather_small", _pack([c, conv_w, gla_gate_up, gla_gate_b])))
    c_all, conv_w_sh, gate_up_sh, gate_b_sh = _unpack(got, small_shapes)
    conv_w_full = _from_chips(conv_w_sh, 2)
    gate_up_full = _from_chips(gate_up_sh, 3)
    gate_b_full = _from_chips(gate_b_sh, 2)

    act = jnp.zeros((16, d), F32).at[0:N_DEV].set(jax.nn.silu(c_all)).at[N_DEV].set(jax.nn.silu(c_ctx))
    mod_sh = jnp.stack([matmul(f"ada_fwd{l}", act, ada_w, view=ada_view(l, n_ada, False)) for l in range(depth)])
    got = _per_device(all_gather_small("gather_mods", _pack([mod_sh])))
    (mod_sh_all,) = _unpack(got, [mod_sh.shape])
    mod_full = _from_chips(mod_sh_all, 2) + ada_b[:, None, :]
    mod_mine = lax.dynamic_index_in_dim(mod_full, dev, axis=1, keepdims=False)
    mods = [jnp.stack([mod_mine[l].reshape(N_MOD, d), mod_full[l, N_DEV].reshape(N_MOD, d)]) for l in range(depth)]
    in_flight, token = split_start("gather_start", GATHER, shards, [(N_CHIPS,) + s.shape for s in shards], after=mod_full)

    layer_w = []
    for l in range(depth):
        up = jnp.zeros((2, LANES, GLA_HEADS * GLA_DK), F32)
        up = up.at[0, 0:GLA_RANK].set(gate_up_full[l, 0]).at[1, GLA_RANK:2 * GLA_RANK].set(gate_up_full[l, 1])
        layer_w.append(LayerWeights(
            norm1_g=norm1_g[l].reshape(1, 1, d), q_g=q_norm_g[l].reshape(1, 1, HEAD_DIM), k_g=k_norm_g[l].reshape(1, 1, HEAD_DIM),
            lg=ret_log_decay[l].reshape(2, RET_HEADS, 1, 1), ret_g=ret_norm_g[l].reshape(1, 1, HEAD_DIM),
            gate_up=up.reshape(1, 2 * LANES, -1), gate_b=gate_b_full[l].reshape(1, 2, -1), gla_g=gla_norm_g[l].reshape(1, 1, HEAD_DIM),
            norm2_g=norm2_g[l].reshape(1, 1, d), conv_w=conv_w_full[l], conv_b=conv_b[l].reshape(1, f)))

    def pieces_of(name, g):
        if name == "w_in":
            full_cols = jnp.concatenate([g["w_main"], g["w_gate"][:, :N_GATE]], axis=1)
            return jnp.stack([full_cols[:, q * n_in:(q + 1) * n_in] for q in range(N_CHIPS)])
        if name == "w_up":
            return g["w_up"]
        return g[name].reshape(N_CHIPS, -1, d)

    groups = {"ffn": ("w_up", "w_down"), "w_out": ("w_out",), "w_in": ("w_in",)}
    reducing = {}
    to_sibling = []

    def sibling_arrived(after):
        started = None
        while to_sibling:
            l, group, in_flight_halves = to_sibling.pop(0)
            sums = []
            for name, halves in zip(groups[group], in_flight_halves):
                pieces, from_sibling = split_wait(f"rs_sibling_wait_{name}{l}", SIBLING_HALF, halves, after)
                sums.append(add_sibling_half(f"rs_add_sibling_{name}{l}", pieces, from_sibling, ci))
            in_flight_sums, token = split_start(f"rs_start_{group}{l}", SCATTER, sums, [(3,) + s.shape[1:] for s in sums])
            reducing.update({(l, name): grp for name, grp in zip(groups[group], in_flight_sums)})
            started = token if started is None else started + token
        return started

    def grad_ready(l, group, g):
        pieces = [pieces_of(name, g) for name in groups[group]]
        before = None if (l, group) == (0, "w_in") else sibling_arrived(pieces[0])
        in_flight_halves, started = split_start(f"rs_sibling_{group}{l}", SIBLING_HALF, pieces,
                                                [(N_CHIPS, pc.shape[1] // 2, pc.shape[2]) for pc in pieces])
        to_sibling.append((l, group, in_flight_halves))
        return started if before is None else started + before

    xs = jnp.concatenate([x[0], ctx[0]], axis=0) + token[0, 0]
    loss, dx, dmods, grads, dgf = local_step(xs, loss_target[0], mods, layer_w, fetch, final_norm_g.reshape(1, d), n_lat, grad_ready)

    def gate_up_grad(g):
        return jnp.stack([g[0, 0:GLA_RANK], g[0, LANES + GLA_RANK:LANES + 2 * GLA_RANK]])

    per_layer = [[dmods[l][0], dmods[l][1], grads[l]["norm1_g"], grads[l]["norm2_g"], grads[l]["q_g"], grads[l]["k_g"],
                  grads[l]["ret_g"], grads[l]["gla_g"], grads[l]["lg"], gate_up_grad(grads[l]["gate_up"]), grads[l]["gate_b"],
                  grads[l]["conv_w"], grads[l]["conv_b"]] for l in range(depth)]
    layer_shapes = [(N_MOD * d,), (N_MOD * d,), (d,), (d,), (HEAD_DIM,), (HEAD_DIM,), (HEAD_DIM,), (HEAD_DIM,), (2, RET_HEADS),
                    (2, GLA_RANK, GLA_HEADS * GLA_DK), (2, GLA_HEADS * GLA_DK), (3, f), (f,)]
    packed = _pack([a for lay in per_layer for a in lay] + [dgf, loss[0, 0:1]])
    gathered = all_gather_small("gather_small_grads", packed)
    every = _unpack(_per_device(gathered), layer_shapes * depth + [(d,), (1,)])
    total = _unpack(sum_device_blocks("sum_small_grads", gathered).reshape(1, -1), layer_shapes * depth + [(d,), (1,)])
    nl = len(layer_shapes)

    def tot(l, k):
        return total[l * nl + k][0]

    out = {"norm1_g": jnp.stack([tot(l, 2) for l in range(depth)]), "norm2_g": jnp.stack([tot(l, 3) for l in range(depth)]),
           "q_norm_g": jnp.stack([tot(l, 4) for l in range(depth)]), "k_norm_g": jnp.stack([tot(l, 5) for l in range(depth)]),
           "ret_norm_g": jnp.stack([tot(l, 6) for l in range(depth)]), "gla_norm_g": jnp.stack([tot(l, 7) for l in range(depth)]),
           "ret_log_decay": jnp.stack([tot(l, 8) for l in range(depth)]),
           "gla_gate_up": lax.dynamic_slice_in_dim(jnp.stack([tot(l, 9) for l in range(depth)]), chip * gla_gate_up.shape[3], gla_gate_up.shape[3], axis=3),
           "gla_gate_b": lax.dynamic_slice_in_dim(jnp.stack([tot(l, 10) for l in range(depth)]), chip * gla_gate_b.shape[2], gla_gate_b.shape[2], axis=2),
           "conv_w": lax.dynamic_slice_in_dim(jnp.stack([tot(l, 11) for l in range(depth)]), chip * conv_w.shape[2], conv_w.shape[2], axis=2),
           "conv_b": jnp.stack([tot(l, 12) for l in range(depth)]),
           "final_norm_g": total[depth * nl][0],
           "ada_b": jnp.stack([tot(l, 0) + tot(l, 1) for l in range(depth)])}
    loss_total = total[depth * nl + 1][0, 0]

    dmod_all = jnp.zeros((depth, 16, N_MOD * d), F32)
    for l in range(depth):
        dmod_all = dmod_all.at[l, 0:N_DEV].set(every[l * nl][:, :]).at[l, N_DEV].set(tot(l, 1))
    dmod_cols = lax.dynamic_slice_in_dim(dmod_all, chip * n_ada, n_ada, axis=2)
    for l in range(depth):
        slab = OView((depth, d, n_ada), functools.partial(lambda i, j, kk, l: (l, i, j), l=l), None, out.get("ada_w"))
        out["ada_w"] = matmul(f"ada_dw{l}", act, dmod_cols[l], ta=True, o_view=slab)
    dact = matmul("ada_dx0", dmod_cols[0], ada_w, tb=True, view=ada_view(0, n_ada, True))
    for l in range(1, depth):
        dact = matmul(f"ada_dx{l}", dmod_cols[l], ada_w, tb=True, view=ada_view(l, n_ada, True), add=dact)
    got = _per_device(all_gather_small("gather_dcctx", _pack([dact[N_DEV]])))
    sibling_arrived(got)
    got = got[0::2, :d]
    dsilu = got[0] + got[1] + got[2] + got[3]
    sig = jax.nn.sigmoid(c_ctx)
    out["c_ctx"] = dsilu * (sig + c_ctx * sig * (1.0 - sig))

    deltas, new_m, new_v = {}, {}, {}

    def update(name):
        out[name] = out[name].reshape(weights[name].shape)
        deltas[name], new_m[name], new_v[name] = adamw("adamw_" + name, weights[name], out[name], mom_m[name], mom_v[name])

    for name in WEIGHT_NAMES:
        if name not in big:
            update(name)
    behind = new_v["ada_w"]
    joining = []

    def joined(after):
        name, in_flight_halves = joining.pop()
        per_layer = [split_wait(f"rs_join_wait_{name}{l}", JOIN, grp, after)[0] for l, grp in enumerate(in_flight_halves)]
        (deltas[name], new_m[name], new_v[name]), out[name] = adamw_layers(
            "adamw_" + name, weights[name], per_layer, mom_m[name], mom_v[name])
        return new_v[name]

    for name in ("w_down", "w_up", "w_out", "w_in"):
        halves = []
        for l in range(depth):
            sums, got = split_wait(f"rs_wait_{name}{l}", SCATTER, reducing[(l, name)], behind)
            halves.append(add_chip_sums(f"rs_add_chips_{name}{l}", sums, got, chip, ci))
        in_flight_halves, _ = split_start("rs_join_" + name, JOIN, halves)
        if joining:
            behind = joined(behind)
        joining.append((name, in_flight_halves))
    joined(behind)
    grad_x = dx[:n_lat].reshape(x.shape)
    return (loss_total, grad_x, *[out[n] for n in WEIGHT_NAMES], *[deltas[n] for n in WEIGHT_NAMES],
            *[new_m[n] for n in WEIGHT_NAMES], *[new_v[n] for n in WEIGHT_NAMES])
```

```python
import functools
from typing import NamedTuple

import numpy as np
import jax
import jax.numpy as jnp
from jax import lax
from jax.experimental import pallas as pl
from jax.experimental.pallas import tpu as pltpu

F32 = jnp.float32
BF16 = jnp.bfloat16

D_MODEL = 2048
HEAD_DIM = 128
ATT_Q_HEADS = 8
ATT_KV_HEADS = 2
ATT_GROUP = ATT_Q_HEADS // ATT_KV_HEADS
RET_HEADS = 4
GLA_HEADS = 4
GLA_DK = 64
GLA_DV = 128
GLA_RANK = 16
GLA_TAU = 16.0
RET_CHUNK = 256
GLA_CHUNK = 128
GRID_W = 64
ROPE_THETA = 10000.0
N_MOD = 6
EPS = 1e-6
N_MAIN = 5120
N_GATE = 2 * GLA_RANK
LANES = 128
ROW_TILE = 256
FFN_COL_TILE = 256
VMEM_LIMIT = 56 * 1024 * 1024

ADAM_LR = 0.001
ADAM_B1 = 0.9
ADAM_B2 = 0.999
ADAM_EPS = 1e-08
ADAM_WD = 0.01
ADAM_STEP = 10

Z_AQ, Z_AK, Z_AV = 0, 1024, 1280
Z_RQ, Z_RK, Z_RV, Z_RG = 1536, 2048, 2560, 3072
Z_GQ, Z_GK, Z_GV, Z_GR = 3584, 3840, 4096, 4608
P_AQ, P_AK, P_GQ, P_RQ, P_RK, P_LA = 0, 1024, 1280, 1536, 2048, 2560
P_W = 3072


def _params(sem=None):
    return pltpu.CompilerParams(dimension_semantics=sem, vmem_limit_bytes=VMEM_LIMIT)


def _pick(n, cands):
    for c in cands:
        if n % c == 0:
            return c
    return n


_NN = (((1,), (0,)), ((), ()))
_NT = (((1,), (1,)), ((), ()))
_TN = (((0,), (0,)), ((), ()))


def _dg(a, b, dims):
    return lax.dot_general(a.astype(BF16), b.astype(BF16), dims, preferred_element_type=F32)


@jax.custom_vjp
def bdot(a, b):
    return _dg(a, b, _NN)


def _bdot_fwd(a, b):
    return _dg(a, b, _NN), (a, b)


def _bdot_bwd(res, ct):
    a, b = res
    return _dg(ct, b, _NT), _dg(a, ct, _TN)


bdot.defvjp(_bdot_fwd, _bdot_bwd)


@jax.custom_vjp
def bdot_nt(a, b):
    return _dg(a, b, _NT)


def _bdot_nt_fwd(a, b):
    return _dg(a, b, _NT), (a, b)


def _bdot_nt_bwd(res, ct):
    a, b = res
    return _dg(ct, b, _NN), _dg(ct, a, _TN)


bdot_nt.defvjp(_bdot_nt_fwd, _bdot_nt_bwd)


@jax.custom_vjp
def bdot_tn(a, b):
    return _dg(a, b, _TN)


def _bdot_tn_fwd(a, b):
    return _dg(a, b, _TN), (a, b)


def _bdot_tn_bwd(res, ct):
    a, b = res
    return _dg(b, ct, _NT), _dg(a, ct, _NN)


bdot_tn.defvjp(_bdot_tn_fwd, _bdot_tn_bwd)


def _split3(x):
    x1 = x.astype(BF16)
    r1 = x - x1.astype(F32)
    x2 = r1.astype(BF16)
    x3 = (r1 - x2.astype(F32)).astype(BF16)
    return x1, x2, x3


def _mask_dot(mask_bf16, x, dims):
    x1, x2, x3 = _split3(x)
    f = lambda t: lax.dot_general(mask_bf16, t, dims, preferred_element_type=F32)
    return f(x1) + f(x2) + f(x3)


@jax.custom_vjp
def mask_cumsum(mask, x):
    return _mask_dot(mask.astype(BF16), x, _NN)


def _mask_cumsum_fwd(mask, x):
    return mask_cumsum(mask, x), mask


def _mask_cumsum_bwd(mask, ct):
    return jnp.zeros_like(mask), _mask_dot(mask.astype(BF16), ct, _TN)


mask_cumsum.defvjp(_mask_cumsum_fwd, _mask_cumsum_bwd)


def _roll(x, shift, axis):
    return pltpu.roll(x, shift % x.shape[axis], axis)


@functools.partial(jax.custom_vjp, nondiff_argnums=(1, 2))
def roll(x, shift, axis):
    return _roll(x, shift, axis)


def _roll_fwd(x, shift, axis):
    return _roll(x, shift, axis), None


def _roll_bwd(shift, axis, _, ct):
    return (_roll(ct, -shift, axis),)


roll.defvjp(_roll_fwd, _roll_bwd)


def rms(x):
    return x * lax.rsqrt(jnp.mean(x * x, axis=-1, keepdims=True) + EPS)


def silu(x):
    return x * (1.0 / (1.0 + jnp.exp(-x)))


def log_sigmoid(x):
    return jnp.minimum(x, 0.0) - jnp.log(1.0 + jnp.exp(-jnp.abs(x)))


def rope(t, cos, sin):
    return t * cos + roll(t, HEAD_DIM // 2, 1) * sin


def _heads(x, n, width=HEAD_DIM):
    return [x[:, h * width:(h + 1) * width] for h in range(n)]


class Row(NamedTuple):
    arr: jax.Array
    width: int
    idx: int = 0
    diff: bool = True


class Par(NamedTuple):
    arr: jax.Array
    grouped: bool = False
    diff: bool = True


def _row_specs(rows, pars, tm, n_lat_tiles):
    def grp(i):
        return jnp.minimum(i // n_lat_tiles, 1)

    specs = [pl.BlockSpec((tm, r.width), functools.partial(lambda i, k: (i, k), k=r.idx)) for r in rows]
    for p in pars:
        blk = (1,) + p.arr.shape[1:]
        if p.grouped:
            specs.append(pl.BlockSpec(blk, lambda i: (grp(i), 0, 0)))
        else:
            specs.append(pl.BlockSpec(blk, lambda i: (0, 0, 0)))
    return specs


def row_map(name, fn, rows, pars, outs, n_rows, n_lat):
    tm = ROW_TILE
    nr, npar = len(rows), len(pars)

    def body(*refs):
        vals = [r[...] for r in refs[:nr]] + [p[0] for p in refs[nr:nr + npar]]
        res = fn(*vals)
        for o, v in zip(refs[nr + npar:], res):
            o[...] = v.astype(o.dtype)

    return pl.pallas_call(
        body, name=name, grid=(n_rows // tm,),
        in_specs=_row_specs(rows, pars, tm, n_lat // tm),
        out_specs=[pl.BlockSpec((tm, w), lambda i: (i, 0)) for w, _ in outs],
        out_shape=[jax.ShapeDtypeStruct((n_rows, w), dt) for w, dt in outs],
        compiler_params=_params(("arbitrary",)),
    )(*[r.arr for r in rows], *[p.arr for p in pars])


def row_vjp(name, fn, rows, pars, cts, n_rows, n_lat, add_to_first=None, row_grad_dtype=F32, after=None):
    tm = ROW_TILE
    nr, npar, nc = len(rows), len(pars), len(cts)
    n_lat_tiles = n_lat // tm
    args = list(rows) + list(pars)
    diff_pos = [k for k, a in enumerate(args) if a.diff]
    n_add = 0 if add_to_first is None else 1
    n_after = 0 if after is None else 1

    def body(*refs):
        i = pl.program_id(0)
        vals = [r[...] for r in refs[:nr]] + [p[0] for p in refs[nr:nr + npar]]
        ct_vals = tuple(c[...] for c in refs[nr + npar:nr + npar + nc])
        out_refs = refs[nr + npar + nc + n_add + n_after:]

        def g(*dv):
            full = list(vals)
            for k, v in zip(diff_pos, dv):
                full[k] = v
            return tuple(fn(*full))

        _, vjp = jax.vjp(g, *[vals[k] for k in diff_pos])
        grads = vjp(ct_vals)
        for n, (k, o, gr) in enumerate(zip(diff_pos, out_refs, grads)):
            if k < nr:
                o[...] = (gr + refs[nr + npar + nc][...] if (n == 0 and n_add) else gr).astype(o.dtype)
            else:
                first = (i == 0) | (i == n_lat_tiles) if args[k].grouped else (i == 0)

                @pl.when(first)
                def _():
                    o[0] = gr

                @pl.when(jnp.logical_not(first))
                def _():
                    o[0] += gr

    def grp(i):
        return jnp.minimum(i // n_lat_tiles, 1)

    out_specs, out_shape = [], []
    for k in diff_pos:
        a = args[k]
        if k < nr:
            out_specs.append(pl.BlockSpec((tm, a.width), lambda i: (i, 0)))
            out_shape.append(jax.ShapeDtypeStruct((n_rows, a.width), row_grad_dtype))
        else:
            blk = (1,) + a.arr.shape[1:]
            out_specs.append(pl.BlockSpec(blk, (lambda i: (grp(i), 0, 0)) if a.grouped else (lambda i: (0, 0, 0))))
            out_shape.append(jax.ShapeDtypeStruct(a.arr.shape, F32))
    extra = list(cts) + ([add_to_first] if n_add else [])
    ct_specs = [pl.BlockSpec((tm, c.shape[1]), lambda i: (i, 0)) for c in extra]
    if n_after:
        extra.append(after)
        ct_specs.append(pl.BlockSpec(memory_space=pl.ANY))
    return pl.pallas_call(
        body, name=name, grid=(n_rows // tm,),
        in_specs=_row_specs(rows, pars, tm, n_lat_tiles) + ct_specs,
        out_specs=out_specs, out_shape=out_shape,
        compiler_params=_params(("arbitrary",)),
    )(*[r.arr for r in rows], *[p.arr for p in pars], *extra)


class BView(NamedTuple):
    n: int
    k: int
    tn: int
    tk: int
    index_map: object
    lead: int = 1


MATMUL_VMEM_BUDGET = 40 * 1024 * 1024


def _matmul_tiles(m, n, k, a_bytes, b_bytes, o_bytes):
    tms = [c for c in (1152, 1024, 768, 512, 256, 128) if m % c == 0] or [m]
    tns = [c for c in (2048, 1408, 1280, 1024, 768, 512, 256, 128) if n % c == 0] or [n]
    tks = [k] + [c for c in (2816, 2304, 2048, 1408, 1024, 512, 256, 128) if k % c == 0 and c < k]
    for tk in tks:
        fits = [(tm * tn, tm, tn) for tm in tms for tn in tns
                if 2 * (tm * tk * a_bytes + tk * tn * b_bytes + tm * tn * o_bytes) + 2 * tm * tn * 4 <= MATMUL_VMEM_BUDGET]
        if fits and (max(fits)[0] >= 512 * 512 or tms == [m] or tk == tks[-1]):
            _, tm, tn = max(fits)
            return tm, tn, tk
    raise ValueError(f"no matmul tiling for {(m, n, k)}")


class OView(NamedTuple):
    shape: tuple
    index_map: object
    tn: int = None
    into: object = None


def matmul(name, a, b, *, ta=False, tb=False, add=None, out_dtype=F32, view=None, o_view=None, after=None):
    m = a.shape[1] if ta else a.shape[0]
    o_bytes = jnp.dtype(out_dtype).itemsize * (1 if add is None else 2)
    if view is None:
        k = a.shape[0] if ta else a.shape[1]
        n = b.shape[0] if tb else b.shape[1]
        assert (b.shape[1] if tb else b.shape[0]) == k, (a.shape, b.shape, ta, tb)
        if o_view is not None and o_view.tn is not None:
            tn = o_view.tn
            tm, _, tk = _matmul_tiles(m, tn, k, a.dtype.itemsize, b.dtype.itemsize, o_bytes)
        else:
            tm, tn, tk = _matmul_tiles(m, n, k, a.dtype.itemsize, b.dtype.itemsize, o_bytes)
    else:
        n, k, tn, tk = view.n, view.k, view.tn, view.tk
        tm, _, _ = _matmul_tiles(m, tn, tk, a.dtype.itemsize, b.dtype.itemsize, o_bytes)
    nk = k // tk
    dims = (((0 if ta else 1,), (1 if tb else 0,)), ((), ()))

    def body(a_ref, b_ref, *rest):
        prod = lax.dot_general(a_ref[...].astype(BF16), b_ref[...].astype(BF16), dims, preferred_element_type=F32)
        if nk == 1:
            o_ref = rest[-1]
            o_ref[...] = (prod if add is None else prod + rest[0][...]).astype(o_ref.dtype)
            return
        o_ref, acc = rest[-2:]
        kk = pl.program_id(2)

        @pl.when(kk == 0)
        def _():
            acc[...] = prod

        @pl.when(kk != 0)
        def _():
            acc[...] += prod

        @pl.when(kk == nk - 1)
        def _():
            r = acc[...]
            if add is not None:
                r = r + rest[0][...]
            o_ref[...] = r.astype(o_ref.dtype)

    if ta:
        a_spec = pl.BlockSpec((tk, tm), lambda i, j, kk: (kk, i))
    else:
        a_spec = pl.BlockSpec((tm, tk), lambda i, j, kk: (i, kk))
    b_tile = (tn, tk) if tb else (tk, tn)
    if view is not None:
        b_spec = pl.BlockSpec((None,) * view.lead + b_tile, view.index_map)
    elif tb:
        b_spec = pl.BlockSpec(b_tile, lambda i, j, kk: (j, kk))
    else:
        b_spec = pl.BlockSpec(b_tile, lambda i, j, kk: (kk, j))
    o_spec = pl.BlockSpec((tm, tn), lambda i, j, kk: (i, j))
    ins = [a, b] + ([add] if add is not None else [])
    in_specs = [a_spec, b_spec] + ([o_spec] if add is not None else [])
    out_shape, aliases = jax.ShapeDtypeStruct((m, n), out_dtype), {}
    if o_view is not None:
        assert add is None
        o_spec = pl.BlockSpec((None, tm, tn), o_view.index_map)
        out_shape = jax.ShapeDtypeStruct(o_view.shape, out_dtype)
        if o_view.into is not None:
            aliases = {len(ins): 0}
            ins.append(o_view.into)
            in_specs.append(pl.BlockSpec(memory_space=pl.ANY))
    if after is not None:
        ins.append(after)
        in_specs.append(pl.BlockSpec(memory_space=pl.ANY))
    return pl.pallas_call(
        body, name=name, grid=(m // tm, n // tn, nk),
        in_specs=in_specs, out_specs=o_spec, out_shape=out_shape, input_output_aliases=aliases,
        scratch_shapes=[pltpu.VMEM((tm, tn), F32)] if nk > 1 else [],
        compiler_params=_params(("parallel", "parallel", "arbitrary")),
    )(*ins)


def normmod_tile(x, g, shift, scale):
    return (rms(x) * g * (1.0 + scale) + shift,)


def resid_tile(x, y, gate):
    return (x + gate * y,)


def prep_tile(z_qk, z_rq, z_rk, z_gq, zg, cos, sin, qg, kg, gate_up, gate_b):
    out = []
    for h, t in enumerate(_heads(z_qk, ATT_Q_HEADS + ATT_KV_HEADS)):
        out.append(rope(rms(t) * (qg if h < ATT_Q_HEADS else kg), cos, sin))
    gq = z_gq * (GLA_DK ** -0.5)
    rq = [rope(t, cos, sin) for t in _heads(z_rq, RET_HEADS)]
    rk = [rope(t * (HEAD_DIM ** -0.5), cos, sin) for t in _heads(z_rk, RET_HEADS)]
    la = [log_sigmoid(bdot(zg, gate_up[d * LANES:(d + 1) * LANES]) + gate_b[d:d + 1]) * (1.0 / GLA_TAU) for d in range(2)]
    return (jnp.concatenate(out + [gq] + rq + rk + la, axis=1),)


def post_tile(o_att, o_ret_f, o_ret_b, o_gla_f, o_gla_b, rg, gr, ret_g, gla_g):
    ret = jnp.concatenate([rms(t) * ret_g for t in _heads(o_ret_f + o_ret_b, RET_HEADS)], axis=1) * silu(rg)
    gla = jnp.concatenate([rms(t) * gla_g for t in _heads(o_gla_f + o_gla_b, GLA_HEADS)], axis=1) * silu(gr)
    return (jnp.concatenate([o_att, ret, gla], axis=1),)


def _convglu_tile(n_lat, a, v, cw, cb):
    t = a.shape[0]
    row = lax.broadcasted_iota(jnp.int32, (t, 1), 0)
    has_prev = ((row != 0) & (row != n_lat)).astype(F32)
    has_next = ((row != n_lat - 1) & (row != t - 1)).astype(F32)
    conv = roll(a, 1, 0) * has_prev * cw[0:1] + a * cw[1:2] + roll(a, -1, 0) * has_next * cw[2:3] + cb
    return silu(conv) * v


def convglu(name, u, cw, cb, n_lat):
    t, f2 = u.shape
    f, tc = f2 // 2, FFN_COL_TILE
    nb = f // tc

    def body(a_ref, v_ref, cw_ref, cb_ref, o_ref):
        o_ref[...] = _convglu_tile(n_lat, a_ref[...], v_ref[...], cw_ref[...], cb_ref[...]).astype(o_ref.dtype)

    return pl.pallas_call(
        body, name=name, grid=(nb,),
        in_specs=[pl.BlockSpec((t, tc), lambda j: (0, j)), pl.BlockSpec((t, tc), lambda j: (0, nb + j)),
                  pl.BlockSpec((3, tc), lambda j: (0, j)), pl.BlockSpec((1, tc), lambda j: (0, j))],
        out_specs=pl.BlockSpec((t, tc), lambda j: (0, j)),
        out_shape=jax.ShapeDtypeStruct((t, f), BF16),
        compiler_params=_params(("parallel",)),
    )(u, u, cw, cb)


def convglu_bwd(name, u, cw, cb, dg, n_lat):
    t, f2 = u.shape
    f, tc = f2 // 2, FFN_COL_TILE
    nb = f // tc

    def body(a_ref, v_ref, cw_ref, cb_ref, dg_ref, da_ref, dv_ref, dcw_ref, dcb_ref):
        _, vjp = jax.vjp(functools.partial(_convglu_tile, n_lat), a_ref[...], v_ref[...], cw_ref[...], cb_ref[...])
        da, dv, dcw_ref[...], dcb_ref[...] = vjp(dg_ref[...])
        da_ref[...], dv_ref[...] = da.astype(BF16), dv.astype(BF16)

    col = pl.BlockSpec((t, tc), lambda j: (0, j))
    return pl.pallas_call(
        body, name=name, grid=(nb,),
        in_specs=[col, pl.BlockSpec((t, tc), lambda j: (0, nb + j)), pl.BlockSpec((3, tc), lambda j: (0, j)),
                  pl.BlockSpec((1, tc), lambda j: (0, j)), col],
        out_specs=[col, col, pl.BlockSpec((3, tc), lambda j: (0, j)), pl.BlockSpec((1, tc), lambda j: (0, j))],
        out_shape=[jax.ShapeDtypeStruct((t, f), BF16), jax.ShapeDtypeStruct((t, f), BF16),
                   jax.ShapeDtypeStruct((3, f), F32), jax.ShapeDtypeStruct((1, f), F32)],
        compiler_params=_params(("parallel",)),
    )(u, u, cw, cb, dg)


def final_loss(x, target, g, n_lat):
    tm = ROW_TILE
    d = x.shape[1]

    def body(x_ref, t_ref, g_ref, loss_ref, dx_ref, dg_ref):
        i = pl.program_id(0)
        tgt = t_ref[...]

        def f(xv, gv):
            e = rms(xv) * gv - tgt
            s = jnp.sum(jnp.sum(e * e, axis=1, keepdims=True), axis=0, keepdims=True)
            return s * (0.5 / d)

        val, vjp = jax.vjp(f, x_ref[...], g_ref[...])
        dx, dgv = vjp(jnp.ones((1, 1), F32))
        dx_ref[...] = dx

        @pl.when(i == 0)
        def _():
            dg_ref[...] = dgv
            loss_ref[...] = jnp.broadcast_to(val, loss_ref.shape)

        @pl.when(i != 0)
        def _():
            dg_ref[...] += dgv
            loss_ref[...] += jnp.broadcast_to(val, loss_ref.shape)

    return pl.pallas_call(
        body, name="final_loss", grid=(n_lat // tm,),
        in_specs=[pl.BlockSpec((tm, d), lambda i: (i, 0)), pl.BlockSpec((tm, d), lambda i: (i, 0)),
                  pl.BlockSpec((1, d), lambda i: (0, 0))],
        out_specs=[pl.BlockSpec((1, LANES), lambda i: (0, 0)), pl.BlockSpec((tm, d), lambda i: (i, 0)),
                   pl.BlockSpec((1, d), lambda i: (0, 0))],
        out_shape=[jax.ShapeDtypeStruct((1, LANES), F32), jax.ShapeDtypeStruct((n_lat, d), F32),
                   jax.ShapeDtypeStruct((1, d), F32)],
        compiler_params=_params(("arbitrary",)),
    )(x, target, g)


ATT_SCALE = HEAD_DIM ** -0.5
_AK_BLK = P_AK // HEAD_DIM
_AV_BLK = Z_AV // HEAD_DIM


def _att_specs(t, tq):
    gw = ATT_GROUP * HEAD_DIM
    q_spec = pl.BlockSpec((tq, gw), lambda kv, i: (i, kv))
    k_spec = pl.BlockSpec((t, HEAD_DIM), lambda kv, i: (0, _AK_BLK + kv))
    v_spec = pl.BlockSpec((t, HEAD_DIM), lambda kv, i: (0, _AV_BLK + kv))
    row_spec = pl.BlockSpec((ATT_GROUP, tq, 1), lambda kv, i: (kv, i, 0))
    return q_spec, k_spec, v_spec, row_spec


def _att_mask(i, t, tq, n_lat):
    col = lax.broadcasted_iota(jnp.int32, (1, t), 1)
    return jnp.where((i >= n_lat // tq) & (col < n_lat), -jnp.inf, 0.0).astype(F32)


def attn_fwd(p, z, n_lat):
    t = p.shape[0]
    tq = ROW_TILE

    def body(q_ref, k_ref, v_ref, o_ref, lse_ref):
        mask = _att_mask(pl.program_id(1), t, tq, n_lat)
        k, v = k_ref[...].astype(BF16), v_ref[...].astype(BF16)
        for g in range(ATT_GROUP):
            cols = slice(g * HEAD_DIM, (g + 1) * HEAD_DIM)
            s = _dg(q_ref[:, cols], k, _NT) * ATT_SCALE + mask
            m = jnp.max(s, axis=1, keepdims=True)
            pr = jnp.exp(s - m)
            l = jnp.sum(pr, axis=1, keepdims=True)
            o_ref[:, cols] = _dg(pr, v, _NN) / l
            lse_ref[g] = m + jnp.log(l)

    q_spec, k_spec, v_spec, row_spec = _att_specs(t, tq)
    return pl.pallas_call(
        body, name="attn_fwd", grid=(ATT_KV_HEADS, t // tq),
        in_specs=[q_spec, k_spec, v_spec], out_specs=[q_spec, row_spec],
        out_shape=[jax.ShapeDtypeStruct((t, ATT_Q_HEADS * HEAD_DIM), F32),
                   jax.ShapeDtypeStruct((ATT_Q_HEADS, t, 1), F32)],
        compiler_params=_params(("parallel", "parallel")),
    )(p, p, z)


def attn_bwd(p, z, o, lse, do, n_lat):
    t = p.shape[0]
    tq = ROW_TILE

    def body(q_ref, k_ref, v_ref, o_ref, do_ref, lse_ref, dq_ref, dk_ref, dv_ref):
        i = pl.program_id(1)

        @pl.when(i == 0)
        def _():
            dk_ref[...] = jnp.zeros_like(dk_ref)
            dv_ref[...] = jnp.zeros_like(dv_ref)

        mask = _att_mask(i, t, tq, n_lat)
        k, v = k_ref[...].astype(BF16), v_ref[...].astype(BF16)
        dk, dv = dk_ref[...], dv_ref[...]
        for g in range(ATT_GROUP):
            cols = slice(g * HEAD_DIM, (g + 1) * HEAD_DIM)
            q, do_g = q_ref[:, cols].astype(BF16), do_ref[:, cols]
            pr = jnp.exp(_dg(q, k, _NT) * ATT_SCALE + mask - lse_ref[g])
            delta = jnp.sum(o_ref[:, cols] * do_g, axis=1, keepdims=True)
            ds = pr * (_dg(do_g, v, _NT) - delta) * ATT_SCALE
            dq_ref[:, cols] = _dg(ds, k, _NN)
            dk = dk + _dg(ds, q, _TN)
            dv = dv + _dg(pr, do_g, _TN)
        dk_ref[...], dv_ref[...] = dk, dv

    q_spec, k_spec, v_spec, row_spec = _att_specs(t, tq)
    kv_out = pl.BlockSpec((t, HEAD_DIM), lambda kv, i: (0, kv))
    return pl.pallas_call(
        body, name="attn_bwd", grid=(ATT_KV_HEADS, t // tq),
        in_specs=[q_spec, k_spec, v_spec, q_spec, q_spec, row_spec],
        out_specs=[q_spec, kv_out, kv_out],
        out_shape=[jax.ShapeDtypeStruct((t, ATT_Q_HEADS * HEAD_DIM), F32),
                   jax.ShapeDtypeStruct((t, ATT_KV_HEADS * HEAD_DIM), F32),
                   jax.ShapeDtypeStruct((t, ATT_KV_HEADS * HEAD_DIM), F32)],
        compiler_params=_params(("parallel", "arbitrary")),
    )(p, p, z, o, do, lse)


_RQ_BLK = P_RQ // HEAD_DIM
_RK_BLK = P_RK // HEAD_DIM
_RV_BLK = Z_RV // HEAD_DIM


def _scan_chunk(direction, step, n_chunks, n_lat_chunks):
    return jnp.where(direction == 0, (step + n_lat_chunks) % n_chunks, n_chunks - 1 - step)


def _ret_geometry(direction):
    c = RET_CHUNK
    i = lax.broadcasted_iota(jnp.int32, (c, c), 0)
    j = lax.broadcasted_iota(jnp.int32, (c, c), 1)
    rel = jnp.where(direction == 0, i - j, j - i).astype(F32)
    r = lax.broadcasted_iota(jnp.int32, (c, 1), 0)
    pos = jnp.where(direction == 0, r, c - 1 - r).astype(F32)
    return rel, pos


def ret_chunk(q, k, v, s, lg, rel, pos):
    c = RET_CHUNK
    causal = rel >= 0
    d_in = jnp.where(causal, jnp.exp(lg * jnp.where(causal, rel, 0.0)), 0.0)
    q_dec = jnp.exp(lg * (pos + 1.0))
    k_dec = jnp.exp(lg * (c - 1.0 - pos))
    c_dec = jnp.exp(lg * c)
    att = bdot_nt(q, k) * d_in
    o = bdot(att, v) + bdot(q * q_dec, s)
    s_new = c_dec * s + bdot_tn(k * k_dec, v)
    return o, s_new


def ret_fwd(p, z, lg, n_lat):
    t = p.shape[0]
    c = RET_CHUNK
    nc, nlc = t // c, n_lat // c

    def body(q_ref, k_ref, v_ref, lg_ref, o_ref, ssave_ref, s_s):
        d, n = pl.program_id(0), pl.program_id(1)

        @pl.when(n == 0)
        def _():
            s_s[...] = jnp.zeros_like(s_s)

        rel, pos = _ret_geometry(d)
        for h in range(RET_HEADS):
            cols = slice(h * HEAD_DIM, (h + 1) * HEAD_DIM)
            ssave_ref[0, h, 0] = s_s[h]
            o, s_new = ret_chunk(q_ref[:, cols], k_ref[:, cols], v_ref[:, cols], s_s[h], lg_ref[0, h], rel, pos)
            o_ref[:, cols] = o
            s_s[h] = s_new

    w = RET_HEADS * HEAD_DIM

    def blk(base):
        return pl.BlockSpec((c, w), lambda d, n: (_scan_chunk(d, n, nc, nlc), base // RET_HEADS))

    return pl.pallas_call(
        body, name="ret_fwd", grid=(2, nc),
        in_specs=[blk(_RQ_BLK), blk(_RK_BLK), blk(_RV_BLK), pl.BlockSpec((1, RET_HEADS, 1, 1), lambda d, n: (d, 0, 0, 0))],
        out_specs=[pl.BlockSpec((c, w), lambda d, n: (_scan_chunk(d, n, nc, nlc), d)),
                   pl.BlockSpec((1, RET_HEADS, 1, HEAD_DIM, HEAD_DIM), lambda d, n: (d, 0, n, 0, 0))],
        out_shape=[jax.ShapeDtypeStruct((t, 2 * w), F32),
                   jax.ShapeDtypeStruct((2, RET_HEADS, nc, HEAD_DIM, HEAD_DIM), F32)],
        scratch_shapes=[pltpu.VMEM((RET_HEADS, HEAD_DIM, HEAD_DIM), F32)],
        compiler_params=_params(("parallel", "arbitrary")),
    )(p, p, z, lg)


def ret_bwd(p, z, lg, states, do, n_lat):
    t = p.shape[0]
    c = RET_CHUNK
    nc, nlc = t // c, n_lat // c

    def body(q_ref, k_ref, v_ref, lg_ref, s_ref, do_ref, dq_ref, dk_ref, dv_ref, dlg_ref, ds_s):
        d, n = pl.program_id(0), pl.program_id(1)

        @pl.when(n == 0)
        def _():
            ds_s[...] = jnp.zeros_like(ds_s)
            dlg_ref[...] = jnp.zeros_like(dlg_ref)

        rel, pos = _ret_geometry(d)
        f = functools.partial(ret_chunk, rel=rel, pos=pos)
        for h in range(RET_HEADS):
            cols = slice(h * HEAD_DIM, (h + 1) * HEAD_DIM)
            _, vjp = jax.vjp(f, q_ref[:, cols], k_ref[:, cols], v_ref[:, cols], s_ref[0, h, 0], lg_ref[0, h])
            dq, dk, dv, ds, dlg = vjp((do_ref[:, cols], ds_s[h]))
            dq_ref[:, cols], dk_ref[:, cols], dv_ref[:, cols] = dq, dk, dv
            ds_s[h] = ds
            dlg_ref[0, h] += dlg

    def chunk_of(d, n):
        return _scan_chunk(d, nc - 1 - n, nc, nlc)

    w = RET_HEADS * HEAD_DIM

    def blk(base):
        return pl.BlockSpec((c, w), lambda d, n: (chunk_of(d, n), base // RET_HEADS))

    out_blk = pl.BlockSpec((c, w), lambda d, n: (chunk_of(d, n), d))
    lg_blk = pl.BlockSpec((1, RET_HEADS, 1, 1), lambda d, n: (d, 0, 0, 0))
    grad_shape = jax.ShapeDtypeStruct((t, 2 * w), F32)
    return pl.pallas_call(
        body, name="ret_bwd", grid=(2, nc),
        in_specs=[blk(_RQ_BLK), blk(_RK_BLK), blk(_RV_BLK), lg_blk,
                  pl.BlockSpec((1, RET_HEADS, 1, HEAD_DIM, HEAD_DIM), lambda d, n: (d, 0, nc - 1 - n, 0, 0)),
                  pl.BlockSpec((c, w), lambda d, n: (chunk_of(d, n), 0))],
        out_specs=[out_blk, out_blk, out_blk, lg_blk],
        out_shape=[grad_shape, grad_shape, grad_shape, jax.ShapeDtypeStruct((2, RET_HEADS, 1, 1), F32)],
        scratch_shapes=[pltpu.VMEM((RET_HEADS, HEAD_DIM, HEAD_DIM), F32)],
        compiler_params=_params(("parallel", "arbitrary")),
    )(p, p, z, lg, states, do)


_GQ_BLK = P_GQ // (GLA_HEADS * GLA_DK)
_GK_BLK = Z_GK // (GLA_HEADS * GLA_DK)
_GV_BLK = Z_GV // (GLA_HEADS * GLA_DV)
_LA_BLK = P_LA // (GLA_HEADS * GLA_DK)


def _gla_mask(direction):
    c = GLA_CHUNK
    i = lax.broadcasted_iota(jnp.int32, (c, c), 0)
    j = lax.broadcasted_iota(jnp.int32, (c, c), 1)
    return (jnp.where(direction == 0, i - j, j - i) >= 0).astype(F32)


def gla_chunk(q, k, v, la, st, mask):
    b = mask_cumsum(mask, la)
    btot = jnp.sum(la, axis=0, keepdims=True)
    half = 0.5 * btot
    qt, kt = q * jnp.exp(b - half), k * jnp.exp(half - b)
    qs, ke = q * jnp.exp(b), k * jnp.exp(btot - b)
    outs, upd = [], []
    for h in range(GLA_HEADS):
        ks = slice(h * GLA_DK, (h + 1) * GLA_DK)
        vh = v[:, h * GLA_DV:(h + 1) * GLA_DV]
        att = bdot_nt(qt[:, ks], kt[:, ks]) * mask
        outs.append(bdot(att, vh) + bdot_nt(qs[:, ks], st[:, ks]))
        upd.append(bdot_tn(vh, ke[:, ks]))
    st_new = st * jnp.exp(btot) + jnp.concatenate(upd, axis=1)
    return jnp.concatenate(outs, axis=1), st_new


def gla_fwd(p, z, n_lat):
    t = p.shape[0]
    c = GLA_CHUNK
    nc, nlc = t // c, n_lat // c
    kw, vw = GLA_HEADS * GLA_DK, GLA_HEADS * GLA_DV

    def body(q_ref, k_ref, v_ref, la_ref, o_ref, ssave_ref, s_s):
        d, n = pl.program_id(0), pl.program_id(1)

        @pl.when(n == 0)
        def _():
            s_s[...] = jnp.zeros_like(s_s)

        ssave_ref[0, 0] = s_s[...]
        o, s_new = gla_chunk(q_ref[...], k_ref[...], v_ref[...], la_ref[...], s_s[...], _gla_mask(d))
        o_ref[...] = o
        s_s[...] = s_new

    def chunk_of(d, n):
        return _scan_chunk(d, n, nc, nlc)

    return pl.pallas_call(
        body, name="gla_fwd", grid=(2, nc),
        in_specs=[pl.BlockSpec((c, kw), lambda d, n: (chunk_of(d, n), _GQ_BLK)),
                  pl.BlockSpec((c, kw), lambda d, n: (chunk_of(d, n), _GK_BLK)),
                  pl.BlockSpec((c, vw), lambda d, n: (chunk_of(d, n), _GV_BLK)),
                  pl.BlockSpec((c, kw), lambda d, n: (chunk_of(d, n), _LA_BLK + d))],
        out_specs=[pl.BlockSpec((c, vw), lambda d, n: (chunk_of(d, n), d)),
                   pl.BlockSpec((1, 1, GLA_DV, kw), lambda d, n: (d, n, 0, 0))],
        out_shape=[jax.ShapeDtypeStruct((t, 2 * vw), F32), jax.ShapeDtypeStruct((2, nc, GLA_DV, kw), F32)],
        scratch_shapes=[pltpu.VMEM((GLA_DV, kw), F32)],
        compiler_params=_params(("parallel", "arbitrary")),
    )(p, z, z, p)


def gla_bwd(p, z, states, do, n_lat):
    t = p.shape[0]
    c = GLA_CHUNK
    nc, nlc = t // c, n_lat // c
    kw, vw = GLA_HEADS * GLA_DK, GLA_HEADS * GLA_DV

    def body(q_ref, k_ref, v_ref, la_ref, s_ref, do_ref, dq_ref, dk_ref, dv_ref, dla_ref, ds_s):
        d, n = pl.program_id(0), pl.program_id(1)

        @pl.when(n == 0)
        def _():
            ds_s[...] = jnp.zeros_like(ds_s)

        f = functools.partial(gla_chunk, mask=_gla_mask(d))
        _, vjp = jax.vjp(f, q_ref[...], k_ref[...], v_ref[...], la_ref[...], s_ref[0, 0])
        dq_ref[...], dk_ref[...], dv_ref[...], dla_ref[...], ds_s[...] = vjp((do_ref[...], ds_s[...]))

    def chunk_of(d, n):
        return _scan_chunk(d, nc - 1 - n, nc, nlc)

    k_out = pl.BlockSpec((c, kw), lambda d, n: (chunk_of(d, n), d))
    return pl.pallas_call(
        body, name="gla_bwd", grid=(2, nc),
        in_specs=[pl.BlockSpec((c, kw), lambda d, n: (chunk_of(d, n), _GQ_BLK)),
                  pl.BlockSpec((c, kw), lambda d, n: (chunk_of(d, n), _GK_BLK)),
                  pl.BlockSpec((c, vw), lambda d, n: (chunk_of(d, n), _GV_BLK)),
                  pl.BlockSpec((c, kw), lambda d, n: (chunk_of(d, n), _LA_BLK + d)),
                  pl.BlockSpec((1, 1, GLA_DV, kw), lambda d, n: (d, nc - 1 - n, 0, 0)),
                  pl.BlockSpec((c, vw), lambda d, n: (chunk_of(d, n), 0))],
        out_specs=[k_out, k_out, pl.BlockSpec((c, vw), lambda d, n: (chunk_of(d, n), d)), k_out],
        out_shape=[jax.ShapeDtypeStruct((t, 2 * kw), F32), jax.ShapeDtypeStruct((t, 2 * kw), F32),
                   jax.ShapeDtypeStruct((t, 2 * vw), F32), jax.ShapeDtypeStruct((t, 2 * kw), F32)],
        scratch_shapes=[pltpu.VMEM((GLA_DV, kw), F32)],
        compiler_params=_params(("parallel", "arbitrary")),
    )(p, z, z, p, states, do)


def _adam_tile(w, g, m, v):
    m = ADAM_B1 * m + (1.0 - ADAM_B1) * g
    v = ADAM_B2 * v + (1.0 - ADAM_B2) * (g * g)
    m_hat = m / (1.0 - ADAM_B1 ** ADAM_STEP)
    v_hat = v / (1.0 - ADAM_B2 ** ADAM_STEP)
    delta = -ADAM_LR * (m_hat / (jnp.sqrt(v_hat) + ADAM_EPS) + ADAM_WD * w)
    return delta, m, v


def adamw(name, w, g, m, v):
    shape = w.shape
    cols = shape[-1] if w.ndim > 1 and shape[-1] >= LANES else int(np.prod(shape))
    rows = int(np.prod(shape)) // cols
    tr = rows
    for cand in (512, 256, 128, 64, 32, 16, 8):
        if rows % cand == 0 and cand * cols * 4 <= (1 << 20):
            tr = cand
            break
    flat = [a.reshape(rows, cols) for a in (w, g, m, v)]

    def body(w_ref, g_ref, m_ref, v_ref, d_ref, mo_ref, vo_ref):
        d_ref[...], mo_ref[...], vo_ref[...] = _adam_tile(w_ref[...], g_ref[...], m_ref[...], v_ref[...])

    spec = pl.BlockSpec((tr, cols), lambda i: (i, 0))
    outs = pl.pallas_call(
        body, name=name, grid=(rows // tr,),
        in_specs=[spec] * 4, out_specs=[spec] * 3,
        out_shape=[jax.ShapeDtypeStruct((rows, cols), F32)] * 3,
        compiler_params=_params(("parallel",)),
    )(*flat)
    return tuple(o.reshape(shape) for o in outs)


def adamw_layers(name, w, grads, m, v):
    depth, rows, cols = w.shape
    tr = _rows_tile(rows, cols)
    nb = rows // tr

    def body(w_ref, m_ref, v_ref, *rest):
        g_refs, (g_ref, d_ref, mo_ref, vo_ref) = rest[:depth], rest[depth:]
        l = pl.program_id(0)
        for k in range(depth):
            @pl.when(l == k)
            def _():
                g = g_refs[k][...]
                g_ref[...] = g
                d_ref[...], mo_ref[...], vo_ref[...] = _adam_tile(w_ref[...], g, m_ref[...], v_ref[...])

    def layer_grad(k):
        return pl.BlockSpec((tr, cols), lambda l, i: (jnp.where(l < k, 0, jnp.where(l == k, i, nb - 1)), 0))

    spec = pl.BlockSpec((tr, cols), lambda l, i: (l * nb + i, 0))
    flat = [a.reshape(depth * rows, cols) for a in (w, m, v)]
    g_all, delta, new_m, new_v = pl.pallas_call(
        body, name=name, grid=(depth, nb),
        in_specs=[spec] * 3 + [layer_grad(k) for k in range(depth)], out_specs=[spec] * 4,
        out_shape=[jax.ShapeDtypeStruct((depth * rows, cols), F32)] * 4,
        compiler_params=_params(("arbitrary", "arbitrary")),
    )(*flat, *grads)
    return tuple(a.reshape(w.shape) for a in (delta, new_m, new_v)), g_all.reshape(w.shape)


MESH = pl.DeviceIdType.MESH
_HBM = pl.BlockSpec(memory_space=pltpu.HBM)
N_CHIPS = 4
N_DEV = 8


def _place():
    x, y, c = lax.axis_index("x"), lax.axis_index("y"), lax.axis_index("c")
    chips = [(1 - x, y), (x, 1 - y), (1 - x, 1 - y)]
    return x, y, c, chips


def _remote(src, dst, send_sem, recv_sem, to):
    return pltpu.make_async_remote_copy(src_ref=src, dst_ref=dst, send_sem=send_sem, recv_sem=recv_sem,
                                        device_id=to, device_id_type=MESH)


def all_gather_small(name, v):
    m_per, n = v.shape

    def body(x_ref, out_ref, send_sems, recv_sems, local_sem):
        x, y, c, chips = _place()
        me, sibling = (x, y, c), (x, y, 1 - c)

        def rows(px, py, pc):
            return out_ref.at[pl.ds((4 * px + 2 * py + pc) * m_per, m_per), :]

        def copy(k, block, to, src=None):
            return _remote(rows(*block) if src is None else src, rows(*block), send_sems.at[k], recv_sems.at[k], to)

        mine = pltpu.make_async_copy(x_ref, rows(*me), local_sem)
        mine.start()
        first = [copy(0, me, sibling, src=x_ref)]
        first += [copy(1 + j, me, (*chip, c), src=x_ref) for j, chip in enumerate(chips)]
        for cp in first:
            cp.start()
        passed = [copy(4 + j, (*chip, c), sibling) for j, chip in enumerate(chips)]
        for j, chip in enumerate(chips):
            copy(1 + j, (*chip, c), me).wait_recv()
            passed[j].start()
        copy(0, sibling, me).wait_recv()
        for j, chip in enumerate(chips):
            copy(4 + j, (*chip, 1 - c), me).wait_recv()
        for cp in first + passed:
            cp.wait_send()
        mine.wait()

    return pl.pallas_call(
        body, name=name,
        out_shape=jax.ShapeDtypeStruct((N_DEV * m_per, n), v.dtype),
        in_specs=[pl.BlockSpec(memory_space=pltpu.VMEM)],
        out_specs=pl.BlockSpec(memory_space=pltpu.VMEM),
        scratch_shapes=[pltpu.SemaphoreType.DMA((7,)), pltpu.SemaphoreType.DMA((7,)), pltpu.SemaphoreType.DMA],
        compiler_params=pltpu.CompilerParams(vmem_limit_bytes=VMEM_LIMIT),
    )(v)


_SEM = pl.BlockSpec(memory_space=pltpu.SEMAPHORE)
_SPLIT_COPY = pltpu.CompilerParams(has_side_effects=pltpu.SideEffectType.DATAFLOW_SIDE_EFFECTING)


class CopyPlan(NamedTuple):
    copies: object
    n: int
    in_place: bool = False


def _gather_copies(x_ref, land_ref, x, y, c, chips):
    half = x_ref.shape[0] // 2
    rows = pl.ds(c * half, half)
    return [(x_ref.at[rows, :], land_ref.at[2 * x + y, rows, :], (*chip, c), land_ref.at[2 * chip[0] + chip[1], rows, :])
            for chip in chips]


def _pass_copies(land_ref, _, x, y, c, chips):
    half = land_ref.shape[1] // 2
    mine, other = pl.ds(c * half, half), pl.ds((1 - c) * half, half)
    return [(land_ref.at[2 * chip[0] + chip[1], mine, :], land_ref.at[2 * chip[0] + chip[1], mine, :], (x, y, 1 - c),
             land_ref.at[2 * chip[0] + chip[1], other, :]) for chip in chips]


def _sibling_half_copies(p_ref, land_ref, x, y, c, chips):
    half = p_ref.shape[1] // 2
    return [(p_ref.at[:, pl.ds((1 - c) * half, half), :], land_ref, (x, y, 1 - c), land_ref)]


def _scatter_copies(s_ref, land_ref, x, y, c, chips):
    return [(s_ref.at[2 * chip[0] + chip[1]], land_ref.at[j], (*chip, c), land_ref.at[j]) for j, chip in enumerate(chips)]


def _join_copies(buf_ref, _, x, y, c, chips):
    half = buf_ref.shape[0] // 2
    mine = buf_ref.at[pl.ds(c * half, half), :]
    return [(mine, mine, (x, y, 1 - c), buf_ref.at[pl.ds((1 - c) * half, half), :])]


GATHER = CopyPlan(_gather_copies, 3)
PASS_ON = CopyPlan(_pass_copies, 3, in_place=True)
SIBLING_HALF = CopyPlan(_sibling_half_copies, 1)
SCATTER = CopyPlan(_scatter_copies, 3)
JOIN = CopyPlan(_join_copies, 1, in_place=True)


def split_start(name, plan, srcs, land_shapes=None, after=None):
    nt = len(srcs)
    arrays = [pltpu.with_memory_space_constraint(s, pltpu.HBM) for s in srcs]
    if not plan.in_place:
        arrays += [pltpu.with_memory_space_constraint(lax.empty(shape, s.dtype), pltpu.HBM) for shape, s in zip(land_shapes, srcs)]
    na = len(arrays)
    behind = [] if after is None else [after]
    n_in = na + len(behind)

    def body(*refs):
        x_refs = refs[:nt]
        land_refs = x_refs if plan.in_place else refs[nt:na]
        send, recv = refs[n_in:n_in + nt], refs[n_in + nt:n_in + 2 * nt]
        x, y, c, chips = _place()
        for t in range(nt):
            for j, (src, dst, to, _) in enumerate(plan.copies(x_refs[t], land_refs[t], x, y, c, chips)):
                _remote(src, dst, send[t].at[j], recv[t].at[j], to).start()
        refs[-1][...] = jnp.zeros_like(refs[-1])

    outs = pl.pallas_call(
        body, name=name,
        out_shape=tuple([pltpu.SemaphoreType.DMA((plan.n,))] * (2 * nt) + [pltpu.HBM(a.shape, a.dtype) for a in arrays]
                        + [jax.ShapeDtypeStruct((8, LANES), F32)]),
        in_specs=[_HBM] * na + [pl.BlockSpec(memory_space=pl.ANY)] * len(behind),
        out_specs=tuple([_SEM] * (2 * nt) + [_HBM] * na + [pl.BlockSpec(memory_space=pltpu.VMEM)]),
        input_output_aliases={i: 2 * nt + i for i in range(na)},
        compiler_params=_SPLIT_COPY,
    )(*arrays, *behind)
    groups = [(outs[t], outs[nt + t]) + tuple(outs[2 * nt + t + k * nt] for k in range(na // nt)) for t in range(nt)]
    return groups, outs[-1]


def split_wait(name, plan, group, after):
    send, recv, *arrays = group
    na = len(arrays)

    def body(*refs):
        x_ref, land_ref = refs[0], refs[na - 1]
        send_sem, recv_sem = refs[na], refs[na + 1]
        x, y, c, chips = _place()
        for j, (s, _, to, arrival) in enumerate(plan.copies(x_ref, land_ref, x, y, c, chips)):
            cp = _remote(s, arrival, send_sem.at[j], recv_sem.at[j], to)
            cp.wait_send()
            cp.wait_recv()

    return pl.pallas_call(
        body, name=name,
        out_shape=tuple(pltpu.HBM(a.shape, a.dtype) for a in arrays),
        in_specs=tuple([_HBM] * na + [_SEM, _SEM, pl.BlockSpec(memory_space=pl.ANY)]), out_specs=tuple([_HBM] * na),
        input_output_aliases={i: i for i in range(na)}, compiler_params=_SPLIT_COPY,
    )(*arrays, send, recv, after)


def _rows_tile(rows, cols):
    for cand in (512, 256, 128, 64, 32, 16):
        if rows % cand == 0 and cand * cols * 4 <= (1 << 21):
            return cand
    return rows


def add_sibling_half(name, pieces, from_sibling, core):
    n, h, cols = from_sibling.shape
    tr = _rows_tile(h, cols)
    nb = h // tr

    def body(c_ref, a_ref, b_ref, o_ref):
        o_ref[...] = (a_ref[...].astype(F32) + b_ref[...].astype(F32)).astype(o_ref.dtype)

    blk = pl.BlockSpec((1, tr, cols), lambda q, i, c_ref: (q, i, 0))
    return pl.pallas_call(
        body, name=name,
        grid_spec=pltpu.PrefetchScalarGridSpec(
            num_scalar_prefetch=1, grid=(n, nb),
            in_specs=[pl.BlockSpec((1, tr, cols), lambda q, i, c_ref: (q, c_ref[0] * nb + i, 0)), blk], out_specs=blk),
        out_shape=jax.ShapeDtypeStruct((n, h, cols), BF16),
        compiler_params=_params(("parallel", "parallel")),
    )(core.reshape(1).astype(jnp.int32), pieces, from_sibling)


def add_chip_sums(name, chip_sums, from_chips, chip, core):
    _, h, cols = chip_sums.shape
    tr = _rows_tile(h, cols)
    nb = h // tr

    def body(s_ref, own_ref, r0_ref, r1_ref, r2_ref, o_ref):
        acc = own_ref[0].astype(F32) + r0_ref[0].astype(F32)
        o_ref[...] = acc + r1_ref[0].astype(F32) + r2_ref[0].astype(F32)

    def got(j):
        return pl.BlockSpec((1, tr, cols), lambda i, s_ref: (j, i, 0))

    return pl.pallas_call(
        body, name=name,
        grid_spec=pltpu.PrefetchScalarGridSpec(
            num_scalar_prefetch=1, grid=(nb,),
            in_specs=[pl.BlockSpec((1, tr, cols), lambda i, s_ref: (s_ref[0], i, 0)), got(0), got(1), got(2)],
            out_specs=pl.BlockSpec((tr, cols), lambda i, s_ref: (s_ref[1] * nb + i, 0))),
        out_shape=jax.ShapeDtypeStruct((2 * h, cols), F32),
        compiler_params=_params(("parallel",)),
    )(jnp.stack([chip, core]).astype(jnp.int32), chip_sums, from_chips, from_chips, from_chips)


def sum_device_blocks(name, g):
    n = g.shape[1]

    def body(g_ref, o_ref):
        acc = g_ref[0:8, :]
        for d in range(1, N_DEV):
            acc = acc + g_ref[8 * d:8 * (d + 1), :]
        o_ref[...] = acc

    return pl.pallas_call(body, name=name, out_shape=jax.ShapeDtypeStruct((8, n), F32),
                          compiler_params=pltpu.CompilerParams(vmem_limit_bytes=VMEM_LIMIT))(g)


class LayerWeights(NamedTuple):
    norm1_g: jax.Array
    q_g: jax.Array
    k_g: jax.Array
    lg: jax.Array
    ret_g: jax.Array
    gate_up: jax.Array
    gate_b: jax.Array
    gla_g: jax.Array
    norm2_g: jax.Array
    conv_w: jax.Array
    conv_b: jax.Array


def _mod(mods, k):
    return mods[:, k:k + 1, :]


def out_view(l, tb):
    rows = D_MODEL // N_CHIPS
    if tb:
        return BView(n=D_MODEL, k=D_MODEL, tn=rows, tk=D_MODEL, index_map=lambda i, j, kk: (j, l, kk))
    return BView(n=D_MODEL, k=D_MODEL, tn=1024, tk=rows, index_map=lambda i, j, kk: (kk, l, j))


def down_view(l, f, tb):
    rows = f // N_CHIPS
    if tb:
        return BView(n=f, k=D_MODEL, tn=rows, tk=D_MODEL, index_map=lambda i, j, kk: (j, l, kk))
    return BView(n=D_MODEL, k=f, tn=1024, tk=rows, index_map=lambda i, j, kk: (kk, l, j))


def up_view(l, f, part=None):
    cols = 2 * f // N_CHIPS
    tc = _pick(cols, (1408, 1024, 512, 256))
    nbc = cols // tc
    if part is None:
        return BView(n=2 * f, k=D_MODEL, tn=tc, tk=D_MODEL, index_map=lambda i, j, kk: (j // nbc, l, j % nbc))
    nnb = D_MODEL // 1024
    return BView(n=D_MODEL, k=f, tn=1024, tk=tc, index_map=lambda i, j, kk: (2 * part + kk // nbc, l * nnb + j, kk % nbc))


def up_grad_view(f, part, into):
    cols = f // 2
    tn = _pick(cols, (1408, 1024, 512, 256))
    nbc = cols // tn
    return OView((N_CHIPS, D_MODEL, cols), lambda i, j, kk: (2 * part + j // nbc, i, j % nbc), tn, into)


def ada_view(l, n_ada, tb):
    if tb:
        return BView(n=D_MODEL, k=n_ada, tn=1024, tk=n_ada, index_map=lambda i, j, kk: (l, j, 0))
    return BView(n=n_ada, k=D_MODEL, tn=1024, tk=D_MODEL, index_map=lambda i, j, kk: (l, 0, j))


def _prep_args(z, zg, cos, sin, w):
    rows = [Row(z, Z_AV, 0), Row(z, 512, Z_RQ // 512), Row(z, 512, Z_RK // 512), Row(z, 256, Z_GQ // 256),
            Row(zg, LANES, 0), Row(cos, HEAD_DIM, 0, False), Row(sin, HEAD_DIM, 0, False)]
    return rows, [Par(w.q_g), Par(w.k_g), Par(w.gate_up), Par(w.gate_b)]


def _post_args(o_att, o_ret, o_gla, z, w):
    rows = [Row(o_att, 1024), Row(o_ret, 512, 0), Row(o_ret, 512, 1, False), Row(o_gla, 512, 0), Row(o_gla, 512, 1, False),
            Row(z, 512, Z_RG // 512), Row(z, 512, Z_GR // 512)]
    return rows, [Par(w.ret_g), Par(w.gla_g)]


def layer_fwd(l, xs, mods, w, fetch, cos, sin, n_lat):
    t, d = xs.shape
    tag = f"l{l}_"
    nm1 = [Par(w.norm1_g), Par(_mod(mods, 0), True), Par(_mod(mods, 1), True)]
    (h,) = row_map(tag + "norm1", normmod_tile, [Row(xs, d)], nm1, [(d, BF16)], t, n_lat)
    (w_main, w_gate), started = fetch("w_in", h)
    z = matmul(tag + "in_proj", h, w_main, after=started)
    zg = matmul(tag + "gate_proj", h, w_gate)
    rows, pars = _prep_args(z, zg, cos, sin, w)
    (p,) = row_map(tag + "prep", prep_tile, rows, pars, [(P_W, F32)], t, n_lat)
    o_att, lse = attn_fwd(p, z, n_lat)
    o_ret, s_ret = ret_fwd(p, z, w.lg, n_lat)
    o_gla, s_gla = gla_fwd(p, z, n_lat)
    rows, pars = _post_args(o_att, o_ret, o_gla, z, w)
    (m,) = row_map(tag + "post", post_tile, rows, pars, [(d, BF16)], t, n_lat)
    g_out, started = fetch("w_out", m)
    y = matmul(tag + "out_proj", m, g_out, view=out_view(0, False), after=started)
    (x1,) = row_map(tag + "resid1", resid_tile, [Row(xs, d), Row(y, d)], [Par(_mod(mods, 2), True)], [(d, F32)], t, n_lat)
    nm2 = [Par(w.norm2_g), Par(_mod(mods, 3), True), Par(_mod(mods, 4), True)]
    (h2,) = row_map(tag + "norm2", normmod_tile, [Row(x1, d)], nm2, [(d, BF16)], t, n_lat)
    f = w.conv_b.shape[1]
    g_up, started = fetch("w_up", h2)
    u = matmul(tag + "up_proj", h2, g_up, view=up_view(0, f), after=started)
    g = convglu(tag + "convglu", u, w.conv_w, w.conv_b, n_lat)
    g_down, started = fetch("w_down", g)
    yd = matmul(tag + "down_proj", g, g_down, view=down_view(0, f, False), after=started)
    (x2,) = row_map(tag + "resid2", resid_tile, [Row(x1, d), Row(yd, d)], [Par(_mod(mods, 5), True)], [(d, F32)], t, n_lat)
    saved = dict(xs=xs, h=h, z=z, zg=zg, p=p, o_att=o_att, lse=lse, o_ret=o_ret, s_ret=s_ret, o_gla=o_gla, s_gla=s_gla,
                 m=m, y=y, x1=x1, h2=h2, u=u, g=g, yd=yd, w_main=w_main, w_gate=w_gate, g_out=g_out, g_up=g_up, g_down=g_down)
    return x2, saved


def _sum_dirs(a):
    w = a.shape[1] // 2
    return a[:, :w] + a[:, w:]


def layer_bwd(l, dx2, s, mods, w, cos, sin, n_lat, grad_ready):
    t, d = dx2.shape
    tag = f"l{l}_b_"
    dyd, dgate5 = row_vjp(tag + "resid2", resid_tile, [Row(s["x1"], d, 0, False), Row(s["yd"], d)],
                          [Par(_mod(mods, 5), True)], [dx2], t, n_lat, row_grad_dtype=BF16)
    f = w.conv_b.shape[1]
    dg = matmul(tag + "down_dx", dyd, s["g_down"], tb=True, view=down_view(0, f, True))
    dw_down = matmul(tag + "down_dw", s["g"], dyd, ta=True, out_dtype=BF16)
    da, dv, dcw, dcb = convglu_bwd(tag + "convglu", s["u"], w.conv_w, w.conv_b, dg, n_lat)
    dh2 = matmul(tag + "up_dx_gate", da, s["g_up"], tb=True, view=up_view(0, f, 0))
    dh2 = matmul(tag + "up_dx_value", dv, s["g_up"], tb=True, view=up_view(0, f, 1), add=dh2)
    dw_up = matmul(tag + "up_dw_gate", s["h2"], da, ta=True, out_dtype=BF16, o_view=up_grad_view(f, 0, None))
    dw_up = matmul(tag + "up_dw_value", s["h2"], dv, ta=True, out_dtype=BF16, o_view=up_grad_view(f, 1, dw_up))
    started = grad_ready("ffn", dict(w_up=dw_up, w_down=dw_down))
    nm2 = [Par(w.norm2_g), Par(_mod(mods, 3), True), Par(_mod(mods, 4), True)]
    dx1, dg2, dshift3, dscale4 = row_vjp(tag + "norm2", normmod_tile, [Row(s["x1"], d)], nm2, [dh2], t, n_lat,
                                         add_to_first=dx2, after=started)
    dy, dgate2 = row_vjp(tag + "resid1", resid_tile, [Row(s["xs"], d, 0, False), Row(s["y"], d)],
                         [Par(_mod(mods, 2), True)], [dx1], t, n_lat, row_grad_dtype=BF16)
    dm = matmul(tag + "out_dx", dy, s["g_out"], tb=True, view=out_view(0, True))
    dw_out = matmul(tag + "out_dw", s["m"], dy, ta=True, out_dtype=BF16)
    rows, pars = _post_args(s["o_att"], s["o_ret"], s["o_gla"], s["z"], w)
    started = grad_ready("w_out", dict(w_out=dw_out))
    do_att, do_ret, do_gla, d_rg, d_gr, d_ret_g, d_gla_g = row_vjp(tag + "post", post_tile, rows, pars, [dm], t, n_lat, after=started)
    dq_a, dk_a, dv_a = attn_bwd(s["p"], s["z"], s["o_att"], s["lse"], do_att, n_lat)
    dq_r, dk_r, dv_r, dlg = ret_bwd(s["p"], s["z"], w.lg, s["s_ret"], do_ret, n_lat)
    dq_g, dk_g, dv_g, dla = gla_bwd(s["p"], s["z"], s["s_gla"], do_gla, n_lat)
    dp = jnp.concatenate([dq_a, dk_a, _sum_dirs(dq_g), _sum_dirs(dq_r), _sum_dirs(dk_r), dla], axis=1)
    rows, pars = _prep_args(s["z"], s["zg"], cos, sin, w)
    d_zqk, d_zrq, d_zrk, d_zgq, dzg, d_qg, d_kg, d_up, d_gb = row_vjp(tag + "prep", prep_tile, rows, pars, [dp], t, n_lat)
    dz = jnp.concatenate([d_zqk, dv_a, d_zrq, d_zrk, _sum_dirs(dv_r), d_rg, d_zgq, _sum_dirs(dk_g), _sum_dirs(dv_g), d_gr], axis=1)
    dz, dzg = dz.astype(BF16), dzg.astype(BF16)
    dh_gate = matmul(tag + "gate_dx", dzg, s["w_gate"], tb=True)
    dh = matmul(tag + "in_dx", dz, s["w_main"], tb=True, add=dh_gate)
    dw_main = matmul(tag + "in_dw", s["h"], dz, ta=True, out_dtype=BF16)
    dw_gate = matmul(tag + "gate_dw", s["h"], dzg, ta=True, out_dtype=BF16)
    started = grad_ready("w_in", dict(w_main=dw_main, w_gate=dw_gate))
    nm1 = [Par(w.norm1_g), Par(_mod(mods, 0), True), Par(_mod(mods, 1), True)]
    dx, dg1, dshift0, dscale1 = row_vjp(tag + "norm1", normmod_tile, [Row(s["xs"], d)], nm1, [dh], t, n_lat,
                                        add_to_first=dx1, after=started)
    dmods = jnp.concatenate([dshift0, dscale1, dgate2, dshift3, dscale4, dgate5], axis=1)
    grads = dict(w_main=dw_main, w_gate=dw_gate, w_out=dw_out, w_up=dw_up, w_down=dw_down, norm1_g=dg1, q_g=d_qg, k_g=d_kg,
                 lg=dlg, ret_g=d_ret_g, gate_up=d_up, gate_b=d_gb, gla_g=d_gla_g, norm2_g=dg2, conv_w=dcw, conv_b=dcb)
    return dx, dmods, grads


def rope_tables(n_lat, n_ctx):
    rows = n_lat // GRID_W
    row = jnp.repeat(jnp.arange(rows, dtype=F32), GRID_W)
    col = jnp.tile(jnp.arange(GRID_W, dtype=F32), rows)
    n_freq = HEAD_DIM // 4
    inv_freq = ROPE_THETA ** (-jnp.arange(n_freq, dtype=F32) / n_freq)
    ang = jnp.concatenate([row[:, None] * inv_freq, col[:, None] * inv_freq], axis=-1)
    cos, sin = jnp.cos(ang), jnp.sin(ang)
    cos = jnp.concatenate([jnp.concatenate([cos, cos], axis=1), jnp.ones((n_ctx, HEAD_DIM), F32)], axis=0)
    sin = jnp.concatenate([jnp.concatenate([-sin, sin], axis=1), jnp.zeros((n_ctx, HEAD_DIM), F32)], axis=0)
    return cos, sin


def local_step(xs, target, mods, weights, fetch, final_g, n_lat, grad_ready):
    t, d = xs.shape
    cos, sin = rope_tables(n_lat, t - n_lat)
    saved = []
    h = xs
    for l, w in enumerate(weights):
        h, s = layer_fwd(l, h, mods[l], w, functools.partial(fetch, l), cos, sin, n_lat)
        saved.append(s)
    loss, dlat, dgf = final_loss(h, target, final_g, n_lat)
    dx = jnp.concatenate([dlat, jnp.zeros((t - n_lat, d), F32)], axis=0)
    dmods, grads = [None] * len(weights), [None] * len(weights)
    for l in reversed(range(len(weights))):
        dx, dmods[l], grads[l] = layer_bwd(l, dx, saved[l], mods[l], weights[l], cos, sin, n_lat, functools.partial(grad_ready, l))
    return loss, dx, dmods, grads, dgf


WEIGHT_NAMES = ("c_ctx", "ada_w", "ada_b", "norm1_g", "w_in", "q_norm_g", "k_norm_g", "ret_log_decay", "ret_norm_g",
                "gla_gate_up", "gla_gate_b", "gla_norm_g", "w_out", "norm2_g", "w_up", "conv_w", "conv_b", "w_down", "final_norm_g")
PACK_QUANTUM = 8 * LANES


def _pack(arrays):
    flat = jnp.concatenate([a.reshape(-1).astype(F32) for a in arrays])
    n = -(-flat.shape[0] // PACK_QUANTUM) * PACK_QUANTUM
    return jnp.pad(flat, (0, n - flat.shape[0])).reshape(8, n // 8)


def _unpack(flat2d, shapes):
    out, at = [], 0
    for s in shapes:
        size = int(np.prod(s))
        out.append(flat2d[:, at:at + size].reshape((flat2d.shape[0],) + tuple(s)))
        at += size
    return out


def _per_device(gathered):
    return gathered.reshape(N_DEV, -1)


def _from_chips(per_device, axis):
    chips = per_device[0::2]
    moved = jnp.moveaxis(chips, 0, axis)
    shape = moved.shape
    return moved.reshape(shape[:axis] + (shape[axis] * shape[axis + 1],) + shape[axis + 2:])


def kernel(x, c, ctx, c_ctx, ada_w, ada_b, norm1_g, w_in, q_norm_g, k_norm_g, ret_log_decay, ret_norm_g, gla_gate_up, gla_gate_b, gla_norm_g, w_out, norm2_g, w_up, conv_w, conv_b, w_down, final_norm_g, loss_target, m_c_ctx, m_ada_w, m_ada_b, m_norm1_g, m_w_in, m_q_norm_g, m_k_norm_g, m_ret_log_decay, m_ret_norm_g, m_gla_gate_up, m_gla_gate_b, m_gla_norm_g, m_w_out, m_norm2_g, m_w_up, m_conv_w, m_conv_b, m_w_down, m_final_norm_g, v_c_ctx, v_ada_w, v_ada_b, v_norm1_g, v_w_in, v_q_norm_g, v_k_norm_g, v_ret_log_decay, v_ret_norm_g, v_gla_gate_up, v_gla_gate_b, v_gla_norm_g, v_w_out, v_norm2_g, v_w_up, v_conv_w, v_conv_b, v_w_down, v_final_norm_g):
    weights = dict(zip(WEIGHT_NAMES, (c_ctx, ada_w, ada_b, norm1_g, w_in, q_norm_g, k_norm_g, ret_log_decay, ret_norm_g,
                                      gla_gate_up, gla_gate_b, gla_norm_g, w_out, norm2_g, w_up, conv_w, conv_b, w_down, final_norm_g)))
    mom_m = dict(zip(WEIGHT_NAMES, (m_c_ctx, m_ada_w, m_ada_b, m_norm1_g, m_w_in, m_q_norm_g, m_k_norm_g, m_ret_log_decay, m_ret_norm_g,
                                    m_gla_gate_up, m_gla_gate_b, m_gla_norm_g, m_w_out, m_norm2_g, m_w_up, m_conv_w, m_conv_b, m_w_down, m_final_norm_g)))
    mom_v = dict(zip(WEIGHT_NAMES, (v_c_ctx, v_ada_w, v_ada_b, v_norm1_g, v_w_in, v_q_norm_g, v_k_norm_g, v_ret_log_decay, v_ret_norm_g,
                                    v_gla_gate_up, v_gla_gate_b, v_gla_norm_g, v_w_out, v_norm2_g, v_w_up, v_conv_w, v_conv_b, v_w_down, v_final_norm_g)))
    depth, d = norm1_g.shape
    assert d == D_MODEL and x.shape[0] == 1
    n_lat, n_ctx, f = x.shape[1], ctx.shape[1], conv_b.shape[1]
    assert n_lat % ROW_TILE == 0 and n_ctx % ROW_TILE == 0 and f % FFN_COL_TILE == 0 and f % N_CHIPS == 0
    n_in = w_in.shape[2]
    n_ada = ada_w.shape[2]
    xi, yi, ci = lax.axis_index("x"), lax.axis_index("y"), lax.axis_index("c")
    chip = 2 * xi + yi
    dev = 2 * chip + ci

    big = ("w_in", "w_out", "w_up", "w_down")
    order = [(l, name) for l in range(depth) for name in big]
    shards = [weights[name][l].astype(BF16) for l, name in order]
    passing = {}

    def pass_on(k, after):
        tag = "{1}{0}".format(*order[k])
        own, land = split_wait("gather_wait_" + tag, GATHER, in_flight[k], after)
        (moving,), started = split_start("gather_pass_" + tag, PASS_ON, [land])
        passing[k] = (own, moving)
        return started

    def fetch(l, name, after):
        k = order.index((l, name))
        if k == 0:
            pass_on(0, after)
        own, moving = passing.pop(k)
        (land,) = split_wait(f"gather_pass_wait_{name}{l}", PASS_ON, moving, after)
        started = pass_on(k + 1, after) if k + 1 < len(order) else None
        land = lax.dynamic_update_slice_in_dim(land, own[None], chip, axis=0)
        if name != "w_in":
            return land, started
        cols = jnp.concatenate([land[q] for q in range(N_CHIPS)], axis=1)
        return (cols[:, :N_MAIN], jnp.pad(cols[:, N_MAIN:], ((0, 0), (0, LANES - N_GATE)))), started

    small_shapes = [c.shape[1:], conv_w.shape, gla_gate_up.shape, gla_gate_b.shape]
    got = _per_device(all_gather_small("gather_small", _pack([c, conv_w, gla_gate_up, gla_gate_b])))
    c_all, conv_w_sh, gate_up_sh, gate_b_sh = _unpack(got, small_shapes)
    conv_w_full = _from_chips(conv_w_sh, 2)
    gate_up_full = _from_chips(gate_up_sh, 3)
    gate_b_full = _from_chips(gate_b_sh, 2)

    act = jnp.zeros((16, d), F32).at[0:N_DEV].set(jax.nn.silu(c_all)).at[N_DEV].set(jax.nn.silu(c_ctx))
    mod_sh = jnp.stack([matmul(f"ada_fwd{l}", act, ada_w, view=ada_view(l, n_ada, False)) for l in range(depth)])
    got = _per_device(all_gather_small("gather_mods", _pack([mod_sh])))
    (mod_sh_all,) = _unpack(got, [mod_sh.shape])
    mod_full = _from_chips(mod_sh_all, 2) + ada_b[:, None, :]
    mod_mine = lax.dynamic_index_in_dim(mod_full, dev, axis=1, keepdims=False)
    mods = [jnp.stack([mod_mine[l].reshape(N_MOD, d), mod_full[l, N_DEV].reshape(N_MOD, d)]) for l in range(depth)]
    in_flight, token = split_start("gather_start", GATHER, shards, [(N_CHIPS,) + s.shape for s in shards], after=mod_full)

    layer_w = []
    for l in range(depth):
        up = jnp.zeros((2, LANES, GLA_HEADS * GLA_DK), F32)
        up = up.at[0, 0:GLA_RANK].set(gate_up_full[l, 0]).at[1, GLA_RANK:2 * GLA_RANK].set(gate_up_full[l, 1])
        layer_w.append(LayerWeights(
            norm1_g=norm1_g[l].reshape(1, 1, d), q_g=q_norm_g[l].reshape(1, 1, HEAD_DIM), k_g=k_norm_g[l].reshape(1, 1, HEAD_DIM),
            lg=ret_log_decay[l].reshape(2, RET_HEADS, 1, 1), ret_g=ret_norm_g[l].reshape(1, 1, HEAD_DIM),
            gate_up=up.reshape(1, 2 * LANES, -1), gate_b=gate_b_full[l].reshape(1, 2, -1), gla_g=gla_norm_g[l].reshape(1, 1, HEAD_DIM),
            norm2_g=norm2_g[l].reshape(1, 1, d), conv_w=conv_w_full[l], conv_b=conv_b[l].reshape(1, f)))

    def pieces_of(name, g):
        if name == "w_in":
            full_cols = jnp.concatenate([g["w_main"], g["w_gate"][:, :N_GATE]], axis=1)
            return jnp.stack([full_cols[:, q * n_in:(q + 1) * n_in] for q in range(N_CHIPS)])
        if name == "w_up":
            return g["w_up"]
        return g[name].reshape(N_CHIPS, -1, d)

    groups = {"ffn": ("w_up", "w_down"), "w_out": ("w_out",), "w_in": ("w_in",)}
    reducing = {}
    to_sibling = []

    def sibling_arrived(after):
        started = None
        while to_sibling:
            l, group, in_flight_halves = to_sibling.pop(0)
            sums = []
            for name, halves in zip(groups[group], in_flight_halves):
                pieces, from_sibling = split_wait(f"rs_sibling_wait_{name}{l}", SIBLING_HALF, halves, after)
                sums.append(add_sibling_half(f"rs_add_sibling_{name}{l}", pieces, from_sibling, ci))
            in_flight_sums, token = split_start(f"rs_start_{group}{l}", SCATTER, sums, [(3,) + s.shape[1:] for s in sums])
            reducing.update({(l, name): grp for name, grp in zip(groups[group], in_flight_sums)})
            started = token if started is None else started + token
        return started

    def grad_ready(l, group, g):
        pieces = [pieces_of(name, g) for name in groups[group]]
        before = None if (l, group) == (0, "w_in") else sibling_arrived(pieces[0])
        in_flight_halves, started = split_start(f"rs_sibling_{group}{l}", SIBLING_HALF, pieces,
                                                [(N_CHIPS, pc.shape[1] // 2, pc.shape[2]) for pc in pieces])
        to_sibling.append((l, group, in_flight_halves))
        return started if before is None else started + before

    xs = jnp.concatenate([x[0], ctx[0]], axis=0) + token[0, 0]
    loss, dx, dmods, grads, dgf = local_step(xs, loss_target[0], mods, layer_w, fetch, final_norm_g.reshape(1, d), n_lat, grad_ready)

    def gate_up_grad(g):
        return jnp.stack([g[0, 0:GLA_RANK], g[0, LANES + GLA_RANK:LANES + 2 * GLA_RANK]])

    per_layer = [[dmods[l][0], dmods[l][1], grads[l]["norm1_g"], grads[l]["norm2_g"], grads[l]["q_g"], grads[l]["k_g"],
                  grads[l]["ret_g"], grads[l]["gla_g"], grads[l]["lg"], gate_up_grad(grads[l]["gate_up"]), grads[l]["gate_b"],
                  grads[l]["conv_w"], grads[l]["conv_b"]] for l in range(depth)]
    layer_shapes = [(N_MOD * d,), (N_MOD * d,), (d,), (d,), (HEAD_DIM,), (HEAD_DIM,), (HEAD_DIM,), (HEAD_DIM,), (2, RET_HEADS),
                    (2, GLA_RANK, GLA_HEADS * GLA_DK), (2, GLA_HEADS * GLA_DK), (3, f), (f,)]
    packed = _pack([a for lay in per_layer for a in lay] + [dgf, loss[0, 0:1]])
    gathered = all_gather_small("gather_small_grads", packed)
    every = _unpack(_per_device(gathered), layer_shapes * depth + [(d,), (1,)])
    total = _unpack(sum_device_blocks("sum_small_grads", gathered).reshape(1, -1), layer_shapes * depth + [(d,), (1,)])
    nl = len(layer_shapes)

    def tot(l, k):
        return total[l * nl + k][0]

    out = {"norm1_g": jnp.stack([tot(l, 2) for l in range(depth)]), "norm2_g": jnp.stack([tot(l, 3) for l in range(depth)]),
           "q_norm_g": jnp.stack([tot(l, 4) for l in range(depth)]), "k_norm_g": jnp.stack([tot(l, 5) for l in range(depth)]),
           "ret_norm_g": jnp.stack([tot(l, 6) for l in range(depth)]), "gla_norm_g": jnp.stack([tot(l, 7) for l in range(depth)]),
           "ret_log_decay": jnp.stack([tot(l, 8) for l in range(depth)]),
           "gla_gate_up": lax.dynamic_slice_in_dim(jnp.stack([tot(l, 9) for l in range(depth)]), chip * gla_gate_up.shape[3], gla_gate_up.shape[3], axis=3),
           "gla_gate_b": lax.dynamic_slice_in_dim(jnp.stack([tot(l, 10) for l in range(depth)]), chip * gla_gate_b.shape[2], gla_gate_b.shape[2], axis=2),
           "conv_w": lax.dynamic_slice_in_dim(jnp.stack([tot(l, 11) for l in range(depth)]), chip * conv_w.shape[2], conv_w.shape[2], axis=2),
           "conv_b": jnp.stack([tot(l, 12) for l in range(depth)]),
           "final_norm_g": total[depth * nl][0],
           "ada_b": jnp.stack([tot(l, 0) + tot(l, 1) for l in range(depth)])}
    loss_total = total[depth * nl + 1][0, 0]

    dmod_all = jnp.zeros((depth, 16, N_MOD * d), F32)
    for l in range(depth):
        dmod_all = dmod_all.at[l, 0:N_DEV].set(every[l * nl][:, :]).at[l, N_DEV].set(tot(l, 1))
    dmod_cols = lax.dynamic_slice_in_dim(dmod_all, chip * n_ada, n_ada, axis=2)
    for l in range(depth):
        slab = OView((depth, d, n_ada), functools.partial(lambda i, j, kk, l: (l, i, j), l=l), None, out.get("ada_w"))
        out["ada_w"] = matmul(f"ada_dw{l}", act, dmod_cols[l], ta=True, o_view=slab)
    dact = matmul("ada_dx0", dmod_cols[0], ada_w, tb=True, view=ada_view(0, n_ada, True))
    for l in range(1, depth):
        dact = matmul(f"ada_dx{l}", dmod_cols[l], ada_w, tb=True, view=ada_view(l, n_ada, True), add=dact)
    got = _per_device(all_gather_small("gather_dcctx", _pack([dact[N_DEV]])))
    sibling_arrived(got)
    got = got[0::2, :d]
    dsilu = got[0] + got[1] + got[2] + got[3]
    sig = jax.nn.sigmoid(c_ctx)
    out["c_ctx"] = dsilu * (sig + c_ctx * sig * (1.0 - sig))

    deltas, new_m, new_v = {}, {}, {}

    def update(name):
        out[name] = out[name].reshape(weights[name].shape)
        deltas[name], new_m[name], new_v[name] = adamw("adamw_" + name, weights[name], out[name], mom_m[name], mom_v[name])

    for name in WEIGHT_NAMES:
        if name not in big:
            update(name)
    behind = new_v["ada_w"]
    joining = []

    def joined(after):
        name, in_flight_halves = joining.pop()
        per_layer = [split_wait(f"rs_join_wait_{name}{l}", JOIN, grp, after)[0] for l, grp in enumerate(in_flight_halves)]
        (deltas[name], new_m[name], new_v[name]), out[name] = adamw_layers(
            "adamw_" + name, weights[name], per_layer, mom_m[name], mom_v[name])
        return new_v[name]

    for name in ("w_down", "w_up", "w_out", "w_in"):
        halves = []
        for l in range(depth):
            sums, got = split_wait(f"rs_wait_{name}{l}", SCATTER, reducing[(l, name)], behind)
            halves.append(add_chip_sums(f"rs_add_chips_{name}{l}", sums, got, chip, ci))
        in_flight_halves, _ = split_start("rs_join_" + name, JOIN, halves)
        if joining:
            behind = joined(behind)
        joining.append((name, in_flight_halves))
    joined(behind)
    grad_x = dx[:n_lat].reshape(x.shape)
    return (loss_total, grad_x, *[out[n] for n in WEIGHT_NAMES], *[deltas[n] for n in WEIGHT_NAMES],
            *[new_m[n] for n in WEIGHT_NAMES], *[new_v[n] for n in WEIGHT_NAMES])
```

```python
import functools
from typing import NamedTuple

import numpy as np
import jax
import jax.numpy as jnp
from jax import lax
from jax.experimental import pallas as pl
from jax.experimental.pallas import tpu as pltpu

F32 = jnp.float32
BF16 = jnp.bfloat16

D_MODEL = 2048
HEAD_DIM = 128
ATT_Q_HEADS = 8
ATT_KV_HEADS = 2
ATT_GROUP = ATT_Q_HEADS // ATT_KV_HEADS
RET_HEADS = 4
GLA_HEADS = 4
GLA_DK = 64
GLA_DV = 128
GLA_RANK = 16
GLA_TAU = 16.0
RET_CHUNK = 256
GLA_CHUNK = 128
GRID_W = 64
ROPE_THETA = 10000.0
N_MOD = 6
EPS = 1e-6
N_MAIN = 5120
N_GATE = 2 * GLA_RANK
LANES = 128
ROW_TILE = 256
FFN_COL_TILE = 256
VMEM_LIMIT = 56 * 1024 * 1024

ADAM_LR = 0.001
ADAM_B1 = 0.9
ADAM_B2 = 0.999
ADAM_EPS = 1e-08
ADAM_WD = 0.01
ADAM_STEP = 10

Z_AQ, Z_AK, Z_AV = 0, 1024, 1280
Z_RQ, Z_RK, Z_RV, Z_RG = 1536, 2048, 2560, 3072
Z_GQ, Z_GK, Z_GV, Z_GR = 3584, 3840, 4096, 4608
P_AQ, P_AK, P_GQ, P_RQ, P_RK, P_LA = 0, 1024, 1280, 1536, 2048, 2560
P_W = 3072


def _params(sem=None):
    return pltpu.CompilerParams(dimension_semantics=sem, vmem_limit_bytes=VMEM_LIMIT)


def _pick(n, cands):
    for c in cands:
        if n % c == 0:
            return c
    return n


_NN = (((1,), (0,)), ((), ()))
_NT = (((1,), (1,)), ((), ()))
_TN = (((0,), (0,)), ((), ()))


def _dg(a, b, dims):
    return lax.dot_general(a.astype(BF16), b.astype(BF16), dims, preferred_element_type=F32)


@jax.custom_vjp
def bdot(a, b):
    return _dg(a, b, _NN)


def _bdot_fwd(a, b):
    return _dg(a, b, _NN), (a, b)


def _bdot_bwd(res, ct):
    a, b = res
    return _dg(ct, b, _NT), _dg(a, ct, _TN)


bdot.defvjp(_bdot_fwd, _bdot_bwd)


@jax.custom_vjp
def bdot_nt(a, b):
    return _dg(a, b, _NT)


def _bdot_nt_fwd(a, b):
    return _dg(a, b, _NT), (a, b)


def _bdot_nt_bwd(res, ct):
    a, b = res
    return _dg(ct, b, _NN), _dg(ct, a, _TN)


bdot_nt.defvjp(_bdot_nt_fwd, _bdot_nt_bwd)


@jax.custom_vjp
def bdot_tn(a, b):
    return _dg(a, b, _TN)


def _bdot_tn_fwd(a, b):
    return _dg(a, b, _TN), (a, b)


def _bdot_tn_bwd(res, ct):
    a, b = res
    return _dg(b, ct, _NT), _dg(a, ct, _NN)


bdot_tn.defvjp(_bdot_tn_fwd, _bdot_tn_bwd)


def _split3(x):
    x1 = x.astype(BF16)
    r1 = x - x1.astype(F32)
    x2 = r1.astype(BF16)
    x3 = (r1 - x2.astype(F32)).astype(BF16)
    return x1, x2, x3


def _mask_dot(mask_bf16, x, dims):
    x1, x2, x3 = _split3(x)
    f = lambda t: lax.dot_general(mask_bf16, t, dims, preferred_element_type=F32)
    return f(x1) + f(x2) + f(x3)


@jax.custom_vjp
def mask_cumsum(mask, x):
    return _mask_dot(mask.astype(BF16), x, _NN)


def _mask_cumsum_fwd(mask, x):
    return mask_cumsum(mask, x), mask


def _mask_cumsum_bwd(mask, ct):
    return jnp.zeros_like(mask), _mask_dot(mask.astype(BF16), ct, _TN)


mask_cumsum.defvjp(_mask_cumsum_fwd, _mask_cumsum_bwd)


def _roll(x, shift, axis):
    return pltpu.roll(x, shift % x.shape[axis], axis)


@functools.partial(jax.custom_vjp, nondiff_argnums=(1, 2))
def roll(x, shift, axis):
    return _roll(x, shift, axis)


def _roll_fwd(x, shift, axis):
    return _roll(x, shift, axis), None


def _roll_bwd(shift, axis, _, ct):
    return (_roll(ct, -shift, axis),)


roll.defvjp(_roll_fwd, _roll_bwd)


def rms(x):
    return x * lax.rsqrt(jnp.mean(x * x, axis=-1, keepdims=True) + EPS)


def silu(x):
    return x * (1.0 / (1.0 + jnp.exp(-x)))


def log_sigmoid(x):
    return jnp.minimum(x, 0.0) - jnp.log(1.0 + jnp.exp(-jnp.abs(x)))


def rope(t, cos, sin):
    return t * cos + roll(t, HEAD_DIM // 2, 1) * sin


def _heads(x, n, width=HEAD_DIM):
    return [x[:, h * width:(h + 1) * width] for h in range(n)]


class Row(NamedTuple):
    arr: jax.Array
    width: int
    idx: int = 0
    diff: bool = True


class Par(NamedTuple):
    arr: jax.Array
    grouped: bool = False
    diff: bool = True


def _row_specs(rows, pars, tm, n_lat_tiles):
    def grp(i):
        return jnp.minimum(i // n_lat_tiles, 1)

    specs = [pl.BlockSpec((tm, r.width), functools.partial(lambda i, k: (i, k), k=r.idx)) for r in rows]
    for p in pars:
        blk = (1,) + p.arr.shape[1:]
        if p.grouped:
            specs.append(pl.BlockSpec(blk, lambda i: (grp(i), 0, 0)))
        else:
            specs.append(pl.BlockSpec(blk, lambda i: (0, 0, 0)))
    return specs


def row_map(name, fn, rows, pars, outs, n_rows, n_lat):
    tm = ROW_TILE
    nr, npar = len(rows), len(pars)

    def body(*refs):
        vals = [r[...] for r in refs[:nr]] + [p[0] for p in refs[nr:nr + npar]]
        res = fn(*vals)
        for o, v in zip(refs[nr + npar:], res):
            o[...] = v.astype(o.dtype)

    return pl.pallas_call(
        body, name=name, grid=(n_rows // tm,),
        in_specs=_row_specs(rows, pars, tm, n_lat // tm),
        out_specs=[pl.BlockSpec((tm, w), lambda i: (i, 0)) for w, _ in outs],
        out_shape=[jax.ShapeDtypeStruct((n_rows, w), dt) for w, dt in outs],
        compiler_params=_params(("arbitrary",)),
    )(*[r.arr for r in rows], *[p.arr for p in pars])


def row_vjp(name, fn, rows, pars, cts, n_rows, n_lat, add_to_first=None, row_grad_dtype=F32, after=None):
    tm = ROW_TILE
    nr, npar, nc = len(rows), len(pars), len(cts)
    n_lat_tiles = n_lat // tm
    args = list(rows) + list(pars)
    diff_pos = [k for k, a in enumerate(args) if a.diff]
    n_add = 0 if add_to_first is None else 1
    n_after = 0 if after is None else 1

    def body(*refs):
        i = pl.program_id(0)
        vals = [r[...] for r in refs[:nr]] + [p[0] for p in refs[nr:nr + npar]]
        ct_vals = tuple(c[...] for c in refs[nr + npar:nr + npar + nc])
        out_refs = refs[nr + npar + nc + n_add + n_after:]

        def g(*dv):
            full = list(vals)
            for k, v in zip(diff_pos, dv):
                full[k] = v
            return tuple(fn(*full))

        _, vjp = jax.vjp(g, *[vals[k] for k in diff_pos])
        grads = vjp(ct_vals)
        for n, (k, o, gr) in enumerate(zip(diff_pos, out_refs, grads)):
            if k < nr:
                o[...] = (gr + refs[nr + npar + nc][...] if (n == 0 and n_add) else gr).astype(o.dtype)
            else:
                first = (i == 0) | (i == n_lat_tiles) if args[k].grouped else (i == 0)

                @pl.when(first)
                def _():
                    o[0] = gr

                @pl.when(jnp.logical_not(first))
                def _():
                    o[0] += gr

    def grp(i):
        return jnp.minimum(i // n_lat_tiles, 1)

    out_specs, out_shape = [], []
    for k in diff_pos:
        a = args[k]
        if k < nr:
            out_specs.append(pl.BlockSpec((tm, a.width), lambda i: (i, 0)))
            out_shape.append(jax.ShapeDtypeStruct((n_rows, a.width), row_grad_dtype))
        else:
            blk = (1,) + a.arr.shape[1:]
            out_specs.append(pl.BlockSpec(blk, (lambda i: (grp(i), 0, 0)) if a.grouped else (lambda i: (0, 0, 0))))
            out_shape.append(jax.ShapeDtypeStruct(a.arr.shape, F32))
    extra = list(cts) + ([add_to_first] if n_add else [])
    ct_specs = [pl.BlockSpec((tm, c.shape[1]), lambda i: (i, 0)) for c in extra]
    if n_after:
        extra.append(after)
        ct_specs.append(pl.BlockSpec(memory_space=pl.ANY))
    return pl.pallas_call(
        body, name=name, grid=(n_rows // tm,),
        in_specs=_row_specs(rows, pars, tm, n_lat_tiles) + ct_specs,
        out_specs=out_specs, out_shape=out_shape,
        compiler_params=_params(("arbitrary",)),
    )(*[r.arr for r in rows], *[p.arr for p in pars], *extra)


class BView(NamedTuple):
    n: int
    k: int
    tn: int
    tk: int
    index_map: object
    lead: int = 1


MATMUL_VMEM_BUDGET = 40 * 1024 * 1024


def _matmul_tiles(m, n, k, a_bytes, b_bytes, o_bytes):
    tms = [c for c in (1152, 1024, 768, 512, 256, 128) if m % c == 0] or [m]
    tns = [c for c in (2048, 1408, 1280, 1024, 768, 512, 256, 128) if n % c == 0] or [n]
    tks = [k] + [c for c in (2816, 2304, 2048, 1408, 1024, 512, 256, 128) if k % c == 0 and c < k]
    for tk in tks:
        fits = [(tm * tn, tm, tn) for tm in tms for tn in tns
                if 2 * (tm * tk * a_bytes + tk * tn * b_bytes + tm * tn * o_bytes) + 2 * tm * tn * 4 <= MATMUL_VMEM_BUDGET]
        if fits and (max(fits)[0] >= 512 * 512 or tms == [m] or tk == tks[-1]):
            _, tm, tn = max(fits)
            return tm, tn, tk
    raise ValueError(f"no matmul tiling for {(m, n, k)}")


class OView(NamedTuple):
    shape: tuple
    index_map: object
    tn: int = None
    into: object = None


def matmul(name, a, b, *, ta=False, tb=False, add=None, out_dtype=F32, view=None, o_view=None, after=None):
    m = a.shape[1] if ta else a.shape[0]
    o_bytes = jnp.dtype(out_dtype).itemsize * (1 if add is None else 2)
    if view is None:
        k = a.shape[0] if ta else a.shape[1]
        n = b.shape[0] if tb else b.shape[1]
        assert (b.shape[1] if tb else b.shape[0]) == k, (a.shape, b.shape, ta, tb)
        if o_view is not None and o_view.tn is not None:
            tn = o_view.tn
            tm, _, tk = _matmul_tiles(m, tn, k, a.dtype.itemsize, b.dtype.itemsize, o_bytes)
        else:
            tm, tn, tk = _matmul_tiles(m, n, k, a.dtype.itemsize, b.dtype.itemsize, o_bytes)
    else:
        n, k, tn, tk = view.n, view.k, view.tn, view.tk
        tm, _, _ = _matmul_tiles(m, tn, tk, a.dtype.itemsize, b.dtype.itemsize, o_bytes)
    nk = k // tk
    dims = (((0 if ta else 1,), (1 if tb else 0,)), ((), ()))

    def body(a_ref, b_ref, *rest):
        prod = lax.dot_general(a_ref[...].astype(BF16), b_ref[...].astype(BF16), dims, preferred_element_type=F32)
        if nk == 1:
            o_ref = rest[-1]
            o_ref[...] = (prod if add is None else prod + rest[0][...]).astype(o_ref.dtype)
            return
        o_ref, acc = rest[-2:]
        kk = pl.program_id(2)

        @pl.when(kk == 0)
        def _():
            acc[...] = prod

        @pl.when(kk != 0)
        def _():
            acc[...] += prod

        @pl.when(kk == nk - 1)
        def _():
            r = acc[...]
            if add is not None:
                r = r + rest[0][...]
            o_ref[...] = r.astype(o_ref.dtype)

    if ta:
        a_spec = pl.BlockSpec((tk, tm), lambda i, j, kk: (kk, i))
    else:
        a_spec = pl.BlockSpec((tm, tk), lambda i, j, kk: (i, kk))
    b_tile = (tn, tk) if tb else (tk, tn)
    if view is not None:
        b_spec = pl.BlockSpec((None,) * view.lead + b_tile, view.index_map)
    elif tb:
        b_spec = pl.BlockSpec(b_tile, lambda i, j, kk: (j, kk))
    else:
        b_spec = pl.BlockSpec(b_tile, lambda i, j, kk: (kk, j))
    o_spec = pl.BlockSpec((tm, tn), lambda i, j, kk: (i, j))
    ins = [a, b] + ([add] if add is not None else [])
    in_specs = [a_spec, b_spec] + ([o_spec] if add is not None else [])
    out_shape, aliases = jax.ShapeDtypeStruct((m, n), out_dtype), {}
    if o_view is not None:
        assert add is None
        o_spec = pl.BlockSpec((None, tm, tn), o_view.index_map)
        out_shape = jax.ShapeDtypeStruct(o_view.shape, out_dtype)
        if o_view.into is not None:
            aliases = {len(ins): 0}
            ins.append(o_view.into)
            in_specs.append(pl.BlockSpec(memory_space=pl.ANY))
    if after is not None:
        ins.append(after)
        in_specs.append(pl.BlockSpec(memory_space=pl.ANY))
    return pl.pallas_call(
        body, name=name, grid=(m // tm, n // tn, nk),
        in_specs=in_specs, out_specs=o_spec, out_shape=out_shape, input_output_aliases=aliases,
        scratch_shapes=[pltpu.VMEM((tm, tn), F32)] if nk > 1 else [],
        compiler_params=_params(("parallel", "parallel", "arbitrary")),
    )(*ins)


def normmod_tile(x, g, shift, scale):
    return (rms(x) * g * (1.0 + scale) + shift,)


def resid_tile(x, y, gate):
    return (x + gate * y,)


def gated_tile(y, gate):
    return (gate * y,)


def prep_tile(z_qk, z_rq, z_rk, z_gq, zg, cos, sin, qg, kg, gate_up, gate_b):
    out = []
    for h, t in enumerate(_heads(z_qk, ATT_Q_HEADS + ATT_KV_HEADS)):
        out.append(rope(rms(t) * (qg if h < ATT_Q_HEADS else kg), cos, sin))
    gq = z_gq * (GLA_DK ** -0.5)
    rq = [rope(t, cos, sin) for t in _heads(z_rq, RET_HEADS)]
    rk = [rope(t * (HEAD_DIM ** -0.5), cos, sin) for t in _heads(z_rk, RET_HEADS)]
    la = [log_sigmoid(bdot(zg, gate_up[d * LANES:(d + 1) * LANES]) + gate_b[d:d + 1]) * (1.0 / GLA_TAU) for d in range(2)]
    return (jnp.concatenate(out + [gq] + rq + rk + la, axis=1),)


def post_tile(o_att, o_ret_f, o_ret_b, o_gla_f, o_gla_b, rg, gr, ret_g, gla_g):
    ret = jnp.concatenate([rms(t) * ret_g for t in _heads(o_ret_f + o_ret_b, RET_HEADS)], axis=1) * silu(rg)
    gla = jnp.concatenate([rms(t) * gla_g for t in _heads(o_gla_f + o_gla_b, GLA_HEADS)], axis=1) * silu(gr)
    return (jnp.concatenate([o_att, ret, gla], axis=1),)


def _convglu_tile(n_lat, a, v, cw, cb):
    t = a.shape[0]
    row = lax.broadcasted_iota(jnp.int32, (t, 1), 0)
    has_prev = ((row != 0) & (row != n_lat)).astype(F32)
    has_next = ((row != n_lat - 1) & (row != t - 1)).astype(F32)
    conv = roll(a, 1, 0) * has_prev * cw[0:1] + a * cw[1:2] + roll(a, -1, 0) * has_next * cw[2:3] + cb
    return silu(conv) * v


def convglu(name, u, cw, cb, n_lat):
    t, f2 = u.shape
    f, tc = f2 // 2, FFN_COL_TILE
    nb = f // tc

    def body(a_ref, v_ref, cw_ref, cb_ref, o_ref):
        o_ref[...] = _convglu_tile(n_lat, a_ref[...].astype(F32), v_ref[...].astype(F32), cw_ref[...], cb_ref[...]).astype(o_ref.dtype)

    return pl.pallas_call(
        body, name=name, grid=(nb,),
        in_specs=[pl.BlockSpec((t, tc), lambda j: (0, j)), pl.BlockSpec((t, tc), lambda j: (0, nb + j)),
                  pl.BlockSpec((3, tc), lambda j: (0, j)), pl.BlockSpec((1, tc), lambda j: (0, j))],
        out_specs=pl.BlockSpec((t, tc), lambda j: (0, j)),
        out_shape=jax.ShapeDtypeStruct((t, f), BF16),
        compiler_params=_params(("parallel",)),
    )(u, u, cw, cb)


def convglu_bwd(name, u, cw, cb, dg, n_lat):
    t, f2 = u.shape
    f, tc = f2 // 2, FFN_COL_TILE
    nb = f // tc

    def body(a_ref, v_ref, cw_ref, cb_ref, dg_ref, da_ref, dv_ref, dcw_ref, dcb_ref):
        _, vjp = jax.vjp(functools.partial(_convglu_tile, n_lat), a_ref[...].astype(F32), v_ref[...].astype(F32),
                         cw_ref[...], cb_ref[...])
        da, dv, dcw_ref[...], dcb_ref[...] = vjp(dg_ref[...])
        da_ref[...], dv_ref[...] = da.astype(BF16), dv.astype(BF16)

    col = pl.BlockSpec((t, tc), lambda j: (0, j))
    return pl.pallas_call(
        body, name=name, grid=(nb,),
        in_specs=[col, pl.BlockSpec((t, tc), lambda j: (0, nb + j)), pl.BlockSpec((3, tc), lambda j: (0, j)),
                  pl.BlockSpec((1, tc), lambda j: (0, j)), col],
        out_specs=[col, col, pl.BlockSpec((3, tc), lambda j: (0, j)), pl.BlockSpec((1, tc), lambda j: (0, j))],
        out_shape=[jax.ShapeDtypeStruct((t, f), BF16), jax.ShapeDtypeStruct((t, f), BF16),
                   jax.ShapeDtypeStruct((3, f), F32), jax.ShapeDtypeStruct((1, f), F32)],
        compiler_params=_params(("parallel",)),
    )(u, u, cw, cb, dg)


def final_loss(x, target, g, n_lat):
    tm = ROW_TILE
    d = x.shape[1]

    def body(x_ref, t_ref, g_ref, loss_ref, dx_ref, dg_ref):
        i = pl.program_id(0)
        tgt = t_ref[...]

        def f(xv, gv):
            e = rms(xv) * gv - tgt
            s = jnp.sum(jnp.sum(e * e, axis=1, keepdims=True), axis=0, keepdims=True)
            return s * (0.5 / d)

        val, vjp = jax.vjp(f, x_ref[...], g_ref[...])
        dx, dgv = vjp(jnp.ones((1, 1), F32))
        dx_ref[...] = dx

        @pl.when(i == 0)
        def _():
            dg_ref[...] = dgv
            loss_ref[...] = jnp.broadcast_to(val, loss_ref.shape)

        @pl.when(i != 0)
        def _():
            dg_ref[...] += dgv
            loss_ref[...] += jnp.broadcast_to(val, loss_ref.shape)

    return pl.pallas_call(
        body, name="final_loss", grid=(n_lat // tm,),
        in_specs=[pl.BlockSpec((tm, d), lambda i: (i, 0)), pl.BlockSpec((tm, d), lambda i: (i, 0)),
                  pl.BlockSpec((1, d), lambda i: (0, 0))],
        out_specs=[pl.BlockSpec((1, LANES), lambda i: (0, 0)), pl.BlockSpec((tm, d), lambda i: (i, 0)),
                   pl.BlockSpec((1, d), lambda i: (0, 0))],
        out_shape=[jax.ShapeDtypeStruct((1, LANES), F32), jax.ShapeDtypeStruct((n_lat, d), F32),
                   jax.ShapeDtypeStruct((1, d), F32)],
        compiler_params=_params(("arbitrary",)),
    )(x, target, g)


ATT_SCALE = HEAD_DIM ** -0.5
_AK_BLK = P_AK // HEAD_DIM
_AV_BLK = Z_AV // HEAD_DIM


def _att_specs(t, tq):
    gw = ATT_GROUP * HEAD_DIM
    q_spec = pl.BlockSpec((tq, gw), lambda kv, i: (i, kv))
    k_spec = pl.BlockSpec((t, HEAD_DIM), lambda kv, i: (0, _AK_BLK + kv))
    v_spec = pl.BlockSpec((t, HEAD_DIM), lambda kv, i: (0, _AV_BLK + kv))
    row_spec = pl.BlockSpec((ATT_GROUP, tq, 1), lambda kv, i: (kv, i, 0))
    return q_spec, k_spec, v_spec, row_spec


def _att_mask(i, t, tq, n_lat):
    col = lax.broadcasted_iota(jnp.int32, (1, t), 1)
    return jnp.where((i >= n_lat // tq) & (col < n_lat), -jnp.inf, 0.0).astype(F32)


def attn_fwd(p, z, n_lat):
    t = p.shape[0]
    tq = ROW_TILE

    def body(q_ref, k_ref, v_ref, o_ref, lse_ref):
        mask = _att_mask(pl.program_id(1), t, tq, n_lat)
        k, v = k_ref[...].astype(BF16), v_ref[...].astype(BF16)
        for g in range(ATT_GROUP):
            cols = slice(g * HEAD_DIM, (g + 1) * HEAD_DIM)
            s = _dg(q_ref[:, cols], k, _NT) * ATT_SCALE + mask
            m = jnp.max(s, axis=1, keepdims=True)
            pr = jnp.exp(s - m)
            l = jnp.sum(pr, axis=1, keepdims=True)
            o_ref[:, cols] = _dg(pr, v, _NN) / l
            lse_ref[g] = m + jnp.log(l)

    q_spec, k_spec, v_spec, row_spec = _att_specs(t, tq)
    return pl.pallas_call(
        body, name="attn_fwd", grid=(ATT_KV_HEADS, t // tq),
        in_specs=[q_spec, k_spec, v_spec], out_specs=[q_spec, row_spec],
        out_shape=[jax.ShapeDtypeStruct((t, ATT_Q_HEADS * HEAD_DIM), F32),
                   jax.ShapeDtypeStruct((ATT_Q_HEADS, t, 1), F32)],
        compiler_params=_params(("parallel", "parallel")),
    )(p, p, z)


def attn_bwd(p, z, o, lse, do, n_lat):
    t = p.shape[0]
    tq = ROW_TILE

    def body(q_ref, k_ref, v_ref, o_ref, do_ref, lse_ref, dq_ref, dk_ref, dv_ref):
        i = pl.program_id(1)

        @pl.when(i == 0)
        def _():
            dk_ref[...] = jnp.zeros_like(dk_ref)
            dv_ref[...] = jnp.zeros_like(dv_ref)

        mask = _att_mask(i, t, tq, n_lat)
        k, v = k_ref[...].astype(BF16), v_ref[...].astype(BF16)
        dk, dv = dk_ref[...], dv_ref[...]
        for g in range(ATT_GROUP):
            cols = slice(g * HEAD_DIM, (g + 1) * HEAD_DIM)
            q, do_g = q_ref[:, cols].astype(BF16), do_ref[:, cols]
            pr = jnp.exp(_dg(q, k, _NT) * ATT_SCALE + mask - lse_ref[g])
            delta = jnp.sum(o_ref[:, cols] * do_g, axis=1, keepdims=True)
            ds = pr * (_dg(do_g, v, _NT) - delta) * ATT_SCALE
            dq_ref[:, cols] = _dg(ds, k, _NN)
            dk = dk + _dg(ds, q, _TN)
            dv = dv + _dg(pr, do_g, _TN)
        dk_ref[...], dv_ref[...] = dk, dv

    q_spec, k_spec, v_spec, row_spec = _att_specs(t, tq)
    kv_out = pl.BlockSpec((t, HEAD_DIM), lambda kv, i: (0, kv))
    return pl.pallas_call(
        body, name="attn_bwd", grid=(ATT_KV_HEADS, t // tq),
        in_specs=[q_spec, k_spec, v_spec, q_spec, q_spec, row_spec],
        out_specs=[q_spec, kv_out, kv_out],
        out_shape=[jax.ShapeDtypeStruct((t, ATT_Q_HEADS * HEAD_DIM), F32),
                   jax.ShapeDtypeStruct((t, ATT_KV_HEADS * HEAD_DIM), F32),
                   jax.ShapeDtypeStruct((t, ATT_KV_HEADS * HEAD_DIM), F32)],
        compiler_params=_params(("parallel", "arbitrary")),
    )(p, p, z, o, do, lse)


_RQ_BLK = P_RQ // HEAD_DIM
_RK_BLK = P_RK // HEAD_DIM
_RV_BLK = Z_RV // HEAD_DIM


def _scan_chunk(direction, step, n_chunks, n_lat_chunks):
    return jnp.where(direction == 0, (step + n_lat_chunks) % n_chunks, n_chunks - 1 - step)


def _ret_geometry(direction):
    c = RET_CHUNK
    i = lax.broadcasted_iota(jnp.int32, (c, c), 0)
    j = lax.broadcasted_iota(jnp.int32, (c, c), 1)
    rel = jnp.where(direction == 0, i - j, j - i).astype(F32)
    r = lax.broadcasted_iota(jnp.int32, (c, 1), 0)
    pos = jnp.where(direction == 0, r, c - 1 - r).astype(F32)
    return rel, pos


def ret_chunk(q, k, v, s, lg, rel, pos):
    c = RET_CHUNK
    causal = rel >= 0
    d_in = jnp.where(causal, jnp.exp(lg * jnp.where(causal, rel, 0.0)), 0.0)
    q_dec = jnp.exp(lg * (pos + 1.0))
    k_dec = jnp.exp(lg * (c - 1.0 - pos))
    c_dec = jnp.exp(lg * c)
    att = bdot_nt(q, k) * d_in
    o = bdot(att, v) + bdot(q * q_dec, s)
    s_new = c_dec * s + bdot_tn(k * k_dec, v)
    return o, s_new


def ret_fwd(p, z, lg, n_lat):
    t = p.shape[0]
    c = RET_CHUNK
    nc, nlc = t // c, n_lat // c

    def body(q_ref, k_ref, v_ref, lg_ref, o_ref, ssave_ref, s_s):
        d, n = pl.program_id(0), pl.program_id(1)

        @pl.when(n == 0)
        def _():
            s_s[...] = jnp.zeros_like(s_s)

        rel, pos = _ret_geometry(d)
        for h in range(RET_HEADS):
            cols = slice(h * HEAD_DIM, (h + 1) * HEAD_DIM)
            ssave_ref[0, h, 0] = s_s[h]
            o, s_new = ret_chunk(q_ref[:, cols], k_ref[:, cols], v_ref[:, cols], s_s[h], lg_ref[0, h], rel, pos)
            o_ref[:, cols] = o
            s_s[h] = s_new

    w = RET_HEADS * HEAD_DIM

    def blk(base):
        return pl.BlockSpec((c, w), lambda d, n: (_scan_chunk(d, n, nc, nlc), base // RET_HEADS))

    return pl.pallas_call(
        body, name="ret_fwd", grid=(2, nc),
        in_specs=[blk(_RQ_BLK), blk(_RK_BLK), blk(_RV_BLK), pl.BlockSpec((1, RET_HEADS, 1, 1), lambda d, n: (d, 0, 0, 0))],
        out_specs=[pl.BlockSpec((c, w), lambda d, n: (_scan_chunk(d, n, nc, nlc), d)),
                   pl.BlockSpec((1, RET_HEADS, 1, HEAD_DIM, HEAD_DIM), lambda d, n: (d, 0, n, 0, 0))],
        out_shape=[jax.ShapeDtypeStruct((t, 2 * w), F32),
                   jax.ShapeDtypeStruct((2, RET_HEADS, nc, HEAD_DIM, HEAD_DIM), F32)],
        scratch_shapes=[pltpu.VMEM((RET_HEADS, HEAD_DIM, HEAD_DIM), F32)],
        compiler_params=_params(("parallel", "arbitrary")),
    )(p, p, z, lg)


def ret_bwd(p, z, lg, states, do, n_lat):
    t = p.shape[0]
    c = RET_CHUNK
    nc, nlc = t // c, n_lat // c

    def body(q_ref, k_ref, v_ref, lg_ref, s_ref, do_ref, dq_ref, dk_ref, dv_ref, dlg_ref, ds_s):
        d, n = pl.program_id(0), pl.program_id(1)

        @pl.when(n == 0)
        def _():
            ds_s[...] = jnp.zeros_like(ds_s)
            dlg_ref[...] = jnp.zeros_like(dlg_ref)

        rel, pos = _ret_geometry(d)
        f = functools.partial(ret_chunk, rel=rel, pos=pos)
        for h in range(RET_HEADS):
            cols = slice(h * HEAD_DIM, (h + 1) * HEAD_DIM)
            _, vjp = jax.vjp(f, q_ref[:, cols], k_ref[:, cols], v_ref[:, cols], s_ref[0, h, 0], lg_ref[0, h])
            dq, dk, dv, ds, dlg = vjp((do_ref[:, cols], ds_s[h]))
            dq_ref[:, cols], dk_ref[:, cols], dv_ref[:, cols] = dq, dk, dv
            ds_s[h] = ds
            dlg_ref[0, h] += dlg

    def chunk_of(d, n):
        return _scan_chunk(d, nc - 1 - n, nc, nlc)

    w = RET_HEADS * HEAD_DIM

    def blk(base):
        return pl.BlockSpec((c, w), lambda d, n: (chunk_of(d, n), base // RET_HEADS))

    out_blk = pl.BlockSpec((c, w), lambda d, n: (chunk_of(d, n), d))
    lg_blk = pl.BlockSpec((1, RET_HEADS, 1, 1), lambda d, n: (d, 0, 0, 0))
    grad_shape = jax.ShapeDtypeStruct((t, 2 * w), F32)
    return pl.pallas_call(
        body, name="ret_bwd", grid=(2, nc),
        in_specs=[blk(_RQ_BLK), blk(_RK_BLK), blk(_RV_BLK), lg_blk,
                  pl.BlockSpec((1, RET_HEADS, 1, HEAD_DIM, HEAD_DIM), lambda d, n: (d, 0, nc - 1 - n, 0, 0)),
                  pl.BlockSpec((c, w), lambda d, n: (chunk_of(d, n), 0))],
        out_specs=[out_blk, out_blk, out_blk, lg_blk],
        out_shape=[grad_shape, grad_shape, grad_shape, jax.ShapeDtypeStruct((2, RET_HEADS, 1, 1), F32)],
        scratch_shapes=[pltpu.VMEM((RET_HEADS, HEAD_DIM, HEAD_DIM), F32)],
        compiler_params=_params(("parallel", "arbitrary")),
    )(p, p, z, lg, states, do)


_GQ_BLK = P_GQ // (GLA_HEADS * GLA_DK)
_GK_BLK = Z_GK // (GLA_HEADS * GLA_DK)
_GV_BLK = Z_GV // (GLA_HEADS * GLA_DV)
_LA_BLK = P_LA // (GLA_HEADS * GLA_DK)


def _gla_mask(direction):
    c = GLA_CHUNK
    i = lax.broadcasted_iota(jnp.int32, (c, c), 0)
    j = lax.broadcasted_iota(jnp.int32, (c, c), 1)
    return (jnp.where(direction == 0, i - j, j - i) >= 0).astype(F32)


def gla_chunk(q, k, v, la, st, mask):
    b = mask_cumsum(mask, la)
    btot = jnp.sum(la, axis=0, keepdims=True)
    half = 0.5 * btot
    qt, kt = q * jnp.exp(b - half), k * jnp.exp(half - b)
    qs, ke = q * jnp.exp(b), k * jnp.exp(btot - b)
    outs, upd = [], []
    for h in range(GLA_HEADS):
        ks = slice(h * GLA_DK, (h + 1) * GLA_DK)
        vh = v[:, h * GLA_DV:(h + 1) * GLA_DV]
        att = bdot_nt(qt[:, ks], kt[:, ks]) * mask
        outs.append(bdot(att, vh) + bdot_nt(qs[:, ks], st[:, ks]))
        upd.append(bdot_tn(vh, ke[:, ks]))
    st_new = st * jnp.exp(btot) + jnp.concatenate(upd, axis=1)
    return jnp.concatenate(outs, axis=1), st_new


def gla_fwd(p, z, n_lat):
    t = p.shape[0]
    c = GLA_CHUNK
    nc, nlc = t // c, n_lat // c
    kw, vw = GLA_HEADS * GLA_DK, GLA_HEADS * GLA_DV

    def body(q_ref, k_ref, v_ref, la_ref, o_ref, ssave_ref, s_s):
        d, n = pl.program_id(0), pl.program_id(1)

        @pl.when(n == 0)
        def _():
            s_s[...] = jnp.zeros_like(s_s)

        ssave_ref[0, 0] = s_s[...]
        o, s_new = gla_chunk(q_ref[...], k_ref[...], v_ref[...], la_ref[...], s_s[...], _gla_mask(d))
        o_ref[...] = o
        s_s[...] = s_new

    def chunk_of(d, n):
        return _scan_chunk(d, n, nc, nlc)

    return pl.pallas_call(
        body, name="gla_fwd", grid=(2, nc),
        in_specs=[pl.BlockSpec((c, kw), lambda d, n: (chunk_of(d, n), _GQ_BLK)),
                  pl.BlockSpec((c, kw), lambda d, n: (chunk_of(d, n), _GK_BLK)),
                  pl.BlockSpec((c, vw), lambda d, n: (chunk_of(d, n), _GV_BLK)),
                  pl.BlockSpec((c, kw), lambda d, n: (chunk_of(d, n), _LA_BLK + d))],
        out_specs=[pl.BlockSpec((c, vw), lambda d, n: (chunk_of(d, n), d)),
                   pl.BlockSpec((1, 1, GLA_DV, kw), lambda d, n: (d, n, 0, 0))],
        out_shape=[jax.ShapeDtypeStruct((t, 2 * vw), F32), jax.ShapeDtypeStruct((2, nc, GLA_DV, kw), F32)],
        scratch_shapes=[pltpu.VMEM((GLA_DV, kw), F32)],
        compiler_params=_params(("parallel", "arbitrary")),
    )(p, z, z, p)


def gla_bwd(p, z, states, do, n_lat):
    t = p.shape[0]
    c = GLA_CHUNK
    nc, nlc = t // c, n_lat // c
    kw, vw = GLA_HEADS * GLA_DK, GLA_HEADS * GLA_DV

    def body(q_ref, k_ref, v_ref, la_ref, s_ref, do_ref, dq_ref, dk_ref, dv_ref, dla_ref, ds_s):
        d, n = pl.program_id(0), pl.program_id(1)

        @pl.when(n == 0)
        def _():
            ds_s[...] = jnp.zeros_like(ds_s)

        f = functools.partial(gla_chunk, mask=_gla_mask(d))
        _, vjp = jax.vjp(f, q_ref[...], k_ref[...], v_ref[...], la_ref[...], s_ref[0, 0])
        dq_ref[...], dk_ref[...], dv_ref[...], dla_ref[...], ds_s[...] = vjp((do_ref[...], ds_s[...]))

    def chunk_of(d, n):
        return _scan_chunk(d, nc - 1 - n, nc, nlc)

    k_out = pl.BlockSpec((c, kw), lambda d, n: (chunk_of(d, n), d))
    return pl.pallas_call(
        body, name="gla_bwd", grid=(2, nc),
        in_specs=[pl.BlockSpec((c, kw), lambda d, n: (chunk_of(d, n), _GQ_BLK)),
                  pl.BlockSpec((c, kw), lambda d, n: (chunk_of(d, n), _GK_BLK)),
                  pl.BlockSpec((c, vw), lambda d, n: (chunk_of(d, n), _GV_BLK)),
                  pl.BlockSpec((c, kw), lambda d, n: (chunk_of(d, n), _LA_BLK + d)),
                  pl.BlockSpec((1, 1, GLA_DV, kw), lambda d, n: (d, nc - 1 - n, 0, 0)),
                  pl.BlockSpec((c, vw), lambda d, n: (chunk_of(d, n), 0))],
        out_specs=[k_out, k_out, pl.BlockSpec((c, vw), lambda d, n: (chunk_of(d, n), d)), k_out],
        out_shape=[jax.ShapeDtypeStruct((t, 2 * kw), F32), jax.ShapeDtypeStruct((t, 2 * kw), F32),
                   jax.ShapeDtypeStruct((t, 2 * vw), F32), jax.ShapeDtypeStruct((t, 2 * kw), F32)],
        scratch_shapes=[pltpu.VMEM((GLA_DV, kw), F32)],
        compiler_params=_params(("parallel", "arbitrary")),
    )(p, z, z, p, states, do)


def _adam_tile(w, g, m, v):
    m = ADAM_B1 * m + (1.0 - ADAM_B1) * g
    v = ADAM_B2 * v + (1.0 - ADAM_B2) * (g * g)
    m_hat = m / (1.0 - ADAM_B1 ** ADAM_STEP)
    v_hat = v / (1.0 - ADAM_B2 ** ADAM_STEP)
    delta = -ADAM_LR * (m_hat / (jnp.sqrt(v_hat) + ADAM_EPS) + ADAM_WD * w)
    return delta, m, v


def adamw(name, w, g, m, v):
    shape = w.shape
    cols = shape[-1] if w.ndim > 1 and shape[-1] >= LANES else int(np.prod(shape))
    rows = int(np.prod(shape)) // cols
    tr = rows
    for cand in (512, 256, 128, 64, 32, 16, 8):
        if rows % cand == 0 and cand * cols * 4 <= (1 << 20):
            tr = cand
            break
    flat = [a.reshape(rows, cols) for a in (w, g, m, v)]

    def body(w_ref, g_ref, m_ref, v_ref, d_ref, mo_ref, vo_ref):
        d_ref[...], mo_ref[...], vo_ref[...] = _adam_tile(w_ref[...], g_ref[...], m_ref[...], v_ref[...])

    spec = pl.BlockSpec((tr, cols), lambda i: (i, 0))
    outs = pl.pallas_call(
        body, name=name, grid=(rows // tr,),
        in_specs=[spec] * 4, out_specs=[spec] * 3,
        out_shape=[jax.ShapeDtypeStruct((rows, cols), F32)] * 3,
        compiler_params=_params(("parallel",)),
    )(*flat)
    return tuple(o.reshape(shape) for o in outs)


def adamw_layers(name, w, grads, m, v):
    depth, rows, cols = w.shape
    tr = _rows_tile(rows, cols)
    nb = rows // tr

    def body(w_ref, m_ref, v_ref, *rest):
        g_refs, (g_ref, d_ref, mo_ref, vo_ref) = rest[:depth], rest[depth:]
        l = pl.program_id(0)
        for k in range(depth):
            @pl.when(l == k)
            def _():
                g = g_refs[k][...]
                g_ref[...] = g
                d_ref[...], mo_ref[...], vo_ref[...] = _adam_tile(w_ref[...], g, m_ref[...], v_ref[...])

    def layer_grad(k):
        return pl.BlockSpec((tr, cols), lambda l, i: (jnp.where(l < k, 0, jnp.where(l == k, i, nb - 1)), 0))

    spec = pl.BlockSpec((tr, cols), lambda l, i: (l * nb + i, 0))
    flat = [a.reshape(depth * rows, cols) for a in (w, m, v)]
    g_all, delta, new_m, new_v = pl.pallas_call(
        body, name=name, grid=(depth, nb),
        in_specs=[spec] * 3 + [layer_grad(k) for k in range(depth)], out_specs=[spec] * 4,
        out_shape=[jax.ShapeDtypeStruct((depth * rows, cols), F32)] * 4,
        compiler_params=_params(("arbitrary", "arbitrary")),
    )(*flat, *grads)
    return tuple(a.reshape(w.shape) for a in (delta, new_m, new_v)), g_all.reshape(w.shape)


MESH = pl.DeviceIdType.MESH
_HBM = pl.BlockSpec(memory_space=pltpu.HBM)
N_CHIPS = 4
N_DEV = 8


def _place():
    x, y, c = lax.axis_index("x"), lax.axis_index("y"), lax.axis_index("c")
    chips = [(1 - x, y), (x, 1 - y), (1 - x, 1 - y)]
    return x, y, c, chips


def _remote(src, dst, send_sem, recv_sem, to):
    return pltpu.make_async_remote_copy(src_ref=src, dst_ref=dst, send_sem=send_sem, recv_sem=recv_sem,
                                        device_id=to, device_id_type=MESH)


def all_gather_small(name, v):
    m_per, n = v.shape

    def body(x_ref, out_ref, send_sems, recv_sems, local_sem):
        x, y, c, chips = _place()
        me, sibling = (x, y, c), (x, y, 1 - c)

        def rows(px, py, pc):
            return out_ref.at[pl.ds((4 * px + 2 * py + pc) * m_per, m_per), :]

        def copy(k, block, to, src=None):
            return _remote(rows(*block) if src is None else src, rows(*block), send_sems.at[k], recv_sems.at[k], to)

        mine = pltpu.make_async_copy(x_ref, rows(*me), local_sem)
        mine.start()
        first = [copy(0, me, sibling, src=x_ref)]
        first += [copy(1 + j, me, (*chip, c), src=x_ref) for j, chip in enumerate(chips)]
        for cp in first:
            cp.start()
        passed = [copy(4 + j, (*chip, c), sibling) for j, chip in enumerate(chips)]
        for j, chip in enumerate(chips):
            copy(1 + j, (*chip, c), me).wait_recv()
            passed[j].start()
        copy(0, sibling, me).wait_recv()
        for j, chip in enumerate(chips):
            copy(4 + j, (*chip, 1 - c), me).wait_recv()
        for cp in first + passed:
            cp.wait_send()
        mine.wait()

    return pl.pallas_call(
        body, name=name,
        out_shape=jax.ShapeDtypeStruct((N_DEV * m_per, n), v.dtype),
        in_specs=[pl.BlockSpec(memory_space=pltpu.VMEM)],
        out_specs=pl.BlockSpec(memory_space=pltpu.VMEM),
        scratch_shapes=[pltpu.SemaphoreType.DMA((7,)), pltpu.SemaphoreType.DMA((7,)), pltpu.SemaphoreType.DMA],
        compiler_params=pltpu.CompilerParams(vmem_limit_bytes=VMEM_LIMIT),
    )(v)


_SEM = pl.BlockSpec(memory_space=pltpu.SEMAPHORE)
_SPLIT_COPY = pltpu.CompilerParams(has_side_effects=pltpu.SideEffectType.DATAFLOW_SIDE_EFFECTING)


class CopyPlan(NamedTuple):
    copies: object
    n: int
    in_place: bool = False


def _gather_copies(x_ref, land_ref, x, y, c, chips):
    half = x_ref.shape[0] // 2
    rows = pl.ds(c * half, half)
    return [(x_ref.at[rows, :], land_ref.at[2 * x + y, rows, :], (*chip, c), land_ref.at[2 * chip[0] + chip[1], rows, :])
            for chip in chips]


def _pass_copies(land_ref, _, x, y, c, chips):
    half = land_ref.shape[1] // 2
    mine, other = pl.ds(c * half, half), pl.ds((1 - c) * half, half)
    return [(land_ref.at[2 * chip[0] + chip[1], mine, :], land_ref.at[2 * chip[0] + chip[1], mine, :], (x, y, 1 - c),
             land_ref.at[2 * chip[0] + chip[1], other, :]) for chip in chips]


def _sibling_half_copies(p_ref, land_ref, x, y, c, chips):
    half = p_ref.shape[1] // 2
    return [(p_ref.at[:, pl.ds((1 - c) * half, half), :], land_ref, (x, y, 1 - c), land_ref)]


def _scatter_copies(s_ref, land_ref, x, y, c, chips):
    return [(s_ref.at[2 * chip[0] + chip[1]], land_ref.at[j], (*chip, c), land_ref.at[j]) for j, chip in enumerate(chips)]


def _join_copies(buf_ref, _, x, y, c, chips):
    half = buf_ref.shape[0] // 2
    mine = buf_ref.at[pl.ds(c * half, half), :]
    return [(mine, mine, (x, y, 1 - c), buf_ref.at[pl.ds((1 - c) * half, half), :])]


GATHER = CopyPlan(_gather_copies, 3)
PASS_ON = CopyPlan(_pass_copies, 3, in_place=True)
SIBLING_HALF = CopyPlan(_sibling_half_copies, 1)
SCATTER = CopyPlan(_scatter_copies, 3)
JOIN = CopyPlan(_join_copies, 1, in_place=True)


def split_start(name, plan, srcs, land_shapes=None, after=None):
    nt = len(srcs)
    arrays = [pltpu.with_memory_space_constraint(s, pltpu.HBM) for s in srcs]
    if not plan.in_place:
        arrays += [pltpu.with_memory_space_constraint(lax.empty(shape, s.dtype), pltpu.HBM) for shape, s in zip(land_shapes, srcs)]
    na = len(arrays)
    behind = [] if after is None else [after]
    n_in = na + len(behind)

    def body(*refs):
        x_refs = refs[:nt]
        land_refs = x_refs if plan.in_place else refs[nt:na]
        send, recv = refs[n_in:n_in + nt], refs[n_in + nt:n_in + 2 * nt]
        x, y, c, chips = _place()
        for t in range(nt):
            for j, (src, dst, to, _) in enumerate(plan.copies(x_refs[t], land_refs[t], x, y, c, chips)):
                _remote(src, dst, send[t].at[j], recv[t].at[j], to).start()
        refs[-1][...] = jnp.zeros_like(refs[-1])

    outs = pl.pallas_call(
        body, name=name,
        out_shape=tuple([pltpu.SemaphoreType.DMA((plan.n,))] * (2 * nt) + [pltpu.HBM(a.shape, a.dtype) for a in arrays]
                        + [jax.ShapeDtypeStruct((8, LANES), F32)]),
        in_specs=[_HBM] * na + [pl.BlockSpec(memory_space=pl.ANY)] * len(behind),
        out_specs=tuple([_SEM] * (2 * nt) + [_HBM] * na + [pl.BlockSpec(memory_space=pltpu.VMEM)]),
        input_output_aliases={i: 2 * nt + i for i in range(na)},
        compiler_params=_SPLIT_COPY,
    )(*arrays, *behind)
    groups = [(outs[t], outs[nt + t]) + tuple(outs[2 * nt + t + k * nt] for k in range(na // nt)) for t in range(nt)]
    return groups, outs[-1]


def split_wait(name, plan, group, after):
    send, recv, *arrays = group
    na = len(arrays)

    def body(*refs):
        x_ref, land_ref = refs[0], refs[na - 1]
        send_sem, recv_sem = refs[na], refs[na + 1]
        x, y, c, chips = _place()
        for j, (s, _, to, arrival) in enumerate(plan.copies(x_ref, land_ref, x, y, c, chips)):
            cp = _remote(s, arrival, send_sem.at[j], recv_sem.at[j], to)
            cp.wait_send()
            cp.wait_recv()

    return pl.pallas_call(
        body, name=name,
        out_shape=tuple(pltpu.HBM(a.shape, a.dtype) for a in arrays),
        in_specs=tuple([_HBM] * na + [_SEM, _SEM, pl.BlockSpec(memory_space=pl.ANY)]), out_specs=tuple([_HBM] * na),
        input_output_aliases={i: i for i in range(na)}, compiler_params=_SPLIT_COPY,
    )(*arrays, send, recv, after)


def _rows_tile(rows, cols):
    for cand in (512, 256, 128, 64, 32, 16):
        if rows % cand == 0 and cand * cols * 4 <= (1 << 21):
            return cand
    return rows


def add_sibling_half(name, pieces, from_sibling, core):
    n, h, cols = from_sibling.shape
    tr = _rows_tile(h, cols // 2)
    nb = h // tr

    def body(c_ref, a_ref, b_ref, o_ref):
        o_ref[...] = (a_ref[...].astype(F32) + b_ref[...].astype(F32)).astype(o_ref.dtype)

    blk = pl.BlockSpec((1, tr, cols), lambda q, i, c_ref: (q, i, 0))
    return pl.pallas_call(
        body, name=name,
        grid_spec=pltpu.PrefetchScalarGridSpec(
            num_scalar_prefetch=1, grid=(n, nb),
            in_specs=[pl.BlockSpec((1, tr, cols), lambda q, i, c_ref: (q, c_ref[0] * nb + i, 0)), blk], out_specs=blk),
        out_shape=jax.ShapeDtypeStruct((n, h, cols), BF16),
        compiler_params=_params(("parallel", "parallel")),
    )(core.reshape(1).astype(jnp.int32), pieces, from_sibling)


def add_chip_sums(name, chip_sums, from_chips, chip, core):
    _, h, cols = chip_sums.shape
    tr = _rows_tile(h, cols)
    nb = h // tr

    def body(s_ref, own_ref, r0_ref, r1_ref, r2_ref, o_ref):
        acc = own_ref[0].astype(F32) + r0_ref[0].astype(F32)
        o_ref[...] = acc + r1_ref[0].astype(F32) + r2_ref[0].astype(F32)

    def got(j):
        return pl.BlockSpec((1, tr, cols), lambda i, s_ref: (j, i, 0))

    return pl.pallas_call(
        body, name=name,
        grid_spec=pltpu.PrefetchScalarGridSpec(
            num_scalar_prefetch=1, grid=(nb,),
            in_specs=[pl.BlockSpec((1, tr, cols), lambda i, s_ref: (s_ref[0], i, 0)), got(0), got(1), got(2)],
            out_specs=pl.BlockSpec((tr, cols), lambda i, s_ref: (s_ref[1] * nb + i, 0))),
        out_shape=jax.ShapeDtypeStruct((2 * h, cols), F32),
        compiler_params=_params(("parallel",)),
    )(jnp.stack([chip, core]).astype(jnp.int32), chip_sums, from_chips, from_chips, from_chips)


def sum_device_blocks(name, g):
    n = g.shape[1]

    def body(g_ref, o_ref):
        acc = g_ref[0:8, :]
        for d in range(1, N_DEV):
            acc = acc + g_ref[8 * d:8 * (d + 1), :]
        o_ref[...] = acc

    return pl.pallas_call(body, name=name, out_shape=jax.ShapeDtypeStruct((8, n), F32),
                          compiler_params=pltpu.CompilerParams(vmem_limit_bytes=VMEM_LIMIT))(g)


class LayerWeights(NamedTuple):
    norm1_g: jax.Array
    q_g: jax.Array
    k_g: jax.Array
    lg: jax.Array
    ret_g: jax.Array
    gate_up: jax.Array
    gate_b: jax.Array
    gla_g: jax.Array
    norm2_g: jax.Array
    conv_w: jax.Array
    conv_b: jax.Array


def _mod(mods, k):
    return mods[:, k:k + 1, :]


def out_view(l, tb):
    rows = D_MODEL // N_CHIPS
    if tb:
        return BView(n=D_MODEL, k=D_MODEL, tn=rows, tk=D_MODEL, index_map=lambda i, j, kk: (j, l, kk))
    return BView(n=D_MODEL, k=D_MODEL, tn=1024, tk=rows, index_map=lambda i, j, kk: (kk, l, j))


def down_view(l, f, tb):
    rows = f // N_CHIPS
    if tb:
        return BView(n=f, k=D_MODEL, tn=rows, tk=D_MODEL, index_map=lambda i, j, kk: (j, l, kk))
    return BView(n=D_MODEL, k=f, tn=1024, tk=rows, index_map=lambda i, j, kk: (kk, l, j))


def up_view(l, f, part=None):
    cols = 2 * f // N_CHIPS
    tc = _pick(cols, (1408, 1024, 512, 256))
    nbc = cols // tc
    if part is None:
        return BView(n=2 * f, k=D_MODEL, tn=tc, tk=D_MODEL, index_map=lambda i, j, kk: (j // nbc, l, j % nbc))
    nnb = D_MODEL // 1024
    return BView(n=D_MODEL, k=f, tn=1024, tk=tc, index_map=lambda i, j, kk: (2 * part + kk // nbc, l * nnb + j, kk % nbc))


def up_grad_view(f, part, into):
    cols = f // 2
    tn = _pick(cols, (1408, 1024, 512, 256))
    nbc = cols // tn
    return OView((N_CHIPS, D_MODEL, cols), lambda i, j, kk: (2 * part + j // nbc, i, j % nbc), tn, into)


def ada_view(l, n_ada, tb):
    if tb:
        return BView(n=D_MODEL, k=n_ada, tn=1024, tk=n_ada, index_map=lambda i, j, kk: (l, j, 0))
    return BView(n=n_ada, k=D_MODEL, tn=1024, tk=D_MODEL, index_map=lambda i, j, kk: (l, 0, j))


def _prep_args(z, zg, cos, sin, w):
    rows = [Row(z, Z_AV, 0), Row(z, 512, Z_RQ // 512), Row(z, 512, Z_RK // 512), Row(z, 256, Z_GQ // 256),
            Row(zg, LANES, 0), Row(cos, HEAD_DIM, 0, False), Row(sin, HEAD_DIM, 0, False)]
    return rows, [Par(w.q_g), Par(w.k_g), Par(w.gate_up), Par(w.gate_b)]


def _post_args(o_att, o_ret, o_gla, z, w):
    rows = [Row(o_att, 1024), Row(o_ret, 512, 0), Row(o_ret, 512, 1, False), Row(o_gla, 512, 0), Row(o_gla, 512, 1, False),
            Row(z, 512, Z_RG // 512), Row(z, 512, Z_GR // 512)]
    return rows, [Par(w.ret_g), Par(w.gla_g)]


def layer_fwd(l, xs, mods, w, fetch, cos, sin, n_lat):
    t, d = xs.shape
    tag = f"l{l}_"
    nm1 = [Par(w.norm1_g), Par(_mod(mods, 0), True), Par(_mod(mods, 1), True)]
    (h,) = row_map(tag + "norm1", normmod_tile, [Row(xs, d)], nm1, [(d, BF16)], t, n_lat)
    (w_main, w_gate), started = fetch("w_in", h)
    z = matmul(tag + "in_proj", h, w_main, after=started)
    zg = matmul(tag + "gate_proj", h, w_gate)
    rows, pars = _prep_args(z, zg, cos, sin, w)
    (p,) = row_map(tag + "prep", prep_tile, rows, pars, [(P_W, F32)], t, n_lat)
    o_att, lse = attn_fwd(p, z, n_lat)
    o_ret, s_ret = ret_fwd(p, z, w.lg, n_lat)
    o_gla, s_gla = gla_fwd(p, z, n_lat)
    rows, pars = _post_args(o_att, o_ret, o_gla, z, w)
    (m,) = row_map(tag + "post", post_tile, rows, pars, [(d, BF16)], t, n_lat)
    g_out, started = fetch("w_out", m)
    y = matmul(tag + "out_proj", m, g_out, view=out_view(0, False), after=started)
    (x1,) = row_map(tag + "resid1", resid_tile, [Row(xs, d), Row(y, d)], [Par(_mod(mods, 2), True)], [(d, F32)], t, n_lat)
    nm2 = [Par(w.norm2_g), Par(_mod(mods, 3), True), Par(_mod(mods, 4), True)]
    (h2,) = row_map(tag + "norm2", normmod_tile, [Row(x1, d)], nm2, [(d, BF16)], t, n_lat)
    f = w.conv_b.shape[1]
    g_up, started = fetch("w_up", h2)
    u = matmul(tag + "up_proj", h2, g_up, view=up_view(0, f), after=started, out_dtype=BF16)
    g = convglu(tag + "convglu", u, w.conv_w, w.conv_b, n_lat)
    g_down, started = fetch("w_down", g)
    yd = matmul(tag + "down_proj", g, g_down, view=down_view(0, f, False), after=started)
    (x2,) = row_map(tag + "resid2", resid_tile, [Row(x1, d), Row(yd, d)], [Par(_mod(mods, 5), True)], [(d, F32)], t, n_lat)
    saved = dict(xs=xs, h=h, z=z, zg=zg, p=p, o_att=o_att, lse=lse, o_ret=o_ret, s_ret=s_ret, o_gla=o_gla, s_gla=s_gla,
                 m=m, y=y, x1=x1, h2=h2, u=u, g=g, yd=yd, w_main=w_main, w_gate=w_gate, g_out=g_out, g_up=g_up, g_down=g_down)
    return x2, saved


def _sum_dirs(a):
    w = a.shape[1] // 2
    return a[:, :w] + a[:, w:]


def layer_bwd(l, dx2, s, mods, w, cos, sin, n_lat, grad_ready):
    t, d = dx2.shape
    tag = f"l{l}_b_"
    dyd, dgate5 = row_vjp(tag + "resid2", gated_tile, [Row(s["yd"], d)], [Par(_mod(mods, 5), True)], [dx2], t, n_lat,
                          row_grad_dtype=BF16)
    f = w.conv_b.shape[1]
    dg = matmul(tag + "down_dx", dyd, s["g_down"], tb=True, view=down_view(0, f, True))
    dw_down = matmul(tag + "down_dw", s["g"], dyd, ta=True, out_dtype=BF16)
    da, dv, dcw, dcb = convglu_bwd(tag + "convglu", s["u"], w.conv_w, w.conv_b, dg, n_lat)
    dh2 = matmul(tag + "up_dx_gate", da, s["g_up"], tb=True, view=up_view(0, f, 0))
    dh2 = matmul(tag + "up_dx_value", dv, s["g_up"], tb=True, view=up_view(0, f, 1), add=dh2)
    dw_up = matmul(tag + "up_dw_gate", s["h2"], da, ta=True, out_dtype=BF16, o_view=up_grad_view(f, 0, None))
    dw_up = matmul(tag + "up_dw_value", s["h2"], dv, ta=True, out_dtype=BF16, o_view=up_grad_view(f, 1, dw_up))
    started = grad_ready("ffn", dict(w_up=dw_up, w_down=dw_down))
    nm2 = [Par(w.norm2_g), Par(_mod(mods, 3), True), Par(_mod(mods, 4), True)]
    dx1, dg2, dshift3, dscale4 = row_vjp(tag + "norm2", normmod_tile, [Row(s["x1"], d)], nm2, [dh2], t, n_lat,
                                         add_to_first=dx2, after=started)
    dy, dgate2 = row_vjp(tag + "resid1", gated_tile, [Row(s["y"], d)], [Par(_mod(mods, 2), True)], [dx1], t, n_lat,
                         row_grad_dtype=BF16)
    dm = matmul(tag + "out_dx", dy, s["g_out"], tb=True, view=out_view(0, True))
    dw_out = matmul(tag + "out_dw", s["m"], dy, ta=True, out_dtype=BF16)
    rows, pars = _post_args(s["o_att"], s["o_ret"], s["o_gla"], s["z"], w)
    started = grad_ready("w_out", dict(w_out=dw_out))
    do_att, do_ret, do_gla, d_rg, d_gr, d_ret_g, d_gla_g = row_vjp(tag + "post", post_tile, rows, pars, [dm], t, n_lat, after=started)
    dq_a, dk_a, dv_a = attn_bwd(s["p"], s["z"], s["o_att"], s["lse"], do_att, n_lat)
    dq_r, dk_r, dv_r, dlg = ret_bwd(s["p"], s["z"], w.lg, s["s_ret"], do_ret, n_lat)
    dq_g, dk_g, dv_g, dla = gla_bwd(s["p"], s["z"], s["s_gla"], do_gla, n_lat)
    dp = jnp.concatenate([dq_a, dk_a, _sum_dirs(dq_g), _sum_dirs(dq_r), _sum_dirs(dk_r), dla], axis=1)
    rows, pars = _prep_args(s["z"], s["zg"], cos, sin, w)
    d_zqk, d_zrq, d_zrk, d_zgq, dzg, d_qg, d_kg, d_up, d_gb = row_vjp(tag + "prep", prep_tile, rows, pars, [dp], t, n_lat)
    dz = jnp.concatenate([d_zqk, dv_a, d_zrq, d_zrk, _sum_dirs(dv_r), d_rg, d_zgq, _sum_dirs(dk_g), _sum_dirs(dv_g), d_gr], axis=1)
    dz, dzg = dz.astype(BF16), dzg.astype(BF16)
    dh_gate = matmul(tag + "gate_dx", dzg, s["w_gate"], tb=True)
    dh = matmul(tag + "in_dx", dz, s["w_main"], tb=True, add=dh_gate)
    dw_main = matmul(tag + "in_dw", s["h"], dz, ta=True, out_dtype=BF16)
    dw_gate = matmul(tag + "gate_dw", s["h"], dzg, ta=True, out_dtype=BF16)
    started = grad_ready("w_in", dict(w_main=dw_main, w_gate=dw_gate))
    nm1 = [Par(w.norm1_g), Par(_mod(mods, 0), True), Par(_mod(mods, 1), True)]
    dx, dg1, dshift0, dscale1 = row_vjp(tag + "norm1", normmod_tile, [Row(s["xs"], d)], nm1, [dh], t, n_lat,
                                        add_to_first=dx1, after=started)
    dmods = jnp.concatenate([dshift0, dscale1, dgate2, dshift3, dscale4, dgate5], axis=1)
    grads = dict(w_main=dw_main, w_gate=dw_gate, w_out=dw_out, w_up=dw_up, w_down=dw_down, norm1_g=dg1, q_g=d_qg, k_g=d_kg,
                 lg=dlg, ret_g=d_ret_g, gate_up=d_up, gate_b=d_gb, gla_g=d_gla_g, norm2_g=dg2, conv_w=dcw, conv_b=dcb)
    return dx, dmods, grads


def rope_tables(n_lat, n_ctx):
    rows = n_lat // GRID_W
    row = jnp.repeat(jnp.arange(rows, dtype=F32), GRID_W)
    col = jnp.tile(jnp.arange(GRID_W, dtype=F32), rows)
    n_freq = HEAD_DIM // 4
    inv_freq = ROPE_THETA ** (-jnp.arange(n_freq, dtype=F32) / n_freq)
    ang = jnp.concatenate([row[:, None] * inv_freq, col[:, None] * inv_freq], axis=-1)
    cos, sin = jnp.cos(ang), jnp.sin(ang)
    cos = jnp.concatenate([jnp.concatenate([cos, cos], axis=1), jnp.ones((n_ctx, HEAD_DIM), F32)], axis=0)
    sin = jnp.concatenate([jnp.concatenate([-sin, sin], axis=1), jnp.zeros((n_ctx, HEAD_DIM), F32)], axis=0)
    return cos, sin


def local_step(xs, target, mods, weights, fetch, final_g, n_lat, grad_ready):
    t, d = xs.shape
    cos, sin = rope_tables(n_lat, t - n_lat)
    saved = []
    h = xs
    for l, w in enumerate(weights):
        h, s = layer_fwd(l, h, mods[l], w, functools.partial(fetch, l), cos, sin, n_lat)
        saved.append(s)
    loss, dlat, dgf = final_loss(h, target, final_g, n_lat)
    dx = jnp.concatenate([dlat, jnp.zeros((t - n_lat, d), F32)], axis=0)
    dmods, grads = [None] * len(weights), [None] * len(weights)
    for l in reversed(range(len(weights))):
        dx, dmods[l], grads[l] = layer_bwd(l, dx, saved[l], mods[l], weights[l], cos, sin, n_lat, functools.partial(grad_ready, l))
    return loss, dx, dmods, grads, dgf


WEIGHT_NAMES = ("c_ctx", "ada_w", "ada_b", "norm1_g", "w_in", "q_norm_g", "k_norm_g", "ret_log_decay", "ret_norm_g",
                "gla_gate_up", "gla_gate_b", "gla_norm_g", "w_out", "norm2_g", "w_up", "conv_w", "conv_b", "w_down", "final_norm_g")
PACK_QUANTUM = 8 * LANES


def _pack(arrays):
    flat = jnp.concatenate([a.reshape(-1).astype(F32) for a in arrays])
    n = -(-flat.shape[0] // PACK_QUANTUM) * PACK_QUANTUM
    return jnp.pad(flat, (0, n - flat.shape[0])).reshape(8, n // 8)


def _unpack(flat2d, shapes):
    out, at = [], 0
    for s in shapes:
        size = int(np.prod(s))
        out.append(flat2d[:, at:at + size].reshape((flat2d.shape[0],) + tuple(s)))
        at += size
    return out


def _per_device(gathered):
    return gathered.reshape(N_DEV, -1)


def _from_chips(per_device, axis):
    chips = per_device[0::2]
    moved = jnp.moveaxis(chips, 0, axis)
    shape = moved.shape
    return moved.reshape(shape[:axis] + (shape[axis] * shape[axis + 1],) + shape[axis + 2:])


def kernel(x, c, ctx, c_ctx, ada_w, ada_b, norm1_g, w_in, q_norm_g, k_norm_g, ret_log_decay, ret_norm_g, gla_gate_up, gla_gate_b, gla_norm_g, w_out, norm2_g, w_up, conv_w, conv_b, w_down, final_norm_g, loss_target, m_c_ctx, m_ada_w, m_ada_b, m_norm1_g, m_w_in, m_q_norm_g, m_k_norm_g, m_ret_log_decay, m_ret_norm_g, m_gla_gate_up, m_gla_gate_b, m_gla_norm_g, m_w_out, m_norm2_g, m_w_up, m_conv_w, m_conv_b, m_w_down, m_final_norm_g, v_c_ctx, v_ada_w, v_ada_b, v_norm1_g, v_w_in, v_q_norm_g, v_k_norm_g, v_ret_log_decay, v_ret_norm_g, v_gla_gate_up, v_gla_gate_b, v_gla_norm_g, v_w_out, v_norm2_g, v_w_up, v_conv_w, v_conv_b, v_w_down, v_final_norm_g):
    weights = dict(zip(WEIGHT_NAMES, (c_ctx, ada_w, ada_b, norm1_g, w_in, q_norm_g, k_norm_g, ret_log_decay, ret_norm_g,
                                      gla_gate_up, gla_gate_b, gla_norm_g, w_out, norm2_g, w_up, conv_w, conv_b, w_down, final_norm_g)))
    mom_m = dict(zip(WEIGHT_NAMES, (m_c_ctx, m_ada_w, m_ada_b, m_norm1_g, m_w_in, m_q_norm_g, m_k_norm_g, m_ret_log_decay, m_ret_norm_g,
                                    m_gla_gate_up, m_gla_gate_b, m_gla_norm_g, m_w_out, m_norm2_g, m_w_up, m_conv_w, m_conv_b, m_w_down, m_final_norm_g)))
    mom_v = dict(zip(WEIGHT_NAMES, (v_c_ctx, v_ada_w, v_ada_b, v_norm1_g, v_w_in, v_q_norm_g, v_k_norm_g, v_ret_log_decay, v_ret_norm_g,
                                    v_gla_gate_up, v_gla_gate_b, v_gla_norm_g, v_w_out, v_norm2_g, v_w_up, v_conv_w, v_conv_b, v_w_down, v_final_norm_g)))
    depth, d = norm1_g.shape
    assert d == D_MODEL and x.shape[0] == 1
    n_lat, n_ctx, f = x.shape[1], ctx.shape[1], conv_b.shape[1]
    assert n_lat % ROW_TILE == 0 and n_ctx % ROW_TILE == 0 and f % FFN_COL_TILE == 0 and f % N_CHIPS == 0
    n_in = w_in.shape[2]
    n_ada = ada_w.shape[2]
    xi, yi, ci = lax.axis_index("x"), lax.axis_index("y"), lax.axis_index("c")
    chip = 2 * xi + yi
    dev = 2 * chip + ci

    big = ("w_in", "w_out", "w_up", "w_down")
    order = [(l, name) for l in range(depth) for name in big]
    shards = [weights[name][l].astype(BF16) for l, name in order]
    passing = {}

    def pass_on(k, after):
        tag = "{1}{0}".format(*order[k])
        own, land = split_wait("gather_wait_" + tag, GATHER, in_flight[k], after)
        (moving,), started = split_start("gather_pass_" + tag, PASS_ON, [land])
        passing[k] = (own, moving)
        return started

    def fetch(l, name, after):
        k = order.index((l, name))
        if k == 0:
            pass_on(0, after)
        own, moving = passing.pop(k)
        (land,) = split_wait(f"gather_pass_wait_{name}{l}", PASS_ON, moving, after)
        started = pass_on(k + 1, after) if k + 1 < len(order) else None
        land = lax.dynamic_update_slice_in_dim(land, own[None], chip, axis=0)
        if name != "w_in":
            return land, started
        cols = jnp.concatenate([land[q] for q in range(N_CHIPS)], axis=1)
        return (cols[:, :N_MAIN], jnp.pad(cols[:, N_MAIN:], ((0, 0), (0, LANES - N_GATE)))), started

    small_shapes = [c.shape[1:], conv_w.shape, gla_gate_up.shape, gla_gate_b.shape]
    got = _per_device(all_gather_small("gather_small", _pack([c, conv_w, gla_gate_up, gla_gate_b])))
    c_all, conv_w_sh, gate_up_sh, gate_b_sh = _unpack(got, small_shapes)
    conv_w_full = _from_chips(conv_w_sh, 2)
    gate_up_full = _from_chips(gate_up_sh, 3)
    gate_b_full = _from_chips(gate_b_sh, 2)

    act = jnp.zeros((16, d), F32).at[0:N_DEV].set(jax.nn.silu(c_all)).at[N_DEV].set(jax.nn.silu(c_ctx))
    mod_sh = jnp.stack([matmul(f"ada_fwd{l}", act, ada_w, view=ada_view(l, n_ada, False)) for l in range(depth)])
    got = _per_device(all_gather_small("gather_mods", _pack([mod_sh])))
    (mod_sh_all,) = _unpack(got, [mod_sh.shape])
    mod_full = _from_chips(mod_sh_all, 2) + ada_b[:, None, :]
    mod_mine = lax.dynamic_index_in_dim(mod_full, dev, axis=1, keepdims=False)
    mods = [jnp.stack([mod_mine[l].reshape(N_MOD, d), mod_full[l, N_DEV].reshape(N_MOD, d)]) for l in range(depth)]
    in_flight, token = split_start("gather_start", GATHER, shards, [(N_CHIPS,) + s.shape for s in shards], after=mod_full)

    layer_w = []
    for l in range(depth):
        up = jnp.zeros((2, LANES, GLA_HEADS * GLA_DK), F32)
        up = up.at[0, 0:GLA_RANK].set(gate_up_full[l, 0]).at[1, GLA_RANK:2 * GLA_RANK].set(gate_up_full[l, 1])
        layer_w.append(LayerWeights(
            norm1_g=norm1_g[l].reshape(1, 1, d), q_g=q_norm_g[l].reshape(1, 1, HEAD_DIM), k_g=k_norm_g[l].reshape(1, 1, HEAD_DIM),
            lg=ret_log_decay[l].reshape(2, RET_HEADS, 1, 1), ret_g=ret_norm_g[l].reshape(1, 1, HEAD_DIM),
            gate_up=up.reshape(1, 2 * LANES, -1), gate_b=gate_b_full[l].reshape(1, 2, -1), gla_g=gla_norm_g[l].reshape(1, 1, HEAD_DIM),
            norm2_g=norm2_g[l].reshape(1, 1, d), conv_w=conv_w_full[l], conv_b=conv_b[l].reshape(1, f)))

    def pieces_of(name, g):
        if name == "w_in":
            full_cols = jnp.concatenate([g["w_main"], g["w_gate"][:, :N_GATE]], axis=1)
            return jnp.stack([full_cols[:, q * n_in:(q + 1) * n_in] for q in range(N_CHIPS)])
        if name == "w_up":
            return g["w_up"]
        return g[name].reshape(N_CHIPS, -1, d)

    groups = {"ffn": ("w_up", "w_down"), "w_out": ("w_out",), "w_in": ("w_in",)}
    reducing = {}
    to_sibling = []

    def sibling_arrived(after):
        started = None
        while to_sibling:
            l, group, in_flight_halves = to_sibling.pop(0)
            sums = []
            for name, halves in zip(groups[group], in_flight_halves):
                pieces, from_sibling = split_wait(f"rs_sibling_wait_{name}{l}", SIBLING_HALF, halves, after)
                sums.append(add_sibling_half(f"rs_add_sibling_{name}{l}", pieces, from_sibling, ci))
            in_flight_sums, token = split_start(f"rs_start_{group}{l}", SCATTER, sums, [(3,) + s.shape[1:] for s in sums])
            reducing.update({(l, name): grp for name, grp in zip(groups[group], in_flight_sums)})
            started = token if started is None else started + token
        return started

    def grad_ready(l, group, g):
        pieces = [pieces_of(name, g) for name in groups[group]]
        before = None if (l, group) == (0, "w_in") else sibling_arrived(pieces[0])
        in_flight_halves, started = split_start(f"rs_sibling_{group}{l}", SIBLING_HALF, pieces,
                                                [(N_CHIPS, pc.shape[1] // 2, pc.shape[2]) for pc in pieces])
        to_sibling.append((l, group, in_flight_halves))
        return started if before is None else started + before

    xs = jnp.concatenate([x[0], ctx[0]], axis=0) + token[0, 0]
    loss, dx, dmods, grads, dgf = local_step(xs, loss_target[0], mods, layer_w, fetch, final_norm_g.reshape(1, d), n_lat, grad_ready)

    def gate_up_grad(g):
        return jnp.stack([g[0, 0:GLA_RANK], g[0, LANES + GLA_RANK:LANES + 2 * GLA_RANK]])

    per_layer = [[dmods[l][0], dmods[l][1], grads[l]["norm1_g"], grads[l]["norm2_g"], grads[l]["q_g"], grads[l]["k_g"],
                  grads[l]["ret_g"], grads[l]["gla_g"], grads[l]["lg"], gate_up_grad(grads[l]["gate_up"]), grads[l]["gate_b"],
                  grads[l]["conv_w"], grads[l]["conv_b"]] for l in range(depth)]
    layer_shapes = [(N_MOD * d,), (N_MOD * d,), (d,), (d,), (HEAD_DIM,), (HEAD_DIM,), (HEAD_DIM,), (HEAD_DIM,), (2, RET_HEADS),
                    (2, GLA_RANK, GLA_HEADS * GLA_DK), (2, GLA_HEADS * GLA_DK), (3, f), (f,)]
    packed = _pack([a for lay in per_layer for a in lay] + [dgf, loss[0, 0:1]])
    gathered = all_gather_small("gather_small_grads", packed)
    every = _unpack(_per_device(gathered), layer_shapes * depth + [(d,), (1,)])
    total = _unpack(sum_device_blocks("sum_small_grads", gathered).reshape(1, -1), layer_shapes * depth + [(d,), (1,)])
    nl = len(layer_shapes)

    def tot(l, k):
        return total[l * nl + k][0]

    out = {"norm1_g": jnp.stack([tot(l, 2) for l in range(depth)]), "norm2_g": jnp.stack([tot(l, 3) for l in range(depth)]),
           "q_norm_g": jnp.stack([tot(l, 4) for l in range(depth)]), "k_norm_g": jnp.stack([tot(l, 5) for l in range(depth)]),
           "ret_norm_g": jnp.stack([tot(l, 6) for l in range(depth)]), "gla_norm_g": jnp.stack([tot(l, 7) for l in range(depth)]),
           "ret_log_decay": jnp.stack([tot(l, 8) for l in range(depth)]),
           "gla_gate_up": lax.dynamic_slice_in_dim(jnp.stack([tot(l, 9) for l in range(depth)]), chip * gla_gate_up.shape[3], gla_gate_up.shape[3], axis=3),
           "gla_gate_b": lax.dynamic_slice_in_dim(jnp.stack([tot(l, 10) for l in range(depth)]), chip * gla_gate_b.shape[2], gla_gate_b.shape[2], axis=2),
           "conv_w": lax.dynamic_slice_in_dim(jnp.stack([tot(l, 11) for l in range(depth)]), chip * conv_w.shape[2], conv_w.shape[2], axis=2),
           "conv_b": jnp.stack([tot(l, 12) for l in range(depth)]),
           "final_norm_g": total[depth * nl][0],
           "ada_b": jnp.stack([tot(l, 0) + tot(l, 1) for l in range(depth)])}
    loss_total = total[depth * nl + 1][0, 0]

    dmod_all = jnp.zeros((depth, 16, N_MOD * d), F32)
    for l in range(depth):
        dmod_all = dmod_all.at[l, 0:N_DEV].set(every[l * nl][:, :]).at[l, N_DEV].set(tot(l, 1))
    dmod_cols = lax.dynamic_slice_in_dim(dmod_all, chip * n_ada, n_ada, axis=2)
    for l in range(depth):
        slab = OView((depth, d, n_ada), functools.partial(lambda i, j, kk, l: (l, i, j), l=l), None, out.get("ada_w"))
        out["ada_w"] = matmul(f"ada_dw{l}", act, dmod_cols[l], ta=True, o_view=slab)
    dact = matmul("ada_dx0", dmod_cols[0], ada_w, tb=True, view=ada_view(0, n_ada, True))
    for l in range(1, depth):
        dact = matmul(f"ada_dx{l}", dmod_cols[l], ada_w, tb=True, view=ada_view(l, n_ada, True), add=dact)
    got = _per_device(all_gather_small("gather_dcctx", _pack([dact[N_DEV]])))
    sibling_arrived(got)
    got = got[0::2, :d]
    dsilu = got[0] + got[1] + got[2] + got[3]
    sig = jax.nn.sigmoid(c_ctx)
    out["c_ctx"] = dsilu * (sig + c_ctx * sig * (1.0 - sig))

    deltas, new_m, new_v = {}, {}, {}

    def update(name):
        out[name] = out[name].reshape(weights[name].shape)
        deltas[name], new_m[name], new_v[name] = adamw("adamw_" + name, weights[name], out[name], mom_m[name], mom_v[name])

    for name in WEIGHT_NAMES:
        if name not in big:
            update(name)
    behind = new_v["ada_w"]
    joining = []

    def joined(after):
        name, in_flight_halves = joining.pop()
        per_layer = [split_wait(f"rs_join_wait_{name}{l}", JOIN, grp, after)[0] for l, grp in enumerate(in_flight_halves)]
        (deltas[name], new_m[name], new_v[name]), out[name] = adamw_layers(
            "adamw_" + name, weights[name], per_layer, mom_m[name], mom_v[name])
        return new_v[name]

    for name in ("w_down", "w_up", "w_out", "w_in"):
        halves = []
        for l in range(depth):
            sums, got = split_wait(f"rs_wait_{name}{l}", SCATTER, reducing[(l, name)], behind)
            halves.append(add_chip_sums(f"rs_add_chips_{name}{l}", sums, got, chip, ci))
        in_flight_halves, _ = split_start("rs_join_" + name, JOIN, halves)
        if joining:
            behind = joined(behind)
        joining.append((name, in_flight_halves))
    joined(behind)
    grad_x = dx[:n_lat].reshape(x.shape)
    return (loss_total, grad_x, *[out[n] for n in WEIGHT_NAMES], *[deltas[n] for n in WEIGHT_NAMES],
            *[new_m[n] for n in WEIGHT_NAMES], *[new_v[n] for n in WEIGHT_NAMES])
```

```python
import functools
from typing import NamedTuple

import numpy as np
import jax
import jax.numpy as jnp
from jax import lax
from jax.experimental import pallas as pl
from jax.experimental.pallas import tpu as pltpu

F32 = jnp.float32
BF16 = jnp.bfloat16

D_MODEL = 2048
HEAD_DIM = 128
ATT_Q_HEADS = 8
ATT_KV_HEADS = 2
ATT_GROUP = ATT_Q_HEADS // ATT_KV_HEADS
RET_HEADS = 4
GLA_HEADS = 4
GLA_DK = 64
GLA_DV = 128
GLA_RANK = 16
GLA_TAU = 16.0
RET_CHUNK = 256
GLA_CHUNK = 128
GRID_W = 64
ROPE_THETA = 10000.0
N_MOD = 6
EPS = 1e-6
N_MAIN = 5120
N_GATE = 2 * GLA_RANK
LANES = 128
ROW_TILE = 256
FFN_COL_TILE = 256
VMEM_LIMIT = 56 * 1024 * 1024

ADAM_LR = 0.001
ADAM_B1 = 0.9
ADAM_B2 = 0.999
ADAM_EPS = 1e-08
ADAM_WD = 0.01
ADAM_STEP = 10

Z_AQ, Z_AK, Z_AV = 0, 1024, 1280
Z_RQ, Z_RK, Z_RV, Z_RG = 1536, 2048, 2560, 3072
Z_GQ, Z_GK, Z_GV, Z_GR = 3584, 3840, 4096, 4608
P_AQ, P_AK, P_GQ, P_RQ, P_RK, P_LA = 0, 1024, 1280, 1536, 2048, 2560
P_W = 3072


def _params(sem=None):
    return pltpu.CompilerParams(dimension_semantics=sem, vmem_limit_bytes=VMEM_LIMIT)


def _pick(n, cands):
    for c in cands:
        if n % c == 0:
            return c
    return n


_NN = (((1,), (0,)), ((), ()))
_NT = (((1,), (1,)), ((), ()))
_TN = (((0,), (0,)), ((), ()))


def _dg(a, b, dims):
    return lax.dot_general(a.astype(BF16), b.astype(BF16), dims, preferred_element_type=F32)


@jax.custom_vjp
def bdot(a, b):
    return _dg(a, b, _NN)


def _bdot_fwd(a, b):
    return _dg(a, b, _NN), (a, b)


def _bdot_bwd(res, ct):
    a, b = res
    return _dg(ct, b, _NT), _dg(a, ct, _TN)


bdot.defvjp(_bdot_fwd, _bdot_bwd)


@jax.custom_vjp
def bdot_nt(a, b):
    return _dg(a, b, _NT)


def _bdot_nt_fwd(a, b):
    return _dg(a, b, _NT), (a, b)


def _bdot_nt_bwd(res, ct):
    a, b = res
    return _dg(ct, b, _NN), _dg(ct, a, _TN)


bdot_nt.defvjp(_bdot_nt_fwd, _bdot_nt_bwd)


@jax.custom_vjp
def bdot_tn(a, b):
    return _dg(a, b, _TN)


def _bdot_tn_fwd(a, b):
    return _dg(a, b, _TN), (a, b)


def _bdot_tn_bwd(res, ct):
    a, b = res
    return _dg(b, ct, _NT), _dg(a, ct, _NN)


bdot_tn.defvjp(_bdot_tn_fwd, _bdot_tn_bwd)


def _split3(x):
    x1 = x.astype(BF16)
    r1 = x - x1.astype(F32)
    x2 = r1.astype(BF16)
    x3 = (r1 - x2.astype(F32)).astype(BF16)
    return x1, x2, x3


def _mask_dot(mask_bf16, x, dims):
    x1, x2, x3 = _split3(x)
    f = lambda t: lax.dot_general(mask_bf16, t, dims, preferred_element_type=F32)
    return f(x1) + f(x2) + f(x3)


@jax.custom_vjp
def mask_cumsum(mask, x):
    return _mask_dot(mask.astype(BF16), x, _NN)


def _mask_cumsum_fwd(mask, x):
    return mask_cumsum(mask, x), mask


def _mask_cumsum_bwd(mask, ct):
    return jnp.zeros_like(mask), _mask_dot(mask.astype(BF16), ct, _TN)


mask_cumsum.defvjp(_mask_cumsum_fwd, _mask_cumsum_bwd)


def _roll(x, shift, axis):
    return pltpu.roll(x, shift % x.shape[axis], axis)


@functools.partial(jax.custom_vjp, nondiff_argnums=(1, 2))
def roll(x, shift, axis):
    return _roll(x, shift, axis)


def _roll_fwd(x, shift, axis):
    return _roll(x, shift, axis), None


def _roll_bwd(shift, axis, _, ct):
    return (_roll(ct, -shift, axis),)


roll.defvjp(_roll_fwd, _roll_bwd)


def rms(x):
    return x * lax.rsqrt(jnp.mean(x * x, axis=-1, keepdims=True) + EPS)


def silu(x):
    return x * (0.5 + 0.5 * jnp.tanh(0.5 * x))


def log_sigmoid(x):
    return jnp.minimum(x, 0.0) - jnp.log(1.0 + jnp.exp(-jnp.abs(x)))


def rope(t, cos, sin):
    return t * cos + roll(t, HEAD_DIM // 2, 1) * sin


def _heads(x, n, width=HEAD_DIM):
    return [x[:, h * width:(h + 1) * width] for h in range(n)]


class Row(NamedTuple):
    arr: jax.Array
    width: int
    idx: int = 0
    diff: bool = True


class Par(NamedTuple):
    arr: jax.Array
    grouped: bool = False
    diff: bool = True


def _row_specs(rows, pars, tm, n_lat_tiles):
    def grp(i):
        return jnp.minimum(i // n_lat_tiles, 1)

    specs = [pl.BlockSpec((tm, r.width), functools.partial(lambda i, k: (i, k), k=r.idx)) for r in rows]
    for p in pars:
        blk = (1,) + p.arr.shape[1:]
        if p.grouped:
            specs.append(pl.BlockSpec(blk, lambda i: (grp(i), 0, 0)))
        else:
            specs.append(pl.BlockSpec(blk, lambda i: (0, 0, 0)))
    return specs


def row_map(name, fn, rows, pars, outs, n_rows, n_lat):
    tm = ROW_TILE
    nr, npar = len(rows), len(pars)

    def body(*refs):
        vals = [r[...] for r in refs[:nr]] + [p[0] for p in refs[nr:nr + npar]]
        res = fn(*vals)
        for o, v in zip(refs[nr + npar:], res):
            o[...] = v.astype(o.dtype)

    return pl.pallas_call(
        body, name=name, grid=(n_rows // tm,),
        in_specs=_row_specs(rows, pars, tm, n_lat // tm),
        out_specs=[pl.BlockSpec((tm, w), lambda i: (i, 0)) for w, _ in outs],
        out_shape=[jax.ShapeDtypeStruct((n_rows, w), dt) for w, dt in outs],
        compiler_params=_params(("arbitrary",)),
    )(*[r.arr for r in rows], *[p.arr for p in pars])


def row_vjp(name, fn, rows, pars, cts, n_rows, n_lat, add_to_first=None, row_grad_dtype=F32, after=None):
    tm = ROW_TILE
    nr, npar, nc = len(rows), len(pars), len(cts)
    n_lat_tiles = n_lat // tm
    args = list(rows) + list(pars)
    diff_pos = [k for k, a in enumerate(args) if a.diff]
    n_add = 0 if add_to_first is None else 1
    n_after = 0 if after is None else 1

    def body(*refs):
        i = pl.program_id(0)
        vals = [r[...] for r in refs[:nr]] + [p[0] for p in refs[nr:nr + npar]]
        ct_vals = tuple(c[...] for c in refs[nr + npar:nr + npar + nc])
        out_refs = refs[nr + npar + nc + n_add + n_after:]

        def g(*dv):
            full = list(vals)
            for k, v in zip(diff_pos, dv):
                full[k] = v
            return tuple(fn(*full))

        _, vjp = jax.vjp(g, *[vals[k] for k in diff_pos])
        grads = vjp(ct_vals)
        for n, (k, o, gr) in enumerate(zip(diff_pos, out_refs, grads)):
            if k < nr:
                o[...] = (gr + refs[nr + npar + nc][...] if (n == 0 and n_add) else gr).astype(o.dtype)
            else:
                first = (i == 0) | (i == n_lat_tiles) if args[k].grouped else (i == 0)

                @pl.when(first)
                def _():
                    o[0] = gr

                @pl.when(jnp.logical_not(first))
                def _():
                    o[0] += gr

    def grp(i):
        return jnp.minimum(i // n_lat_tiles, 1)

    out_specs, out_shape = [], []
    for k in diff_pos:
        a = args[k]
        if k < nr:
            out_specs.append(pl.BlockSpec((tm, a.width), lambda i: (i, 0)))
            out_shape.append(jax.ShapeDtypeStruct((n_rows, a.width), row_grad_dtype))
        else:
            blk = (1,) + a.arr.shape[1:]
            out_specs.append(pl.BlockSpec(blk, (lambda i: (grp(i), 0, 0)) if a.grouped else (lambda i: (0, 0, 0))))
            out_shape.append(jax.ShapeDtypeStruct(a.arr.shape, F32))
    extra = list(cts) + ([add_to_first] if n_add else [])
    ct_specs = [pl.BlockSpec((tm, c.shape[1]), lambda i: (i, 0)) for c in extra]
    if n_after:
        extra.append(after)
        ct_specs.append(pl.BlockSpec(memory_space=pl.ANY))
    return pl.pallas_call(
        body, name=name, grid=(n_rows // tm,),
        in_specs=_row_specs(rows, pars, tm, n_lat_tiles) + ct_specs,
        out_specs=out_specs, out_shape=out_shape,
        compiler_params=_params(("arbitrary",)),
    )(*[r.arr for r in rows], *[p.arr for p in pars], *extra)


class BView(NamedTuple):
    n: int
    k: int
    tn: int
    tk: int
    index_map: object
    lead: int = 1
    part_maps: tuple = ()


MATMUL_VMEM_BUDGET = 40 * 1024 * 1024


def _matmul_tiles(m, n, k, a_bytes, b_bytes, o_bytes):
    tms = [c for c in (1152, 1024, 768, 512, 256, 128) if m % c == 0] or [m]
    tns = [c for c in (2048, 1408, 1280, 1024, 768, 512, 256, 128) if n % c == 0] or [n]
    tks = [k] + [c for c in (2816, 2304, 2048, 1408, 1024, 512, 256, 128) if k % c == 0 and c < k]
    for tk in tks:
        fits = [(tm * tn, tm, tn) for tm in tms for tn in tns
                if 2 * (tm * tk * a_bytes + tk * tn * b_bytes + tm * tn * o_bytes) + 2 * tm * tn * 4 <= MATMUL_VMEM_BUDGET]
        if fits and (max(fits)[0] >= min(512 * 512, tms[0] * tns[0]) or tk == tks[-1]):
            _, tm, tn = max(fits)
            return tm, tn, tk
    raise ValueError(f"no matmul tiling for {(m, n, k)}")


class OView(NamedTuple):
    shape: tuple
    index_map: object
    tn: int = None
    into: object = None


def matmul(name, a, b, *, ta=False, tb=False, add=None, out_dtype=F32, view=None, o_view=None, after=None):
    m = a.shape[1] if ta else a.shape[0]
    o_bytes = jnp.dtype(out_dtype).itemsize * (1 if add is None else 2)
    if view is None:
        k = a.shape[0] if ta else a.shape[1]
        n = b.shape[0] if tb else b.shape[1]
        assert (b.shape[1] if tb else b.shape[0]) == k, (a.shape, b.shape, ta, tb)
        if o_view is not None and o_view.tn is not None:
            tn = o_view.tn
            tm, _, tk = _matmul_tiles(m, tn, k, a.dtype.itemsize, b.dtype.itemsize, o_bytes)
        else:
            tm, tn, tk = _matmul_tiles(m, n, k, a.dtype.itemsize, b.dtype.itemsize, o_bytes)
    else:
        n, k, tn, tk = view.n, view.k, view.tn, view.tk
        b_maps = view.part_maps or (view.index_map,)
        tm, _, whole = _matmul_tiles(m, tn, tk * len(b_maps), a.dtype.itemsize, b.dtype.itemsize, o_bytes)
        assert whole == tk * len(b_maps) and not (ta and len(b_maps) > 1), (name, tm, whole)
    parts = 1 if view is None else len(b_maps)
    k_step = tk * parts
    nk = k // k_step
    dims = (((0 if ta else 1,), (1 if tb else 0,)), ((), ()))

    def body(a_ref, *rest):
        b_refs, rest = rest[:parts], rest[parts:]
        if parts == 1:
            prod = lax.dot_general(a_ref[...].astype(BF16), b_refs[0][...].astype(BF16), dims, preferred_element_type=F32)
        else:
            prod = sum(lax.dot_general(a_ref[:, p * tk:(p + 1) * tk].astype(BF16), b_refs[p][...].astype(BF16), dims,
                                       preferred_element_type=F32) for p in range(parts))
        if nk == 1:
            o_ref = rest[-1]
            o_ref[...] = (prod if add is None else prod + rest[0][...]).astype(o_ref.dtype)
            return
        o_ref, acc = rest[-2:]
        kk = pl.program_id(2)

        @pl.when(kk == 0)
        def _():
            acc[...] = prod

        @pl.when(kk != 0)
        def _():
            acc[...] += prod

        @pl.when(kk == nk - 1)
        def _():
            r = acc[...]
            if add is not None:
                r = r + rest[0][...]
            o_ref[...] = r.astype(o_ref.dtype)

    if ta:
        a_spec = pl.BlockSpec((k_step, tm), lambda i, j, kk: (kk, i))
    else:
        a_spec = pl.BlockSpec((tm, k_step), lambda i, j, kk: (i, kk))
    b_tile = (tn, tk) if tb else (tk, tn)
    if view is not None:
        b_specs = [pl.BlockSpec((None,) * view.lead + b_tile, index_map) for index_map in b_maps]
    elif tb:
        b_specs = [pl.BlockSpec(b_tile, lambda i, j, kk: (j, kk))]
    else:
        b_specs = [pl.BlockSpec(b_tile, lambda i, j, kk: (kk, j))]
    o_spec = pl.BlockSpec((tm, tn), lambda i, j, kk: (i, j))
    ins = [a] + [b] * parts + ([add] if add is not None else [])
    in_specs = [a_spec] + b_specs + ([o_spec] if add is not None else [])
    out_shape, aliases = jax.ShapeDtypeStruct((m, n), out_dtype), {}
    if o_view is not None:
        assert add is None
        o_spec = pl.BlockSpec((None, tm, tn), o_view.index_map)
        out_shape = jax.ShapeDtypeStruct(o_view.shape, out_dtype)
        if o_view.into is not None:
            aliases = {len(ins): 0}
            ins.append(o_view.into)
            in_specs.append(pl.BlockSpec(memory_space=pl.ANY))
    if after is not None:
        ins.append(after)
        in_specs.append(pl.BlockSpec(memory_space=pl.ANY))
    return pl.pallas_call(
        body, name=name, grid=(m // tm, n // tn, nk),
        in_specs=in_specs, out_specs=o_spec, out_shape=out_shape, input_output_aliases=aliases,
        scratch_shapes=[pltpu.VMEM((tm, tn), F32)] if nk > 1 else [],
        compiler_params=_params(("parallel", "parallel", "arbitrary")),
    )(*ins)


def normmod_tile(x, g, shift, scale):
    return (rms(x) * g * (1.0 + scale) + shift,)


def resid_tile(x, y, gate):
    return (x + gate * y,)


def gated_tile(y, gate):
    return (gate * y,)


def prep_tile(z_qk, z_rq, z_rk, z_gq, zg, cos, sin, qg, kg, gate_up, gate_b):
    out = []
    for h, t in enumerate(_heads(z_qk, ATT_Q_HEADS + ATT_KV_HEADS)):
        out.append(rope(rms(t) * (qg if h < ATT_Q_HEADS else kg), cos, sin))
    gq = z_gq * (GLA_DK ** -0.5)
    rq = [rope(t, cos, sin) for t in _heads(z_rq, RET_HEADS)]
    rk = [rope(t * (HEAD_DIM ** -0.5), cos, sin) for t in _heads(z_rk, RET_HEADS)]
    la = [log_sigmoid(bdot(zg, gate_up[d * LANES:(d + 1) * LANES]) + gate_b[d:d + 1]) * (1.0 / GLA_TAU) for d in range(2)]
    return (jnp.concatenate(out + [gq] + rq + rk + la, axis=1),)


def post_tile(o_att, o_ret_f, o_ret_b, o_gla_f, o_gla_b, rg, gr, ret_g, gla_g):
    ret = jnp.concatenate([rms(t) * ret_g for t in _heads(o_ret_f + o_ret_b, RET_HEADS)], axis=1) * silu(rg)
    gla = jnp.concatenate([rms(t) * gla_g for t in _heads(o_gla_f + o_gla_b, GLA_HEADS)], axis=1) * silu(gr)
    return (jnp.concatenate([o_att, ret, gla], axis=1),)


def _convglu_tile(n_lat, a, v, cw, cb):
    t = a.shape[0]
    row = lax.broadcasted_iota(jnp.int32, (t, 1), 0)
    has_prev = ((row != 0) & (row != n_lat)).astype(F32)
    has_next = ((row != n_lat - 1) & (row != t - 1)).astype(F32)
    conv = roll(a, 1, 0) * has_prev * cw[0:1] + a * cw[1:2] + roll(a, -1, 0) * has_next * cw[2:3] + cb
    return silu(conv) * v


def convglu(name, u, cw, cb, n_lat):
    t, f2 = u.shape
    f, tc = f2 // 2, FFN_COL_TILE
    nb = f // tc

    def body(a_ref, v_ref, cw_ref, cb_ref, o_ref):
        o_ref[...] = _convglu_tile(n_lat, a_ref[...].astype(F32), v_ref[...].astype(F32), cw_ref[...], cb_ref[...]).astype(o_ref.dtype)

    return pl.pallas_call(
        body, name=name, grid=(nb,),
        in_specs=[pl.BlockSpec((t, tc), lambda j: (0, j)), pl.BlockSpec((t, tc), lambda j: (0, nb + j)),
                  pl.BlockSpec((3, tc), lambda j: (0, j)), pl.BlockSpec((1, tc), lambda j: (0, j))],
        out_specs=pl.BlockSpec((t, tc), lambda j: (0, j)),
        out_shape=jax.ShapeDtypeStruct((t, f), BF16),
        compiler_params=_params(("parallel",)),
    )(u, u, cw, cb)


def convglu_bwd(name, u, cw, cb, dg, n_lat):
    t, f2 = u.shape
    f, tc = f2 // 2, FFN_COL_TILE
    nb = f // tc

    def body(a_ref, v_ref, cw_ref, cb_ref, dg_ref, da_ref, dv_ref, dcw_ref, dcb_ref):
        _, vjp = jax.vjp(functools.partial(_convglu_tile, n_lat), a_ref[...].astype(F32), v_ref[...].astype(F32),
                         cw_ref[...], cb_ref[...])
        da, dv, dcw_ref[...], dcb_ref[...] = vjp(dg_ref[...])
        da_ref[...], dv_ref[...] = da.astype(BF16), dv.astype(BF16)

    col = pl.BlockSpec((t, tc), lambda j: (0, j))
    return pl.pallas_call(
        body, name=name, grid=(nb,),
        in_specs=[col, pl.BlockSpec((t, tc), lambda j: (0, nb + j)), pl.BlockSpec((3, tc), lambda j: (0, j)),
                  pl.BlockSpec((1, tc), lambda j: (0, j)), col],
        out_specs=[col, col, pl.BlockSpec((3, tc), lambda j: (0, j)), pl.BlockSpec((1, tc), lambda j: (0, j))],
        out_shape=[jax.ShapeDtypeStruct((t, f), BF16), jax.ShapeDtypeStruct((t, f), BF16),
                   jax.ShapeDtypeStruct((3, f), F32), jax.ShapeDtypeStruct((1, f), F32)],
        compiler_params=_params(("parallel",)),
    )(u, u, cw, cb, dg)


def final_loss(x, target, g, n_lat):
    tm = ROW_TILE
    d = x.shape[1]

    def body(x_ref, t_ref, g_ref, loss_ref, dx_ref, dg_ref):
        i = pl.program_id(0)
        tgt = t_ref[...]

        def f(xv, gv):
            e = rms(xv) * gv - tgt
            s = jnp.sum(jnp.sum(e * e, axis=1, keepdims=True), axis=0, keepdims=True)
            return s * (0.5 / d)

        val, vjp = jax.vjp(f, x_ref[...], g_ref[...])
        dx, dgv = vjp(jnp.ones((1, 1), F32))
        dx_ref[...] = dx

        @pl.when(i == 0)
        def _():
            dg_ref[...] = dgv
            loss_ref[...] = jnp.broadcast_to(val, loss_ref.shape)

        @pl.when(i != 0)
        def _():
            dg_ref[...] += dgv
            loss_ref[...] += jnp.broadcast_to(val, loss_ref.shape)

    return pl.pallas_call(
        body, name="final_loss", grid=(n_lat // tm,),
        in_specs=[pl.BlockSpec((tm, d), lambda i: (i, 0)), pl.BlockSpec((tm, d), lambda i: (i, 0)),
                  pl.BlockSpec((1, d), lambda i: (0, 0))],
        out_specs=[pl.BlockSpec((1, LANES), lambda i: (0, 0)), pl.BlockSpec((tm, d), lambda i: (i, 0)),
                   pl.BlockSpec((1, d), lambda i: (0, 0))],
        out_shape=[jax.ShapeDtypeStruct((1, LANES), F32), jax.ShapeDtypeStruct((n_lat, d), F32),
                   jax.ShapeDtypeStruct((1, d), F32)],
        compiler_params=_params(("arbitrary",)),
    )(x, target, g)


ATT_SCALE = HEAD_DIM ** -0.5
_AK_BLK = P_AK // HEAD_DIM
_AV_BLK = Z_AV // HEAD_DIM


def _att_specs(t, tq):
    gw = ATT_GROUP * HEAD_DIM
    q_spec = pl.BlockSpec((tq, gw), lambda kv, i: (i, kv))
    k_spec = pl.BlockSpec((t, HEAD_DIM), lambda kv, i: (0, _AK_BLK + kv))
    v_spec = pl.BlockSpec((t, HEAD_DIM), lambda kv, i: (0, _AV_BLK + kv))
    row_spec = pl.BlockSpec((ATT_GROUP, tq, 1), lambda kv, i: (kv, i, 0))
    return q_spec, k_spec, v_spec, row_spec


def _att_mask(i, t, tq, n_lat):
    col = lax.broadcasted_iota(jnp.int32, (1, t), 1)
    return jnp.where((i >= n_lat // tq) & (col < n_lat), -jnp.inf, 0.0).astype(F32)


def attn_fwd(p, z, n_lat):
    t = p.shape[0]
    tq = ROW_TILE

    def body(q_ref, k_ref, v_ref, o_ref, lse_ref):
        mask = _att_mask(pl.program_id(1), t, tq, n_lat)
        k, v = k_ref[...].astype(BF16), v_ref[...].astype(BF16)
        for g in range(ATT_GROUP):
            cols = slice(g * HEAD_DIM, (g + 1) * HEAD_DIM)
            s = _dg(q_ref[:, cols], k, _NT) * ATT_SCALE + mask
            m = jnp.max(s, axis=1, keepdims=True)
            pr = jnp.exp(s - m)
            l = jnp.sum(pr, axis=1, keepdims=True)
            o_ref[:, cols] = _dg(pr, v, _NN) / l
            lse_ref[g] = m + jnp.log(l)

    q_spec, k_spec, v_spec, row_spec = _att_specs(t, tq)
    return pl.pallas_call(
        body, name="attn_fwd", grid=(ATT_KV_HEADS, t // tq),
        in_specs=[q_spec, k_spec, v_spec], out_specs=[q_spec, row_spec],
        out_shape=[jax.ShapeDtypeStruct((t, ATT_Q_HEADS * HEAD_DIM), F32),
                   jax.ShapeDtypeStruct((ATT_Q_HEADS, t, 1), F32)],
        compiler_params=_params(("parallel", "parallel")),
    )(p, p, z)


def attn_bwd(p, z, o, lse, do, n_lat):
    t = p.shape[0]
    tq = ROW_TILE

    def body(q_ref, k_ref, v_ref, o_ref, do_ref, lse_ref, dq_ref, dk_ref, dv_ref):
        i = pl.program_id(1)

        @pl.when(i == 0)
        def _():
            dk_ref[...] = jnp.zeros_like(dk_ref)
            dv_ref[...] = jnp.zeros_like(dv_ref)

        mask = _att_mask(i, t, tq, n_lat)
        k, v = k_ref[...].astype(BF16), v_ref[...].astype(BF16)
        dk, dv = dk_ref[...], dv_ref[...]
        for g in range(ATT_GROUP):
            cols = slice(g * HEAD_DIM, (g + 1) * HEAD_DIM)
            q, do_g = q_ref[:, cols].astype(BF16), do_ref[:, cols]
            pr = jnp.exp(_dg(q, k, _NT) * ATT_SCALE + mask - lse_ref[g])
            delta = jnp.sum(o_ref[:, cols] * do_g, axis=1, keepdims=True)
            ds = pr * (_dg(do_g, v, _NT) - delta) * ATT_SCALE
            dq_ref[:, cols] = _dg(ds, k, _NN)
            dk = dk + _dg(ds, q, _TN)
            dv = dv + _dg(pr, do_g, _TN)
        dk_ref[...], dv_ref[...] = dk, dv

    q_spec, k_spec, v_spec, row_spec = _att_specs(t, tq)
    kv_out = pl.BlockSpec((t, HEAD_DIM), lambda kv, i: (0, kv))
    return pl.pallas_call(
        body, name="attn_bwd", grid=(ATT_KV_HEADS, t // tq),
        in_specs=[q_spec, k_spec, v_spec, q_spec, q_spec, row_spec],
        out_specs=[q_spec, kv_out, kv_out],
        out_shape=[jax.ShapeDtypeStruct((t, ATT_Q_HEADS * HEAD_DIM), F32),
                   jax.ShapeDtypeStruct((t, ATT_KV_HEADS * HEAD_DIM), F32),
                   jax.ShapeDtypeStruct((t, ATT_KV_HEADS * HEAD_DIM), F32)],
        compiler_params=_params(("parallel", "arbitrary")),
    )(p, p, z, o, do, lse)


_RQ_BLK = P_RQ // HEAD_DIM
_RK_BLK = P_RK // HEAD_DIM
_RV_BLK = Z_RV // HEAD_DIM


def _scan_chunk(direction, step, n_chunks, n_lat_chunks):
    return jnp.where(direction == 0, (step + n_lat_chunks) % n_chunks, n_chunks - 1 - step)


def _ret_geometry(direction):
    c = RET_CHUNK
    i = lax.broadcasted_iota(jnp.int32, (c, c), 0)
    j = lax.broadcasted_iota(jnp.int32, (c, c), 1)
    rel = jnp.where(direction == 0, i - j, j - i).astype(F32)
    r = lax.broadcasted_iota(jnp.int32, (c, 1), 0)
    pos = jnp.where(direction == 0, r, c - 1 - r).astype(F32)
    return rel, pos


def ret_chunk(q, k, v, s, lg, rel, pos):
    c = RET_CHUNK
    causal = rel >= 0
    d_in = jnp.where(causal, jnp.exp(lg * jnp.where(causal, rel, 0.0)), 0.0)
    q_dec = jnp.exp(lg * (pos + 1.0))
    k_dec = jnp.exp(lg * (c - 1.0 - pos))
    c_dec = jnp.exp(lg * c)
    att = bdot_nt(q, k) * d_in
    o = bdot(att, v) + bdot(q * q_dec, s)
    s_new = c_dec * s + bdot_tn(k * k_dec, v)
    return o, s_new


def ret_fwd(p, z, lg, n_lat):
    t = p.shape[0]
    c = RET_CHUNK
    nc, nlc = t // c, n_lat // c

    def body(q_ref, k_ref, v_ref, lg_ref, o_ref, ssave_ref, s_s):
        d, n = pl.program_id(0), pl.program_id(1)

        @pl.when(n == 0)
        def _():
            s_s[...] = jnp.zeros_like(s_s)

        rel, pos = _ret_geometry(d)
        for h in range(RET_HEADS):
            cols = slice(h * HEAD_DIM, (h + 1) * HEAD_DIM)
            ssave_ref[0, h, 0] = s_s[h]
            o, s_new = ret_chunk(q_ref[:, cols], k_ref[:, cols], v_ref[:, cols], s_s[h], lg_ref[0, h], rel, pos)
            o_ref[:, cols] = o
            s_s[h] = s_new

    w = RET_HEADS * HEAD_DIM

    def blk(base):
        return pl.BlockSpec((c, w), lambda d, n: (_scan_chunk(d, n, nc, nlc), base // RET_HEADS))

    return pl.pallas_call(
        body, name="ret_fwd", grid=(2, nc),
        in_specs=[blk(_RQ_BLK), blk(_RK_BLK), blk(_RV_BLK), pl.BlockSpec((1, RET_HEADS, 1, 1), lambda d, n: (d, 0, 0, 0))],
        out_specs=[pl.BlockSpec((c, w), lambda d, n: (_scan_chunk(d, n, nc, nlc), d)),
                   pl.BlockSpec((1, RET_HEADS, 1, HEAD_DIM, HEAD_DIM), lambda d, n: (d, 0, n, 0, 0))],
        out_shape=[jax.ShapeDtypeStruct((t, 2 * w), F32),
                   jax.ShapeDtypeStruct((2, RET_HEADS, nc, HEAD_DIM, HEAD_DIM), F32)],
        scratch_shapes=[pltpu.VMEM((RET_HEADS, HEAD_DIM, HEAD_DIM), F32)],
        compiler_params=_params(("parallel", "arbitrary")),
    )(p, p, z, lg)


def ret_bwd(p, z, lg, states, do, n_lat):
    t = p.shape[0]
    c = RET_CHUNK
    nc, nlc = t // c, n_lat // c

    def body(q_ref, k_ref, v_ref, lg_ref, s_ref, do_ref, dq_ref, dk_ref, dv_ref, dlg_ref, ds_s):
        d, n = pl.program_id(0), pl.program_id(1)

        @pl.when(n == 0)
        def _():
            ds_s[...] = jnp.zeros_like(ds_s)
            dlg_ref[...] = jnp.zeros_like(dlg_ref)

        rel, pos = _ret_geometry(d)
        f = functools.partial(ret_chunk, rel=rel, pos=pos)
        for h in range(RET_HEADS):
            cols = slice(h * HEAD_DIM, (h + 1) * HEAD_DIM)
            _, vjp = jax.vjp(f, q_ref[:, cols], k_ref[:, cols], v_ref[:, cols], s_ref[0, h, 0], lg_ref[0, h])
            dq, dk, dv, ds, dlg = vjp((do_ref[:, cols], ds_s[h]))
            dq_ref[:, cols], dk_ref[:, cols], dv_ref[:, cols] = dq, dk, dv
            ds_s[h] = ds
            dlg_ref[0, h] += dlg

    def chunk_of(d, n):
        return _scan_chunk(d, nc - 1 - n, nc, nlc)

    w = RET_HEADS * HEAD_DIM

    def blk(base):
        return pl.BlockSpec((c, w), lambda d, n: (chunk_of(d, n), base // RET_HEADS))

    out_blk = pl.BlockSpec((c, w), lambda d, n: (chunk_of(d, n), d))
    lg_blk = pl.BlockSpec((1, RET_HEADS, 1, 1), lambda d, n: (d, 0, 0, 0))
    grad_shape = jax.ShapeDtypeStruct((t, 2 * w), F32)
    return pl.pallas_call(
        body, name="ret_bwd", grid=(2, nc),
        in_specs=[blk(_RQ_BLK), blk(_RK_BLK), blk(_RV_BLK), lg_blk,
                  pl.BlockSpec((1, RET_HEADS, 1, HEAD_DIM, HEAD_DIM), lambda d, n: (d, 0, nc - 1 - n, 0, 0)),
                  pl.BlockSpec((c, w), lambda d, n: (chunk_of(d, n), 0))],
        out_specs=[out_blk, out_blk, out_blk, lg_blk],
        out_shape=[grad_shape, grad_shape, grad_shape, jax.ShapeDtypeStruct((2, RET_HEADS, 1, 1), F32)],
        scratch_shapes=[pltpu.VMEM((RET_HEADS, HEAD_DIM, HEAD_DIM), F32)],
        compiler_params=_params(("parallel", "arbitrary")),
    )(p, p, z, lg, states, do)


_GQ_BLK = P_GQ // (GLA_HEADS * GLA_DK)
_GK_BLK = Z_GK // (GLA_HEADS * GLA_DK)
_GV_BLK = Z_GV // (GLA_HEADS * GLA_DV)
_LA_BLK = P_LA // (GLA_HEADS * GLA_DK)


def _gla_mask(direction):
    c = GLA_CHUNK
    i = lax.broadcasted_iota(jnp.int32, (c, c), 0)
    j = lax.broadcasted_iota(jnp.int32, (c, c), 1)
    return (jnp.where(direction == 0, i - j, j - i) >= 0).astype(F32)


def gla_chunk(q, k, v, la, st, mask):
    b = mask_cumsum(mask, la)
    btot = jnp.sum(la, axis=0, keepdims=True)
    half = 0.5 * btot
    qt, kt = q * jnp.exp(b - half), k * jnp.exp(half - b)
    qs, ke = q * jnp.exp(b), k * jnp.exp(btot - b)
    outs, upd = [], []
    for h in range(GLA_HEADS):
        ks = slice(h * GLA_DK, (h + 1) * GLA_DK)
        vh = v[:, h * GLA_DV:(h + 1) * GLA_DV]
        att = bdot_nt(qt[:, ks], kt[:, ks]) * mask
        outs.append(bdot(att, vh) + bdot_nt(qs[:, ks], st[:, ks]))
        upd.append(bdot_tn(vh, ke[:, ks]))
    st_new = st * jnp.exp(btot) + jnp.concatenate(upd, axis=1)
    return jnp.concatenate(outs, axis=1), st_new


def gla_fwd(p, z, n_lat):
    t = p.shape[0]
    c = GLA_CHUNK
    nc, nlc = t // c, n_lat // c
    kw, vw = GLA_HEADS * GLA_DK, GLA_HEADS * GLA_DV

    def body(q_ref, k_ref, v_ref, la_ref, o_ref, ssave_ref, s_s):
        d, n = pl.program_id(0), pl.program_id(1)

        @pl.when(n == 0)
        def _():
            s_s[...] = jnp.zeros_like(s_s)

        ssave_ref[0, 0] = s_s[...]
        o, s_new = gla_chunk(q_ref[...], k_ref[...], v_ref[...], la_ref[...], s_s[...], _gla_mask(d))
        o_ref[...] = o
        s_s[...] = s_new

    def chunk_of(d, n):
        return _scan_chunk(d, n, nc, nlc)

    return pl.pallas_call(
        body, name="gla_fwd", grid=(2, nc),
        in_specs=[pl.BlockSpec((c, kw), lambda d, n: (chunk_of(d, n), _GQ_BLK)),
                  pl.BlockSpec((c, kw), lambda d, n: (chunk_of(d, n), _GK_BLK)),
                  pl.BlockSpec((c, vw), lambda d, n: (chunk_of(d, n), _GV_BLK)),
                  pl.BlockSpec((c, kw), lambda d, n: (chunk_of(d, n), _LA_BLK + d))],
        out_specs=[pl.BlockSpec((c, vw), lambda d, n: (chunk_of(d, n), d)),
                   pl.BlockSpec((1, 1, GLA_DV, kw), lambda d, n: (d, n, 0, 0))],
        out_shape=[jax.ShapeDtypeStruct((t, 2 * vw), F32), jax.ShapeDtypeStruct((2, nc, GLA_DV, kw), F32)],
        scratch_shapes=[pltpu.VMEM((GLA_DV, kw), F32)],
        compiler_params=_params(("parallel", "arbitrary")),
    )(p, z, z, p)


def gla_bwd(p, z, states, do, n_lat):
    t = p.shape[0]
    c = GLA_CHUNK
    nc, nlc = t // c, n_lat // c
    kw, vw = GLA_HEADS * GLA_DK, GLA_HEADS * GLA_DV

    def body(q_ref, k_ref, v_ref, la_ref, s_ref, do_ref, dq_ref, dk_ref, dv_ref, dla_ref, ds_s):
        d, n = pl.program_id(0), pl.program_id(1)

        @pl.when(n == 0)
        def _():
            ds_s[...] = jnp.zeros_like(ds_s)

        f = functools.partial(gla_chunk, mask=_gla_mask(d))
        _, vjp = jax.vjp(f, q_ref[...], k_ref[...], v_ref[...], la_ref[...], s_ref[0, 0])
        dq_ref[...], dk_ref[...], dv_ref[...], dla_ref[...], ds_s[...] = vjp((do_ref[...], ds_s[...]))

    def chunk_of(d, n):
        return _scan_chunk(d, nc - 1 - n, nc, nlc)

    k_out = pl.BlockSpec((c, kw), lambda d, n: (chunk_of(d, n), d))
    return pl.pallas_call(
        body, name="gla_bwd", grid=(2, nc),
        in_specs=[pl.BlockSpec((c, kw), lambda d, n: (chunk_of(d, n), _GQ_BLK)),
                  pl.BlockSpec((c, kw), lambda d, n: (chunk_of(d, n), _GK_BLK)),
                  pl.BlockSpec((c, vw), lambda d, n: (chunk_of(d, n), _GV_BLK)),
                  pl.BlockSpec((c, kw), lambda d, n: (chunk_of(d, n), _LA_BLK + d)),
                  pl.BlockSpec((1, 1, GLA_DV, kw), lambda d, n: (d, nc - 1 - n, 0, 0)),
                  pl.BlockSpec((c, vw), lambda d, n: (chunk_of(d, n), 0))],
        out_specs=[k_out, k_out, pl.BlockSpec((c, vw), lambda d, n: (chunk_of(d, n), d)), k_out],
        out_shape=[jax.ShapeDtypeStruct((t, 2 * kw), F32), jax.ShapeDtypeStruct((t, 2 * kw), F32),
                   jax.ShapeDtypeStruct((t, 2 * vw), F32), jax.ShapeDtypeStruct((t, 2 * kw), F32)],
        scratch_shapes=[pltpu.VMEM((GLA_DV, kw), F32)],
        compiler_params=_params(("parallel", "arbitrary")),
    )(p, z, z, p, states, do)


def _adam_tile(w, g, m, v):
    m = ADAM_B1 * m + (1.0 - ADAM_B1) * g
    v = ADAM_B2 * v + (1.0 - ADAM_B2) * (g * g)
    m_hat = m / (1.0 - ADAM_B1 ** ADAM_STEP)
    v_hat = v / (1.0 - ADAM_B2 ** ADAM_STEP)
    delta = -ADAM_LR * (m_hat / (jnp.sqrt(v_hat) + ADAM_EPS) + ADAM_WD * w)
    return delta, m, v


def adamw(name, w, g, m, v):
    shape = w.shape
    cols = shape[-1] if w.ndim > 1 and shape[-1] >= LANES else int(np.prod(shape))
    rows = int(np.prod(shape)) // cols
    tr = rows
    for cand in (512, 256, 128, 64, 32, 16, 8):
        if rows % cand == 0 and cand * cols * 4 <= (1 << 20):
            tr = cand
            break
    flat = [a.reshape(rows, cols) for a in (w, g, m, v)]

    def body(w_ref, g_ref, m_ref, v_ref, d_ref, mo_ref, vo_ref):
        d_ref[...], mo_ref[...], vo_ref[...] = _adam_tile(w_ref[...], g_ref[...], m_ref[...], v_ref[...])

    spec = pl.BlockSpec((tr, cols), lambda i: (i, 0))
    outs = pl.pallas_call(
        body, name=name, grid=(rows // tr,),
        in_specs=[spec] * 4, out_specs=[spec] * 3,
        out_shape=[jax.ShapeDtypeStruct((rows, cols), F32)] * 3,
        compiler_params=_params(("parallel",)),
    )(*flat)
    return tuple(o.reshape(shape) for o in outs)


def adamw_layers(name, w, grads, m, v):
    depth, rows, cols = w.shape
    tr = _rows_tile(rows, cols)
    nb = rows // tr

    def body(w_ref, m_ref, v_ref, *rest):
        g_refs, (g_ref, d_ref, mo_ref, vo_ref) = rest[:depth], rest[depth:]
        l = pl.program_id(0)
        for k in range(depth):
            @pl.when(l == k)
            def _():
                g = g_refs[k][...]
                g_ref[...] = g
                d_ref[...], mo_ref[...], vo_ref[...] = _adam_tile(w_ref[...], g, m_ref[...], v_ref[...])

    def layer_grad(k):
        return pl.BlockSpec((tr, cols), lambda l, i: (jnp.where(l < k, 0, jnp.where(l == k, i, nb - 1)), 0))

    spec = pl.BlockSpec((tr, cols), lambda l, i: (l * nb + i, 0))
    flat = [a.reshape(depth * rows, cols) for a in (w, m, v)]
    g_all, delta, new_m, new_v = pl.pallas_call(
        body, name=name, grid=(depth, nb),
        in_specs=[spec] * 3 + [layer_grad(k) for k in range(depth)], out_specs=[spec] * 4,
        out_shape=[jax.ShapeDtypeStruct((depth * rows, cols), F32)] * 4,
        compiler_params=_params(("arbitrary", "arbitrary")),
    )(*flat, *grads)
    return tuple(a.reshape(w.shape) for a in (delta, new_m, new_v)), g_all.reshape(w.shape)


MESH = pl.DeviceIdType.MESH
_HBM = pl.BlockSpec(memory_space=pltpu.HBM)
N_CHIPS = 4
N_DEV = 8


def _place():
    x, y, c = lax.axis_index("x"), lax.axis_index("y"), lax.axis_index("c")
    chips = [(1 - x, y), (x, 1 - y), (1 - x, 1 - y)]
    return x, y, c, chips


def _remote(src, dst, send_sem, recv_sem, to):
    return pltpu.make_async_remote_copy(src_ref=src, dst_ref=dst, send_sem=send_sem, recv_sem=recv_sem,
                                        device_id=to, device_id_type=MESH)


def all_gather_small(name, v):
    m_per, n = v.shape

    def body(x_ref, out_ref, send_sems, recv_sems, local_sem):
        x, y, c, chips = _place()
        me, sibling = (x, y, c), (x, y, 1 - c)

        def rows(px, py, pc):
            return out_ref.at[pl.ds((4 * px + 2 * py + pc) * m_per, m_per), :]

        def copy(k, block, to, src=None):
            return _remote(rows(*block) if src is None else src, rows(*block), send_sems.at[k], recv_sems.at[k], to)

        mine = pltpu.make_async_copy(x_ref, rows(*me), local_sem)
        mine.start()
        first = [copy(0, me, sibling, src=x_ref)]
        first += [copy(1 + j, me, (*chip, c), src=x_ref) for j, chip in enumerate(chips)]
        for cp in first:
            cp.start()
        passed = [copy(4 + j, (*chip, c), sibling) for j, chip in enumerate(chips)]
        for j, chip in enumerate(chips):
            copy(1 + j, (*chip, c), me).wait_recv()
            passed[j].start()
        copy(0, sibling, me).wait_recv()
        for j, chip in enumerate(chips):
            copy(4 + j, (*chip, 1 - c), me).wait_recv()
        for cp in first + passed:
            cp.wait_send()
        mine.wait()

    return pl.pallas_call(
        body, name=name,
        out_shape=jax.ShapeDtypeStruct((N_DEV * m_per, n), v.dtype),
        in_specs=[pl.BlockSpec(memory_space=pltpu.VMEM)],
        out_specs=pl.BlockSpec(memory_space=pltpu.VMEM),
        scratch_shapes=[pltpu.SemaphoreType.DMA((7,)), pltpu.SemaphoreType.DMA((7,)), pltpu.SemaphoreType.DMA],
        compiler_params=pltpu.CompilerParams(vmem_limit_bytes=VMEM_LIMIT),
    )(v)


_SEM = pl.BlockSpec(memory_space=pltpu.SEMAPHORE)
_SPLIT_COPY = pltpu.CompilerParams(has_side_effects=pltpu.SideEffectType.DATAFLOW_SIDE_EFFECTING)


class CopyPlan(NamedTuple):
    copies: object
    n: int
    in_place: bool = False


def _gather_copies(x_ref, land_ref, x, y, c, chips):
    half = x_ref.shape[0] // 2
    rows = pl.ds(c * half, half)
    return [(x_ref.at[rows, :], land_ref.at[2 * x + y, rows, :], (*chip, c), land_ref.at[2 * chip[0] + chip[1], rows, :])
            for chip in chips]


def _pass_copies(land_ref, _, x, y, c, chips):
    half = land_ref.shape[1] // 2
    mine, other = pl.ds(c * half, half), pl.ds((1 - c) * half, half)
    return [(land_ref.at[2 * chip[0] + chip[1], mine, :], land_ref.at[2 * chip[0] + chip[1], mine, :], (x, y, 1 - c),
             land_ref.at[2 * chip[0] + chip[1], other, :]) for chip in chips]


def _sibling_half_copies(p_ref, land_ref, x, y, c, chips):
    half = p_ref.shape[1] // 2
    return [(p_ref.at[:, pl.ds((1 - c) * half, half), :], land_ref, (x, y, 1 - c), land_ref)]


def _scatter_copies(s_ref, land_ref, x, y, c, chips):
    return [(s_ref.at[2 * chip[0] + chip[1]], land_ref.at[j], (*chip, c), land_ref.at[j]) for j, chip in enumerate(chips)]


def _join_copies(buf_ref, _, x, y, c, chips):
    half = buf_ref.shape[0] // 2
    mine = buf_ref.at[pl.ds(c * half, half), :]
    return [(mine, mine, (x, y, 1 - c), buf_ref.at[pl.ds((1 - c) * half, half), :])]


GATHER = CopyPlan(_gather_copies, 3)
PASS_ON = CopyPlan(_pass_copies, 3, in_place=True)
SIBLING_HALF = CopyPlan(_sibling_half_copies, 1)
SCATTER = CopyPlan(_scatter_copies, 3)
JOIN = CopyPlan(_join_copies, 1, in_place=True)


def split_start(name, plan, srcs, land_shapes=None, after=None):
    nt = len(srcs)
    arrays = [pltpu.with_memory_space_constraint(s, pltpu.HBM) for s in srcs]
    if not plan.in_place:
        arrays += [pltpu.with_memory_space_constraint(lax.empty(shape, s.dtype), pltpu.HBM) for shape, s in zip(land_shapes, srcs)]
    na = len(arrays)
    behind = [] if after is None else [after]
    n_in = na + len(behind)

    def body(*refs):
        x_refs = refs[:nt]
        land_refs = x_refs if plan.in_place else refs[nt:na]
        send, recv = refs[n_in:n_in + nt], refs[n_in + nt:n_in + 2 * nt]
        x, y, c, chips = _place()
        for t in range(nt):
            for j, (src, dst, to, _) in enumerate(plan.copies(x_refs[t], land_refs[t], x, y, c, chips)):
                _remote(src, dst, send[t].at[j], recv[t].at[j], to).start()
        refs[-1][...] = jnp.zeros_like(refs[-1])

    outs = pl.pallas_call(
        body, name=name,
        out_shape=tuple([pltpu.SemaphoreType.DMA((plan.n,))] * (2 * nt) + [pltpu.HBM(a.shape, a.dtype) for a in arrays]
                        + [jax.ShapeDtypeStruct((8, LANES), F32)]),
        in_specs=[_HBM] * na + [pl.BlockSpec(memory_space=pl.ANY)] * len(behind),
        out_specs=tuple([_SEM] * (2 * nt) + [_HBM] * na + [pl.BlockSpec(memory_space=pltpu.VMEM)]),
        input_output_aliases={i: 2 * nt + i for i in range(na)},
        compiler_params=_SPLIT_COPY,
    )(*arrays, *behind)
    groups = [(outs[t], outs[nt + t]) + tuple(outs[2 * nt + t + k * nt] for k in range(na // nt)) for t in range(nt)]
    return groups, outs[-1]


def split_wait(name, plan, group, after):
    send, recv, *arrays = group
    na = len(arrays)

    def body(*refs):
        x_ref, land_ref = refs[0], refs[na - 1]
        send_sem, recv_sem = refs[na], refs[na + 1]
        x, y, c, chips = _place()
        for j, (s, _, to, arrival) in enumerate(plan.copies(x_ref, land_ref, x, y, c, chips)):
            cp = _remote(s, arrival, send_sem.at[j], recv_sem.at[j], to)
            cp.wait_send()
            cp.wait_recv()

    return pl.pallas_call(
        body, name=name,
        out_shape=tuple(pltpu.HBM(a.shape, a.dtype) for a in arrays),
        in_specs=tuple([_HBM] * na + [_SEM, _SEM, pl.BlockSpec(memory_space=pl.ANY)]), out_specs=tuple([_HBM] * na),
        input_output_aliases={i: i for i in range(na)}, compiler_params=_SPLIT_COPY,
    )(*arrays, send, recv, after)


def _rows_tile(rows, cols):
    for cand in (512, 256, 128, 64, 32, 16):
        if rows % cand == 0 and cand * cols * 4 <= (1 << 21):
            return cand
    return rows


def add_sibling_half(name, pieces, from_sibling, core):
    n, h, cols = from_sibling.shape
    tr = _rows_tile(h, cols // 2)
    nb = h // tr

    def body(c_ref, a_ref, b_ref, o_ref):
        o_ref[...] = (a_ref[...].astype(F32) + b_ref[...].astype(F32)).astype(o_ref.dtype)

    blk = pl.BlockSpec((1, tr, cols), lambda q, i, c_ref: (q, i, 0))
    return pl.pallas_call(
        body, name=name,
        grid_spec=pltpu.PrefetchScalarGridSpec(
            num_scalar_prefetch=1, grid=(n, nb),
            in_specs=[pl.BlockSpec((1, tr, cols), lambda q, i, c_ref: (q, c_ref[0] * nb + i, 0)), blk], out_specs=blk),
        out_shape=jax.ShapeDtypeStruct((n, h, cols), BF16),
        compiler_params=_params(("parallel", "parallel")),
    )(core.reshape(1).astype(jnp.int32), pieces, from_sibling)


def add_chip_sums(name, chip_sums, from_chips, chip, core):
    _, h, cols = chip_sums.shape
    tr = _rows_tile(h, cols)
    nb = h // tr

    def body(s_ref, own_ref, r0_ref, r1_ref, r2_ref, o_ref):
        acc = own_ref[0].astype(F32) + r0_ref[0].astype(F32)
        o_ref[...] = acc + r1_ref[0].astype(F32) + r2_ref[0].astype(F32)

    def got(j):
        return pl.BlockSpec((1, tr, cols), lambda i, s_ref: (j, i, 0))

    return pl.pallas_call(
        body, name=name,
        grid_spec=pltpu.PrefetchScalarGridSpec(
            num_scalar_prefetch=1, grid=(nb,),
            in_specs=[pl.BlockSpec((1, tr, cols), lambda i, s_ref: (s_ref[0], i, 0)), got(0), got(1), got(2)],
            out_specs=pl.BlockSpec((tr, cols), lambda i, s_ref: (s_ref[1] * nb + i, 0))),
        out_shape=jax.ShapeDtypeStruct((2 * h, cols), F32),
        compiler_params=_params(("parallel",)),
    )(jnp.stack([chip, core]).astype(jnp.int32), chip_sums, from_chips, from_chips, from_chips)


def sum_device_blocks(name, g):
    n = g.shape[1]

    def body(g_ref, o_ref):
        acc = g_ref[0:8, :]
        for d in range(1, N_DEV):
            acc = acc + g_ref[8 * d:8 * (d + 1), :]
        o_ref[...] = acc

    return pl.pallas_call(body, name=name, out_shape=jax.ShapeDtypeStruct((8, n), F32),
                          compiler_params=pltpu.CompilerParams(vmem_limit_bytes=VMEM_LIMIT))(g)


class LayerWeights(NamedTuple):
    norm1_g: jax.Array
    q_g: jax.Array
    k_g: jax.Array
    lg: jax.Array
    ret_g: jax.Array
    gate_up: jax.Array
    gate_b: jax.Array
    gla_g: jax.Array
    norm2_g: jax.Array
    conv_w: jax.Array
    conv_b: jax.Array


def _mod(mods, k):
    return mods[:, k:k + 1, :]


def out_view(l, tb):
    rows = D_MODEL // N_CHIPS
    if tb:
        return BView(n=D_MODEL, k=D_MODEL, tn=rows, tk=D_MODEL, index_map=lambda i, j, kk: (j, l, kk))
    chips = tuple(functools.partial(lambda i, j, kk, q: (q, l, j), q=q) for q in range(N_CHIPS))
    return BView(n=D_MODEL, k=D_MODEL, tn=1024, tk=rows, index_map=None, part_maps=chips)


def down_view(l, f, tb):
    rows = f // N_CHIPS
    if tb:
        return BView(n=f, k=D_MODEL, tn=rows, tk=D_MODEL, index_map=lambda i, j, kk: (j, l, kk))
    chips = tuple(functools.partial(lambda i, j, kk, q: (q, l, j), q=q) for q in range(N_CHIPS))
    return BView(n=D_MODEL, k=f, tn=512, tk=rows, index_map=None, part_maps=chips)


def up_view(l, f, part=None):
    cols = 2 * f // N_CHIPS
    tc = _pick(cols, (1408, 1024, 512, 256))
    nbc = cols // tc
    if part is None:
        return BView(n=2 * f, k=D_MODEL, tn=tc, tk=D_MODEL, index_map=lambda i, j, kk: (j // nbc, l, j % nbc))
    nnb = D_MODEL // 512
    tiles = tuple(functools.partial(lambda i, j, kk, p: (2 * part + p // nbc, l * nnb + j, p % nbc), p=p) for p in range(2 * nbc))
    return BView(n=D_MODEL, k=f, tn=512, tk=tc, index_map=None, part_maps=tiles)


def up_grad_view(f, part, into):
    cols = f // 2
    tn = _pick(cols, (1408, 1024, 512, 256))
    nbc = cols // tn
    return OView((N_CHIPS, D_MODEL, cols), lambda i, j, kk: (2 * part + j // nbc, i, j % nbc), tn, into)


def ada_view(l, n_ada, tb):
    if tb:
        return BView(n=D_MODEL, k=n_ada, tn=1024, tk=n_ada, index_map=lambda i, j, kk: (l, j, 0))
    return BView(n=n_ada, k=D_MODEL, tn=1024, tk=D_MODEL, index_map=lambda i, j, kk: (l, 0, j))


def _prep_args(z, zg, cos, sin, w):
    rows = [Row(z, Z_AV, 0), Row(z, 512, Z_RQ // 512), Row(z, 512, Z_RK // 512), Row(z, 256, Z_GQ // 256),
            Row(zg, LANES, 0), Row(cos, HEAD_DIM, 0, False), Row(sin, HEAD_DIM, 0, False)]
    return rows, [Par(w.q_g), Par(w.k_g), Par(w.gate_up), Par(w.gate_b)]


def _post_args(o_att, o_ret, o_gla, z, w):
    rows = [Row(o_att, 1024), Row(o_ret, 512, 0), Row(o_ret, 512, 1, False), Row(o_gla, 512, 0), Row(o_gla, 512, 1, False),
            Row(z, 512, Z_RG // 512), Row(z, 512, Z_GR // 512)]
    return rows, [Par(w.ret_g), Par(w.gla_g)]


def layer_fwd(l, xs, mods, w, fetch, cos, sin, n_lat):
    t, d = xs.shape
    tag = f"l{l}_"
    nm1 = [Par(w.norm1_g), Par(_mod(mods, 0), True), Par(_mod(mods, 1), True)]
    (h,) = row_map(tag + "norm1", normmod_tile, [Row(xs, d)], nm1, [(d, BF16)], t, n_lat)
    (w_main, w_gate), started = fetch("w_in", h)
    z = matmul(tag + "in_proj", h, w_main, after=started)
    zg = matmul(tag + "gate_proj", h, w_gate)
    rows, pars = _prep_args(z, zg, cos, sin, w)
    (p,) = row_map(tag + "prep", prep_tile, rows, pars, [(P_W, F32)], t, n_lat)
    o_att, lse = attn_fwd(p, z, n_lat)
    o_ret, s_ret = ret_fwd(p, z, w.lg, n_lat)
    o_gla, s_gla = gla_fwd(p, z, n_lat)
    rows, pars = _post_args(o_att, o_ret, o_gla, z, w)
    (m,) = row_map(tag + "post", post_tile, rows, pars, [(d, BF16)], t, n_lat)
    g_out, started = fetch("w_out", m)
    y = matmul(tag + "out_proj", m, g_out, view=out_view(0, False), after=started)
    (x1,) = row_map(tag + "resid1", resid_tile, [Row(xs, d), Row(y, d)], [Par(_mod(mods, 2), True)], [(d, F32)], t, n_lat)
    nm2 = [Par(w.norm2_g), Par(_mod(mods, 3), True), Par(_mod(mods, 4), True)]
    (h2,) = row_map(tag + "norm2", normmod_tile, [Row(x1, d)], nm2, [(d, BF16)], t, n_lat)
    f = w.conv_b.shape[1]
    g_up, started = fetch("w_up", h2)
    u = matmul(tag + "up_proj", h2, g_up, view=up_view(0, f), after=started, out_dtype=BF16)
    g = convglu(tag + "convglu", u, w.conv_w, w.conv_b, n_lat)
    g_down, started = fetch("w_down", g)
    yd = matmul(tag + "down_proj", g, g_down, view=down_view(0, f, False), after=started)
    (x2,) = row_map(tag + "resid2", resid_tile, [Row(x1, d), Row(yd, d)], [Par(_mod(mods, 5), True)], [(d, F32)], t, n_lat)
    saved = dict(xs=xs, h=h, z=z, zg=zg, p=p, o_att=o_att, lse=lse, o_ret=o_ret, s_ret=s_ret, o_gla=o_gla, s_gla=s_gla,
                 m=m, y=y, x1=x1, h2=h2, u=u, g=g, yd=yd, w_main=w_main, w_gate=w_gate, g_out=g_out, g_up=g_up, g_down=g_down)
    return x2, saved


def _sum_dirs(a):
    w = a.shape[1] // 2
    return a[:, :w] + a[:, w:]


def layer_bwd(l, dx2, s, mods, w, cos, sin, n_lat, grad_ready):
    t, d = dx2.shape
    tag = f"l{l}_b_"
    dyd, dgate5 = row_vjp(tag + "resid2", gated_tile, [Row(s["yd"], d)], [Par(_mod(mods, 5), True)], [dx2], t, n_lat,
                          row_grad_dtype=BF16)
    f = w.conv_b.shape[1]
    dg = matmul(tag + "down_dx", dyd, s["g_down"], tb=True, view=down_view(0, f, True))
    dw_down = matmul(tag + "down_dw", s["g"], dyd, ta=True, out_dtype=BF16)
    da, dv, dcw, dcb = convglu_bwd(tag + "convglu", s["u"], w.conv_w, w.conv_b, dg, n_lat)
    dh2 = matmul(tag + "up_dx_gate", da, s["g_up"], tb=True, view=up_view(0, f, 0))
    dh2 = matmul(tag + "up_dx_value", dv, s["g_up"], tb=True, view=up_view(0, f, 1), add=dh2)
    dw_up = matmul(tag + "up_dw_gate", s["h2"], da, ta=True, out_dtype=BF16, o_view=up_grad_view(f, 0, None))
    dw_up = matmul(tag + "up_dw_value", s["h2"], dv, ta=True, out_dtype=BF16, o_view=up_grad_view(f, 1, dw_up))
    started = grad_ready("ffn", dict(w_up=dw_up, w_down=dw_down))
    nm2 = [Par(w.norm2_g), Par(_mod(mods, 3), True), Par(_mod(mods, 4), True)]
    dx1, dg2, dshift3, dscale4 = row_vjp(tag + "norm2", normmod_tile, [Row(s["x1"], d)], nm2, [dh2], t, n_lat,
                                         add_to_first=dx2, after=started)
    dy, dgate2 = row_vjp(tag + "resid1", gated_tile, [Row(s["y"], d)], [Par(_mod(mods, 2), True)], [dx1], t, n_lat,
                         row_grad_dtype=BF16)
    dm = matmul(tag + "out_dx", dy, s["g_out"], tb=True, view=out_view(0, True))
    dw_out = matmul(tag + "out_dw", s["m"], dy, ta=True, out_dtype=BF16)
    rows, pars = _post_args(s["o_att"], s["o_ret"], s["o_gla"], s["z"], w)
    started = grad_ready("w_out", dict(w_out=dw_out))
    do_att, do_ret, do_gla, d_rg, d_gr, d_ret_g, d_gla_g = row_vjp(tag + "post", post_tile, rows, pars, [dm], t, n_lat, after=started)
    dq_a, dk_a, dv_a = attn_bwd(s["p"], s["z"], s["o_att"], s["lse"], do_att, n_lat)
    dq_r, dk_r, dv_r, dlg = ret_bwd(s["p"], s["z"], w.lg, s["s_ret"], do_ret, n_lat)
    dq_g, dk_g, dv_g, dla = gla_bwd(s["p"], s["z"], s["s_gla"], do_gla, n_lat)
    dp = jnp.concatenate([dq_a, dk_a, _sum_dirs(dq_g), _sum_dirs(dq_r), _sum_dirs(dk_r), dla], axis=1)
    rows, pars = _prep_args(s["z"], s["zg"], cos, sin, w)
    d_zqk, d_zrq, d_zrk, d_zgq, dzg, d_qg, d_kg, d_up, d_gb = row_vjp(tag + "prep", prep_tile, rows, pars, [dp], t, n_lat)
    dz = jnp.concatenate([d_zqk, dv_a, d_zrq, d_zrk, _sum_dirs(dv_r), d_rg, d_zgq, _sum_dirs(dk_g), _sum_dirs(dv_g), d_gr], axis=1)
    dz, dzg = dz.astype(BF16), dzg.astype(BF16)
    dh_gate = matmul(tag + "gate_dx", dzg, s["w_gate"], tb=True)
    dh = matmul(tag + "in_dx", dz, s["w_main"], tb=True, add=dh_gate)
    dw_main = matmul(tag + "in_dw", s["h"], dz, ta=True, out_dtype=BF16)
    dw_gate = matmul(tag + "gate_dw", s["h"], dzg, ta=True, out_dtype=BF16)
    started = grad_ready("w_in", dict(w_main=dw_main, w_gate=dw_gate))
    nm1 = [Par(w.norm1_g), Par(_mod(mods, 0), True), Par(_mod(mods, 1), True)]
    dx, dg1, dshift0, dscale1 = row_vjp(tag + "norm1", normmod_tile, [Row(s["xs"], d)], nm1, [dh], t, n_lat,
                                        add_to_first=dx1, after=started)
    dmods = jnp.concatenate([dshift0, dscale1, dgate2, dshift3, dscale4, dgate5], axis=1)
    grads = dict(w_main=dw_main, w_gate=dw_gate, w_out=dw_out, w_up=dw_up, w_down=dw_down, norm1_g=dg1, q_g=d_qg, k_g=d_kg,
                 lg=dlg, ret_g=d_ret_g, gate_up=d_up, gate_b=d_gb, gla_g=d_gla_g, norm2_g=dg2, conv_w=dcw, conv_b=dcb)
    return dx, dmods, grads


def rope_tables(n_lat, n_ctx):
    rows = n_lat // GRID_W
    row = jnp.repeat(jnp.arange(rows, dtype=F32), GRID_W)
    col = jnp.tile(jnp.arange(GRID_W, dtype=F32), rows)
    n_freq = HEAD_DIM // 4
    inv_freq = ROPE_THETA ** (-jnp.arange(n_freq, dtype=F32) / n_freq)
    ang = jnp.concatenate([row[:, None] * inv_freq, col[:, None] * inv_freq], axis=-1)
    cos, sin = jnp.cos(ang), jnp.sin(ang)
    cos = jnp.concatenate([jnp.concatenate([cos, cos], axis=1), jnp.ones((n_ctx, HEAD_DIM), F32)], axis=0)
    sin = jnp.concatenate([jnp.concatenate([-sin, sin], axis=1), jnp.zeros((n_ctx, HEAD_DIM), F32)], axis=0)
    return cos, sin


def local_step(xs, target, mods, weights, fetch, final_g, n_lat, grad_ready):
    t, d = xs.shape
    cos, sin = rope_tables(n_lat, t - n_lat)
    saved = []
    h = xs
    for l, w in enumerate(weights):
        h, s = layer_fwd(l, h, mods[l], w, functools.partial(fetch, l), cos, sin, n_lat)
        saved.append(s)
    loss, dlat, dgf = final_loss(h, target, final_g, n_lat)
    dx = jnp.concatenate([dlat, jnp.zeros((t - n_lat, d), F32)], axis=0)
    dmods, grads = [None] * len(weights), [None] * len(weights)
    for l in reversed(range(len(weights))):
        dx, dmods[l], grads[l] = layer_bwd(l, dx, saved[l], mods[l], weights[l], cos, sin, n_lat, functools.partial(grad_ready, l))
    return loss, dx, dmods, grads, dgf


WEIGHT_NAMES = ("c_ctx", "ada_w", "ada_b", "norm1_g", "w_in", "q_norm_g", "k_norm_g", "ret_log_decay", "ret_norm_g",
                "gla_gate_up", "gla_gate_b", "gla_norm_g", "w_out", "norm2_g", "w_up", "conv_w", "conv_b", "w_down", "final_norm_g")
PACK_QUANTUM = 8 * LANES


def _pack(arrays):
    flat = jnp.concatenate([a.reshape(-1).astype(F32) for a in arrays])
    n = -(-flat.shape[0] // PACK_QUANTUM) * PACK_QUANTUM
    return jnp.pad(flat, (0, n - flat.shape[0])).reshape(8, n // 8)


def _unpack(flat2d, shapes):
    out, at = [], 0
    for s in shapes:
        size = int(np.prod(s))
        out.append(flat2d[:, at:at + size].reshape((flat2d.shape[0],) + tuple(s)))
        at += size
    return out


def _per_device(gathered):
    return gathered.reshape(N_DEV, -1)


def _from_chips(per_device, axis):
    chips = per_device[0::2]
    moved = jnp.moveaxis(chips, 0, axis)
    shape = moved.shape
    return moved.reshape(shape[:axis] + (shape[axis] * shape[axis + 1],) + shape[axis + 2:])


def kernel(x, c, ctx, c_ctx, ada_w, ada_b, norm1_g, w_in, q_norm_g, k_norm_g, ret_log_decay, ret_norm_g, gla_gate_up, gla_gate_b, gla_norm_g, w_out, norm2_g, w_up, conv_w, conv_b, w_down, final_norm_g, loss_target, m_c_ctx, m_ada_w, m_ada_b, m_norm1_g, m_w_in, m_q_norm_g, m_k_norm_g, m_ret_log_decay, m_ret_norm_g, m_gla_gate_up, m_gla_gate_b, m_gla_norm_g, m_w_out, m_norm2_g, m_w_up, m_conv_w, m_conv_b, m_w_down, m_final_norm_g, v_c_ctx, v_ada_w, v_ada_b, v_norm1_g, v_w_in, v_q_norm_g, v_k_norm_g, v_ret_log_decay, v_ret_norm_g, v_gla_gate_up, v_gla_gate_b, v_gla_norm_g, v_w_out, v_norm2_g, v_w_up, v_conv_w, v_conv_b, v_w_down, v_final_norm_g):
    weights = dict(zip(WEIGHT_NAMES, (c_ctx, ada_w, ada_b, norm1_g, w_in, q_norm_g, k_norm_g, ret_log_decay, ret_norm_g,
                                      gla_gate_up, gla_gate_b, gla_norm_g, w_out, norm2_g, w_up, conv_w, conv_b, w_down, final_norm_g)))
    mom_m = dict(zip(WEIGHT_NAMES, (m_c_ctx, m_ada_w, m_ada_b, m_norm1_g, m_w_in, m_q_norm_g, m_k_norm_g, m_ret_log_decay, m_ret_norm_g,
                                    m_gla_gate_up, m_gla_gate_b, m_gla_norm_g, m_w_out, m_norm2_g, m_w_up, m_conv_w, m_conv_b, m_w_down, m_final_norm_g)))
    mom_v = dict(zip(WEIGHT_NAMES, (v_c_ctx, v_ada_w, v_ada_b, v_norm1_g, v_w_in, v_q_norm_g, v_k_norm_g, v_ret_log_decay, v_ret_norm_g,
                                    v_gla_gate_up, v_gla_gate_b, v_gla_norm_g, v_w_out, v_norm2_g, v_w_up, v_conv_w, v_conv_b, v_w_down, v_final_norm_g)))
    depth, d = norm1_g.shape
    assert d == D_MODEL and x.shape[0] == 1
    n_lat, n_ctx, f = x.shape[1], ctx.shape[1], conv_b.shape[1]
    assert n_lat % ROW_TILE == 0 and n_ctx % ROW_TILE == 0 and f % FFN_COL_TILE == 0 and f % N_CHIPS == 0
    n_in = w_in.shape[2]
    n_ada = ada_w.shape[2]
    xi, yi, ci = lax.axis_index("x"), lax.axis_index("y"), lax.axis_index("c")
    chip = 2 * xi + yi
    dev = 2 * chip + ci

    big = ("w_in", "w_out", "w_up", "w_down")
    order = [(l, name) for l in range(depth) for name in big]
    shards = [weights[name][l].astype(BF16) for l, name in order]
    passing = {}

    def pass_on(k, after):
        tag = "{1}{0}".format(*order[k])
        own, land = split_wait("gather_wait_" + tag, GATHER, in_flight[k], after)
        (moving,), started = split_start("gather_pass_" + tag, PASS_ON, [land])
        passing[k] = (own, moving)
        return started

    def fetch(l, name, after):
        k = order.index((l, name))
        if k == 0:
            pass_on(0, after)
        own, moving = passing.pop(k)
        (land,) = split_wait(f"gather_pass_wait_{name}{l}", PASS_ON, moving, after)
        started = pass_on(k + 1, after) if k + 1 < len(order) else None
        land = lax.dynamic_update_slice_in_dim(land, own[None], chip, axis=0)
        if name != "w_in":
            return land, started
        cols = jnp.concatenate([land[q] for q in range(N_CHIPS)], axis=1)
        return (cols[:, :N_MAIN], jnp.pad(cols[:, N_MAIN:], ((0, 0), (0, LANES - N_GATE)))), started

    small_shapes = [c.shape[1:], conv_w.shape, gla_gate_up.shape, gla_gate_b.shape]
    got = _per_device(all_gather_small("gather_small", _pack([c, conv_w, gla_gate_up, gla_gate_b])))
    c_all, conv_w_sh, gate_up_sh, gate_b_sh = _unpack(got, small_shapes)
    conv_w_full = _from_chips(conv_w_sh, 2)
    gate_up_full = _from_chips(gate_up_sh, 3)
    gate_b_full = _from_chips(gate_b_sh, 2)

    act = jnp.zeros((16, d), F32).at[0:N_DEV].set(jax.nn.silu(c_all)).at[N_DEV].set(jax.nn.silu(c_ctx))
    mod_sh = jnp.stack([matmul(f"ada_fwd{l}", act, ada_w, view=ada_view(l, n_ada, False)) for l in range(depth)])
    got = _per_device(all_gather_small("gather_mods", _pack([mod_sh])))
    (mod_sh_all,) = _unpack(got, [mod_sh.shape])
    mod_full = _from_chips(mod_sh_all, 2) + ada_b[:, None, :]
    mod_mine = lax.dynamic_index_in_dim(mod_full, dev, axis=1, keepdims=False)
    mods = [jnp.stack([mod_mine[l].reshape(N_MOD, d), mod_full[l, N_DEV].reshape(N_MOD, d)]) for l in range(depth)]
    in_flight, token = split_start("gather_start", GATHER, shards, [(N_CHIPS,) + s.shape for s in shards], after=mod_full)

    layer_w = []
    for l in range(depth):
        up = jnp.zeros((2, LANES, GLA_HEADS * GLA_DK), F32)
        up = up.at[0, 0:GLA_RANK].set(gate_up_full[l, 0]).at[1, GLA_RANK:2 * GLA_RANK].set(gate_up_full[l, 1])
        layer_w.append(LayerWeights(
            norm1_g=norm1_g[l].reshape(1, 1, d), q_g=q_norm_g[l].reshape(1, 1, HEAD_DIM), k_g=k_norm_g[l].reshape(1, 1, HEAD_DIM),
            lg=ret_log_decay[l].reshape(2, RET_HEADS, 1, 1), ret_g=ret_norm_g[l].reshape(1, 1, HEAD_DIM),
            gate_up=up.reshape(1, 2 * LANES, -1), gate_b=gate_b_full[l].reshape(1, 2, -1), gla_g=gla_norm_g[l].reshape(1, 1, HEAD_DIM),
            norm2_g=norm2_g[l].reshape(1, 1, d), conv_w=conv_w_full[l], conv_b=conv_b[l].reshape(1, f)))

    def pieces_of(name, g):
        if name == "w_in":
            full_cols = jnp.concatenate([g["w_main"], g["w_gate"][:, :N_GATE]], axis=1)
            return jnp.stack([full_cols[:, q * n_in:(q + 1) * n_in] for q in range(N_CHIPS)])
        if name == "w_up":
            return g["w_up"]
        return g[name].reshape(N_CHIPS, -1, d)

    groups = {"ffn": ("w_up", "w_down"), "w_out": ("w_out",), "w_in": ("w_in",)}
    reducing = {}
    to_sibling = []

    def sibling_arrived(after):
        started = None
        while to_sibling:
            l, group, in_flight_halves = to_sibling.pop(0)
            sums = []
            for name, halves in zip(groups[group], in_flight_halves):
                pieces, from_sibling = split_wait(f"rs_sibling_wait_{name}{l}", SIBLING_HALF, halves, after)
                sums.append(add_sibling_half(f"rs_add_sibling_{name}{l}", pieces, from_sibling, ci))
            in_flight_sums, token = split_start(f"rs_start_{group}{l}", SCATTER, sums, [(3,) + s.shape[1:] for s in sums])
            reducing.update({(l, name): grp for name, grp in zip(groups[group], in_flight_sums)})
            started = token if started is None else started + token
        return started

    def grad_ready(l, group, g):
        pieces = [pieces_of(name, g) for name in groups[group]]
        before = None if (l, group) == (0, "w_in") else sibling_arrived(pieces[0])
        in_flight_halves, started = split_start(f"rs_sibling_{group}{l}", SIBLING_HALF, pieces,
                                                [(N_CHIPS, pc.shape[1] // 2, pc.shape[2]) for pc in pieces])
        to_sibling.append((l, group, in_flight_halves))
        return started if before is None else started + before

    xs = jnp.concatenate([x[0], ctx[0]], axis=0) + token[0, 0]
    loss, dx, dmods, grads, dgf = local_step(xs, loss_target[0], mods, layer_w, fetch, final_norm_g.reshape(1, d), n_lat, grad_ready)

    def gate_up_grad(g):
        return jnp.stack([g[0, 0:GLA_RANK], g[0, LANES + GLA_RANK:LANES + 2 * GLA_RANK]])

    per_layer = [[dmods[l][0], dmods[l][1], grads[l]["norm1_g"], grads[l]["norm2_g"], grads[l]["q_g"], grads[l]["k_g"],
                  grads[l]["ret_g"], grads[l]["gla_g"], grads[l]["lg"], gate_up_grad(grads[l]["gate_up"]), grads[l]["gate_b"],
                  grads[l]["conv_w"], grads[l]["conv_b"]] for l in range(depth)]
    layer_shapes = [(N_MOD * d,), (N_MOD * d,), (d,), (d,), (HEAD_DIM,), (HEAD_DIM,), (HEAD_DIM,), (HEAD_DIM,), (2, RET_HEADS),
                    (2, GLA_RANK, GLA_HEADS * GLA_DK), (2, GLA_HEADS * GLA_DK), (3, f), (f,)]
    packed = _pack([a for lay in per_layer for a in lay] + [dgf, loss[0, 0:1]])
    gathered = all_gather_small("gather_small_grads", packed)
    every = _unpack(_per_device(gathered), layer_shapes * depth + [(d,), (1,)])
    total = _unpack(sum_device_blocks("sum_small_grads", gathered).reshape(1, -1), layer_shapes * depth + [(d,), (1,)])
    nl = len(layer_shapes)

    def tot(l, k):
        return total[l * nl + k][0]

    out = {"norm1_g": jnp.stack([tot(l, 2) for l in range(depth)]), "norm2_g": jnp.stack([tot(l, 3) for l in range(depth)]),
           "q_norm_g": jnp.stack([tot(l, 4) for l in range(depth)]), "k_norm_g": jnp.stack([tot(l, 5) for l in range(depth)]),
           "ret_norm_g": jnp.stack([tot(l, 6) for l in range(depth)]), "gla_norm_g": jnp.stack([tot(l, 7) for l in range(depth)]),
           "ret_log_decay": jnp.stack([tot(l, 8) for l in range(depth)]),
           "gla_gate_up": lax.dynamic_slice_in_dim(jnp.stack([tot(l, 9) for l in range(depth)]), chip * gla_gate_up.shape[3], gla_gate_up.shape[3], axis=3),
           "gla_gate_b": lax.dynamic_slice_in_dim(jnp.stack([tot(l, 10) for l in range(depth)]), chip * gla_gate_b.shape[2], gla_gate_b.shape[2], axis=2),
           "conv_w": lax.dynamic_slice_in_dim(jnp.stack([tot(l, 11) for l in range(depth)]), chip * conv_w.shape[2], conv_w.shape[2], axis=2),
           "conv_b": jnp.stack([tot(l, 12) for l in range(depth)]),
           "final_norm_g": total[depth * nl][0],
           "ada_b": jnp.stack([tot(l, 0) + tot(l, 1) for l in range(depth)])}
    loss_total = total[depth * nl + 1][0, 0]

    dmod_all = jnp.zeros((depth, 16, N_MOD * d), F32)
    for l in range(depth):
        dmod_all = dmod_all.at[l, 0:N_DEV].set(every[l * nl][:, :]).at[l, N_DEV].set(tot(l, 1))
    dmod_cols = lax.dynamic_slice_in_dim(dmod_all, chip * n_ada, n_ada, axis=2)
    for l in range(depth):
        slab = OView((depth, d, n_ada), functools.partial(lambda i, j, kk, l: (l, i, j), l=l), None, out.get("ada_w"))
        out["ada_w"] = matmul(f"ada_dw{l}", act, dmod_cols[l], ta=True, o_view=slab)
    dact = matmul("ada_dx0", dmod_cols[0], ada_w, tb=True, view=ada_view(0, n_ada, True))
    for l in range(1, depth):
        dact = matmul(f"ada_dx{l}", dmod_cols[l], ada_w, tb=True, view=ada_view(l, n_ada, True), add=dact)
    got = _per_device(all_gather_small("gather_dcctx", _pack([dact[N_DEV]])))
    sibling_arrived(got)
    got = got[0::2, :d]
    dsilu = got[0] + got[1] + got[2] + got[3]
    sig = jax.nn.sigmoid(c_ctx)
    out["c_ctx"] = dsilu * (sig + c_ctx * sig * (1.0 - sig))

    deltas, new_m, new_v = {}, {}, {}

    def update(name):
        out[name] = out[name].reshape(weights[name].shape)
        deltas[name], new_m[name], new_v[name] = adamw("adamw_" + name, weights[name], out[name], mom_m[name], mom_v[name])

    for name in WEIGHT_NAMES:
        if name not in big:
            update(name)
    behind = new_v["ada_w"]
    joining = []

    def joined(after):
        name, in_flight_halves = joining.pop()
        per_layer = [split_wait(f"rs_join_wait_{name}{l}", JOIN, grp, after)[0] for l, grp in enumerate(in_flight_halves)]
        (deltas[name], new_m[name], new_v[name]), out[name] = adamw_layers(
            "adamw_" + name, weights[name], per_layer, mom_m[name], mom_v[name])
        return new_v[name]

    for name in ("w_down", "w_up", "w_out", "w_in"):
        halves = []
        for l in range(depth):
            sums, got = split_wait(f"rs_wait_{name}{l}", SCATTER, reducing[(l, name)], behind)
            halves.append(add_chip_sums(f"rs_add_chips_{name}{l}", sums, got, chip, ci))
        in_flight_halves, _ = split_start("rs_join_" + name, JOIN, halves)
        if joining:
            behind = joined(behind)
        joining.append((name, in_flight_halves))
    joined(behind)
    grad_x = dx[:n_lat].reshape(x.shape)
    return (loss_total, grad_x, *[out[n] for n in WEIGHT_NAMES], *[deltas[n] for n in WEIGHT_NAMES],
            *[new_m[n] for n in WEIGHT_NAMES], *[new_v[n] for n in WEIGHT_NAMES])
```

```python
import functools
from typing import NamedTuple

import numpy as np
import jax
import jax.numpy as jnp
from jax import lax
from jax.experimental import pallas as pl
from jax.experimental.pallas import tpu as pltpu

F32 = jnp.float32
BF16 = jnp.bfloat16

D_MODEL = 2048
HEAD_DIM = 128
ATT_Q_HEADS = 8
ATT_KV_HEADS = 2
ATT_GROUP = ATT_Q_HEADS // ATT_KV_HEADS
RET_HEADS = 4
GLA_HEADS = 4
GLA_DK = 64
GLA_DV = 128
GLA_RANK = 16
GLA_TAU = 16.0
RET_CHUNK = 256
GLA_CHUNK = 128
GRID_W = 64
ROPE_THETA = 10000.0
N_MOD = 6
EPS = 1e-6
N_MAIN = 5120
N_GATE = 2 * GLA_RANK
LANES = 128
ROW_TILE = 256
FFN_COL_TILE = 256
VMEM_LIMIT = 56 * 1024 * 1024

ADAM_LR = 0.001
ADAM_B1 = 0.9
ADAM_B2 = 0.999
ADAM_EPS = 1e-08
ADAM_WD = 0.01
ADAM_STEP = 10

Z_AQ, Z_AK, Z_AV = 0, 1024, 1280
Z_RQ, Z_RK, Z_RV, Z_RG = 1536, 2048, 2560, 3072
Z_GQ, Z_GK, Z_GV, Z_GR = 3584, 3840, 4096, 4608
P_AQ, P_AK, P_GQ, P_RQ, P_RK, P_LA = 0, 1024, 1280, 1536, 2048, 2560
P_W = 3072


def _params(sem=None):
    return pltpu.CompilerParams(dimension_semantics=sem, vmem_limit_bytes=VMEM_LIMIT)


def _pick(n, cands):
    for c in cands:
        if n % c == 0:
            return c
    return n


_NN = (((1,), (0,)), ((), ()))
_NT = (((1,), (1,)), ((), ()))
_TN = (((0,), (0,)), ((), ()))


def _dg(a, b, dims):
    return lax.dot_general(a.astype(BF16), b.astype(BF16), dims, preferred_element_type=F32)


@jax.custom_vjp
def bdot(a, b):
    return _dg(a, b, _NN)


def _bdot_fwd(a, b):
    return _dg(a, b, _NN), (a, b)


def _bdot_bwd(res, ct):
    a, b = res
    return _dg(ct, b, _NT), _dg(a, ct, _TN)


bdot.defvjp(_bdot_fwd, _bdot_bwd)


@jax.custom_vjp
def bdot_nt(a, b):
    return _dg(a, b, _NT)


def _bdot_nt_fwd(a, b):
    return _dg(a, b, _NT), (a, b)


def _bdot_nt_bwd(res, ct):
    a, b = res
    return _dg(ct, b, _NN), _dg(ct, a, _TN)


bdot_nt.defvjp(_bdot_nt_fwd, _bdot_nt_bwd)


@jax.custom_vjp
def bdot_tn(a, b):
    return _dg(a, b, _TN)


def _bdot_tn_fwd(a, b):
    return _dg(a, b, _TN), (a, b)


def _bdot_tn_bwd(res, ct):
    a, b = res
    return _dg(b, ct, _NT), _dg(a, ct, _NN)


bdot_tn.defvjp(_bdot_tn_fwd, _bdot_tn_bwd)


def _split3(x):
    x1 = x.astype(BF16)
    r1 = x - x1.astype(F32)
    x2 = r1.astype(BF16)
    x3 = (r1 - x2.astype(F32)).astype(BF16)
    return x1, x2, x3


def _mask_dot(mask_bf16, x, dims):
    x1, x2, x3 = _split3(x)
    f = lambda t: lax.dot_general(mask_bf16, t, dims, preferred_element_type=F32)
    return f(x1) + f(x2) + f(x3)


@jax.custom_vjp
def mask_cumsum(mask, x):
    return _mask_dot(mask.astype(BF16), x, _NN)


def _mask_cumsum_fwd(mask, x):
    return mask_cumsum(mask, x), mask


def _mask_cumsum_bwd(mask, ct):
    return jnp.zeros_like(mask), _mask_dot(mask.astype(BF16), ct, _TN)


mask_cumsum.defvjp(_mask_cumsum_fwd, _mask_cumsum_bwd)


def _roll(x, shift, axis):
    return pltpu.roll(x, shift % x.shape[axis], axis)


@functools.partial(jax.custom_vjp, nondiff_argnums=(1, 2))
def roll(x, shift, axis):
    return _roll(x, shift, axis)


def _roll_fwd(x, shift, axis):
    return _roll(x, shift, axis), None


def _roll_bwd(shift, axis, _, ct):
    return (_roll(ct, -shift, axis),)


roll.defvjp(_roll_fwd, _roll_bwd)


def rms(x):
    return x * lax.rsqrt(jnp.mean(x * x, axis=-1, keepdims=True) + EPS)


def silu(x):
    return x * (0.5 + 0.5 * jnp.tanh(0.5 * x))


def log_sigmoid(x):
    return jnp.minimum(x, 0.0) - jnp.log(1.0 + jnp.exp(-jnp.abs(x)))


def rope(t, cos, sin):
    return t * cos + roll(t, HEAD_DIM // 2, 1) * sin


def _heads(x, n, width=HEAD_DIM):
    return [x[:, h * width:(h + 1) * width] for h in range(n)]


class Row(NamedTuple):
    arr: jax.Array
    width: int
    idx: int = 0
    diff: bool = True


class Par(NamedTuple):
    arr: jax.Array
    grouped: bool = False
    diff: bool = True


def _row_specs(rows, pars, tm, n_lat_tiles):
    def grp(i):
        return jnp.minimum(i // n_lat_tiles, 1)

    specs = [pl.BlockSpec((tm, r.width), functools.partial(lambda i, k: (i, k), k=r.idx)) for r in rows]
    for p in pars:
        blk = (1,) + p.arr.shape[1:]
        if p.grouped:
            specs.append(pl.BlockSpec(blk, lambda i: (grp(i), 0, 0)))
        else:
            specs.append(pl.BlockSpec(blk, lambda i: (0, 0, 0)))
    return specs


def row_map(name, fn, rows, pars, outs, n_rows, n_lat):
    tm = ROW_TILE
    nr, npar = len(rows), len(pars)

    def body(*refs):
        vals = [r[...] for r in refs[:nr]] + [p[0] for p in refs[nr:nr + npar]]
        res = fn(*vals)
        for o, v in zip(refs[nr + npar:], res):
            o[...] = v.astype(o.dtype)

    return pl.pallas_call(
        body, name=name, grid=(n_rows // tm,),
        in_specs=_row_specs(rows, pars, tm, n_lat // tm),
        out_specs=[pl.BlockSpec((tm, w), lambda i: (i, 0)) for w, _ in outs],
        out_shape=[jax.ShapeDtypeStruct((n_rows, w), dt) for w, dt in outs],
        compiler_params=_params(("arbitrary",)),
    )(*[r.arr for r in rows], *[p.arr for p in pars])


def row_vjp(name, fn, rows, pars, cts, n_rows, n_lat, add_to_first=None, row_grad_dtype=F32, after=None):
    tm = ROW_TILE
    nr, npar, nc = len(rows), len(pars), len(cts)
    n_lat_tiles = n_lat // tm
    args = list(rows) + list(pars)
    diff_pos = [k for k, a in enumerate(args) if a.diff]
    n_add = 0 if add_to_first is None else 1
    n_after = 0 if after is None else 1

    def body(*refs):
        i = pl.program_id(0)
        vals = [r[...] for r in refs[:nr]] + [p[0] for p in refs[nr:nr + npar]]
        ct_vals = tuple(c[...] for c in refs[nr + npar:nr + npar + nc])
        out_refs = refs[nr + npar + nc + n_add + n_after:]

        def g(*dv):
            full = list(vals)
            for k, v in zip(diff_pos, dv):
                full[k] = v
            return tuple(fn(*full))

        _, vjp = jax.vjp(g, *[vals[k] for k in diff_pos])
        grads = vjp(ct_vals)
        for n, (k, o, gr) in enumerate(zip(diff_pos, out_refs, grads)):
            if k < nr:
                o[...] = (gr + refs[nr + npar + nc][...] if (n == 0 and n_add) else gr).astype(o.dtype)
            else:
                first = (i == 0) | (i == n_lat_tiles) if args[k].grouped else (i == 0)

                @pl.when(first)
                def _():
                    o[0] = gr

                @pl.when(jnp.logical_not(first))
                def _():
                    o[0] += gr

    def grp(i):
        return jnp.minimum(i // n_lat_tiles, 1)

    out_specs, out_shape = [], []
    for k in diff_pos:
        a = args[k]
        if k < nr:
            out_specs.append(pl.BlockSpec((tm, a.width), lambda i: (i, 0)))
            out_shape.append(jax.ShapeDtypeStruct((n_rows, a.width), row_grad_dtype))
        else:
            blk = (1,) + a.arr.shape[1:]
            out_specs.append(pl.BlockSpec(blk, (lambda i: (grp(i), 0, 0)) if a.grouped else (lambda i: (0, 0, 0))))
            out_shape.append(jax.ShapeDtypeStruct(a.arr.shape, F32))
    extra = list(cts) + ([add_to_first] if n_add else [])
    ct_specs = [pl.BlockSpec((tm, c.shape[1]), lambda i: (i, 0)) for c in extra]
    if n_after:
        extra.append(after)
        ct_specs.append(pl.BlockSpec(memory_space=pl.ANY))
    return pl.pallas_call(
        body, name=name, grid=(n_rows // tm,),
        in_specs=_row_specs(rows, pars, tm, n_lat_tiles) + ct_specs,
        out_specs=out_specs, out_shape=out_shape,
        compiler_params=_params(("arbitrary",)),
    )(*[r.arr for r in rows], *[p.arr for p in pars], *extra)


class BView(NamedTuple):
    n: int
    k: int
    tn: int
    tk: int
    index_map: object
    lead: int = 1
    part_maps: tuple = ()


MATMUL_VMEM_BUDGET = 40 * 1024 * 1024


def _matmul_tiles(m, n, k, a_bytes, b_bytes, o_bytes):
    tms = [c for c in (1152, 1024, 768, 512, 256, 128) if m % c == 0] or [m]
    tns = [c for c in (2048, 1408, 1280, 1024, 768, 512, 256, 128) if n % c == 0] or [n]
    tks = [k] + [c for c in (2816, 2304, 2048, 1408, 1024, 512, 256, 128) if k % c == 0 and c < k]
    for tk in tks:
        fits = [(tm * tn, tm, tn) for tm in tms for tn in tns
                if 2 * (tm * tk * a_bytes + tk * tn * b_bytes + tm * tn * o_bytes) + 2 * tm * tn * 4 <= MATMUL_VMEM_BUDGET]
        if fits and (max(fits)[0] >= min(512 * 512, tms[0] * tns[0]) or tk == tks[-1]):
            _, tm, tn = max(fits)
            return tm, tn, tk
    raise ValueError(f"no matmul tiling for {(m, n, k)}")


class OView(NamedTuple):
    shape: tuple
    index_map: object
    tn: int = None
    into: object = None


def matmul(name, a, b, *, ta=False, tb=False, add=None, out_dtype=F32, view=None, o_view=None, after=None):
    m = a.shape[1] if ta else a.shape[0]
    o_bytes = jnp.dtype(out_dtype).itemsize * (1 if add is None else 2)
    if view is None:
        k = a.shape[0] if ta else a.shape[1]
        n = b.shape[0] if tb else b.shape[1]
        assert (b.shape[1] if tb else b.shape[0]) == k, (a.shape, b.shape, ta, tb)
        if o_view is not None and o_view.tn is not None:
            tn = o_view.tn
            tm, _, tk = _matmul_tiles(m, tn, k, a.dtype.itemsize, b.dtype.itemsize, o_bytes)
        else:
            tm, tn, tk = _matmul_tiles(m, n, k, a.dtype.itemsize, b.dtype.itemsize, o_bytes)
    else:
        n, k, tn, tk = view.n, view.k, view.tn, view.tk
        b_maps = view.part_maps or (view.index_map,)
        tm, _, whole = _matmul_tiles(m, tn, tk * len(b_maps), a.dtype.itemsize, b.dtype.itemsize, o_bytes)
        assert whole == tk * len(b_maps) and not (ta and len(b_maps) > 1), (name, tm, whole)
    parts = 1 if view is None else len(b_maps)
    k_step = tk * parts
    nk = k // k_step
    dims = (((0 if ta else 1,), (1 if tb else 0,)), ((), ()))

    def body(a_ref, *rest):
        b_refs, rest = rest[:parts], rest[parts:]
        if parts == 1:
            prod = lax.dot_general(a_ref[...].astype(BF16), b_refs[0][...].astype(BF16), dims, preferred_element_type=F32)
        else:
            prod = sum(lax.dot_general(a_ref[:, p * tk:(p + 1) * tk].astype(BF16), b_refs[p][...].astype(BF16), dims,
                                       preferred_element_type=F32) for p in range(parts))
        if nk == 1:
            o_ref = rest[-1]
            o_ref[...] = (prod if add is None else prod + rest[0][...]).astype(o_ref.dtype)
            return
        o_ref, acc = rest[-2:]
        kk = pl.program_id(2)

        @pl.when(kk == 0)
        def _():
            acc[...] = prod

        @pl.when(kk != 0)
        def _():
            acc[...] += prod

        @pl.when(kk == nk - 1)
        def _():
            r = acc[...]
            if add is not None:
                r = r + rest[0][...]
            o_ref[...] = r.astype(o_ref.dtype)

    if ta:
        a_spec = pl.BlockSpec((k_step, tm), lambda i, j, kk: (kk, i))
    else:
        a_spec = pl.BlockSpec((tm, k_step), lambda i, j, kk: (i, kk))
    b_tile = (tn, tk) if tb else (tk, tn)
    if view is not None:
        b_specs = [pl.BlockSpec((None,) * view.lead + b_tile, index_map) for index_map in b_maps]
    elif tb:
        b_specs = [pl.BlockSpec(b_tile, lambda i, j, kk: (j, kk))]
    else:
        b_specs = [pl.BlockSpec(b_tile, lambda i, j, kk: (kk, j))]
    o_spec = pl.BlockSpec((tm, tn), lambda i, j, kk: (i, j))
    ins = [a] + [b] * parts + ([add] if add is not None else [])
    in_specs = [a_spec] + b_specs + ([o_spec] if add is not None else [])
    out_shape, aliases = jax.ShapeDtypeStruct((m, n), out_dtype), {}
    if o_view is not None:
        assert add is None
        o_spec = pl.BlockSpec((None, tm, tn), o_view.index_map)
        out_shape = jax.ShapeDtypeStruct(o_view.shape, out_dtype)
        if o_view.into is not None:
            aliases = {len(ins): 0}
            ins.append(o_view.into)
            in_specs.append(pl.BlockSpec(memory_space=pl.ANY))
    if after is not None:
        ins.append(after)
        in_specs.append(pl.BlockSpec(memory_space=pl.ANY))
    return pl.pallas_call(
        body, name=name, grid=(m // tm, n // tn, nk),
        in_specs=in_specs, out_specs=o_spec, out_shape=out_shape, input_output_aliases=aliases,
        scratch_shapes=[pltpu.VMEM((tm, tn), F32)] if nk > 1 else [],
        compiler_params=_params(("parallel", "parallel", "arbitrary")),
    )(*ins)


def normmod_tile(x, g, shift, scale):
    return (rms(x) * g * (1.0 + scale) + shift,)


def resid_tile(x, y, gate):
    return (x + gate * y,)


def gated_tile(y, gate):
    return (gate * y,)


def prep_tile(z_qk, z_rq, z_rk, z_gq, zg, cos, sin, qg, kg, gate_up, gate_b):
    out = []
    for h, t in enumerate(_heads(z_qk, ATT_Q_HEADS + ATT_KV_HEADS)):
        out.append(rope(rms(t) * (qg if h < ATT_Q_HEADS else kg), cos, sin))
    gq = z_gq * (GLA_DK ** -0.5)
    rq = [rope(t, cos, sin) for t in _heads(z_rq, RET_HEADS)]
    rk = [rope(t * (HEAD_DIM ** -0.5), cos, sin) for t in _heads(z_rk, RET_HEADS)]
    la = [log_sigmoid(bdot(zg, gate_up[d * LANES:(d + 1) * LANES]) + gate_b[d:d + 1]) * (1.0 / GLA_TAU) for d in range(2)]
    return (jnp.concatenate(out + [gq] + rq + rk + la, axis=1),)


def post_tile(o_att, o_ret_f, o_ret_b, o_gla_f, o_gla_b, rg, gr, ret_g, gla_g):
    ret = jnp.concatenate([rms(t) * ret_g for t in _heads(o_ret_f + o_ret_b, RET_HEADS)], axis=1) * silu(rg)
    gla = jnp.concatenate([rms(t) * gla_g for t in _heads(o_gla_f + o_gla_b, GLA_HEADS)], axis=1) * silu(gr)
    return (jnp.concatenate([o_att, ret, gla], axis=1),)


def _convglu_tile(n_lat, a, v, cw, cb):
    t = a.shape[0]
    row = lax.broadcasted_iota(jnp.int32, (t, 1), 0)
    has_prev = ((row != 0) & (row != n_lat)).astype(F32)
    has_next = ((row != n_lat - 1) & (row != t - 1)).astype(F32)
    conv = roll(a, 1, 0) * has_prev * cw[0:1] + a * cw[1:2] + roll(a, -1, 0) * has_next * cw[2:3] + cb
    return silu(conv) * v


def convglu(name, u, cw, cb, n_lat):
    t, f2 = u.shape
    f, tc = f2 // 2, FFN_COL_TILE
    nb = f // tc

    def body(a_ref, v_ref, cw_ref, cb_ref, o_ref):
        o_ref[...] = _convglu_tile(n_lat, a_ref[...].astype(F32), v_ref[...].astype(F32), cw_ref[...], cb_ref[...]).astype(o_ref.dtype)

    return pl.pallas_call(
        body, name=name, grid=(nb,),
        in_specs=[pl.BlockSpec((t, tc), lambda j: (0, j)), pl.BlockSpec((t, tc), lambda j: (0, nb + j)),
                  pl.BlockSpec((3, tc), lambda j: (0, j)), pl.BlockSpec((1, tc), lambda j: (0, j))],
        out_specs=pl.BlockSpec((t, tc), lambda j: (0, j)),
        out_shape=jax.ShapeDtypeStruct((t, f), BF16),
        compiler_params=_params(("parallel",)),
    )(u, u, cw, cb)


def convglu_bwd(name, u, cw, cb, dg, n_lat):
    t, f2 = u.shape
    f, tc = f2 // 2, FFN_COL_TILE
    nb = f // tc

    def body(a_ref, v_ref, cw_ref, cb_ref, dg_ref, da_ref, dv_ref, dcw_ref, dcb_ref):
        _, vjp = jax.vjp(functools.partial(_convglu_tile, n_lat), a_ref[...].astype(F32), v_ref[...].astype(F32),
                         cw_ref[...], cb_ref[...])
        da, dv, dcw_ref[...], dcb_ref[...] = vjp(dg_ref[...])
        da_ref[...], dv_ref[...] = da.astype(BF16), dv.astype(BF16)

    col = pl.BlockSpec((t, tc), lambda j: (0, j))
    return pl.pallas_call(
        body, name=name, grid=(nb,),
        in_specs=[col, pl.BlockSpec((t, tc), lambda j: (0, nb + j)), pl.BlockSpec((3, tc), lambda j: (0, j)),
                  pl.BlockSpec((1, tc), lambda j: (0, j)), col],
        out_specs=[col, col, pl.BlockSpec((3, tc), lambda j: (0, j)), pl.BlockSpec((1, tc), lambda j: (0, j))],
        out_shape=[jax.ShapeDtypeStruct((t, f), BF16), jax.ShapeDtypeStruct((t, f), BF16),
                   jax.ShapeDtypeStruct((3, f), F32), jax.ShapeDtypeStruct((1, f), F32)],
        compiler_params=_params(("parallel",)),
    )(u, u, cw, cb, dg)


def final_loss(x, target, g, n_lat):
    tm = ROW_TILE
    d = x.shape[1]

    def body(x_ref, t_ref, g_ref, loss_ref, dx_ref, dg_ref):
        i = pl.program_id(0)
        tgt = t_ref[...]

        def f(xv, gv):
            e = rms(xv) * gv - tgt
            s = jnp.sum(jnp.sum(e * e, axis=1, keepdims=True), axis=0, keepdims=True)
            return s * (0.5 / d)

        val, vjp = jax.vjp(f, x_ref[...], g_ref[...])
        dx, dgv = vjp(jnp.ones((1, 1), F32))
        dx_ref[...] = dx

        @pl.when(i == 0)
        def _():
            dg_ref[...] = dgv
            loss_ref[...] = jnp.broadcast_to(val, loss_ref.shape)

        @pl.when(i != 0)
        def _():
            dg_ref[...] += dgv
            loss_ref[...] += jnp.broadcast_to(val, loss_ref.shape)

    return pl.pallas_call(
        body, name="final_loss", grid=(n_lat // tm,),
        in_specs=[pl.BlockSpec((tm, d), lambda i: (i, 0)), pl.BlockSpec((tm, d), lambda i: (i, 0)),
                  pl.BlockSpec((1, d), lambda i: (0, 0))],
        out_specs=[pl.BlockSpec((1, LANES), lambda i: (0, 0)), pl.BlockSpec((tm, d), lambda i: (i, 0)),
                   pl.BlockSpec((1, d), lambda i: (0, 0))],
        out_shape=[jax.ShapeDtypeStruct((1, LANES), F32), jax.ShapeDtypeStruct((n_lat, d), F32),
                   jax.ShapeDtypeStruct((1, d), F32)],
        compiler_params=_params(("arbitrary",)),
    )(x, target, g)


ATT_SCALE = HEAD_DIM ** -0.5
_AK_BLK = P_AK // HEAD_DIM
_AV_BLK = Z_AV // HEAD_DIM


def _att_specs(t, tq):
    gw = ATT_GROUP * HEAD_DIM
    q_spec = pl.BlockSpec((tq, gw), lambda kv, i: (i, kv))
    k_spec = pl.BlockSpec((t, HEAD_DIM), lambda kv, i: (0, _AK_BLK + kv))
    v_spec = pl.BlockSpec((t, HEAD_DIM), lambda kv, i: (0, _AV_BLK + kv))
    row_spec = pl.BlockSpec((ATT_GROUP, tq, 1), lambda kv, i: (kv, i, 0))
    return q_spec, k_spec, v_spec, row_spec


def _att_mask(i, t, tq, n_lat):
    col = lax.broadcasted_iota(jnp.int32, (1, t), 1)
    return jnp.where((i >= n_lat // tq) & (col < n_lat), -jnp.inf, 0.0).astype(F32)


def attn_fwd(p, z, n_lat):
    t = p.shape[0]
    tq = ROW_TILE

    def body(q_ref, k_ref, v_ref, o_ref, lse_ref):
        mask = _att_mask(pl.program_id(1), t, tq, n_lat)
        k, v = k_ref[...].astype(BF16), v_ref[...].astype(BF16)
        for g in range(ATT_GROUP):
            cols = slice(g * HEAD_DIM, (g + 1) * HEAD_DIM)
            s = _dg(q_ref[:, cols], k, _NT) * ATT_SCALE + mask
            m = jnp.max(s, axis=1, keepdims=True)
            pr = jnp.exp(s - m)
            l = jnp.sum(pr, axis=1, keepdims=True)
            o_ref[:, cols] = _dg(pr, v, _NN) / l
            lse_ref[g] = m + jnp.log(l)

    q_spec, k_spec, v_spec, row_spec = _att_specs(t, tq)
    return pl.pallas_call(
        body, name="attn_fwd", grid=(ATT_KV_HEADS, t // tq),
        in_specs=[q_spec, k_spec, v_spec], out_specs=[q_spec, row_spec],
        out_shape=[jax.ShapeDtypeStruct((t, ATT_Q_HEADS * HEAD_DIM), F32),
                   jax.ShapeDtypeStruct((ATT_Q_HEADS, t, 1), F32)],
        compiler_params=_params(("parallel", "parallel")),
    )(p, p, z)


def attn_bwd(p, z, o, lse, do, n_lat):
    t = p.shape[0]
    tq = ROW_TILE

    def body(q_ref, k_ref, v_ref, o_ref, do_ref, lse_ref, dq_ref, dk_ref, dv_ref):
        i = pl.program_id(1)

        @pl.when(i == 0)
        def _():
            dk_ref[...] = jnp.zeros_like(dk_ref)
            dv_ref[...] = jnp.zeros_like(dv_ref)

        mask = _att_mask(i, t, tq, n_lat)
        k, v = k_ref[...].astype(BF16), v_ref[...].astype(BF16)
        dk, dv = dk_ref[...], dv_ref[...]
        for g in range(ATT_GROUP):
            cols = slice(g * HEAD_DIM, (g + 1) * HEAD_DIM)
            q, do_g = q_ref[:, cols].astype(BF16), do_ref[:, cols]
            pr = jnp.exp(_dg(q, k, _NT) * ATT_SCALE + mask - lse_ref[g])
            delta = jnp.sum(o_ref[:, cols] * do_g, axis=1, keepdims=True)
            ds = pr * (_dg(do_g, v, _NT) - delta) * ATT_SCALE
            dq_ref[:, cols] = _dg(ds, k, _NN)
            dk = dk + _dg(ds, q, _TN)
            dv = dv + _dg(pr, do_g, _TN)
        dk_ref[...], dv_ref[...] = dk, dv

    q_spec, k_spec, v_spec, row_spec = _att_specs(t, tq)
    kv_out = pl.BlockSpec((t, HEAD_DIM), lambda kv, i: (0, kv))
    return pl.pallas_call(
        body, name="attn_bwd", grid=(ATT_KV_HEADS, t // tq),
        in_specs=[q_spec, k_spec, v_spec, q_spec, q_spec, row_spec],
        out_specs=[q_spec, kv_out, kv_out],
        out_shape=[jax.ShapeDtypeStruct((t, ATT_Q_HEADS * HEAD_DIM), F32),
                   jax.ShapeDtypeStruct((t, ATT_KV_HEADS * HEAD_DIM), F32),
                   jax.ShapeDtypeStruct((t, ATT_KV_HEADS * HEAD_DIM), F32)],
        compiler_params=_params(("parallel", "arbitrary")),
    )(p, p, z, o, do, lse)


_RQ_BLK = P_RQ // HEAD_DIM
_RK_BLK = P_RK // HEAD_DIM
_RV_BLK = Z_RV // HEAD_DIM


def _scan_chunk(direction, step, n_chunks, n_lat_chunks):
    return jnp.where(direction == 0, (step + n_lat_chunks) % n_chunks, n_chunks - 1 - step)


def _ret_geometry(direction):
    c = RET_CHUNK
    i = lax.broadcasted_iota(jnp.int32, (c, c), 0)
    j = lax.broadcasted_iota(jnp.int32, (c, c), 1)
    rel = jnp.where(direction == 0, i - j, j - i).astype(F32)
    r = lax.broadcasted_iota(jnp.int32, (c, 1), 0)
    pos = jnp.where(direction == 0, r, c - 1 - r).astype(F32)
    return rel, pos


def ret_chunk(q, k, v, s, lg, rel, pos):
    c = RET_CHUNK
    causal = rel >= 0
    d_in = jnp.where(causal, jnp.exp(lg * jnp.where(causal, rel, 0.0)), 0.0)
    q_dec = jnp.exp(lg * (pos + 1.0))
    k_dec = jnp.exp(lg * (c - 1.0 - pos))
    c_dec = jnp.exp(lg * c)
    att = bdot_nt(q, k) * d_in
    o = bdot(att, v) + bdot(q * q_dec, s)
    s_new = c_dec * s + bdot_tn(k * k_dec, v)
    return o, s_new


def ret_fwd(p, z, lg, n_lat):
    t = p.shape[0]
    c = RET_CHUNK
    nc, nlc = t // c, n_lat // c

    def body(q_ref, k_ref, v_ref, lg_ref, o_ref, ssave_ref, s_s):
        d, n = pl.program_id(0), pl.program_id(1)

        @pl.when(n == 0)
        def _():
            s_s[...] = jnp.zeros_like(s_s)

        rel, pos = _ret_geometry(d)
        for h in range(RET_HEADS):
            cols = slice(h * HEAD_DIM, (h + 1) * HEAD_DIM)
            ssave_ref[0, h, 0] = s_s[h]
            o, s_new = ret_chunk(q_ref[:, cols], k_ref[:, cols], v_ref[:, cols], s_s[h], lg_ref[0, h], rel, pos)
            o_ref[:, cols] = o
            s_s[h] = s_new

    w = RET_HEADS * HEAD_DIM

    def blk(base):
        return pl.BlockSpec((c, w), lambda d, n: (_scan_chunk(d, n, nc, nlc), base // RET_HEADS))

    return pl.pallas_call(
        body, name="ret_fwd", grid=(2, nc),
        in_specs=[blk(_RQ_BLK), blk(_RK_BLK), blk(_RV_BLK), pl.BlockSpec((1, RET_HEADS, 1, 1), lambda d, n: (d, 0, 0, 0))],
        out_specs=[pl.BlockSpec((c, w), lambda d, n: (_scan_chunk(d, n, nc, nlc), d)),
                   pl.BlockSpec((1, RET_HEADS, 1, HEAD_DIM, HEAD_DIM), lambda d, n: (d, 0, n, 0, 0))],
        out_shape=[jax.ShapeDtypeStruct((t, 2 * w), F32),
                   jax.ShapeDtypeStruct((2, RET_HEADS, nc, HEAD_DIM, HEAD_DIM), F32)],
        scratch_shapes=[pltpu.VMEM((RET_HEADS, HEAD_DIM, HEAD_DIM), F32)],
        compiler_params=_params(("parallel", "arbitrary")),
    )(p, p, z, lg)


def ret_bwd(p, z, lg, states, do, n_lat):
    t = p.shape[0]
    c = RET_CHUNK
    nc, nlc = t // c, n_lat // c

    def body(q_ref, k_ref, v_ref, lg_ref, s_ref, do_ref, dq_ref, dk_ref, dv_ref, dlg_ref, ds_s):
        d, n = pl.program_id(0), pl.program_id(1)

        @pl.when(n == 0)
        def _():
            ds_s[...] = jnp.zeros_like(ds_s)
            dlg_ref[...] = jnp.zeros_like(dlg_ref)

        rel, pos = _ret_geometry(d)
        f = functools.partial(ret_chunk, rel=rel, pos=pos)
        for h in range(RET_HEADS):
            cols = slice(h * HEAD_DIM, (h + 1) * HEAD_DIM)
            _, vjp = jax.vjp(f, q_ref[:, cols], k_ref[:, cols], v_ref[:, cols], s_ref[0, h, 0], lg_ref[0, h])
            dq, dk, dv, ds, dlg = vjp((do_ref[:, cols], ds_s[h]))
            dq_ref[:, cols], dk_ref[:, cols], dv_ref[:, cols] = dq, dk, dv
            ds_s[h] = ds
            dlg_ref[0, h] += dlg

    def chunk_of(d, n):
        return _scan_chunk(d, nc - 1 - n, nc, nlc)

    w = RET_HEADS * HEAD_DIM

    def blk(base):
        return pl.BlockSpec((c, w), lambda d, n: (chunk_of(d, n), base // RET_HEADS))

    out_blk = pl.BlockSpec((c, w), lambda d, n: (chunk_of(d, n), d))
    lg_blk = pl.BlockSpec((1, RET_HEADS, 1, 1), lambda d, n: (d, 0, 0, 0))
    grad_shape = jax.ShapeDtypeStruct((t, 2 * w), F32)
    return pl.pallas_call(
        body, name="ret_bwd", grid=(2, nc),
        in_specs=[blk(_RQ_BLK), blk(_RK_BLK), blk(_RV_BLK), lg_blk,
                  pl.BlockSpec((1, RET_HEADS, 1, HEAD_DIM, HEAD_DIM), lambda d, n: (d, 0, nc - 1 - n, 0, 0)),
                  pl.BlockSpec((c, w), lambda d, n: (chunk_of(d, n), 0))],
        out_specs=[out_blk, out_blk, out_blk, lg_blk],
        out_shape=[grad_shape, grad_shape, grad_shape, jax.ShapeDtypeStruct((2, RET_HEADS, 1, 1), F32)],
        scratch_shapes=[pltpu.VMEM((RET_HEADS, HEAD_DIM, HEAD_DIM), F32)],
        compiler_params=_params(("parallel", "arbitrary")),
    )(p, p, z, lg, states, do)


_GQ_BLK = P_GQ // (GLA_HEADS * GLA_DK)
_GK_BLK = Z_GK // (GLA_HEADS * GLA_DK)
_GV_BLK = Z_GV // (GLA_HEADS * GLA_DV)
_LA_BLK = P_LA // (GLA_HEADS * GLA_DK)


def _gla_mask(direction):
    c = GLA_CHUNK
    i = lax.broadcasted_iota(jnp.int32, (c, c), 0)
    j = lax.broadcasted_iota(jnp.int32, (c, c), 1)
    return (jnp.where(direction == 0, i - j, j - i) >= 0).astype(F32)


def gla_chunk(q, k, v, la, st, mask):
    b = mask_cumsum(mask, la)
    btot = jnp.sum(la, axis=0, keepdims=True)
    half = 0.5 * btot
    qt, kt = q * jnp.exp(b - half), k * jnp.exp(half - b)
    qs, ke = q * jnp.exp(b), k * jnp.exp(btot - b)
    outs, upd = [], []
    for h in range(GLA_HEADS):
        ks = slice(h * GLA_DK, (h + 1) * GLA_DK)
        vh = v[:, h * GLA_DV:(h + 1) * GLA_DV]
        att = bdot_nt(qt[:, ks], kt[:, ks]) * mask
        outs.append(bdot(att, vh) + bdot_nt(qs[:, ks], st[:, ks]))
        upd.append(bdot_tn(vh, ke[:, ks]))
    st_new = st * jnp.exp(btot) + jnp.concatenate(upd, axis=1)
    return jnp.concatenate(outs, axis=1), st_new


def gla_fwd(p, z, n_lat):
    t = p.shape[0]
    c = GLA_CHUNK
    nc, nlc = t // c, n_lat // c
    kw, vw = GLA_HEADS * GLA_DK, GLA_HEADS * GLA_DV

    def body(q_ref, k_ref, v_ref, la_ref, o_ref, ssave_ref, s_s):
        d, n = pl.program_id(0), pl.program_id(1)

        @pl.when(n == 0)
        def _():
            s_s[...] = jnp.zeros_like(s_s)

        ssave_ref[0, 0] = s_s[...]
        o, s_new = gla_chunk(q_ref[...], k_ref[...], v_ref[...], la_ref[...], s_s[...], _gla_mask(d))
        o_ref[...] = o
        s_s[...] = s_new

    def chunk_of(d, n):
        return _scan_chunk(d, n, nc, nlc)

    return pl.pallas_call(
        body, name="gla_fwd", grid=(2, nc),
        in_specs=[pl.BlockSpec((c, kw), lambda d, n: (chunk_of(d, n), _GQ_BLK)),
                  pl.BlockSpec((c, kw), lambda d, n: (chunk_of(d, n), _GK_BLK)),
                  pl.BlockSpec((c, vw), lambda d, n: (chunk_of(d, n), _GV_BLK)),
                  pl.BlockSpec((c, kw), lambda d, n: (chunk_of(d, n), _LA_BLK + d))],
        out_specs=[pl.BlockSpec((c, vw), lambda d, n: (chunk_of(d, n), d)),
                   pl.BlockSpec((1, 1, GLA_DV, kw), lambda d, n: (d, n, 0, 0))],
        out_shape=[jax.ShapeDtypeStruct((t, 2 * vw), F32), jax.ShapeDtypeStruct((2, nc, GLA_DV, kw), F32)],
        scratch_shapes=[pltpu.VMEM((GLA_DV, kw), F32)],
        compiler_params=_params(("parallel", "arbitrary")),
    )(p, z, z, p)


def gla_bwd(p, z, states, do, n_lat):
    t = p.shape[0]
    c = GLA_CHUNK
    nc, nlc = t // c, n_lat // c
    kw, vw = GLA_HEADS * GLA_DK, GLA_HEADS * GLA_DV

    def body(q_ref, k_ref, v_ref, la_ref, s_ref, do_ref, dq_ref, dk_ref, dv_ref, dla_ref, ds_s):
        d, n = pl.program_id(0), pl.program_id(1)

        @pl.when(n == 0)
        def _():
            ds_s[...] = jnp.zeros_like(ds_s)

        f = functools.partial(gla_chunk, mask=_gla_mask(d))
        _, vjp = jax.vjp(f, q_ref[...], k_ref[...], v_ref[...], la_ref[...], s_ref[0, 0])
        dq_ref[...], dk_ref[...], dv_ref[...], dla_ref[...], ds_s[...] = vjp((do_ref[...], ds_s[...]))

    def chunk_of(d, n):
        return _scan_chunk(d, nc - 1 - n, nc, nlc)

    k_out = pl.BlockSpec((c, kw), lambda d, n: (chunk_of(d, n), d))
    return pl.pallas_call(
        body, name="gla_bwd", grid=(2, nc),
        in_specs=[pl.BlockSpec((c, kw), lambda d, n: (chunk_of(d, n), _GQ_BLK)),
                  pl.BlockSpec((c, kw), lambda d, n: (chunk_of(d, n), _GK_BLK)),
                  pl.BlockSpec((c, vw), lambda d, n: (chunk_of(d, n), _GV_BLK)),
                  pl.BlockSpec((c, kw), lambda d, n: (chunk_of(d, n), _LA_BLK + d)),
                  pl.BlockSpec((1, 1, GLA_DV, kw), lambda d, n: (d, nc - 1 - n, 0, 0)),
                  pl.BlockSpec((c, vw), lambda d, n: (chunk_of(d, n), 0))],
        out_specs=[k_out, k_out, pl.BlockSpec((c, vw), lambda d, n: (chunk_of(d, n), d)), k_out],
        out_shape=[jax.ShapeDtypeStruct((t, 2 * kw), F32), jax.ShapeDtypeStruct((t, 2 * kw), F32),
                   jax.ShapeDtypeStruct((t, 2 * vw), F32), jax.ShapeDtypeStruct((t, 2 * kw), F32)],
        scratch_shapes=[pltpu.VMEM((GLA_DV, kw), F32)],
        compiler_params=_params(("parallel", "arbitrary")),
    )(p, z, z, p, states, do)


def _adam_tile(w, g, m, v):
    m = ADAM_B1 * m + (1.0 - ADAM_B1) * g
    v = ADAM_B2 * v + (1.0 - ADAM_B2) * (g * g)
    m_hat = m / (1.0 - ADAM_B1 ** ADAM_STEP)
    v_hat = v / (1.0 - ADAM_B2 ** ADAM_STEP)
    delta = -ADAM_LR * (m_hat / (jnp.sqrt(v_hat) + ADAM_EPS) + ADAM_WD * w)
    return delta, m, v


def adamw(name, w, g, m, v):
    shape = w.shape
    cols = shape[-1] if w.ndim > 1 and shape[-1] >= LANES else int(np.prod(shape))
    rows = int(np.prod(shape)) // cols
    tr = rows
    for cand in (512, 256, 128, 64, 32, 16, 8):
        if rows % cand == 0 and cand * cols * 4 <= (1 << 20):
            tr = cand
            break
    flat = [a.reshape(rows, cols) for a in (w, g, m, v)]

    def body(w_ref, g_ref, m_ref, v_ref, d_ref, mo_ref, vo_ref):
        d_ref[...], mo_ref[...], vo_ref[...] = _adam_tile(w_ref[...], g_ref[...], m_ref[...], v_ref[...])

    spec = pl.BlockSpec((tr, cols), lambda i: (i, 0))
    outs = pl.pallas_call(
        body, name=name, grid=(rows // tr,),
        in_specs=[spec] * 4, out_specs=[spec] * 3,
        out_shape=[jax.ShapeDtypeStruct((rows, cols), F32)] * 3,
        compiler_params=_params(("parallel",)),
    )(*flat)
    return tuple(o.reshape(shape) for o in outs)


def adamw_layers(name, w, grads, m, v):
    depth, rows, cols = w.shape
    tr = _rows_tile(rows, cols)
    nb = rows // tr

    def body(w_ref, m_ref, v_ref, *rest):
        g_refs, (g_ref, d_ref, mo_ref, vo_ref) = rest[:depth], rest[depth:]
        l = pl.program_id(0)
        for k in range(depth):
            @pl.when(l == k)
            def _():
                g = g_refs[k][...]
                g_ref[...] = g
                d_ref[...], mo_ref[...], vo_ref[...] = _adam_tile(w_ref[...], g, m_ref[...], v_ref[...])

    def layer_grad(k):
        return pl.BlockSpec((tr, cols), lambda l, i: (jnp.where(l < k, 0, jnp.where(l == k, i, nb - 1)), 0))

    spec = pl.BlockSpec((tr, cols), lambda l, i: (l * nb + i, 0))
    flat = [a.reshape(depth * rows, cols) for a in (w, m, v)]
    g_all, delta, new_m, new_v = pl.pallas_call(
        body, name=name, grid=(depth, nb),
        in_specs=[spec] * 3 + [layer_grad(k) for k in range(depth)], out_specs=[spec] * 4,
        out_shape=[jax.ShapeDtypeStruct((depth * rows, cols), F32)] * 4,
        compiler_params=_params(("arbitrary", "arbitrary")),
    )(*flat, *grads)
    return tuple(a.reshape(w.shape) for a in (delta, new_m, new_v)), g_all.reshape(w.shape)


MESH = pl.DeviceIdType.MESH
_HBM = pl.BlockSpec(memory_space=pltpu.HBM)
N_CHIPS = 4
N_DEV = 8


def _place():
    x, y, c = lax.axis_index("x"), lax.axis_index("y"), lax.axis_index("c")
    chips = [(1 - x, y), (x, 1 - y), (1 - x, 1 - y)]
    return x, y, c, chips


def _remote(src, dst, send_sem, recv_sem, to):
    return pltpu.make_async_remote_copy(src_ref=src, dst_ref=dst, send_sem=send_sem, recv_sem=recv_sem,
                                        device_id=to, device_id_type=MESH)


def all_gather_small(name, v):
    m_per, n = v.shape

    def body(x_ref, out_ref, send_sems, recv_sems, local_sem):
        x, y, c, chips = _place()
        me, sibling = (x, y, c), (x, y, 1 - c)

        def rows(px, py, pc):
            return out_ref.at[pl.ds((4 * px + 2 * py + pc) * m_per, m_per), :]

        def copy(k, block, to, src=None):
            return _remote(rows(*block) if src is None else src, rows(*block), send_sems.at[k], recv_sems.at[k], to)

        mine = pltpu.make_async_copy(x_ref, rows(*me), local_sem)
        mine.start()
        first = [copy(0, me, sibling, src=x_ref)]
        first += [copy(1 + j, me, (*chip, c), src=x_ref) for j, chip in enumerate(chips)]
        for cp in first:
            cp.start()
        passed = [copy(4 + j, (*chip, c), sibling) for j, chip in enumerate(chips)]
        for j, chip in enumerate(chips):
            copy(1 + j, (*chip, c), me).wait_recv()
            passed[j].start()
        copy(0, sibling, me).wait_recv()
        for j, chip in enumerate(chips):
            copy(4 + j, (*chip, 1 - c), me).wait_recv()
        for cp in first + passed:
            cp.wait_send()
        mine.wait()

    return pl.pallas_call(
        body, name=name,
        out_shape=jax.ShapeDtypeStruct((N_DEV * m_per, n), v.dtype),
        in_specs=[pl.BlockSpec(memory_space=pltpu.VMEM)],
        out_specs=pl.BlockSpec(memory_space=pltpu.VMEM),
        scratch_shapes=[pltpu.SemaphoreType.DMA((7,)), pltpu.SemaphoreType.DMA((7,)), pltpu.SemaphoreType.DMA],
        compiler_params=pltpu.CompilerParams(vmem_limit_bytes=VMEM_LIMIT),
    )(v)


_SEM = pl.BlockSpec(memory_space=pltpu.SEMAPHORE)
_SPLIT_COPY = pltpu.CompilerParams(has_side_effects=pltpu.SideEffectType.DATAFLOW_SIDE_EFFECTING)


class CopyPlan(NamedTuple):
    copies: object
    n: int
    in_place: bool = False


def _gather_copies(x_ref, land_ref, x, y, c, chips):
    half = x_ref.shape[0] // 2
    rows = pl.ds(c * half, half)
    return [(x_ref.at[rows, :], land_ref.at[2 * x + y, rows, :], (*chip, c), land_ref.at[2 * chip[0] + chip[1], rows, :])
            for chip in chips]


def _pass_copies(land_ref, _, x, y, c, chips):
    half = land_ref.shape[1] // 2
    mine, other = pl.ds(c * half, half), pl.ds((1 - c) * half, half)
    return [(land_ref.at[2 * chip[0] + chip[1], mine, :], land_ref.at[2 * chip[0] + chip[1], mine, :], (x, y, 1 - c),
             land_ref.at[2 * chip[0] + chip[1], other, :]) for chip in chips]


def _sibling_half_copies(p_ref, land_ref, x, y, c, chips):
    half = p_ref.shape[1] // 2
    return [(p_ref.at[:, pl.ds((1 - c) * half, half), :], land_ref, (x, y, 1 - c), land_ref)]


def _scatter_copies(s_ref, land_ref, x, y, c, chips):
    return [(s_ref.at[2 * chip[0] + chip[1]], land_ref.at[j], (*chip, c), land_ref.at[j]) for j, chip in enumerate(chips)]


def _join_copies(buf_ref, _, x, y, c, chips):
    half = buf_ref.shape[0] // 2
    mine = buf_ref.at[pl.ds(c * half, half), :]
    return [(mine, mine, (x, y, 1 - c), buf_ref.at[pl.ds((1 - c) * half, half), :])]


GATHER = CopyPlan(_gather_copies, 3)
PASS_ON = CopyPlan(_pass_copies, 3, in_place=True)
SIBLING_HALF = CopyPlan(_sibling_half_copies, 1)
SCATTER = CopyPlan(_scatter_copies, 3)
JOIN = CopyPlan(_join_copies, 1, in_place=True)


def split_start(name, plan, srcs, land_shapes=None, after=None):
    nt = len(srcs)
    arrays = [pltpu.with_memory_space_constraint(s, pltpu.HBM) for s in srcs]
    if not plan.in_place:
        arrays += [pltpu.with_memory_space_constraint(lax.empty(shape, s.dtype), pltpu.HBM) for shape, s in zip(land_shapes, srcs)]
    na = len(arrays)
    behind = [] if after is None else [after]
    n_in = na + len(behind)

    def body(*refs):
        x_refs = refs[:nt]
        land_refs = x_refs if plan.in_place else refs[nt:na]
        send, recv = refs[n_in:n_in + nt], refs[n_in + nt:n_in + 2 * nt]
        x, y, c, chips = _place()
        for t in range(nt):
            for j, (src, dst, to, _) in enumerate(plan.copies(x_refs[t], land_refs[t], x, y, c, chips)):
                _remote(src, dst, send[t].at[j], recv[t].at[j], to).start()
        refs[-1][...] = jnp.zeros_like(refs[-1])

    outs = pl.pallas_call(
        body, name=name,
        out_shape=tuple([pltpu.SemaphoreType.DMA((plan.n,))] * (2 * nt) + [pltpu.HBM(a.shape, a.dtype) for a in arrays]
                        + [jax.ShapeDtypeStruct((8, LANES), F32)]),
        in_specs=[_HBM] * na + [pl.BlockSpec(memory_space=pl.ANY)] * len(behind),
        out_specs=tuple([_SEM] * (2 * nt) + [_HBM] * na + [pl.BlockSpec(memory_space=pltpu.VMEM)]),
        input_output_aliases={i: 2 * nt + i for i in range(na)},
        compiler_params=_SPLIT_COPY,
    )(*arrays, *behind)
    groups = [(outs[t], outs[nt + t]) + tuple(outs[2 * nt + t + k * nt] for k in range(na // nt)) for t in range(nt)]
    return groups, outs[-1]


def split_wait(name, plan, group, after):
    send, recv, *arrays = group
    na = len(arrays)

    def body(*refs):
        x_ref, land_ref = refs[0], refs[na - 1]
        send_sem, recv_sem = refs[na], refs[na + 1]
        x, y, c, chips = _place()
        for j, (s, _, to, arrival) in enumerate(plan.copies(x_ref, land_ref, x, y, c, chips)):
            cp = _remote(s, arrival, send_sem.at[j], recv_sem.at[j], to)
            cp.wait_send()
            cp.wait_recv()

    return pl.pallas_call(
        body, name=name,
        out_shape=tuple(pltpu.HBM(a.shape, a.dtype) for a in arrays),
        in_specs=tuple([_HBM] * na + [_SEM, _SEM, pl.BlockSpec(memory_space=pl.ANY)]), out_specs=tuple([_HBM] * na),
        input_output_aliases={i: i for i in range(na)}, compiler_params=_SPLIT_COPY,
    )(*arrays, send, recv, after)


def _rows_tile(rows, cols):
    for cand in (512, 256, 128, 64, 32, 16):
        if rows % cand == 0 and cand * cols * 4 <= (1 << 21):
            return cand
    return rows


def add_sibling_half(name, pieces, from_sibling, core):
    n, h, cols = from_sibling.shape
    tr = _rows_tile(h, cols // 2)
    nb = h // tr

    def body(c_ref, a_ref, b_ref, o_ref):
        o_ref[...] = (a_ref[...].astype(F32) + b_ref[...].astype(F32)).astype(o_ref.dtype)

    blk = pl.BlockSpec((1, tr, cols), lambda q, i, c_ref: (q, i, 0))
    return pl.pallas_call(
        body, name=name,
        grid_spec=pltpu.PrefetchScalarGridSpec(
            num_scalar_prefetch=1, grid=(n, nb),
            in_specs=[pl.BlockSpec((1, tr, cols), lambda q, i, c_ref: (q, c_ref[0] * nb + i, 0)), blk], out_specs=blk),
        out_shape=jax.ShapeDtypeStruct((n, h, cols), BF16),
        compiler_params=_params(("parallel", "parallel")),
    )(core.reshape(1).astype(jnp.int32), pieces, from_sibling)


def add_chip_sums(name, chip_sums, from_chips, chip, core):
    _, h, cols = chip_sums.shape
    tr = _rows_tile(h, cols)
    nb = h // tr

    def body(s_ref, own_ref, r0_ref, r1_ref, r2_ref, o_ref):
        acc = own_ref[0].astype(F32) + r0_ref[0].astype(F32)
        o_ref[...] = acc + r1_ref[0].astype(F32) + r2_ref[0].astype(F32)

    def got(j):
        return pl.BlockSpec((1, tr, cols), lambda i, s_ref: (j, i, 0))

    return pl.pallas_call(
        body, name=name,
        grid_spec=pltpu.PrefetchScalarGridSpec(
            num_scalar_prefetch=1, grid=(nb,),
            in_specs=[pl.BlockSpec((1, tr, cols), lambda i, s_ref: (s_ref[0], i, 0)), got(0), got(1), got(2)],
            out_specs=pl.BlockSpec((tr, cols), lambda i, s_ref: (s_ref[1] * nb + i, 0))),
        out_shape=jax.ShapeDtypeStruct((2 * h, cols), F32),
        compiler_params=_params(("parallel",)),
    )(jnp.stack([chip, core]).astype(jnp.int32), chip_sums, from_chips, from_chips, from_chips)


def sum_device_blocks(name, g):
    n = g.shape[1]

    def body(g_ref, o_ref):
        acc = g_ref[0:8, :]
        for d in range(1, N_DEV):
            acc = acc + g_ref[8 * d:8 * (d + 1), :]
        o_ref[...] = acc

    return pl.pallas_call(body, name=name, out_shape=jax.ShapeDtypeStruct((8, n), F32),
                          compiler_params=pltpu.CompilerParams(vmem_limit_bytes=VMEM_LIMIT))(g)


class LayerWeights(NamedTuple):
    norm1_g: jax.Array
    q_g: jax.Array
    k_g: jax.Array
    lg: jax.Array
    ret_g: jax.Array
    gate_up: jax.Array
    gate_b: jax.Array
    gla_g: jax.Array
    norm2_g: jax.Array
    conv_w: jax.Array
    conv_b: jax.Array


def _mod(mods, k):
    return mods[:, k:k + 1, :]


def out_view(l, tb):
    rows = D_MODEL // N_CHIPS
    if tb:
        return BView(n=D_MODEL, k=D_MODEL, tn=rows, tk=D_MODEL, index_map=lambda i, j, kk: (j, l, kk))
    chips = tuple(functools.partial(lambda i, j, kk, q: (q, l, j), q=q) for q in range(N_CHIPS))
    return BView(n=D_MODEL, k=D_MODEL, tn=1024, tk=rows, index_map=None, part_maps=chips)


def down_view(l, f, tb):
    rows = f // N_CHIPS
    if tb:
        return BView(n=f, k=D_MODEL, tn=rows, tk=D_MODEL, index_map=lambda i, j, kk: (j, l, kk))
    chips = tuple(functools.partial(lambda i, j, kk, q: (q, l, j), q=q) for q in range(N_CHIPS))
    return BView(n=D_MODEL, k=f, tn=512, tk=rows, index_map=None, part_maps=chips)


def up_view(l, f, part=None):
    cols = 2 * f // N_CHIPS
    tc = _pick(cols, (1408, 1024, 512, 256))
    nbc = cols // tc
    if part is None:
        return BView(n=2 * f, k=D_MODEL, tn=tc, tk=D_MODEL, index_map=lambda i, j, kk: (j // nbc, l, j % nbc))
    nnb = D_MODEL // 512
    tiles = tuple(functools.partial(lambda i, j, kk, p: (2 * part + p // nbc, l * nnb + j, p % nbc), p=p) for p in range(2 * nbc))
    return BView(n=D_MODEL, k=f, tn=512, tk=tc, index_map=None, part_maps=tiles)


def up_grad_view(f, part, into):
    cols = f // 2
    tn = _pick(cols, (1408, 1024, 512, 256))
    nbc = cols // tn
    return OView((N_CHIPS, D_MODEL, cols), lambda i, j, kk: (2 * part + j // nbc, i, j % nbc), tn, into)


def ada_view(l, n_ada, tb):
    if tb:
        return BView(n=D_MODEL, k=n_ada, tn=1024, tk=n_ada, index_map=lambda i, j, kk: (l, j, 0))
    return BView(n=n_ada, k=D_MODEL, tn=1024, tk=D_MODEL, index_map=lambda i, j, kk: (l, 0, j))


def _prep_args(z, zg, cos, sin, w):
    rows = [Row(z, Z_AV, 0), Row(z, 512, Z_RQ // 512), Row(z, 512, Z_RK // 512), Row(z, 256, Z_GQ // 256),
            Row(zg, LANES, 0), Row(cos, HEAD_DIM, 0, False), Row(sin, HEAD_DIM, 0, False)]
    return rows, [Par(w.q_g), Par(w.k_g), Par(w.gate_up), Par(w.gate_b)]


def _post_args(o_att, o_ret, o_gla, z, w):
    rows = [Row(o_att, 1024), Row(o_ret, 512, 0), Row(o_ret, 512, 1, False), Row(o_gla, 512, 0), Row(o_gla, 512, 1, False),
            Row(z, 512, Z_RG // 512), Row(z, 512, Z_GR // 512)]
    return rows, [Par(w.ret_g), Par(w.gla_g)]


def layer_fwd(l, xs, mods, w, fetch, cos, sin, n_lat, n_out):
    t, d = xs.shape
    tag = f"l{l}_"
    nm1 = [Par(w.norm1_g), Par(_mod(mods, 0), True), Par(_mod(mods, 1), True)]
    (h,) = row_map(tag + "norm1", normmod_tile, [Row(xs, d)], nm1, [(d, BF16)], t, n_lat)
    (w_main, w_gate), started = fetch("w_in", h)
    z = matmul(tag + "in_proj", h, w_main, after=started)
    zg = matmul(tag + "gate_proj", h, w_gate)
    rows, pars = _prep_args(z, zg, cos, sin, w)
    (p,) = row_map(tag + "prep", prep_tile, rows, pars, [(P_W, F32)], t, n_lat)
    o_att, lse = attn_fwd(p, z, n_lat)
    o_ret, s_ret = ret_fwd(p, z, w.lg, n_lat)
    o_gla, s_gla = gla_fwd(p, z, n_lat)
    rows, pars = _post_args(o_att, o_ret, o_gla, z, w)
    (m,) = row_map(tag + "post", post_tile, rows, pars, [(d, BF16)], t, n_lat)
    m = m[:n_out]
    g_out, started = fetch("w_out", m)
    y = matmul(tag + "out_proj", m, g_out, view=out_view(0, False), after=started)
    (x1,) = row_map(tag + "resid1", resid_tile, [Row(xs, d), Row(y, d)], [Par(_mod(mods, 2), True)], [(d, F32)], n_out, n_lat)
    nm2 = [Par(w.norm2_g), Par(_mod(mods, 3), True), Par(_mod(mods, 4), True)]
    (h2,) = row_map(tag + "norm2", normmod_tile, [Row(x1, d)], nm2, [(d, BF16)], n_out, n_lat)
    f = w.conv_b.shape[1]
    g_up, started = fetch("w_up", h2)
    u = matmul(tag + "up_proj", h2, g_up, view=up_view(0, f), after=started, out_dtype=BF16)
    g = convglu(tag + "convglu", u, w.conv_w, w.conv_b, n_lat)
    g_down, started = fetch("w_down", g)
    yd = matmul(tag + "down_proj", g, g_down, view=down_view(0, f, False), after=started)
    (x2,) = row_map(tag + "resid2", resid_tile, [Row(x1, d), Row(yd, d)], [Par(_mod(mods, 5), True)], [(d, F32)], n_out, n_lat)
    saved = dict(xs=xs, h=h, z=z, zg=zg, p=p, o_att=o_att, lse=lse, o_ret=o_ret, s_ret=s_ret, o_gla=o_gla, s_gla=s_gla,
                 m=m, y=y, x1=x1, h2=h2, u=u, g=g, yd=yd, w_main=w_main, w_gate=w_gate, g_out=g_out, g_up=g_up, g_down=g_down)
    return x2, saved


def _sum_dirs(a):
    w = a.shape[1] // 2
    return a[:, :w] + a[:, w:]


def layer_bwd(l, dx2, s, mods, w, cos, sin, n_lat, grad_ready):
    (t, d), n_out = s["xs"].shape, dx2.shape[0]
    tag = f"l{l}_b_"

    def all_rows(a):
        return a if n_out == t else jnp.pad(a, ((0, t - n_out), (0, 0)))

    dyd, dgate5 = row_vjp(tag + "resid2", gated_tile, [Row(s["yd"], d)], [Par(_mod(mods, 5), True)], [dx2], n_out, n_lat,
                          row_grad_dtype=BF16)
    f = w.conv_b.shape[1]
    dg = matmul(tag + "down_dx", dyd, s["g_down"], tb=True, view=down_view(0, f, True))
    dw_down = matmul(tag + "down_dw", s["g"], dyd, ta=True, out_dtype=BF16)
    da, dv, dcw, dcb = convglu_bwd(tag + "convglu", s["u"], w.conv_w, w.conv_b, dg, n_lat)
    dh2 = matmul(tag + "up_dx_gate", da, s["g_up"], tb=True, view=up_view(0, f, 0))
    dh2 = matmul(tag + "up_dx_value", dv, s["g_up"], tb=True, view=up_view(0, f, 1), add=dh2)
    dw_up = matmul(tag + "up_dw_gate", s["h2"], da, ta=True, out_dtype=BF16, o_view=up_grad_view(f, 0, None))
    dw_up = matmul(tag + "up_dw_value", s["h2"], dv, ta=True, out_dtype=BF16, o_view=up_grad_view(f, 1, dw_up))
    started = grad_ready("ffn", dict(w_up=dw_up, w_down=dw_down))
    nm2 = [Par(w.norm2_g), Par(_mod(mods, 3), True), Par(_mod(mods, 4), True)]
    dx1, dg2, dshift3, dscale4 = row_vjp(tag + "norm2", normmod_tile, [Row(s["x1"], d)], nm2, [dh2], n_out, n_lat,
                                         add_to_first=dx2, after=started)
    dy, dgate2 = row_vjp(tag + "resid1", gated_tile, [Row(s["y"], d)], [Par(_mod(mods, 2), True)], [dx1], n_out, n_lat,
                         row_grad_dtype=BF16)
    if n_out < t:
        dgate5, dshift3, dscale4, dgate2 = [g.at[1].set(0.0) for g in (dgate5, dshift3, dscale4, dgate2)]
    dm = matmul(tag + "out_dx", dy, s["g_out"], tb=True, view=out_view(0, True))
    dw_out = matmul(tag + "out_dw", s["m"], dy, ta=True, out_dtype=BF16)
    rows, pars = _post_args(s["o_att"], s["o_ret"], s["o_gla"], s["z"], w)
    started = grad_ready("w_out", dict(w_out=dw_out))
    do_att, do_ret, do_gla, d_rg, d_gr, d_ret_g, d_gla_g = row_vjp(tag + "post", post_tile, rows, pars, [dm], n_out, n_lat, after=started)
    do_att, do_ret, do_gla, d_rg, d_gr, dx1 = [all_rows(a) for a in (do_att, do_ret, do_gla, d_rg, d_gr, dx1)]
    dq_a, dk_a, dv_a = attn_bwd(s["p"], s["z"], s["o_att"], s["lse"], do_att, n_lat)
    dq_r, dk_r, dv_r, dlg = ret_bwd(s["p"], s["z"], w.lg, s["s_ret"], do_ret, n_lat)
    dq_g, dk_g, dv_g, dla = gla_bwd(s["p"], s["z"], s["s_gla"], do_gla, n_lat)
    dp = jnp.concatenate([dq_a, dk_a, _sum_dirs(dq_g), _sum_dirs(dq_r), _sum_dirs(dk_r), dla], axis=1)
    rows, pars = _prep_args(s["z"], s["zg"], cos, sin, w)
    d_zqk, d_zrq, d_zrk, d_zgq, dzg, d_qg, d_kg, d_up, d_gb = row_vjp(tag + "prep", prep_tile, rows, pars, [dp], t, n_lat)
    dz = jnp.concatenate([d_zqk, dv_a, d_zrq, d_zrk, _sum_dirs(dv_r), d_rg, d_zgq, _sum_dirs(dk_g), _sum_dirs(dv_g), d_gr], axis=1)
    dz, dzg = dz.astype(BF16), dzg.astype(BF16)
    dh_gate = matmul(tag + "gate_dx", dzg, s["w_gate"], tb=True)
    dh = matmul(tag + "in_dx", dz, s["w_main"], tb=True, add=dh_gate)
    dw_main = matmul(tag + "in_dw", s["h"], dz, ta=True, out_dtype=BF16)
    dw_gate = matmul(tag + "gate_dw", s["h"], dzg, ta=True, out_dtype=BF16)
    started = grad_ready("w_in", dict(w_main=dw_main, w_gate=dw_gate))
    nm1 = [Par(w.norm1_g), Par(_mod(mods, 0), True), Par(_mod(mods, 1), True)]
    dx, dg1, dshift0, dscale1 = row_vjp(tag + "norm1", normmod_tile, [Row(s["xs"], d)], nm1, [dh], t, n_lat,
                                        add_to_first=dx1, after=started)
    dmods = jnp.concatenate([dshift0, dscale1, dgate2, dshift3, dscale4, dgate5], axis=1)
    grads = dict(w_main=dw_main, w_gate=dw_gate, w_out=dw_out, w_up=dw_up, w_down=dw_down, norm1_g=dg1, q_g=d_qg, k_g=d_kg,
                 lg=dlg, ret_g=d_ret_g, gate_up=d_up, gate_b=d_gb, gla_g=d_gla_g, norm2_g=dg2, conv_w=dcw, conv_b=dcb)
    return dx, dmods, grads


def rope_tables(n_lat, n_ctx):
    rows = n_lat // GRID_W
    row = jnp.repeat(jnp.arange(rows, dtype=F32), GRID_W)
    col = jnp.tile(jnp.arange(GRID_W, dtype=F32), rows)
    n_freq = HEAD_DIM // 4
    inv_freq = ROPE_THETA ** (-jnp.arange(n_freq, dtype=F32) / n_freq)
    ang = jnp.concatenate([row[:, None] * inv_freq, col[:, None] * inv_freq], axis=-1)
    cos, sin = jnp.cos(ang), jnp.sin(ang)
    cos = jnp.concatenate([jnp.concatenate([cos, cos], axis=1), jnp.ones((n_ctx, HEAD_DIM), F32)], axis=0)
    sin = jnp.concatenate([jnp.concatenate([-sin, sin], axis=1), jnp.zeros((n_ctx, HEAD_DIM), F32)], axis=0)
    return cos, sin


def local_step(xs, target, mods, weights, fetch, final_g, n_lat, grad_ready):
    t, d = xs.shape
    cos, sin = rope_tables(n_lat, t - n_lat)
    saved = []
    h = xs
    for l, w in enumerate(weights):
        n_out = t if l + 1 < len(weights) else n_lat
        h, s = layer_fwd(l, h, mods[l], w, functools.partial(fetch, l), cos, sin, n_lat, n_out)
        saved.append(s)
    loss, dx, dgf = final_loss(h, target, final_g, n_lat)
    dmods, grads = [None] * len(weights), [None] * len(weights)
    for l in reversed(range(len(weights))):
        dx, dmods[l], grads[l] = layer_bwd(l, dx, saved[l], mods[l], weights[l], cos, sin, n_lat, functools.partial(grad_ready, l))
    return loss, dx, dmods, grads, dgf


WEIGHT_NAMES = ("c_ctx", "ada_w", "ada_b", "norm1_g", "w_in", "q_norm_g", "k_norm_g", "ret_log_decay", "ret_norm_g",
                "gla_gate_up", "gla_gate_b", "gla_norm_g", "w_out", "norm2_g", "w_up", "conv_w", "conv_b", "w_down", "final_norm_g")
PACK_QUANTUM = 8 * LANES


def _pack(arrays):
    flat = jnp.concatenate([a.reshape(-1).astype(F32) for a in arrays])
    n = -(-flat.shape[0] // PACK_QUANTUM) * PACK_QUANTUM
    return jnp.pad(flat, (0, n - flat.shape[0])).reshape(8, n // 8)


def _unpack(flat2d, shapes):
    out, at = [], 0
    for s in shapes:
        size = int(np.prod(s))
        out.append(flat2d[:, at:at + size].reshape((flat2d.shape[0],) + tuple(s)))
        at += size
    return out


def _per_device(gathered):
    return gathered.reshape(N_DEV, -1)


def _from_chips(per_device, axis):
    chips = per_device[0::2]
    moved = jnp.moveaxis(chips, 0, axis)
    shape = moved.shape
    return moved.reshape(shape[:axis] + (shape[axis] * shape[axis + 1],) + shape[axis + 2:])


def kernel(x, c, ctx, c_ctx, ada_w, ada_b, norm1_g, w_in, q_norm_g, k_norm_g, ret_log_decay, ret_norm_g, gla_gate_up, gla_gate_b, gla_norm_g, w_out, norm2_g, w_up, conv_w, conv_b, w_down, final_norm_g, loss_target, m_c_ctx, m_ada_w, m_ada_b, m_norm1_g, m_w_in, m_q_norm_g, m_k_norm_g, m_ret_log_decay, m_ret_norm_g, m_gla_gate_up, m_gla_gate_b, m_gla_norm_g, m_w_out, m_norm2_g, m_w_up, m_conv_w, m_conv_b, m_w_down, m_final_norm_g, v_c_ctx, v_ada_w, v_ada_b, v_norm1_g, v_w_in, v_q_norm_g, v_k_norm_g, v_ret_log_decay, v_ret_norm_g, v_gla_gate_up, v_gla_gate_b, v_gla_norm_g, v_w_out, v_norm2_g, v_w_up, v_conv_w, v_conv_b, v_w_down, v_final_norm_g):
    weights = dict(zip(WEIGHT_NAMES, (c_ctx, ada_w, ada_b, norm1_g, w_in, q_norm_g, k_norm_g, ret_log_decay, ret_norm_g,
                                      gla_gate_up, gla_gate_b, gla_norm_g, w_out, norm2_g, w_up, conv_w, conv_b, w_down, final_norm_g)))
    mom_m = dict(zip(WEIGHT_NAMES, (m_c_ctx, m_ada_w, m_ada_b, m_norm1_g, m_w_in, m_q_norm_g, m_k_norm_g, m_ret_log_decay, m_ret_norm_g,
                                    m_gla_gate_up, m_gla_gate_b, m_gla_norm_g, m_w_out, m_norm2_g, m_w_up, m_conv_w, m_conv_b, m_w_down, m_final_norm_g)))
    mom_v = dict(zip(WEIGHT_NAMES, (v_c_ctx, v_ada_w, v_ada_b, v_norm1_g, v_w_in, v_q_norm_g, v_k_norm_g, v_ret_log_decay, v_ret_norm_g,
                                    v_gla_gate_up, v_gla_gate_b, v_gla_norm_g, v_w_out, v_norm2_g, v_w_up, v_conv_w, v_conv_b, v_w_down, v_final_norm_g)))
    depth, d = norm1_g.shape
    assert d == D_MODEL and x.shape[0] == 1
    n_lat, n_ctx, f = x.shape[1], ctx.shape[1], conv_b.shape[1]
    assert n_lat % ROW_TILE == 0 and n_ctx % ROW_TILE == 0 and f % FFN_COL_TILE == 0 and f % N_CHIPS == 0
    n_in = w_in.shape[2]
    n_ada = ada_w.shape[2]
    xi, yi, ci = lax.axis_index("x"), lax.axis_index("y"), lax.axis_index("c")
    chip = 2 * xi + yi
    dev = 2 * chip + ci

    big = ("w_in", "w_out", "w_up", "w_down")
    order = [(l, name) for l in range(depth) for name in big]
    shards = [weights[name][l].astype(BF16) for l, name in order]
    passing = {}

    def pass_on(k, after):
        tag = "{1}{0}".format(*order[k])
        own, land = split_wait("gather_wait_" + tag, GATHER, in_flight[k], after)
        (moving,), started = split_start("gather_pass_" + tag, PASS_ON, [land])
        passing[k] = (own, moving)
        return started

    def fetch(l, name, after):
        k = order.index((l, name))
        if k == 0:
            pass_on(0, after)
        own, moving = passing.pop(k)
        (land,) = split_wait(f"gather_pass_wait_{name}{l}", PASS_ON, moving, after)
        started = pass_on(k + 1, after) if k + 1 < len(order) else None
        land = lax.dynamic_update_slice_in_dim(land, own[None], chip, axis=0)
        if name != "w_in":
            return land, started
        cols = jnp.concatenate([land[q] for q in range(N_CHIPS)], axis=1)
        return (cols[:, :N_MAIN], jnp.pad(cols[:, N_MAIN:], ((0, 0), (0, LANES - N_GATE)))), started

    small_shapes = [c.shape[1:], conv_w.shape, gla_gate_up.shape, gla_gate_b.shape]
    got = _per_device(all_gather_small("gather_small", _pack([c, conv_w, gla_gate_up, gla_gate_b])))
    c_all, conv_w_sh, gate_up_sh, gate_b_sh = _unpack(got, small_shapes)
    conv_w_full = _from_chips(conv_w_sh, 2)
    gate_up_full = _from_chips(gate_up_sh, 3)
    gate_b_full = _from_chips(gate_b_sh, 2)

    act = jnp.zeros((16, d), F32).at[0:N_DEV].set(jax.nn.silu(c_all)).at[N_DEV].set(jax.nn.silu(c_ctx))
    mod_sh = jnp.stack([matmul(f"ada_fwd{l}", act, ada_w, view=ada_view(l, n_ada, False)) for l in range(depth)])
    got = _per_device(all_gather_small("gather_mods", _pack([mod_sh])))
    (mod_sh_all,) = _unpack(got, [mod_sh.shape])
    mod_full = _from_chips(mod_sh_all, 2) + ada_b[:, None, :]
    mod_mine = lax.dynamic_index_in_dim(mod_full, dev, axis=1, keepdims=False)
    mods = [jnp.stack([mod_mine[l].reshape(N_MOD, d), mod_full[l, N_DEV].reshape(N_MOD, d)]) for l in range(depth)]
    in_flight, token = split_start("gather_start", GATHER, shards, [(N_CHIPS,) + s.shape for s in shards], after=mod_full)

    layer_w = []
    for l in range(depth):
        up = jnp.zeros((2, LANES, GLA_HEADS * GLA_DK), F32)
        up = up.at[0, 0:GLA_RANK].set(gate_up_full[l, 0]).at[1, GLA_RANK:2 * GLA_RANK].set(gate_up_full[l, 1])
        layer_w.append(LayerWeights(
            norm1_g=norm1_g[l].reshape(1, 1, d), q_g=q_norm_g[l].reshape(1, 1, HEAD_DIM), k_g=k_norm_g[l].reshape(1, 1, HEAD_DIM),
            lg=ret_log_decay[l].reshape(2, RET_HEADS, 1, 1), ret_g=ret_norm_g[l].reshape(1, 1, HEAD_DIM),
            gate_up=up.reshape(1, 2 * LANES, -1), gate_b=gate_b_full[l].reshape(1, 2, -1), gla_g=gla_norm_g[l].reshape(1, 1, HEAD_DIM),
            norm2_g=norm2_g[l].reshape(1, 1, d), conv_w=conv_w_full[l], conv_b=conv_b[l].reshape(1, f)))

    def pieces_of(name, g):
        if name == "w_in":
            full_cols = jnp.concatenate([g["w_main"], g["w_gate"][:, :N_GATE]], axis=1)
            return jnp.stack([full_cols[:, q * n_in:(q + 1) * n_in] for q in range(N_CHIPS)])
        if name == "w_up":
            return g["w_up"]
        return g[name].reshape(N_CHIPS, -1, d)

    groups = {"ffn": ("w_up", "w_down"), "w_out": ("w_out",), "w_in": ("w_in",)}
    reducing = {}
    to_sibling = []

    def sibling_arrived(after):
        started = None
        while to_sibling:
            l, group, in_flight_halves = to_sibling.pop(0)
            sums = []
            for name, halves in zip(groups[group], in_flight_halves):
                pieces, from_sibling = split_wait(f"rs_sibling_wait_{name}{l}", SIBLING_HALF, halves, after)
                sums.append(add_sibling_half(f"rs_add_sibling_{name}{l}", pieces, from_sibling, ci))
            in_flight_sums, token = split_start(f"rs_start_{group}{l}", SCATTER, sums, [(3,) + s.shape[1:] for s in sums])
            reducing.update({(l, name): grp for name, grp in zip(groups[group], in_flight_sums)})
            started = token if started is None else started + token
        return started

    def grad_ready(l, group, g):
        pieces = [pieces_of(name, g) for name in groups[group]]
        before = None if (l, group) == (0, "w_in") else sibling_arrived(pieces[0])
        in_flight_halves, started = split_start(f"rs_sibling_{group}{l}", SIBLING_HALF, pieces,
                                                [(N_CHIPS, pc.shape[1] // 2, pc.shape[2]) for pc in pieces])
        to_sibling.append((l, group, in_flight_halves))
        return started if before is None else started + before

    xs = jnp.concatenate([x[0], ctx[0]], axis=0) + token[0, 0]
    loss, dx, dmods, grads, dgf = local_step(xs, loss_target[0], mods, layer_w, fetch, final_norm_g.reshape(1, d), n_lat, grad_ready)

    def gate_up_grad(g):
        return jnp.stack([g[0, 0:GLA_RANK], g[0, LANES + GLA_RANK:LANES + 2 * GLA_RANK]])

    per_layer = [[dmods[l][0], dmods[l][1], grads[l]["norm1_g"], grads[l]["norm2_g"], grads[l]["q_g"], grads[l]["k_g"],
                  grads[l]["ret_g"], grads[l]["gla_g"], grads[l]["lg"], gate_up_grad(grads[l]["gate_up"]), grads[l]["gate_b"],
                  grads[l]["conv_w"], grads[l]["conv_b"]] for l in range(depth)]
    layer_shapes = [(N_MOD * d,), (N_MOD * d,), (d,), (d,), (HEAD_DIM,), (HEAD_DIM,), (HEAD_DIM,), (HEAD_DIM,), (2, RET_HEADS),
                    (2, GLA_RANK, GLA_HEADS * GLA_DK), (2, GLA_HEADS * GLA_DK), (3, f), (f,)]
    packed = _pack([a for lay in per_layer for a in lay] + [dgf, loss[0, 0:1]])
    gathered = all_gather_small("gather_small_grads", packed)
    every = _unpack(_per_device(gathered), layer_shapes * depth + [(d,), (1,)])
    total = _unpack(sum_device_blocks("sum_small_grads", gathered).reshape(1, -1), layer_shapes * depth + [(d,), (1,)])
    nl = len(layer_shapes)

    def tot(l, k):
        return total[l * nl + k][0]

    out = {"norm1_g": jnp.stack([tot(l, 2) for l in range(depth)]), "norm2_g": jnp.stack([tot(l, 3) for l in range(depth)]),
           "q_norm_g": jnp.stack([tot(l, 4) for l in range(depth)]), "k_norm_g": jnp.stack([tot(l, 5) for l in range(depth)]),
           "ret_norm_g": jnp.stack([tot(l, 6) for l in range(depth)]), "gla_norm_g": jnp.stack([tot(l, 7) for l in range(depth)]),
           "ret_log_decay": jnp.stack([tot(l, 8) for l in range(depth)]),
           "gla_gate_up": lax.dynamic_slice_in_dim(jnp.stack([tot(l, 9) for l in range(depth)]), chip * gla_gate_up.shape[3], gla_gate_up.shape[3], axis=3),
           "gla_gate_b": lax.dynamic_slice_in_dim(jnp.stack([tot(l, 10) for l in range(depth)]), chip * gla_gate_b.shape[2], gla_gate_b.shape[2], axis=2),
           "conv_w": lax.dynamic_slice_in_dim(jnp.stack([tot(l, 11) for l in range(depth)]), chip * conv_w.shape[2], conv_w.shape[2], axis=2),
           "conv_b": jnp.stack([tot(l, 12) for l in range(depth)]),
           "final_norm_g": total[depth * nl][0],
           "ada_b": jnp.stack([tot(l, 0) + tot(l, 1) for l in range(depth)])}
    loss_total = total[depth * nl + 1][0, 0]

    dmod_all = jnp.zeros((depth, 16, N_MOD * d), F32)
    for l in range(depth):
        dmod_all = dmod_all.at[l, 0:N_DEV].set(every[l * nl][:, :]).at[l, N_DEV].set(tot(l, 1))
    dmod_cols = lax.dynamic_slice_in_dim(dmod_all, chip * n_ada, n_ada, axis=2)
    for l in range(depth):
        slab = OView((depth, d, n_ada), functools.partial(lambda i, j, kk, l: (l, i, j), l=l), None, out.get("ada_w"))
        out["ada_w"] = matmul(f"ada_dw{l}", act, dmod_cols[l], ta=True, o_view=slab)
    dact = matmul("ada_dx0", dmod_cols[0], ada_w, tb=True, view=ada_view(0, n_ada, True))
    for l in range(1, depth):
        dact = matmul(f"ada_dx{l}", dmod_cols[l], ada_w, tb=True, view=ada_view(l, n_ada, True), add=dact)
    got = _per_device(all_gather_small("gather_dcctx", _pack([dact[N_DEV]])))
    sibling_arrived(got)
    got = got[0::2, :d]
    dsilu = got[0] + got[1] + got[2] + got[3]
    sig = jax.nn.sigmoid(c_ctx)
    out["c_ctx"] = dsilu * (sig + c_ctx * sig * (1.0 - sig))

    deltas, new_m, new_v = {}, {}, {}

    def update(name):
        out[name] = out[name].reshape(weights[name].shape)
        deltas[name], new_m[name], new_v[name] = adamw("adamw_" + name, weights[name], out[name], mom_m[name], mom_v[name])

    for name in WEIGHT_NAMES:
        if name not in big:
            update(name)
    behind = new_v["ada_w"]
    joining = []

    def joined(after):
        name, in_flight_halves = joining.pop()
        per_layer = [split_wait(f"rs_join_wait_{name}{l}", JOIN, grp, after)[0] for l, grp in enumerate(in_flight_halves)]
        (deltas[name], new_m[name], new_v[name]), out[name] = adamw_layers(
            "adamw_" + name, weights[name], per_layer, mom_m[name], mom_v[name])
        return new_v[name]

    for name in ("w_down", "w_up", "w_out", "w_in"):
        halves = []
        for l in range(depth):
            sums, got = split_wait(f"rs_wait_{name}{l}", SCATTER, reducing[(l, name)], behind)
            halves.append(add_chip_sums(f"rs_add_chips_{name}{l}", sums, got, chip, ci))
        in_flight_halves, _ = split_start("rs_join_" + name, JOIN, halves)
        if joining:
            behind = joined(behind)
        joining.append((name, in_flight_halves))
    joined(behind)
    grad_x = dx[:n_lat].reshape(x.shape)
    return (loss_total, grad_x, *[out[n] for n in WEIGHT_NAMES], *[deltas[n] for n in WEIGHT_NAMES],
            *[new_m[n] for n in WEIGHT_NAMES], *[new_v[n] for n in WEIGHT_NAMES])
```

```python
import functools
from typing import NamedTuple

import numpy as np
import jax
import jax.numpy as jnp
from jax import lax
from jax.experimental import pallas as pl
from jax.experimental.pallas import tpu as pltpu

F32 = jnp.float32
BF16 = jnp.bfloat16

D_MODEL = 2048
HEAD_DIM = 128
ATT_Q_HEADS = 8
ATT_KV_HEADS = 2
ATT_GROUP = ATT_Q_HEADS // ATT_KV_HEADS
RET_HEADS = 4
GLA_HEADS = 4
GLA_DK = 64
GLA_DV = 128
GLA_RANK = 16
GLA_TAU = 16.0
RET_CHUNK = 256
GLA_CHUNK = 128
GRID_W = 64
ROPE_THETA = 10000.0
N_MOD = 6
EPS = 1e-6
N_MAIN = 5120
N_GATE = 2 * GLA_RANK
LANES = 128
ROW_TILE = 256
FFN_COL_TILE = 256
VMEM_LIMIT = 56 * 1024 * 1024

ADAM_LR = 0.001
ADAM_B1 = 0.9
ADAM_B2 = 0.999
ADAM_EPS = 1e-08
ADAM_WD = 0.01
ADAM_STEP = 10

Z_AQ, Z_AK, Z_AV = 0, 1024, 1280
Z_RQ, Z_RK, Z_RV, Z_RG = 1536, 2048, 2560, 3072
Z_GQ, Z_GK, Z_GV, Z_GR = 3584, 3840, 4096, 4608
P_AQ, P_AK, P_GQ, P_RQ, P_RK, P_LA = 0, 1024, 1280, 1536, 2048, 2560
P_W = 3072


def _params(sem=None):
    return pltpu.CompilerParams(dimension_semantics=sem, vmem_limit_bytes=VMEM_LIMIT)


def _pick(n, cands):
    for c in cands:
        if n % c == 0:
            return c
    return n


_NN = (((1,), (0,)), ((), ()))
_NT = (((1,), (1,)), ((), ()))
_TN = (((0,), (0,)), ((), ()))


def _dg(a, b, dims):
    return lax.dot_general(a.astype(BF16), b.astype(BF16), dims, preferred_element_type=F32)


@jax.custom_vjp
def bdot(a, b):
    return _dg(a, b, _NN)


def _bdot_fwd(a, b):
    return _dg(a, b, _NN), (a, b)


def _bdot_bwd(res, ct):
    a, b = res
    return _dg(ct, b, _NT), _dg(a, ct, _TN)


bdot.defvjp(_bdot_fwd, _bdot_bwd)


@jax.custom_vjp
def bdot_nt(a, b):
    return _dg(a, b, _NT)


def _bdot_nt_fwd(a, b):
    return _dg(a, b, _NT), (a, b)


def _bdot_nt_bwd(res, ct):
    a, b = res
    return _dg(ct, b, _NN), _dg(ct, a, _TN)


bdot_nt.defvjp(_bdot_nt_fwd, _bdot_nt_bwd)


@jax.custom_vjp
def bdot_tn(a, b):
    return _dg(a, b, _TN)


def _bdot_tn_fwd(a, b):
    return _dg(a, b, _TN), (a, b)


def _bdot_tn_bwd(res, ct):
    a, b = res
    return _dg(b, ct, _NT), _dg(a, ct, _NN)


bdot_tn.defvjp(_bdot_tn_fwd, _bdot_tn_bwd)


def _split3(x):
    x1 = x.astype(BF16)
    r1 = x - x1.astype(F32)
    x2 = r1.astype(BF16)
    x3 = (r1 - x2.astype(F32)).astype(BF16)
    return x1, x2, x3


def _mask_dot(mask_bf16, x, dims):
    x1, x2, x3 = _split3(x)
    f = lambda t: lax.dot_general(mask_bf16, t, dims, preferred_element_type=F32)
    return f(x1) + f(x2) + f(x3)


@jax.custom_vjp
def mask_cumsum(mask, x):
    return _mask_dot(mask.astype(BF16), x, _NN)


def _mask_cumsum_fwd(mask, x):
    return mask_cumsum(mask, x), mask


def _mask_cumsum_bwd(mask, ct):
    return jnp.zeros_like(mask), _mask_dot(mask.astype(BF16), ct, _TN)


mask_cumsum.defvjp(_mask_cumsum_fwd, _mask_cumsum_bwd)


def _roll(x, shift, axis):
    return pltpu.roll(x, shift % x.shape[axis], axis)


@functools.partial(jax.custom_vjp, nondiff_argnums=(1, 2))
def roll(x, shift, axis):
    return _roll(x, shift, axis)


def _roll_fwd(x, shift, axis):
    return _roll(x, shift, axis), None


def _roll_bwd(shift, axis, _, ct):
    return (_roll(ct, -shift, axis),)


roll.defvjp(_roll_fwd, _roll_bwd)


def rms(x):
    return x * lax.rsqrt(jnp.mean(x * x, axis=-1, keepdims=True) + EPS)


def silu(x):
    return x * (0.5 + 0.5 * jnp.tanh(0.5 * x))


def log_sigmoid(x):
    return jnp.minimum(x, 0.0) - jnp.log(1.0 + jnp.exp(-jnp.abs(x)))


def rope(t, cos, sin):
    return t * cos + roll(t, HEAD_DIM // 2, 1) * sin


def _heads(x, n, width=HEAD_DIM):
    return [x[:, h * width:(h + 1) * width] for h in range(n)]


class Row(NamedTuple):
    arr: jax.Array
    width: int
    idx: int = 0
    diff: bool = True


class Par(NamedTuple):
    arr: jax.Array
    grouped: bool = False
    diff: bool = True


def _row_specs(rows, pars, tm, n_lat_tiles):
    def grp(i):
        return jnp.minimum(i // n_lat_tiles, 1)

    specs = [pl.BlockSpec((tm, r.width), functools.partial(lambda i, k: (i, k), k=r.idx)) for r in rows]
    for p in pars:
        blk = (1,) + p.arr.shape[1:]
        if p.grouped:
            specs.append(pl.BlockSpec(blk, lambda i: (grp(i), 0, 0)))
        else:
            specs.append(pl.BlockSpec(blk, lambda i: (0, 0, 0)))
    return specs


def row_map(name, fn, rows, pars, outs, n_rows, n_lat):
    tm = ROW_TILE
    nr, npar = len(rows), len(pars)

    def body(*refs):
        vals = [r[...] for r in refs[:nr]] + [p[0] for p in refs[nr:nr + npar]]
        res = fn(*vals)
        for o, v in zip(refs[nr + npar:], res):
            o[...] = v.astype(o.dtype)

    return pl.pallas_call(
        body, name=name, grid=(n_rows // tm,),
        in_specs=_row_specs(rows, pars, tm, n_lat // tm),
        out_specs=[pl.BlockSpec((tm, w), lambda i: (i, 0)) for w, _ in outs],
        out_shape=[jax.ShapeDtypeStruct((n_rows, w), dt) for w, dt in outs],
        compiler_params=_params(("arbitrary",)),
    )(*[r.arr for r in rows], *[p.arr for p in pars])


def row_vjp(name, fn, rows, pars, cts, n_rows, n_lat, add_to_first=None, row_grad_dtype=F32, after=None):
    tm = ROW_TILE
    nr, npar, nc = len(rows), len(pars), len(cts)
    n_lat_tiles = n_lat // tm
    args = list(rows) + list(pars)
    diff_pos = [k for k, a in enumerate(args) if a.diff]
    n_add = 0 if add_to_first is None else 1
    n_after = 0 if after is None else 1

    def body(*refs):
        i = pl.program_id(0)
        vals = [r[...] for r in refs[:nr]] + [p[0] for p in refs[nr:nr + npar]]
        ct_vals = tuple(c[...] for c in refs[nr + npar:nr + npar + nc])
        out_refs = refs[nr + npar + nc + n_add + n_after:]

        def g(*dv):
            full = list(vals)
            for k, v in zip(diff_pos, dv):
                full[k] = v
            return tuple(fn(*full))

        _, vjp = jax.vjp(g, *[vals[k] for k in diff_pos])
        grads = vjp(ct_vals)
        for n, (k, o, gr) in enumerate(zip(diff_pos, out_refs, grads)):
            if k < nr:
                o[...] = (gr + refs[nr + npar + nc][...] if (n == 0 and n_add) else gr).astype(o.dtype)
            else:
                first = (i == 0) | (i == n_lat_tiles) if args[k].grouped else (i == 0)

                @pl.when(first)
                def _():
                    o[0] = gr

                @pl.when(jnp.logical_not(first))
                def _():
                    o[0] += gr

    def grp(i):
        return jnp.minimum(i // n_lat_tiles, 1)

    out_specs, out_shape = [], []
    for k in diff_pos:
        a = args[k]
        if k < nr:
            out_specs.append(pl.BlockSpec((tm, a.width), lambda i: (i, 0)))
            dtype = row_grad_dtype[len(out_shape)] if isinstance(row_grad_dtype, tuple) else row_grad_dtype
            out_shape.append(jax.ShapeDtypeStruct((n_rows, a.width), dtype))
        else:
            blk = (1,) + a.arr.shape[1:]
            out_specs.append(pl.BlockSpec(blk, (lambda i: (grp(i), 0, 0)) if a.grouped else (lambda i: (0, 0, 0))))
            out_shape.append(jax.ShapeDtypeStruct(a.arr.shape, F32))
    extra = list(cts) + ([add_to_first] if n_add else [])
    ct_specs = [pl.BlockSpec((tm, c.shape[1]), lambda i: (i, 0)) for c in extra]
    if n_after:
        extra.append(after)
        ct_specs.append(pl.BlockSpec(memory_space=pl.ANY))
    return pl.pallas_call(
        body, name=name, grid=(n_rows // tm,),
        in_specs=_row_specs(rows, pars, tm, n_lat_tiles) + ct_specs,
        out_specs=out_specs, out_shape=out_shape,
        compiler_params=_params(("arbitrary",)),
    )(*[r.arr for r in rows], *[p.arr for p in pars], *extra)


class BView(NamedTuple):
    n: int
    k: int
    tn: int
    tk: int
    index_map: object
    lead: int = 1
    part_maps: tuple = ()


MATMUL_VMEM_BUDGET = 40 * 1024 * 1024


def _matmul_tiles(m, n, k, a_bytes, b_bytes, o_bytes):
    tms = [c for c in (1152, 1024, 768, 512, 256, 128) if m % c == 0] or [m]
    tns = [c for c in (2048, 1408, 1280, 1024, 768, 512, 256, 128) if n % c == 0] or [n]
    tks = [k] + [c for c in (2816, 2304, 2048, 1408, 1024, 512, 256, 128) if k % c == 0 and c < k]
    for tk in tks:
        fits = [(tm * tn, tm, tn) for tm in tms for tn in tns
                if 2 * (tm * tk * a_bytes + tk * tn * b_bytes + tm * tn * o_bytes) + 2 * tm * tn * 4 <= MATMUL_VMEM_BUDGET]
        if fits and (max(fits)[0] >= min(512 * 512, tms[0] * tns[0]) or tk == tks[-1]):
            _, tm, tn = max(fits)
            return tm, tn, tk
    raise ValueError(f"no matmul tiling for {(m, n, k)}")


class OView(NamedTuple):
    shape: tuple
    index_map: object
    tn: int = None
    into: object = None


def matmul(name, a, b, *, ta=False, tb=False, add=None, out_dtype=F32, view=None, o_view=None, after=None):
    m = a.shape[1] if ta else a.shape[0]
    o_bytes = jnp.dtype(out_dtype).itemsize * (1 if add is None else 2)
    if view is None:
        k = a.shape[0] if ta else a.shape[1]
        n = b.shape[0] if tb else b.shape[1]
        assert (b.shape[1] if tb else b.shape[0]) == k, (a.shape, b.shape, ta, tb)
        if o_view is not None and o_view.tn is not None:
            tn = o_view.tn
            tm, _, tk = _matmul_tiles(m, tn, k, a.dtype.itemsize, b.dtype.itemsize, o_bytes)
        else:
            tm, tn, tk = _matmul_tiles(m, n, k, a.dtype.itemsize, b.dtype.itemsize, o_bytes)
    else:
        n, k, tn, tk = view.n, view.k, view.tn, view.tk
        b_maps = view.part_maps or (view.index_map,)
        tm, _, whole = _matmul_tiles(m, tn, tk * len(b_maps), a.dtype.itemsize, b.dtype.itemsize, o_bytes)
        assert whole == tk * len(b_maps) and not (ta and len(b_maps) > 1), (name, tm, whole)
    parts = 1 if view is None else len(b_maps)
    k_step = tk * parts
    nk = k // k_step
    dims = (((0 if ta else 1,), (1 if tb else 0,)), ((), ()))

    def body(a_ref, *rest):
        b_refs, rest = rest[:parts], rest[parts:]
        if parts == 1:
            prod = lax.dot_general(a_ref[...].astype(BF16), b_refs[0][...].astype(BF16), dims, preferred_element_type=F32)
        else:
            prod = sum(lax.dot_general(a_ref[:, p * tk:(p + 1) * tk].astype(BF16), b_refs[p][...].astype(BF16), dims,
                                       preferred_element_type=F32) for p in range(parts))
        if nk == 1:
            o_ref = rest[-1]
            o_ref[...] = (prod if add is None else prod + rest[0][...]).astype(o_ref.dtype)
            return
        o_ref, acc = rest[-2:]
        kk = pl.program_id(2)

        @pl.when(kk == 0)
        def _():
            acc[...] = prod

        @pl.when(kk != 0)
        def _():
            acc[...] += prod

        @pl.when(kk == nk - 1)
        def _():
            r = acc[...]
            if add is not None:
                r = r + rest[0][...]
            o_ref[...] = r.astype(o_ref.dtype)

    if ta:
        a_spec = pl.BlockSpec((k_step, tm), lambda i, j, kk: (kk, i))
    else:
        a_spec = pl.BlockSpec((tm, k_step), lambda i, j, kk: (i, kk))
    b_tile = (tn, tk) if tb else (tk, tn)
    if view is not None:
        b_specs = [pl.BlockSpec((None,) * view.lead + b_tile, index_map) for index_map in b_maps]
    elif tb:
        b_specs = [pl.BlockSpec(b_tile, lambda i, j, kk: (j, kk))]
    else:
        b_specs = [pl.BlockSpec(b_tile, lambda i, j, kk: (kk, j))]
    o_spec = pl.BlockSpec((tm, tn), lambda i, j, kk: (i, j))
    ins = [a] + [b] * parts + ([add] if add is not None else [])
    in_specs = [a_spec] + b_specs + ([o_spec] if add is not None else [])
    out_shape, aliases = jax.ShapeDtypeStruct((m, n), out_dtype), {}
    if o_view is not None:
        assert add is None
        o_spec = pl.BlockSpec((None, tm, tn), o_view.index_map)
        out_shape = jax.ShapeDtypeStruct(o_view.shape, out_dtype)
        if o_view.into is not None:
            aliases = {len(ins): 0}
            ins.append(o_view.into)
            in_specs.append(pl.BlockSpec(memory_space=pl.ANY))
    if after is not None:
        ins.append(after)
        in_specs.append(pl.BlockSpec(memory_space=pl.ANY))
    return pl.pallas_call(
        body, name=name, grid=(m // tm, n // tn, nk),
        in_specs=in_specs, out_specs=o_spec, out_shape=out_shape, input_output_aliases=aliases,
        scratch_shapes=[pltpu.VMEM((tm, tn), F32)] if nk > 1 else [],
        compiler_params=_params(("parallel", "parallel", "arbitrary")),
    )(*ins)


def normmod_tile(x, g, shift, scale):
    return (rms(x) * g * (1.0 + scale) + shift,)


def resid_tile(x, y, gate):
    return (x + gate * y,)


def resid_norm_tile(x, y, gate, g, shift, scale):
    x1 = x + gate * y
    return x1, rms(x1) * g * (1.0 + scale) + shift


def gated_tile(y, gate):
    return (gate * y,)


def prep_tile(z_qk, z_rq, z_rk, z_gq, zg, cos, sin, qg, kg, gate_up, gate_b):
    out = []
    for h, t in enumerate(_heads(z_qk, ATT_Q_HEADS + ATT_KV_HEADS)):
        out.append(rope(rms(t) * (qg if h < ATT_Q_HEADS else kg), cos, sin))
    gq = z_gq * (GLA_DK ** -0.5)
    rq = [rope(t, cos, sin) for t in _heads(z_rq, RET_HEADS)]
    rk = [rope(t * (HEAD_DIM ** -0.5), cos, sin) for t in _heads(z_rk, RET_HEADS)]
    la = [log_sigmoid(bdot(zg, gate_up[d * LANES:(d + 1) * LANES]) + gate_b[d:d + 1]) * (1.0 / GLA_TAU) for d in range(2)]
    return (jnp.concatenate(out + [gq] + rq + rk + la, axis=1),)


def post_tile(o_att, o_ret_f, o_ret_b, o_gla_f, o_gla_b, rg, gr, ret_g, gla_g):
    ret = jnp.concatenate([rms(t) * ret_g for t in _heads(o_ret_f + o_ret_b, RET_HEADS)], axis=1) * silu(rg)
    gla = jnp.concatenate([rms(t) * gla_g for t in _heads(o_gla_f + o_gla_b, GLA_HEADS)], axis=1) * silu(gr)
    return (jnp.concatenate([o_att, ret, gla], axis=1),)


def _convglu_tile(n_lat, a, v, cw, cb):
    t = a.shape[0]
    row = lax.broadcasted_iota(jnp.int32, (t, 1), 0)
    has_prev = ((row != 0) & (row != n_lat)).astype(F32)
    has_next = ((row != n_lat - 1) & (row != t - 1)).astype(F32)
    conv = roll(a, 1, 0) * has_prev * cw[0:1] + a * cw[1:2] + roll(a, -1, 0) * has_next * cw[2:3] + cb
    return silu(conv) * v


def convglu(name, u, cw, cb, n_lat):
    t, f2 = u.shape
    f, tc = f2 // 2, FFN_COL_TILE
    nb = f // tc

    def body(a_ref, v_ref, cw_ref, cb_ref, o_ref):
        o_ref[...] = _convglu_tile(n_lat, a_ref[...].astype(F32), v_ref[...].astype(F32), cw_ref[...], cb_ref[...]).astype(o_ref.dtype)

    return pl.pallas_call(
        body, name=name, grid=(nb,),
        in_specs=[pl.BlockSpec((t, tc), lambda j: (0, j)), pl.BlockSpec((t, tc), lambda j: (0, nb + j)),
                  pl.BlockSpec((3, tc), lambda j: (0, j)), pl.BlockSpec((1, tc), lambda j: (0, j))],
        out_specs=pl.BlockSpec((t, tc), lambda j: (0, j)),
        out_shape=jax.ShapeDtypeStruct((t, f), BF16),
        compiler_params=_params(("parallel",)),
    )(u, u, cw, cb)


def convglu_bwd(name, u, cw, cb, dg, n_lat):
    t, f2 = u.shape
    f, tc = f2 // 2, FFN_COL_TILE
    nb = f // tc

    def body(a_ref, v_ref, cw_ref, cb_ref, dg_ref, da_ref, dv_ref, dcw_ref, dcb_ref):
        _, vjp = jax.vjp(functools.partial(_convglu_tile, n_lat), a_ref[...].astype(F32), v_ref[...].astype(F32),
                         cw_ref[...], cb_ref[...])
        da, dv, dcw_ref[...], dcb_ref[...] = vjp(dg_ref[...])
        da_ref[...], dv_ref[...] = da.astype(BF16), dv.astype(BF16)

    col = pl.BlockSpec((t, tc), lambda j: (0, j))
    return pl.pallas_call(
        body, name=name, grid=(nb,),
        in_specs=[col, pl.BlockSpec((t, tc), lambda j: (0, nb + j)), pl.BlockSpec((3, tc), lambda j: (0, j)),
                  pl.BlockSpec((1, tc), lambda j: (0, j)), col],
        out_specs=[col, col, pl.BlockSpec((3, tc), lambda j: (0, j)), pl.BlockSpec((1, tc), lambda j: (0, j))],
        out_shape=[jax.ShapeDtypeStruct((t, f), BF16), jax.ShapeDtypeStruct((t, f), BF16),
                   jax.ShapeDtypeStruct((3, f), F32), jax.ShapeDtypeStruct((1, f), F32)],
        compiler_params=_params(("parallel",)),
    )(u, u, cw, cb, dg)


def final_loss(x, target, g, n_lat):
    tm = ROW_TILE
    d = x.shape[1]

    def body(x_ref, t_ref, g_ref, loss_ref, dx_ref, dg_ref):
        i = pl.program_id(0)
        tgt = t_ref[...]

        def f(xv, gv):
            e = rms(xv) * gv - tgt
            s = jnp.sum(jnp.sum(e * e, axis=1, keepdims=True), axis=0, keepdims=True)
            return s * (0.5 / d)

        val, vjp = jax.vjp(f, x_ref[...], g_ref[...])
        dx, dgv = vjp(jnp.ones((1, 1), F32))
        dx_ref[...] = dx

        @pl.when(i == 0)
        def _():
            dg_ref[...] = dgv
            loss_ref[...] = jnp.broadcast_to(val, loss_ref.shape)

        @pl.when(i != 0)
        def _():
            dg_ref[...] += dgv
            loss_ref[...] += jnp.broadcast_to(val, loss_ref.shape)

    return pl.pallas_call(
        body, name="final_loss", grid=(n_lat // tm,),
        in_specs=[pl.BlockSpec((tm, d), lambda i: (i, 0)), pl.BlockSpec((tm, d), lambda i: (i, 0)),
                  pl.BlockSpec((1, d), lambda i: (0, 0))],
        out_specs=[pl.BlockSpec((1, LANES), lambda i: (0, 0)), pl.BlockSpec((tm, d), lambda i: (i, 0)),
                   pl.BlockSpec((1, d), lambda i: (0, 0))],
        out_shape=[jax.ShapeDtypeStruct((1, LANES), F32), jax.ShapeDtypeStruct((n_lat, d), F32),
                   jax.ShapeDtypeStruct((1, d), F32)],
        compiler_params=_params(("arbitrary",)),
    )(x, target, g)


ATT_SCALE = HEAD_DIM ** -0.5
_AK_BLK = P_AK // HEAD_DIM
_AV_BLK = Z_AV // HEAD_DIM


def _att_specs(t, tq):
    gw = ATT_GROUP * HEAD_DIM
    q_spec = pl.BlockSpec((tq, gw), lambda kv, i: (i, kv))
    k_spec = pl.BlockSpec((t, HEAD_DIM), lambda kv, i: (0, _AK_BLK + kv))
    v_spec = pl.BlockSpec((t, HEAD_DIM), lambda kv, i: (0, _AV_BLK + kv))
    row_spec = pl.BlockSpec((ATT_GROUP, tq, 1), lambda kv, i: (kv, i, 0))
    return q_spec, k_spec, v_spec, row_spec


def _att_mask(i, t, tq, n_lat):
    col = lax.broadcasted_iota(jnp.int32, (1, t), 1)
    return jnp.where((i >= n_lat // tq) & (col < n_lat), -jnp.inf, 0.0).astype(F32)


def attn_fwd(p, z, n_lat):
    t = p.shape[0]
    tq = ROW_TILE

    def body(q_ref, k_ref, v_ref, o_ref, lse_ref):
        mask = _att_mask(pl.program_id(1), t, tq, n_lat)
        k, v = k_ref[...].astype(BF16), v_ref[...].astype(BF16)
        for g in range(ATT_GROUP):
            cols = slice(g * HEAD_DIM, (g + 1) * HEAD_DIM)
            s = _dg(q_ref[:, cols], k, _NT) * ATT_SCALE + mask
            m = jnp.max(s, axis=1, keepdims=True)
            pr = jnp.exp(s - m)
            l = jnp.sum(pr, axis=1, keepdims=True)
            o_ref[:, cols] = _dg(pr, v, _NN) / l
            lse_ref[g] = m + jnp.log(l)

    q_spec, k_spec, v_spec, row_spec = _att_specs(t, tq)
    return pl.pallas_call(
        body, name="attn_fwd", grid=(ATT_KV_HEADS, t // tq),
        in_specs=[q_spec, k_spec, v_spec], out_specs=[q_spec, row_spec],
        out_shape=[jax.ShapeDtypeStruct((t, ATT_Q_HEADS * HEAD_DIM), F32),
                   jax.ShapeDtypeStruct((ATT_Q_HEADS, t, 1), F32)],
        compiler_params=_params(("parallel", "parallel")),
    )(p, p, z)


def attn_bwd(p, z, o, lse, do, n_lat):
    t = p.shape[0]
    tq = ROW_TILE

    def body(q_ref, k_ref, v_ref, o_ref, do_ref, lse_ref, dq_ref, dk_ref, dv_ref):
        i = pl.program_id(1)

        @pl.when(i == 0)
        def _():
            dk_ref[...] = jnp.zeros_like(dk_ref)
            dv_ref[...] = jnp.zeros_like(dv_ref)

        mask = _att_mask(i, t, tq, n_lat)
        k, v = k_ref[...].astype(BF16), v_ref[...].astype(BF16)
        dk, dv = dk_ref[...], dv_ref[...]
        for g in range(ATT_GROUP):
            cols = slice(g * HEAD_DIM, (g + 1) * HEAD_DIM)
            q, do_g = q_ref[:, cols].astype(BF16), do_ref[:, cols]
            pr = jnp.exp(_dg(q, k, _NT) * ATT_SCALE + mask - lse_ref[g])
            delta = jnp.sum(o_ref[:, cols] * do_g, axis=1, keepdims=True)
            ds = pr * (_dg(do_g, v, _NT) - delta) * ATT_SCALE
            dq_ref[:, cols] = _dg(ds, k, _NN)
            dk = dk + _dg(ds, q, _TN)
            dv = dv + _dg(pr, do_g, _TN)
        dk_ref[...], dv_ref[...] = dk, dv

    q_spec, k_spec, v_spec, row_spec = _att_specs(t, tq)
    kv_out = pl.BlockSpec((t, HEAD_DIM), lambda kv, i: (0, kv))
    return pl.pallas_call(
        body, name="attn_bwd", grid=(ATT_KV_HEADS, t // tq),
        in_specs=[q_spec, k_spec, v_spec, q_spec, q_spec, row_spec],
        out_specs=[q_spec, kv_out, kv_out],
        out_shape=[jax.ShapeDtypeStruct((t, ATT_Q_HEADS * HEAD_DIM), F32),
                   jax.ShapeDtypeStruct((t, ATT_KV_HEADS * HEAD_DIM), F32),
                   jax.ShapeDtypeStruct((t, ATT_KV_HEADS * HEAD_DIM), F32)],
        compiler_params=_params(("parallel", "arbitrary")),
    )(p, p, z, o, do, lse)


_RQ_BLK = P_RQ // HEAD_DIM
_RK_BLK = P_RK // HEAD_DIM
_RV_BLK = Z_RV // HEAD_DIM


def _scan_chunk(direction, step, n_chunks, n_lat_chunks):
    return jnp.where(direction == 0, (step + n_lat_chunks) % n_chunks, n_chunks - 1 - step)


def _ret_geometry(direction):
    c = RET_CHUNK
    i = lax.broadcasted_iota(jnp.int32, (c, c), 0)
    j = lax.broadcasted_iota(jnp.int32, (c, c), 1)
    rel = jnp.where(direction == 0, i - j, j - i).astype(F32)
    r = lax.broadcasted_iota(jnp.int32, (c, 1), 0)
    pos = jnp.where(direction == 0, r, c - 1 - r).astype(F32)
    return rel, pos


def ret_chunk(q, k, v, s, lg, rel, pos):
    c = RET_CHUNK
    causal = rel >= 0
    d_in = jnp.where(causal, jnp.exp(lg * jnp.where(causal, rel, 0.0)), 0.0)
    q_dec = jnp.exp(lg * (pos + 1.0))
    k_dec = jnp.exp(lg * (c - 1.0 - pos))
    c_dec = jnp.exp(lg * c)
    att = bdot_nt(q, k) * d_in
    o = bdot(att, v) + bdot(q * q_dec, s)
    s_new = c_dec * s + bdot_tn(k * k_dec, v)
    return o, s_new


def ret_fwd(p, z, lg, n_lat):
    t = p.shape[0]
    c = RET_CHUNK
    nc, nlc = t // c, n_lat // c

    def body(q_ref, k_ref, v_ref, lg_ref, o_ref, ssave_ref, s_s):
        d, n = pl.program_id(0), pl.program_id(1)

        @pl.when(n == 0)
        def _():
            s_s[...] = jnp.zeros_like(s_s)

        rel, pos = _ret_geometry(d)
        for h in range(RET_HEADS):
            cols = slice(h * HEAD_DIM, (h + 1) * HEAD_DIM)
            ssave_ref[0, h, 0] = s_s[h]
            o, s_new = ret_chunk(q_ref[:, cols], k_ref[:, cols], v_ref[:, cols], s_s[h], lg_ref[0, h], rel, pos)
            o_ref[:, cols] = o
            s_s[h] = s_new

    w = RET_HEADS * HEAD_DIM

    def blk(base):
        return pl.BlockSpec((c, w), lambda d, n: (_scan_chunk(d, n, nc, nlc), base // RET_HEADS))

    return pl.pallas_call(
        body, name="ret_fwd", grid=(2, nc),
        in_specs=[blk(_RQ_BLK), blk(_RK_BLK), blk(_RV_BLK), pl.BlockSpec((1, RET_HEADS, 1, 1), lambda d, n: (d, 0, 0, 0))],
        out_specs=[pl.BlockSpec((c, w), lambda d, n: (_scan_chunk(d, n, nc, nlc), d)),
                   pl.BlockSpec((1, RET_HEADS, 1, HEAD_DIM, HEAD_DIM), lambda d, n: (d, 0, n, 0, 0))],
        out_shape=[jax.ShapeDtypeStruct((t, 2 * w), F32),
                   jax.ShapeDtypeStruct((2, RET_HEADS, nc, HEAD_DIM, HEAD_DIM), F32)],
        scratch_shapes=[pltpu.VMEM((RET_HEADS, HEAD_DIM, HEAD_DIM), F32)],
        compiler_params=_params(("parallel", "arbitrary")),
    )(p, p, z, lg)


def ret_bwd(p, z, lg, states, do, n_lat):
    t = p.shape[0]
    c = RET_CHUNK
    nc, nlc = t // c, n_lat // c

    def body(q_ref, k_ref, v_ref, lg_ref, s_ref, do_ref, dq_ref, dk_ref, dv_ref, dlg_ref, ds_s):
        d, n = pl.program_id(0), pl.program_id(1)

        @pl.when(n == 0)
        def _():
            ds_s[...] = jnp.zeros_like(ds_s)
            dlg_ref[...] = jnp.zeros_like(dlg_ref)

        rel, pos = _ret_geometry(d)
        f = functools.partial(ret_chunk, rel=rel, pos=pos)
        for h in range(RET_HEADS):
            cols = slice(h * HEAD_DIM, (h + 1) * HEAD_DIM)
            _, vjp = jax.vjp(f, q_ref[:, cols], k_ref[:, cols], v_ref[:, cols], s_ref[0, h, 0], lg_ref[0, h])
            dq, dk, dv, ds, dlg = vjp((do_ref[:, cols], ds_s[h]))
            dq_ref[:, cols], dk_ref[:, cols], dv_ref[:, cols] = dq, dk, dv
            ds_s[h] = ds
            dlg_ref[0, h] += dlg

    def chunk_of(d, n):
        return _scan_chunk(d, nc - 1 - n, nc, nlc)

    w = RET_HEADS * HEAD_DIM

    def blk(base):
        return pl.BlockSpec((c, w), lambda d, n: (chunk_of(d, n), base // RET_HEADS))

    out_blk = pl.BlockSpec((c, w), lambda d, n: (chunk_of(d, n), d))
    lg_blk = pl.BlockSpec((1, RET_HEADS, 1, 1), lambda d, n: (d, 0, 0, 0))
    grad_shape = jax.ShapeDtypeStruct((t, 2 * w), F32)
    return pl.pallas_call(
        body, name="ret_bwd", grid=(2, nc),
        in_specs=[blk(_RQ_BLK), blk(_RK_BLK), blk(_RV_BLK), lg_blk,
                  pl.BlockSpec((1, RET_HEADS, 1, HEAD_DIM, HEAD_DIM), lambda d, n: (d, 0, nc - 1 - n, 0, 0)),
                  pl.BlockSpec((c, w), lambda d, n: (chunk_of(d, n), 0))],
        out_specs=[out_blk, out_blk, out_blk, lg_blk],
        out_shape=[grad_shape, grad_shape, grad_shape, jax.ShapeDtypeStruct((2, RET_HEADS, 1, 1), F32)],
        scratch_shapes=[pltpu.VMEM((RET_HEADS, HEAD_DIM, HEAD_DIM), F32)],
        compiler_params=_params(("parallel", "arbitrary")),
    )(p, p, z, lg, states, do)


_GQ_BLK = P_GQ // (GLA_HEADS * GLA_DK)
_GK_BLK = Z_GK // (GLA_HEADS * GLA_DK)
_GV_BLK = Z_GV // (GLA_HEADS * GLA_DV)
_LA_BLK = P_LA // (GLA_HEADS * GLA_DK)


def _gla_mask(direction):
    c = GLA_CHUNK
    i = lax.broadcasted_iota(jnp.int32, (c, c), 0)
    j = lax.broadcasted_iota(jnp.int32, (c, c), 1)
    return (jnp.where(direction == 0, i - j, j - i) >= 0).astype(F32)


def gla_chunk(q, k, v, la, st, mask):
    b = mask_cumsum(mask, la)
    btot = jnp.sum(la, axis=0, keepdims=True)
    half = 0.5 * btot
    qt, kt = q * jnp.exp(b - half), k * jnp.exp(half - b)
    qs, ke = q * jnp.exp(b), k * jnp.exp(btot - b)
    outs, upd = [], []
    for h in range(GLA_HEADS):
        ks = slice(h * GLA_DK, (h + 1) * GLA_DK)
        vh = v[:, h * GLA_DV:(h + 1) * GLA_DV]
        att = bdot_nt(qt[:, ks], kt[:, ks]) * mask
        outs.append(bdot(att, vh) + bdot_nt(qs[:, ks], st[:, ks]))
        upd.append(bdot_tn(vh, ke[:, ks]))
    st_new = st * jnp.exp(btot) + jnp.concatenate(upd, axis=1)
    return jnp.concatenate(outs, axis=1), st_new


def gla_fwd(p, z, n_lat):
    t = p.shape[0]
    c = GLA_CHUNK
    nc, nlc = t // c, n_lat // c
    kw, vw = GLA_HEADS * GLA_DK, GLA_HEADS * GLA_DV

    def body(q_ref, k_ref, v_ref, la_ref, o_ref, ssave_ref, s_s):
        d, n = pl.program_id(0), pl.program_id(1)

        @pl.when(n == 0)
        def _():
            s_s[...] = jnp.zeros_like(s_s)

        ssave_ref[0, 0] = s_s[...]
        o, s_new = gla_chunk(q_ref[...], k_ref[...], v_ref[...], la_ref[...], s_s[...], _gla_mask(d))
        o_ref[...] = o
        s_s[...] = s_new

    def chunk_of(d, n):
        return _scan_chunk(d, n, nc, nlc)

    return pl.pallas_call(
        body, name="gla_fwd", grid=(2, nc),
        in_specs=[pl.BlockSpec((c, kw), lambda d, n: (chunk_of(d, n), _GQ_BLK)),
                  pl.BlockSpec((c, kw), lambda d, n: (chunk_of(d, n), _GK_BLK)),
                  pl.BlockSpec((c, vw), lambda d, n: (chunk_of(d, n), _GV_BLK)),
                  pl.BlockSpec((c, kw), lambda d, n: (chunk_of(d, n), _LA_BLK + d))],
        out_specs=[pl.BlockSpec((c, vw), lambda d, n: (chunk_of(d, n), d)),
                   pl.BlockSpec((1, 1, GLA_DV, kw), lambda d, n: (d, n, 0, 0))],
        out_shape=[jax.ShapeDtypeStruct((t, 2 * vw), F32), jax.ShapeDtypeStruct((2, nc, GLA_DV, kw), F32)],
        scratch_shapes=[pltpu.VMEM((GLA_DV, kw), F32)],
        compiler_params=_params(("parallel", "arbitrary")),
    )(p, z, z, p)


def gla_bwd(p, z, states, do, n_lat):
    t = p.shape[0]
    c = GLA_CHUNK
    nc, nlc = t // c, n_lat // c
    kw, vw = GLA_HEADS * GLA_DK, GLA_HEADS * GLA_DV

    def body(q_ref, k_ref, v_ref, la_ref, s_ref, do_ref, dq_ref, dk_ref, dv_ref, dla_ref, ds_s):
        d, n = pl.program_id(0), pl.program_id(1)

        @pl.when(n == 0)
        def _():
            ds_s[...] = jnp.zeros_like(ds_s)

        f = functools.partial(gla_chunk, mask=_gla_mask(d))
        _, vjp = jax.vjp(f, q_ref[...], k_ref[...], v_ref[...], la_ref[...], s_ref[0, 0])
        dq_ref[...], dk_ref[...], dv_ref[...], dla_ref[...], ds_s[...] = vjp((do_ref[...], ds_s[...]))

    def chunk_of(d, n):
        return _scan_chunk(d, nc - 1 - n, nc, nlc)

    k_out = pl.BlockSpec((c, kw), lambda d, n: (chunk_of(d, n), d))
    return pl.pallas_call(
        body, name="gla_bwd", grid=(2, nc),
        in_specs=[pl.BlockSpec((c, kw), lambda d, n: (chunk_of(d, n), _GQ_BLK)),
                  pl.BlockSpec((c, kw), lambda d, n: (chunk_of(d, n), _GK_BLK)),
                  pl.BlockSpec((c, vw), lambda d, n: (chunk_of(d, n), _GV_BLK)),
                  pl.BlockSpec((c, kw), lambda d, n: (chunk_of(d, n), _LA_BLK + d)),
                  pl.BlockSpec((1, 1, GLA_DV, kw), lambda d, n: (d, nc - 1 - n, 0, 0)),
                  pl.BlockSpec((c, vw), lambda d, n: (chunk_of(d, n), 0))],
        out_specs=[k_out, k_out, pl.BlockSpec((c, vw), lambda d, n: (chunk_of(d, n), d)), k_out],
        out_shape=[jax.ShapeDtypeStruct((t, 2 * kw), F32), jax.ShapeDtypeStruct((t, 2 * kw), F32),
                   jax.ShapeDtypeStruct((t, 2 * vw), F32), jax.ShapeDtypeStruct((t, 2 * kw), F32)],
        scratch_shapes=[pltpu.VMEM((GLA_DV, kw), F32)],
        compiler_params=_params(("parallel", "arbitrary")),
    )(p, z, z, p, states, do)


def _adam_tile(w, g, m, v):
    m = ADAM_B1 * m + (1.0 - ADAM_B1) * g
    v = ADAM_B2 * v + (1.0 - ADAM_B2) * (g * g)
    m_hat = m / (1.0 - ADAM_B1 ** ADAM_STEP)
    v_hat = v / (1.0 - ADAM_B2 ** ADAM_STEP)
    delta = -ADAM_LR * (m_hat / (jnp.sqrt(v_hat) + ADAM_EPS) + ADAM_WD * w)
    return delta, m, v


def adamw(name, w, g, m, v):
    shape = w.shape
    cols = shape[-1] if w.ndim > 1 and shape[-1] >= LANES else int(np.prod(shape))
    rows = int(np.prod(shape)) // cols
    tr = rows
    for cand in (512, 256, 128, 64, 32, 16, 8):
        if rows % cand == 0 and cand * cols * 4 <= (1 << 21):
            tr = cand
            break
    flat = [a.reshape(rows, cols) for a in (w, g, m, v)]

    def body(w_ref, g_ref, m_ref, v_ref, d_ref, mo_ref, vo_ref):
        d_ref[...], mo_ref[...], vo_ref[...] = _adam_tile(w_ref[...], g_ref[...], m_ref[...], v_ref[...])

    spec = pl.BlockSpec((tr, cols), lambda i: (i, 0))
    outs = pl.pallas_call(
        body, name=name, grid=(rows // tr,),
        in_specs=[spec] * 4, out_specs=[spec] * 3,
        out_shape=[jax.ShapeDtypeStruct((rows, cols), F32)] * 3,
        compiler_params=_params(("parallel",)),
    )(*flat)
    return tuple(o.reshape(shape) for o in outs)


def adamw_layers(name, w, grads, m, v):
    depth, rows, cols = w.shape
    tr = _rows_tile(rows, cols)
    nb = rows // tr

    def body(w_ref, m_ref, v_ref, *rest):
        g_refs, (g_ref, d_ref, mo_ref, vo_ref) = rest[:depth], rest[depth:]
        l = pl.program_id(0)
        for k in range(depth):
            @pl.when(l == k)
            def _():
                g = g_refs[k][...]
                g_ref[...] = g
                d_ref[...], mo_ref[...], vo_ref[...] = _adam_tile(w_ref[...], g, m_ref[...], v_ref[...])

    def layer_grad(k):
        return pl.BlockSpec((tr, cols), lambda l, i: (jnp.where(l < k, 0, jnp.where(l == k, i, nb - 1)), 0))

    spec = pl.BlockSpec((tr, cols), lambda l, i: (l * nb + i, 0))
    flat = [a.reshape(depth * rows, cols) for a in (w, m, v)]
    g_all, delta, new_m, new_v = pl.pallas_call(
        body, name=name, grid=(depth, nb),
        in_specs=[spec] * 3 + [layer_grad(k) for k in range(depth)], out_specs=[spec] * 4,
        out_shape=[jax.ShapeDtypeStruct((depth * rows, cols), F32)] * 4,
        compiler_params=_params(("arbitrary", "arbitrary")),
    )(*flat, *grads)
    return tuple(a.reshape(w.shape) for a in (delta, new_m, new_v)), g_all.reshape(w.shape)


MESH = pl.DeviceIdType.MESH
_HBM = pl.BlockSpec(memory_space=pltpu.HBM)
N_CHIPS = 4
N_DEV = 8


def _place():
    x, y, c = lax.axis_index("x"), lax.axis_index("y"), lax.axis_index("c")
    chips = [(1 - x, y), (x, 1 - y), (1 - x, 1 - y)]
    return x, y, c, chips


def _remote(src, dst, send_sem, recv_sem, to):
    return pltpu.make_async_remote_copy(src_ref=src, dst_ref=dst, send_sem=send_sem, recv_sem=recv_sem,
                                        device_id=to, device_id_type=MESH)


def all_gather_small(name, v):
    m_per, n = v.shape

    def body(x_ref, out_ref, send_sems, recv_sems, local_sem):
        x, y, c, chips = _place()
        me, sibling = (x, y, c), (x, y, 1 - c)

        def rows(px, py, pc):
            return out_ref.at[pl.ds((4 * px + 2 * py + pc) * m_per, m_per), :]

        def copy(k, block, to, src=None):
            return _remote(rows(*block) if src is None else src, rows(*block), send_sems.at[k], recv_sems.at[k], to)

        mine = pltpu.make_async_copy(x_ref, rows(*me), local_sem)
        mine.start()
        first = [copy(0, me, sibling, src=x_ref)]
        first += [copy(1 + j, me, (*chip, c), src=x_ref) for j, chip in enumerate(chips)]
        for cp in first:
            cp.start()
        passed = [copy(4 + j, (*chip, c), sibling) for j, chip in enumerate(chips)]
        for j, chip in enumerate(chips):
            copy(1 + j, (*chip, c), me).wait_recv()
            passed[j].start()
        copy(0, sibling, me).wait_recv()
        for j, chip in enumerate(chips):
            copy(4 + j, (*chip, 1 - c), me).wait_recv()
        for cp in first + passed:
            cp.wait_send()
        mine.wait()

    return pl.pallas_call(
        body, name=name,
        out_shape=jax.ShapeDtypeStruct((N_DEV * m_per, n), v.dtype),
        in_specs=[pl.BlockSpec(memory_space=pltpu.VMEM)],
        out_specs=pl.BlockSpec(memory_space=pltpu.VMEM),
        scratch_shapes=[pltpu.SemaphoreType.DMA((7,)), pltpu.SemaphoreType.DMA((7,)), pltpu.SemaphoreType.DMA],
        compiler_params=pltpu.CompilerParams(vmem_limit_bytes=VMEM_LIMIT),
    )(v)


_SEM = pl.BlockSpec(memory_space=pltpu.SEMAPHORE)
_SPLIT_COPY = pltpu.CompilerParams(has_side_effects=pltpu.SideEffectType.DATAFLOW_SIDE_EFFECTING)


class CopyPlan(NamedTuple):
    copies: object
    n: int
    in_place: bool = False


def _gather_copies(x_ref, land_ref, x, y, c, chips):
    half = x_ref.shape[0] // 2
    rows = pl.ds(c * half, half)
    return [(x_ref.at[rows, :], land_ref.at[2 * x + y, rows, :], (*chip, c), land_ref.at[2 * chip[0] + chip[1], rows, :])
            for chip in chips]


def _pass_copies(land_ref, _, x, y, c, chips):
    half = land_ref.shape[1] // 2
    mine, other = pl.ds(c * half, half), pl.ds((1 - c) * half, half)
    return [(land_ref.at[2 * chip[0] + chip[1], mine, :], land_ref.at[2 * chip[0] + chip[1], mine, :], (x, y, 1 - c),
             land_ref.at[2 * chip[0] + chip[1], other, :]) for chip in chips]


def _sibling_half_copies(p_ref, land_ref, x, y, c, chips):
    half = p_ref.shape[1] // 2
    return [(p_ref.at[:, pl.ds((1 - c) * half, half), :], land_ref, (x, y, 1 - c), land_ref)]


def _scatter_copies(s_ref, land_ref, x, y, c, chips):
    return [(s_ref.at[2 * chip[0] + chip[1]], land_ref.at[j], (*chip, c), land_ref.at[j]) for j, chip in enumerate(chips)]


def _join_copies(buf_ref, _, x, y, c, chips):
    half = buf_ref.shape[0] // 2
    mine = buf_ref.at[pl.ds(c * half, half), :]
    return [(mine, mine, (x, y, 1 - c), buf_ref.at[pl.ds((1 - c) * half, half), :])]


GATHER = CopyPlan(_gather_copies, 3)
PASS_ON = CopyPlan(_pass_copies, 3, in_place=True)
SIBLING_HALF = CopyPlan(_sibling_half_copies, 1)
SCATTER = CopyPlan(_scatter_copies, 3)
JOIN = CopyPlan(_join_copies, 1, in_place=True)


def split_start(name, plan, srcs, land_shapes=None, after=None):
    nt = len(srcs)
    arrays = [pltpu.with_memory_space_constraint(s, pltpu.HBM) for s in srcs]
    if not plan.in_place:
        arrays += [pltpu.with_memory_space_constraint(lax.empty(shape, s.dtype), pltpu.HBM) for shape, s in zip(land_shapes, srcs)]
    na = len(arrays)
    behind = [] if after is None else [after]
    n_in = na + len(behind)

    def body(*refs):
        x_refs = refs[:nt]
        land_refs = x_refs if plan.in_place else refs[nt:na]
        send, recv = refs[n_in:n_in + nt], refs[n_in + nt:n_in + 2 * nt]
        x, y, c, chips = _place()
        for t in range(nt):
            for j, (src, dst, to, _) in enumerate(plan.copies(x_refs[t], land_refs[t], x, y, c, chips)):
                _remote(src, dst, send[t].at[j], recv[t].at[j], to).start()
        refs[-1][...] = jnp.zeros_like(refs[-1])

    outs = pl.pallas_call(
        body, name=name,
        out_shape=tuple([pltpu.SemaphoreType.DMA((plan.n,))] * (2 * nt) + [pltpu.HBM(a.shape, a.dtype) for a in arrays]
                        + [jax.ShapeDtypeStruct((8, LANES), F32)]),
        in_specs=[_HBM] * na + [pl.BlockSpec(memory_space=pl.ANY)] * len(behind),
        out_specs=tuple([_SEM] * (2 * nt) + [_HBM] * na + [pl.BlockSpec(memory_space=pltpu.VMEM)]),
        input_output_aliases={i: 2 * nt + i for i in range(na)},
        compiler_params=_SPLIT_COPY,
    )(*arrays, *behind)
    groups = [(outs[t], outs[nt + t]) + tuple(outs[2 * nt + t + k * nt] for k in range(na // nt)) for t in range(nt)]
    return groups, outs[-1]


def split_wait(name, plan, group, after):
    send, recv, *arrays = group
    na = len(arrays)

    def body(*refs):
        x_ref, land_ref = refs[0], refs[na - 1]
        send_sem, recv_sem = refs[na], refs[na + 1]
        x, y, c, chips = _place()
        for j, (s, _, to, arrival) in enumerate(plan.copies(x_ref, land_ref, x, y, c, chips)):
            cp = _remote(s, arrival, send_sem.at[j], recv_sem.at[j], to)
            cp.wait_send()
            cp.wait_recv()

    return pl.pallas_call(
        body, name=name,
        out_shape=tuple(pltpu.HBM(a.shape, a.dtype) for a in arrays),
        in_specs=tuple([_HBM] * na + [_SEM, _SEM, pl.BlockSpec(memory_space=pl.ANY)]), out_specs=tuple([_HBM] * na),
        input_output_aliases={i: i for i in range(na)}, compiler_params=_SPLIT_COPY,
    )(*arrays, send, recv, after)


def _rows_tile(rows, cols):
    for cand in (512, 256, 128, 64, 32, 16):
        if rows % cand == 0 and cand * cols * 4 <= (1 << 21):
            return cand
    return rows


def add_sibling_half(name, pieces, from_sibling, core):
    n, h, cols = from_sibling.shape
    tr = _rows_tile(h, cols // 2)
    nb = h // tr

    def body(c_ref, a_ref, b_ref, o_ref):
        o_ref[...] = (a_ref[...].astype(F32) + b_ref[...].astype(F32)).astype(o_ref.dtype)

    blk = pl.BlockSpec((1, tr, cols), lambda q, i, c_ref: (q, i, 0))
    return pl.pallas_call(
        body, name=name,
        grid_spec=pltpu.PrefetchScalarGridSpec(
            num_scalar_prefetch=1, grid=(n, nb),
            in_specs=[pl.BlockSpec((1, tr, cols), lambda q, i, c_ref: (q, c_ref[0] * nb + i, 0)), blk], out_specs=blk),
        out_shape=jax.ShapeDtypeStruct((n, h, cols), BF16),
        compiler_params=_params(("parallel", "parallel")),
    )(core.reshape(1).astype(jnp.int32), pieces, from_sibling)


def add_chip_sums(name, chip_sums, from_chips, chip, core):
    _, h, cols = chip_sums.shape
    tr = _rows_tile(h, cols)
    nb = h // tr

    def body(s_ref, own_ref, r0_ref, r1_ref, r2_ref, o_ref):
        acc = own_ref[0].astype(F32) + r0_ref[0].astype(F32)
        o_ref[...] = acc + r1_ref[0].astype(F32) + r2_ref[0].astype(F32)

    def got(j):
        return pl.BlockSpec((1, tr, cols), lambda i, s_ref: (j, i, 0))

    return pl.pallas_call(
        body, name=name,
        grid_spec=pltpu.PrefetchScalarGridSpec(
            num_scalar_prefetch=1, grid=(nb,),
            in_specs=[pl.BlockSpec((1, tr, cols), lambda i, s_ref: (s_ref[0], i, 0)), got(0), got(1), got(2)],
            out_specs=pl.BlockSpec((tr, cols), lambda i, s_ref: (s_ref[1] * nb + i, 0))),
        out_shape=jax.ShapeDtypeStruct((2 * h, cols), F32),
        compiler_params=_params(("parallel",)),
    )(jnp.stack([chip, core]).astype(jnp.int32), chip_sums, from_chips, from_chips, from_chips)


def sum_device_blocks(name, g):
    n = g.shape[1]

    def body(g_ref, o_ref):
        acc = g_ref[0:8, :]
        for d in range(1, N_DEV):
            acc = acc + g_ref[8 * d:8 * (d + 1), :]
        o_ref[...] = acc

    return pl.pallas_call(body, name=name, out_shape=jax.ShapeDtypeStruct((8, n), F32),
                          compiler_params=pltpu.CompilerParams(vmem_limit_bytes=VMEM_LIMIT))(g)


class LayerWeights(NamedTuple):
    norm1_g: jax.Array
    q_g: jax.Array
    k_g: jax.Array
    lg: jax.Array
    ret_g: jax.Array
    gate_up: jax.Array
    gate_b: jax.Array
    gla_g: jax.Array
    norm2_g: jax.Array
    conv_w: jax.Array
    conv_b: jax.Array


def _mod(mods, k):
    return mods[:, k:k + 1, :]


def out_view(l, tb):
    rows = D_MODEL // N_CHIPS
    if tb:
        return BView(n=D_MODEL, k=D_MODEL, tn=rows, tk=D_MODEL, index_map=lambda i, j, kk: (j, l, kk))
    chips = tuple(functools.partial(lambda i, j, kk, q: (q, l, j), q=q) for q in range(N_CHIPS))
    return BView(n=D_MODEL, k=D_MODEL, tn=1024, tk=rows, index_map=None, part_maps=chips)


def down_view(l, f, tb):
    rows = f // N_CHIPS
    if tb:
        return BView(n=f, k=D_MODEL, tn=rows, tk=D_MODEL, index_map=lambda i, j, kk: (j, l, kk))
    chips = tuple(functools.partial(lambda i, j, kk, q: (q, l, j), q=q) for q in range(N_CHIPS))
    return BView(n=D_MODEL, k=f, tn=512, tk=rows, index_map=None, part_maps=chips)


def up_view(l, f, part=None):
    cols = 2 * f // N_CHIPS
    tc = _pick(cols, (1408, 1024, 512, 256))
    nbc = cols // tc
    if part is None:
        return BView(n=2 * f, k=D_MODEL, tn=tc, tk=D_MODEL, index_map=lambda i, j, kk: (j // nbc, l, j % nbc))
    nnb = D_MODEL // 512
    tiles = tuple(functools.partial(lambda i, j, kk, p: (2 * part + p // nbc, l * nnb + j, p % nbc), p=p) for p in range(2 * nbc))
    return BView(n=D_MODEL, k=f, tn=512, tk=tc, index_map=None, part_maps=tiles)


def up_grad_view(f, part, into):
    cols = f // 2
    tn = _pick(cols, (1408, 1024, 512, 256))
    nbc = cols // tn
    return OView((N_CHIPS, D_MODEL, cols), lambda i, j, kk: (2 * part + j // nbc, i, j % nbc), tn, into)


def ada_view(l, n_ada, tb):
    if tb:
        return BView(n=D_MODEL, k=n_ada, tn=1024, tk=n_ada, index_map=lambda i, j, kk: (l, j, 0))
    return BView(n=n_ada, k=D_MODEL, tn=1024, tk=D_MODEL, index_map=lambda i, j, kk: (l, 0, j))


def _prep_args(z, zg, cos, sin, w):
    rows = [Row(z, Z_AV, 0), Row(z, 512, Z_RQ // 512), Row(z, 512, Z_RK // 512), Row(z, 256, Z_GQ // 256),
            Row(zg, LANES, 0), Row(cos, HEAD_DIM, 0, False), Row(sin, HEAD_DIM, 0, False)]
    return rows, [Par(w.q_g), Par(w.k_g), Par(w.gate_up), Par(w.gate_b)]


def _post_args(o_att, o_ret, o_gla, z, w):
    rows = [Row(o_att, 1024), Row(o_ret, 512, 0), Row(o_ret, 512, 1, False), Row(o_gla, 512, 0), Row(o_gla, 512, 1, False),
            Row(z, 512, Z_RG // 512), Row(z, 512, Z_GR // 512)]
    return rows, [Par(w.ret_g), Par(w.gla_g)]


def layer_fwd(l, xs, mods, w, fetch, cos, sin, n_lat, n_out):
    t, d = xs.shape
    tag = f"l{l}_"
    nm1 = [Par(w.norm1_g), Par(_mod(mods, 0), True), Par(_mod(mods, 1), True)]
    (h,) = row_map(tag + "norm1", normmod_tile, [Row(xs, d)], nm1, [(d, BF16)], t, n_lat)
    (w_main, w_gate), started = fetch("w_in", h)
    z = matmul(tag + "in_proj", h, w_main, after=started)
    zg = matmul(tag + "gate_proj", h, w_gate)
    rows, pars = _prep_args(z, zg, cos, sin, w)
    (p,) = row_map(tag + "prep", prep_tile, rows, pars, [(P_W, F32)], t, n_lat)
    o_att, lse = attn_fwd(p, z, n_lat)
    o_ret, s_ret = ret_fwd(p, z, w.lg, n_lat)
    o_gla, s_gla = gla_fwd(p, z, n_lat)
    rows, pars = _post_args(o_att, o_ret, o_gla, z, w)
    (m,) = row_map(tag + "post", post_tile, rows, pars, [(d, BF16)], t, n_lat)
    m = m[:n_out]
    g_out, started = fetch("w_out", m)
    y = matmul(tag + "out_proj", m, g_out, view=out_view(0, False), after=started)
    rn = [Par(_mod(mods, 2), True), Par(w.norm2_g), Par(_mod(mods, 3), True), Par(_mod(mods, 4), True)]
    x1, h2 = row_map(tag + "resid1_norm2", resid_norm_tile, [Row(xs, d), Row(y, d)], rn, [(d, F32), (d, BF16)], n_out, n_lat)
    f = w.conv_b.shape[1]
    g_up, started = fetch("w_up", h2)
    u = matmul(tag + "up_proj", h2, g_up, view=up_view(0, f), after=started, out_dtype=BF16)
    g = convglu(tag + "convglu", u, w.conv_w, w.conv_b, n_lat)
    g_down, started = fetch("w_down", g)
    yd = matmul(tag + "down_proj", g, g_down, view=down_view(0, f, False), after=started)
    (x2,) = row_map(tag + "resid2", resid_tile, [Row(x1, d), Row(yd, d)], [Par(_mod(mods, 5), True)], [(d, F32)], n_out, n_lat)
    saved = dict(xs=xs, h=h, z=z, zg=zg, p=p, o_att=o_att, lse=lse, o_ret=o_ret, s_ret=s_ret, o_gla=o_gla, s_gla=s_gla,
                 m=m, y=y, x1=x1, h2=h2, u=u, g=g, yd=yd, w_main=w_main, w_gate=w_gate, g_out=g_out, g_up=g_up, g_down=g_down)
    return x2, saved


def _sum_dirs(a):
    w = a.shape[1] // 2
    return a[:, :w] + a[:, w:]


def layer_bwd(l, dx2, s, mods, w, cos, sin, n_lat, grad_ready):
    (t, d), n_out = s["xs"].shape, dx2.shape[0]
    tag = f"l{l}_b_"

    def all_rows(a):
        return a if n_out == t else jnp.pad(a, ((0, t - n_out), (0, 0)))

    dyd, dgate5 = row_vjp(tag + "resid2", gated_tile, [Row(s["yd"], d)], [Par(_mod(mods, 5), True)], [dx2], n_out, n_lat,
                          row_grad_dtype=BF16)
    f = w.conv_b.shape[1]
    dg = matmul(tag + "down_dx", dyd, s["g_down"], tb=True, view=down_view(0, f, True))
    dw_down = matmul(tag + "down_dw", s["g"], dyd, ta=True, out_dtype=BF16)
    da, dv, dcw, dcb = convglu_bwd(tag + "convglu", s["u"], w.conv_w, w.conv_b, dg, n_lat)
    dh2 = matmul(tag + "up_dx_gate", da, s["g_up"], tb=True, view=up_view(0, f, 0))
    dh2 = matmul(tag + "up_dx_value", dv, s["g_up"], tb=True, view=up_view(0, f, 1), add=dh2)
    dw_up = matmul(tag + "up_dw_gate", s["h2"], da, ta=True, out_dtype=BF16, o_view=up_grad_view(f, 0, None))
    dw_up = matmul(tag + "up_dw_value", s["h2"], dv, ta=True, out_dtype=BF16, o_view=up_grad_view(f, 1, dw_up))
    started = grad_ready("ffn", dict(w_up=dw_up, w_down=dw_down))
    rn = [Par(_mod(mods, 2), True), Par(w.norm2_g), Par(_mod(mods, 3), True), Par(_mod(mods, 4), True)]
    dx1, dy, dgate2, dg2, dshift3, dscale4 = row_vjp(
        tag + "resid1_norm2", resid_norm_tile, [Row(s["xs"], d), Row(s["y"], d)], rn, [dx2, dh2], n_out, n_lat,
        row_grad_dtype=(F32, BF16), after=started)
    if n_out < t:
        dgate5, dshift3, dscale4, dgate2 = [g.at[1].set(0.0) for g in (dgate5, dshift3, dscale4, dgate2)]
    dm = matmul(tag + "out_dx", dy, s["g_out"], tb=True, view=out_view(0, True))
    dw_out = matmul(tag + "out_dw", s["m"], dy, ta=True, out_dtype=BF16)
    rows, pars = _post_args(s["o_att"], s["o_ret"], s["o_gla"], s["z"], w)
    started = grad_ready("w_out", dict(w_out=dw_out))
    do_att, do_ret, do_gla, d_rg, d_gr, d_ret_g, d_gla_g = row_vjp(tag + "post", post_tile, rows, pars, [dm], n_out, n_lat, after=started)
    do_att, do_ret, do_gla, d_rg, d_gr, dx1 = [all_rows(a) for a in (do_att, do_ret, do_gla, d_rg, d_gr, dx1)]
    dq_a, dk_a, dv_a = attn_bwd(s["p"], s["z"], s["o_att"], s["lse"], do_att, n_lat)
    dq_r, dk_r, dv_r, dlg = ret_bwd(s["p"], s["z"], w.lg, s["s_ret"], do_ret, n_lat)
    dq_g, dk_g, dv_g, dla = gla_bwd(s["p"], s["z"], s["s_gla"], do_gla, n_lat)
    dp = jnp.concatenate([dq_a, dk_a, _sum_dirs(dq_g), _sum_dirs(dq_r), _sum_dirs(dk_r), dla], axis=1)
    rows, pars = _prep_args(s["z"], s["zg"], cos, sin, w)
    d_zqk, d_zrq, d_zrk, d_zgq, dzg, d_qg, d_kg, d_up, d_gb = row_vjp(tag + "prep", prep_tile, rows, pars, [dp], t, n_lat)
    dz = jnp.concatenate([d_zqk, dv_a, d_zrq, d_zrk, _sum_dirs(dv_r), d_rg, d_zgq, _sum_dirs(dk_g), _sum_dirs(dv_g), d_gr], axis=1)
    dz, dzg = dz.astype(BF16), dzg.astype(BF16)
    dh_gate = matmul(tag + "gate_dx", dzg, s["w_gate"], tb=True)
    dh = matmul(tag + "in_dx", dz, s["w_main"], tb=True, add=dh_gate)
    dw_main = matmul(tag + "in_dw", s["h"], dz, ta=True, out_dtype=BF16)
    dw_gate = matmul(tag + "gate_dw", s["h"], dzg, ta=True, out_dtype=BF16)
    started = grad_ready("w_in", dict(w_main=dw_main, w_gate=dw_gate))
    nm1 = [Par(w.norm1_g), Par(_mod(mods, 0), True), Par(_mod(mods, 1), True)]
    dx, dg1, dshift0, dscale1 = row_vjp(tag + "norm1", normmod_tile, [Row(s["xs"], d)], nm1, [dh], t, n_lat,
                                        add_to_first=dx1, after=started)
    dmods = jnp.concatenate([dshift0, dscale1, dgate2, dshift3, dscale4, dgate5], axis=1)
    grads = dict(w_main=dw_main, w_gate=dw_gate, w_out=dw_out, w_up=dw_up, w_down=dw_down, norm1_g=dg1, q_g=d_qg, k_g=d_kg,
                 lg=dlg, ret_g=d_ret_g, gate_up=d_up, gate_b=d_gb, gla_g=d_gla_g, norm2_g=dg2, conv_w=dcw, conv_b=dcb)
    return dx, dmods, grads


def rope_tables(n_lat, n_ctx):
    rows = n_lat // GRID_W
    row = jnp.repeat(jnp.arange(rows, dtype=F32), GRID_W)
    col = jnp.tile(jnp.arange(GRID_W, dtype=F32), rows)
    n_freq = HEAD_DIM // 4
    inv_freq = ROPE_THETA ** (-jnp.arange(n_freq, dtype=F32) / n_freq)
    ang = jnp.concatenate([row[:, None] * inv_freq, col[:, None] * inv_freq], axis=-1)
    cos, sin = jnp.cos(ang), jnp.sin(ang)
    cos = jnp.concatenate([jnp.concatenate([cos, cos], axis=1), jnp.ones((n_ctx, HEAD_DIM), F32)], axis=0)
    sin = jnp.concatenate([jnp.concatenate([-sin, sin], axis=1), jnp.zeros((n_ctx, HEAD_DIM), F32)], axis=0)
    return cos, sin


def local_step(xs, target, mods, weights, fetch, final_g, n_lat, grad_ready):
    t, d = xs.shape
    cos, sin = rope_tables(n_lat, t - n_lat)
    saved = []
    h = xs
    for l, w in enumerate(weights):
        n_out = t if l + 1 < len(weights) else n_lat
        h, s = layer_fwd(l, h, mods[l], w, functools.partial(fetch, l), cos, sin, n_lat, n_out)
        saved.append(s)
    loss, dx, dgf = final_loss(h, target, final_g, n_lat)
    dmods, grads = [None] * len(weights), [None] * len(weights)
    for l in reversed(range(len(weights))):
        dx, dmods[l], grads[l] = layer_bwd(l, dx, saved[l], mods[l], weights[l], cos, sin, n_lat, functools.partial(grad_ready, l))
    return loss, dx, dmods, grads, dgf


WEIGHT_NAMES = ("c_ctx", "ada_w", "ada_b", "norm1_g", "w_in", "q_norm_g", "k_norm_g", "ret_log_decay", "ret_norm_g",
                "gla_gate_up", "gla_gate_b", "gla_norm_g", "w_out", "norm2_g", "w_up", "conv_w", "conv_b", "w_down", "final_norm_g")
PACK_QUANTUM = 8 * LANES


def _pack(arrays):
    flat = jnp.concatenate([a.reshape(-1).astype(F32) for a in arrays])
    n = -(-flat.shape[0] // PACK_QUANTUM) * PACK_QUANTUM
    return jnp.pad(flat, (0, n - flat.shape[0])).reshape(8, n // 8)


def _unpack(flat2d, shapes):
    out, at = [], 0
    for s in shapes:
        size = int(np.prod(s))
        out.append(flat2d[:, at:at + size].reshape((flat2d.shape[0],) + tuple(s)))
        at += size
    return out


def _per_device(gathered):
    return gathered.reshape(N_DEV, -1)


def _from_chips(per_device, axis):
    chips = per_device[0::2]
    moved = jnp.moveaxis(chips, 0, axis)
    shape = moved.shape
    return moved.reshape(shape[:axis] + (shape[axis] * shape[axis + 1],) + shape[axis + 2:])


def kernel(x, c, ctx, c_ctx, ada_w, ada_b, norm1_g, w_in, q_norm_g, k_norm_g, ret_log_decay, ret_norm_g, gla_gate_up, gla_gate_b, gla_norm_g, w_out, norm2_g, w_up, conv_w, conv_b, w_down, final_norm_g, loss_target, m_c_ctx, m_ada_w, m_ada_b, m_norm1_g, m_w_in, m_q_norm_g, m_k_norm_g, m_ret_log_decay, m_ret_norm_g, m_gla_gate_up, m_gla_gate_b, m_gla_norm_g, m_w_out, m_norm2_g, m_w_up, m_conv_w, m_conv_b, m_w_down, m_final_norm_g, v_c_ctx, v_ada_w, v_ada_b, v_norm1_g, v_w_in, v_q_norm_g, v_k_norm_g, v_ret_log_decay, v_ret_norm_g, v_gla_gate_up, v_gla_gate_b, v_gla_norm_g, v_w_out, v_norm2_g, v_w_up, v_conv_w, v_conv_b, v_w_down, v_final_norm_g):
    weights = dict(zip(WEIGHT_NAMES, (c_ctx, ada_w, ada_b, norm1_g, w_in, q_norm_g, k_norm_g, ret_log_decay, ret_norm_g,
                                      gla_gate_up, gla_gate_b, gla_norm_g, w_out, norm2_g, w_up, conv_w, conv_b, w_down, final_norm_g)))
    mom_m = dict(zip(WEIGHT_NAMES, (m_c_ctx, m_ada_w, m_ada_b, m_norm1_g, m_w_in, m_q_norm_g, m_k_norm_g, m_ret_log_decay, m_ret_norm_g,
                                    m_gla_gate_up, m_gla_gate_b, m_gla_norm_g, m_w_out, m_norm2_g, m_w_up, m_conv_w, m_conv_b, m_w_down, m_final_norm_g)))
    mom_v = dict(zip(WEIGHT_NAMES, (v_c_ctx, v_ada_w, v_ada_b, v_norm1_g, v_w_in, v_q_norm_g, v_k_norm_g, v_ret_log_decay, v_ret_norm_g,
                                    v_gla_gate_up, v_gla_gate_b, v_gla_norm_g, v_w_out, v_norm2_g, v_w_up, v_conv_w, v_conv_b, v_w_down, v_final_norm_g)))
    depth, d = norm1_g.shape
    assert d == D_MODEL and x.shape[0] == 1
    n_lat, n_ctx, f = x.shape[1], ctx.shape[1], conv_b.shape[1]
    assert n_lat % ROW_TILE == 0 and n_ctx % ROW_TILE == 0 and f % FFN_COL_TILE == 0 and f % N_CHIPS == 0
    n_in = w_in.shape[2]
    n_ada = ada_w.shape[2]
    xi, yi, ci = lax.axis_index("x"), lax.axis_index("y"), lax.axis_index("c")
    chip = 2 * xi + yi
    dev = 2 * chip + ci

    big = ("w_in", "w_out", "w_up", "w_down")
    order = [(l, name) for l in range(depth) for name in big]
    shards = [weights[name][l].astype(BF16) for l, name in order]
    passing = {}

    def pass_on(k, after):
        tag = "{1}{0}".format(*order[k])
        own, land = split_wait("gather_wait_" + tag, GATHER, in_flight[k], after)
        (moving,), started = split_start("gather_pass_" + tag, PASS_ON, [land])
        passing[k] = (own, moving)
        return started

    def fetch(l, name, after):
        k = order.index((l, name))
        if k == 0:
            pass_on(0, after)
        own, moving = passing.pop(k)
        (land,) = split_wait(f"gather_pass_wait_{name}{l}", PASS_ON, moving, after)
        started = pass_on(k + 1, after) if k + 1 < len(order) else None
        land = lax.dynamic_update_slice_in_dim(land, own[None], chip, axis=0)
        if name != "w_in":
            return land, started
        cols = jnp.concatenate([land[q] for q in range(N_CHIPS)], axis=1)
        return (cols[:, :N_MAIN], jnp.pad(cols[:, N_MAIN:], ((0, 0), (0, LANES - N_GATE)))), started

    small_shapes = [c.shape[1:], conv_w.shape, gla_gate_up.shape, gla_gate_b.shape]
    got = _per_device(all_gather_small("gather_small", _pack([c, conv_w, gla_gate_up, gla_gate_b])))
    c_all, conv_w_sh, gate_up_sh, gate_b_sh = _unpack(got, small_shapes)
    conv_w_full = _from_chips(conv_w_sh, 2)
    gate_up_full = _from_chips(gate_up_sh, 3)
    gate_b_full = _from_chips(gate_b_sh, 2)

    act = jnp.zeros((16, d), F32).at[0:N_DEV].set(jax.nn.silu(c_all)).at[N_DEV].set(jax.nn.silu(c_ctx))
    mod_sh = jnp.stack([matmul(f"ada_fwd{l}", act, ada_w, view=ada_view(l, n_ada, False)) for l in range(depth)])
    got = _per_device(all_gather_small("gather_mods", _pack([mod_sh])))
    (mod_sh_all,) = _unpack(got, [mod_sh.shape])
    mod_full = _from_chips(mod_sh_all, 2) + ada_b[:, None, :]
    mod_mine = lax.dynamic_index_in_dim(mod_full, dev, axis=1, keepdims=False)
    mods = [jnp.stack([mod_mine[l].reshape(N_MOD, d), mod_full[l, N_DEV].reshape(N_MOD, d)]) for l in range(depth)]
    in_flight, token = split_start("gather_start", GATHER, shards, [(N_CHIPS,) + s.shape for s in shards], after=mod_full)

    layer_w = []
    for l in range(depth):
        up = jnp.zeros((2, LANES, GLA_HEADS * GLA_DK), F32)
        up = up.at[0, 0:GLA_RANK].set(gate_up_full[l, 0]).at[1, GLA_RANK:2 * GLA_RANK].set(gate_up_full[l, 1])
        layer_w.append(LayerWeights(
            norm1_g=norm1_g[l].reshape(1, 1, d), q_g=q_norm_g[l].reshape(1, 1, HEAD_DIM), k_g=k_norm_g[l].reshape(1, 1, HEAD_DIM),
            lg=ret_log_decay[l].reshape(2, RET_HEADS, 1, 1), ret_g=ret_norm_g[l].reshape(1, 1, HEAD_DIM),
            gate_up=up.reshape(1, 2 * LANES, -1), gate_b=gate_b_full[l].reshape(1, 2, -1), gla_g=gla_norm_g[l].reshape(1, 1, HEAD_DIM),
            norm2_g=norm2_g[l].reshape(1, 1, d), conv_w=conv_w_full[l], conv_b=conv_b[l].reshape(1, f)))

    def pieces_of(name, g):
        if name == "w_in":
            full_cols = jnp.concatenate([g["w_main"], g["w_gate"][:, :N_GATE]], axis=1)
            return jnp.stack([full_cols[:, q * n_in:(q + 1) * n_in] for q in range(N_CHIPS)])
        if name == "w_up":
            return g["w_up"]
        return g[name].reshape(N_CHIPS, -1, d)

    groups = {"ffn": ("w_up", "w_down"), "w_out": ("w_out",), "w_in": ("w_in",)}
    reducing = {}
    to_sibling = []

    def sibling_arrived(after):
        started = None
        while to_sibling:
            l, group, in_flight_halves = to_sibling.pop(0)
            sums = []
            for name, halves in zip(groups[group], in_flight_halves):
                pieces, from_sibling = split_wait(f"rs_sibling_wait_{name}{l}", SIBLING_HALF, halves, after)
                sums.append(add_sibling_half(f"rs_add_sibling_{name}{l}", pieces, from_sibling, ci))
            in_flight_sums, token = split_start(f"rs_start_{group}{l}", SCATTER, sums, [(3,) + s.shape[1:] for s in sums])
            reducing.update({(l, name): grp for name, grp in zip(groups[group], in_flight_sums)})
            started = token if started is None else started + token
        return started

    def grad_ready(l, group, g):
        pieces = [pieces_of(name, g) for name in groups[group]]
        before = None if (l, group) == (0, "w_in") else sibling_arrived(pieces[0])
        in_flight_halves, started = split_start(f"rs_sibling_{group}{l}", SIBLING_HALF, pieces,
                                                [(N_CHIPS, pc.shape[1] // 2, pc.shape[2]) for pc in pieces])
        to_sibling.append((l, group, in_flight_halves))
        return started if before is None else started + before

    xs = jnp.concatenate([x[0], ctx[0]], axis=0) + token[0, 0]
    loss, dx, dmods, grads, dgf = local_step(xs, loss_target[0], mods, layer_w, fetch, final_norm_g.reshape(1, d), n_lat, grad_ready)

    def gate_up_grad(g):
        return jnp.stack([g[0, 0:GLA_RANK], g[0, LANES + GLA_RANK:LANES + 2 * GLA_RANK]])

    per_layer = [[dmods[l][0], dmods[l][1], grads[l]["norm1_g"], grads[l]["norm2_g"], grads[l]["q_g"], grads[l]["k_g"],
                  grads[l]["ret_g"], grads[l]["gla_g"], grads[l]["lg"], gate_up_grad(grads[l]["gate_up"]), grads[l]["gate_b"],
                  grads[l]["conv_w"], grads[l]["conv_b"]] for l in range(depth)]
    layer_shapes = [(N_MOD * d,), (N_MOD * d,), (d,), (d,), (HEAD_DIM,), (HEAD_DIM,), (HEAD_DIM,), (HEAD_DIM,), (2, RET_HEADS),
                    (2, GLA_RANK, GLA_HEADS * GLA_DK), (2, GLA_HEADS * GLA_DK), (3, f), (f,)]
    packed = _pack([a for lay in per_layer for a in lay] + [dgf, loss[0, 0:1]])
    gathered = all_gather_small("gather_small_grads", packed)
    every = _unpack(_per_device(gathered), layer_shapes * depth + [(d,), (1,)])
    total = _unpack(sum_device_blocks("sum_small_grads", gathered).reshape(1, -1), layer_shapes * depth + [(d,), (1,)])
    nl = len(layer_shapes)

    def tot(l, k):
        return total[l * nl + k][0]

    out = {"norm1_g": jnp.stack([tot(l, 2) for l in range(depth)]), "norm2_g": jnp.stack([tot(l, 3) for l in range(depth)]),
           "q_norm_g": jnp.stack([tot(l, 4) for l in range(depth)]), "k_norm_g": jnp.stack([tot(l, 5) for l in range(depth)]),
           "ret_norm_g": jnp.stack([tot(l, 6) for l in range(depth)]), "gla_norm_g": jnp.stack([tot(l, 7) for l in range(depth)]),
           "ret_log_decay": jnp.stack([tot(l, 8) for l in range(depth)]),
           "gla_gate_up": lax.dynamic_slice_in_dim(jnp.stack([tot(l, 9) for l in range(depth)]), chip * gla_gate_up.shape[3], gla_gate_up.shape[3], axis=3),
           "gla_gate_b": lax.dynamic_slice_in_dim(jnp.stack([tot(l, 10) for l in range(depth)]), chip * gla_gate_b.shape[2], gla_gate_b.shape[2], axis=2),
           "conv_w": lax.dynamic_slice_in_dim(jnp.stack([tot(l, 11) for l in range(depth)]), chip * conv_w.shape[2], conv_w.shape[2], axis=2),
           "conv_b": jnp.stack([tot(l, 12) for l in range(depth)]),
           "final_norm_g": total[depth * nl][0],
           "ada_b": jnp.stack([tot(l, 0) + tot(l, 1) for l in range(depth)])}
    loss_total = total[depth * nl + 1][0, 0]

    dmod_all = jnp.zeros((depth, 16, N_MOD * d), F32)
    for l in range(depth):
        dmod_all = dmod_all.at[l, 0:N_DEV].set(every[l * nl][:, :]).at[l, N_DEV].set(tot(l, 1))
    dmod_cols = lax.dynamic_slice_in_dim(dmod_all, chip * n_ada, n_ada, axis=2)
    for l in range(depth):
        slab = OView((depth, d, n_ada), functools.partial(lambda i, j, kk, l: (l, i, j), l=l), None, out.get("ada_w"))
        out["ada_w"] = matmul(f"ada_dw{l}", act, dmod_cols[l], ta=True, o_view=slab)
    dact = matmul("ada_dx0", dmod_cols[0], ada_w, tb=True, view=ada_view(0, n_ada, True))
    for l in range(1, depth):
        dact = matmul(f"ada_dx{l}", dmod_cols[l], ada_w, tb=True, view=ada_view(l, n_ada, True), add=dact)
    got = _per_device(all_gather_small("gather_dcctx", _pack([dact[N_DEV]])))
    sibling_arrived(got)
    got = got[0::2, :d]
    dsilu = got[0] + got[1] + got[2] + got[3]
    sig = jax.nn.sigmoid(c_ctx)
    out["c_ctx"] = dsilu * (sig + c_ctx * sig * (1.0 - sig))

    deltas, new_m, new_v = {}, {}, {}

    def update(name):
        out[name] = out[name].reshape(weights[name].shape)
        deltas[name], new_m[name], new_v[name] = adamw("adamw_" + name, weights[name], out[name], mom_m[name], mom_v[name])

    for name in WEIGHT_NAMES:
        if name not in big:
            update(name)
    behind = new_v["ada_w"]
    joining = []

    def joined(after):
        name, in_flight_halves = joining.pop()
        per_layer = [split_wait(f"rs_join_wait_{name}{l}", JOIN, grp, after)[0] for l, grp in enumerate(in_flight_halves)]
        (deltas[name], new_m[name], new_v[name]), out[name] = adamw_layers(
            "adamw_" + name, weights[name], per_layer, mom_m[name], mom_v[name])
        return new_v[name]

    for name in ("w_down", "w_up", "w_out", "w_in"):
        halves = []
        for l in range(depth):
            sums, got = split_wait(f"rs_wait_{name}{l}", SCATTER, reducing[(l, name)], behind)
            halves.append(add_chip_sums(f"rs_add_chips_{name}{l}", sums, got, chip, ci))
        in_flight_halves, _ = split_start("rs_join_" + name, JOIN, halves)
        if joining:
            behind = joined(behind)
        joining.append((name, in_flight_halves))
    joined(behind)
    grad_x = dx[:n_lat].reshape(x.shape)
    return (loss_total, grad_x, *[out[n] for n in WEIGHT_NAMES], *[deltas[n] for n in WEIGHT_NAMES],
            *[new_m[n] for n in WEIGHT_NAMES], *[new_v[n] for n in WEIGHT_NAMES])
```

```python
import functools
from typing import NamedTuple

import numpy as np
import jax
import jax.numpy as jnp
from jax import lax
from jax.experimental import pallas as pl
from jax.experimental.pallas import tpu as pltpu

F32 = jnp.float32
BF16 = jnp.bfloat16

D_MODEL = 2048
HEAD_DIM = 128
ATT_Q_HEADS = 8
ATT_KV_HEADS = 2
ATT_GROUP = ATT_Q_HEADS // ATT_KV_HEADS
RET_HEADS = 4
GLA_HEADS = 4
GLA_DK = 64
GLA_DV = 128
GLA_RANK = 16
GLA_TAU = 16.0
RET_CHUNK = 256
GLA_CHUNK = 128
GRID_W = 64
ROPE_THETA = 10000.0
N_MOD = 6
EPS = 1e-6
N_MAIN = 5120
N_GATE = 2 * GLA_RANK
LANES = 128
ROW_TILE = 256
FFN_COL_TILE = 256
VMEM_LIMIT = 56 * 1024 * 1024

ADAM_LR = 0.001
ADAM_B1 = 0.9
ADAM_B2 = 0.999
ADAM_EPS = 1e-08
ADAM_WD = 0.01
ADAM_STEP = 10

Z_AQ, Z_AK, Z_AV = 0, 1024, 1280
Z_RQ, Z_RK, Z_RV, Z_RG = 1536, 2048, 2560, 3072
Z_GQ, Z_GK, Z_GV, Z_GR = 3584, 3840, 4096, 4608
P_AQ, P_AK, P_GQ, P_RQ, P_RK, P_LA = 0, 1024, 1280, 1536, 2048, 2560
P_W = 3072


def _params(sem=None):
    return pltpu.CompilerParams(dimension_semantics=sem, vmem_limit_bytes=VMEM_LIMIT)


def _pick(n, cands):
    for c in cands:
        if n % c == 0:
            return c
    return n


_NN = (((1,), (0,)), ((), ()))
_NT = (((1,), (1,)), ((), ()))
_TN = (((0,), (0,)), ((), ()))


def _dg(a, b, dims):
    return lax.dot_general(a.astype(BF16), b.astype(BF16), dims, preferred_element_type=F32)


@jax.custom_vjp
def bdot(a, b):
    return _dg(a, b, _NN)


def _bdot_fwd(a, b):
    return _dg(a, b, _NN), (a, b)


def _bdot_bwd(res, ct):
    a, b = res
    return _dg(ct, b, _NT), _dg(a, ct, _TN)


bdot.defvjp(_bdot_fwd, _bdot_bwd)


@jax.custom_vjp
def bdot_nt(a, b):
    return _dg(a, b, _NT)


def _bdot_nt_fwd(a, b):
    return _dg(a, b, _NT), (a, b)


def _bdot_nt_bwd(res, ct):
    a, b = res
    return _dg(ct, b, _NN), _dg(ct, a, _TN)


bdot_nt.defvjp(_bdot_nt_fwd, _bdot_nt_bwd)


@jax.custom_vjp
def bdot_tn(a, b):
    return _dg(a, b, _TN)


def _bdot_tn_fwd(a, b):
    return _dg(a, b, _TN), (a, b)


def _bdot_tn_bwd(res, ct):
    a, b = res
    return _dg(b, ct, _NT), _dg(a, ct, _NN)


bdot_tn.defvjp(_bdot_tn_fwd, _bdot_tn_bwd)


def _split3(x):
    x1 = x.astype(BF16)
    r1 = x - x1.astype(F32)
    x2 = r1.astype(BF16)
    x3 = (r1 - x2.astype(F32)).astype(BF16)
    return x1, x2, x3


def _mask_dot(mask_bf16, x, dims):
    x1, x2, x3 = _split3(x)
    f = lambda t: lax.dot_general(mask_bf16, t, dims, preferred_element_type=F32)
    return f(x1) + f(x2) + f(x3)


@jax.custom_vjp
def mask_cumsum(mask, x):
    return _mask_dot(mask.astype(BF16), x, _NN)


def _mask_cumsum_fwd(mask, x):
    return mask_cumsum(mask, x), mask


def _mask_cumsum_bwd(mask, ct):
    return jnp.zeros_like(mask), _mask_dot(mask.astype(BF16), ct, _TN)


mask_cumsum.defvjp(_mask_cumsum_fwd, _mask_cumsum_bwd)


def _roll(x, shift, axis):
    return pltpu.roll(x, shift % x.shape[axis], axis)


@functools.partial(jax.custom_vjp, nondiff_argnums=(1, 2))
def roll(x, shift, axis):
    return _roll(x, shift, axis)


def _roll_fwd(x, shift, axis):
    return _roll(x, shift, axis), None


def _roll_bwd(shift, axis, _, ct):
    return (_roll(ct, -shift, axis),)


roll.defvjp(_roll_fwd, _roll_bwd)


def rms(x):
    return x * lax.rsqrt(jnp.mean(x * x, axis=-1, keepdims=True) + EPS)


def silu(x):
    return x * (0.5 + 0.5 * jnp.tanh(0.5 * x))


def log_sigmoid(x):
    return jnp.minimum(x, 0.0) - jnp.log(1.0 + jnp.exp(-jnp.abs(x)))


def rope(t, cos, sin):
    return t * cos + roll(t, HEAD_DIM // 2, 1) * sin


def _heads(x, n, width=HEAD_DIM):
    return [x[:, h * width:(h + 1) * width] for h in range(n)]


class Row(NamedTuple):
    arr: jax.Array
    width: int
    idx: int = 0
    diff: bool = True


class Par(NamedTuple):
    arr: jax.Array
    grouped: bool = False
    diff: bool = True


def _row_specs(rows, pars, tm, n_lat_tiles):
    def grp(i):
        return jnp.minimum(i // n_lat_tiles, 1)

    specs = [pl.BlockSpec((tm, r.width), functools.partial(lambda i, k: (i, k), k=r.idx)) for r in rows]
    for p in pars:
        blk = (1,) + p.arr.shape[1:]
        if p.grouped:
            specs.append(pl.BlockSpec(blk, lambda i: (grp(i), 0, 0)))
        else:
            specs.append(pl.BlockSpec(blk, lambda i: (0, 0, 0)))
    return specs


def row_map(name, fn, rows, pars, outs, n_rows, n_lat):
    tm = ROW_TILE
    nr, npar = len(rows), len(pars)

    def body(*refs):
        vals = [r[...] for r in refs[:nr]] + [p[0] for p in refs[nr:nr + npar]]
        res = fn(*vals)
        for o, v in zip(refs[nr + npar:], res):
            o[...] = v.astype(o.dtype)

    return pl.pallas_call(
        body, name=name, grid=(n_rows // tm,),
        in_specs=_row_specs(rows, pars, tm, n_lat // tm),
        out_specs=[pl.BlockSpec((tm, w), lambda i: (i, 0)) for w, _ in outs],
        out_shape=[jax.ShapeDtypeStruct((n_rows, w), dt) for w, dt in outs],
        compiler_params=_params(("arbitrary",)),
    )(*[r.arr for r in rows], *[p.arr for p in pars])


def row_vjp(name, fn, rows, pars, cts, n_rows, n_lat, add_to_first=None, row_grad_dtype=F32, after=None):
    tm = ROW_TILE
    nr, npar, nc = len(rows), len(pars), len(cts)
    n_lat_tiles = n_lat // tm
    args = list(rows) + list(pars)
    diff_pos = [k for k, a in enumerate(args) if a.diff]
    n_add = 0 if add_to_first is None else 1
    n_after = 0 if after is None else 1

    def body(*refs):
        i = pl.program_id(0)
        vals = [r[...] for r in refs[:nr]] + [p[0] for p in refs[nr:nr + npar]]
        ct_vals = tuple(c[...] for c in refs[nr + npar:nr + npar + nc])
        out_refs = refs[nr + npar + nc + n_add + n_after:]

        def g(*dv):
            full = list(vals)
            for k, v in zip(diff_pos, dv):
                full[k] = v
            return tuple(fn(*full))

        _, vjp = jax.vjp(g, *[vals[k] for k in diff_pos])
        grads = vjp(ct_vals)
        for n, (k, o, gr) in enumerate(zip(diff_pos, out_refs, grads)):
            if k < nr:
                o[...] = (gr + refs[nr + npar + nc][...] if (n == 0 and n_add) else gr).astype(o.dtype)
            else:
                first = (i == 0) | (i == n_lat_tiles) if args[k].grouped else (i == 0)

                @pl.when(first)
                def _():
                    o[0] = gr

                @pl.when(jnp.logical_not(first))
                def _():
                    o[0] += gr

    def grp(i):
        return jnp.minimum(i // n_lat_tiles, 1)

    out_specs, out_shape = [], []
    for k in diff_pos:
        a = args[k]
        if k < nr:
            out_specs.append(pl.BlockSpec((tm, a.width), lambda i: (i, 0)))
            dtype = row_grad_dtype[len(out_shape)] if isinstance(row_grad_dtype, tuple) else row_grad_dtype
            out_shape.append(jax.ShapeDtypeStruct((n_rows, a.width), dtype))
        else:
            blk = (1,) + a.arr.shape[1:]
            out_specs.append(pl.BlockSpec(blk, (lambda i: (grp(i), 0, 0)) if a.grouped else (lambda i: (0, 0, 0))))
            out_shape.append(jax.ShapeDtypeStruct(a.arr.shape, F32))
    extra = list(cts) + ([add_to_first] if n_add else [])
    ct_specs = [pl.BlockSpec((tm, c.shape[1]), lambda i: (i, 0)) for c in extra]
    if n_after:
        extra.append(after)
        ct_specs.append(pl.BlockSpec(memory_space=pl.ANY))
    return pl.pallas_call(
        body, name=name, grid=(n_rows // tm,),
        in_specs=_row_specs(rows, pars, tm, n_lat_tiles) + ct_specs,
        out_specs=out_specs, out_shape=out_shape,
        compiler_params=_params(("arbitrary",)),
    )(*[r.arr for r in rows], *[p.arr for p in pars], *extra)


class BView(NamedTuple):
    n: int
    k: int
    tn: int
    tk: int
    index_map: object
    lead: int = 1
    part_maps: tuple = ()


MATMUL_VMEM_BUDGET = 40 * 1024 * 1024


def _matmul_tiles(m, n, k, a_bytes, b_bytes, o_bytes):
    tms = [c for c in (1152, 1024, 768, 512, 256, 128) if m % c == 0] or [m]
    tns = [c for c in (2048, 1408, 1280, 1024, 768, 512, 256, 128) if n % c == 0] or [n]
    tks = [k] + [c for c in (2816, 2304, 2048, 1408, 1024, 512, 256, 128) if k % c == 0 and c < k]
    for tk in tks:
        fits = [(tm * tn, tm, tn) for tm in tms for tn in tns
                if 2 * (tm * tk * a_bytes + tk * tn * b_bytes + tm * tn * o_bytes) + 2 * tm * tn * 4 <= MATMUL_VMEM_BUDGET]
        if fits and (max(fits)[0] >= min(512 * 512, tms[0] * tns[0]) or tk == tks[-1]):
            _, tm, tn = max(fits)
            return tm, tn, tk
    raise ValueError(f"no matmul tiling for {(m, n, k)}")


class OView(NamedTuple):
    shape: tuple
    index_map: object
    tn: int = None
    into: object = None


def matmul(name, a, b, *, ta=False, tb=False, add=None, out_dtype=F32, view=None, o_view=None, after=None):
    m = a.shape[1] if ta else a.shape[0]
    o_bytes = jnp.dtype(out_dtype).itemsize * (1 if add is None else 2)
    if view is None:
        k = a.shape[0] if ta else a.shape[1]
        n = b.shape[0] if tb else b.shape[1]
        assert (b.shape[1] if tb else b.shape[0]) == k, (a.shape, b.shape, ta, tb)
        if o_view is not None and o_view.tn is not None:
            tn = o_view.tn
            tm, _, tk = _matmul_tiles(m, tn, k, a.dtype.itemsize, b.dtype.itemsize, o_bytes)
        else:
            tm, tn, tk = _matmul_tiles(m, n, k, a.dtype.itemsize, b.dtype.itemsize, o_bytes)
    else:
        n, k, tn, tk = view.n, view.k, view.tn, view.tk
        b_maps = view.part_maps or (view.index_map,)
        tm, _, whole = _matmul_tiles(m, tn, tk * len(b_maps), a.dtype.itemsize, b.dtype.itemsize, o_bytes)
        assert whole == tk * len(b_maps) and not (ta and len(b_maps) > 1), (name, tm, whole)
    parts = 1 if view is None else len(b_maps)
    k_step = tk * parts
    nk = k // k_step
    dims = (((0 if ta else 1,), (1 if tb else 0,)), ((), ()))

    def body(a_ref, *rest):
        b_refs, rest = rest[:parts], rest[parts:]
        if parts == 1:
            prod = lax.dot_general(a_ref[...].astype(BF16), b_refs[0][...].astype(BF16), dims, preferred_element_type=F32)
        else:
            prod = sum(lax.dot_general(a_ref[:, p * tk:(p + 1) * tk].astype(BF16), b_refs[p][...].astype(BF16), dims,
                                       preferred_element_type=F32) for p in range(parts))
        if nk == 1:
            o_ref = rest[-1]
            o_ref[...] = (prod if add is None else prod + rest[0][...]).astype(o_ref.dtype)
            return
        o_ref, acc = rest[-2:]
        kk = pl.program_id(2)

        @pl.when(kk == 0)
        def _():
            acc[...] = prod

        @pl.when(kk != 0)
        def _():
            acc[...] += prod

        @pl.when(kk == nk - 1)
        def _():
            r = acc[...]
            if add is not None:
                r = r + rest[0][...]
            o_ref[...] = r.astype(o_ref.dtype)

    if ta:
        a_spec = pl.BlockSpec((k_step, tm), lambda i, j, kk: (kk, i))
    else:
        a_spec = pl.BlockSpec((tm, k_step), lambda i, j, kk: (i, kk))
    b_tile = (tn, tk) if tb else (tk, tn)
    if view is not None:
        b_specs = [pl.BlockSpec((None,) * view.lead + b_tile, index_map) for index_map in b_maps]
    elif tb:
        b_specs = [pl.BlockSpec(b_tile, lambda i, j, kk: (j, kk))]
    else:
        b_specs = [pl.BlockSpec(b_tile, lambda i, j, kk: (kk, j))]
    o_spec = pl.BlockSpec((tm, tn), lambda i, j, kk: (i, j))
    ins = [a] + [b] * parts + ([add] if add is not None else [])
    in_specs = [a_spec] + b_specs + ([o_spec] if add is not None else [])
    out_shape, aliases = jax.ShapeDtypeStruct((m, n), out_dtype), {}
    if o_view is not None:
        assert add is None
        o_spec = pl.BlockSpec((None, tm, tn), o_view.index_map)
        out_shape = jax.ShapeDtypeStruct(o_view.shape, out_dtype)
        if o_view.into is not None:
            aliases = {len(ins): 0}
            ins.append(o_view.into)
            in_specs.append(pl.BlockSpec(memory_space=pl.ANY))
    if after is not None:
        ins.append(after)
        in_specs.append(pl.BlockSpec(memory_space=pl.ANY))
    return pl.pallas_call(
        body, name=name, grid=(m // tm, n // tn, nk),
        in_specs=in_specs, out_specs=o_spec, out_shape=out_shape, input_output_aliases=aliases,
        scratch_shapes=[pltpu.VMEM((tm, tn), F32)] if nk > 1 else [],
        compiler_params=_params(("parallel", "parallel", "arbitrary")),
    )(*ins)


def normmod_tile(x, g, shift, scale):
    return (rms(x) * g * (1.0 + scale) + shift,)


def resid_tile(x, y, gate):
    return (x + gate * y,)


def resid_norm_tile(x, y, gate, g, shift, scale):
    x1 = x + gate * y
    return x1, rms(x1) * g * (1.0 + scale) + shift


def gated_tile(y, gate):
    return (gate * y,)


def prep_tile(z_qk, z_rq, z_rk, z_gq, zg, cos, sin, qg, kg, gate_up, gate_b):
    out = []
    for h, t in enumerate(_heads(z_qk, ATT_Q_HEADS + ATT_KV_HEADS)):
        out.append(rope(rms(t) * (qg if h < ATT_Q_HEADS else kg), cos, sin))
    gq = z_gq * (GLA_DK ** -0.5)
    rq = [rope(t, cos, sin) for t in _heads(z_rq, RET_HEADS)]
    rk = [rope(t * (HEAD_DIM ** -0.5), cos, sin) for t in _heads(z_rk, RET_HEADS)]
    la = [log_sigmoid(bdot(zg, gate_up[d * LANES:(d + 1) * LANES]) + gate_b[d:d + 1]) * (1.0 / GLA_TAU) for d in range(2)]
    return (jnp.concatenate(out + [gq] + rq + rk + la, axis=1),)


def post_tile(o_att, o_ret_f, o_ret_b, o_gla_f, o_gla_b, rg, gr, ret_g, gla_g):
    ret = jnp.concatenate([rms(t) * ret_g for t in _heads(o_ret_f + o_ret_b, RET_HEADS)], axis=1) * silu(rg)
    gla = jnp.concatenate([rms(t) * gla_g for t in _heads(o_gla_f + o_gla_b, GLA_HEADS)], axis=1) * silu(gr)
    return (jnp.concatenate([o_att, ret, gla], axis=1),)


def _convglu_tile(n_lat, a, v, cw, cb):
    t = a.shape[0]
    row = lax.broadcasted_iota(jnp.int32, (t, 1), 0)
    has_prev = ((row != 0) & (row != n_lat)).astype(F32)
    has_next = ((row != n_lat - 1) & (row != t - 1)).astype(F32)
    conv = roll(a, 1, 0) * has_prev * cw[0:1] + a * cw[1:2] + roll(a, -1, 0) * has_next * cw[2:3] + cb
    return silu(conv) * v


def convglu(name, u, cw, cb, n_lat):
    t, f2 = u.shape
    f, tc = f2 // 2, FFN_COL_TILE
    nb = f // tc

    def body(a_ref, v_ref, cw_ref, cb_ref, o_ref):
        o_ref[...] = _convglu_tile(n_lat, a_ref[...].astype(F32), v_ref[...].astype(F32), cw_ref[...], cb_ref[...]).astype(o_ref.dtype)

    return pl.pallas_call(
        body, name=name, grid=(nb,),
        in_specs=[pl.BlockSpec((t, tc), lambda j: (0, j)), pl.BlockSpec((t, tc), lambda j: (0, nb + j)),
                  pl.BlockSpec((3, tc), lambda j: (0, j)), pl.BlockSpec((1, tc), lambda j: (0, j))],
        out_specs=pl.BlockSpec((t, tc), lambda j: (0, j)),
        out_shape=jax.ShapeDtypeStruct((t, f), BF16),
        compiler_params=_params(("parallel",)),
    )(u, u, cw, cb)


def convglu_bwd(name, u, cw, cb, dg, n_lat):
    t, f2 = u.shape
    f, tc = f2 // 2, FFN_COL_TILE
    nb = f // tc

    def body(a_ref, v_ref, cw_ref, cb_ref, dg_ref, da_ref, dv_ref, dcw_ref, dcb_ref):
        _, vjp = jax.vjp(functools.partial(_convglu_tile, n_lat), a_ref[...].astype(F32), v_ref[...].astype(F32),
                         cw_ref[...], cb_ref[...])
        da, dv, dcw_ref[...], dcb_ref[...] = vjp(dg_ref[...])
        da_ref[...], dv_ref[...] = da.astype(BF16), dv.astype(BF16)

    col = pl.BlockSpec((t, tc), lambda j: (0, j))
    return pl.pallas_call(
        body, name=name, grid=(nb,),
        in_specs=[col, pl.BlockSpec((t, tc), lambda j: (0, nb + j)), pl.BlockSpec((3, tc), lambda j: (0, j)),
                  pl.BlockSpec((1, tc), lambda j: (0, j)), col],
        out_specs=[col, col, pl.BlockSpec((3, tc), lambda j: (0, j)), pl.BlockSpec((1, tc), lambda j: (0, j))],
        out_shape=[jax.ShapeDtypeStruct((t, f), BF16), jax.ShapeDtypeStruct((t, f), BF16),
                   jax.ShapeDtypeStruct((3, f), F32), jax.ShapeDtypeStruct((1, f), F32)],
        compiler_params=_params(("parallel",)),
    )(u, u, cw, cb, dg)


def final_loss(x, target, g, n_lat):
    tm = ROW_TILE
    d = x.shape[1]

    def body(x_ref, t_ref, g_ref, loss_ref, dx_ref, dg_ref):
        i = pl.program_id(0)
        tgt = t_ref[...]

        def f(xv, gv):
            e = rms(xv) * gv - tgt
            s = jnp.sum(jnp.sum(e * e, axis=1, keepdims=True), axis=0, keepdims=True)
            return s * (0.5 / d)

        val, vjp = jax.vjp(f, x_ref[...], g_ref[...])
        dx, dgv = vjp(jnp.ones((1, 1), F32))
        dx_ref[...] = dx

        @pl.when(i == 0)
        def _():
            dg_ref[...] = dgv
            loss_ref[...] = jnp.broadcast_to(val, loss_ref.shape)

        @pl.when(i != 0)
        def _():
            dg_ref[...] += dgv
            loss_ref[...] += jnp.broadcast_to(val, loss_ref.shape)

    return pl.pallas_call(
        body, name="final_loss", grid=(n_lat // tm,),
        in_specs=[pl.BlockSpec((tm, d), lambda i: (i, 0)), pl.BlockSpec((tm, d), lambda i: (i, 0)),
                  pl.BlockSpec((1, d), lambda i: (0, 0))],
        out_specs=[pl.BlockSpec((1, LANES), lambda i: (0, 0)), pl.BlockSpec((tm, d), lambda i: (i, 0)),
                   pl.BlockSpec((1, d), lambda i: (0, 0))],
        out_shape=[jax.ShapeDtypeStruct((1, LANES), F32), jax.ShapeDtypeStruct((n_lat, d), F32),
                   jax.ShapeDtypeStruct((1, d), F32)],
        compiler_params=_params(("arbitrary",)),
    )(x, target, g)


ATT_SCALE = HEAD_DIM ** -0.5
_AK_BLK = P_AK // HEAD_DIM
_AV_BLK = Z_AV // HEAD_DIM


def _att_specs(t, tq):
    gw = ATT_GROUP * HEAD_DIM
    q_spec = pl.BlockSpec((tq, gw), lambda kv, i: (i, kv))
    k_spec = pl.BlockSpec((t, HEAD_DIM), lambda kv, i: (0, _AK_BLK + kv))
    v_spec = pl.BlockSpec((t, HEAD_DIM), lambda kv, i: (0, _AV_BLK + kv))
    row_spec = pl.BlockSpec((ATT_GROUP, tq, 1), lambda kv, i: (kv, i, 0))
    return q_spec, k_spec, v_spec, row_spec


def _att_mask(i, t, tq, n_lat):
    col = lax.broadcasted_iota(jnp.int32, (1, t), 1)
    return jnp.where((i >= n_lat // tq) & (col < n_lat), -jnp.inf, 0.0).astype(F32)


def attn_fwd(p, z, n_lat):
    t = p.shape[0]
    tq = ROW_TILE

    def body(q_ref, k_ref, v_ref, o_ref, lse_ref):
        mask = _att_mask(pl.program_id(1), t, tq, n_lat)
        k, v = k_ref[...].astype(BF16), v_ref[...].astype(BF16)
        for g in range(ATT_GROUP):
            cols = slice(g * HEAD_DIM, (g + 1) * HEAD_DIM)
            s = _dg(q_ref[:, cols], k, _NT) * ATT_SCALE + mask
            m = jnp.max(s, axis=1, keepdims=True)
            pr = jnp.exp(s - m)
            l = jnp.sum(pr, axis=1, keepdims=True)
            o_ref[:, cols] = _dg(pr, v, _NN) / l
            lse_ref[g] = m + jnp.log(l)

    q_spec, k_spec, v_spec, row_spec = _att_specs(t, tq)
    return pl.pallas_call(
        body, name="attn_fwd", grid=(ATT_KV_HEADS, t // tq),
        in_specs=[q_spec, k_spec, v_spec], out_specs=[q_spec, row_spec],
        out_shape=[jax.ShapeDtypeStruct((t, ATT_Q_HEADS * HEAD_DIM), F32),
                   jax.ShapeDtypeStruct((ATT_Q_HEADS, t, 1), F32)],
        compiler_params=_params(("parallel", "parallel")),
    )(p, p, z)


def attn_bwd(p, z, o, lse, do, n_lat):
    t = p.shape[0]
    tq = ROW_TILE

    def body(q_ref, k_ref, v_ref, o_ref, do_ref, lse_ref, dq_ref, dk_ref, dv_ref):
        i = pl.program_id(1)

        @pl.when(i == 0)
        def _():
            dk_ref[...] = jnp.zeros_like(dk_ref)
            dv_ref[...] = jnp.zeros_like(dv_ref)

        mask = _att_mask(i, t, tq, n_lat)
        k, v = k_ref[...].astype(BF16), v_ref[...].astype(BF16)
        dk, dv = dk_ref[...], dv_ref[...]
        for g in range(ATT_GROUP):
            cols = slice(g * HEAD_DIM, (g + 1) * HEAD_DIM)
            q, do_g = q_ref[:, cols].astype(BF16), do_ref[:, cols]
            pr = jnp.exp(_dg(q, k, _NT) * ATT_SCALE + mask - lse_ref[g])
            delta = jnp.sum(o_ref[:, cols] * do_g, axis=1, keepdims=True)
            ds = pr * (_dg(do_g, v, _NT) - delta) * ATT_SCALE
            dq_ref[:, cols] = _dg(ds, k, _NN)
            dk = dk + _dg(ds, q, _TN)
            dv = dv + _dg(pr, do_g, _TN)
        dk_ref[...], dv_ref[...] = dk, dv

    q_spec, k_spec, v_spec, row_spec = _att_specs(t, tq)
    kv_out = pl.BlockSpec((t, HEAD_DIM), lambda kv, i: (0, kv))
    return pl.pallas_call(
        body, name="attn_bwd", grid=(ATT_KV_HEADS, t // tq),
        in_specs=[q_spec, k_spec, v_spec, q_spec, q_spec, row_spec],
        out_specs=[q_spec, kv_out, kv_out],
        out_shape=[jax.ShapeDtypeStruct((t, ATT_Q_HEADS * HEAD_DIM), F32),
                   jax.ShapeDtypeStruct((t, ATT_KV_HEADS * HEAD_DIM), F32),
                   jax.ShapeDtypeStruct((t, ATT_KV_HEADS * HEAD_DIM), F32)],
        compiler_params=_params(("parallel", "arbitrary")),
    )(p, p, z, o, do, lse)


_RQ_BLK = P_RQ // HEAD_DIM
_RK_BLK = P_RK // HEAD_DIM
_RV_BLK = Z_RV // HEAD_DIM


def _scan_chunk(direction, step, n_chunks, n_lat_chunks):
    return jnp.where(direction == 0, (step + n_lat_chunks) % n_chunks, n_chunks - 1 - step)


def _ret_geometry(direction):
    c = RET_CHUNK
    i = lax.broadcasted_iota(jnp.int32, (c, c), 0)
    j = lax.broadcasted_iota(jnp.int32, (c, c), 1)
    rel = jnp.where(direction == 0, i - j, j - i).astype(F32)
    r = lax.broadcasted_iota(jnp.int32, (c, 1), 0)
    pos = jnp.where(direction == 0, r, c - 1 - r).astype(F32)
    return rel, pos


def ret_chunk(q, k, v, s, lg, rel, pos):
    c = RET_CHUNK
    causal = rel >= 0
    d_in = jnp.where(causal, jnp.exp(lg * jnp.where(causal, rel, 0.0)), 0.0)
    q_dec = jnp.exp(lg * (pos + 1.0))
    k_dec = jnp.exp(lg * (c - 1.0 - pos))
    c_dec = jnp.exp(lg * c)
    att = bdot_nt(q, k) * d_in
    o = bdot(att, v) + bdot(q * q_dec, s)
    s_new = c_dec * s + bdot_tn(k * k_dec, v)
    return o, s_new


def ret_fwd(p, z, lg, n_lat):
    t = p.shape[0]
    c = RET_CHUNK
    nc, nlc = t // c, n_lat // c

    def body(q_ref, k_ref, v_ref, lg_ref, o_ref, ssave_ref, s_s):
        d, n = pl.program_id(0), pl.program_id(1)

        @pl.when(n == 0)
        def _():
            s_s[...] = jnp.zeros_like(s_s)

        rel, pos = _ret_geometry(d)
        for h in range(RET_HEADS):
            cols = slice(h * HEAD_DIM, (h + 1) * HEAD_DIM)
            ssave_ref[0, h, 0] = s_s[h]
            o, s_new = ret_chunk(q_ref[:, cols], k_ref[:, cols], v_ref[:, cols], s_s[h], lg_ref[0, h], rel, pos)
            o_ref[:, cols] = o
            s_s[h] = s_new

    w = RET_HEADS * HEAD_DIM

    def blk(base):
        return pl.BlockSpec((c, w), lambda d, n: (_scan_chunk(d, n, nc, nlc), base // RET_HEADS))

    return pl.pallas_call(
        body, name="ret_fwd", grid=(2, nc),
        in_specs=[blk(_RQ_BLK), blk(_RK_BLK), blk(_RV_BLK), pl.BlockSpec((1, RET_HEADS, 1, 1), lambda d, n: (d, 0, 0, 0))],
        out_specs=[pl.BlockSpec((c, w), lambda d, n: (_scan_chunk(d, n, nc, nlc), d)),
                   pl.BlockSpec((1, RET_HEADS, 1, HEAD_DIM, HEAD_DIM), lambda d, n: (d, 0, n, 0, 0))],
        out_shape=[jax.ShapeDtypeStruct((t, 2 * w), F32),
                   jax.ShapeDtypeStruct((2, RET_HEADS, nc, HEAD_DIM, HEAD_DIM), F32)],
        scratch_shapes=[pltpu.VMEM((RET_HEADS, HEAD_DIM, HEAD_DIM), F32)],
        compiler_params=_params(("parallel", "arbitrary")),
    )(p, p, z, lg)


def ret_bwd(p, z, lg, states, do, n_lat):
    t = p.shape[0]
    c = RET_CHUNK
    nc, nlc = t // c, n_lat // c

    def body(q_ref, k_ref, v_ref, lg_ref, s_ref, do_ref, dq_ref, dk_ref, dv_ref, dlg_ref, ds_s):
        d, n = pl.program_id(0), pl.program_id(1)

        @pl.when(n == 0)
        def _():
            ds_s[...] = jnp.zeros_like(ds_s)
            dlg_ref[...] = jnp.zeros_like(dlg_ref)

        rel, pos = _ret_geometry(d)
        f = functools.partial(ret_chunk, rel=rel, pos=pos)
        for h in range(RET_HEADS):
            cols = slice(h * HEAD_DIM, (h + 1) * HEAD_DIM)
            _, vjp = jax.vjp(f, q_ref[:, cols], k_ref[:, cols], v_ref[:, cols], s_ref[0, h, 0], lg_ref[0, h])
            dq, dk, dv, ds, dlg = vjp((do_ref[:, cols], ds_s[h]))
            dq_ref[:, cols], dk_ref[:, cols], dv_ref[:, cols] = dq, dk, dv
            ds_s[h] = ds
            dlg_ref[0, h] += dlg

    def chunk_of(d, n):
        return _scan_chunk(d, nc - 1 - n, nc, nlc)

    w = RET_HEADS * HEAD_DIM

    def blk(base):
        return pl.BlockSpec((c, w), lambda d, n: (chunk_of(d, n), base // RET_HEADS))

    out_blk = pl.BlockSpec((c, w), lambda d, n: (chunk_of(d, n), d))
    lg_blk = pl.BlockSpec((1, RET_HEADS, 1, 1), lambda d, n: (d, 0, 0, 0))
    grad_shape = jax.ShapeDtypeStruct((t, 2 * w), F32)
    return pl.pallas_call(
        body, name="ret_bwd", grid=(2, nc),
        in_specs=[blk(_RQ_BLK), blk(_RK_BLK), blk(_RV_BLK), lg_blk,
                  pl.BlockSpec((1, RET_HEADS, 1, HEAD_DIM, HEAD_DIM), lambda d, n: (d, 0, nc - 1 - n, 0, 0)),
                  pl.BlockSpec((c, w), lambda d, n: (chunk_of(d, n), 0))],
        out_specs=[out_blk, out_blk, out_blk, lg_blk],
        out_shape=[grad_shape, grad_shape, grad_shape, jax.ShapeDtypeStruct((2, RET_HEADS, 1, 1), F32)],
        scratch_shapes=[pltpu.VMEM((RET_HEADS, HEAD_DIM, HEAD_DIM), F32)],
        compiler_params=_params(("parallel", "arbitrary")),
    )(p, p, z, lg, states, do)


_GQ_BLK = P_GQ // (GLA_HEADS * GLA_DK)
_GK_BLK = Z_GK // (GLA_HEADS * GLA_DK)
_GV_BLK = Z_GV // (GLA_HEADS * GLA_DV)
_LA_BLK = P_LA // (GLA_HEADS * GLA_DK)


def _gla_mask(direction):
    c = GLA_CHUNK
    i = lax.broadcasted_iota(jnp.int32, (c, c), 0)
    j = lax.broadcasted_iota(jnp.int32, (c, c), 1)
    return (jnp.where(direction == 0, i - j, j - i) >= 0).astype(F32)


def gla_chunk(q, k, v, la, st, mask):
    b = mask_cumsum(mask, la)
    btot = jnp.sum(la, axis=0, keepdims=True)
    half = 0.5 * btot
    qt, kt = q * jnp.exp(b - half), k * jnp.exp(half - b)
    qs, ke = q * jnp.exp(b), k * jnp.exp(btot - b)
    outs, upd = [], []
    for h in range(GLA_HEADS):
        ks = slice(h * GLA_DK, (h + 1) * GLA_DK)
        vh = v[:, h * GLA_DV:(h + 1) * GLA_DV]
        att = bdot_nt(qt[:, ks], kt[:, ks]) * mask
        outs.append(bdot(att, vh) + bdot_nt(qs[:, ks], st[:, ks]))
        upd.append(bdot_tn(vh, ke[:, ks]))
    st_new = st * jnp.exp(btot) + jnp.concatenate(upd, axis=1)
    return jnp.concatenate(outs, axis=1), st_new


def gla_fwd(p, z, n_lat):
    t = p.shape[0]
    c = GLA_CHUNK
    nc, nlc = t // c, n_lat // c
    kw, vw = GLA_HEADS * GLA_DK, GLA_HEADS * GLA_DV

    def body(q_ref, k_ref, v_ref, la_ref, o_ref, ssave_ref, s_s):
        d, n = pl.program_id(0), pl.program_id(1)

        @pl.when(n == 0)
        def _():
            s_s[...] = jnp.zeros_like(s_s)

        ssave_ref[0, 0] = s_s[...]
        o, s_new = gla_chunk(q_ref[...], k_ref[...], v_ref[...], la_ref[...], s_s[...], _gla_mask(d))
        o_ref[...] = o
        s_s[...] = s_new

    def chunk_of(d, n):
        return _scan_chunk(d, n, nc, nlc)

    return pl.pallas_call(
        body, name="gla_fwd", grid=(2, nc),
        in_specs=[pl.BlockSpec((c, kw), lambda d, n: (chunk_of(d, n), _GQ_BLK)),
                  pl.BlockSpec((c, kw), lambda d, n: (chunk_of(d, n), _GK_BLK)),
                  pl.BlockSpec((c, vw), lambda d, n: (chunk_of(d, n), _GV_BLK)),
                  pl.BlockSpec((c, kw), lambda d, n: (chunk_of(d, n), _LA_BLK + d))],
        out_specs=[pl.BlockSpec((c, vw), lambda d, n: (chunk_of(d, n), d)),
                   pl.BlockSpec((1, 1, GLA_DV, kw), lambda d, n: (d, n, 0, 0))],
        out_shape=[jax.ShapeDtypeStruct((t, 2 * vw), F32), jax.ShapeDtypeStruct((2, nc, GLA_DV, kw), F32)],
        scratch_shapes=[pltpu.VMEM((GLA_DV, kw), F32)],
        compiler_params=_params(("parallel", "arbitrary")),
    )(p, z, z, p)


def gla_bwd(p, z, states, do, n_lat):
    t = p.shape[0]
    c = GLA_CHUNK
    nc, nlc = t // c, n_lat // c
    kw, vw = GLA_HEADS * GLA_DK, GLA_HEADS * GLA_DV

    def body(q_ref, k_ref, v_ref, la_ref, s_ref, do_ref, dq_ref, dk_ref, dv_ref, dla_ref, ds_s):
        d, n = pl.program_id(0), pl.program_id(1)

        @pl.when(n == 0)
        def _():
            ds_s[...] = jnp.zeros_like(ds_s)

        f = functools.partial(gla_chunk, mask=_gla_mask(d))
        _, vjp = jax.vjp(f, q_ref[...], k_ref[...], v_ref[...], la_ref[...], s_ref[0, 0])
        dq_ref[...], dk_ref[...], dv_ref[...], dla_ref[...], ds_s[...] = vjp((do_ref[...], ds_s[...]))

    def chunk_of(d, n):
        return _scan_chunk(d, nc - 1 - n, nc, nlc)

    k_out = pl.BlockSpec((c, kw), lambda d, n: (chunk_of(d, n), d))
    return pl.pallas_call(
        body, name="gla_bwd", grid=(2, nc),
        in_specs=[pl.BlockSpec((c, kw), lambda d, n: (chunk_of(d, n), _GQ_BLK)),
                  pl.BlockSpec((c, kw), lambda d, n: (chunk_of(d, n), _GK_BLK)),
                  pl.BlockSpec((c, vw), lambda d, n: (chunk_of(d, n), _GV_BLK)),
                  pl.BlockSpec((c, kw), lambda d, n: (chunk_of(d, n), _LA_BLK + d)),
                  pl.BlockSpec((1, 1, GLA_DV, kw), lambda d, n: (d, nc - 1 - n, 0, 0)),
                  pl.BlockSpec((c, vw), lambda d, n: (chunk_of(d, n), 0))],
        out_specs=[k_out, k_out, pl.BlockSpec((c, vw), lambda d, n: (chunk_of(d, n), d)), k_out],
        out_shape=[jax.ShapeDtypeStruct((t, 2 * kw), F32), jax.ShapeDtypeStruct((t, 2 * kw), F32),
                   jax.ShapeDtypeStruct((t, 2 * vw), F32), jax.ShapeDtypeStruct((t, 2 * kw), F32)],
        scratch_shapes=[pltpu.VMEM((GLA_DV, kw), F32)],
        compiler_params=_params(("parallel", "arbitrary")),
    )(p, z, z, p, states, do)


def _adam_tile(w, g, m, v):
    m = ADAM_B1 * m + (1.0 - ADAM_B1) * g
    v = ADAM_B2 * v + (1.0 - ADAM_B2) * (g * g)
    m_hat = m / (1.0 - ADAM_B1 ** ADAM_STEP)
    v_hat = v / (1.0 - ADAM_B2 ** ADAM_STEP)
    delta = -ADAM_LR * (m_hat / (jnp.sqrt(v_hat) + ADAM_EPS) + ADAM_WD * w)
    return delta, m, v


def adamw(name, w, g, m, v):
    shape = w.shape
    cols = shape[-1] if w.ndim > 1 and shape[-1] >= LANES else int(np.prod(shape))
    rows = int(np.prod(shape)) // cols
    tr = rows
    for cand in (512, 256, 128, 64, 32, 16, 8):
        if rows % cand == 0 and cand * cols * 4 <= (1 << 21):
            tr = cand
            break
    flat = [a.reshape(rows, cols) for a in (w, g, m, v)]

    def body(w_ref, g_ref, m_ref, v_ref, d_ref, mo_ref, vo_ref):
        d_ref[...], mo_ref[...], vo_ref[...] = _adam_tile(w_ref[...], g_ref[...], m_ref[...], v_ref[...])

    spec = pl.BlockSpec((tr, cols), lambda i: (i, 0))
    outs = pl.pallas_call(
        body, name=name, grid=(rows // tr,),
        in_specs=[spec] * 4, out_specs=[spec] * 3,
        out_shape=[jax.ShapeDtypeStruct((rows, cols), F32)] * 3,
        compiler_params=_params(("parallel",)),
    )(*flat)
    return tuple(o.reshape(shape) for o in outs)


def adamw_layers(name, w, grads, m, v):
    depth, rows, cols = w.shape
    tr = _rows_tile(rows, cols)
    nb = rows // tr

    def body(w_ref, m_ref, v_ref, *rest):
        g_refs, (g_ref, d_ref, mo_ref, vo_ref) = rest[:depth], rest[depth:]
        l = pl.program_id(0)
        for k in range(depth):
            @pl.when(l == k)
            def _():
                g = g_refs[k][...]
                g_ref[...] = g
                d_ref[...], mo_ref[...], vo_ref[...] = _adam_tile(w_ref[...], g, m_ref[...], v_ref[...])

    def layer_grad(k):
        return pl.BlockSpec((tr, cols), lambda l, i: (jnp.where(l < k, 0, jnp.where(l == k, i, nb - 1)), 0))

    spec = pl.BlockSpec((tr, cols), lambda l, i: (l * nb + i, 0))
    flat = [a.reshape(depth * rows, cols) for a in (w, m, v)]
    g_all, delta, new_m, new_v = pl.pallas_call(
        body, name=name, grid=(depth, nb),
        in_specs=[spec] * 3 + [layer_grad(k) for k in range(depth)], out_specs=[spec] * 4,
        out_shape=[jax.ShapeDtypeStruct((depth * rows, cols), F32)] * 4,
        compiler_params=_params(("arbitrary", "arbitrary")),
    )(*flat, *grads)
    return tuple(a.reshape(w.shape) for a in (delta, new_m, new_v)), g_all.reshape(w.shape)


MESH = pl.DeviceIdType.MESH
_HBM = pl.BlockSpec(memory_space=pltpu.HBM)
N_CHIPS = 4
N_DEV = 8


def _place():
    x, y, c = lax.axis_index("x"), lax.axis_index("y"), lax.axis_index("c")
    chips = [(1 - x, y), (x, 1 - y), (1 - x, 1 - y)]
    return x, y, c, chips


def _remote(src, dst, send_sem, recv_sem, to):
    return pltpu.make_async_remote_copy(src_ref=src, dst_ref=dst, send_sem=send_sem, recv_sem=recv_sem,
                                        device_id=to, device_id_type=MESH)


def all_gather_small(name, v):
    m_per, n = v.shape

    def body(x_ref, out_ref, send_sems, recv_sems, local_sem):
        x, y, c, chips = _place()
        me, sibling = (x, y, c), (x, y, 1 - c)

        def rows(px, py, pc):
            return out_ref.at[pl.ds((4 * px + 2 * py + pc) * m_per, m_per), :]

        def copy(k, block, to, src=None):
            return _remote(rows(*block) if src is None else src, rows(*block), send_sems.at[k], recv_sems.at[k], to)

        mine = pltpu.make_async_copy(x_ref, rows(*me), local_sem)
        mine.start()
        first = [copy(0, me, sibling, src=x_ref)]
        first += [copy(1 + j, me, (*chip, c), src=x_ref) for j, chip in enumerate(chips)]
        for cp in first:
            cp.start()
        passed = [copy(4 + j, (*chip, c), sibling) for j, chip in enumerate(chips)]
        for j, chip in enumerate(chips):
            copy(1 + j, (*chip, c), me).wait_recv()
            passed[j].start()
        copy(0, sibling, me).wait_recv()
        for j, chip in enumerate(chips):
            copy(4 + j, (*chip, 1 - c), me).wait_recv()
        for cp in first + passed:
            cp.wait_send()
        mine.wait()

    return pl.pallas_call(
        body, name=name,
        out_shape=jax.ShapeDtypeStruct((N_DEV * m_per, n), v.dtype),
        in_specs=[pl.BlockSpec(memory_space=pltpu.VMEM)],
        out_specs=pl.BlockSpec(memory_space=pltpu.VMEM),
        scratch_shapes=[pltpu.SemaphoreType.DMA((7,)), pltpu.SemaphoreType.DMA((7,)), pltpu.SemaphoreType.DMA],
        compiler_params=pltpu.CompilerParams(vmem_limit_bytes=VMEM_LIMIT),
    )(v)


_SEM = pl.BlockSpec(memory_space=pltpu.SEMAPHORE)
_SPLIT_COPY = pltpu.CompilerParams(has_side_effects=pltpu.SideEffectType.DATAFLOW_SIDE_EFFECTING)


class CopyPlan(NamedTuple):
    copies: object
    n: int
    in_place: bool = False


def _gather_copies(x_ref, land_ref, x, y, c, chips):
    half = x_ref.shape[0] // 2
    rows = pl.ds(c * half, half)
    return [(x_ref.at[rows, :], land_ref.at[2 * x + y, rows, :], (*chip, c), land_ref.at[2 * chip[0] + chip[1], rows, :])
            for chip in chips]


def _pass_copies(land_ref, _, x, y, c, chips):
    half = land_ref.shape[1] // 2
    mine, other = pl.ds(c * half, half), pl.ds((1 - c) * half, half)
    return [(land_ref.at[2 * chip[0] + chip[1], mine, :], land_ref.at[2 * chip[0] + chip[1], mine, :], (x, y, 1 - c),
             land_ref.at[2 * chip[0] + chip[1], other, :]) for chip in chips]


def _sibling_half_copies(p_ref, land_ref, x, y, c, chips):
    half = p_ref.shape[1] // 2
    return [(p_ref.at[:, pl.ds((1 - c) * half, half), :], land_ref, (x, y, 1 - c), land_ref)]


def _scatter_copies(s_ref, land_ref, x, y, c, chips):
    return [(s_ref.at[2 * chip[0] + chip[1]], land_ref.at[j], (*chip, c), land_ref.at[j]) for j, chip in enumerate(chips)]


def _join_copies(buf_ref, _, x, y, c, chips):
    half = buf_ref.shape[0] // 2
    mine = buf_ref.at[pl.ds(c * half, half), :]
    return [(mine, mine, (x, y, 1 - c), buf_ref.at[pl.ds((1 - c) * half, half), :])]


GATHER = CopyPlan(_gather_copies, 3)
PASS_ON = CopyPlan(_pass_copies, 3, in_place=True)
SIBLING_HALF = CopyPlan(_sibling_half_copies, 1)
SCATTER = CopyPlan(_scatter_copies, 3)
JOIN = CopyPlan(_join_copies, 1, in_place=True)


def split_start(name, plan, srcs, land_shapes=None, after=None):
    nt = len(srcs)
    arrays = [pltpu.with_memory_space_constraint(s, pltpu.HBM) for s in srcs]
    if not plan.in_place:
        arrays += [pltpu.with_memory_space_constraint(lax.empty(shape, s.dtype), pltpu.HBM) for shape, s in zip(land_shapes, srcs)]
    na = len(arrays)
    behind = [] if after is None else [after]
    n_in = na + len(behind)

    def body(*refs):
        x_refs = refs[:nt]
        land_refs = x_refs if plan.in_place else refs[nt:na]
        send, recv = refs[n_in:n_in + nt], refs[n_in + nt:n_in + 2 * nt]
        x, y, c, chips = _place()
        for t in range(nt):
            for j, (src, dst, to, _) in enumerate(plan.copies(x_refs[t], land_refs[t], x, y, c, chips)):
                _remote(src, dst, send[t].at[j], recv[t].at[j], to).start()
        refs[-1][...] = jnp.zeros_like(refs[-1])

    outs = pl.pallas_call(
        body, name=name,
        out_shape=tuple([pltpu.SemaphoreType.DMA((plan.n,))] * (2 * nt) + [pltpu.HBM(a.shape, a.dtype) for a in arrays]
                        + [jax.ShapeDtypeStruct((8, LANES), F32)]),
        in_specs=[_HBM] * na + [pl.BlockSpec(memory_space=pl.ANY)] * len(behind),
        out_specs=tuple([_SEM] * (2 * nt) + [_HBM] * na + [pl.BlockSpec(memory_space=pltpu.VMEM)]),
        input_output_aliases={i: 2 * nt + i for i in range(na)},
        compiler_params=_SPLIT_COPY,
    )(*arrays, *behind)
    groups = [(outs[t], outs[nt + t]) + tuple(outs[2 * nt + t + k * nt] for k in range(na // nt)) for t in range(nt)]
    return groups, outs[-1]


def split_wait(name, plan, group, after):
    send, recv, *arrays = group
    na = len(arrays)

    def body(*refs):
        x_ref, land_ref = refs[0], refs[na - 1]
        send_sem, recv_sem = refs[na], refs[na + 1]
        x, y, c, chips = _place()
        for j, (s, _, to, arrival) in enumerate(plan.copies(x_ref, land_ref, x, y, c, chips)):
            cp = _remote(s, arrival, send_sem.at[j], recv_sem.at[j], to)
            cp.wait_send()
            cp.wait_recv()

    return pl.pallas_call(
        body, name=name,
        out_shape=tuple(pltpu.HBM(a.shape, a.dtype) for a in arrays),
        in_specs=tuple([_HBM] * na + [_SEM, _SEM, pl.BlockSpec(memory_space=pl.ANY)]), out_specs=tuple([_HBM] * na),
        input_output_aliases={i: i for i in range(na)}, compiler_params=_SPLIT_COPY,
    )(*arrays, send, recv, after)


def _rows_tile(rows, cols):
    for cand in (512, 256, 128, 64, 32, 16):
        if rows % cand == 0 and cand * cols * 4 <= (1 << 21):
            return cand
    return rows


def add_sibling_half(name, pieces, from_sibling, core):
    n, h, cols = from_sibling.shape
    tr = _rows_tile(h, cols // 4)
    nb = h // tr

    def body(c_ref, a_ref, b_ref, o_ref):
        o_ref[...] = (a_ref[...].astype(F32) + b_ref[...].astype(F32)).astype(o_ref.dtype)

    blk = pl.BlockSpec((1, tr, cols), lambda q, i, c_ref: (q, i, 0))
    return pl.pallas_call(
        body, name=name,
        grid_spec=pltpu.PrefetchScalarGridSpec(
            num_scalar_prefetch=1, grid=(n, nb),
            in_specs=[pl.BlockSpec((1, tr, cols), lambda q, i, c_ref: (q, c_ref[0] * nb + i, 0)), blk], out_specs=blk),
        out_shape=jax.ShapeDtypeStruct((n, h, cols), BF16),
        compiler_params=_params(("parallel", "parallel")),
    )(core.reshape(1).astype(jnp.int32), pieces, from_sibling)


def add_chip_sums(name, chip_sums, from_chips, chip, core):
    _, h, cols = chip_sums.shape
    tr = _rows_tile(h, cols)
    nb = h // tr

    def body(s_ref, own_ref, r0_ref, r1_ref, r2_ref, o_ref):
        acc = own_ref[0].astype(F32) + r0_ref[0].astype(F32)
        o_ref[...] = acc + r1_ref[0].astype(F32) + r2_ref[0].astype(F32)

    def got(j):
        return pl.BlockSpec((1, tr, cols), lambda i, s_ref: (j, i, 0))

    return pl.pallas_call(
        body, name=name,
        grid_spec=pltpu.PrefetchScalarGridSpec(
            num_scalar_prefetch=1, grid=(nb,),
            in_specs=[pl.BlockSpec((1, tr, cols), lambda i, s_ref: (s_ref[0], i, 0)), got(0), got(1), got(2)],
            out_specs=pl.BlockSpec((tr, cols), lambda i, s_ref: (s_ref[1] * nb + i, 0))),
        out_shape=jax.ShapeDtypeStruct((2 * h, cols), F32),
        compiler_params=_params(("parallel",)),
    )(jnp.stack([chip, core]).astype(jnp.int32), chip_sums, from_chips, from_chips, from_chips)


def sum_device_blocks(name, g):
    n = g.shape[1]

    def body(g_ref, o_ref):
        acc = g_ref[0:8, :]
        for d in range(1, N_DEV):
            acc = acc + g_ref[8 * d:8 * (d + 1), :]
        o_ref[...] = acc

    return pl.pallas_call(body, name=name, out_shape=jax.ShapeDtypeStruct((8, n), F32),
                          compiler_params=pltpu.CompilerParams(vmem_limit_bytes=VMEM_LIMIT))(g)


class LayerWeights(NamedTuple):
    norm1_g: jax.Array
    q_g: jax.Array
    k_g: jax.Array
    lg: jax.Array
    ret_g: jax.Array
    gate_up: jax.Array
    gate_b: jax.Array
    gla_g: jax.Array
    norm2_g: jax.Array
    conv_w: jax.Array
    conv_b: jax.Array


def _mod(mods, k):
    return mods[:, k:k + 1, :]


def out_view(l, tb):
    rows = D_MODEL // N_CHIPS
    if tb:
        return BView(n=D_MODEL, k=D_MODEL, tn=rows, tk=D_MODEL, index_map=lambda i, j, kk: (j, l, kk))
    chips = tuple(functools.partial(lambda i, j, kk, q: (q, l, j), q=q) for q in range(N_CHIPS))
    return BView(n=D_MODEL, k=D_MODEL, tn=1024, tk=rows, index_map=None, part_maps=chips)


def down_view(l, f, tb):
    rows = f // N_CHIPS
    if tb:
        return BView(n=f, k=D_MODEL, tn=rows, tk=D_MODEL, index_map=lambda i, j, kk: (j, l, kk))
    chips = tuple(functools.partial(lambda i, j, kk, q: (q, l, j), q=q) for q in range(N_CHIPS))
    return BView(n=D_MODEL, k=f, tn=512, tk=rows, index_map=None, part_maps=chips)


def up_view(l, f, part=None):
    cols = 2 * f // N_CHIPS
    tc = _pick(cols, (1408, 1024, 512, 256))
    nbc = cols // tc
    if part is None:
        return BView(n=2 * f, k=D_MODEL, tn=tc, tk=D_MODEL, index_map=lambda i, j, kk: (j // nbc, l, j % nbc))
    nnb = D_MODEL // 512
    tiles = tuple(functools.partial(lambda i, j, kk, p: (2 * part + p // nbc, l * nnb + j, p % nbc), p=p) for p in range(2 * nbc))
    return BView(n=D_MODEL, k=f, tn=512, tk=tc, index_map=None, part_maps=tiles)


def up_grad_view(f, part, into):
    cols = f // 2
    tn = _pick(cols, (1408, 1024, 512, 256))
    nbc = cols // tn
    return OView((N_CHIPS, D_MODEL, cols), lambda i, j, kk: (2 * part + j // nbc, i, j % nbc), tn, into)


def ada_view(l, n_ada, tb):
    if tb:
        return BView(n=D_MODEL, k=n_ada, tn=1024, tk=n_ada, index_map=lambda i, j, kk: (l, j, 0))
    return BView(n=n_ada, k=D_MODEL, tn=1024, tk=D_MODEL, index_map=lambda i, j, kk: (l, 0, j))


def _prep_args(z, zg, cos, sin, w):
    rows = [Row(z, Z_AV, 0), Row(z, 512, Z_RQ // 512), Row(z, 512, Z_RK // 512), Row(z, 256, Z_GQ // 256),
            Row(zg, LANES, 0), Row(cos, HEAD_DIM, 0, False), Row(sin, HEAD_DIM, 0, False)]
    return rows, [Par(w.q_g), Par(w.k_g), Par(w.gate_up), Par(w.gate_b)]


def _post_args(o_att, o_ret, o_gla, z, w):
    rows = [Row(o_att, 1024), Row(o_ret, 512, 0), Row(o_ret, 512, 1, False), Row(o_gla, 512, 0), Row(o_gla, 512, 1, False),
            Row(z, 512, Z_RG // 512), Row(z, 512, Z_GR // 512)]
    return rows, [Par(w.ret_g), Par(w.gla_g)]


def layer_fwd(l, xs, mods, w, fetch, cos, sin, n_lat, n_out):
    t, d = xs.shape
    tag = f"l{l}_"
    nm1 = [Par(w.norm1_g), Par(_mod(mods, 0), True), Par(_mod(mods, 1), True)]
    (h,) = row_map(tag + "norm1", normmod_tile, [Row(xs, d)], nm1, [(d, BF16)], t, n_lat)
    (w_main, w_gate), started = fetch("w_in", h)
    z = matmul(tag + "in_proj", h, w_main, after=started)
    zg = matmul(tag + "gate_proj", h, w_gate)
    rows, pars = _prep_args(z, zg, cos, sin, w)
    (p,) = row_map(tag + "prep", prep_tile, rows, pars, [(P_W, F32)], t, n_lat)
    o_att, lse = attn_fwd(p, z, n_lat)
    o_ret, s_ret = ret_fwd(p, z, w.lg, n_lat)
    o_gla, s_gla = gla_fwd(p, z, n_lat)
    rows, pars = _post_args(o_att, o_ret, o_gla, z, w)
    (m,) = row_map(tag + "post", post_tile, rows, pars, [(d, BF16)], t, n_lat)
    m = m[:n_out]
    g_out, started = fetch("w_out", m)
    y = matmul(tag + "out_proj", m, g_out, view=out_view(0, False), after=started)
    rn = [Par(_mod(mods, 2), True), Par(w.norm2_g), Par(_mod(mods, 3), True), Par(_mod(mods, 4), True)]
    x1, h2 = row_map(tag + "resid1_norm2", resid_norm_tile, [Row(xs, d), Row(y, d)], rn, [(d, F32), (d, BF16)], n_out, n_lat)
    f = w.conv_b.shape[1]
    g_up, started = fetch("w_up", h2)
    u = matmul(tag + "up_proj", h2, g_up, view=up_view(0, f), after=started, out_dtype=BF16)
    g = convglu(tag + "convglu", u, w.conv_w, w.conv_b, n_lat)
    g_down, started = fetch("w_down", g)
    yd = matmul(tag + "down_proj", g, g_down, view=down_view(0, f, False), after=started)
    (x2,) = row_map(tag + "resid2", resid_tile, [Row(x1, d), Row(yd, d)], [Par(_mod(mods, 5), True)], [(d, F32)], n_out, n_lat)
    saved = dict(xs=xs, h=h, z=z, zg=zg, p=p, o_att=o_att, lse=lse, o_ret=o_ret, s_ret=s_ret, o_gla=o_gla, s_gla=s_gla,
                 m=m, y=y, x1=x1, h2=h2, u=u, g=g, yd=yd, w_main=w_main, w_gate=w_gate, g_out=g_out, g_up=g_up, g_down=g_down)
    return x2, saved


def _sum_dirs(a):
    w = a.shape[1] // 2
    return a[:, :w] + a[:, w:]


def layer_bwd(l, dx2, s, mods, w, cos, sin, n_lat, grad_ready):
    (t, d), n_out = s["xs"].shape, dx2.shape[0]
    tag = f"l{l}_b_"

    def all_rows(a):
        return a if n_out == t else jnp.pad(a, ((0, t - n_out), (0, 0)))

    dyd, dgate5 = row_vjp(tag + "resid2", gated_tile, [Row(s["yd"], d)], [Par(_mod(mods, 5), True)], [dx2], n_out, n_lat,
                          row_grad_dtype=BF16)
    f = w.conv_b.shape[1]
    dg = matmul(tag + "down_dx", dyd, s["g_down"], tb=True, view=down_view(0, f, True))
    dw_down = matmul(tag + "down_dw", s["g"], dyd, ta=True, out_dtype=BF16)
    da, dv, dcw, dcb = convglu_bwd(tag + "convglu", s["u"], w.conv_w, w.conv_b, dg, n_lat)
    dh2 = matmul(tag + "up_dx_gate", da, s["g_up"], tb=True, view=up_view(0, f, 0))
    dh2 = matmul(tag + "up_dx_value", dv, s["g_up"], tb=True, view=up_view(0, f, 1), add=dh2)
    dw_up = matmul(tag + "up_dw_gate", s["h2"], da, ta=True, out_dtype=BF16, o_view=up_grad_view(f, 0, None))
    dw_up = matmul(tag + "up_dw_value", s["h2"], dv, ta=True, out_dtype=BF16, o_view=up_grad_view(f, 1, dw_up))
    started = grad_ready("ffn", dict(w_up=dw_up, w_down=dw_down))
    rn = [Par(_mod(mods, 2), True), Par(w.norm2_g), Par(_mod(mods, 3), True), Par(_mod(mods, 4), True)]
    dx1, dy, dgate2, dg2, dshift3, dscale4 = row_vjp(
        tag + "resid1_norm2", resid_norm_tile, [Row(s["xs"], d), Row(s["y"], d)], rn, [dx2, dh2], n_out, n_lat,
        row_grad_dtype=(F32, BF16), after=started)
    if n_out < t:
        dgate5, dshift3, dscale4, dgate2 = [g.at[1].set(0.0) for g in (dgate5, dshift3, dscale4, dgate2)]
    dm = matmul(tag + "out_dx", dy, s["g_out"], tb=True, view=out_view(0, True))
    dw_out = matmul(tag + "out_dw", s["m"], dy, ta=True, out_dtype=BF16)
    rows, pars = _post_args(s["o_att"], s["o_ret"], s["o_gla"], s["z"], w)
    started = grad_ready("w_out", dict(w_out=dw_out))
    do_att, do_ret, do_gla, d_rg, d_gr, d_ret_g, d_gla_g = row_vjp(tag + "post", post_tile, rows, pars, [dm], n_out, n_lat, after=started)
    do_att, do_ret, do_gla, d_rg, d_gr, dx1 = [all_rows(a) for a in (do_att, do_ret, do_gla, d_rg, d_gr, dx1)]
    dq_a, dk_a, dv_a = attn_bwd(s["p"], s["z"], s["o_att"], s["lse"], do_att, n_lat)
    dq_r, dk_r, dv_r, dlg = ret_bwd(s["p"], s["z"], w.lg, s["s_ret"], do_ret, n_lat)
    dq_g, dk_g, dv_g, dla = gla_bwd(s["p"], s["z"], s["s_gla"], do_gla, n_lat)
    dp = jnp.concatenate([dq_a, dk_a, _sum_dirs(dq_g), _sum_dirs(dq_r), _sum_dirs(dk_r), dla], axis=1)
    rows, pars = _prep_args(s["z"], s["zg"], cos, sin, w)
    d_zqk, d_zrq, d_zrk, d_zgq, dzg, d_qg, d_kg, d_up, d_gb = row_vjp(tag + "prep", prep_tile, rows, pars, [dp], t, n_lat)
    dz = jnp.concatenate([d_zqk, dv_a, d_zrq, d_zrk, _sum_dirs(dv_r), d_rg, d_zgq, _sum_dirs(dk_g), _sum_dirs(dv_g), d_gr], axis=1)
    dz, dzg = dz.astype(BF16), dzg.astype(BF16)
    dh_gate = matmul(tag + "gate_dx", dzg, s["w_gate"], tb=True)
    dh = matmul(tag + "in_dx", dz, s["w_main"], tb=True, add=dh_gate)
    dw_main = matmul(tag + "in_dw", s["h"], dz, ta=True, out_dtype=BF16)
    dw_gate = matmul(tag + "gate_dw", s["h"], dzg, ta=True, out_dtype=BF16)
    started = grad_ready("w_in", dict(w_main=dw_main, w_gate=dw_gate))
    nm1 = [Par(w.norm1_g), Par(_mod(mods, 0), True), Par(_mod(mods, 1), True)]
    dx, dg1, dshift0, dscale1 = row_vjp(tag + "norm1", normmod_tile, [Row(s["xs"], d)], nm1, [dh], t, n_lat,
                                        add_to_first=dx1, after=started)
    dmods = jnp.concatenate([dshift0, dscale1, dgate2, dshift3, dscale4, dgate5], axis=1)
    grads = dict(w_main=dw_main, w_gate=dw_gate, w_out=dw_out, w_up=dw_up, w_down=dw_down, norm1_g=dg1, q_g=d_qg, k_g=d_kg,
                 lg=dlg, ret_g=d_ret_g, gate_up=d_up, gate_b=d_gb, gla_g=d_gla_g, norm2_g=dg2, conv_w=dcw, conv_b=dcb)
    return dx, dmods, grads


def rope_tables(n_lat, n_ctx):
    rows = n_lat // GRID_W
    row = jnp.repeat(jnp.arange(rows, dtype=F32), GRID_W)
    col = jnp.tile(jnp.arange(GRID_W, dtype=F32), rows)
    n_freq = HEAD_DIM // 4
    inv_freq = ROPE_THETA ** (-jnp.arange(n_freq, dtype=F32) / n_freq)
    ang = jnp.concatenate([row[:, None] * inv_freq, col[:, None] * inv_freq], axis=-1)
    cos, sin = jnp.cos(ang), jnp.sin(ang)
    cos = jnp.concatenate([jnp.concatenate([cos, cos], axis=1), jnp.ones((n_ctx, HEAD_DIM), F32)], axis=0)
    sin = jnp.concatenate([jnp.concatenate([-sin, sin], axis=1), jnp.zeros((n_ctx, HEAD_DIM), F32)], axis=0)
    return cos, sin


def local_step(xs, target, mods, weights, fetch, final_g, n_lat, grad_ready):
    t, d = xs.shape
    cos, sin = rope_tables(n_lat, t - n_lat)
    saved = []
    h = xs
    for l, w in enumerate(weights):
        n_out = t if l + 1 < len(weights) else n_lat
        h, s = layer_fwd(l, h, mods[l], w, functools.partial(fetch, l), cos, sin, n_lat, n_out)
        saved.append(s)
    loss, dx, dgf = final_loss(h, target, final_g, n_lat)
    dmods, grads = [None] * len(weights), [None] * len(weights)
    for l in reversed(range(len(weights))):
        dx, dmods[l], grads[l] = layer_bwd(l, dx, saved[l], mods[l], weights[l], cos, sin, n_lat, functools.partial(grad_ready, l))
    return loss, dx, dmods, grads, dgf


WEIGHT_NAMES = ("c_ctx", "ada_w", "ada_b", "norm1_g", "w_in", "q_norm_g", "k_norm_g", "ret_log_decay", "ret_norm_g",
                "gla_gate_up", "gla_gate_b", "gla_norm_g", "w_out", "norm2_g", "w_up", "conv_w", "conv_b", "w_down", "final_norm_g")
PACK_QUANTUM = 8 * LANES


def _pack(arrays):
    flat = jnp.concatenate([a.reshape(-1).astype(F32) for a in arrays])
    n = -(-flat.shape[0] // PACK_QUANTUM) * PACK_QUANTUM
    return jnp.pad(flat, (0, n - flat.shape[0])).reshape(8, n // 8)


def _unpack(flat2d, shapes):
    out, at = [], 0
    for s in shapes:
        size = int(np.prod(s))
        out.append(flat2d[:, at:at + size].reshape((flat2d.shape[0],) + tuple(s)))
        at += size
    return out


def _per_device(gathered):
    return gathered.reshape(N_DEV, -1)


def _from_chips(per_device, axis):
    chips = per_device[0::2]
    moved = jnp.moveaxis(chips, 0, axis)
    shape = moved.shape
    return moved.reshape(shape[:axis] + (shape[axis] * shape[axis + 1],) + shape[axis + 2:])


def kernel(x, c, ctx, c_ctx, ada_w, ada_b, norm1_g, w_in, q_norm_g, k_norm_g, ret_log_decay, ret_norm_g, gla_gate_up, gla_gate_b, gla_norm_g, w_out, norm2_g, w_up, conv_w, conv_b, w_down, final_norm_g, loss_target, m_c_ctx, m_ada_w, m_ada_b, m_norm1_g, m_w_in, m_q_norm_g, m_k_norm_g, m_ret_log_decay, m_ret_norm_g, m_gla_gate_up, m_gla_gate_b, m_gla_norm_g, m_w_out, m_norm2_g, m_w_up, m_conv_w, m_conv_b, m_w_down, m_final_norm_g, v_c_ctx, v_ada_w, v_ada_b, v_norm1_g, v_w_in, v_q_norm_g, v_k_norm_g, v_ret_log_decay, v_ret_norm_g, v_gla_gate_up, v_gla_gate_b, v_gla_norm_g, v_w_out, v_norm2_g, v_w_up, v_conv_w, v_conv_b, v_w_down, v_final_norm_g):
    weights = dict(zip(WEIGHT_NAMES, (c_ctx, ada_w, ada_b, norm1_g, w_in, q_norm_g, k_norm_g, ret_log_decay, ret_norm_g,
                                      gla_gate_up, gla_gate_b, gla_norm_g, w_out, norm2_g, w_up, conv_w, conv_b, w_down, final_norm_g)))
    mom_m = dict(zip(WEIGHT_NAMES, (m_c_ctx, m_ada_w, m_ada_b, m_norm1_g, m_w_in, m_q_norm_g, m_k_norm_g, m_ret_log_decay, m_ret_norm_g,
                                    m_gla_gate_up, m_gla_gate_b, m_gla_norm_g, m_w_out, m_norm2_g, m_w_up, m_conv_w, m_conv_b, m_w_down, m_final_norm_g)))
    mom_v = dict(zip(WEIGHT_NAMES, (v_c_ctx, v_ada_w, v_ada_b, v_norm1_g, v_w_in, v_q_norm_g, v_k_norm_g, v_ret_log_decay, v_ret_norm_g,
                                    v_gla_gate_up, v_gla_gate_b, v_gla_norm_g, v_w_out, v_norm2_g, v_w_up, v_conv_w, v_conv_b, v_w_down, v_final_norm_g)))
    depth, d = norm1_g.shape
    assert d == D_MODEL and x.shape[0] == 1
    n_lat, n_ctx, f = x.shape[1], ctx.shape[1], conv_b.shape[1]
    assert n_lat % ROW_TILE == 0 and n_ctx % ROW_TILE == 0 and f % FFN_COL_TILE == 0 and f % N_CHIPS == 0
    n_in = w_in.shape[2]
    n_ada = ada_w.shape[2]
    xi, yi, ci = lax.axis_index("x"), lax.axis_index("y"), lax.axis_index("c")
    chip = 2 * xi + yi
    dev = 2 * chip + ci

    big = ("w_in", "w_out", "w_up", "w_down")
    order = [(l, name) for l in range(depth) for name in big]
    shards = [weights[name][l].astype(BF16) for l, name in order]
    passing = {}

    def pass_on(k, after):
        tag = "{1}{0}".format(*order[k])
        own, land = split_wait("gather_wait_" + tag, GATHER, in_flight[k], after)
        (moving,), started = split_start("gather_pass_" + tag, PASS_ON, [land])
        passing[k] = (own, moving)
        return started

    def fetch(l, name, after):
        k = order.index((l, name))
        if k == 0:
            pass_on(0, after)
        own, moving = passing.pop(k)
        (land,) = split_wait(f"gather_pass_wait_{name}{l}", PASS_ON, moving, after)
        started = pass_on(k + 1, after) if k + 1 < len(order) else None
        land = lax.dynamic_update_slice_in_dim(land, own[None], chip, axis=0)
        if name != "w_in":
            return land, started
        last = N_MAIN - (N_CHIPS - 1) * n_in
        w_main = jnp.concatenate([land[q] for q in range(N_CHIPS - 1)] + [land[N_CHIPS - 1][:, :last]], axis=1)
        return (w_main, jnp.pad(land[N_CHIPS - 1][:, last:], ((0, 0), (0, LANES - N_GATE)))), started

    small_shapes = [c.shape[1:], conv_w.shape, gla_gate_up.shape, gla_gate_b.shape]
    got = _per_device(all_gather_small("gather_small", _pack([c, conv_w, gla_gate_up, gla_gate_b])))
    c_all, conv_w_sh, gate_up_sh, gate_b_sh = _unpack(got, small_shapes)
    conv_w_full = _from_chips(conv_w_sh, 2)
    gate_up_full = _from_chips(gate_up_sh, 3)
    gate_b_full = _from_chips(gate_b_sh, 2)

    act = jnp.zeros((16, d), F32).at[0:N_DEV].set(jax.nn.silu(c_all)).at[N_DEV].set(jax.nn.silu(c_ctx))
    mod_sh = jnp.stack([matmul(f"ada_fwd{l}", act, ada_w, view=ada_view(l, n_ada, False)) for l in range(depth)])
    got = _per_device(all_gather_small("gather_mods", _pack([mod_sh])))
    (mod_sh_all,) = _unpack(got, [mod_sh.shape])
    mod_full = _from_chips(mod_sh_all, 2) + ada_b[:, None, :]
    mod_mine = lax.dynamic_index_in_dim(mod_full, dev, axis=1, keepdims=False)
    mods = [jnp.stack([mod_mine[l].reshape(N_MOD, d), mod_full[l, N_DEV].reshape(N_MOD, d)]) for l in range(depth)]
    in_flight, token = split_start("gather_start", GATHER, shards, [(N_CHIPS,) + s.shape for s in shards], after=mod_full)

    layer_w = []
    for l in range(depth):
        up = jnp.zeros((2, LANES, GLA_HEADS * GLA_DK), F32)
        up = up.at[0, 0:GLA_RANK].set(gate_up_full[l, 0]).at[1, GLA_RANK:2 * GLA_RANK].set(gate_up_full[l, 1])
        layer_w.append(LayerWeights(
            norm1_g=norm1_g[l].reshape(1, 1, d), q_g=q_norm_g[l].reshape(1, 1, HEAD_DIM), k_g=k_norm_g[l].reshape(1, 1, HEAD_DIM),
            lg=ret_log_decay[l].reshape(2, RET_HEADS, 1, 1), ret_g=ret_norm_g[l].reshape(1, 1, HEAD_DIM),
            gate_up=up.reshape(1, 2 * LANES, -1), gate_b=gate_b_full[l].reshape(1, 2, -1), gla_g=gla_norm_g[l].reshape(1, 1, HEAD_DIM),
            norm2_g=norm2_g[l].reshape(1, 1, d), conv_w=conv_w_full[l], conv_b=conv_b[l].reshape(1, f)))

    def pieces_of(name, g):
        if name == "w_in":
            shards = [g["w_main"][:, q * n_in:(q + 1) * n_in] for q in range(N_CHIPS - 1)]
            tail = jnp.concatenate([g["w_main"][:, (N_CHIPS - 1) * n_in:], g["w_gate"][:, :N_GATE]], axis=1)
            return jnp.stack(shards + [tail])
        if name == "w_up":
            return g["w_up"]
        return g[name].reshape(N_CHIPS, -1, d)

    groups = {"ffn": ("w_up", "w_down"), "w_out": ("w_out",), "w_in": ("w_in",)}
    reducing = {}
    to_sibling = []

    def sibling_arrived(after):
        started = None
        while to_sibling:
            l, group, in_flight_halves = to_sibling.pop(0)
            sums = []
            for name, halves in zip(groups[group], in_flight_halves):
                pieces, from_sibling = split_wait(f"rs_sibling_wait_{name}{l}", SIBLING_HALF, halves, after)
                sums.append(add_sibling_half(f"rs_add_sibling_{name}{l}", pieces, from_sibling, ci))
            in_flight_sums, token = split_start(f"rs_start_{group}{l}", SCATTER, sums, [(3,) + s.shape[1:] for s in sums])
            reducing.update({(l, name): grp for name, grp in zip(groups[group], in_flight_sums)})
            started = token if started is None else started + token
        return started

    def grad_ready(l, group, g):
        pieces = [pieces_of(name, g) for name in groups[group]]
        before = None if (l, group) == (0, "w_in") else sibling_arrived(pieces[0])
        in_flight_halves, started = split_start(f"rs_sibling_{group}{l}", SIBLING_HALF, pieces,
                                                [(N_CHIPS, pc.shape[1] // 2, pc.shape[2]) for pc in pieces])
        to_sibling.append((l, group, in_flight_halves))
        return started if before is None else started + before

    xs = jnp.concatenate([x[0], ctx[0]], axis=0) + token[0, 0]
    loss, dx, dmods, grads, dgf = local_step(xs, loss_target[0], mods, layer_w, fetch, final_norm_g.reshape(1, d), n_lat, grad_ready)

    def gate_up_grad(g):
        return jnp.stack([g[0, 0:GLA_RANK], g[0, LANES + GLA_RANK:LANES + 2 * GLA_RANK]])

    per_layer = [[dmods[l][0], dmods[l][1], grads[l]["norm1_g"], grads[l]["norm2_g"], grads[l]["q_g"], grads[l]["k_g"],
                  grads[l]["ret_g"], grads[l]["gla_g"], grads[l]["lg"], gate_up_grad(grads[l]["gate_up"]), grads[l]["gate_b"],
                  grads[l]["conv_w"], grads[l]["conv_b"]] for l in range(depth)]
    layer_shapes = [(N_MOD * d,), (N_MOD * d,), (d,), (d,), (HEAD_DIM,), (HEAD_DIM,), (HEAD_DIM,), (HEAD_DIM,), (2, RET_HEADS),
                    (2, GLA_RANK, GLA_HEADS * GLA_DK), (2, GLA_HEADS * GLA_DK), (3, f), (f,)]
    packed = _pack([a for lay in per_layer for a in lay] + [dgf, loss[0, 0:1]])
    gathered = all_gather_small("gather_small_grads", packed)
    every = _unpack(_per_device(gathered), layer_shapes * depth + [(d,), (1,)])
    total = _unpack(sum_device_blocks("sum_small_grads", gathered).reshape(1, -1), layer_shapes * depth + [(d,), (1,)])
    nl = len(layer_shapes)

    def tot(l, k):
        return total[l * nl + k][0]

    out = {"norm1_g": jnp.stack([tot(l, 2) for l in range(depth)]), "norm2_g": jnp.stack([tot(l, 3) for l in range(depth)]),
           "q_norm_g": jnp.stack([tot(l, 4) for l in range(depth)]), "k_norm_g": jnp.stack([tot(l, 5) for l in range(depth)]),
           "ret_norm_g": jnp.stack([tot(l, 6) for l in range(depth)]), "gla_norm_g": jnp.stack([tot(l, 7) for l in range(depth)]),
           "ret_log_decay": jnp.stack([tot(l, 8) for l in range(depth)]),
           "gla_gate_up": lax.dynamic_slice_in_dim(jnp.stack([tot(l, 9) for l in range(depth)]), chip * gla_gate_up.shape[3], gla_gate_up.shape[3], axis=3),
           "gla_gate_b": lax.dynamic_slice_in_dim(jnp.stack([tot(l, 10) for l in range(depth)]), chip * gla_gate_b.shape[2], gla_gate_b.shape[2], axis=2),
           "conv_w": lax.dynamic_slice_in_dim(jnp.stack([tot(l, 11) for l in range(depth)]), chip * conv_w.shape[2], conv_w.shape[2], axis=2),
           "conv_b": jnp.stack([tot(l, 12) for l in range(depth)]),
           "final_norm_g": total[depth * nl][0],
           "ada_b": jnp.stack([tot(l, 0) + tot(l, 1) for l in range(depth)])}
    loss_total = total[depth * nl + 1][0, 0]

    dmod_all = jnp.zeros((depth, 16, N_MOD * d), F32)
    for l in range(depth):
        dmod_all = dmod_all.at[l, 0:N_DEV].set(every[l * nl][:, :]).at[l, N_DEV].set(tot(l, 1))
    dmod_cols = lax.dynamic_slice_in_dim(dmod_all, chip * n_ada, n_ada, axis=2)
    for l in range(depth):
        slab = OView((depth, d, n_ada), functools.partial(lambda i, j, kk, l: (l, i, j), l=l), None, out.get("ada_w"))
        out["ada_w"] = matmul(f"ada_dw{l}", act, dmod_cols[l], ta=True, o_view=slab)
    dact = matmul("ada_dx0", dmod_cols[0], ada_w, tb=True, view=ada_view(0, n_ada, True))
    for l in range(1, depth):
        dact = matmul(f"ada_dx{l}", dmod_cols[l], ada_w, tb=True, view=ada_view(l, n_ada, True), add=dact)
    got = _per_device(all_gather_small("gather_dcctx", _pack([dact[N_DEV]])))
    sibling_arrived(got)
    got = got[0::2, :d]
    dsilu = got[0] + got[1] + got[2] + got[3]
    sig = jax.nn.sigmoid(c_ctx)
    out["c_ctx"] = dsilu * (sig + c_ctx * sig * (1.0 - sig))

    deltas, new_m, new_v = {}, {}, {}

    def update(name):
        out[name] = out[name].reshape(weights[name].shape)
        deltas[name], new_m[name], new_v[name] = adamw("adamw_" + name, weights[name], out[name], mom_m[name], mom_v[name])

    for name in WEIGHT_NAMES:
        if name not in big:
            update(name)
    behind = new_v["ada_w"]
    joining = []

    def joined(after):
        name, in_flight_halves = joining.pop()
        per_layer = [split_wait(f"rs_join_wait_{name}{l}", JOIN, grp, after)[0] for l, grp in enumerate(in_flight_halves)]
        (deltas[name], new_m[name], new_v[name]), out[name] = adamw_layers(
            "adamw_" + name, weights[name], per_layer, mom_m[name], mom_v[name])
        return new_v[name]

    for name in ("w_down", "w_up", "w_out", "w_in"):
        halves = []
        for l in range(depth):
            sums, got = split_wait(f"rs_wait_{name}{l}", SCATTER, reducing[(l, name)], behind)
            halves.append(add_chip_sums(f"rs_add_chips_{name}{l}", sums, got, chip, ci))
        in_flight_halves, _ = split_start("rs_join_" + name, JOIN, halves)
        if joining:
            behind = joined(behind)
        joining.append((name, in_flight_halves))
    joined(behind)
    grad_x = dx[:n_lat].reshape(x.shape)
    return (loss_total, grad_x, *[out[n] for n in WEIGHT_NAMES], *[deltas[n] for n in WEIGHT_NAMES],
            *[new_m[n] for n in WEIGHT_NAMES], *[new_v[n] for n in WEIGHT_NAMES])
```

```python
import functools
from typing import NamedTuple

import numpy as np
import jax
import jax.numpy as jnp
from jax import lax
from jax.experimental import pallas as pl
from jax.experimental.pallas import tpu as pltpu

F32 = jnp.float32
BF16 = jnp.bfloat16

D_MODEL = 2048
HEAD_DIM = 128
ATT_Q_HEADS = 8
ATT_KV_HEADS = 2
ATT_GROUP = ATT_Q_HEADS // ATT_KV_HEADS
RET_HEADS = 4
GLA_HEADS = 4
GLA_DK = 64
GLA_DV = 128
GLA_RANK = 16
GLA_TAU = 16.0
RET_CHUNK = 256
GLA_CHUNK = 128
GRID_W = 64
ROPE_THETA = 10000.0
N_MOD = 6
EPS = 1e-6
N_MAIN = 5120
N_GATE = 2 * GLA_RANK
LANES = 128
ROW_TILE = 256
FFN_COL_TILE = 256
VMEM_LIMIT = 56 * 1024 * 1024

ADAM_LR = 0.001
ADAM_B1 = 0.9
ADAM_B2 = 0.999
ADAM_EPS = 1e-08
ADAM_WD = 0.01
ADAM_STEP = 10

Z_AQ, Z_AK, Z_AV = 0, 1024, 1280
Z_RQ, Z_RK, Z_RV, Z_RG = 1536, 2048, 2560, 3072
Z_GQ, Z_GK, Z_GV, Z_GR = 3584, 3840, 4096, 4608
P_AQ, P_AK, P_GQ, P_RQ, P_RK, P_LA = 0, 1024, 1280, 1536, 2048, 2560
P_W = 3072


def _params(sem=None):
    return pltpu.CompilerParams(dimension_semantics=sem, vmem_limit_bytes=VMEM_LIMIT)


def _pick(n, cands):
    for c in cands:
        if n % c == 0:
            return c
    return n


_NN = (((1,), (0,)), ((), ()))
_NT = (((1,), (1,)), ((), ()))
_TN = (((0,), (0,)), ((), ()))


def _dg(a, b, dims):
    return lax.dot_general(a.astype(BF16), b.astype(BF16), dims, preferred_element_type=F32)


@jax.custom_vjp
def bdot(a, b):
    return _dg(a, b, _NN)


def _bdot_fwd(a, b):
    return _dg(a, b, _NN), (a, b)


def _bdot_bwd(res, ct):
    a, b = res
    return _dg(ct, b, _NT), _dg(a, ct, _TN)


bdot.defvjp(_bdot_fwd, _bdot_bwd)


@jax.custom_vjp
def bdot_nt(a, b):
    return _dg(a, b, _NT)


def _bdot_nt_fwd(a, b):
    return _dg(a, b, _NT), (a, b)


def _bdot_nt_bwd(res, ct):
    a, b = res
    return _dg(ct, b, _NN), _dg(ct, a, _TN)


bdot_nt.defvjp(_bdot_nt_fwd, _bdot_nt_bwd)


@jax.custom_vjp
def bdot_tn(a, b):
    return _dg(a, b, _TN)


def _bdot_tn_fwd(a, b):
    return _dg(a, b, _TN), (a, b)


def _bdot_tn_bwd(res, ct):
    a, b = res
    return _dg(b, ct, _NT), _dg(a, ct, _NN)


bdot_tn.defvjp(_bdot_tn_fwd, _bdot_tn_bwd)


def _split3(x):
    x1 = x.astype(BF16)
    r1 = x - x1.astype(F32)
    x2 = r1.astype(BF16)
    x3 = (r1 - x2.astype(F32)).astype(BF16)
    return x1, x2, x3


def _mask_dot(mask_bf16, x, dims):
    x1, x2, x3 = _split3(x)
    f = lambda t: lax.dot_general(mask_bf16, t, dims, preferred_element_type=F32)
    return f(x1) + f(x2) + f(x3)


@jax.custom_vjp
def mask_cumsum(mask, x):
    return _mask_dot(mask.astype(BF16), x, _NN)


def _mask_cumsum_fwd(mask, x):
    return mask_cumsum(mask, x), mask


def _mask_cumsum_bwd(mask, ct):
    return jnp.zeros_like(mask), _mask_dot(mask.astype(BF16), ct, _TN)


mask_cumsum.defvjp(_mask_cumsum_fwd, _mask_cumsum_bwd)


def _roll(x, shift, axis):
    return pltpu.roll(x, shift % x.shape[axis], axis)


@functools.partial(jax.custom_vjp, nondiff_argnums=(1, 2))
def roll(x, shift, axis):
    return _roll(x, shift, axis)


def _roll_fwd(x, shift, axis):
    return _roll(x, shift, axis), None


def _roll_bwd(shift, axis, _, ct):
    return (_roll(ct, -shift, axis),)


roll.defvjp(_roll_fwd, _roll_bwd)


def rms(x):
    return x * lax.rsqrt(jnp.mean(x * x, axis=-1, keepdims=True) + EPS)


def silu(x):
    return x * (0.5 + 0.5 * jnp.tanh(0.5 * x))


def log_sigmoid(x):
    return jnp.minimum(x, 0.0) - jnp.log(1.0 + jnp.exp(-jnp.abs(x)))


def rope(t, cos, sin):
    return t * cos + roll(t, HEAD_DIM // 2, 1) * sin


def _heads(x, n, width=HEAD_DIM):
    return [x[:, h * width:(h + 1) * width] for h in range(n)]


class Row(NamedTuple):
    arr: jax.Array
    width: int
    idx: int = 0
    diff: bool = True


class Par(NamedTuple):
    arr: jax.Array
    grouped: bool = False
    diff: bool = True


def _row_specs(rows, pars, tm, n_lat_tiles):
    def grp(i):
        return jnp.minimum(i // n_lat_tiles, 1)

    specs = [pl.BlockSpec((tm, r.width), functools.partial(lambda i, k: (i, k), k=r.idx)) for r in rows]
    for p in pars:
        blk = (1,) + p.arr.shape[1:]
        if p.grouped:
            specs.append(pl.BlockSpec(blk, lambda i: (grp(i), 0, 0)))
        else:
            specs.append(pl.BlockSpec(blk, lambda i: (0, 0, 0)))
    return specs


def row_map(name, fn, rows, pars, outs, n_rows, n_lat):
    tm = ROW_TILE
    nr, npar = len(rows), len(pars)

    def body(*refs):
        vals = [r[...].astype(F32) for r in refs[:nr]] + [p[0] for p in refs[nr:nr + npar]]
        res = fn(*vals)
        for o, v in zip(refs[nr + npar:], res):
            o[...] = v.astype(o.dtype)

    return pl.pallas_call(
        body, name=name, grid=(n_rows // tm,),
        in_specs=_row_specs(rows, pars, tm, n_lat // tm),
        out_specs=[pl.BlockSpec((tm, w), lambda i: (i, 0)) for w, _ in outs],
        out_shape=[jax.ShapeDtypeStruct((n_rows, w), dt) for w, dt in outs],
        compiler_params=_params(("arbitrary",)),
    )(*[r.arr for r in rows], *[p.arr for p in pars])


def row_vjp(name, fn, rows, pars, cts, n_rows, n_lat, add_to_first=None, row_grad_dtype=F32, after=None):
    tm = ROW_TILE
    nr, npar, nc = len(rows), len(pars), len(cts)
    n_lat_tiles = n_lat // tm
    args = list(rows) + list(pars)
    diff_pos = [k for k, a in enumerate(args) if a.diff]
    n_add = 0 if add_to_first is None else 1
    n_after = 0 if after is None else 1

    def body(*refs):
        i = pl.program_id(0)
        vals = [r[...].astype(F32) for r in refs[:nr]] + [p[0] for p in refs[nr:nr + npar]]
        ct_vals = tuple(c[...] for c in refs[nr + npar:nr + npar + nc])
        out_refs = refs[nr + npar + nc + n_add + n_after:]

        def g(*dv):
            full = list(vals)
            for k, v in zip(diff_pos, dv):
                full[k] = v
            return tuple(fn(*full))

        _, vjp = jax.vjp(g, *[vals[k] for k in diff_pos])
        grads = vjp(ct_vals)
        for n, (k, o, gr) in enumerate(zip(diff_pos, out_refs, grads)):
            if k < nr:
                o[...] = (gr + refs[nr + npar + nc][...] if (n == 0 and n_add) else gr).astype(o.dtype)
            else:
                first = (i == 0) | (i == n_lat_tiles) if args[k].grouped else (i == 0)

                @pl.when(first)
                def _():
                    o[0] = gr

                @pl.when(jnp.logical_not(first))
                def _():
                    o[0] += gr

    def grp(i):
        return jnp.minimum(i // n_lat_tiles, 1)

    out_specs, out_shape = [], []
    for k in diff_pos:
        a = args[k]
        if k < nr:
            out_specs.append(pl.BlockSpec((tm, a.width), lambda i: (i, 0)))
            dtype = row_grad_dtype[len(out_shape)] if isinstance(row_grad_dtype, tuple) else row_grad_dtype
            out_shape.append(jax.ShapeDtypeStruct((n_rows, a.width), dtype))
        else:
            blk = (1,) + a.arr.shape[1:]
            out_specs.append(pl.BlockSpec(blk, (lambda i: (grp(i), 0, 0)) if a.grouped else (lambda i: (0, 0, 0))))
            out_shape.append(jax.ShapeDtypeStruct(a.arr.shape, F32))
    extra = list(cts) + ([add_to_first] if n_add else [])
    ct_specs = [pl.BlockSpec((tm, c.shape[1]), lambda i: (i, 0)) for c in extra]
    if n_after:
        extra.append(after)
        ct_specs.append(pl.BlockSpec(memory_space=pl.ANY))
    return pl.pallas_call(
        body, name=name, grid=(n_rows // tm,),
        in_specs=_row_specs(rows, pars, tm, n_lat_tiles) + ct_specs,
        out_specs=out_specs, out_shape=out_shape,
        compiler_params=_params(("arbitrary",)),
    )(*[r.arr for r in rows], *[p.arr for p in pars], *extra)


class BView(NamedTuple):
    n: int
    k: int
    tn: int
    tk: int
    index_map: object
    lead: int = 1
    part_maps: tuple = ()


MATMUL_VMEM_BUDGET = 40 * 1024 * 1024


def _matmul_tiles(m, n, k, a_bytes, b_bytes, o_bytes):
    tms = [c for c in (1152, 1024, 768, 512, 256, 128) if m % c == 0] or [m]
    tns = [c for c in (2048, 1408, 1280, 1024, 768, 512, 256, 128) if n % c == 0] or [n]
    tks = [k] + [c for c in (2816, 2304, 2048, 1408, 1024, 512, 256, 128) if k % c == 0 and c < k]
    for tk in tks:
        fits = [(tm * tn, tm, tn) for tm in tms for tn in tns
                if 2 * (tm * tk * a_bytes + tk * tn * b_bytes + tm * tn * o_bytes) + 2 * tm * tn * 4 <= MATMUL_VMEM_BUDGET]
        if fits and (max(fits)[0] >= min(512 * 512, tms[0] * tns[0]) or tk == tks[-1]):
            _, tm, tn = max(fits)
            return tm, tn, tk
    raise ValueError(f"no matmul tiling for {(m, n, k)}")


class OView(NamedTuple):
    shape: tuple
    index_map: object
    tn: int = None
    into: object = None


def matmul(name, a, b, *, ta=False, tb=False, add=None, out_dtype=F32, view=None, o_view=None, after=None):
    m = a.shape[1] if ta else a.shape[0]
    o_bytes = jnp.dtype(out_dtype).itemsize * (1 if add is None else 2)
    if view is None:
        k = a.shape[0] if ta else a.shape[1]
        n = b.shape[0] if tb else b.shape[1]
        assert (b.shape[1] if tb else b.shape[0]) == k, (a.shape, b.shape, ta, tb)
        if o_view is not None and o_view.tn is not None:
            tn = o_view.tn
            tm, _, tk = _matmul_tiles(m, tn, k, a.dtype.itemsize, b.dtype.itemsize, o_bytes)
        else:
            tm, tn, tk = _matmul_tiles(m, n, k, a.dtype.itemsize, b.dtype.itemsize, o_bytes)
    else:
        n, k, tn, tk = view.n, view.k, view.tn, view.tk
        b_maps = view.part_maps or (view.index_map,)
        tm, _, whole = _matmul_tiles(m, tn, tk * len(b_maps), a.dtype.itemsize, b.dtype.itemsize, o_bytes)
        assert whole == tk * len(b_maps) and not (ta and len(b_maps) > 1), (name, tm, whole)
    parts = 1 if view is None else len(b_maps)
    k_step = tk * parts
    nk = k // k_step
    dims = (((0 if ta else 1,), (1 if tb else 0,)), ((), ()))

    def body(a_ref, *rest):
        b_refs, rest = rest[:parts], rest[parts:]
        if parts == 1:
            prod = lax.dot_general(a_ref[...].astype(BF16), b_refs[0][...].astype(BF16), dims, preferred_element_type=F32)
        else:
            prod = sum(lax.dot_general(a_ref[:, p * tk:(p + 1) * tk].astype(BF16), b_refs[p][...].astype(BF16), dims,
                                       preferred_element_type=F32) for p in range(parts))
        if nk == 1:
            o_ref = rest[-1]
            o_ref[...] = (prod if add is None else prod + rest[0][...]).astype(o_ref.dtype)
            return
        o_ref, acc = rest[-2:]
        kk = pl.program_id(2)

        @pl.when(kk == 0)
        def _():
            acc[...] = prod

        @pl.when(kk != 0)
        def _():
            acc[...] += prod

        @pl.when(kk == nk - 1)
        def _():
            r = acc[...]
            if add is not None:
                r = r + rest[0][...]
            o_ref[...] = r.astype(o_ref.dtype)

    if ta:
        a_spec = pl.BlockSpec((k_step, tm), lambda i, j, kk: (kk, i))
    else:
        a_spec = pl.BlockSpec((tm, k_step), lambda i, j, kk: (i, kk))
    b_tile = (tn, tk) if tb else (tk, tn)
    if view is not None:
        b_specs = [pl.BlockSpec((None,) * view.lead + b_tile, index_map) for index_map in b_maps]
    elif tb:
        b_specs = [pl.BlockSpec(b_tile, lambda i, j, kk: (j, kk))]
    else:
        b_specs = [pl.BlockSpec(b_tile, lambda i, j, kk: (kk, j))]
    o_spec = pl.BlockSpec((tm, tn), lambda i, j, kk: (i, j))
    ins = [a] + [b] * parts + ([add] if add is not None else [])
    in_specs = [a_spec] + b_specs + ([o_spec] if add is not None else [])
    out_shape, aliases = jax.ShapeDtypeStruct((m, n), out_dtype), {}
    if o_view is not None:
        assert add is None
        o_spec = pl.BlockSpec((None, tm, tn), o_view.index_map)
        out_shape = jax.ShapeDtypeStruct(o_view.shape, out_dtype)
        if o_view.into is not None:
            aliases = {len(ins): 0}
            ins.append(o_view.into)
            in_specs.append(pl.BlockSpec(memory_space=pl.ANY))
    if after is not None:
        ins.append(after)
        in_specs.append(pl.BlockSpec(memory_space=pl.ANY))
    return pl.pallas_call(
        body, name=name, grid=(m // tm, n // tn, nk),
        in_specs=in_specs, out_specs=o_spec, out_shape=out_shape, input_output_aliases=aliases,
        scratch_shapes=[pltpu.VMEM((tm, tn), F32)] if nk > 1 else [],
        compiler_params=_params(("parallel", "parallel", "arbitrary")),
    )(*ins)


def normmod_tile(x, g, shift, scale):
    return (rms(x) * g * (1.0 + scale) + shift,)


def resid_tile(x, y, gate):
    return (x + gate * y,)


def resid_norm_tile(x, y, gate, g, shift, scale):
    x1 = x + gate * y
    return x1, rms(x1) * g * (1.0 + scale) + shift


def gated_tile(y, gate):
    return (gate * y,)


def prep_tile(z_qk, z_rq, z_rk, z_gq, zg, cos, sin, qg, kg, gate_up, gate_b):
    out = []
    for h, t in enumerate(_heads(z_qk, ATT_Q_HEADS + ATT_KV_HEADS)):
        out.append(rope(rms(t) * (qg if h < ATT_Q_HEADS else kg), cos, sin))
    gq = z_gq * (GLA_DK ** -0.5)
    rq = [rope(t, cos, sin) for t in _heads(z_rq, RET_HEADS)]
    rk = [rope(t * (HEAD_DIM ** -0.5), cos, sin) for t in _heads(z_rk, RET_HEADS)]
    la = [log_sigmoid(bdot(zg, gate_up[d * LANES:(d + 1) * LANES]) + gate_b[d:d + 1]) * (1.0 / GLA_TAU) for d in range(2)]
    return (jnp.concatenate(out + [gq] + rq + rk + la, axis=1),)


def post_tile(o_att, o_ret_f, o_ret_b, o_gla_f, o_gla_b, rg, gr, ret_g, gla_g):
    ret = jnp.concatenate([rms(t) * ret_g for t in _heads(o_ret_f + o_ret_b, RET_HEADS)], axis=1) * silu(rg)
    gla = jnp.concatenate([rms(t) * gla_g for t in _heads(o_gla_f + o_gla_b, GLA_HEADS)], axis=1) * silu(gr)
    return (jnp.concatenate([o_att, ret, gla], axis=1),)


def _convglu_tile(n_lat, a, v, cw, cb):
    t = a.shape[0]
    row = lax.broadcasted_iota(jnp.int32, (t, 1), 0)
    has_prev = ((row != 0) & (row != n_lat)).astype(F32)
    has_next = ((row != n_lat - 1) & (row != t - 1)).astype(F32)
    conv = roll(a, 1, 0) * has_prev * cw[0:1] + a * cw[1:2] + roll(a, -1, 0) * has_next * cw[2:3] + cb
    return silu(conv) * v


def convglu(name, u, cw, cb, n_lat):
    t, f2 = u.shape
    f, tc = f2 // 2, FFN_COL_TILE
    nb = f // tc

    def body(a_ref, v_ref, cw_ref, cb_ref, o_ref):
        o_ref[...] = _convglu_tile(n_lat, a_ref[...].astype(F32), v_ref[...].astype(F32), cw_ref[...], cb_ref[...]).astype(o_ref.dtype)

    return pl.pallas_call(
        body, name=name, grid=(nb,),
        in_specs=[pl.BlockSpec((t, tc), lambda j: (0, j)), pl.BlockSpec((t, tc), lambda j: (0, nb + j)),
                  pl.BlockSpec((3, tc), lambda j: (0, j)), pl.BlockSpec((1, tc), lambda j: (0, j))],
        out_specs=pl.BlockSpec((t, tc), lambda j: (0, j)),
        out_shape=jax.ShapeDtypeStruct((t, f), BF16),
        compiler_params=_params(("parallel",)),
    )(u, u, cw, cb)


def convglu_bwd(name, u, cw, cb, dg, n_lat):
    t, f2 = u.shape
    f, tc = f2 // 2, FFN_COL_TILE
    nb = f // tc

    def body(a_ref, v_ref, cw_ref, cb_ref, dg_ref, da_ref, dv_ref, dcw_ref, dcb_ref):
        _, vjp = jax.vjp(functools.partial(_convglu_tile, n_lat), a_ref[...].astype(F32), v_ref[...].astype(F32),
                         cw_ref[...], cb_ref[...])
        da, dv, dcw_ref[...], dcb_ref[...] = vjp(dg_ref[...])
        da_ref[...], dv_ref[...] = da.astype(BF16), dv.astype(BF16)

    col = pl.BlockSpec((t, tc), lambda j: (0, j))
    return pl.pallas_call(
        body, name=name, grid=(nb,),
        in_specs=[col, pl.BlockSpec((t, tc), lambda j: (0, nb + j)), pl.BlockSpec((3, tc), lambda j: (0, j)),
                  pl.BlockSpec((1, tc), lambda j: (0, j)), col],
        out_specs=[col, col, pl.BlockSpec((3, tc), lambda j: (0, j)), pl.BlockSpec((1, tc), lambda j: (0, j))],
        out_shape=[jax.ShapeDtypeStruct((t, f), BF16), jax.ShapeDtypeStruct((t, f), BF16),
                   jax.ShapeDtypeStruct((3, f), F32), jax.ShapeDtypeStruct((1, f), F32)],
        compiler_params=_params(("parallel",)),
    )(u, u, cw, cb, dg)


def final_loss(x, target, g, n_lat):
    tm = ROW_TILE
    d = x.shape[1]

    def body(x_ref, t_ref, g_ref, loss_ref, dx_ref, dg_ref):
        i = pl.program_id(0)
        tgt = t_ref[...]

        def f(xv, gv):
            e = rms(xv) * gv - tgt
            s = jnp.sum(jnp.sum(e * e, axis=1, keepdims=True), axis=0, keepdims=True)
            return s * (0.5 / d)

        val, vjp = jax.vjp(f, x_ref[...], g_ref[...])
        dx, dgv = vjp(jnp.ones((1, 1), F32))
        dx_ref[...] = dx

        @pl.when(i == 0)
        def _():
            dg_ref[...] = dgv
            loss_ref[...] = jnp.broadcast_to(val, loss_ref.shape)

        @pl.when(i != 0)
        def _():
            dg_ref[...] += dgv
            loss_ref[...] += jnp.broadcast_to(val, loss_ref.shape)

    return pl.pallas_call(
        body, name="final_loss", grid=(n_lat // tm,),
        in_specs=[pl.BlockSpec((tm, d), lambda i: (i, 0)), pl.BlockSpec((tm, d), lambda i: (i, 0)),
                  pl.BlockSpec((1, d), lambda i: (0, 0))],
        out_specs=[pl.BlockSpec((1, LANES), lambda i: (0, 0)), pl.BlockSpec((tm, d), lambda i: (i, 0)),
                   pl.BlockSpec((1, d), lambda i: (0, 0))],
        out_shape=[jax.ShapeDtypeStruct((1, LANES), F32), jax.ShapeDtypeStruct((n_lat, d), F32),
                   jax.ShapeDtypeStruct((1, d), F32)],
        compiler_params=_params(("arbitrary",)),
    )(x, target, g)


ATT_SCALE = HEAD_DIM ** -0.5
_AK_BLK = P_AK // HEAD_DIM
_AV_BLK = Z_AV // HEAD_DIM


def _att_specs(t, tq):
    gw = ATT_GROUP * HEAD_DIM
    q_spec = pl.BlockSpec((tq, gw), lambda kv, i: (i, kv))
    k_spec = pl.BlockSpec((t, HEAD_DIM), lambda kv, i: (0, _AK_BLK + kv))
    v_spec = pl.BlockSpec((t, HEAD_DIM), lambda kv, i: (0, _AV_BLK + kv))
    row_spec = pl.BlockSpec((ATT_GROUP, tq, 1), lambda kv, i: (kv, i, 0))
    return q_spec, k_spec, v_spec, row_spec


def _att_mask(i, t, tq, n_lat):
    col = lax.broadcasted_iota(jnp.int32, (1, t), 1)
    return jnp.where((i >= n_lat // tq) & (col < n_lat), -jnp.inf, 0.0).astype(F32)


def attn_fwd(p, z, n_lat):
    t = p.shape[0]
    tq = ROW_TILE

    def body(q_ref, k_ref, v_ref, o_ref, lse_ref):
        mask = _att_mask(pl.program_id(1), t, tq, n_lat)
        k, v = k_ref[...].astype(BF16), v_ref[...].astype(BF16)
        for g in range(ATT_GROUP):
            cols = slice(g * HEAD_DIM, (g + 1) * HEAD_DIM)
            s = _dg(q_ref[:, cols], k, _NT) * ATT_SCALE + mask
            m = jnp.max(s, axis=1, keepdims=True)
            pr = jnp.exp(s - m)
            l = jnp.sum(pr, axis=1, keepdims=True)
            o_ref[:, cols] = _dg(pr, v, _NN) / l
            lse_ref[g] = m + jnp.log(l)

    q_spec, k_spec, v_spec, row_spec = _att_specs(t, tq)
    return pl.pallas_call(
        body, name="attn_fwd", grid=(ATT_KV_HEADS, t // tq),
        in_specs=[q_spec, k_spec, v_spec], out_specs=[q_spec, row_spec],
        out_shape=[jax.ShapeDtypeStruct((t, ATT_Q_HEADS * HEAD_DIM), F32),
                   jax.ShapeDtypeStruct((ATT_Q_HEADS, t, 1), F32)],
        compiler_params=_params(("parallel", "parallel")),
    )(p, p, z)


def attn_bwd(p, z, o, lse, do, n_lat):
    t = p.shape[0]
    tq = ROW_TILE

    def body(q_ref, k_ref, v_ref, o_ref, do_ref, lse_ref, dq_ref, dk_ref, dv_ref):
        i = pl.program_id(1)

        @pl.when(i == 0)
        def _():
            dk_ref[...] = jnp.zeros_like(dk_ref)
            dv_ref[...] = jnp.zeros_like(dv_ref)

        mask = _att_mask(i, t, tq, n_lat)
        k, v = k_ref[...].astype(BF16), v_ref[...].astype(BF16)
        dk, dv = dk_ref[...], dv_ref[...]
        for g in range(ATT_GROUP):
            cols = slice(g * HEAD_DIM, (g + 1) * HEAD_DIM)
            q, do_g = q_ref[:, cols].astype(BF16), do_ref[:, cols]
            pr = jnp.exp(_dg(q, k, _NT) * ATT_SCALE + mask - lse_ref[g])
            delta = jnp.sum(o_ref[:, cols] * do_g, axis=1, keepdims=True)
            ds = pr * (_dg(do_g, v, _NT) - delta) * ATT_SCALE
            dq_ref[:, cols] = _dg(ds, k, _NN)
            dk = dk + _dg(ds, q, _TN)
            dv = dv + _dg(pr, do_g, _TN)
        dk_ref[...], dv_ref[...] = dk, dv

    q_spec, k_spec, v_spec, row_spec = _att_specs(t, tq)
    kv_out = pl.BlockSpec((t, HEAD_DIM), lambda kv, i: (0, kv))
    return pl.pallas_call(
        body, name="attn_bwd", grid=(ATT_KV_HEADS, t // tq),
        in_specs=[q_spec, k_spec, v_spec, q_spec, q_spec, row_spec],
        out_specs=[q_spec, kv_out, kv_out],
        out_shape=[jax.ShapeDtypeStruct((t, ATT_Q_HEADS * HEAD_DIM), F32),
                   jax.ShapeDtypeStruct((t, ATT_KV_HEADS * HEAD_DIM), F32),
                   jax.ShapeDtypeStruct((t, ATT_KV_HEADS * HEAD_DIM), F32)],
        compiler_params=_params(("parallel", "arbitrary")),
    )(p, p, z, o, do, lse)


_RQ_BLK = P_RQ // HEAD_DIM
_RK_BLK = P_RK // HEAD_DIM
_RV_BLK = Z_RV // HEAD_DIM


def _scan_chunk(direction, step, n_chunks, n_lat_chunks):
    return jnp.where(direction == 0, (step + n_lat_chunks) % n_chunks, n_chunks - 1 - step)


def _ret_geometry(direction):
    c = RET_CHUNK
    i = lax.broadcasted_iota(jnp.int32, (c, c), 0)
    j = lax.broadcasted_iota(jnp.int32, (c, c), 1)
    rel = jnp.where(direction == 0, i - j, j - i).astype(F32)
    r = lax.broadcasted_iota(jnp.int32, (c, 1), 0)
    pos = jnp.where(direction == 0, r, c - 1 - r).astype(F32)
    return rel, pos


def ret_chunk(q, k, v, s, lg, rel, pos):
    c = RET_CHUNK
    causal = rel >= 0
    d_in = jnp.where(causal, jnp.exp(lg * jnp.where(causal, rel, 0.0)), 0.0)
    q_dec = jnp.exp(lg * (pos + 1.0))
    k_dec = jnp.exp(lg * (c - 1.0 - pos))
    c_dec = jnp.exp(lg * c)
    att = bdot_nt(q, k) * d_in
    o = bdot(att, v) + bdot(q * q_dec, s)
    s_new = c_dec * s + bdot_tn(k * k_dec, v)
    return o, s_new


def ret_fwd(p, z, lg, n_lat):
    t = p.shape[0]
    c = RET_CHUNK
    nc, nlc = t // c, n_lat // c

    def body(q_ref, k_ref, v_ref, lg_ref, o_ref, ssave_ref, s_s):
        d, n = pl.program_id(0), pl.program_id(1)

        @pl.when(n == 0)
        def _():
            s_s[...] = jnp.zeros_like(s_s)

        rel, pos = _ret_geometry(d)
        for h in range(RET_HEADS):
            cols = slice(h * HEAD_DIM, (h + 1) * HEAD_DIM)
            ssave_ref[0, h, 0] = s_s[h]
            o, s_new = ret_chunk(q_ref[:, cols], k_ref[:, cols], v_ref[:, cols].astype(F32), s_s[h], lg_ref[0, h], rel, pos)
            o_ref[:, cols] = o
            s_s[h] = s_new

    w = RET_HEADS * HEAD_DIM

    def blk(base):
        return pl.BlockSpec((c, w), lambda d, n: (_scan_chunk(d, n, nc, nlc), base // RET_HEADS))

    return pl.pallas_call(
        body, name="ret_fwd", grid=(2, nc),
        in_specs=[blk(_RQ_BLK), blk(_RK_BLK), blk(_RV_BLK), pl.BlockSpec((1, RET_HEADS, 1, 1), lambda d, n: (d, 0, 0, 0))],
        out_specs=[pl.BlockSpec((c, w), lambda d, n: (_scan_chunk(d, n, nc, nlc), d)),
                   pl.BlockSpec((1, RET_HEADS, 1, HEAD_DIM, HEAD_DIM), lambda d, n: (d, 0, n, 0, 0))],
        out_shape=[jax.ShapeDtypeStruct((t, 2 * w), F32),
                   jax.ShapeDtypeStruct((2, RET_HEADS, nc, HEAD_DIM, HEAD_DIM), F32)],
        scratch_shapes=[pltpu.VMEM((RET_HEADS, HEAD_DIM, HEAD_DIM), F32)],
        compiler_params=_params(("parallel", "arbitrary")),
    )(p, p, z, lg)


def ret_bwd(p, z, lg, states, do, n_lat):
    t = p.shape[0]
    c = RET_CHUNK
    nc, nlc = t // c, n_lat // c

    def body(q_ref, k_ref, v_ref, lg_ref, s_ref, do_ref, dq_ref, dk_ref, dv_ref, dlg_ref, ds_s):
        d, n = pl.program_id(0), pl.program_id(1)

        @pl.when(n == 0)
        def _():
            ds_s[...] = jnp.zeros_like(ds_s)
            dlg_ref[...] = jnp.zeros_like(dlg_ref)

        rel, pos = _ret_geometry(d)
        f = functools.partial(ret_chunk, rel=rel, pos=pos)
        for h in range(RET_HEADS):
            cols = slice(h * HEAD_DIM, (h + 1) * HEAD_DIM)
            _, vjp = jax.vjp(f, q_ref[:, cols], k_ref[:, cols], v_ref[:, cols].astype(F32), s_ref[0, h, 0], lg_ref[0, h])
            dq, dk, dv, ds, dlg = vjp((do_ref[:, cols], ds_s[h]))
            dq_ref[:, cols], dk_ref[:, cols], dv_ref[:, cols] = dq, dk, dv
            ds_s[h] = ds
            dlg_ref[0, h] += dlg

    def chunk_of(d, n):
        return _scan_chunk(d, nc - 1 - n, nc, nlc)

    w = RET_HEADS * HEAD_DIM

    def blk(base):
        return pl.BlockSpec((c, w), lambda d, n: (chunk_of(d, n), base // RET_HEADS))

    out_blk = pl.BlockSpec((c, w), lambda d, n: (chunk_of(d, n), d))
    lg_blk = pl.BlockSpec((1, RET_HEADS, 1, 1), lambda d, n: (d, 0, 0, 0))
    grad_shape = jax.ShapeDtypeStruct((t, 2 * w), F32)
    return pl.pallas_call(
        body, name="ret_bwd", grid=(2, nc),
        in_specs=[blk(_RQ_BLK), blk(_RK_BLK), blk(_RV_BLK), lg_blk,
                  pl.BlockSpec((1, RET_HEADS, 1, HEAD_DIM, HEAD_DIM), lambda d, n: (d, 0, nc - 1 - n, 0, 0)),
                  pl.BlockSpec((c, w), lambda d, n: (chunk_of(d, n), 0))],
        out_specs=[out_blk, out_blk, out_blk, lg_blk],
        out_shape=[grad_shape, grad_shape, grad_shape, jax.ShapeDtypeStruct((2, RET_HEADS, 1, 1), F32)],
        scratch_shapes=[pltpu.VMEM((RET_HEADS, HEAD_DIM, HEAD_DIM), F32)],
        compiler_params=_params(("parallel", "arbitrary")),
    )(p, p, z, lg, states, do)


_GQ_BLK = P_GQ // (GLA_HEADS * GLA_DK)
_GK_BLK = Z_GK // (GLA_HEADS * GLA_DK)
_GV_BLK = Z_GV // (GLA_HEADS * GLA_DV)
_LA_BLK = P_LA // (GLA_HEADS * GLA_DK)


def _gla_mask(direction):
    c = GLA_CHUNK
    i = lax.broadcasted_iota(jnp.int32, (c, c), 0)
    j = lax.broadcasted_iota(jnp.int32, (c, c), 1)
    return (jnp.where(direction == 0, i - j, j - i) >= 0).astype(F32)


def gla_chunk(q, k, v, la, st, mask):
    b = mask_cumsum(mask, la)
    btot = jnp.sum(la, axis=0, keepdims=True)
    half = 0.5 * btot
    qt, kt = q * jnp.exp(b - half), k * jnp.exp(half - b)
    qs, ke = q * jnp.exp(b), k * jnp.exp(btot - b)
    outs, upd = [], []
    for h in range(GLA_HEADS):
        ks = slice(h * GLA_DK, (h + 1) * GLA_DK)
        vh = v[:, h * GLA_DV:(h + 1) * GLA_DV]
        att = bdot_nt(qt[:, ks], kt[:, ks]) * mask
        outs.append(bdot(att, vh) + bdot_nt(qs[:, ks], st[:, ks]))
        upd.append(bdot_tn(vh, ke[:, ks]))
    st_new = st * jnp.exp(btot) + jnp.concatenate(upd, axis=1)
    return jnp.concatenate(outs, axis=1), st_new


def gla_fwd(p, z, n_lat):
    t = p.shape[0]
    c = GLA_CHUNK
    nc, nlc = t // c, n_lat // c
    kw, vw = GLA_HEADS * GLA_DK, GLA_HEADS * GLA_DV

    def body(q_ref, k_ref, v_ref, la_ref, o_ref, ssave_ref, s_s):
        d, n = pl.program_id(0), pl.program_id(1)

        @pl.when(n == 0)
        def _():
            s_s[...] = jnp.zeros_like(s_s)

        ssave_ref[0, 0] = s_s[...]
        o, s_new = gla_chunk(q_ref[...], k_ref[...].astype(F32), v_ref[...].astype(F32), la_ref[...], s_s[...], _gla_mask(d))
        o_ref[...] = o
        s_s[...] = s_new

    def chunk_of(d, n):
        return _scan_chunk(d, n, nc, nlc)

    return pl.pallas_call(
        body, name="gla_fwd", grid=(2, nc),
        in_specs=[pl.BlockSpec((c, kw), lambda d, n: (chunk_of(d, n), _GQ_BLK)),
                  pl.BlockSpec((c, kw), lambda d, n: (chunk_of(d, n), _GK_BLK)),
                  pl.BlockSpec((c, vw), lambda d, n: (chunk_of(d, n), _GV_BLK)),
                  pl.BlockSpec((c, kw), lambda d, n: (chunk_of(d, n), _LA_BLK + d))],
        out_specs=[pl.BlockSpec((c, vw), lambda d, n: (chunk_of(d, n), d)),
                   pl.BlockSpec((1, 1, GLA_DV, kw), lambda d, n: (d, n, 0, 0))],
        out_shape=[jax.ShapeDtypeStruct((t, 2 * vw), F32), jax.ShapeDtypeStruct((2, nc, GLA_DV, kw), F32)],
        scratch_shapes=[pltpu.VMEM((GLA_DV, kw), F32)],
        compiler_params=_params(("parallel", "arbitrary")),
    )(p, z, z, p)


def gla_bwd(p, z, states, do, n_lat):
    t = p.shape[0]
    c = GLA_CHUNK
    nc, nlc = t // c, n_lat // c
    kw, vw = GLA_HEADS * GLA_DK, GLA_HEADS * GLA_DV

    def body(q_ref, k_ref, v_ref, la_ref, s_ref, do_ref, dq_ref, dk_ref, dv_ref, dla_ref, ds_s):
        d, n = pl.program_id(0), pl.program_id(1)

        @pl.when(n == 0)
        def _():
            ds_s[...] = jnp.zeros_like(ds_s)

        f = functools.partial(gla_chunk, mask=_gla_mask(d))
        _, vjp = jax.vjp(f, q_ref[...], k_ref[...].astype(F32), v_ref[...].astype(F32), la_ref[...], s_ref[0, 0])
        dq_ref[...], dk_ref[...], dv_ref[...], dla_ref[...], ds_s[...] = vjp((do_ref[...], ds_s[...]))

    def chunk_of(d, n):
        return _scan_chunk(d, nc - 1 - n, nc, nlc)

    k_out = pl.BlockSpec((c, kw), lambda d, n: (chunk_of(d, n), d))
    return pl.pallas_call(
        body, name="gla_bwd", grid=(2, nc),
        in_specs=[pl.BlockSpec((c, kw), lambda d, n: (chunk_of(d, n), _GQ_BLK)),
                  pl.BlockSpec((c, kw), lambda d, n: (chunk_of(d, n), _GK_BLK)),
                  pl.BlockSpec((c, vw), lambda d, n: (chunk_of(d, n), _GV_BLK)),
                  pl.BlockSpec((c, kw), lambda d, n: (chunk_of(d, n), _LA_BLK + d)),
                  pl.BlockSpec((1, 1, GLA_DV, kw), lambda d, n: (d, nc - 1 - n, 0, 0)),
                  pl.BlockSpec((c, vw), lambda d, n: (chunk_of(d, n), 0))],
        out_specs=[k_out, k_out, pl.BlockSpec((c, vw), lambda d, n: (chunk_of(d, n), d)), k_out],
        out_shape=[jax.ShapeDtypeStruct((t, 2 * kw), F32), jax.ShapeDtypeStruct((t, 2 * kw), F32),
                   jax.ShapeDtypeStruct((t, 2 * vw), F32), jax.ShapeDtypeStruct((t, 2 * kw), F32)],
        scratch_shapes=[pltpu.VMEM((GLA_DV, kw), F32)],
        compiler_params=_params(("parallel", "arbitrary")),
    )(p, z, z, p, states, do)


def _adam_tile(w, g, m, v):
    m = ADAM_B1 * m + (1.0 - ADAM_B1) * g
    v = ADAM_B2 * v + (1.0 - ADAM_B2) * (g * g)
    m_hat = m / (1.0 - ADAM_B1 ** ADAM_STEP)
    v_hat = v / (1.0 - ADAM_B2 ** ADAM_STEP)
    delta = -ADAM_LR * (m_hat / (jnp.sqrt(v_hat) + ADAM_EPS) + ADAM_WD * w)
    return delta, m, v


def adamw(name, w, g, m, v):
    shape = w.shape
    cols = shape[-1] if w.ndim > 1 and shape[-1] >= LANES else int(np.prod(shape))
    rows = int(np.prod(shape)) // cols
    tr = rows
    for cand in (512, 256, 128, 64, 32, 16, 8):
        if rows % cand == 0 and cand * cols * 4 <= (1 << 21):
            tr = cand
            break
    flat = [a.reshape(rows, cols) for a in (w, g, m, v)]

    def body(w_ref, g_ref, m_ref, v_ref, d_ref, mo_ref, vo_ref):
        d_ref[...], mo_ref[...], vo_ref[...] = _adam_tile(w_ref[...], g_ref[...], m_ref[...], v_ref[...])

    spec = pl.BlockSpec((tr, cols), lambda i: (i, 0))
    outs = pl.pallas_call(
        body, name=name, grid=(rows // tr,),
        in_specs=[spec] * 4, out_specs=[spec] * 3,
        out_shape=[jax.ShapeDtypeStruct((rows, cols), F32)] * 3,
        compiler_params=_params(("parallel",)),
    )(*flat)
    return tuple(o.reshape(shape) for o in outs)


def adamw_layers(name, w, grads, m, v):
    depth, rows, cols = w.shape
    tr = _rows_tile(rows, cols)
    nb = rows // tr

    def body(w_ref, m_ref, v_ref, *rest):
        g_refs, (g_ref, d_ref, mo_ref, vo_ref) = rest[:depth], rest[depth:]
        l = pl.program_id(0)
        for k in range(depth):
            @pl.when(l == k)
            def _():
                g = g_refs[k][...]
                g_ref[...] = g
                d_ref[...], mo_ref[...], vo_ref[...] = _adam_tile(w_ref[...], g, m_ref[...], v_ref[...])

    def layer_grad(k):
        return pl.BlockSpec((tr, cols), lambda l, i: (jnp.where(l < k, 0, jnp.where(l == k, i, nb - 1)), 0))

    spec = pl.BlockSpec((tr, cols), lambda l, i: (l * nb + i, 0))
    flat = [a.reshape(depth * rows, cols) for a in (w, m, v)]
    g_all, delta, new_m, new_v = pl.pallas_call(
        body, name=name, grid=(depth, nb),
        in_specs=[spec] * 3 + [layer_grad(k) for k in range(depth)], out_specs=[spec] * 4,
        out_shape=[jax.ShapeDtypeStruct((depth * rows, cols), F32)] * 4,
        compiler_params=_params(("arbitrary", "arbitrary")),
    )(*flat, *grads)
    return tuple(a.reshape(w.shape) for a in (delta, new_m, new_v)), g_all.reshape(w.shape)


MESH = pl.DeviceIdType.MESH
_HBM = pl.BlockSpec(memory_space=pltpu.HBM)
N_CHIPS = 4
N_DEV = 8


def _place():
    x, y, c = lax.axis_index("x"), lax.axis_index("y"), lax.axis_index("c")
    chips = [(1 - x, y), (x, 1 - y), (1 - x, 1 - y)]
    return x, y, c, chips


def _remote(src, dst, send_sem, recv_sem, to):
    return pltpu.make_async_remote_copy(src_ref=src, dst_ref=dst, send_sem=send_sem, recv_sem=recv_sem,
                                        device_id=to, device_id_type=MESH)


def all_gather_small(name, v):
    m_per, n = v.shape

    def body(x_ref, out_ref, send_sems, recv_sems, local_sem):
        x, y, c, chips = _place()
        me, sibling = (x, y, c), (x, y, 1 - c)

        def rows(px, py, pc):
            return out_ref.at[pl.ds((4 * px + 2 * py + pc) * m_per, m_per), :]

        def copy(k, block, to, src=None):
            return _remote(rows(*block) if src is None else src, rows(*block), send_sems.at[k], recv_sems.at[k], to)

        mine = pltpu.make_async_copy(x_ref, rows(*me), local_sem)
        mine.start()
        first = [copy(0, me, sibling, src=x_ref)]
        first += [copy(1 + j, me, (*chip, c), src=x_ref) for j, chip in enumerate(chips)]
        for cp in first:
            cp.start()
        passed = [copy(4 + j, (*chip, c), sibling) for j, chip in enumerate(chips)]
        for j, chip in enumerate(chips):
            copy(1 + j, (*chip, c), me).wait_recv()
            passed[j].start()
        copy(0, sibling, me).wait_recv()
        for j, chip in enumerate(chips):
            copy(4 + j, (*chip, 1 - c), me).wait_recv()
        for cp in first + passed:
            cp.wait_send()
        mine.wait()

    return pl.pallas_call(
        body, name=name,
        out_shape=jax.ShapeDtypeStruct((N_DEV * m_per, n), v.dtype),
        in_specs=[pl.BlockSpec(memory_space=pltpu.VMEM)],
        out_specs=pl.BlockSpec(memory_space=pltpu.VMEM),
        scratch_shapes=[pltpu.SemaphoreType.DMA((7,)), pltpu.SemaphoreType.DMA((7,)), pltpu.SemaphoreType.DMA],
        compiler_params=pltpu.CompilerParams(vmem_limit_bytes=VMEM_LIMIT),
    )(v)


_SEM = pl.BlockSpec(memory_space=pltpu.SEMAPHORE)
_SPLIT_COPY = pltpu.CompilerParams(has_side_effects=pltpu.SideEffectType.DATAFLOW_SIDE_EFFECTING)


class CopyPlan(NamedTuple):
    copies: object
    n: int
    in_place: bool = False


def _gather_copies(x_ref, land_ref, x, y, c, chips):
    half = x_ref.shape[0] // 2
    rows = pl.ds(c * half, half)
    return [(x_ref.at[rows, :], land_ref.at[2 * x + y, rows, :], (*chip, c), land_ref.at[2 * chip[0] + chip[1], rows, :])
            for chip in chips]


def _pass_copies(land_ref, _, x, y, c, chips):
    half = land_ref.shape[1] // 2
    mine, other = pl.ds(c * half, half), pl.ds((1 - c) * half, half)
    return [(land_ref.at[2 * chip[0] + chip[1], mine, :], land_ref.at[2 * chip[0] + chip[1], mine, :], (x, y, 1 - c),
             land_ref.at[2 * chip[0] + chip[1], other, :]) for chip in chips]


def _sibling_half_copies(p_ref, land_ref, x, y, c, chips):
    half = p_ref.shape[1] // 2
    return [(p_ref.at[:, pl.ds((1 - c) * half, half), :], land_ref, (x, y, 1 - c), land_ref)]


def _scatter_copies(s_ref, land_ref, x, y, c, chips):
    return [(s_ref.at[2 * chip[0] + chip[1]], land_ref.at[j], (*chip, c), land_ref.at[j]) for j, chip in enumerate(chips)]


def _join_copies(buf_ref, _, x, y, c, chips):
    half = buf_ref.shape[0] // 2
    mine = buf_ref.at[pl.ds(c * half, half), :]
    return [(mine, mine, (x, y, 1 - c), buf_ref.at[pl.ds((1 - c) * half, half), :])]


GATHER = CopyPlan(_gather_copies, 3)
PASS_ON = CopyPlan(_pass_copies, 3, in_place=True)
SIBLING_HALF = CopyPlan(_sibling_half_copies, 1)
SCATTER = CopyPlan(_scatter_copies, 3)
JOIN = CopyPlan(_join_copies, 1, in_place=True)


def split_start(name, plan, srcs, land_shapes=None, after=None):
    nt = len(srcs)
    arrays = [pltpu.with_memory_space_constraint(s, pltpu.HBM) for s in srcs]
    if not plan.in_place:
        arrays += [pltpu.with_memory_space_constraint(lax.empty(shape, s.dtype), pltpu.HBM) for shape, s in zip(land_shapes, srcs)]
    na = len(arrays)
    behind = [] if after is None else [after]
    n_in = na + len(behind)

    def body(*refs):
        x_refs = refs[:nt]
        land_refs = x_refs if plan.in_place else refs[nt:na]
        send, recv = refs[n_in:n_in + nt], refs[n_in + nt:n_in + 2 * nt]
        x, y, c, chips = _place()
        for t in range(nt):
            for j, (src, dst, to, _) in enumerate(plan.copies(x_refs[t], land_refs[t], x, y, c, chips)):
                _remote(src, dst, send[t].at[j], recv[t].at[j], to).start()
        refs[-1][...] = jnp.zeros_like(refs[-1])

    outs = pl.pallas_call(
        body, name=name,
        out_shape=tuple([pltpu.SemaphoreType.DMA((plan.n,))] * (2 * nt) + [pltpu.HBM(a.shape, a.dtype) for a in arrays]
                        + [jax.ShapeDtypeStruct((8, LANES), F32)]),
        in_specs=[_HBM] * na + [pl.BlockSpec(memory_space=pl.ANY)] * len(behind),
        out_specs=tuple([_SEM] * (2 * nt) + [_HBM] * na + [pl.BlockSpec(memory_space=pltpu.VMEM)]),
        input_output_aliases={i: 2 * nt + i for i in range(na)},
        compiler_params=_SPLIT_COPY,
    )(*arrays, *behind)
    groups = [(outs[t], outs[nt + t]) + tuple(outs[2 * nt + t + k * nt] for k in range(na // nt)) for t in range(nt)]
    return groups, outs[-1]


def split_wait(name, plan, group, after):
    send, recv, *arrays = group
    na = len(arrays)

    def body(*refs):
        x_ref, land_ref = refs[0], refs[na - 1]
        send_sem, recv_sem = refs[na], refs[na + 1]
        x, y, c, chips = _place()
        for j, (s, _, to, arrival) in enumerate(plan.copies(x_ref, land_ref, x, y, c, chips)):
            cp = _remote(s, arrival, send_sem.at[j], recv_sem.at[j], to)
            cp.wait_send()
            cp.wait_recv()

    return pl.pallas_call(
        body, name=name,
        out_shape=tuple(pltpu.HBM(a.shape, a.dtype) for a in arrays),
        in_specs=tuple([_HBM] * na + [_SEM, _SEM, pl.BlockSpec(memory_space=pl.ANY)]), out_specs=tuple([_HBM] * na),
        input_output_aliases={i: i for i in range(na)}, compiler_params=_SPLIT_COPY,
    )(*arrays, send, recv, after)


def _rows_tile(rows, cols):
    for cand in (512, 256, 128, 64, 32, 16):
        if rows % cand == 0 and cand * cols * 4 <= (1 << 21):
            return cand
    return rows


def add_sibling_half(name, pieces, from_sibling, core):
    n, h, cols = from_sibling.shape
    tr = _rows_tile(h, cols // 4)
    nb = h // tr

    def body(c_ref, a_ref, b_ref, o_ref):
        o_ref[...] = (a_ref[...].astype(F32) + b_ref[...].astype(F32)).astype(o_ref.dtype)

    blk = pl.BlockSpec((1, tr, cols), lambda q, i, c_ref: (q, i, 0))
    return pl.pallas_call(
        body, name=name,
        grid_spec=pltpu.PrefetchScalarGridSpec(
            num_scalar_prefetch=1, grid=(n, nb),
            in_specs=[pl.BlockSpec((1, tr, cols), lambda q, i, c_ref: (q, c_ref[0] * nb + i, 0)), blk], out_specs=blk),
        out_shape=jax.ShapeDtypeStruct((n, h, cols), BF16),
        compiler_params=_params(("parallel", "parallel")),
    )(core.reshape(1).astype(jnp.int32), pieces, from_sibling)


def add_chip_sums(name, chip_sums, from_chips, chip, core):
    _, h, cols = chip_sums.shape
    tr = _rows_tile(h, cols)
    nb = h // tr

    def body(s_ref, own_ref, r0_ref, r1_ref, r2_ref, o_ref):
        acc = own_ref[0].astype(F32) + r0_ref[0].astype(F32)
        o_ref[...] = acc + r1_ref[0].astype(F32) + r2_ref[0].astype(F32)

    def got(j):
        return pl.BlockSpec((1, tr, cols), lambda i, s_ref: (j, i, 0))

    return pl.pallas_call(
        body, name=name,
        grid_spec=pltpu.PrefetchScalarGridSpec(
            num_scalar_prefetch=1, grid=(nb,),
            in_specs=[pl.BlockSpec((1, tr, cols), lambda i, s_ref: (s_ref[0], i, 0)), got(0), got(1), got(2)],
            out_specs=pl.BlockSpec((tr, cols), lambda i, s_ref: (s_ref[1] * nb + i, 0))),
        out_shape=jax.ShapeDtypeStruct((2 * h, cols), F32),
        compiler_params=_params(("parallel",)),
    )(jnp.stack([chip, core]).astype(jnp.int32), chip_sums, from_chips, from_chips, from_chips)


def sum_device_blocks(name, g):
    n = g.shape[1]

    def body(g_ref, o_ref):
        acc = g_ref[0:8, :]
        for d in range(1, N_DEV):
            acc = acc + g_ref[8 * d:8 * (d + 1), :]
        o_ref[...] = acc

    return pl.pallas_call(body, name=name, out_shape=jax.ShapeDtypeStruct((8, n), F32),
                          compiler_params=pltpu.CompilerParams(vmem_limit_bytes=VMEM_LIMIT))(g)


class LayerWeights(NamedTuple):
    norm1_g: jax.Array
    q_g: jax.Array
    k_g: jax.Array
    lg: jax.Array
    ret_g: jax.Array
    gate_up: jax.Array
    gate_b: jax.Array
    gla_g: jax.Array
    norm2_g: jax.Array
    conv_w: jax.Array
    conv_b: jax.Array


def _mod(mods, k):
    return mods[:, k:k + 1, :]


def out_view(l, tb):
    rows = D_MODEL // N_CHIPS
    if tb:
        return BView(n=D_MODEL, k=D_MODEL, tn=rows, tk=D_MODEL, index_map=lambda i, j, kk: (j, l, kk))
    chips = tuple(functools.partial(lambda i, j, kk, q: (q, l, j), q=q) for q in range(N_CHIPS))
    return BView(n=D_MODEL, k=D_MODEL, tn=1024, tk=rows, index_map=None, part_maps=chips)


def down_view(l, f, tb):
    rows = f // N_CHIPS
    if tb:
        return BView(n=f, k=D_MODEL, tn=rows, tk=D_MODEL, index_map=lambda i, j, kk: (j, l, kk))
    chips = tuple(functools.partial(lambda i, j, kk, q: (q, l, j), q=q) for q in range(N_CHIPS))
    return BView(n=D_MODEL, k=f, tn=512, tk=rows, index_map=None, part_maps=chips)


def up_view(l, f, part=None):
    cols = 2 * f // N_CHIPS
    tc = _pick(cols, (1408, 1024, 512, 256))
    nbc = cols // tc
    if part is None:
        return BView(n=2 * f, k=D_MODEL, tn=tc, tk=D_MODEL, index_map=lambda i, j, kk: (j // nbc, l, j % nbc))
    nnb = D_MODEL // 512
    tiles = tuple(functools.partial(lambda i, j, kk, p: (2 * part + p // nbc, l * nnb + j, p % nbc), p=p) for p in range(2 * nbc))
    return BView(n=D_MODEL, k=f, tn=512, tk=tc, index_map=None, part_maps=tiles)


def up_grad_view(f, part, into):
    cols = f // 2
    tn = _pick(cols, (1408, 1024, 512, 256))
    nbc = cols // tn
    return OView((N_CHIPS, D_MODEL, cols), lambda i, j, kk: (2 * part + j // nbc, i, j % nbc), tn, into)


def ada_view(l, n_ada, tb):
    if tb:
        return BView(n=D_MODEL, k=n_ada, tn=1024, tk=n_ada, index_map=lambda i, j, kk: (l, j, 0))
    return BView(n=n_ada, k=D_MODEL, tn=1024, tk=D_MODEL, index_map=lambda i, j, kk: (l, 0, j))


def _prep_args(z, zg, cos, sin, w):
    rows = [Row(z, Z_AV, 0), Row(z, 512, Z_RQ // 512), Row(z, 512, Z_RK // 512), Row(z, 256, Z_GQ // 256),
            Row(zg, LANES, 0), Row(cos, HEAD_DIM, 0, False), Row(sin, HEAD_DIM, 0, False)]
    return rows, [Par(w.q_g), Par(w.k_g), Par(w.gate_up), Par(w.gate_b)]


def _post_args(o_att, o_ret, o_gla, z, w):
    rows = [Row(o_att, 1024), Row(o_ret, 512, 0), Row(o_ret, 512, 1, False), Row(o_gla, 512, 0), Row(o_gla, 512, 1, False),
            Row(z, 512, Z_RG // 512), Row(z, 512, Z_GR // 512)]
    return rows, [Par(w.ret_g), Par(w.gla_g)]


def layer_fwd(l, xs, mods, w, fetch, cos, sin, n_lat, n_out):
    t, d = xs.shape
    tag = f"l{l}_"
    nm1 = [Par(w.norm1_g), Par(_mod(mods, 0), True), Par(_mod(mods, 1), True)]
    (h,) = row_map(tag + "norm1", normmod_tile, [Row(xs, d)], nm1, [(d, BF16)], t, n_lat)
    (w_main, w_gate), started = fetch("w_in", h)
    z = matmul(tag + "in_proj", h, w_main, after=started, out_dtype=BF16)
    zg = matmul(tag + "gate_proj", h, w_gate)
    rows, pars = _prep_args(z, zg, cos, sin, w)
    (p,) = row_map(tag + "prep", prep_tile, rows, pars, [(P_W, F32)], t, n_lat)
    o_att, lse = attn_fwd(p, z, n_lat)
    o_ret, s_ret = ret_fwd(p, z, w.lg, n_lat)
    o_gla, s_gla = gla_fwd(p, z, n_lat)
    rows, pars = _post_args(o_att, o_ret, o_gla, z, w)
    (m,) = row_map(tag + "post", post_tile, rows, pars, [(d, BF16)], t, n_lat)
    m = m[:n_out]
    g_out, started = fetch("w_out", m)
    y = matmul(tag + "out_proj", m, g_out, view=out_view(0, False), after=started)
    rn = [Par(_mod(mods, 2), True), Par(w.norm2_g), Par(_mod(mods, 3), True), Par(_mod(mods, 4), True)]
    x1, h2 = row_map(tag + "resid1_norm2", resid_norm_tile, [Row(xs, d), Row(y, d)], rn, [(d, F32), (d, BF16)], n_out, n_lat)
    f = w.conv_b.shape[1]
    g_up, started = fetch("w_up", h2)
    u = matmul(tag + "up_proj", h2, g_up, view=up_view(0, f), after=started, out_dtype=BF16)
    g = convglu(tag + "convglu", u, w.conv_w, w.conv_b, n_lat)
    g_down, started = fetch("w_down", g)
    yd = matmul(tag + "down_proj", g, g_down, view=down_view(0, f, False), after=started)
    (x2,) = row_map(tag + "resid2", resid_tile, [Row(x1, d), Row(yd, d)], [Par(_mod(mods, 5), True)], [(d, F32)], n_out, n_lat)
    saved = dict(xs=xs, h=h, z=z, zg=zg, p=p, o_att=o_att, lse=lse, o_ret=o_ret, s_ret=s_ret, o_gla=o_gla, s_gla=s_gla,
                 m=m, y=y, x1=x1, h2=h2, u=u, g=g, yd=yd, w_main=w_main, w_gate=w_gate, g_out=g_out, g_up=g_up, g_down=g_down)
    return x2, saved


def _sum_dirs(a):
    w = a.shape[1] // 2
    return a[:, :w] + a[:, w:]


def layer_bwd(l, dx2, s, mods, w, cos, sin, n_lat, grad_ready):
    (t, d), n_out = s["xs"].shape, dx2.shape[0]
    tag = f"l{l}_b_"

    def all_rows(a):
        return a if n_out == t else jnp.pad(a, ((0, t - n_out), (0, 0)))

    dyd, dgate5 = row_vjp(tag + "resid2", gated_tile, [Row(s["yd"], d)], [Par(_mod(mods, 5), True)], [dx2], n_out, n_lat,
                          row_grad_dtype=BF16)
    f = w.conv_b.shape[1]
    dg = matmul(tag + "down_dx", dyd, s["g_down"], tb=True, view=down_view(0, f, True))
    dw_down = matmul(tag + "down_dw", s["g"], dyd, ta=True, out_dtype=BF16)
    da, dv, dcw, dcb = convglu_bwd(tag + "convglu", s["u"], w.conv_w, w.conv_b, dg, n_lat)
    dh2 = matmul(tag + "up_dx_gate", da, s["g_up"], tb=True, view=up_view(0, f, 0))
    dh2 = matmul(tag + "up_dx_value", dv, s["g_up"], tb=True, view=up_view(0, f, 1), add=dh2)
    dw_up = matmul(tag + "up_dw_gate", s["h2"], da, ta=True, out_dtype=BF16, o_view=up_grad_view(f, 0, None))
    dw_up = matmul(tag + "up_dw_value", s["h2"], dv, ta=True, out_dtype=BF16, o_view=up_grad_view(f, 1, dw_up))
    started = grad_ready("ffn", dict(w_up=dw_up, w_down=dw_down))
    rn = [Par(_mod(mods, 2), True), Par(w.norm2_g), Par(_mod(mods, 3), True), Par(_mod(mods, 4), True)]
    dx1, dy, dgate2, dg2, dshift3, dscale4 = row_vjp(
        tag + "resid1_norm2", resid_norm_tile, [Row(s["xs"], d), Row(s["y"], d)], rn, [dx2, dh2], n_out, n_lat,
        row_grad_dtype=(F32, BF16), after=started)
    if n_out < t:
        dgate5, dshift3, dscale4, dgate2 = [g.at[1].set(0.0) for g in (dgate5, dshift3, dscale4, dgate2)]
    dm = matmul(tag + "out_dx", dy, s["g_out"], tb=True, view=out_view(0, True))
    dw_out = matmul(tag + "out_dw", s["m"], dy, ta=True, out_dtype=BF16)
    rows, pars = _post_args(s["o_att"], s["o_ret"], s["o_gla"], s["z"], w)
    started = grad_ready("w_out", dict(w_out=dw_out))
    do_att, do_ret, do_gla, d_rg, d_gr, d_ret_g, d_gla_g = row_vjp(tag + "post", post_tile, rows, pars, [dm], n_out, n_lat, after=started)
    do_att, do_ret, do_gla, d_rg, d_gr, dx1 = [all_rows(a) for a in (do_att, do_ret, do_gla, d_rg, d_gr, dx1)]
    dq_a, dk_a, dv_a = attn_bwd(s["p"], s["z"], s["o_att"], s["lse"], do_att, n_lat)
    dq_r, dk_r, dv_r, dlg = ret_bwd(s["p"], s["z"], w.lg, s["s_ret"], do_ret, n_lat)
    dq_g, dk_g, dv_g, dla = gla_bwd(s["p"], s["z"], s["s_gla"], do_gla, n_lat)
    dp = jnp.concatenate([dq_a, dk_a, _sum_dirs(dq_g), _sum_dirs(dq_r), _sum_dirs(dk_r), dla], axis=1)
    rows, pars = _prep_args(s["z"], s["zg"], cos, sin, w)
    d_zqk, d_zrq, d_zrk, d_zgq, dzg, d_qg, d_kg, d_up, d_gb = row_vjp(tag + "prep", prep_tile, rows, pars, [dp], t, n_lat)
    dz = jnp.concatenate([d_zqk, dv_a, d_zrq, d_zrk, _sum_dirs(dv_r), d_rg, d_zgq, _sum_dirs(dk_g), _sum_dirs(dv_g), d_gr], axis=1)
    dz, dzg = dz.astype(BF16), dzg.astype(BF16)
    dh_gate = matmul(tag + "gate_dx", dzg, s["w_gate"], tb=True)
    dh = matmul(tag + "in_dx", dz, s["w_main"], tb=True, add=dh_gate)
    dw_main = matmul(tag + "in_dw", s["h"], dz, ta=True, out_dtype=BF16)
    dw_gate = matmul(tag + "gate_dw", s["h"], dzg, ta=True, out_dtype=BF16)
    started = grad_ready("w_in", dict(w_main=dw_main, w_gate=dw_gate))
    nm1 = [Par(w.norm1_g), Par(_mod(mods, 0), True), Par(_mod(mods, 1), True)]
    dx, dg1, dshift0, dscale1 = row_vjp(tag + "norm1", normmod_tile, [Row(s["xs"], d)], nm1, [dh], t, n_lat,
                                        add_to_first=dx1, after=started)
    dmods = jnp.concatenate([dshift0, dscale1, dgate2, dshift3, dscale4, dgate5], axis=1)
    grads = dict(w_main=dw_main, w_gate=dw_gate, w_out=dw_out, w_up=dw_up, w_down=dw_down, norm1_g=dg1, q_g=d_qg, k_g=d_kg,
                 lg=dlg, ret_g=d_ret_g, gate_up=d_up, gate_b=d_gb, gla_g=d_gla_g, norm2_g=dg2, conv_w=dcw, conv_b=dcb)
    return dx, dmods, grads


def rope_tables(n_lat, n_ctx):
    rows = n_lat // GRID_W
    row = jnp.repeat(jnp.arange(rows, dtype=F32), GRID_W)
    col = jnp.tile(jnp.arange(GRID_W, dtype=F32), rows)
    n_freq = HEAD_DIM // 4
    inv_freq = ROPE_THETA ** (-jnp.arange(n_freq, dtype=F32) / n_freq)
    ang = jnp.concatenate([row[:, None] * inv_freq, col[:, None] * inv_freq], axis=-1)
    cos, sin = jnp.cos(ang), jnp.sin(ang)
    cos = jnp.concatenate([jnp.concatenate([cos, cos], axis=1), jnp.ones((n_ctx, HEAD_DIM), F32)], axis=0)
    sin = jnp.concatenate([jnp.concatenate([-sin, sin], axis=1), jnp.zeros((n_ctx, HEAD_DIM), F32)], axis=0)
    return cos, sin


def local_step(xs, target, mods, weights, fetch, final_g, n_lat, grad_ready):
    t, d = xs.shape
    cos, sin = rope_tables(n_lat, t - n_lat)
    saved = []
    h = xs
    for l, w in enumerate(weights):
        n_out = t if l + 1 < len(weights) else n_lat
        h, s = layer_fwd(l, h, mods[l], w, functools.partial(fetch, l), cos, sin, n_lat, n_out)
        saved.append(s)
    loss, dx, dgf = final_loss(h, target, final_g, n_lat)
    dmods, grads = [None] * len(weights), [None] * len(weights)
    for l in reversed(range(len(weights))):
        dx, dmods[l], grads[l] = layer_bwd(l, dx, saved[l], mods[l], weights[l], cos, sin, n_lat, functools.partial(grad_ready, l))
    return loss, dx, dmods, grads, dgf


WEIGHT_NAMES = ("c_ctx", "ada_w", "ada_b", "norm1_g", "w_in", "q_norm_g", "k_norm_g", "ret_log_decay", "ret_norm_g",
                "gla_gate_up", "gla_gate_b", "gla_norm_g", "w_out", "norm2_g", "w_up", "conv_w", "conv_b", "w_down", "final_norm_g")
PACK_QUANTUM = 8 * LANES


def _pack(arrays):
    flat = jnp.concatenate([a.reshape(-1).astype(F32) for a in arrays])
    n = -(-flat.shape[0] // PACK_QUANTUM) * PACK_QUANTUM
    return jnp.pad(flat, (0, n - flat.shape[0])).reshape(8, n // 8)


def _unpack(flat2d, shapes):
    out, at = [], 0
    for s in shapes:
        size = int(np.prod(s))
        out.append(flat2d[:, at:at + size].reshape((flat2d.shape[0],) + tuple(s)))
        at += size
    return out


def _per_device(gathered):
    return gathered.reshape(N_DEV, -1)


def _from_chips(per_device, axis):
    chips = per_device[0::2]
    moved = jnp.moveaxis(chips, 0, axis)
    shape = moved.shape
    return moved.reshape(shape[:axis] + (shape[axis] * shape[axis + 1],) + shape[axis + 2:])


def kernel(x, c, ctx, c_ctx, ada_w, ada_b, norm1_g, w_in, q_norm_g, k_norm_g, ret_log_decay, ret_norm_g, gla_gate_up, gla_gate_b, gla_norm_g, w_out, norm2_g, w_up, conv_w, conv_b, w_down, final_norm_g, loss_target, m_c_ctx, m_ada_w, m_ada_b, m_norm1_g, m_w_in, m_q_norm_g, m_k_norm_g, m_ret_log_decay, m_ret_norm_g, m_gla_gate_up, m_gla_gate_b, m_gla_norm_g, m_w_out, m_norm2_g, m_w_up, m_conv_w, m_conv_b, m_w_down, m_final_norm_g, v_c_ctx, v_ada_w, v_ada_b, v_norm1_g, v_w_in, v_q_norm_g, v_k_norm_g, v_ret_log_decay, v_ret_norm_g, v_gla_gate_up, v_gla_gate_b, v_gla_norm_g, v_w_out, v_norm2_g, v_w_up, v_conv_w, v_conv_b, v_w_down, v_final_norm_g):
    weights = dict(zip(WEIGHT_NAMES, (c_ctx, ada_w, ada_b, norm1_g, w_in, q_norm_g, k_norm_g, ret_log_decay, ret_norm_g,
                                      gla_gate_up, gla_gate_b, gla_norm_g, w_out, norm2_g, w_up, conv_w, conv_b, w_down, final_norm_g)))
    mom_m = dict(zip(WEIGHT_NAMES, (m_c_ctx, m_ada_w, m_ada_b, m_norm1_g, m_w_in, m_q_norm_g, m_k_norm_g, m_ret_log_decay, m_ret_norm_g,
                                    m_gla_gate_up, m_gla_gate_b, m_gla_norm_g, m_w_out, m_norm2_g, m_w_up, m_conv_w, m_conv_b, m_w_down, m_final_norm_g)))
    mom_v = dict(zip(WEIGHT_NAMES, (v_c_ctx, v_ada_w, v_ada_b, v_norm1_g, v_w_in, v_q_norm_g, v_k_norm_g, v_ret_log_decay, v_ret_norm_g,
                                    v_gla_gate_up, v_gla_gate_b, v_gla_norm_g, v_w_out, v_norm2_g, v_w_up, v_conv_w, v_conv_b, v_w_down, v_final_norm_g)))
    depth, d = norm1_g.shape
    assert d == D_MODEL and x.shape[0] == 1
    n_lat, n_ctx, f = x.shape[1], ctx.shape[1], conv_b.shape[1]
    assert n_lat % ROW_TILE == 0 and n_ctx % ROW_TILE == 0 and f % FFN_COL_TILE == 0 and f % N_CHIPS == 0
    n_in = w_in.shape[2]
    n_ada = ada_w.shape[2]
    xi, yi, ci = lax.axis_index("x"), lax.axis_index("y"), lax.axis_index("c")
    chip = 2 * xi + yi
    dev = 2 * chip + ci

    big = ("w_in", "w_out", "w_up", "w_down")
    order = [(l, name) for l in range(depth) for name in big]
    shards = [weights[name][l].astype(BF16) for l, name in order]
    passing = {}

    def pass_on(k, after):
        tag = "{1}{0}".format(*order[k])
        own, land = split_wait("gather_wait_" + tag, GATHER, in_flight[k], after)
        (moving,), started = split_start("gather_pass_" + tag, PASS_ON, [land])
        passing[k] = (own, moving)
        return started

    def fetch(l, name, after):
        k = order.index((l, name))
        if k == 0:
            pass_on(0, after)
        own, moving = passing.pop(k)
        (land,) = split_wait(f"gather_pass_wait_{name}{l}", PASS_ON, moving, after)
        started = pass_on(k + 1, after) if k + 1 < len(order) else None
        land = lax.dynamic_update_slice_in_dim(land, own[None], chip, axis=0)
        if name != "w_in":
            return land, started
        last = N_MAIN - (N_CHIPS - 1) * n_in
        w_main = jnp.concatenate([land[q] for q in range(N_CHIPS - 1)] + [land[N_CHIPS - 1][:, :last]], axis=1)
        return (w_main, jnp.pad(land[N_CHIPS - 1][:, last:], ((0, 0), (0, LANES - N_GATE)))), started

    small_shapes = [c.shape[1:], conv_w.shape, gla_gate_up.shape, gla_gate_b.shape]
    got = _per_device(all_gather_small("gather_small", _pack([c, conv_w, gla_gate_up, gla_gate_b])))
    c_all, conv_w_sh, gate_up_sh, gate_b_sh = _unpack(got, small_shapes)
    conv_w_full = _from_chips(conv_w_sh, 2)
    gate_up_full = _from_chips(gate_up_sh, 3)
    gate_b_full = _from_chips(gate_b_sh, 2)

    act = jnp.zeros((16, d), F32).at[0:N_DEV].set(jax.nn.silu(c_all)).at[N_DEV].set(jax.nn.silu(c_ctx))
    mod_sh = jnp.stack([matmul(f"ada_fwd{l}", act, ada_w, view=ada_view(l, n_ada, False)) for l in range(depth)])
    got = _per_device(all_gather_small("gather_mods", _pack([mod_sh])))
    (mod_sh_all,) = _unpack(got, [mod_sh.shape])
    mod_full = _from_chips(mod_sh_all, 2) + ada_b[:, None, :]
    mod_mine = lax.dynamic_index_in_dim(mod_full, dev, axis=1, keepdims=False)
    mods = [jnp.stack([mod_mine[l].reshape(N_MOD, d), mod_full[l, N_DEV].reshape(N_MOD, d)]) for l in range(depth)]
    in_flight, token = split_start("gather_start", GATHER, shards, [(N_CHIPS,) + s.shape for s in shards], after=mod_full)

    layer_w = []
    for l in range(depth):
        up = jnp.zeros((2, LANES, GLA_HEADS * GLA_DK), F32)
        up = up.at[0, 0:GLA_RANK].set(gate_up_full[l, 0]).at[1, GLA_RANK:2 * GLA_RANK].set(gate_up_full[l, 1])
        layer_w.append(LayerWeights(
            norm1_g=norm1_g[l].reshape(1, 1, d), q_g=q_norm_g[l].reshape(1, 1, HEAD_DIM), k_g=k_norm_g[l].reshape(1, 1, HEAD_DIM),
            lg=ret_log_decay[l].reshape(2, RET_HEADS, 1, 1), ret_g=ret_norm_g[l].reshape(1, 1, HEAD_DIM),
            gate_up=up.reshape(1, 2 * LANES, -1), gate_b=gate_b_full[l].reshape(1, 2, -1), gla_g=gla_norm_g[l].reshape(1, 1, HEAD_DIM),
            norm2_g=norm2_g[l].reshape(1, 1, d), conv_w=conv_w_full[l], conv_b=conv_b[l].reshape(1, f)))

    def pieces_of(name, g):
        if name == "w_in":
            shards = [g["w_main"][:, q * n_in:(q + 1) * n_in] for q in range(N_CHIPS - 1)]
            tail = jnp.concatenate([g["w_main"][:, (N_CHIPS - 1) * n_in:], g["w_gate"][:, :N_GATE]], axis=1)
            return jnp.stack(shards + [tail])
        if name == "w_up":
            return g["w_up"]
        return g[name].reshape(N_CHIPS, -1, d)

    groups = {"ffn": ("w_up", "w_down"), "w_out": ("w_out",), "w_in": ("w_in",)}
    reducing = {}
    to_sibling = []

    def sibling_arrived(after):
        started = None
        while to_sibling:
            l, group, in_flight_halves = to_sibling.pop(0)
            sums = []
            for name, halves in zip(groups[group], in_flight_halves):
                pieces, from_sibling = split_wait(f"rs_sibling_wait_{name}{l}", SIBLING_HALF, halves, after)
                sums.append(add_sibling_half(f"rs_add_sibling_{name}{l}", pieces, from_sibling, ci))
            in_flight_sums, token = split_start(f"rs_start_{group}{l}", SCATTER, sums, [(3,) + s.shape[1:] for s in sums])
            reducing.update({(l, name): grp for name, grp in zip(groups[group], in_flight_sums)})
            started = token if started is None else started + token
        return started

    def grad_ready(l, group, g):
        pieces = [pieces_of(name, g) for name in groups[group]]
        before = None if (l, group) == (0, "w_in") else sibling_arrived(pieces[0])
        in_flight_halves, started = split_start(f"rs_sibling_{group}{l}", SIBLING_HALF, pieces,
                                                [(N_CHIPS, pc.shape[1] // 2, pc.shape[2]) for pc in pieces])
        to_sibling.append((l, group, in_flight_halves))
        return started if before is None else started + before

    xs = jnp.concatenate([x[0], ctx[0]], axis=0) + token[0, 0]
    loss, dx, dmods, grads, dgf = local_step(xs, loss_target[0], mods, layer_w, fetch, final_norm_g.reshape(1, d), n_lat, grad_ready)

    def gate_up_grad(g):
        return jnp.stack([g[0, 0:GLA_RANK], g[0, LANES + GLA_RANK:LANES + 2 * GLA_RANK]])

    per_layer = [[dmods[l][0], dmods[l][1], grads[l]["norm1_g"], grads[l]["norm2_g"], grads[l]["q_g"], grads[l]["k_g"],
                  grads[l]["ret_g"], grads[l]["gla_g"], grads[l]["lg"], gate_up_grad(grads[l]["gate_up"]), grads[l]["gate_b"],
                  grads[l]["conv_w"], grads[l]["conv_b"]] for l in range(depth)]
    layer_shapes = [(N_MOD * d,), (N_MOD * d,), (d,), (d,), (HEAD_DIM,), (HEAD_DIM,), (HEAD_DIM,), (HEAD_DIM,), (2, RET_HEADS),
                    (2, GLA_RANK, GLA_HEADS * GLA_DK), (2, GLA_HEADS * GLA_DK), (3, f), (f,)]
    packed = _pack([a for lay in per_layer for a in lay] + [dgf, loss[0, 0:1]])
    gathered = all_gather_small("gather_small_grads", packed)
    every = _unpack(_per_device(gathered), layer_shapes * depth + [(d,), (1,)])
    total = _unpack(sum_device_blocks("sum_small_grads", gathered).reshape(1, -1), layer_shapes * depth + [(d,), (1,)])
    nl = len(layer_shapes)

    def tot(l, k):
        return total[l * nl + k][0]

    out = {"norm1_g": jnp.stack([tot(l, 2) for l in range(depth)]), "norm2_g": jnp.stack([tot(l, 3) for l in range(depth)]),
           "q_norm_g": jnp.stack([tot(l, 4) for l in range(depth)]), "k_norm_g": jnp.stack([tot(l, 5) for l in range(depth)]),
           "ret_norm_g": jnp.stack([tot(l, 6) for l in range(depth)]), "gla_norm_g": jnp.stack([tot(l, 7) for l in range(depth)]),
           "ret_log_decay": jnp.stack([tot(l, 8) for l in range(depth)]),
           "gla_gate_up": lax.dynamic_slice_in_dim(jnp.stack([tot(l, 9) for l in range(depth)]), chip * gla_gate_up.shape[3], gla_gate_up.shape[3], axis=3),
           "gla_gate_b": lax.dynamic_slice_in_dim(jnp.stack([tot(l, 10) for l in range(depth)]), chip * gla_gate_b.shape[2], gla_gate_b.shape[2], axis=2),
           "conv_w": lax.dynamic_slice_in_dim(jnp.stack([tot(l, 11) for l in range(depth)]), chip * conv_w.shape[2], conv_w.shape[2], axis=2),
           "conv_b": jnp.stack([tot(l, 12) for l in range(depth)]),
           "final_norm_g": total[depth * nl][0],
           "ada_b": jnp.stack([tot(l, 0) + tot(l, 1) for l in range(depth)])}
    loss_total = total[depth * nl + 1][0, 0]

    dmod_all = jnp.zeros((depth, 16, N_MOD * d), F32)
    for l in range(depth):
        dmod_all = dmod_all.at[l, 0:N_DEV].set(every[l * nl][:, :]).at[l, N_DEV].set(tot(l, 1))
    dmod_cols = lax.dynamic_slice_in_dim(dmod_all, chip * n_ada, n_ada, axis=2)
    for l in range(depth):
        slab = OView((depth, d, n_ada), functools.partial(lambda i, j, kk, l: (l, i, j), l=l), None, out.get("ada_w"))
        out["ada_w"] = matmul(f"ada_dw{l}", act, dmod_cols[l], ta=True, o_view=slab)
    dact = matmul("ada_dx0", dmod_cols[0], ada_w, tb=True, view=ada_view(0, n_ada, True))
    for l in range(1, depth):
        dact = matmul(f"ada_dx{l}", dmod_cols[l], ada_w, tb=True, view=ada_view(l, n_ada, True), add=dact)
    got = _per_device(all_gather_small("gather_dcctx", _pack([dact[N_DEV]])))
    sibling_arrived(got)
    got = got[0::2, :d]
    dsilu = got[0] + got[1] + got[2] + got[3]
    sig = jax.nn.sigmoid(c_ctx)
    out["c_ctx"] = dsilu * (sig + c_ctx * sig * (1.0 - sig))

    deltas, new_m, new_v = {}, {}, {}

    def update(name):
        out[name] = out[name].reshape(weights[name].shape)
        deltas[name], new_m[name], new_v[name] = adamw("adamw_" + name, weights[name], out[name], mom_m[name], mom_v[name])

    for name in WEIGHT_NAMES:
        if name not in big:
            update(name)
    behind = new_v["ada_w"]
    joining = []

    def joined(after):
        name, in_flight_halves = joining.pop()
        per_layer = [split_wait(f"rs_join_wait_{name}{l}", JOIN, grp, after)[0] for l, grp in enumerate(in_flight_halves)]
        (deltas[name], new_m[name], new_v[name]), out[name] = adamw_layers(
            "adamw_" + name, weights[name], per_layer, mom_m[name], mom_v[name])
        return new_v[name]

    for name in ("w_down", "w_up", "w_out", "w_in"):
        halves = []
        for l in range(depth):
            sums, got = split_wait(f"rs_wait_{name}{l}", SCATTER, reducing[(l, name)], behind)
            halves.append(add_chip_sums(f"rs_add_chips_{name}{l}", sums, got, chip, ci))
        in_flight_halves, _ = split_start("rs_join_" + name, JOIN, halves)
        if joining:
            behind = joined(behind)
        joining.append((name, in_flight_halves))
    joined(behind)
    grad_x = dx[:n_lat].reshape(x.shape)
    return (loss_total, grad_x, *[out[n] for n in WEIGHT_NAMES], *[deltas[n] for n in WEIGHT_NAMES],
            *[new_m[n] for n in WEIGHT_NAMES], *[new_v[n] for n in WEIGHT_NAMES])
```

```python
import functools
from typing import NamedTuple

import numpy as np
import jax
import jax.numpy as jnp
from jax import lax
from jax.experimental import pallas as pl
from jax.experimental.pallas import tpu as pltpu

F32 = jnp.float32
BF16 = jnp.bfloat16

D_MODEL = 2048
HEAD_DIM = 128
ATT_Q_HEADS = 8
ATT_KV_HEADS = 2
ATT_GROUP = ATT_Q_HEADS // ATT_KV_HEADS
RET_HEADS = 4
GLA_HEADS = 4
GLA_DK = 64
GLA_DV = 128
GLA_RANK = 16
GLA_TAU = 16.0
RET_CHUNK = 256
GLA_CHUNK = 128
GRID_W = 64
ROPE_THETA = 10000.0
N_MOD = 6
EPS = 1e-6
N_MAIN = 5120
N_GATE = 2 * GLA_RANK
LANES = 128
ROW_TILE = 256
FFN_COL_TILE = 256
VMEM_LIMIT = 56 * 1024 * 1024

ADAM_LR = 0.001
ADAM_B1 = 0.9
ADAM_B2 = 0.999
ADAM_EPS = 1e-08
ADAM_WD = 0.01
ADAM_STEP = 10

Z_AQ, Z_AK, Z_AV = 0, 1024, 1280
Z_RQ, Z_RK, Z_RV, Z_RG = 1536, 2048, 2560, 3072
Z_GQ, Z_GK, Z_GV, Z_GR = 3584, 3840, 4096, 4608
P_AQ, P_AK, P_GQ, P_RQ, P_RK, P_LA = 0, 1024, 1280, 1536, 2048, 2560
P_W = 3072


def _params(sem=None):
    return pltpu.CompilerParams(dimension_semantics=sem, vmem_limit_bytes=VMEM_LIMIT)


def _pick(n, cands):
    for c in cands:
        if n % c == 0:
            return c
    return n


_NN = (((1,), (0,)), ((), ()))
_NT = (((1,), (1,)), ((), ()))
_TN = (((0,), (0,)), ((), ()))


def _dg(a, b, dims):
    return lax.dot_general(a.astype(BF16), b.astype(BF16), dims, preferred_element_type=F32)


@jax.custom_vjp
def bdot(a, b):
    return _dg(a, b, _NN)


def _bdot_fwd(a, b):
    return _dg(a, b, _NN), (a, b)


def _bdot_bwd(res, ct):
    a, b = res
    return _dg(ct, b, _NT), _dg(a, ct, _TN)


bdot.defvjp(_bdot_fwd, _bdot_bwd)


@jax.custom_vjp
def bdot_nt(a, b):
    return _dg(a, b, _NT)


def _bdot_nt_fwd(a, b):
    return _dg(a, b, _NT), (a, b)


def _bdot_nt_bwd(res, ct):
    a, b = res
    return _dg(ct, b, _NN), _dg(ct, a, _TN)


bdot_nt.defvjp(_bdot_nt_fwd, _bdot_nt_bwd)


@jax.custom_vjp
def bdot_tn(a, b):
    return _dg(a, b, _TN)


def _bdot_tn_fwd(a, b):
    return _dg(a, b, _TN), (a, b)


def _bdot_tn_bwd(res, ct):
    a, b = res
    return _dg(b, ct, _NT), _dg(a, ct, _NN)


bdot_tn.defvjp(_bdot_tn_fwd, _bdot_tn_bwd)


def _split3(x):
    x1 = x.astype(BF16)
    r1 = x - x1.astype(F32)
    x2 = r1.astype(BF16)
    x3 = (r1 - x2.astype(F32)).astype(BF16)
    return x1, x2, x3


def _mask_dot(mask_bf16, x, dims):
    x1, x2, x3 = _split3(x)
    f = lambda t: lax.dot_general(mask_bf16, t, dims, preferred_element_type=F32)
    return f(x1) + f(x2) + f(x3)


@jax.custom_vjp
def mask_cumsum(mask, x):
    return _mask_dot(mask.astype(BF16), x, _NN)


def _mask_cumsum_fwd(mask, x):
    return mask_cumsum(mask, x), mask


def _mask_cumsum_bwd(mask, ct):
    return jnp.zeros_like(mask), _mask_dot(mask.astype(BF16), ct, _TN)


mask_cumsum.defvjp(_mask_cumsum_fwd, _mask_cumsum_bwd)


def _roll(x, shift, axis):
    return pltpu.roll(x, shift % x.shape[axis], axis)


@functools.partial(jax.custom_vjp, nondiff_argnums=(1, 2))
def roll(x, shift, axis):
    return _roll(x, shift, axis)


def _roll_fwd(x, shift, axis):
    return _roll(x, shift, axis), None


def _roll_bwd(shift, axis, _, ct):
    return (_roll(ct, -shift, axis),)


roll.defvjp(_roll_fwd, _roll_bwd)


def rms(x):
    return x * lax.rsqrt(jnp.mean(x * x, axis=-1, keepdims=True) + EPS)


def silu(x):
    return x * (0.5 + 0.5 * jnp.tanh(0.5 * x))


def log_sigmoid(x):
    return jnp.minimum(x, 0.0) - jnp.log(1.0 + jnp.exp(-jnp.abs(x)))


def rope(t, cos, sin):
    return t * cos + roll(t, HEAD_DIM // 2, 1) * sin


def _heads(x, n, width=HEAD_DIM):
    return [x[:, h * width:(h + 1) * width] for h in range(n)]


class Row(NamedTuple):
    arr: jax.Array
    width: int
    idx: int = 0
    diff: bool = True


class Par(NamedTuple):
    arr: jax.Array
    grouped: bool = False
    diff: bool = True


def _row_specs(rows, pars, tm, n_lat_tiles):
    def grp(i):
        return jnp.minimum(i // n_lat_tiles, 1)

    specs = [pl.BlockSpec((tm, r.width), functools.partial(lambda i, k: (i, k), k=r.idx)) for r in rows]
    for p in pars:
        blk = (1,) + p.arr.shape[1:]
        if p.grouped:
            specs.append(pl.BlockSpec(blk, lambda i: (grp(i), 0, 0)))
        else:
            specs.append(pl.BlockSpec(blk, lambda i: (0, 0, 0)))
    return specs


def row_map(name, fn, rows, pars, outs, n_rows, n_lat):
    tm = ROW_TILE
    nr, npar = len(rows), len(pars)

    def body(*refs):
        vals = [r[...].astype(F32) for r in refs[:nr]] + [p[0] for p in refs[nr:nr + npar]]
        res = fn(*vals)
        for o, v in zip(refs[nr + npar:], res):
            o[...] = v.astype(o.dtype)

    return pl.pallas_call(
        body, name=name, grid=(n_rows // tm,),
        in_specs=_row_specs(rows, pars, tm, n_lat // tm),
        out_specs=[pl.BlockSpec((tm, w), lambda i: (i, 0)) for w, _ in outs],
        out_shape=[jax.ShapeDtypeStruct((n_rows, w), dt) for w, dt in outs],
        compiler_params=_params(("arbitrary",)),
    )(*[r.arr for r in rows], *[p.arr for p in pars])


def row_vjp(name, fn, rows, pars, cts, n_rows, n_lat, add_to_first=None, row_grad_dtype=F32, after=None):
    tm = ROW_TILE
    nr, npar, nc = len(rows), len(pars), len(cts)
    n_lat_tiles = n_lat // tm
    args = list(rows) + list(pars)
    diff_pos = [k for k, a in enumerate(args) if a.diff]
    n_add = 0 if add_to_first is None else 1
    n_after = 0 if after is None else 1

    def body(*refs):
        i = pl.program_id(0)
        vals = [r[...].astype(F32) for r in refs[:nr]] + [p[0] for p in refs[nr:nr + npar]]
        ct_vals = tuple(c[...] for c in refs[nr + npar:nr + npar + nc])
        out_refs = refs[nr + npar + nc + n_add + n_after:]

        def g(*dv):
            full = list(vals)
            for k, v in zip(diff_pos, dv):
                full[k] = v
            return tuple(fn(*full))

        _, vjp = jax.vjp(g, *[vals[k] for k in diff_pos])
        grads = vjp(ct_vals)
        for n, (k, o, gr) in enumerate(zip(diff_pos, out_refs, grads)):
            if k < nr:
                o[...] = (gr + refs[nr + npar + nc][...] if (n == 0 and n_add) else gr).astype(o.dtype)
            else:
                first = (i == 0) | (i == n_lat_tiles) if args[k].grouped else (i == 0)

                @pl.when(first)
                def _():
                    o[0] = gr

                @pl.when(jnp.logical_not(first))
                def _():
                    o[0] += gr

    def grp(i):
        return jnp.minimum(i // n_lat_tiles, 1)

    out_specs, out_shape = [], []
    for k in diff_pos:
        a = args[k]
        if k < nr:
            out_specs.append(pl.BlockSpec((tm, a.width), lambda i: (i, 0)))
            dtype = row_grad_dtype[len(out_shape)] if isinstance(row_grad_dtype, tuple) else row_grad_dtype
            out_shape.append(jax.ShapeDtypeStruct((n_rows, a.width), dtype))
        else:
            blk = (1,) + a.arr.shape[1:]
            out_specs.append(pl.BlockSpec(blk, (lambda i: (grp(i), 0, 0)) if a.grouped else (lambda i: (0, 0, 0))))
            out_shape.append(jax.ShapeDtypeStruct(a.arr.shape, F32))
    extra = list(cts) + ([add_to_first] if n_add else [])
    ct_specs = [pl.BlockSpec((tm, c.shape[1]), lambda i: (i, 0)) for c in extra]
    if n_after:
        extra.append(after)
        ct_specs.append(pl.BlockSpec(memory_space=pl.ANY))
    return pl.pallas_call(
        body, name=name, grid=(n_rows // tm,),
        in_specs=_row_specs(rows, pars, tm, n_lat_tiles) + ct_specs,
        out_specs=out_specs, out_shape=out_shape,
        compiler_params=_params(("arbitrary",)),
    )(*[r.arr for r in rows], *[p.arr for p in pars], *extra)


class BView(NamedTuple):
    n: int
    k: int
    tn: int
    tk: int
    index_map: object
    lead: int = 1
    part_maps: tuple = ()


MATMUL_VMEM_BUDGET = 40 * 1024 * 1024


def _matmul_tiles(m, n, k, a_bytes, b_bytes, o_bytes):
    tms = [c for c in (1152, 1024, 768, 512, 256, 128) if m % c == 0] or [m]
    tns = [c for c in (2048, 1408, 1280, 1024, 768, 512, 256, 128) if n % c == 0] or [n]
    tks = [k] + [c for c in (2816, 2304, 2048, 1408, 1024, 512, 256, 128) if k % c == 0 and c < k]
    for tk in tks:
        fits = [(tm * tn, tm, tn) for tm in tms for tn in tns
                if 2 * (tm * tk * a_bytes + tk * tn * b_bytes + tm * tn * o_bytes) + 2 * tm * tn * 4 <= MATMUL_VMEM_BUDGET]
        if fits and (max(fits)[0] >= min(512 * 512, tms[0] * tns[0]) or tk == tks[-1]):
            _, tm, tn = max(fits)
            return tm, tn, tk
    raise ValueError(f"no matmul tiling for {(m, n, k)}")


class OView(NamedTuple):
    shape: tuple
    index_map: object
    tn: int = None
    into: object = None


def matmul(name, a, b, *, ta=False, tb=False, add=None, out_dtype=F32, view=None, o_view=None, after=None):
    m = a.shape[1] if ta else a.shape[0]
    o_bytes = jnp.dtype(out_dtype).itemsize * (1 if add is None else 2)
    if view is None:
        k = a.shape[0] if ta else a.shape[1]
        n = b.shape[0] if tb else b.shape[1]
        assert (b.shape[1] if tb else b.shape[0]) == k, (a.shape, b.shape, ta, tb)
        if o_view is not None and o_view.tn is not None:
            tn = o_view.tn
            tm, _, tk = _matmul_tiles(m, tn, k, a.dtype.itemsize, b.dtype.itemsize, o_bytes)
        else:
            tm, tn, tk = _matmul_tiles(m, n, k, a.dtype.itemsize, b.dtype.itemsize, o_bytes)
    else:
        n, k, tn, tk = view.n, view.k, view.tn, view.tk
        b_maps = view.part_maps or (view.index_map,)
        tm, _, whole = _matmul_tiles(m, tn, tk * len(b_maps), a.dtype.itemsize, b.dtype.itemsize, o_bytes)
        assert whole == tk * len(b_maps) and not (ta and len(b_maps) > 1), (name, tm, whole)
    parts = 1 if view is None else len(b_maps)
    k_step = tk * parts
    nk = k // k_step
    dims = (((0 if ta else 1,), (1 if tb else 0,)), ((), ()))

    def body(a_ref, *rest):
        b_refs, rest = rest[:parts], rest[parts:]
        if parts == 1:
            prod = lax.dot_general(a_ref[...].astype(BF16), b_refs[0][...].astype(BF16), dims, preferred_element_type=F32)
        else:
            prod = sum(lax.dot_general(a_ref[:, p * tk:(p + 1) * tk].astype(BF16), b_refs[p][...].astype(BF16), dims,
                                       preferred_element_type=F32) for p in range(parts))
        if nk == 1:
            o_ref = rest[-1]
            o_ref[...] = (prod if add is None else prod + rest[0][...]).astype(o_ref.dtype)
            return
        o_ref, acc = rest[-2:]
        kk = pl.program_id(2)

        @pl.when(kk == 0)
        def _():
            acc[...] = prod

        @pl.when(kk != 0)
        def _():
            acc[...] += prod

        @pl.when(kk == nk - 1)
        def _():
            r = acc[...]
            if add is not None:
                r = r + rest[0][...]
            o_ref[...] = r.astype(o_ref.dtype)

    if ta:
        a_spec = pl.BlockSpec((k_step, tm), lambda i, j, kk: (kk, i))
    else:
        a_spec = pl.BlockSpec((tm, k_step), lambda i, j, kk: (i, kk))
    b_tile = (tn, tk) if tb else (tk, tn)
    if view is not None:
        b_specs = [pl.BlockSpec((None,) * view.lead + b_tile, index_map) for index_map in b_maps]
    elif tb:
        b_specs = [pl.BlockSpec(b_tile, lambda i, j, kk: (j, kk))]
    else:
        b_specs = [pl.BlockSpec(b_tile, lambda i, j, kk: (kk, j))]
    o_spec = pl.BlockSpec((tm, tn), lambda i, j, kk: (i, j))
    ins = [a] + [b] * parts + ([add] if add is not None else [])
    in_specs = [a_spec] + b_specs + ([o_spec] if add is not None else [])
    out_shape, aliases = jax.ShapeDtypeStruct((m, n), out_dtype), {}
    if o_view is not None:
        assert add is None
        o_spec = pl.BlockSpec((None, tm, tn), o_view.index_map)
        out_shape = jax.ShapeDtypeStruct(o_view.shape, out_dtype)
        if o_view.into is not None:
            aliases = {len(ins): 0}
            ins.append(o_view.into)
            in_specs.append(pl.BlockSpec(memory_space=pl.ANY))
    if after is not None:
        ins.append(after)
        in_specs.append(pl.BlockSpec(memory_space=pl.ANY))
    return pl.pallas_call(
        body, name=name, grid=(m // tm, n // tn, nk),
        in_specs=in_specs, out_specs=o_spec, out_shape=out_shape, input_output_aliases=aliases,
        scratch_shapes=[pltpu.VMEM((tm, tn), F32)] if nk > 1 else [],
        compiler_params=_params(("parallel", "parallel", "arbitrary")),
    )(*ins)


def normmod_tile(x, g, shift, scale):
    return (rms(x) * g * (1.0 + scale) + shift,)


def resid_tile(x, y, gate):
    return (x + gate * y,)


def resid_norm_tile(x, y, gate, g, shift, scale):
    x1 = x + gate * y
    return x1, rms(x1) * g * (1.0 + scale) + shift


def gated_tile(y, gate):
    return (gate * y,)


def prep_tile(z_qk, z_rq, z_rk, z_gq, zg, cos, sin, qg, kg, gate_up, gate_b):
    out = []
    for h, t in enumerate(_heads(z_qk, ATT_Q_HEADS + ATT_KV_HEADS)):
        out.append(rope(rms(t) * (qg if h < ATT_Q_HEADS else kg), cos, sin))
    gq = z_gq * (GLA_DK ** -0.5)
    rq = [rope(t, cos, sin) for t in _heads(z_rq, RET_HEADS)]
    rk = [rope(t * (HEAD_DIM ** -0.5), cos, sin) for t in _heads(z_rk, RET_HEADS)]
    la = [log_sigmoid(bdot(zg, gate_up[d * LANES:(d + 1) * LANES]) + gate_b[d:d + 1]) * (1.0 / GLA_TAU) for d in range(2)]
    return (jnp.concatenate(out + [gq] + rq + rk + la, axis=1),)


def post_tile(o_att, o_ret_f, o_ret_b, o_gla_f, o_gla_b, rg, gr, ret_g, gla_g):
    ret = jnp.concatenate([rms(t) * ret_g for t in _heads(o_ret_f + o_ret_b, RET_HEADS)], axis=1) * silu(rg)
    gla = jnp.concatenate([rms(t) * gla_g for t in _heads(o_gla_f + o_gla_b, GLA_HEADS)], axis=1) * silu(gr)
    return (jnp.concatenate([o_att, ret, gla], axis=1),)


def _convglu_tile(n_lat, a, v, cw, cb):
    t = a.shape[0]
    row = lax.broadcasted_iota(jnp.int32, (t, 1), 0)
    has_prev = ((row != 0) & (row != n_lat)).astype(F32)
    has_next = ((row != n_lat - 1) & (row != t - 1)).astype(F32)
    conv = roll(a, 1, 0) * has_prev * cw[0:1] + a * cw[1:2] + roll(a, -1, 0) * has_next * cw[2:3] + cb
    return silu(conv) * v


def convglu(name, u, cw, cb, n_lat):
    t, f2 = u.shape
    f, tc = f2 // 2, FFN_COL_TILE
    nb = f // tc

    def body(a_ref, v_ref, cw_ref, cb_ref, o_ref):
        o_ref[...] = _convglu_tile(n_lat, a_ref[...].astype(F32), v_ref[...].astype(F32), cw_ref[...], cb_ref[...]).astype(o_ref.dtype)

    return pl.pallas_call(
        body, name=name, grid=(nb,),
        in_specs=[pl.BlockSpec((t, tc), lambda j: (0, j)), pl.BlockSpec((t, tc), lambda j: (0, nb + j)),
                  pl.BlockSpec((3, tc), lambda j: (0, j)), pl.BlockSpec((1, tc), lambda j: (0, j))],
        out_specs=pl.BlockSpec((t, tc), lambda j: (0, j)),
        out_shape=jax.ShapeDtypeStruct((t, f), BF16),
        compiler_params=_params(("parallel",)),
    )(u, u, cw, cb)


def convglu_bwd(name, u, cw, cb, dg, n_lat):
    t, f2 = u.shape
    f, tc = f2 // 2, FFN_COL_TILE
    nb = f // tc

    def body(a_ref, v_ref, cw_ref, cb_ref, dg_ref, da_ref, dv_ref, dcw_ref, dcb_ref):
        _, vjp = jax.vjp(functools.partial(_convglu_tile, n_lat), a_ref[...].astype(F32), v_ref[...].astype(F32),
                         cw_ref[...], cb_ref[...])
        da, dv, dcw_ref[...], dcb_ref[...] = vjp(dg_ref[...])
        da_ref[...], dv_ref[...] = da.astype(BF16), dv.astype(BF16)

    col = pl.BlockSpec((t, tc), lambda j: (0, j))
    return pl.pallas_call(
        body, name=name, grid=(nb,),
        in_specs=[col, pl.BlockSpec((t, tc), lambda j: (0, nb + j)), pl.BlockSpec((3, tc), lambda j: (0, j)),
                  pl.BlockSpec((1, tc), lambda j: (0, j)), col],
        out_specs=[col, col, pl.BlockSpec((3, tc), lambda j: (0, j)), pl.BlockSpec((1, tc), lambda j: (0, j))],
        out_shape=[jax.ShapeDtypeStruct((t, f), BF16), jax.ShapeDtypeStruct((t, f), BF16),
                   jax.ShapeDtypeStruct((3, f), F32), jax.ShapeDtypeStruct((1, f), F32)],
        compiler_params=_params(("parallel",)),
    )(u, u, cw, cb, dg)


def final_loss(x, target, g, n_lat):
    tm = ROW_TILE
    d = x.shape[1]

    def body(x_ref, t_ref, g_ref, loss_ref, dx_ref, dg_ref):
        i = pl.program_id(0)
        tgt = t_ref[...]

        def f(xv, gv):
            e = rms(xv) * gv - tgt
            s = jnp.sum(jnp.sum(e * e, axis=1, keepdims=True), axis=0, keepdims=True)
            return s * (0.5 / d)

        val, vjp = jax.vjp(f, x_ref[...], g_ref[...])
        dx, dgv = vjp(jnp.ones((1, 1), F32))
        dx_ref[...] = dx

        @pl.when(i == 0)
        def _():
            dg_ref[...] = dgv
            loss_ref[...] = jnp.broadcast_to(val, loss_ref.shape)

        @pl.when(i != 0)
        def _():
            dg_ref[...] += dgv
            loss_ref[...] += jnp.broadcast_to(val, loss_ref.shape)

    return pl.pallas_call(
        body, name="final_loss", grid=(n_lat // tm,),
        in_specs=[pl.BlockSpec((tm, d), lambda i: (i, 0)), pl.BlockSpec((tm, d), lambda i: (i, 0)),
                  pl.BlockSpec((1, d), lambda i: (0, 0))],
        out_specs=[pl.BlockSpec((1, LANES), lambda i: (0, 0)), pl.BlockSpec((tm, d), lambda i: (i, 0)),
                   pl.BlockSpec((1, d), lambda i: (0, 0))],
        out_shape=[jax.ShapeDtypeStruct((1, LANES), F32), jax.ShapeDtypeStruct((n_lat, d), F32),
                   jax.ShapeDtypeStruct((1, d), F32)],
        compiler_params=_params(("arbitrary",)),
    )(x, target, g)


ATT_SCALE = HEAD_DIM ** -0.5
_AK_BLK = P_AK // HEAD_DIM
_AV_BLK = Z_AV // HEAD_DIM


def _att_specs(t, tq):
    gw = ATT_GROUP * HEAD_DIM
    q_spec = pl.BlockSpec((tq, gw), lambda kv, i: (i, kv))
    k_spec = pl.BlockSpec((t, HEAD_DIM), lambda kv, i: (0, _AK_BLK + kv))
    v_spec = pl.BlockSpec((t, HEAD_DIM), lambda kv, i: (0, _AV_BLK + kv))
    row_spec = pl.BlockSpec((ATT_GROUP, tq, 1), lambda kv, i: (kv, i, 0))
    return q_spec, k_spec, v_spec, row_spec


def _att_mask(i, t, tq, n_lat):
    col = lax.broadcasted_iota(jnp.int32, (1, t), 1)
    return jnp.where((i >= n_lat // tq) & (col < n_lat), -jnp.inf, 0.0).astype(F32)


def attn_fwd(p, z, n_lat):
    t = p.shape[0]
    tq = ROW_TILE

    def body(q_ref, k_ref, v_ref, o_ref, lse_ref):
        mask = _att_mask(pl.program_id(1), t, tq, n_lat)
        k, v = k_ref[...].astype(BF16), v_ref[...].astype(BF16)
        for g in range(ATT_GROUP):
            cols = slice(g * HEAD_DIM, (g + 1) * HEAD_DIM)
            s = _dg(q_ref[:, cols], k, _NT) * ATT_SCALE + mask
            m = jnp.max(s, axis=1, keepdims=True)
            pr = jnp.exp(s - m)
            l = jnp.sum(pr, axis=1, keepdims=True)
            o_ref[:, cols] = _dg(pr, v, _NN) / l
            lse_ref[g] = m + jnp.log(l)

    q_spec, k_spec, v_spec, row_spec = _att_specs(t, tq)
    return pl.pallas_call(
        body, name="attn_fwd", grid=(ATT_KV_HEADS, t // tq),
        in_specs=[q_spec, k_spec, v_spec], out_specs=[q_spec, row_spec],
        out_shape=[jax.ShapeDtypeStruct((t, ATT_Q_HEADS * HEAD_DIM), F32),
                   jax.ShapeDtypeStruct((ATT_Q_HEADS, t, 1), F32)],
        compiler_params=_params(("parallel", "parallel")),
    )(p, p, z)


def attn_bwd(p, z, o, lse, do, n_lat):
    t = p.shape[0]
    tq = ROW_TILE

    def body(q_ref, k_ref, v_ref, o_ref, do_ref, lse_ref, dq_ref, dk_ref, dv_ref):
        i = pl.program_id(1)

        @pl.when(i == 0)
        def _():
            dk_ref[...] = jnp.zeros_like(dk_ref)
            dv_ref[...] = jnp.zeros_like(dv_ref)

        mask = _att_mask(i, t, tq, n_lat)
        k, v = k_ref[...].astype(BF16), v_ref[...].astype(BF16)
        dk, dv = dk_ref[...], dv_ref[...]
        for g in range(ATT_GROUP):
            cols = slice(g * HEAD_DIM, (g + 1) * HEAD_DIM)
            q, do_g = q_ref[:, cols].astype(BF16), do_ref[:, cols]
            pr = jnp.exp(_dg(q, k, _NT) * ATT_SCALE + mask - lse_ref[g])
            delta = jnp.sum(o_ref[:, cols] * do_g, axis=1, keepdims=True)
            ds = pr * (_dg(do_g, v, _NT) - delta) * ATT_SCALE
            dq_ref[:, cols] = _dg(ds, k, _NN)
            dk = dk + _dg(ds, q, _TN)
            dv = dv + _dg(pr, do_g, _TN)
        dk_ref[...], dv_ref[...] = dk, dv

    q_spec, k_spec, v_spec, row_spec = _att_specs(t, tq)
    kv_out = pl.BlockSpec((t, HEAD_DIM), lambda kv, i: (0, kv))
    return pl.pallas_call(
        body, name="attn_bwd", grid=(ATT_KV_HEADS, t // tq),
        in_specs=[q_spec, k_spec, v_spec, q_spec, q_spec, row_spec],
        out_specs=[q_spec, kv_out, kv_out],
        out_shape=[jax.ShapeDtypeStruct((t, ATT_Q_HEADS * HEAD_DIM), F32),
                   jax.ShapeDtypeStruct((t, ATT_KV_HEADS * HEAD_DIM), F32),
                   jax.ShapeDtypeStruct((t, ATT_KV_HEADS * HEAD_DIM), F32)],
        compiler_params=_params(("parallel", "arbitrary")),
    )(p, p, z, o, do, lse)


_RQ_BLK = P_RQ // HEAD_DIM
_RK_BLK = P_RK // HEAD_DIM
_RV_BLK = Z_RV // HEAD_DIM


def _scan_chunk(direction, step, n_chunks, n_lat_chunks):
    return jnp.where(direction == 0, (step + n_lat_chunks) % n_chunks, n_chunks - 1 - step)


def _ret_geometry(direction):
    c = RET_CHUNK
    i = lax.broadcasted_iota(jnp.int32, (c, c), 0)
    j = lax.broadcasted_iota(jnp.int32, (c, c), 1)
    rel = jnp.where(direction == 0, i - j, j - i).astype(F32)
    r = lax.broadcasted_iota(jnp.int32, (c, 1), 0)
    pos = jnp.where(direction == 0, r, c - 1 - r).astype(F32)
    return rel, pos


def ret_chunk(q, k, v, s, lg, rel, pos):
    c = RET_CHUNK
    causal = rel >= 0
    d_in = jnp.where(causal, jnp.exp(lg * jnp.where(causal, rel, 0.0)), 0.0)
    q_dec = jnp.exp(lg * (pos + 1.0))
    k_dec = jnp.exp(lg * (c - 1.0 - pos))
    c_dec = jnp.exp(lg * c)
    att = bdot_nt(q, k) * d_in
    o = bdot(att, v) + bdot(q * q_dec, s)
    s_new = c_dec * s + bdot_tn(k * k_dec, v)
    return o, s_new


def ret_fwd(p, z, lg, n_lat):
    t = p.shape[0]
    c = RET_CHUNK
    nc, nlc = t // c, n_lat // c

    def body(q_ref, k_ref, v_ref, lg_ref, o_ref, ssave_ref, s_s):
        d, n = pl.program_id(0), pl.program_id(1)

        @pl.when(n == 0)
        def _():
            s_s[...] = jnp.zeros_like(s_s)

        rel, pos = _ret_geometry(d)
        for h in range(RET_HEADS):
            cols = slice(h * HEAD_DIM, (h + 1) * HEAD_DIM)
            ssave_ref[0, h, 0] = s_s[h]
            o, s_new = ret_chunk(q_ref[:, cols], k_ref[:, cols], v_ref[:, cols].astype(F32), s_s[h], lg_ref[0, h], rel, pos)
            o_ref[:, cols] = o
            s_s[h] = s_new

    w = RET_HEADS * HEAD_DIM

    def blk(base):
        return pl.BlockSpec((c, w), lambda d, n: (_scan_chunk(d, n, nc, nlc), base // RET_HEADS))

    return pl.pallas_call(
        body, name="ret_fwd", grid=(2, nc),
        in_specs=[blk(_RQ_BLK), blk(_RK_BLK), blk(_RV_BLK), pl.BlockSpec((1, RET_HEADS, 1, 1), lambda d, n: (d, 0, 0, 0))],
        out_specs=[pl.BlockSpec((c, w), lambda d, n: (_scan_chunk(d, n, nc, nlc), d)),
                   pl.BlockSpec((1, RET_HEADS, 1, HEAD_DIM, HEAD_DIM), lambda d, n: (d, 0, n, 0, 0))],
        out_shape=[jax.ShapeDtypeStruct((t, 2 * w), F32),
                   jax.ShapeDtypeStruct((2, RET_HEADS, nc, HEAD_DIM, HEAD_DIM), F32)],
        scratch_shapes=[pltpu.VMEM((RET_HEADS, HEAD_DIM, HEAD_DIM), F32)],
        compiler_params=_params(("parallel", "arbitrary")),
    )(p, p, z, lg)


def ret_bwd(p, z, lg, states, do, n_lat):
    t = p.shape[0]
    c = RET_CHUNK
    nc, nlc = t // c, n_lat // c

    def body(q_ref, k_ref, v_ref, lg_ref, s_ref, do_ref, dq_ref, dk_ref, dv_ref, dlg_ref, ds_s):
        d, n = pl.program_id(0), pl.program_id(1)

        @pl.when(n == 0)
        def _():
            ds_s[...] = jnp.zeros_like(ds_s)
            dlg_ref[...] = jnp.zeros_like(dlg_ref)

        rel, pos = _ret_geometry(d)
        f = functools.partial(ret_chunk, rel=rel, pos=pos)
        for h in range(RET_HEADS):
            cols = slice(h * HEAD_DIM, (h + 1) * HEAD_DIM)
            _, vjp = jax.vjp(f, q_ref[:, cols], k_ref[:, cols], v_ref[:, cols].astype(F32), s_ref[0, h, 0], lg_ref[0, h])
            dq, dk, dv, ds, dlg = vjp((do_ref[:, cols], ds_s[h]))
            dq_ref[:, cols], dk_ref[:, cols], dv_ref[:, cols] = dq, dk, dv
            ds_s[h] = ds
            dlg_ref[0, h] += dlg

    def chunk_of(d, n):
        return _scan_chunk(d, nc - 1 - n, nc, nlc)

    w = RET_HEADS * HEAD_DIM

    def blk(base):
        return pl.BlockSpec((c, w), lambda d, n: (chunk_of(d, n), base // RET_HEADS))

    out_blk = pl.BlockSpec((c, w), lambda d, n: (chunk_of(d, n), d))
    lg_blk = pl.BlockSpec((1, RET_HEADS, 1, 1), lambda d, n: (d, 0, 0, 0))
    grad_shape = jax.ShapeDtypeStruct((t, 2 * w), F32)
    return pl.pallas_call(
        body, name="ret_bwd", grid=(2, nc),
        in_specs=[blk(_RQ_BLK), blk(_RK_BLK), blk(_RV_BLK), lg_blk,
                  pl.BlockSpec((1, RET_HEADS, 1, HEAD_DIM, HEAD_DIM), lambda d, n: (d, 0, nc - 1 - n, 0, 0)),
                  pl.BlockSpec((c, w), lambda d, n: (chunk_of(d, n), 0))],
        out_specs=[out_blk, out_blk, out_blk, lg_blk],
        out_shape=[grad_shape, grad_shape, grad_shape, jax.ShapeDtypeStruct((2, RET_HEADS, 1, 1), F32)],
        scratch_shapes=[pltpu.VMEM((RET_HEADS, HEAD_DIM, HEAD_DIM), F32)],
        compiler_params=_params(("parallel", "arbitrary")),
    )(p, p, z, lg, states, do)


_GQ_BLK = P_GQ // (GLA_HEADS * GLA_DK)
_GK_BLK = Z_GK // (GLA_HEADS * GLA_DK)
_GV_BLK = Z_GV // (GLA_HEADS * GLA_DV)
_LA_BLK = P_LA // (GLA_HEADS * GLA_DK)


def _gla_mask(direction):
    c = GLA_CHUNK
    i = lax.broadcasted_iota(jnp.int32, (c, c), 0)
    j = lax.broadcasted_iota(jnp.int32, (c, c), 1)
    return (jnp.where(direction == 0, i - j, j - i) >= 0).astype(F32)


def gla_chunk(q, k, v, la, st, mask):
    b = mask_cumsum(mask, la)
    btot = jnp.sum(la, axis=0, keepdims=True)
    half = 0.5 * btot
    qt, kt = q * jnp.exp(b - half), k * jnp.exp(half - b)
    qs, ke = q * jnp.exp(b), k * jnp.exp(btot - b)
    outs, upd = [], []
    for h in range(GLA_HEADS):
        ks = slice(h * GLA_DK, (h + 1) * GLA_DK)
        vh = v[:, h * GLA_DV:(h + 1) * GLA_DV]
        att = bdot_nt(qt[:, ks], kt[:, ks]) * mask
        outs.append(bdot(att, vh) + bdot_nt(qs[:, ks], st[:, ks]))
        upd.append(bdot_tn(vh, ke[:, ks]))
    st_new = st * jnp.exp(btot) + jnp.concatenate(upd, axis=1)
    return jnp.concatenate(outs, axis=1), st_new


def gla_fwd(p, z, n_lat):
    t = p.shape[0]
    c = GLA_CHUNK
    nc, nlc = t // c, n_lat // c
    kw, vw = GLA_HEADS * GLA_DK, GLA_HEADS * GLA_DV

    def body(q_ref, k_ref, v_ref, la_ref, o_ref, ssave_ref, s_s):
        d, n = pl.program_id(0), pl.program_id(1)

        @pl.when(n == 0)
        def _():
            s_s[...] = jnp.zeros_like(s_s)

        ssave_ref[0, 0] = s_s[...]
        o, s_new = gla_chunk(q_ref[...], k_ref[...].astype(F32), v_ref[...].astype(F32), la_ref[...], s_s[...], _gla_mask(d))
        o_ref[...] = o
        s_s[...] = s_new

    def chunk_of(d, n):
        return _scan_chunk(d, n, nc, nlc)

    return pl.pallas_call(
        body, name="gla_fwd", grid=(2, nc),
        in_specs=[pl.BlockSpec((c, kw), lambda d, n: (chunk_of(d, n), _GQ_BLK)),
                  pl.BlockSpec((c, kw), lambda d, n: (chunk_of(d, n), _GK_BLK)),
                  pl.BlockSpec((c, vw), lambda d, n: (chunk_of(d, n), _GV_BLK)),
                  pl.BlockSpec((c, kw), lambda d, n: (chunk_of(d, n), _LA_BLK + d))],
        out_specs=[pl.BlockSpec((c, vw), lambda d, n: (chunk_of(d, n), d)),
                   pl.BlockSpec((1, 1, GLA_DV, kw), lambda d, n: (d, n, 0, 0))],
        out_shape=[jax.ShapeDtypeStruct((t, 2 * vw), F32), jax.ShapeDtypeStruct((2, nc, GLA_DV, kw), F32)],
        scratch_shapes=[pltpu.VMEM((GLA_DV, kw), F32)],
        compiler_params=_params(("parallel", "arbitrary")),
    )(p, z, z, p)


def gla_bwd(p, z, states, do, n_lat):
    t = p.shape[0]
    c = GLA_CHUNK
    nc, nlc = t // c, n_lat // c
    kw, vw = GLA_HEADS * GLA_DK, GLA_HEADS * GLA_DV

    def body(q_ref, k_ref, v_ref, la_ref, s_ref, do_ref, dq_ref, dk_ref, dv_ref, dla_ref, ds_s):
        d, n = pl.program_id(0), pl.program_id(1)

        @pl.when(n == 0)
        def _():
            ds_s[...] = jnp.zeros_like(ds_s)

        f = functools.partial(gla_chunk, mask=_gla_mask(d))
        _, vjp = jax.vjp(f, q_ref[...], k_ref[...].astype(F32), v_ref[...].astype(F32), la_ref[...], s_ref[0, 0])
        dq_ref[...], dk_ref[...], dv_ref[...], dla_ref[...], ds_s[...] = vjp((do_ref[...], ds_s[...]))

    def chunk_of(d, n):
        return _scan_chunk(d, nc - 1 - n, nc, nlc)

    k_out = pl.BlockSpec((c, kw), lambda d, n: (chunk_of(d, n), d))
    return pl.pallas_call(
        body, name="gla_bwd", grid=(2, nc),
        in_specs=[pl.BlockSpec((c, kw), lambda d, n: (chunk_of(d, n), _GQ_BLK)),
                  pl.BlockSpec((c, kw), lambda d, n: (chunk_of(d, n), _GK_BLK)),
                  pl.BlockSpec((c, vw), lambda d, n: (chunk_of(d, n), _GV_BLK)),
                  pl.BlockSpec((c, kw), lambda d, n: (chunk_of(d, n), _LA_BLK + d)),
                  pl.BlockSpec((1, 1, GLA_DV, kw), lambda d, n: (d, nc - 1 - n, 0, 0)),
                  pl.BlockSpec((c, vw), lambda d, n: (chunk_of(d, n), 0))],
        out_specs=[k_out, k_out, pl.BlockSpec((c, vw), lambda d, n: (chunk_of(d, n), d)), k_out],
        out_shape=[jax.ShapeDtypeStruct((t, 2 * kw), F32), jax.ShapeDtypeStruct((t, 2 * kw), F32),
                   jax.ShapeDtypeStruct((t, 2 * vw), F32), jax.ShapeDtypeStruct((t, 2 * kw), F32)],
        scratch_shapes=[pltpu.VMEM((GLA_DV, kw), F32)],
        compiler_params=_params(("parallel", "arbitrary")),
    )(p, z, z, p, states, do)


def _adam_tile(w, g, m, v):
    m = ADAM_B1 * m + (1.0 - ADAM_B1) * g
    v = ADAM_B2 * v + (1.0 - ADAM_B2) * (g * g)
    m_hat = m / (1.0 - ADAM_B1 ** ADAM_STEP)
    v_hat = v / (1.0 - ADAM_B2 ** ADAM_STEP)
    delta = -ADAM_LR * (m_hat / (jnp.sqrt(v_hat) + ADAM_EPS) + ADAM_WD * w)
    return delta, m, v


def adamw(name, w, g, m, v):
    shape = w.shape
    cols = shape[-1] if w.ndim > 1 and shape[-1] >= LANES else int(np.prod(shape))
    rows = int(np.prod(shape)) // cols
    tr = rows
    for cand in (512, 256, 128, 64, 32, 16, 8):
        if rows % cand == 0 and cand * cols * 4 <= (1 << 21):
            tr = cand
            break
    flat = [a.reshape(rows, cols) for a in (w, g, m, v)]

    def body(w_ref, g_ref, m_ref, v_ref, d_ref, mo_ref, vo_ref):
        d_ref[...], mo_ref[...], vo_ref[...] = _adam_tile(w_ref[...], g_ref[...], m_ref[...], v_ref[...])

    spec = pl.BlockSpec((tr, cols), lambda i: (i, 0))
    outs = pl.pallas_call(
        body, name=name, grid=(rows // tr,),
        in_specs=[spec] * 4, out_specs=[spec] * 3,
        out_shape=[jax.ShapeDtypeStruct((rows, cols), F32)] * 3,
        compiler_params=_params(("parallel",)),
    )(*flat)
    return tuple(o.reshape(shape) for o in outs)


def adamw_layers(name, w, grads, m, v):
    depth, rows, cols = w.shape
    tr = _rows_tile(rows, cols)
    nb = rows // tr

    def body(w_ref, m_ref, v_ref, *rest):
        g_refs, (g_ref, d_ref, mo_ref, vo_ref) = rest[:depth], rest[depth:]
        l = pl.program_id(0)
        for k in range(depth):
            @pl.when(l == k)
            def _():
                g = g_refs[k][...]
                g_ref[...] = g
                d_ref[...], mo_ref[...], vo_ref[...] = _adam_tile(w_ref[...], g, m_ref[...], v_ref[...])

    def layer_grad(k):
        return pl.BlockSpec((tr, cols), lambda l, i: (jnp.where(l < k, 0, jnp.where(l == k, i, nb - 1)), 0))

    spec = pl.BlockSpec((tr, cols), lambda l, i: (l * nb + i, 0))
    flat = [a.reshape(depth * rows, cols) for a in (w, m, v)]
    g_all, delta, new_m, new_v = pl.pallas_call(
        body, name=name, grid=(depth, nb),
        in_specs=[spec] * 3 + [layer_grad(k) for k in range(depth)], out_specs=[spec] * 4,
        out_shape=[jax.ShapeDtypeStruct((depth * rows, cols), F32)] * 4,
        compiler_params=_params(("arbitrary", "arbitrary")),
    )(*flat, *grads)
    return tuple(a.reshape(w.shape) for a in (delta, new_m, new_v)), g_all.reshape(w.shape)


MESH = pl.DeviceIdType.MESH
_HBM = pl.BlockSpec(memory_space=pltpu.HBM)
N_CHIPS = 4
N_DEV = 8


def _place():
    x, y, c = lax.axis_index("x"), lax.axis_index("y"), lax.axis_index("c")
    chips = [(1 - x, y), (x, 1 - y), (1 - x, 1 - y)]
    return x, y, c, chips


def _remote(src, dst, send_sem, recv_sem, to):
    return pltpu.make_async_remote_copy(src_ref=src, dst_ref=dst, send_sem=send_sem, recv_sem=recv_sem,
                                        device_id=to, device_id_type=MESH)


def all_gather_small(name, v):
    m_per, n = v.shape

    def body(x_ref, out_ref, send_sems, recv_sems, local_sem):
        x, y, c, chips = _place()
        me, sibling = (x, y, c), (x, y, 1 - c)

        def rows(px, py, pc):
            return out_ref.at[pl.ds((4 * px + 2 * py + pc) * m_per, m_per), :]

        def copy(k, block, to, src=None):
            return _remote(rows(*block) if src is None else src, rows(*block), send_sems.at[k], recv_sems.at[k], to)

        mine = pltpu.make_async_copy(x_ref, rows(*me), local_sem)
        mine.start()
        first = [copy(0, me, sibling, src=x_ref)]
        first += [copy(1 + j, me, (*chip, c), src=x_ref) for j, chip in enumerate(chips)]
        for cp in first:
            cp.start()
        passed = [copy(4 + j, (*chip, c), sibling) for j, chip in enumerate(chips)]
        for j, chip in enumerate(chips):
            copy(1 + j, (*chip, c), me).wait_recv()
            passed[j].start()
        copy(0, sibling, me).wait_recv()
        for j, chip in enumerate(chips):
            copy(4 + j, (*chip, 1 - c), me).wait_recv()
        for cp in first + passed:
            cp.wait_send()
        mine.wait()

    return pl.pallas_call(
        body, name=name,
        out_shape=jax.ShapeDtypeStruct((N_DEV * m_per, n), v.dtype),
        in_specs=[pl.BlockSpec(memory_space=pltpu.VMEM)],
        out_specs=pl.BlockSpec(memory_space=pltpu.VMEM),
        scratch_shapes=[pltpu.SemaphoreType.DMA((7,)), pltpu.SemaphoreType.DMA((7,)), pltpu.SemaphoreType.DMA],
        compiler_params=pltpu.CompilerParams(vmem_limit_bytes=VMEM_LIMIT),
    )(v)


_SEM = pl.BlockSpec(memory_space=pltpu.SEMAPHORE)
_SPLIT_COPY = pltpu.CompilerParams(has_side_effects=pltpu.SideEffectType.DATAFLOW_SIDE_EFFECTING)


class CopyPlan(NamedTuple):
    copies: object
    n: int
    in_place: bool = False


def _gather_copies(x_ref, land_ref, x, y, c, chips):
    half = x_ref.shape[0] // 2
    rows = pl.ds(c * half, half)
    return [(x_ref.at[rows, :], land_ref.at[2 * x + y, rows, :], (*chip, c), land_ref.at[2 * chip[0] + chip[1], rows, :])
            for chip in chips]


def _pass_copies(land_ref, _, x, y, c, chips):
    half = land_ref.shape[1] // 2
    mine, other = pl.ds(c * half, half), pl.ds((1 - c) * half, half)
    return [(land_ref.at[2 * chip[0] + chip[1], mine, :], land_ref.at[2 * chip[0] + chip[1], mine, :], (x, y, 1 - c),
             land_ref.at[2 * chip[0] + chip[1], other, :]) for chip in chips]


def _sibling_half_copies(p_ref, land_ref, x, y, c, chips):
    half = p_ref.shape[1] // 2
    return [(p_ref.at[:, pl.ds((1 - c) * half, half), :], land_ref, (x, y, 1 - c), land_ref)]


def _scatter_copies(s_ref, land_ref, x, y, c, chips):
    return [(s_ref.at[2 * chip[0] + chip[1]], land_ref.at[j], (*chip, c), land_ref.at[j]) for j, chip in enumerate(chips)]


def _join_copies(buf_ref, _, x, y, c, chips):
    half = buf_ref.shape[0] // 2
    mine = buf_ref.at[pl.ds(c * half, half), :]
    return [(mine, mine, (x, y, 1 - c), buf_ref.at[pl.ds((1 - c) * half, half), :])]


GATHER = CopyPlan(_gather_copies, 3)
PASS_ON = CopyPlan(_pass_copies, 3, in_place=True)
SIBLING_HALF = CopyPlan(_sibling_half_copies, 1)
SCATTER = CopyPlan(_scatter_copies, 3)
JOIN = CopyPlan(_join_copies, 1, in_place=True)


def split_start(name, plan, srcs, land_shapes=None, after=None):
    nt = len(srcs)
    arrays = [pltpu.with_memory_space_constraint(s, pltpu.HBM) for s in srcs]
    if not plan.in_place:
        arrays += [pltpu.with_memory_space_constraint(lax.empty(shape, s.dtype), pltpu.HBM) for shape, s in zip(land_shapes, srcs)]
    na = len(arrays)
    behind = [] if after is None else [after]
    n_in = na + len(behind)

    def body(*refs):
        x_refs = refs[:nt]
        land_refs = x_refs if plan.in_place else refs[nt:na]
        send, recv = refs[n_in:n_in + nt], refs[n_in + nt:n_in + 2 * nt]
        x, y, c, chips = _place()
        for t in range(nt):
            for j, (src, dst, to, _) in enumerate(plan.copies(x_refs[t], land_refs[t], x, y, c, chips)):
                _remote(src, dst, send[t].at[j], recv[t].at[j], to).start()
        refs[-1][...] = jnp.zeros_like(refs[-1])

    outs = pl.pallas_call(
        body, name=name,
        out_shape=tuple([pltpu.SemaphoreType.DMA((plan.n,))] * (2 * nt) + [pltpu.HBM(a.shape, a.dtype) for a in arrays]
                        + [jax.ShapeDtypeStruct((8, LANES), F32)]),
        in_specs=[_HBM] * na + [pl.BlockSpec(memory_space=pl.ANY)] * len(behind),
        out_specs=tuple([_SEM] * (2 * nt) + [_HBM] * na + [pl.BlockSpec(memory_space=pltpu.VMEM)]),
        input_output_aliases={i: 2 * nt + i for i in range(na)},
        compiler_params=_SPLIT_COPY,
    )(*arrays, *behind)
    groups = [(outs[t], outs[nt + t]) + tuple(outs[2 * nt + t + k * nt] for k in range(na // nt)) for t in range(nt)]
    return groups, outs[-1]


def split_wait(name, plan, group, after):
    send, recv, *arrays = group
    na = len(arrays)

    def body(*refs):
        x_ref, land_ref = refs[0], refs[na - 1]
        send_sem, recv_sem = refs[na], refs[na + 1]
        x, y, c, chips = _place()
        for j, (s, _, to, arrival) in enumerate(plan.copies(x_ref, land_ref, x, y, c, chips)):
            cp = _remote(s, arrival, send_sem.at[j], recv_sem.at[j], to)
            cp.wait_send()
            cp.wait_recv()

    return pl.pallas_call(
        body, name=name,
        out_shape=tuple(pltpu.HBM(a.shape, a.dtype) for a in arrays),
        in_specs=tuple([_HBM] * na + [_SEM, _SEM, pl.BlockSpec(memory_space=pl.ANY)]), out_specs=tuple([_HBM] * na),
        input_output_aliases={i: i for i in range(na)}, compiler_params=_SPLIT_COPY,
    )(*arrays, send, recv, after)


def _rows_tile(rows, cols):
    for cand in (512, 256, 128, 64, 32, 16):
        if rows % cand == 0 and cand * cols * 4 <= (1 << 21):
            return cand
    return rows


def add_sibling_half(name, pieces, from_sibling, core):
    n, h, cols = from_sibling.shape
    tr = _rows_tile(h, cols // 4)
    nb = h // tr

    def body(c_ref, a_ref, b_ref, o_ref):
        o_ref[...] = (a_ref[...].astype(F32) + b_ref[...].astype(F32)).astype(o_ref.dtype)

    blk = pl.BlockSpec((1, tr, cols), lambda q, i, c_ref: (q, i, 0))
    return pl.pallas_call(
        body, name=name,
        grid_spec=pltpu.PrefetchScalarGridSpec(
            num_scalar_prefetch=1, grid=(n, nb),
            in_specs=[pl.BlockSpec((1, tr, cols), lambda q, i, c_ref: (q, c_ref[0] * nb + i, 0)), blk], out_specs=blk),
        out_shape=jax.ShapeDtypeStruct((n, h, cols), BF16),
        compiler_params=_params(("parallel", "parallel")),
    )(core.reshape(1).astype(jnp.int32), pieces, from_sibling)


def add_chip_sums(name, chip_sums, from_chips, chip, core):
    _, h, cols = chip_sums.shape
    tr = _rows_tile(h, cols // 2)
    nb = h // tr

    def body(s_ref, own_ref, r0_ref, r1_ref, r2_ref, o_ref):
        acc = own_ref[0].astype(F32) + r0_ref[0].astype(F32)
        o_ref[...] = acc + r1_ref[0].astype(F32) + r2_ref[0].astype(F32)

    def got(j):
        return pl.BlockSpec((1, tr, cols), lambda i, s_ref: (j, i, 0))

    return pl.pallas_call(
        body, name=name,
        grid_spec=pltpu.PrefetchScalarGridSpec(
            num_scalar_prefetch=1, grid=(nb,),
            in_specs=[pl.BlockSpec((1, tr, cols), lambda i, s_ref: (s_ref[0], i, 0)), got(0), got(1), got(2)],
            out_specs=pl.BlockSpec((tr, cols), lambda i, s_ref: (s_ref[1] * nb + i, 0))),
        out_shape=jax.ShapeDtypeStruct((2 * h, cols), F32),
        compiler_params=_params(("parallel",)),
    )(jnp.stack([chip, core]).astype(jnp.int32), chip_sums, from_chips, from_chips, from_chips)


def sum_device_blocks(name, g):
    n = g.shape[1]

    def body(g_ref, o_ref):
        acc = g_ref[0:8, :]
        for d in range(1, N_DEV):
            acc = acc + g_ref[8 * d:8 * (d + 1), :]
        o_ref[...] = acc

    return pl.pallas_call(body, name=name, out_shape=jax.ShapeDtypeStruct((8, n), F32),
                          compiler_params=pltpu.CompilerParams(vmem_limit_bytes=VMEM_LIMIT))(g)


class LayerWeights(NamedTuple):
    norm1_g: jax.Array
    q_g: jax.Array
    k_g: jax.Array
    lg: jax.Array
    ret_g: jax.Array
    gate_up: jax.Array
    gate_b: jax.Array
    gla_g: jax.Array
    norm2_g: jax.Array
    conv_w: jax.Array
    conv_b: jax.Array


def _mod(mods, k):
    return mods[:, k:k + 1, :]


def out_view(l, tb):
    rows = D_MODEL // N_CHIPS
    if tb:
        return BView(n=D_MODEL, k=D_MODEL, tn=rows, tk=D_MODEL, index_map=lambda i, j, kk: (j, l, kk))
    chips = tuple(functools.partial(lambda i, j, kk, q: (q, l, j), q=q) for q in range(N_CHIPS))
    return BView(n=D_MODEL, k=D_MODEL, tn=1024, tk=rows, index_map=None, part_maps=chips)


def down_view(l, f, tb):
    rows = f // N_CHIPS
    if tb:
        return BView(n=f, k=D_MODEL, tn=rows, tk=D_MODEL, index_map=lambda i, j, kk: (j, l, kk))
    chips = tuple(functools.partial(lambda i, j, kk, q: (q, l, j), q=q) for q in range(N_CHIPS))
    return BView(n=D_MODEL, k=f, tn=512, tk=rows, index_map=None, part_maps=chips)


def up_view(l, f, part=None):
    cols = 2 * f // N_CHIPS
    tc = _pick(cols, (1408, 1024, 512, 256))
    nbc = cols // tc
    if part is None:
        return BView(n=2 * f, k=D_MODEL, tn=tc, tk=D_MODEL, index_map=lambda i, j, kk: (j // nbc, l, j % nbc))
    nnb = D_MODEL // 512
    tiles = tuple(functools.partial(lambda i, j, kk, p: (2 * part + p // nbc, l * nnb + j, p % nbc), p=p) for p in range(2 * nbc))
    return BView(n=D_MODEL, k=f, tn=512, tk=tc, index_map=None, part_maps=tiles)


def up_grad_view(f, part, into):
    cols = f // 2
    tn = _pick(cols, (1408, 1024, 512, 256))
    nbc = cols // tn
    return OView((N_CHIPS, D_MODEL, cols), lambda i, j, kk: (2 * part + j // nbc, i, j % nbc), tn, into)


def ada_view(l, n_ada, tb):
    if tb:
        return BView(n=D_MODEL, k=n_ada, tn=1024, tk=n_ada, index_map=lambda i, j, kk: (l, j, 0))
    return BView(n=n_ada, k=D_MODEL, tn=1024, tk=D_MODEL, index_map=lambda i, j, kk: (l, 0, j))


def _prep_args(z, zg, cos, sin, w):
    rows = [Row(z, Z_AV, 0), Row(z, 512, Z_RQ // 512), Row(z, 512, Z_RK // 512), Row(z, 256, Z_GQ // 256),
            Row(zg, LANES, 0), Row(cos, HEAD_DIM, 0, False), Row(sin, HEAD_DIM, 0, False)]
    return rows, [Par(w.q_g), Par(w.k_g), Par(w.gate_up), Par(w.gate_b)]


def _post_args(o_att, o_ret, o_gla, z, w):
    rows = [Row(o_att, 1024), Row(o_ret, 512, 0), Row(o_ret, 512, 1, False), Row(o_gla, 512, 0), Row(o_gla, 512, 1, False),
            Row(z, 512, Z_RG // 512), Row(z, 512, Z_GR // 512)]
    return rows, [Par(w.ret_g), Par(w.gla_g)]


def layer_fwd(l, xs, mods, w, fetch, cos, sin, n_lat, n_out):
    t, d = xs.shape
    tag = f"l{l}_"
    nm1 = [Par(w.norm1_g), Par(_mod(mods, 0), True), Par(_mod(mods, 1), True)]
    (h,) = row_map(tag + "norm1", normmod_tile, [Row(xs, d)], nm1, [(d, BF16)], t, n_lat)
    (w_main, w_gate), started = fetch("w_in", h)
    z = matmul(tag + "in_proj", h, w_main, after=started, out_dtype=BF16)
    zg = matmul(tag + "gate_proj", h, w_gate)
    rows, pars = _prep_args(z, zg, cos, sin, w)
    (p,) = row_map(tag + "prep", prep_tile, rows, pars, [(P_W, F32)], t, n_lat)
    o_att, lse = attn_fwd(p, z, n_lat)
    o_ret, s_ret = ret_fwd(p, z, w.lg, n_lat)
    o_gla, s_gla = gla_fwd(p, z, n_lat)
    rows, pars = _post_args(o_att, o_ret, o_gla, z, w)
    (m,) = row_map(tag + "post", post_tile, rows, pars, [(d, BF16)], t, n_lat)
    m = m[:n_out]
    g_out, started = fetch("w_out", m)
    y = matmul(tag + "out_proj", m, g_out, view=out_view(0, False), after=started)
    rn = [Par(_mod(mods, 2), True), Par(w.norm2_g), Par(_mod(mods, 3), True), Par(_mod(mods, 4), True)]
    x1, h2 = row_map(tag + "resid1_norm2", resid_norm_tile, [Row(xs, d), Row(y, d)], rn, [(d, F32), (d, BF16)], n_out, n_lat)
    f = w.conv_b.shape[1]
    g_up, started = fetch("w_up", h2)
    u = matmul(tag + "up_proj", h2, g_up, view=up_view(0, f), after=started, out_dtype=BF16)
    g = convglu(tag + "convglu", u, w.conv_w, w.conv_b, n_lat)
    g_down, started = fetch("w_down", g)
    yd = matmul(tag + "down_proj", g, g_down, view=down_view(0, f, False), after=started)
    (x2,) = row_map(tag + "resid2", resid_tile, [Row(x1, d), Row(yd, d)], [Par(_mod(mods, 5), True)], [(d, F32)], n_out, n_lat)
    saved = dict(xs=xs, h=h, z=z, zg=zg, p=p, o_att=o_att, lse=lse, o_ret=o_ret, s_ret=s_ret, o_gla=o_gla, s_gla=s_gla,
                 m=m, y=y, x1=x1, h2=h2, u=u, g=g, yd=yd, w_main=w_main, w_gate=w_gate, g_out=g_out, g_up=g_up, g_down=g_down)
    return x2, saved


def _sum_dirs(a):
    w = a.shape[1] // 2
    return a[:, :w] + a[:, w:]


def layer_bwd(l, dx2, s, mods, w, cos, sin, n_lat, grad_ready):
    (t, d), n_out = s["xs"].shape, dx2.shape[0]
    tag = f"l{l}_b_"

    def all_rows(a):
        return a if n_out == t else jnp.pad(a, ((0, t - n_out), (0, 0)))

    dyd, dgate5 = row_vjp(tag + "resid2", gated_tile, [Row(s["yd"], d)], [Par(_mod(mods, 5), True)], [dx2], n_out, n_lat,
                          row_grad_dtype=BF16)
    f = w.conv_b.shape[1]
    dg = matmul(tag + "down_dx", dyd, s["g_down"], tb=True, view=down_view(0, f, True))
    dw_down = matmul(tag + "down_dw", s["g"], dyd, ta=True, out_dtype=BF16)
    da, dv, dcw, dcb = convglu_bwd(tag + "convglu", s["u"], w.conv_w, w.conv_b, dg, n_lat)
    dh2 = matmul(tag + "up_dx_gate", da, s["g_up"], tb=True, view=up_view(0, f, 0))
    dh2 = matmul(tag + "up_dx_value", dv, s["g_up"], tb=True, view=up_view(0, f, 1), add=dh2)
    dw_up = matmul(tag + "up_dw_gate", s["h2"], da, ta=True, out_dtype=BF16, o_view=up_grad_view(f, 0, None))
    dw_up = matmul(tag + "up_dw_value", s["h2"], dv, ta=True, out_dtype=BF16, o_view=up_grad_view(f, 1, dw_up))
    started = grad_ready("ffn", dict(w_up=dw_up, w_down=dw_down))
    rn = [Par(_mod(mods, 2), True), Par(w.norm2_g), Par(_mod(mods, 3), True), Par(_mod(mods, 4), True)]
    dx1, dy, dgate2, dg2, dshift3, dscale4 = row_vjp(
        tag + "resid1_norm2", resid_norm_tile, [Row(s["xs"], d), Row(s["y"], d)], rn, [dx2, dh2], n_out, n_lat,
        row_grad_dtype=(F32, BF16), after=started)
    if n_out < t:
        dgate5, dshift3, dscale4, dgate2 = [g.at[1].set(0.0) for g in (dgate5, dshift3, dscale4, dgate2)]
    dm = matmul(tag + "out_dx", dy, s["g_out"], tb=True, view=out_view(0, True))
    dw_out = matmul(tag + "out_dw", s["m"], dy, ta=True, out_dtype=BF16)
    rows, pars = _post_args(s["o_att"], s["o_ret"], s["o_gla"], s["z"], w)
    started = grad_ready("w_out", dict(w_out=dw_out))
    do_att, do_ret, do_gla, d_rg, d_gr, d_ret_g, d_gla_g = row_vjp(tag + "post", post_tile, rows, pars, [dm], n_out, n_lat, after=started)
    do_att, do_ret, do_gla, d_rg, d_gr, dx1 = [all_rows(a) for a in (do_att, do_ret, do_gla, d_rg, d_gr, dx1)]
    dq_a, dk_a, dv_a = attn_bwd(s["p"], s["z"], s["o_att"], s["lse"], do_att, n_lat)
    dq_r, dk_r, dv_r, dlg = ret_bwd(s["p"], s["z"], w.lg, s["s_ret"], do_ret, n_lat)
    dq_g, dk_g, dv_g, dla = gla_bwd(s["p"], s["z"], s["s_gla"], do_gla, n_lat)
    dp = jnp.concatenate([dq_a, dk_a, _sum_dirs(dq_g), _sum_dirs(dq_r), _sum_dirs(dk_r), dla], axis=1)
    rows, pars = _prep_args(s["z"], s["zg"], cos, sin, w)
    d_zqk, d_zrq, d_zrk, d_zgq, dzg, d_qg, d_kg, d_up, d_gb = row_vjp(tag + "prep", prep_tile, rows, pars, [dp], t, n_lat)
    dz = jnp.concatenate([d_zqk, dv_a, d_zrq, d_zrk, _sum_dirs(dv_r), d_rg, d_zgq, _sum_dirs(dk_g), _sum_dirs(dv_g), d_gr], axis=1)
    dz, dzg = dz.astype(BF16), dzg.astype(BF16)
    dh_gate = matmul(tag + "gate_dx", dzg, s["w_gate"], tb=True)
    dh = matmul(tag + "in_dx", dz, s["w_main"], tb=True, add=dh_gate)
    dw_main = matmul(tag + "in_dw", s["h"], dz, ta=True, out_dtype=BF16)
    dw_gate = matmul(tag + "gate_dw", s["h"], dzg, ta=True, out_dtype=BF16)
    started = grad_ready("w_in", dict(w_main=dw_main, w_gate=dw_gate))
    nm1 = [Par(w.norm1_g), Par(_mod(mods, 0), True), Par(_mod(mods, 1), True)]
    dx, dg1, dshift0, dscale1 = row_vjp(tag + "norm1", normmod_tile, [Row(s["xs"], d)], nm1, [dh], t, n_lat,
                                        add_to_first=dx1, after=started)
    dmods = jnp.concatenate([dshift0, dscale1, dgate2, dshift3, dscale4, dgate5], axis=1)
    grads = dict(w_main=dw_main, w_gate=dw_gate, w_out=dw_out, w_up=dw_up, w_down=dw_down, norm1_g=dg1, q_g=d_qg, k_g=d_kg,
                 lg=dlg, ret_g=d_ret_g, gate_up=d_up, gate_b=d_gb, gla_g=d_gla_g, norm2_g=dg2, conv_w=dcw, conv_b=dcb)
    return dx, dmods, grads


def rope_tables(n_lat, n_ctx):
    rows = n_lat // GRID_W
    row = jnp.repeat(jnp.arange(rows, dtype=F32), GRID_W)
    col = jnp.tile(jnp.arange(GRID_W, dtype=F32), rows)
    n_freq = HEAD_DIM // 4
    inv_freq = ROPE_THETA ** (-jnp.arange(n_freq, dtype=F32) / n_freq)
    ang = jnp.concatenate([row[:, None] * inv_freq, col[:, None] * inv_freq], axis=-1)
    cos, sin = jnp.cos(ang), jnp.sin(ang)
    cos = jnp.concatenate([jnp.concatenate([cos, cos], axis=1), jnp.ones((n_ctx, HEAD_DIM), F32)], axis=0)
    sin = jnp.concatenate([jnp.concatenate([-sin, sin], axis=1), jnp.zeros((n_ctx, HEAD_DIM), F32)], axis=0)
    return cos, sin


def local_step(xs, target, mods, weights, fetch, final_g, n_lat, grad_ready):
    t, d = xs.shape
    cos, sin = rope_tables(n_lat, t - n_lat)
    saved = []
    h = xs
    for l, w in enumerate(weights):
        n_out = t if l + 1 < len(weights) else n_lat
        h, s = layer_fwd(l, h, mods[l], w, functools.partial(fetch, l), cos, sin, n_lat, n_out)
        saved.append(s)
    loss, dx, dgf = final_loss(h, target, final_g, n_lat)
    dmods, grads = [None] * len(weights), [None] * len(weights)
    for l in reversed(range(len(weights))):
        dx, dmods[l], grads[l] = layer_bwd(l, dx, saved[l], mods[l], weights[l], cos, sin, n_lat, functools.partial(grad_ready, l))
    return loss, dx, dmods, grads, dgf


WEIGHT_NAMES = ("c_ctx", "ada_w", "ada_b", "norm1_g", "w_in", "q_norm_g", "k_norm_g", "ret_log_decay", "ret_norm_g",
                "gla_gate_up", "gla_gate_b", "gla_norm_g", "w_out", "norm2_g", "w_up", "conv_w", "conv_b", "w_down", "final_norm_g")
PACK_QUANTUM = 8 * LANES


def _pack(arrays):
    flat = jnp.concatenate([a.reshape(-1).astype(F32) for a in arrays])
    n = -(-flat.shape[0] // PACK_QUANTUM) * PACK_QUANTUM
    return jnp.pad(flat, (0, n - flat.shape[0])).reshape(8, n // 8)


def _unpack(flat2d, shapes):
    out, at = [], 0
    for s in shapes:
        size = int(np.prod(s))
        out.append(flat2d[:, at:at + size].reshape((flat2d.shape[0],) + tuple(s)))
        at += size
    return out


def _per_device(gathered):
    return gathered.reshape(N_DEV, -1)


def _from_chips(per_device, axis):
    chips = per_device[0::2]
    moved = jnp.moveaxis(chips, 0, axis)
    shape = moved.shape
    return moved.reshape(shape[:axis] + (shape[axis] * shape[axis + 1],) + shape[axis + 2:])


def kernel(x, c, ctx, c_ctx, ada_w, ada_b, norm1_g, w_in, q_norm_g, k_norm_g, ret_log_decay, ret_norm_g, gla_gate_up, gla_gate_b, gla_norm_g, w_out, norm2_g, w_up, conv_w, conv_b, w_down, final_norm_g, loss_target, m_c_ctx, m_ada_w, m_ada_b, m_norm1_g, m_w_in, m_q_norm_g, m_k_norm_g, m_ret_log_decay, m_ret_norm_g, m_gla_gate_up, m_gla_gate_b, m_gla_norm_g, m_w_out, m_norm2_g, m_w_up, m_conv_w, m_conv_b, m_w_down, m_final_norm_g, v_c_ctx, v_ada_w, v_ada_b, v_norm1_g, v_w_in, v_q_norm_g, v_k_norm_g, v_ret_log_decay, v_ret_norm_g, v_gla_gate_up, v_gla_gate_b, v_gla_norm_g, v_w_out, v_norm2_g, v_w_up, v_conv_w, v_conv_b, v_w_down, v_final_norm_g):
    weights = dict(zip(WEIGHT_NAMES, (c_ctx, ada_w, ada_b, norm1_g, w_in, q_norm_g, k_norm_g, ret_log_decay, ret_norm_g,
                                      gla_gate_up, gla_gate_b, gla_norm_g, w_out, norm2_g, w_up, conv_w, conv_b, w_down, final_norm_g)))
    mom_m = dict(zip(WEIGHT_NAMES, (m_c_ctx, m_ada_w, m_ada_b, m_norm1_g, m_w_in, m_q_norm_g, m_k_norm_g, m_ret_log_decay, m_ret_norm_g,
                                    m_gla_gate_up, m_gla_gate_b, m_gla_norm_g, m_w_out, m_norm2_g, m_w_up, m_conv_w, m_conv_b, m_w_down, m_final_norm_g)))
    mom_v = dict(zip(WEIGHT_NAMES, (v_c_ctx, v_ada_w, v_ada_b, v_norm1_g, v_w_in, v_q_norm_g, v_k_norm_g, v_ret_log_decay, v_ret_norm_g,
                                    v_gla_gate_up, v_gla_gate_b, v_gla_norm_g, v_w_out, v_norm2_g, v_w_up, v_conv_w, v_conv_b, v_w_down, v_final_norm_g)))
    depth, d = norm1_g.shape
    assert d == D_MODEL and x.shape[0] == 1
    n_lat, n_ctx, f = x.shape[1], ctx.shape[1], conv_b.shape[1]
    assert n_lat % ROW_TILE == 0 and n_ctx % ROW_TILE == 0 and f % FFN_COL_TILE == 0 and f % N_CHIPS == 0
    n_in = w_in.shape[2]
    n_ada = ada_w.shape[2]
    xi, yi, ci = lax.axis_index("x"), lax.axis_index("y"), lax.axis_index("c")
    chip = 2 * xi + yi
    dev = 2 * chip + ci

    big = ("w_in", "w_out", "w_up", "w_down")
    order = [(l, name) for l in range(depth) for name in big]
    passing = {}

    def pass_on(k, after):
        tag = "{1}{0}".format(*order[k])
        own, land = split_wait("gather_wait_" + tag, GATHER, in_flight[k], after)
        (moving,), started = split_start("gather_pass_" + tag, PASS_ON, [land])
        passing[k] = (own, moving)
        return started

    def fetch(l, name, after):
        k = order.index((l, name))
        if k == 0:
            pass_on(0, after)
        own, moving = passing.pop(k)
        (land,) = split_wait(f"gather_pass_wait_{name}{l}", PASS_ON, moving, after)
        started = pass_on(k + 1, after) if k + 1 < len(order) else None
        land = lax.dynamic_update_slice_in_dim(land, own[None], chip, axis=0)
        if name != "w_in":
            return land, started
        last = N_MAIN - (N_CHIPS - 1) * n_in
        w_main = jnp.concatenate([land[q] for q in range(N_CHIPS - 1)] + [land[N_CHIPS - 1][:, :last]], axis=1)
        return (w_main, jnp.pad(land[N_CHIPS - 1][:, last:], ((0, 0), (0, LANES - N_GATE)))), started

    small_shapes = [c.shape[1:], conv_w.shape, gla_gate_up.shape, gla_gate_b.shape]
    got = _per_device(all_gather_small("gather_small", _pack([c, conv_w, gla_gate_up, gla_gate_b])))
    c_all, conv_w_sh, gate_up_sh, gate_b_sh = _unpack(got, small_shapes)
    conv_w_full = _from_chips(conv_w_sh, 2)
    gate_up_full = _from_chips(gate_up_sh, 3)
    gate_b_full = _from_chips(gate_b_sh, 2)

    act = jnp.zeros((16, d), F32).at[0:N_DEV].set(jax.nn.silu(c_all)).at[N_DEV].set(jax.nn.silu(c_ctx))
    mod_sh = jnp.stack([matmul(f"ada_fwd{l}", act, ada_w, view=ada_view(l, n_ada, False)) for l in range(depth)])
    got = _per_device(all_gather_small("gather_mods", _pack([mod_sh])))
    (mod_sh_all,) = _unpack(got, [mod_sh.shape])
    mod_full = _from_chips(mod_sh_all, 2) + ada_b[:, None, :]
    mod_mine = lax.dynamic_index_in_dim(mod_full, dev, axis=1, keepdims=False)
    mods = [jnp.stack([mod_mine[l].reshape(N_MOD, d), mod_full[l, N_DEV].reshape(N_MOD, d)]) for l in range(depth)]
    in_flight, token = [], None
    for l in range(depth):
        behind = 0.0 if token is None else token[0, 0]
        shards = [(weights[name][l] + behind).astype(BF16) for name in big]
        started, token = split_start(f"gather_start{l}", GATHER, shards, [(N_CHIPS,) + s.shape for s in shards],
                                     after=mod_full if l == 0 else None)
        in_flight += started

    layer_w = []
    for l in range(depth):
        up = jnp.zeros((2, LANES, GLA_HEADS * GLA_DK), F32)
        up = up.at[0, 0:GLA_RANK].set(gate_up_full[l, 0]).at[1, GLA_RANK:2 * GLA_RANK].set(gate_up_full[l, 1])
        layer_w.append(LayerWeights(
            norm1_g=norm1_g[l].reshape(1, 1, d), q_g=q_norm_g[l].reshape(1, 1, HEAD_DIM), k_g=k_norm_g[l].reshape(1, 1, HEAD_DIM),
            lg=ret_log_decay[l].reshape(2, RET_HEADS, 1, 1), ret_g=ret_norm_g[l].reshape(1, 1, HEAD_DIM),
            gate_up=up.reshape(1, 2 * LANES, -1), gate_b=gate_b_full[l].reshape(1, 2, -1), gla_g=gla_norm_g[l].reshape(1, 1, HEAD_DIM),
            norm2_g=norm2_g[l].reshape(1, 1, d), conv_w=conv_w_full[l], conv_b=conv_b[l].reshape(1, f)))

    def pieces_of(name, g):
        if name == "w_in":
            shards = [g["w_main"][:, q * n_in:(q + 1) * n_in] for q in range(N_CHIPS - 1)]
            tail = jnp.concatenate([g["w_main"][:, (N_CHIPS - 1) * n_in:], g["w_gate"][:, :N_GATE]], axis=1)
            return jnp.stack(shards + [tail])
        if name == "w_up":
            return g["w_up"]
        return g[name].reshape(N_CHIPS, -1, d)

    groups = {"ffn": ("w_up", "w_down"), "w_out": ("w_out",), "w_in": ("w_in",)}
    reducing = {}
    to_sibling = []

    def sibling_arrived(after):
        started = None
        while to_sibling:
            l, group, in_flight_halves = to_sibling.pop(0)
            sums = []
            for name, halves in zip(groups[group], in_flight_halves):
                pieces, from_sibling = split_wait(f"rs_sibling_wait_{name}{l}", SIBLING_HALF, halves, after)
                sums.append(add_sibling_half(f"rs_add_sibling_{name}{l}", pieces, from_sibling, ci))
            in_flight_sums, token = split_start(f"rs_start_{group}{l}", SCATTER, sums, [(3,) + s.shape[1:] for s in sums])
            reducing.update({(l, name): grp for name, grp in zip(groups[group], in_flight_sums)})
            started = token if started is None else started + token
        return started

    def grad_ready(l, group, g):
        pieces = [pieces_of(name, g) for name in groups[group]]
        before = None if (l, group) == (0, "w_in") else sibling_arrived(pieces[0])
        in_flight_halves, started = split_start(f"rs_sibling_{group}{l}", SIBLING_HALF, pieces,
                                                [(N_CHIPS, pc.shape[1] // 2, pc.shape[2]) for pc in pieces])
        to_sibling.append((l, group, in_flight_halves))
        return started if before is None else started + before

    xs = jnp.concatenate([x[0], ctx[0]], axis=0) + token[0, 0]
    loss, dx, dmods, grads, dgf = local_step(xs, loss_target[0], mods, layer_w, fetch, final_norm_g.reshape(1, d), n_lat, grad_ready)

    def gate_up_grad(g):
        return jnp.stack([g[0, 0:GLA_RANK], g[0, LANES + GLA_RANK:LANES + 2 * GLA_RANK]])

    per_layer = [[dmods[l][0], dmods[l][1], grads[l]["norm1_g"], grads[l]["norm2_g"], grads[l]["q_g"], grads[l]["k_g"],
                  grads[l]["ret_g"], grads[l]["gla_g"], grads[l]["lg"], gate_up_grad(grads[l]["gate_up"]), grads[l]["gate_b"],
                  grads[l]["conv_w"], grads[l]["conv_b"]] for l in range(depth)]
    layer_shapes = [(N_MOD * d,), (N_MOD * d,), (d,), (d,), (HEAD_DIM,), (HEAD_DIM,), (HEAD_DIM,), (HEAD_DIM,), (2, RET_HEADS),
                    (2, GLA_RANK, GLA_HEADS * GLA_DK), (2, GLA_HEADS * GLA_DK), (3, f), (f,)]
    packed = _pack([a for lay in per_layer for a in lay] + [dgf, loss[0, 0:1]])
    gathered = all_gather_small("gather_small_grads", packed)
    every = _unpack(_per_device(gathered), layer_shapes * depth + [(d,), (1,)])
    total = _unpack(sum_device_blocks("sum_small_grads", gathered).reshape(1, -1), layer_shapes * depth + [(d,), (1,)])
    nl = len(layer_shapes)

    def tot(l, k):
        return total[l * nl + k][0]

    out = {"norm1_g": jnp.stack([tot(l, 2) for l in range(depth)]), "norm2_g": jnp.stack([tot(l, 3) for l in range(depth)]),
           "q_norm_g": jnp.stack([tot(l, 4) for l in range(depth)]), "k_norm_g": jnp.stack([tot(l, 5) for l in range(depth)]),
           "ret_norm_g": jnp.stack([tot(l, 6) for l in range(depth)]), "gla_norm_g": jnp.stack([tot(l, 7) for l in range(depth)]),
           "ret_log_decay": jnp.stack([tot(l, 8) for l in range(depth)]),
           "gla_gate_up": lax.dynamic_slice_in_dim(jnp.stack([tot(l, 9) for l in range(depth)]), chip * gla_gate_up.shape[3], gla_gate_up.shape[3], axis=3),
           "gla_gate_b": lax.dynamic_slice_in_dim(jnp.stack([tot(l, 10) for l in range(depth)]), chip * gla_gate_b.shape[2], gla_gate_b.shape[2], axis=2),
           "conv_w": lax.dynamic_slice_in_dim(jnp.stack([tot(l, 11) for l in range(depth)]), chip * conv_w.shape[2], conv_w.shape[2], axis=2),
           "conv_b": jnp.stack([tot(l, 12) for l in range(depth)]),
           "final_norm_g": total[depth * nl][0],
           "ada_b": jnp.stack([tot(l, 0) + tot(l, 1) for l in range(depth)])}
    loss_total = total[depth * nl + 1][0, 0]

    dmod_all = jnp.zeros((depth, 16, N_MOD * d), F32)
    for l in range(depth):
        dmod_all = dmod_all.at[l, 0:N_DEV].set(every[l * nl][:, :]).at[l, N_DEV].set(tot(l, 1))
    dmod_cols = lax.dynamic_slice_in_dim(dmod_all, chip * n_ada, n_ada, axis=2)
    for l in range(depth):
        slab = OView((depth, d, n_ada), functools.partial(lambda i, j, kk, l: (l, i, j), l=l), None, out.get("ada_w"))
        out["ada_w"] = matmul(f"ada_dw{l}", act, dmod_cols[l], ta=True, o_view=slab)
    dact = matmul("ada_dx0", dmod_cols[0], ada_w, tb=True, view=ada_view(0, n_ada, True))
    for l in range(1, depth):
        dact = matmul(f"ada_dx{l}", dmod_cols[l], ada_w, tb=True, view=ada_view(l, n_ada, True), add=dact)
    got = _per_device(all_gather_small("gather_dcctx", _pack([dact[N_DEV]])))
    sibling_arrived(got)
    got = got[0::2, :d]
    dsilu = got[0] + got[1] + got[2] + got[3]
    sig = jax.nn.sigmoid(c_ctx)
    out["c_ctx"] = dsilu * (sig + c_ctx * sig * (1.0 - sig))

    deltas, new_m, new_v = {}, {}, {}

    def update(name):
        out[name] = out[name].reshape(weights[name].shape)
        deltas[name], new_m[name], new_v[name] = adamw("adamw_" + name, weights[name], out[name], mom_m[name], mom_v[name])

    for name in WEIGHT_NAMES:
        if name not in big:
            update(name)
    behind = new_v["ada_w"]
    joining = []

    def joined(after):
        name, in_flight_halves = joining.pop()
        per_layer = [split_wait(f"rs_join_wait_{name}{l}", JOIN, grp, after)[0] for l, grp in enumerate(in_flight_halves)]
        (deltas[name], new_m[name], new_v[name]), out[name] = adamw_layers(
            "adamw_" + name, weights[name], per_layer, mom_m[name], mom_v[name])
        return new_v[name]

    for name in ("w_down", "w_up", "w_out", "w_in"):
        halves = []
        for l in range(depth):
            sums, got = split_wait(f"rs_wait_{name}{l}", SCATTER, reducing[(l, name)], behind)
            halves.append(add_chip_sums(f"rs_add_chips_{name}{l}", sums, got, chip, ci))
        in_flight_halves, _ = split_start("rs_join_" + name, JOIN, halves)
        if joining:
            behind = joined(behind)
        joining.append((name, in_flight_halves))
    joined(behind)
    grad_x = dx[:n_lat].reshape(x.shape)
    return (loss_total, grad_x, *[out[n] for n in WEIGHT_NAMES], *[deltas[n] for n in WEIGHT_NAMES],
            *[new_m[n] for n in WEIGHT_NAMES], *[new_v[n] for n in WEIGHT_NAMES])
```

```python
import functools
from typing import NamedTuple

import numpy as np
import jax
import jax.numpy as jnp
from jax import lax
from jax.experimental import pallas as pl
from jax.experimental.pallas import tpu as pltpu

F32 = jnp.float32
BF16 = jnp.bfloat16

D_MODEL = 2048
HEAD_DIM = 128
ATT_Q_HEADS = 8
ATT_KV_HEADS = 2
ATT_GROUP = ATT_Q_HEADS // ATT_KV_HEADS
RET_HEADS = 4
GLA_HEADS = 4
GLA_DK = 64
GLA_DV = 128
GLA_RANK = 16
GLA_TAU = 16.0
RET_CHUNK = 256
GLA_CHUNK = 128
GRID_W = 64
ROPE_THETA = 10000.0
N_MOD = 6
EPS = 1e-6
N_MAIN = 5120
N_GATE = 2 * GLA_RANK
LANES = 128
ROW_TILE = 256
FFN_COL_TILE = 256
VMEM_LIMIT = 56 * 1024 * 1024

ADAM_LR = 0.001
ADAM_B1 = 0.9
ADAM_B2 = 0.999
ADAM_EPS = 1e-08
ADAM_WD = 0.01
ADAM_STEP = 10

Z_AQ, Z_AK, Z_AV = 0, 1024, 1280
Z_RQ, Z_RK, Z_RV, Z_RG = 1536, 2048, 2560, 3072
Z_GQ, Z_GK, Z_GV, Z_GR = 3584, 3840, 4096, 4608
P_AQ, P_AK, P_GQ, P_RQ, P_RK, P_LA = 0, 1024, 1280, 1536, 2048, 2560
P_W = 3072


def _params(sem=None):
    return pltpu.CompilerParams(dimension_semantics=sem, vmem_limit_bytes=VMEM_LIMIT)


def _pick(n, cands):
    for c in cands:
        if n % c == 0:
            return c
    return n


_NN = (((1,), (0,)), ((), ()))
_NT = (((1,), (1,)), ((), ()))
_TN = (((0,), (0,)), ((), ()))


def _dg(a, b, dims):
    return lax.dot_general(a.astype(BF16), b.astype(BF16), dims, preferred_element_type=F32)


@jax.custom_vjp
def bdot(a, b):
    return _dg(a, b, _NN)


def _bdot_fwd(a, b):
    return _dg(a, b, _NN), (a, b)


def _bdot_bwd(res, ct):
    a, b = res
    return _dg(ct, b, _NT), _dg(a, ct, _TN)


bdot.defvjp(_bdot_fwd, _bdot_bwd)


@jax.custom_vjp
def bdot_nt(a, b):
    return _dg(a, b, _NT)


def _bdot_nt_fwd(a, b):
    return _dg(a, b, _NT), (a, b)


def _bdot_nt_bwd(res, ct):
    a, b = res
    return _dg(ct, b, _NN), _dg(ct, a, _TN)


bdot_nt.defvjp(_bdot_nt_fwd, _bdot_nt_bwd)


@jax.custom_vjp
def bdot_tn(a, b):
    return _dg(a, b, _TN)


def _bdot_tn_fwd(a, b):
    return _dg(a, b, _TN), (a, b)


def _bdot_tn_bwd(res, ct):
    a, b = res
    return _dg(b, ct, _NT), _dg(a, ct, _NN)


bdot_tn.defvjp(_bdot_tn_fwd, _bdot_tn_bwd)


def _split3(x):
    x1 = x.astype(BF16)
    r1 = x - x1.astype(F32)
    x2 = r1.astype(BF16)
    x3 = (r1 - x2.astype(F32)).astype(BF16)
    return x1, x2, x3


def _mask_dot(mask_bf16, x, dims):
    x1, x2, x3 = _split3(x)
    f = lambda t: lax.dot_general(mask_bf16, t, dims, preferred_element_type=F32)
    return f(x1) + f(x2) + f(x3)


@jax.custom_vjp
def mask_cumsum(mask, x):
    return _mask_dot(mask.astype(BF16), x, _NN)


def _mask_cumsum_fwd(mask, x):
    return mask_cumsum(mask, x), mask


def _mask_cumsum_bwd(mask, ct):
    return jnp.zeros_like(mask), _mask_dot(mask.astype(BF16), ct, _TN)


mask_cumsum.defvjp(_mask_cumsum_fwd, _mask_cumsum_bwd)


def _roll(x, shift, axis):
    return pltpu.roll(x, shift % x.shape[axis], axis)


@functools.partial(jax.custom_vjp, nondiff_argnums=(1, 2))
def roll(x, shift, axis):
    return _roll(x, shift, axis)


def _roll_fwd(x, shift, axis):
    return _roll(x, shift, axis), None


def _roll_bwd(shift, axis, _, ct):
    return (_roll(ct, -shift, axis),)


roll.defvjp(_roll_fwd, _roll_bwd)


def rms(x):
    return x * lax.rsqrt(jnp.mean(x * x, axis=-1, keepdims=True) + EPS)


def silu(x):
    return x * (0.5 + 0.5 * jnp.tanh(0.5 * x))


def log_sigmoid(x):
    return jnp.minimum(x, 0.0) - jnp.log(1.0 + jnp.exp(-jnp.abs(x)))


def rope(t, cos, sin):
    return t * cos + roll(t, HEAD_DIM // 2, 1) * sin


def _heads(x, n, width=HEAD_DIM):
    return [x[:, h * width:(h + 1) * width] for h in range(n)]


class Row(NamedTuple):
    arr: jax.Array
    width: int
    idx: int = 0
    diff: bool = True


class Par(NamedTuple):
    arr: jax.Array
    grouped: bool = False
    diff: bool = True


def _row_specs(rows, pars, tm, n_lat_tiles):
    def grp(i):
        return jnp.minimum(i // n_lat_tiles, 1)

    specs = [pl.BlockSpec((tm, r.width), functools.partial(lambda i, k: (i, k), k=r.idx)) for r in rows]
    for p in pars:
        blk = (1,) + p.arr.shape[1:]
        if p.grouped:
            specs.append(pl.BlockSpec(blk, lambda i: (grp(i), 0, 0)))
        else:
            specs.append(pl.BlockSpec(blk, lambda i: (0, 0, 0)))
    return specs


def row_map(name, fn, rows, pars, outs, n_rows, n_lat):
    tm = ROW_TILE
    nr, npar = len(rows), len(pars)

    def body(*refs):
        vals = [r[...].astype(F32) for r in refs[:nr]] + [p[0] for p in refs[nr:nr + npar]]
        res = fn(*vals)
        for o, v in zip(refs[nr + npar:], res):
            o[...] = v.astype(o.dtype)

    return pl.pallas_call(
        body, name=name, grid=(n_rows // tm,),
        in_specs=_row_specs(rows, pars, tm, n_lat // tm),
        out_specs=[pl.BlockSpec((tm, w), lambda i: (i, 0)) for w, _ in outs],
        out_shape=[jax.ShapeDtypeStruct((n_rows, w), dt) for w, dt in outs],
        compiler_params=_params(("arbitrary",)),
    )(*[r.arr for r in rows], *[p.arr for p in pars])


def row_vjp(name, fn, rows, pars, cts, n_rows, n_lat, add_to_first=None, row_grad_dtype=F32, after=None):
    tm = ROW_TILE
    nr, npar, nc = len(rows), len(pars), len(cts)
    n_lat_tiles = n_lat // tm
    args = list(rows) + list(pars)
    diff_pos = [k for k, a in enumerate(args) if a.diff]
    n_add = 0 if add_to_first is None else 1
    n_after = 0 if after is None else 1

    def body(*refs):
        i = pl.program_id(0)
        vals = [r[...].astype(F32) for r in refs[:nr]] + [p[0] for p in refs[nr:nr + npar]]
        ct_vals = tuple(c[...] for c in refs[nr + npar:nr + npar + nc])
        out_refs = refs[nr + npar + nc + n_add + n_after:]

        def g(*dv):
            full = list(vals)
            for k, v in zip(diff_pos, dv):
                full[k] = v
            return tuple(fn(*full))

        _, vjp = jax.vjp(g, *[vals[k] for k in diff_pos])
        grads = vjp(ct_vals)
        for n, (k, o, gr) in enumerate(zip(diff_pos, out_refs, grads)):
            if k < nr:
                o[...] = (gr + refs[nr + npar + nc][...] if (n == 0 and n_add) else gr).astype(o.dtype)
            else:
                first = (i == 0) | (i == n_lat_tiles) if args[k].grouped else (i == 0)

                @pl.when(first)
                def _():
                    o[0] = gr

                @pl.when(jnp.logical_not(first))
                def _():
                    o[0] += gr

    def grp(i):
        return jnp.minimum(i // n_lat_tiles, 1)

    out_specs, out_shape = [], []
    for k in diff_pos:
        a = args[k]
        if k < nr:
            out_specs.append(pl.BlockSpec((tm, a.width), lambda i: (i, 0)))
            dtype = row_grad_dtype[len(out_shape)] if isinstance(row_grad_dtype, tuple) else row_grad_dtype
            out_shape.append(jax.ShapeDtypeStruct((n_rows, a.width), dtype))
        else:
            blk = (1,) + a.arr.shape[1:]
            out_specs.append(pl.BlockSpec(blk, (lambda i: (grp(i), 0, 0)) if a.grouped else (lambda i: (0, 0, 0))))
            out_shape.append(jax.ShapeDtypeStruct(a.arr.shape, F32))
    extra = list(cts) + ([add_to_first] if n_add else [])
    ct_specs = [pl.BlockSpec((tm, c.shape[1]), lambda i: (i, 0)) for c in extra]
    if n_after:
        extra.append(after)
        ct_specs.append(pl.BlockSpec(memory_space=pl.ANY))
    return pl.pallas_call(
        body, name=name, grid=(n_rows // tm,),
        in_specs=_row_specs(rows, pars, tm, n_lat_tiles) + ct_specs,
        out_specs=out_specs, out_shape=out_shape,
        compiler_params=_params(("arbitrary",)),
    )(*[r.arr for r in rows], *[p.arr for p in pars], *extra)


class BView(NamedTuple):
    n: int
    k: int
    tn: int
    tk: int
    index_map: object
    lead: int = 1
    part_maps: tuple = ()


MATMUL_VMEM_BUDGET = 40 * 1024 * 1024


def _matmul_tiles(m, n, k, a_bytes, b_bytes, o_bytes):
    tms = [c for c in (1152, 1024, 768, 512, 256, 128) if m % c == 0] or [m]
    tns = [c for c in (2048, 1408, 1280, 1024, 768, 512, 256, 128) if n % c == 0] or [n]
    tks = [k] + [c for c in (2816, 2304, 2048, 1408, 1024, 512, 256, 128) if k % c == 0 and c < k]
    for tk in tks:
        fits = [(tm * tn, tm, tn) for tm in tms for tn in tns
                if 2 * (tm * tk * a_bytes + tk * tn * b_bytes + tm * tn * o_bytes) + 2 * tm * tn * 4 <= MATMUL_VMEM_BUDGET]
        if fits and (max(fits)[0] >= min(512 * 512, tms[0] * tns[0]) or tk == tks[-1]):
            _, tm, tn = max(fits)
            return tm, tn, tk
    raise ValueError(f"no matmul tiling for {(m, n, k)}")


class OView(NamedTuple):
    shape: tuple
    index_map: object
    tn: int = None
    into: object = None


def matmul(name, a, b, *, ta=False, tb=False, add=None, out_dtype=F32, view=None, o_view=None, after=None):
    m = a.shape[1] if ta else a.shape[0]
    o_bytes = jnp.dtype(out_dtype).itemsize * (1 if add is None else 2)
    if view is None:
        k = a.shape[0] if ta else a.shape[1]
        n = b.shape[0] if tb else b.shape[1]
        assert (b.shape[1] if tb else b.shape[0]) == k, (a.shape, b.shape, ta, tb)
        if o_view is not None and o_view.tn is not None:
            tn = o_view.tn
            tm, _, tk = _matmul_tiles(m, tn, k, a.dtype.itemsize, b.dtype.itemsize, o_bytes)
        else:
            tm, tn, tk = _matmul_tiles(m, n, k, a.dtype.itemsize, b.dtype.itemsize, o_bytes)
    else:
        n, k, tn, tk = view.n, view.k, view.tn, view.tk
        b_maps = view.part_maps or (view.index_map,)
        tm, _, whole = _matmul_tiles(m, tn, tk * len(b_maps), a.dtype.itemsize, b.dtype.itemsize, o_bytes)
        assert whole == tk * len(b_maps) and not (ta and len(b_maps) > 1), (name, tm, whole)
    parts = 1 if view is None else len(b_maps)
    k_step = tk * parts
    nk = k // k_step
    dims = (((0 if ta else 1,), (1 if tb else 0,)), ((), ()))

    def body(a_ref, *rest):
        b_refs, rest = rest[:parts], rest[parts:]
        if parts == 1:
            prod = lax.dot_general(a_ref[...].astype(BF16), b_refs[0][...].astype(BF16), dims, preferred_element_type=F32)
        else:
            prod = sum(lax.dot_general(a_ref[:, p * tk:(p + 1) * tk].astype(BF16), b_refs[p][...].astype(BF16), dims,
                                       preferred_element_type=F32) for p in range(parts))
        if nk == 1:
            o_ref = rest[-1]
            o_ref[...] = (prod if add is None else prod + rest[0][...]).astype(o_ref.dtype)
            return
        o_ref, acc = rest[-2:]
        kk = pl.program_id(2)

        @pl.when(kk == 0)
        def _():
            acc[...] = prod

        @pl.when(kk != 0)
        def _():
            acc[...] += prod

        @pl.when(kk == nk - 1)
        def _():
            r = acc[...]
            if add is not None:
                r = r + rest[0][...]
            o_ref[...] = r.astype(o_ref.dtype)

    if ta:
        a_spec = pl.BlockSpec((k_step, tm), lambda i, j, kk: (kk, i))
    else:
        a_spec = pl.BlockSpec((tm, k_step), lambda i, j, kk: (i, kk))
    b_tile = (tn, tk) if tb else (tk, tn)
    if view is not None:
        b_specs = [pl.BlockSpec((None,) * view.lead + b_tile, index_map) for index_map in b_maps]
    elif tb:
        b_specs = [pl.BlockSpec(b_tile, lambda i, j, kk: (j, kk))]
    else:
        b_specs = [pl.BlockSpec(b_tile, lambda i, j, kk: (kk, j))]
    o_spec = pl.BlockSpec((tm, tn), lambda i, j, kk: (i, j))
    ins = [a] + [b] * parts + ([add] if add is not None else [])
    in_specs = [a_spec] + b_specs + ([o_spec] if add is not None else [])
    out_shape, aliases = jax.ShapeDtypeStruct((m, n), out_dtype), {}
    if o_view is not None:
        assert add is None
        o_spec = pl.BlockSpec((None, tm, tn), o_view.index_map)
        out_shape = jax.ShapeDtypeStruct(o_view.shape, out_dtype)
        if o_view.into is not None:
            aliases = {len(ins): 0}
            ins.append(o_view.into)
            in_specs.append(pl.BlockSpec(memory_space=pl.ANY))
    if after is not None:
        ins.append(after)
        in_specs.append(pl.BlockSpec(memory_space=pl.ANY))
    return pl.pallas_call(
        body, name=name, grid=(m // tm, n // tn, nk),
        in_specs=in_specs, out_specs=o_spec, out_shape=out_shape, input_output_aliases=aliases,
        scratch_shapes=[pltpu.VMEM((tm, tn), F32)] if nk > 1 else [],
        compiler_params=_params(("parallel", "parallel", "arbitrary")),
    )(*ins)


def normmod_tile(x, g, shift, scale):
    return (rms(x) * g * (1.0 + scale) + shift,)


def resid_tile(x, y, gate):
    return (x + gate * y,)


def resid_norm_tile(x, y, gate, g, shift, scale):
    x1 = x + gate * y
    return x1, rms(x1) * g * (1.0 + scale) + shift


def gated_tile(y, gate):
    return (gate * y,)


def prep_tile(z_qk, z_rq, z_rk, z_gq, zg, cos, sin, qg, kg, gate_up, gate_b):
    out = []
    for h, t in enumerate(_heads(z_qk, ATT_Q_HEADS + ATT_KV_HEADS)):
        out.append(rope(rms(t) * (qg if h < ATT_Q_HEADS else kg), cos, sin))
    gq = z_gq * (GLA_DK ** -0.5)
    rq = [rope(t, cos, sin) for t in _heads(z_rq, RET_HEADS)]
    rk = [rope(t * (HEAD_DIM ** -0.5), cos, sin) for t in _heads(z_rk, RET_HEADS)]
    la = [log_sigmoid(bdot(zg, gate_up[d * LANES:(d + 1) * LANES]) + gate_b[d:d + 1]) * (1.0 / GLA_TAU) for d in range(2)]
    return (jnp.concatenate(out + [gq] + rq + rk + la, axis=1),)


def post_tile(o_att, o_ret_f, o_ret_b, o_gla_f, o_gla_b, rg, gr, ret_g, gla_g):
    ret = jnp.concatenate([rms(t) * ret_g for t in _heads(o_ret_f + o_ret_b, RET_HEADS)], axis=1) * silu(rg)
    gla = jnp.concatenate([rms(t) * gla_g for t in _heads(o_gla_f + o_gla_b, GLA_HEADS)], axis=1) * silu(gr)
    return (jnp.concatenate([o_att, ret, gla], axis=1),)


def _convglu_tile(n_lat, a, v, cw, cb):
    t = a.shape[0]
    row = lax.broadcasted_iota(jnp.int32, (t, 1), 0)
    has_prev = ((row != 0) & (row != n_lat)).astype(F32)
    has_next = ((row != n_lat - 1) & (row != t - 1)).astype(F32)
    conv = roll(a, 1, 0) * has_prev * cw[0:1] + a * cw[1:2] + roll(a, -1, 0) * has_next * cw[2:3] + cb
    return silu(conv) * v


def convglu(name, u, cw, cb, n_lat):
    t, f2 = u.shape
    f, tc = f2 // 2, FFN_COL_TILE
    nb = f // tc

    def body(a_ref, v_ref, cw_ref, cb_ref, o_ref):
        o_ref[...] = _convglu_tile(n_lat, a_ref[...].astype(F32), v_ref[...].astype(F32), cw_ref[...], cb_ref[...]).astype(o_ref.dtype)

    return pl.pallas_call(
        body, name=name, grid=(nb,),
        in_specs=[pl.BlockSpec((t, tc), lambda j: (0, j)), pl.BlockSpec((t, tc), lambda j: (0, nb + j)),
                  pl.BlockSpec((3, tc), lambda j: (0, j)), pl.BlockSpec((1, tc), lambda j: (0, j))],
        out_specs=pl.BlockSpec((t, tc), lambda j: (0, j)),
        out_shape=jax.ShapeDtypeStruct((t, f), BF16),
        compiler_params=_params(("parallel",)),
    )(u, u, cw, cb)


def convglu_bwd(name, u, cw, cb, dg, n_lat):
    t, f2 = u.shape
    f, tc = f2 // 2, FFN_COL_TILE
    nb = f // tc

    def body(a_ref, v_ref, cw_ref, cb_ref, dg_ref, da_ref, dv_ref, dcw_ref, dcb_ref):
        _, vjp = jax.vjp(functools.partial(_convglu_tile, n_lat), a_ref[...].astype(F32), v_ref[...].astype(F32),
                         cw_ref[...], cb_ref[...])
        da, dv, dcw_ref[...], dcb_ref[...] = vjp(dg_ref[...])
        da_ref[...], dv_ref[...] = da.astype(BF16), dv.astype(BF16)

    col = pl.BlockSpec((t, tc), lambda j: (0, j))
    return pl.pallas_call(
        body, name=name, grid=(nb,),
        in_specs=[col, pl.BlockSpec((t, tc), lambda j: (0, nb + j)), pl.BlockSpec((3, tc), lambda j: (0, j)),
                  pl.BlockSpec((1, tc), lambda j: (0, j)), col],
        out_specs=[col, col, pl.BlockSpec((3, tc), lambda j: (0, j)), pl.BlockSpec((1, tc), lambda j: (0, j))],
        out_shape=[jax.ShapeDtypeStruct((t, f), BF16), jax.ShapeDtypeStruct((t, f), BF16),
                   jax.ShapeDtypeStruct((3, f), F32), jax.ShapeDtypeStruct((1, f), F32)],
        compiler_params=_params(("parallel",)),
    )(u, u, cw, cb, dg)


def final_loss(x, target, g, n_lat):
    tm = ROW_TILE
    d = x.shape[1]

    def body(x_ref, t_ref, g_ref, loss_ref, dx_ref, dg_ref):
        i = pl.program_id(0)
        tgt = t_ref[...]

        def f(xv, gv):
            e = rms(xv) * gv - tgt
            s = jnp.sum(jnp.sum(e * e, axis=1, keepdims=True), axis=0, keepdims=True)
            return s * (0.5 / d)

        val, vjp = jax.vjp(f, x_ref[...], g_ref[...])
        dx, dgv = vjp(jnp.ones((1, 1), F32))
        dx_ref[...] = dx

        @pl.when(i == 0)
        def _():
            dg_ref[...] = dgv
            loss_ref[...] = jnp.broadcast_to(val, loss_ref.shape)

        @pl.when(i != 0)
        def _():
            dg_ref[...] += dgv
            loss_ref[...] += jnp.broadcast_to(val, loss_ref.shape)

    return pl.pallas_call(
        body, name="final_loss", grid=(n_lat // tm,),
        in_specs=[pl.BlockSpec((tm, d), lambda i: (i, 0)), pl.BlockSpec((tm, d), lambda i: (i, 0)),
                  pl.BlockSpec((1, d), lambda i: (0, 0))],
        out_specs=[pl.BlockSpec((1, LANES), lambda i: (0, 0)), pl.BlockSpec((tm, d), lambda i: (i, 0)),
                   pl.BlockSpec((1, d), lambda i: (0, 0))],
        out_shape=[jax.ShapeDtypeStruct((1, LANES), F32), jax.ShapeDtypeStruct((n_lat, d), F32),
                   jax.ShapeDtypeStruct((1, d), F32)],
        compiler_params=_params(("arbitrary",)),
    )(x, target, g)


ATT_SCALE = HEAD_DIM ** -0.5
_AK_BLK = P_AK // HEAD_DIM
_AV_BLK = Z_AV // HEAD_DIM


def _att_specs(t, tq):
    gw = ATT_GROUP * HEAD_DIM
    q_spec = pl.BlockSpec((tq, gw), lambda kv, i: (i, kv))
    k_spec = pl.BlockSpec((t, HEAD_DIM), lambda kv, i: (0, _AK_BLK + kv))
    v_spec = pl.BlockSpec((t, HEAD_DIM), lambda kv, i: (0, _AV_BLK + kv))
    row_spec = pl.BlockSpec((ATT_GROUP, tq, 1), lambda kv, i: (kv, i, 0))
    return q_spec, k_spec, v_spec, row_spec


def _att_mask(i, t, tq, n_lat):
    col = lax.broadcasted_iota(jnp.int32, (1, t), 1)
    return jnp.where((i >= n_lat // tq) & (col < n_lat), -jnp.inf, 0.0).astype(F32)


def attn_fwd(p, z, n_lat):
    t = p.shape[0]
    tq = ROW_TILE

    def body(q_ref, k_ref, v_ref, o_ref, lse_ref):
        mask = _att_mask(pl.program_id(1), t, tq, n_lat)
        k, v = k_ref[...].astype(BF16), v_ref[...].astype(BF16)
        for g in range(ATT_GROUP):
            cols = slice(g * HEAD_DIM, (g + 1) * HEAD_DIM)
            s = _dg(q_ref[:, cols], k, _NT) * ATT_SCALE + mask
            m = jnp.max(s, axis=1, keepdims=True)
            pr = jnp.exp(s - m)
            l = jnp.sum(pr, axis=1, keepdims=True)
            o_ref[:, cols] = _dg(pr, v, _NN) / l
            lse_ref[g] = m + jnp.log(l)

    q_spec, k_spec, v_spec, row_spec = _att_specs(t, tq)
    return pl.pallas_call(
        body, name="attn_fwd", grid=(ATT_KV_HEADS, t // tq),
        in_specs=[q_spec, k_spec, v_spec], out_specs=[q_spec, row_spec],
        out_shape=[jax.ShapeDtypeStruct((t, ATT_Q_HEADS * HEAD_DIM), F32),
                   jax.ShapeDtypeStruct((ATT_Q_HEADS, t, 1), F32)],
        compiler_params=_params(("parallel", "parallel")),
    )(p, p, z)


def attn_bwd(p, z, o, lse, do, n_lat):
    t = p.shape[0]
    tq = ROW_TILE

    def body(q_ref, k_ref, v_ref, o_ref, do_ref, lse_ref, dq_ref, dk_ref, dv_ref):
        i = pl.program_id(1)

        @pl.when(i == 0)
        def _():
            dk_ref[...] = jnp.zeros_like(dk_ref)
            dv_ref[...] = jnp.zeros_like(dv_ref)

        mask = _att_mask(i, t, tq, n_lat)
        k, v = k_ref[...].astype(BF16), v_ref[...].astype(BF16)
        dk, dv = dk_ref[...], dv_ref[...]
        for g in range(ATT_GROUP):
            cols = slice(g * HEAD_DIM, (g + 1) * HEAD_DIM)
            q, do_g = q_ref[:, cols].astype(BF16), do_ref[:, cols]
            pr = jnp.exp(_dg(q, k, _NT) * ATT_SCALE + mask - lse_ref[g])
            delta = jnp.sum(o_ref[:, cols] * do_g, axis=1, keepdims=True)
            ds = pr * (_dg(do_g, v, _NT) - delta) * ATT_SCALE
            dq_ref[:, cols] = _dg(ds, k, _NN)
            dk = dk + _dg(ds, q, _TN)
            dv = dv + _dg(pr, do_g, _TN)
        dk_ref[...], dv_ref[...] = dk, dv

    q_spec, k_spec, v_spec, row_spec = _att_specs(t, tq)
    kv_out = pl.BlockSpec((t, HEAD_DIM), lambda kv, i: (0, kv))
    return pl.pallas_call(
        body, name="attn_bwd", grid=(ATT_KV_HEADS, t // tq),
        in_specs=[q_spec, k_spec, v_spec, q_spec, q_spec, row_spec],
        out_specs=[q_spec, kv_out, kv_out],
        out_shape=[jax.ShapeDtypeStruct((t, ATT_Q_HEADS * HEAD_DIM), F32),
                   jax.ShapeDtypeStruct((t, ATT_KV_HEADS * HEAD_DIM), F32),
                   jax.ShapeDtypeStruct((t, ATT_KV_HEADS * HEAD_DIM), F32)],
        compiler_params=_params(("parallel", "arbitrary")),
    )(p, p, z, o, do, lse)


_RQ_BLK = P_RQ // HEAD_DIM
_RK_BLK = P_RK // HEAD_DIM
_RV_BLK = Z_RV // HEAD_DIM


def _scan_chunk(direction, step, n_chunks, n_lat_chunks):
    return jnp.where(direction == 0, (step + n_lat_chunks) % n_chunks, n_chunks - 1 - step)


def _ret_geometry(direction):
    c = RET_CHUNK
    i = lax.broadcasted_iota(jnp.int32, (c, c), 0)
    j = lax.broadcasted_iota(jnp.int32, (c, c), 1)
    rel = jnp.where(direction == 0, i - j, j - i).astype(F32)
    r = lax.broadcasted_iota(jnp.int32, (c, 1), 0)
    pos = jnp.where(direction == 0, r, c - 1 - r).astype(F32)
    return rel, pos


def ret_chunk(q, k, v, s, lg, rel, pos):
    c = RET_CHUNK
    causal = rel >= 0
    d_in = jnp.where(causal, jnp.exp(lg * jnp.where(causal, rel, 0.0)), 0.0)
    q_dec = jnp.exp(lg * (pos + 1.0))
    k_dec = jnp.exp(lg * (c - 1.0 - pos))
    c_dec = jnp.exp(lg * c)
    att = bdot_nt(q, k) * d_in
    o = bdot(att, v) + bdot(q * q_dec, s)
    s_new = c_dec * s + bdot_tn(k * k_dec, v)
    return o, s_new


def ret_fwd(p, z, lg, n_lat):
    t = p.shape[0]
    c = RET_CHUNK
    nc, nlc = t // c, n_lat // c

    def body(q_ref, k_ref, v_ref, lg_ref, o_ref, ssave_ref, s_s):
        d, n = pl.program_id(0), pl.program_id(1)

        @pl.when(n == 0)
        def _():
            s_s[...] = jnp.zeros_like(s_s)

        rel, pos = _ret_geometry(d)
        for h in range(RET_HEADS):
            cols = slice(h * HEAD_DIM, (h + 1) * HEAD_DIM)
            ssave_ref[0, h, 0] = s_s[h]
            o, s_new = ret_chunk(q_ref[:, cols], k_ref[:, cols], v_ref[:, cols].astype(F32), s_s[h], lg_ref[0, h], rel, pos)
            o_ref[:, cols] = o
            s_s[h] = s_new

    w = RET_HEADS * HEAD_DIM

    def blk(base):
        return pl.BlockSpec((c, w), lambda d, n: (_scan_chunk(d, n, nc, nlc), base // RET_HEADS))

    return pl.pallas_call(
        body, name="ret_fwd", grid=(2, nc),
        in_specs=[blk(_RQ_BLK), blk(_RK_BLK), blk(_RV_BLK), pl.BlockSpec((1, RET_HEADS, 1, 1), lambda d, n: (d, 0, 0, 0))],
        out_specs=[pl.BlockSpec((c, w), lambda d, n: (_scan_chunk(d, n, nc, nlc), d)),
                   pl.BlockSpec((1, RET_HEADS, 1, HEAD_DIM, HEAD_DIM), lambda d, n: (d, 0, n, 0, 0))],
        out_shape=[jax.ShapeDtypeStruct((t, 2 * w), F32),
                   jax.ShapeDtypeStruct((2, RET_HEADS, nc, HEAD_DIM, HEAD_DIM), F32)],
        scratch_shapes=[pltpu.VMEM((RET_HEADS, HEAD_DIM, HEAD_DIM), F32)],
        compiler_params=_params(("parallel", "arbitrary")),
    )(p, p, z, lg)


def ret_bwd(p, z, lg, states, do, n_lat):
    t = p.shape[0]
    c = RET_CHUNK
    nc, nlc = t // c, n_lat // c

    def body(q_ref, k_ref, v_ref, lg_ref, s_ref, do_ref, dq_ref, dk_ref, dv_ref, dlg_ref, ds_s):
        d, n = pl.program_id(0), pl.program_id(1)

        @pl.when(n == 0)
        def _():
            ds_s[...] = jnp.zeros_like(ds_s)
            dlg_ref[...] = jnp.zeros_like(dlg_ref)

        rel, pos = _ret_geometry(d)
        f = functools.partial(ret_chunk, rel=rel, pos=pos)
        for h in range(RET_HEADS):
            cols = slice(h * HEAD_DIM, (h + 1) * HEAD_DIM)
            _, vjp = jax.vjp(f, q_ref[:, cols], k_ref[:, cols], v_ref[:, cols].astype(F32), s_ref[0, h, 0], lg_ref[0, h])
            dq, dk, dv, ds, dlg = vjp((do_ref[:, cols], ds_s[h]))
            dq_ref[:, cols], dk_ref[:, cols], dv_ref[:, cols] = dq, dk, dv
            ds_s[h] = ds
            dlg_ref[0, h] += dlg

    def chunk_of(d, n):
        return _scan_chunk(d, nc - 1 - n, nc, nlc)

    w = RET_HEADS * HEAD_DIM

    def blk(base):
        return pl.BlockSpec((c, w), lambda d, n: (chunk_of(d, n), base // RET_HEADS))

    out_blk = pl.BlockSpec((c, w), lambda d, n: (chunk_of(d, n), d))
    lg_blk = pl.BlockSpec((1, RET_HEADS, 1, 1), lambda d, n: (d, 0, 0, 0))
    grad_shape = jax.ShapeDtypeStruct((t, 2 * w), F32)
    return pl.pallas_call(
        body, name="ret_bwd", grid=(2, nc),
        in_specs=[blk(_RQ_BLK), blk(_RK_BLK), blk(_RV_BLK), lg_blk,
                  pl.BlockSpec((1, RET_HEADS, 1, HEAD_DIM, HEAD_DIM), lambda d, n: (d, 0, nc - 1 - n, 0, 0)),
                  pl.BlockSpec((c, w), lambda d, n: (chunk_of(d, n), 0))],
        out_specs=[out_blk, out_blk, out_blk, lg_blk],
        out_shape=[grad_shape, grad_shape, grad_shape, jax.ShapeDtypeStruct((2, RET_HEADS, 1, 1), F32)],
        scratch_shapes=[pltpu.VMEM((RET_HEADS, HEAD_DIM, HEAD_DIM), F32)],
        compiler_params=_params(("parallel", "arbitrary")),
    )(p, p, z, lg, states, do)


_GQ_BLK = P_GQ // (GLA_HEADS * GLA_DK)
_GK_BLK = Z_GK // (GLA_HEADS * GLA_DK)
_GV_BLK = Z_GV // (GLA_HEADS * GLA_DV)
_LA_BLK = P_LA // (GLA_HEADS * GLA_DK)


def _gla_mask(direction):
    c = GLA_CHUNK
    i = lax.broadcasted_iota(jnp.int32, (c, c), 0)
    j = lax.broadcasted_iota(jnp.int32, (c, c), 1)
    return (jnp.where(direction == 0, i - j, j - i) >= 0).astype(F32)


def gla_chunk(q, k, v, la, st, mask):
    b = mask_cumsum(mask, la)
    btot = jnp.sum(la, axis=0, keepdims=True)
    half = 0.5 * btot
    qt, kt = q * jnp.exp(b - half), k * jnp.exp(half - b)
    qs, ke = q * jnp.exp(b), k * jnp.exp(btot - b)
    outs, upd = [], []
    for h in range(GLA_HEADS):
        ks = slice(h * GLA_DK, (h + 1) * GLA_DK)
        vh = v[:, h * GLA_DV:(h + 1) * GLA_DV]
        att = bdot_nt(qt[:, ks], kt[:, ks]) * mask
        outs.append(bdot(att, vh) + bdot_nt(qs[:, ks], st[:, ks]))
        upd.append(bdot_tn(vh, ke[:, ks]))
    st_new = st * jnp.exp(btot) + jnp.concatenate(upd, axis=1)
    return jnp.concatenate(outs, axis=1), st_new


def gla_fwd(p, z, n_lat):
    t = p.shape[0]
    c = GLA_CHUNK
    nc, nlc = t // c, n_lat // c
    kw, vw = GLA_HEADS * GLA_DK, GLA_HEADS * GLA_DV

    def body(q_ref, k_ref, v_ref, la_ref, o_ref, ssave_ref, s_s):
        d, n = pl.program_id(0), pl.program_id(1)

        @pl.when(n == 0)
        def _():
            s_s[...] = jnp.zeros_like(s_s)

        ssave_ref[0, 0] = s_s[...]
        o, s_new = gla_chunk(q_ref[...], k_ref[...].astype(F32), v_ref[...].astype(F32), la_ref[...], s_s[...], _gla_mask(d))
        o_ref[...] = o
        s_s[...] = s_new

    def chunk_of(d, n):
        return _scan_chunk(d, n, nc, nlc)

    return pl.pallas_call(
        body, name="gla_fwd", grid=(2, nc),
        in_specs=[pl.BlockSpec((c, kw), lambda d, n: (chunk_of(d, n), _GQ_BLK)),
                  pl.BlockSpec((c, kw), lambda d, n: (chunk_of(d, n), _GK_BLK)),
                  pl.BlockSpec((c, vw), lambda d, n: (chunk_of(d, n), _GV_BLK)),
                  pl.BlockSpec((c, kw), lambda d, n: (chunk_of(d, n), _LA_BLK + d))],
        out_specs=[pl.BlockSpec((c, vw), lambda d, n: (chunk_of(d, n), d)),
                   pl.BlockSpec((1, 1, GLA_DV, kw), lambda d, n: (d, n, 0, 0))],
        out_shape=[jax.ShapeDtypeStruct((t, 2 * vw), F32), jax.ShapeDtypeStruct((2, nc, GLA_DV, kw), F32)],
        scratch_shapes=[pltpu.VMEM((GLA_DV, kw), F32)],
        compiler_params=_params(("parallel", "arbitrary")),
    )(p, z, z, p)


def gla_bwd(p, z, states, do, n_lat):
    t = p.shape[0]
    c = GLA_CHUNK
    nc, nlc = t // c, n_lat // c
    kw, vw = GLA_HEADS * GLA_DK, GLA_HEADS * GLA_DV

    def body(q_ref, k_ref, v_ref, la_ref, s_ref, do_ref, dq_ref, dk_ref, dv_ref, dla_ref, ds_s):
        d, n = pl.program_id(0), pl.program_id(1)

        @pl.when(n == 0)
        def _():
            ds_s[...] = jnp.zeros_like(ds_s)

        f = functools.partial(gla_chunk, mask=_gla_mask(d))
        _, vjp = jax.vjp(f, q_ref[...], k_ref[...].astype(F32), v_ref[...].astype(F32), la_ref[...], s_ref[0, 0])
        dq_ref[...], dk_ref[...], dv_ref[...], dla_ref[...], ds_s[...] = vjp((do_ref[...], ds_s[...]))

    def chunk_of(d, n):
        return _scan_chunk(d, nc - 1 - n, nc, nlc)

    k_out = pl.BlockSpec((c, kw), lambda d, n: (chunk_of(d, n), d))
    return pl.pallas_call(
        body, name="gla_bwd", grid=(2, nc),
        in_specs=[pl.BlockSpec((c, kw), lambda d, n: (chunk_of(d, n), _GQ_BLK)),
                  pl.BlockSpec((c, kw), lambda d, n: (chunk_of(d, n), _GK_BLK)),
                  pl.BlockSpec((c, vw), lambda d, n: (chunk_of(d, n), _GV_BLK)),
                  pl.BlockSpec((c, kw), lambda d, n: (chunk_of(d, n), _LA_BLK + d)),
                  pl.BlockSpec((1, 1, GLA_DV, kw), lambda d, n: (d, nc - 1 - n, 0, 0)),
                  pl.BlockSpec((c, vw), lambda d, n: (chunk_of(d, n), 0))],
        out_specs=[k_out, k_out, pl.BlockSpec((c, vw), lambda d, n: (chunk_of(d, n), d)), k_out],
        out_shape=[jax.ShapeDtypeStruct((t, 2 * kw), F32), jax.ShapeDtypeStruct((t, 2 * kw), F32),
                   jax.ShapeDtypeStruct((t, 2 * vw), F32), jax.ShapeDtypeStruct((t, 2 * kw), F32)],
        scratch_shapes=[pltpu.VMEM((GLA_DV, kw), F32)],
        compiler_params=_params(("parallel", "arbitrary")),
    )(p, z, z, p, states, do)


def _adam_tile(w, g, m, v):
    m = ADAM_B1 * m + (1.0 - ADAM_B1) * g
    v = ADAM_B2 * v + (1.0 - ADAM_B2) * (g * g)
    m_hat = m / (1.0 - ADAM_B1 ** ADAM_STEP)
    v_hat = v / (1.0 - ADAM_B2 ** ADAM_STEP)
    delta = -ADAM_LR * (m_hat / (jnp.sqrt(v_hat) + ADAM_EPS) + ADAM_WD * w)
    return delta, m, v


def adamw(name, w, g, m, v, after=None):
    shape = w.shape
    cols = shape[-1] if w.ndim > 1 and shape[-1] >= LANES else int(np.prod(shape))
    rows = int(np.prod(shape)) // cols
    tr = rows
    for cand in (512, 256, 128, 64, 32, 16, 8):
        if rows % cand == 0 and cand * cols * 4 <= (1 << 21):
            tr = cand
            break
    flat = [a.reshape(rows, cols) for a in (w, g, m, v)]

    behind = [] if after is None else [after]

    def body(w_ref, g_ref, m_ref, v_ref, *rest):
        d_ref, mo_ref, vo_ref = rest[len(behind):]
        d_ref[...], mo_ref[...], vo_ref[...] = _adam_tile(w_ref[...], g_ref[...], m_ref[...], v_ref[...])

    spec = pl.BlockSpec((tr, cols), lambda i: (i, 0))
    outs = pl.pallas_call(
        body, name=name, grid=(rows // tr,),
        in_specs=[spec] * 4 + [pl.BlockSpec(memory_space=pl.ANY)] * len(behind), out_specs=[spec] * 3,
        out_shape=[jax.ShapeDtypeStruct((rows, cols), F32)] * 3,
        compiler_params=_params(("parallel",)),
    )(*flat, *behind)
    return tuple(o.reshape(shape) for o in outs)


def adamw_layers(name, w, grads, m, v):
    depth, rows, cols = w.shape
    tr = _rows_tile(rows, cols)
    nb = rows // tr

    def body(w_ref, m_ref, v_ref, *rest):
        g_refs, (g_ref, d_ref, mo_ref, vo_ref) = rest[:depth], rest[depth:]
        l = pl.program_id(0)
        for k in range(depth):
            @pl.when(l == k)
            def _():
                g = g_refs[k][...]
                g_ref[...] = g
                d_ref[...], mo_ref[...], vo_ref[...] = _adam_tile(w_ref[...], g, m_ref[...], v_ref[...])

    def layer_grad(k):
        return pl.BlockSpec((tr, cols), lambda l, i: (jnp.where(l < k, 0, jnp.where(l == k, i, nb - 1)), 0))

    spec = pl.BlockSpec((tr, cols), lambda l, i: (l * nb + i, 0))
    flat = [a.reshape(depth * rows, cols) for a in (w, m, v)]
    g_all, delta, new_m, new_v = pl.pallas_call(
        body, name=name, grid=(depth, nb),
        in_specs=[spec] * 3 + [layer_grad(k) for k in range(depth)], out_specs=[spec] * 4,
        out_shape=[jax.ShapeDtypeStruct((depth * rows, cols), F32)] * 4,
        compiler_params=_params(("arbitrary", "arbitrary")),
    )(*flat, *grads)
    return tuple(a.reshape(w.shape) for a in (delta, new_m, new_v)), g_all.reshape(w.shape)


MESH = pl.DeviceIdType.MESH
_HBM = pl.BlockSpec(memory_space=pltpu.HBM)
N_CHIPS = 4
N_DEV = 8


def _place():
    x, y, c = lax.axis_index("x"), lax.axis_index("y"), lax.axis_index("c")
    chips = [(1 - x, y), (x, 1 - y), (1 - x, 1 - y)]
    return x, y, c, chips


def _remote(src, dst, send_sem, recv_sem, to):
    return pltpu.make_async_remote_copy(src_ref=src, dst_ref=dst, send_sem=send_sem, recv_sem=recv_sem,
                                        device_id=to, device_id_type=MESH)


def all_gather_small(name, v):
    m_per, n = v.shape

    def body(x_ref, out_ref, send_sems, recv_sems, local_sem):
        x, y, c, chips = _place()
        me, sibling = (x, y, c), (x, y, 1 - c)

        def rows(px, py, pc):
            return out_ref.at[pl.ds((4 * px + 2 * py + pc) * m_per, m_per), :]

        def copy(k, block, to, src=None):
            return _remote(rows(*block) if src is None else src, rows(*block), send_sems.at[k], recv_sems.at[k], to)

        mine = pltpu.make_async_copy(x_ref, rows(*me), local_sem)
        mine.start()
        first = [copy(0, me, sibling, src=x_ref)]
        first += [copy(1 + j, me, (*chip, c), src=x_ref) for j, chip in enumerate(chips)]
        for cp in first:
            cp.start()
        passed = [copy(4 + j, (*chip, c), sibling) for j, chip in enumerate(chips)]
        for j, chip in enumerate(chips):
            copy(1 + j, (*chip, c), me).wait_recv()
            passed[j].start()
        copy(0, sibling, me).wait_recv()
        for j, chip in enumerate(chips):
            copy(4 + j, (*chip, 1 - c), me).wait_recv()
        for cp in first + passed:
            cp.wait_send()
        mine.wait()

    return pl.pallas_call(
        body, name=name,
        out_shape=jax.ShapeDtypeStruct((N_DEV * m_per, n), v.dtype),
        in_specs=[pl.BlockSpec(memory_space=pltpu.VMEM)],
        out_specs=pl.BlockSpec(memory_space=pltpu.VMEM),
        scratch_shapes=[pltpu.SemaphoreType.DMA((7,)), pltpu.SemaphoreType.DMA((7,)), pltpu.SemaphoreType.DMA],
        compiler_params=pltpu.CompilerParams(vmem_limit_bytes=VMEM_LIMIT),
    )(v)


_SEM = pl.BlockSpec(memory_space=pltpu.SEMAPHORE)
_SPLIT_COPY = pltpu.CompilerParams(has_side_effects=pltpu.SideEffectType.DATAFLOW_SIDE_EFFECTING)


class CopyPlan(NamedTuple):
    copies: object
    n: int
    in_place: bool = False


def _gather_copies(x_ref, land_ref, x, y, c, chips):
    half = x_ref.shape[0] // 2
    rows = pl.ds(c * half, half)
    return [(x_ref.at[rows, :], land_ref.at[2 * x + y, rows, :], (*chip, c), land_ref.at[2 * chip[0] + chip[1], rows, :])
            for chip in chips]


def _pass_copies(land_ref, _, x, y, c, chips):
    half = land_ref.shape[1] // 2
    mine, other = pl.ds(c * half, half), pl.ds((1 - c) * half, half)
    return [(land_ref.at[2 * chip[0] + chip[1], mine, :], land_ref.at[2 * chip[0] + chip[1], mine, :], (x, y, 1 - c),
             land_ref.at[2 * chip[0] + chip[1], other, :]) for chip in chips]


def _sibling_half_copies(p_ref, land_ref, x, y, c, chips):
    half = p_ref.shape[1] // 2
    return [(p_ref.at[:, pl.ds((1 - c) * half, half), :], land_ref, (x, y, 1 - c), land_ref)]


def _scatter_copies(s_ref, land_ref, x, y, c, chips):
    return [(s_ref.at[2 * chip[0] + chip[1]], land_ref.at[j], (*chip, c), land_ref.at[j]) for j, chip in enumerate(chips)]


def _join_copies(buf_ref, _, x, y, c, chips):
    half = buf_ref.shape[0] // 2
    mine = buf_ref.at[pl.ds(c * half, half), :]
    return [(mine, mine, (x, y, 1 - c), buf_ref.at[pl.ds((1 - c) * half, half), :])]


GATHER = CopyPlan(_gather_copies, 3)
PASS_ON = CopyPlan(_pass_copies, 3, in_place=True)
SIBLING_HALF = CopyPlan(_sibling_half_copies, 1)
SCATTER = CopyPlan(_scatter_copies, 3)
JOIN = CopyPlan(_join_copies, 1, in_place=True)


def split_start(name, plan, srcs, land_shapes=None, after=None):
    nt = len(srcs)
    arrays = [pltpu.with_memory_space_constraint(s, pltpu.HBM) for s in srcs]
    if not plan.in_place:
        arrays += [pltpu.with_memory_space_constraint(lax.empty(shape, s.dtype), pltpu.HBM) for shape, s in zip(land_shapes, srcs)]
    na = len(arrays)
    behind = [] if after is None else [after]
    n_in = na + len(behind)

    def body(*refs):
        x_refs = refs[:nt]
        land_refs = x_refs if plan.in_place else refs[nt:na]
        send, recv = refs[n_in:n_in + nt], refs[n_in + nt:n_in + 2 * nt]
        x, y, c, chips = _place()
        for t in range(nt):
            for j, (src, dst, to, _) in enumerate(plan.copies(x_refs[t], land_refs[t], x, y, c, chips)):
                _remote(src, dst, send[t].at[j], recv[t].at[j], to).start()
        refs[-1][...] = jnp.zeros_like(refs[-1])

    outs = pl.pallas_call(
        body, name=name,
        out_shape=tuple([pltpu.SemaphoreType.DMA((plan.n,))] * (2 * nt) + [pltpu.HBM(a.shape, a.dtype) for a in arrays]
                        + [jax.ShapeDtypeStruct((8, LANES), F32)]),
        in_specs=[_HBM] * na + [pl.BlockSpec(memory_space=pl.ANY)] * len(behind),
        out_specs=tuple([_SEM] * (2 * nt) + [_HBM] * na + [pl.BlockSpec(memory_space=pltpu.VMEM)]),
        input_output_aliases={i: 2 * nt + i for i in range(na)},
        compiler_params=_SPLIT_COPY,
    )(*arrays, *behind)
    groups = [(outs[t], outs[nt + t]) + tuple(outs[2 * nt + t + k * nt] for k in range(na // nt)) for t in range(nt)]
    return groups, outs[-1]


def split_wait(name, plan, group, after):
    send, recv, *arrays = group
    na = len(arrays)

    def body(*refs):
        x_ref, land_ref = refs[0], refs[na - 1]
        send_sem, recv_sem = refs[na], refs[na + 1]
        x, y, c, chips = _place()
        for j, (s, _, to, arrival) in enumerate(plan.copies(x_ref, land_ref, x, y, c, chips)):
            cp = _remote(s, arrival, send_sem.at[j], recv_sem.at[j], to)
            cp.wait_send()
            cp.wait_recv()

    return pl.pallas_call(
        body, name=name,
        out_shape=tuple(pltpu.HBM(a.shape, a.dtype) for a in arrays),
        in_specs=tuple([_HBM] * na + [_SEM, _SEM, pl.BlockSpec(memory_space=pl.ANY)]), out_specs=tuple([_HBM] * na),
        input_output_aliases={i: i for i in range(na)}, compiler_params=_SPLIT_COPY,
    )(*arrays, send, recv, after)


def _rows_tile(rows, cols):
    for cand in (512, 256, 128, 64, 32, 16):
        if rows % cand == 0 and cand * cols * 4 <= (1 << 21):
            return cand
    return rows


def add_sibling_half(name, pieces, from_sibling, core):
    n, h, cols = from_sibling.shape
    tr = _rows_tile(h, cols // 4)
    nb = h // tr

    def body(c_ref, a_ref, b_ref, o_ref):
        o_ref[...] = (a_ref[...].astype(F32) + b_ref[...].astype(F32)).astype(o_ref.dtype)

    blk = pl.BlockSpec((1, tr, cols), lambda q, i, c_ref: (q, i, 0))
    return pl.pallas_call(
        body, name=name,
        grid_spec=pltpu.PrefetchScalarGridSpec(
            num_scalar_prefetch=1, grid=(n, nb),
            in_specs=[pl.BlockSpec((1, tr, cols), lambda q, i, c_ref: (q, c_ref[0] * nb + i, 0)), blk], out_specs=blk),
        out_shape=jax.ShapeDtypeStruct((n, h, cols), BF16),
        compiler_params=_params(("parallel", "parallel")),
    )(core.reshape(1).astype(jnp.int32), pieces, from_sibling)


def add_chip_sums(name, chip_sums, from_chips, chip, core):
    _, h, cols = chip_sums.shape
    tr = _rows_tile(h, cols // 2)
    nb = h // tr

    def body(s_ref, own_ref, r0_ref, r1_ref, r2_ref, o_ref):
        acc = own_ref[0].astype(F32) + r0_ref[0].astype(F32)
        o_ref[...] = acc + r1_ref[0].astype(F32) + r2_ref[0].astype(F32)

    def got(j):
        return pl.BlockSpec((1, tr, cols), lambda i, s_ref: (j, i, 0))

    return pl.pallas_call(
        body, name=name,
        grid_spec=pltpu.PrefetchScalarGridSpec(
            num_scalar_prefetch=1, grid=(nb,),
            in_specs=[pl.BlockSpec((1, tr, cols), lambda i, s_ref: (s_ref[0], i, 0)), got(0), got(1), got(2)],
            out_specs=pl.BlockSpec((tr, cols), lambda i, s_ref: (s_ref[1] * nb + i, 0))),
        out_shape=jax.ShapeDtypeStruct((2 * h, cols), F32),
        compiler_params=_params(("parallel",)),
    )(jnp.stack([chip, core]).astype(jnp.int32), chip_sums, from_chips, from_chips, from_chips)


def sum_device_blocks(name, g):
    n = g.shape[1]

    def body(g_ref, o_ref):
        acc = g_ref[0:8, :]
        for d in range(1, N_DEV):
            acc = acc + g_ref[8 * d:8 * (d + 1), :]
        o_ref[...] = acc

    return pl.pallas_call(body, name=name, out_shape=jax.ShapeDtypeStruct((8, n), F32),
                          compiler_params=pltpu.CompilerParams(vmem_limit_bytes=VMEM_LIMIT))(g)


class LayerWeights(NamedTuple):
    norm1_g: jax.Array
    q_g: jax.Array
    k_g: jax.Array
    lg: jax.Array
    ret_g: jax.Array
    gate_up: jax.Array
    gate_b: jax.Array
    gla_g: jax.Array
    norm2_g: jax.Array
    conv_w: jax.Array
    conv_b: jax.Array


def _mod(mods, k):
    return mods[:, k:k + 1, :]


def out_view(l, tb):
    rows = D_MODEL // N_CHIPS
    if tb:
        return BView(n=D_MODEL, k=D_MODEL, tn=rows, tk=D_MODEL, index_map=lambda i, j, kk: (j, l, kk))
    chips = tuple(functools.partial(lambda i, j, kk, q: (q, l, j), q=q) for q in range(N_CHIPS))
    return BView(n=D_MODEL, k=D_MODEL, tn=1024, tk=rows, index_map=None, part_maps=chips)


def down_view(l, f, tb):
    rows = f // N_CHIPS
    if tb:
        return BView(n=f, k=D_MODEL, tn=rows, tk=D_MODEL, index_map=lambda i, j, kk: (j, l, kk))
    chips = tuple(functools.partial(lambda i, j, kk, q: (q, l, j), q=q) for q in range(N_CHIPS))
    return BView(n=D_MODEL, k=f, tn=512, tk=rows, index_map=None, part_maps=chips)


def up_view(l, f, part=None):
    cols = 2 * f // N_CHIPS
    tc = _pick(cols, (1408, 1024, 512, 256))
    nbc = cols // tc
    if part is None:
        return BView(n=2 * f, k=D_MODEL, tn=tc, tk=D_MODEL, index_map=lambda i, j, kk: (j // nbc, l, j % nbc))
    nnb = D_MODEL // 512
    tiles = tuple(functools.partial(lambda i, j, kk, p: (2 * part + p // nbc, l * nnb + j, p % nbc), p=p) for p in range(2 * nbc))
    return BView(n=D_MODEL, k=f, tn=512, tk=tc, index_map=None, part_maps=tiles)


def up_grad_view(f, part, into):
    cols = f // 2
    tn = _pick(cols, (1408, 1024, 512, 256))
    nbc = cols // tn
    return OView((N_CHIPS, D_MODEL, cols), lambda i, j, kk: (2 * part + j // nbc, i, j % nbc), tn, into)


def ada_view(l, n_ada, tb):
    if tb:
        return BView(n=D_MODEL, k=n_ada, tn=1024, tk=n_ada, index_map=lambda i, j, kk: (l, j, 0))
    return BView(n=n_ada, k=D_MODEL, tn=1024, tk=D_MODEL, index_map=lambda i, j, kk: (l, 0, j))


def _prep_args(z, zg, cos, sin, w):
    rows = [Row(z, Z_AV, 0), Row(z, 512, Z_RQ // 512), Row(z, 512, Z_RK // 512), Row(z, 256, Z_GQ // 256),
            Row(zg, LANES, 0), Row(cos, HEAD_DIM, 0, False), Row(sin, HEAD_DIM, 0, False)]
    return rows, [Par(w.q_g), Par(w.k_g), Par(w.gate_up), Par(w.gate_b)]


def _post_args(o_att, o_ret, o_gla, z, w):
    rows = [Row(o_att, 1024), Row(o_ret, 512, 0), Row(o_ret, 512, 1, False), Row(o_gla, 512, 0), Row(o_gla, 512, 1, False),
            Row(z, 512, Z_RG // 512), Row(z, 512, Z_GR // 512)]
    return rows, [Par(w.ret_g), Par(w.gla_g)]


def layer_fwd(l, xs, mods, w, fetch, cos, sin, n_lat, n_out):
    t, d = xs.shape
    tag = f"l{l}_"
    nm1 = [Par(w.norm1_g), Par(_mod(mods, 0), True), Par(_mod(mods, 1), True)]
    (h,) = row_map(tag + "norm1", normmod_tile, [Row(xs, d)], nm1, [(d, BF16)], t, n_lat)
    (w_main, w_gate), started = fetch("w_in", h)
    z = matmul(tag + "in_proj", h, w_main, after=started, out_dtype=BF16)
    zg = matmul(tag + "gate_proj", h, w_gate)
    rows, pars = _prep_args(z, zg, cos, sin, w)
    (p,) = row_map(tag + "prep", prep_tile, rows, pars, [(P_W, F32)], t, n_lat)
    o_att, lse = attn_fwd(p, z, n_lat)
    o_ret, s_ret = ret_fwd(p, z, w.lg, n_lat)
    o_gla, s_gla = gla_fwd(p, z, n_lat)
    rows, pars = _post_args(o_att, o_ret, o_gla, z, w)
    (m,) = row_map(tag + "post", post_tile, rows, pars, [(d, BF16)], t, n_lat)
    m = m[:n_out]
    g_out, started = fetch("w_out", m)
    y = matmul(tag + "out_proj", m, g_out, view=out_view(0, False), after=started)
    rn = [Par(_mod(mods, 2), True), Par(w.norm2_g), Par(_mod(mods, 3), True), Par(_mod(mods, 4), True)]
    x1, h2 = row_map(tag + "resid1_norm2", resid_norm_tile, [Row(xs, d), Row(y, d)], rn, [(d, F32), (d, BF16)], n_out, n_lat)
    f = w.conv_b.shape[1]
    g_up, started = fetch("w_up", h2)
    u = matmul(tag + "up_proj", h2, g_up, view=up_view(0, f), after=started, out_dtype=BF16)
    g = convglu(tag + "convglu", u, w.conv_w, w.conv_b, n_lat)
    g_down, started = fetch("w_down", g)
    yd = matmul(tag + "down_proj", g, g_down, view=down_view(0, f, False), after=started)
    (x2,) = row_map(tag + "resid2", resid_tile, [Row(x1, d), Row(yd, d)], [Par(_mod(mods, 5), True)], [(d, F32)], n_out, n_lat)
    saved = dict(xs=xs, h=h, z=z, zg=zg, p=p, o_att=o_att, lse=lse, o_ret=o_ret, s_ret=s_ret, o_gla=o_gla, s_gla=s_gla,
                 m=m, y=y, x1=x1, h2=h2, u=u, g=g, yd=yd, w_main=w_main, w_gate=w_gate, g_out=g_out, g_up=g_up, g_down=g_down)
    return x2, saved


def _sum_dirs(a):
    w = a.shape[1] // 2
    return a[:, :w] + a[:, w:]


def layer_bwd(l, dx2, s, mods, w, cos, sin, n_lat, grad_ready):
    (t, d), n_out = s["xs"].shape, dx2.shape[0]
    tag = f"l{l}_b_"

    def all_rows(a):
        return a if n_out == t else jnp.pad(a, ((0, t - n_out), (0, 0)))

    dyd, dgate5 = row_vjp(tag + "resid2", gated_tile, [Row(s["yd"], d)], [Par(_mod(mods, 5), True)], [dx2], n_out, n_lat,
                          row_grad_dtype=BF16)
    f = w.conv_b.shape[1]
    dg = matmul(tag + "down_dx", dyd, s["g_down"], tb=True, view=down_view(0, f, True))
    dw_down = matmul(tag + "down_dw", s["g"], dyd, ta=True, out_dtype=BF16)
    da, dv, dcw, dcb = convglu_bwd(tag + "convglu", s["u"], w.conv_w, w.conv_b, dg, n_lat)
    dh2 = matmul(tag + "up_dx_gate", da, s["g_up"], tb=True, view=up_view(0, f, 0))
    dh2 = matmul(tag + "up_dx_value", dv, s["g_up"], tb=True, view=up_view(0, f, 1), add=dh2)
    dw_up = matmul(tag + "up_dw_gate", s["h2"], da, ta=True, out_dtype=BF16, o_view=up_grad_view(f, 0, None))
    dw_up = matmul(tag + "up_dw_value", s["h2"], dv, ta=True, out_dtype=BF16, o_view=up_grad_view(f, 1, dw_up))
    started = grad_ready("ffn", dict(w_up=dw_up, w_down=dw_down))
    rn = [Par(_mod(mods, 2), True), Par(w.norm2_g), Par(_mod(mods, 3), True), Par(_mod(mods, 4), True)]
    dx1, dy, dgate2, dg2, dshift3, dscale4 = row_vjp(
        tag + "resid1_norm2", resid_norm_tile, [Row(s["xs"], d), Row(s["y"], d)], rn, [dx2, dh2], n_out, n_lat,
        row_grad_dtype=(F32, BF16), after=started)
    if n_out < t:
        dgate5, dshift3, dscale4, dgate2 = [g.at[1].set(0.0) for g in (dgate5, dshift3, dscale4, dgate2)]
    dm = matmul(tag + "out_dx", dy, s["g_out"], tb=True, view=out_view(0, True))
    dw_out = matmul(tag + "out_dw", s["m"], dy, ta=True, out_dtype=BF16)
    rows, pars = _post_args(s["o_att"], s["o_ret"], s["o_gla"], s["z"], w)
    started = grad_ready("w_out", dict(w_out=dw_out))
    do_att, do_ret, do_gla, d_rg, d_gr, d_ret_g, d_gla_g = row_vjp(tag + "post", post_tile, rows, pars, [dm], n_out, n_lat, after=started)
    do_att, do_ret, do_gla, d_rg, d_gr, dx1 = [all_rows(a) for a in (do_att, do_ret, do_gla, d_rg, d_gr, dx1)]
    dq_a, dk_a, dv_a = attn_bwd(s["p"], s["z"], s["o_att"], s["lse"], do_att, n_lat)
    dq_r, dk_r, dv_r, dlg = ret_bwd(s["p"], s["z"], w.lg, s["s_ret"], do_ret, n_lat)
    dq_g, dk_g, dv_g, dla = gla_bwd(s["p"], s["z"], s["s_gla"], do_gla, n_lat)
    dp = jnp.concatenate([dq_a, dk_a, _sum_dirs(dq_g), _sum_dirs(dq_r), _sum_dirs(dk_r), dla], axis=1)
    rows, pars = _prep_args(s["z"], s["zg"], cos, sin, w)
    d_zqk, d_zrq, d_zrk, d_zgq, dzg, d_qg, d_kg, d_up, d_gb = row_vjp(tag + "prep", prep_tile, rows, pars, [dp], t, n_lat)
    dz = jnp.concatenate([d_zqk, dv_a, d_zrq, d_zrk, _sum_dirs(dv_r), d_rg, d_zgq, _sum_dirs(dk_g), _sum_dirs(dv_g), d_gr], axis=1)
    dz, dzg = dz.astype(BF16), dzg.astype(BF16)
    dh_gate = matmul(tag + "gate_dx", dzg, s["w_gate"], tb=True)
    dh = matmul(tag + "in_dx", dz, s["w_main"], tb=True, add=dh_gate)
    dw_main = matmul(tag + "in_dw", s["h"], dz, ta=True, out_dtype=BF16)
    dw_gate = matmul(tag + "gate_dw", s["h"], dzg, ta=True, out_dtype=BF16)
    started = grad_ready("w_in", dict(w_main=dw_main, w_gate=dw_gate))
    nm1 = [Par(w.norm1_g), Par(_mod(mods, 0), True), Par(_mod(mods, 1), True)]
    dx, dg1, dshift0, dscale1 = row_vjp(tag + "norm1", normmod_tile, [Row(s["xs"], d)], nm1, [dh], t, n_lat,
                                        add_to_first=dx1, after=started)
    dmods = jnp.concatenate([dshift0, dscale1, dgate2, dshift3, dscale4, dgate5], axis=1)
    grads = dict(w_main=dw_main, w_gate=dw_gate, w_out=dw_out, w_up=dw_up, w_down=dw_down, norm1_g=dg1, q_g=d_qg, k_g=d_kg,
                 lg=dlg, ret_g=d_ret_g, gate_up=d_up, gate_b=d_gb, gla_g=d_gla_g, norm2_g=dg2, conv_w=dcw, conv_b=dcb)
    return dx, dmods, grads


def rope_tables(n_lat, n_ctx):
    rows = n_lat // GRID_W
    row = jnp.repeat(jnp.arange(rows, dtype=F32), GRID_W)
    col = jnp.tile(jnp.arange(GRID_W, dtype=F32), rows)
    n_freq = HEAD_DIM // 4
    inv_freq = ROPE_THETA ** (-jnp.arange(n_freq, dtype=F32) / n_freq)
    ang = jnp.concatenate([row[:, None] * inv_freq, col[:, None] * inv_freq], axis=-1)
    cos, sin = jnp.cos(ang), jnp.sin(ang)
    cos = jnp.concatenate([jnp.concatenate([cos, cos], axis=1), jnp.ones((n_ctx, HEAD_DIM), F32)], axis=0)
    sin = jnp.concatenate([jnp.concatenate([-sin, sin], axis=1), jnp.zeros((n_ctx, HEAD_DIM), F32)], axis=0)
    return cos, sin


def local_step(xs, target, mods, weights, fetch, final_g, n_lat, grad_ready):
    t, d = xs.shape
    cos, sin = rope_tables(n_lat, t - n_lat)
    saved = []
    h = xs
    for l, w in enumerate(weights):
        n_out = t if l + 1 < len(weights) else n_lat
        h, s = layer_fwd(l, h, mods[l], w, functools.partial(fetch, l), cos, sin, n_lat, n_out)
        saved.append(s)
    loss, dx, dgf = final_loss(h, target, final_g, n_lat)
    dmods, grads = [None] * len(weights), [None] * len(weights)
    for l in reversed(range(len(weights))):
        dx, dmods[l], grads[l] = layer_bwd(l, dx, saved[l], mods[l], weights[l], cos, sin, n_lat, functools.partial(grad_ready, l))
    return loss, dx, dmods, grads, dgf


WEIGHT_NAMES = ("c_ctx", "ada_w", "ada_b", "norm1_g", "w_in", "q_norm_g", "k_norm_g", "ret_log_decay", "ret_norm_g",
                "gla_gate_up", "gla_gate_b", "gla_norm_g", "w_out", "norm2_g", "w_up", "conv_w", "conv_b", "w_down", "final_norm_g")
PACK_QUANTUM = 8 * LANES


def _pack(arrays):
    flat = jnp.concatenate([a.reshape(-1).astype(F32) for a in arrays])
    n = -(-flat.shape[0] // PACK_QUANTUM) * PACK_QUANTUM
    return jnp.pad(flat, (0, n - flat.shape[0])).reshape(8, n // 8)


def _unpack(flat2d, shapes):
    out, at = [], 0
    for s in shapes:
        size = int(np.prod(s))
        out.append(flat2d[:, at:at + size].reshape((flat2d.shape[0],) + tuple(s)))
        at += size
    return out


def _per_device(gathered):
    return gathered.reshape(N_DEV, -1)


def _from_chips(per_device, axis):
    chips = per_device[0::2]
    moved = jnp.moveaxis(chips, 0, axis)
    shape = moved.shape
    return moved.reshape(shape[:axis] + (shape[axis] * shape[axis + 1],) + shape[axis + 2:])


def kernel(x, c, ctx, c_ctx, ada_w, ada_b, norm1_g, w_in, q_norm_g, k_norm_g, ret_log_decay, ret_norm_g, gla_gate_up, gla_gate_b, gla_norm_g, w_out, norm2_g, w_up, conv_w, conv_b, w_down, final_norm_g, loss_target, m_c_ctx, m_ada_w, m_ada_b, m_norm1_g, m_w_in, m_q_norm_g, m_k_norm_g, m_ret_log_decay, m_ret_norm_g, m_gla_gate_up, m_gla_gate_b, m_gla_norm_g, m_w_out, m_norm2_g, m_w_up, m_conv_w, m_conv_b, m_w_down, m_final_norm_g, v_c_ctx, v_ada_w, v_ada_b, v_norm1_g, v_w_in, v_q_norm_g, v_k_norm_g, v_ret_log_decay, v_ret_norm_g, v_gla_gate_up, v_gla_gate_b, v_gla_norm_g, v_w_out, v_norm2_g, v_w_up, v_conv_w, v_conv_b, v_w_down, v_final_norm_g):
    weights = dict(zip(WEIGHT_NAMES, (c_ctx, ada_w, ada_b, norm1_g, w_in, q_norm_g, k_norm_g, ret_log_decay, ret_norm_g,
                                      gla_gate_up, gla_gate_b, gla_norm_g, w_out, norm2_g, w_up, conv_w, conv_b, w_down, final_norm_g)))
    mom_m = dict(zip(WEIGHT_NAMES, (m_c_ctx, m_ada_w, m_ada_b, m_norm1_g, m_w_in, m_q_norm_g, m_k_norm_g, m_ret_log_decay, m_ret_norm_g,
                                    m_gla_gate_up, m_gla_gate_b, m_gla_norm_g, m_w_out, m_norm2_g, m_w_up, m_conv_w, m_conv_b, m_w_down, m_final_norm_g)))
    mom_v = dict(zip(WEIGHT_NAMES, (v_c_ctx, v_ada_w, v_ada_b, v_norm1_g, v_w_in, v_q_norm_g, v_k_norm_g, v_ret_log_decay, v_ret_norm_g,
                                    v_gla_gate_up, v_gla_gate_b, v_gla_norm_g, v_w_out, v_norm2_g, v_w_up, v_conv_w, v_conv_b, v_w_down, v_final_norm_g)))
    depth, d = norm1_g.shape
    assert d == D_MODEL and x.shape[0] == 1
    n_lat, n_ctx, f = x.shape[1], ctx.shape[1], conv_b.shape[1]
    assert n_lat % ROW_TILE == 0 and n_ctx % ROW_TILE == 0 and f % FFN_COL_TILE == 0 and f % N_CHIPS == 0
    n_in = w_in.shape[2]
    n_ada = ada_w.shape[2]
    xi, yi, ci = lax.axis_index("x"), lax.axis_index("y"), lax.axis_index("c")
    chip = 2 * xi + yi
    dev = 2 * chip + ci

    big = ("w_in", "w_out", "w_up", "w_down")
    order = [(l, name) for l in range(depth) for name in big]
    passing = {}

    def pass_on(k, after):
        tag = "{1}{0}".format(*order[k])
        own, land = split_wait("gather_wait_" + tag, GATHER, in_flight[k], after)
        (moving,), started = split_start("gather_pass_" + tag, PASS_ON, [land])
        passing[k] = (own, moving)
        return started

    def fetch(l, name, after):
        k = order.index((l, name))
        if k == 0:
            pass_on(0, after)
        own, moving = passing.pop(k)
        (land,) = split_wait(f"gather_pass_wait_{name}{l}", PASS_ON, moving, after)
        started = pass_on(k + 1, after) if k + 1 < len(order) else None
        land = lax.dynamic_update_slice_in_dim(land, own[None], chip, axis=0)
        if name != "w_in":
            return land, started
        last = N_MAIN - (N_CHIPS - 1) * n_in
        w_main = jnp.concatenate([land[q] for q in range(N_CHIPS - 1)] + [land[N_CHIPS - 1][:, :last]], axis=1)
        return (w_main, jnp.pad(land[N_CHIPS - 1][:, last:], ((0, 0), (0, LANES - N_GATE)))), started

    small_shapes = [c.shape[1:], conv_w.shape, gla_gate_up.shape, gla_gate_b.shape]
    got = _per_device(all_gather_small("gather_small", _pack([c, conv_w, gla_gate_up, gla_gate_b])))
    c_all, conv_w_sh, gate_up_sh, gate_b_sh = _unpack(got, small_shapes)
    conv_w_full = _from_chips(conv_w_sh, 2)
    gate_up_full = _from_chips(gate_up_sh, 3)
    gate_b_full = _from_chips(gate_b_sh, 2)

    act = jnp.zeros((16, d), F32).at[0:N_DEV].set(jax.nn.silu(c_all)).at[N_DEV].set(jax.nn.silu(c_ctx))
    mod_sh = jnp.stack([matmul(f"ada_fwd{l}", act, ada_w, view=ada_view(l, n_ada, False)) for l in range(depth)])
    got = _per_device(all_gather_small("gather_mods", _pack([mod_sh])))
    (mod_sh_all,) = _unpack(got, [mod_sh.shape])
    mod_full = _from_chips(mod_sh_all, 2) + ada_b[:, None, :]
    mod_mine = lax.dynamic_index_in_dim(mod_full, dev, axis=1, keepdims=False)
    mods = [jnp.stack([mod_mine[l].reshape(N_MOD, d), mod_full[l, N_DEV].reshape(N_MOD, d)]) for l in range(depth)]
    in_flight, token = [], None
    for l in range(depth):
        behind = 0.0 if token is None else token[0, 0]
        shards = [(weights[name][l] + behind).astype(BF16) for name in big]
        started, token = split_start(f"gather_start{l}", GATHER, shards, [(N_CHIPS,) + s.shape for s in shards],
                                     after=mod_full if l == 0 else None)
        in_flight += started

    layer_w = []
    for l in range(depth):
        up = jnp.zeros((2, LANES, GLA_HEADS * GLA_DK), F32)
        up = up.at[0, 0:GLA_RANK].set(gate_up_full[l, 0]).at[1, GLA_RANK:2 * GLA_RANK].set(gate_up_full[l, 1])
        layer_w.append(LayerWeights(
            norm1_g=norm1_g[l].reshape(1, 1, d), q_g=q_norm_g[l].reshape(1, 1, HEAD_DIM), k_g=k_norm_g[l].reshape(1, 1, HEAD_DIM),
            lg=ret_log_decay[l].reshape(2, RET_HEADS, 1, 1), ret_g=ret_norm_g[l].reshape(1, 1, HEAD_DIM),
            gate_up=up.reshape(1, 2 * LANES, -1), gate_b=gate_b_full[l].reshape(1, 2, -1), gla_g=gla_norm_g[l].reshape(1, 1, HEAD_DIM),
            norm2_g=norm2_g[l].reshape(1, 1, d), conv_w=conv_w_full[l], conv_b=conv_b[l].reshape(1, f)))

    def pieces_of(name, g):
        if name == "w_in":
            shards = [g["w_main"][:, q * n_in:(q + 1) * n_in] for q in range(N_CHIPS - 1)]
            tail = jnp.concatenate([g["w_main"][:, (N_CHIPS - 1) * n_in:], g["w_gate"][:, :N_GATE]], axis=1)
            return jnp.stack(shards + [tail])
        if name == "w_up":
            return g["w_up"]
        return g[name].reshape(N_CHIPS, -1, d)

    groups = {"ffn": ("w_up", "w_down"), "w_out": ("w_out",), "w_in": ("w_in",)}
    reducing = {}
    to_sibling = []

    def sibling_arrived(after):
        started = None
        while to_sibling:
            l, group, in_flight_halves = to_sibling.pop(0)
            sums = []
            for name, halves in zip(groups[group], in_flight_halves):
                pieces, from_sibling = split_wait(f"rs_sibling_wait_{name}{l}", SIBLING_HALF, halves, after)
                sums.append(add_sibling_half(f"rs_add_sibling_{name}{l}", pieces, from_sibling, ci))
            in_flight_sums, token = split_start(f"rs_start_{group}{l}", SCATTER, sums, [(3,) + s.shape[1:] for s in sums])
            reducing.update({(l, name): grp for name, grp in zip(groups[group], in_flight_sums)})
            started = token if started is None else started + token
        return started

    def grad_ready(l, group, g):
        pieces = [pieces_of(name, g) for name in groups[group]]
        before = None if (l, group) == (0, "w_in") else sibling_arrived(pieces[0])
        in_flight_halves, started = split_start(f"rs_sibling_{group}{l}", SIBLING_HALF, pieces,
                                                [(N_CHIPS, pc.shape[1] // 2, pc.shape[2]) for pc in pieces])
        to_sibling.append((l, group, in_flight_halves))
        return started if before is None else started + before

    xs = jnp.concatenate([x[0], ctx[0]], axis=0) + token[0, 0]
    loss, dx, dmods, grads, dgf = local_step(xs, loss_target[0], mods, layer_w, fetch, final_norm_g.reshape(1, d), n_lat, grad_ready)

    def gate_up_grad(g):
        return jnp.stack([g[0, 0:GLA_RANK], g[0, LANES + GLA_RANK:LANES + 2 * GLA_RANK]])

    per_layer = [[dmods[l][0], dmods[l][1], grads[l]["norm1_g"], grads[l]["norm2_g"], grads[l]["q_g"], grads[l]["k_g"],
                  grads[l]["ret_g"], grads[l]["gla_g"], grads[l]["lg"], gate_up_grad(grads[l]["gate_up"]), grads[l]["gate_b"],
                  grads[l]["conv_w"], grads[l]["conv_b"]] for l in range(depth)]
    layer_shapes = [(N_MOD * d,), (N_MOD * d,), (d,), (d,), (HEAD_DIM,), (HEAD_DIM,), (HEAD_DIM,), (HEAD_DIM,), (2, RET_HEADS),
                    (2, GLA_RANK, GLA_HEADS * GLA_DK), (2, GLA_HEADS * GLA_DK), (3, f), (f,)]
    packed = _pack([a for lay in per_layer for a in lay] + [dgf, loss[0, 0:1]])
    gathered = all_gather_small("gather_small_grads", packed)
    every = _unpack(_per_device(gathered), layer_shapes * depth + [(d,), (1,)])
    total = _unpack(sum_device_blocks("sum_small_grads", gathered).reshape(1, -1), layer_shapes * depth + [(d,), (1,)])
    nl = len(layer_shapes)

    def tot(l, k):
        return total[l * nl + k][0]

    out = {"norm1_g": jnp.stack([tot(l, 2) for l in range(depth)]), "norm2_g": jnp.stack([tot(l, 3) for l in range(depth)]),
           "q_norm_g": jnp.stack([tot(l, 4) for l in range(depth)]), "k_norm_g": jnp.stack([tot(l, 5) for l in range(depth)]),
           "ret_norm_g": jnp.stack([tot(l, 6) for l in range(depth)]), "gla_norm_g": jnp.stack([tot(l, 7) for l in range(depth)]),
           "ret_log_decay": jnp.stack([tot(l, 8) for l in range(depth)]),
           "gla_gate_up": lax.dynamic_slice_in_dim(jnp.stack([tot(l, 9) for l in range(depth)]), chip * gla_gate_up.shape[3], gla_gate_up.shape[3], axis=3),
           "gla_gate_b": lax.dynamic_slice_in_dim(jnp.stack([tot(l, 10) for l in range(depth)]), chip * gla_gate_b.shape[2], gla_gate_b.shape[2], axis=2),
           "conv_w": lax.dynamic_slice_in_dim(jnp.stack([tot(l, 11) for l in range(depth)]), chip * conv_w.shape[2], conv_w.shape[2], axis=2),
           "conv_b": jnp.stack([tot(l, 12) for l in range(depth)]),
           "final_norm_g": total[depth * nl][0],
           "ada_b": jnp.stack([tot(l, 0) + tot(l, 1) for l in range(depth)])}
    loss_total = total[depth * nl + 1][0, 0]

    dmod_all = jnp.zeros((depth, 16, N_MOD * d), F32)
    for l in range(depth):
        dmod_all = dmod_all.at[l, 0:N_DEV].set(every[l * nl][:, :]).at[l, N_DEV].set(tot(l, 1))
    dmod_cols = lax.dynamic_slice_in_dim(dmod_all, chip * n_ada, n_ada, axis=2)
    for l in range(depth):
        slab = OView((depth, d, n_ada), functools.partial(lambda i, j, kk, l: (l, i, j), l=l), None, out.get("ada_w"))
        out["ada_w"] = matmul(f"ada_dw{l}", act, dmod_cols[l], ta=True, o_view=slab)
    dact = matmul("ada_dx0", dmod_cols[0], ada_w, tb=True, view=ada_view(0, n_ada, True))
    for l in range(1, depth):
        dact = matmul(f"ada_dx{l}", dmod_cols[l], ada_w, tb=True, view=ada_view(l, n_ada, True), add=dact)
    got = _per_device(all_gather_small("gather_dcctx", _pack([dact[N_DEV]])))
    last_started = sibling_arrived(got)
    got = got[0::2, :d]
    dsilu = got[0] + got[1] + got[2] + got[3]
    sig = jax.nn.sigmoid(c_ctx)
    out["c_ctx"] = dsilu * (sig + c_ctx * sig * (1.0 - sig))

    deltas, new_m, new_v = {}, {}, {}

    def update(name):
        out[name] = out[name].reshape(weights[name].shape)
        deltas[name], new_m[name], new_v[name] = adamw("adamw_" + name, weights[name], out[name], mom_m[name], mom_v[name],
                                                       after=last_started)

    for name in WEIGHT_NAMES:
        if name not in big:
            update(name)
    behind = new_v["ada_w"]
    joining = []

    def joined(after):
        name, in_flight_halves = joining.pop()
        per_layer = [split_wait(f"rs_join_wait_{name}{l}", JOIN, grp, after)[0] for l, grp in enumerate(in_flight_halves)]
        (deltas[name], new_m[name], new_v[name]), out[name] = adamw_layers(
            "adamw_" + name, weights[name], per_layer, mom_m[name], mom_v[name])
        return new_v[name]

    for name in ("w_down", "w_up", "w_out", "w_in"):
        halves = []
        for l in range(depth):
            sums, got = split_wait(f"rs_wait_{name}{l}", SCATTER, reducing[(l, name)], behind)
            halves.append(add_chip_sums(f"rs_add_chips_{name}{l}", sums, got, chip, ci))
        in_flight_halves, _ = split_start("rs_join_" + name, JOIN, halves)
        if joining:
            behind = joined(behind)
        joining.append((name, in_flight_halves))
    joined(behind)
    grad_x = dx[:n_lat].reshape(x.shape)
    return (loss_total, grad_x, *[out[n] for n in WEIGHT_NAMES], *[deltas[n] for n in WEIGHT_NAMES],
            *[new_m[n] for n in WEIGHT_NAMES], *[new_v[n] for n in WEIGHT_NAMES])
```

```python
import functools
from typing import NamedTuple

import numpy as np
import jax
import jax.numpy as jnp
from jax import lax
from jax.experimental import pallas as pl
from jax.experimental.pallas import tpu as pltpu

F32 = jnp.float32
BF16 = jnp.bfloat16

D_MODEL = 2048
HEAD_DIM = 128
ATT_Q_HEADS = 8
ATT_KV_HEADS = 2
ATT_GROUP = ATT_Q_HEADS // ATT_KV_HEADS
RET_HEADS = 4
GLA_HEADS = 4
GLA_DK = 64
GLA_DV = 128
GLA_RANK = 16
GLA_TAU = 16.0
RET_CHUNK = 256
GLA_CHUNK = 128
GRID_W = 64
ROPE_THETA = 10000.0
N_MOD = 6
EPS = 1e-6
N_MAIN = 5120
N_GATE = 2 * GLA_RANK
LANES = 128
ROW_TILE = 256
FFN_COL_TILE = 256
VMEM_LIMIT = 56 * 1024 * 1024

ADAM_LR = 0.001
ADAM_B1 = 0.9
ADAM_B2 = 0.999
ADAM_EPS = 1e-08
ADAM_WD = 0.01
ADAM_STEP = 10

Z_AQ, Z_AK, Z_AV = 0, 1024, 1280
Z_RQ, Z_RK, Z_RV, Z_RG = 1536, 2048, 2560, 3072
Z_GQ, Z_GK, Z_GV, Z_GR = 3584, 3840, 4096, 4608
P_AQ, P_AK, P_GQ, P_RQ, P_RK, P_LA = 0, 1024, 1280, 1536, 2048, 2560
P_W = 3072


def _params(sem=None):
    return pltpu.CompilerParams(dimension_semantics=sem, vmem_limit_bytes=VMEM_LIMIT)


def _pick(n, cands):
    for c in cands:
        if n % c == 0:
            return c
    return n


_NN = (((1,), (0,)), ((), ()))
_NT = (((1,), (1,)), ((), ()))
_TN = (((0,), (0,)), ((), ()))


def _dg(a, b, dims):
    return lax.dot_general(a.astype(BF16), b.astype(BF16), dims, preferred_element_type=F32)


@jax.custom_vjp
def bdot(a, b):
    return _dg(a, b, _NN)


def _bdot_fwd(a, b):
    return _dg(a, b, _NN), (a, b)


def _bdot_bwd(res, ct):
    a, b = res
    return _dg(ct, b, _NT), _dg(a, ct, _TN)


bdot.defvjp(_bdot_fwd, _bdot_bwd)


@jax.custom_vjp
def bdot_nt(a, b):
    return _dg(a, b, _NT)


def _bdot_nt_fwd(a, b):
    return _dg(a, b, _NT), (a, b)


def _bdot_nt_bwd(res, ct):
    a, b = res
    return _dg(ct, b, _NN), _dg(ct, a, _TN)


bdot_nt.defvjp(_bdot_nt_fwd, _bdot_nt_bwd)


@jax.custom_vjp
def bdot_tn(a, b):
    return _dg(a, b, _TN)


def _bdot_tn_fwd(a, b):
    return _dg(a, b, _TN), (a, b)


def _bdot_tn_bwd(res, ct):
    a, b = res
    return _dg(b, ct, _NT), _dg(a, ct, _NN)


bdot_tn.defvjp(_bdot_tn_fwd, _bdot_tn_bwd)


def _split3(x):
    x1 = x.astype(BF16)
    r1 = x - x1.astype(F32)
    x2 = r1.astype(BF16)
    x3 = (r1 - x2.astype(F32)).astype(BF16)
    return x1, x2, x3


def _mask_dot(mask_bf16, x, dims):
    x1, x2, x3 = _split3(x)
    f = lambda t: lax.dot_general(mask_bf16, t, dims, preferred_element_type=F32)
    return f(x1) + f(x2) + f(x3)


@jax.custom_vjp
def mask_cumsum(mask, x):
    return _mask_dot(mask.astype(BF16), x, _NN)


def _mask_cumsum_fwd(mask, x):
    return mask_cumsum(mask, x), mask


def _mask_cumsum_bwd(mask, ct):
    return jnp.zeros_like(mask), _mask_dot(mask.astype(BF16), ct, _TN)


mask_cumsum.defvjp(_mask_cumsum_fwd, _mask_cumsum_bwd)


def _roll(x, shift, axis):
    return pltpu.roll(x, shift % x.shape[axis], axis)


@functools.partial(jax.custom_vjp, nondiff_argnums=(1, 2))
def roll(x, shift, axis):
    return _roll(x, shift, axis)


def _roll_fwd(x, shift, axis):
    return _roll(x, shift, axis), None


def _roll_bwd(shift, axis, _, ct):
    return (_roll(ct, -shift, axis),)


roll.defvjp(_roll_fwd, _roll_bwd)


def rms(x):
    return x * lax.rsqrt(jnp.mean(x * x, axis=-1, keepdims=True) + EPS)


def silu(x):
    return x * (0.5 + 0.5 * jnp.tanh(0.5 * x))


def log_sigmoid(x):
    return jnp.minimum(x, 0.0) - jnp.log(1.0 + jnp.exp(-jnp.abs(x)))


def rope(t, cos, sin):
    return t * cos + roll(t, HEAD_DIM // 2, 1) * sin


def _heads(x, n, width=HEAD_DIM):
    return [x[:, h * width:(h + 1) * width] for h in range(n)]


class Row(NamedTuple):
    arr: jax.Array
    width: int
    idx: int = 0
    diff: bool = True


class Par(NamedTuple):
    arr: jax.Array
    grouped: bool = False
    diff: bool = True


def _row_specs(rows, pars, tm, n_lat_tiles):
    def grp(i):
        return jnp.minimum(i // n_lat_tiles, 1)

    specs = [pl.BlockSpec((tm, r.width), functools.partial(lambda i, k: (i, k), k=r.idx)) for r in rows]
    for p in pars:
        blk = (1,) + p.arr.shape[1:]
        if p.grouped:
            specs.append(pl.BlockSpec(blk, lambda i: (grp(i), 0, 0)))
        else:
            specs.append(pl.BlockSpec(blk, lambda i: (0, 0, 0)))
    return specs


def row_map(name, fn, rows, pars, outs, n_rows, n_lat):
    tm = ROW_TILE
    nr, npar = len(rows), len(pars)

    def body(*refs):
        vals = [r[...].astype(F32) for r in refs[:nr]] + [p[0] for p in refs[nr:nr + npar]]
        res = fn(*vals)
        for o, v in zip(refs[nr + npar:], res):
            o[...] = v.astype(o.dtype)

    return pl.pallas_call(
        body, name=name, grid=(n_rows // tm,),
        in_specs=_row_specs(rows, pars, tm, n_lat // tm),
        out_specs=[pl.BlockSpec((tm, w), lambda i: (i, 0)) for w, _ in outs],
        out_shape=[jax.ShapeDtypeStruct((n_rows, w), dt) for w, dt in outs],
        compiler_params=_params(("arbitrary",)),
    )(*[r.arr for r in rows], *[p.arr for p in pars])


def row_vjp(name, fn, rows, pars, cts, n_rows, n_lat, add_to_first=None, row_grad_dtype=F32, after=None):
    tm = ROW_TILE
    nr, npar, nc = len(rows), len(pars), len(cts)
    n_lat_tiles = n_lat // tm
    args = list(rows) + list(pars)
    diff_pos = [k for k, a in enumerate(args) if a.diff]
    n_add = 0 if add_to_first is None else 1
    n_after = 0 if after is None else 1

    def body(*refs):
        i = pl.program_id(0)
        vals = [r[...].astype(F32) for r in refs[:nr]] + [p[0] for p in refs[nr:nr + npar]]
        ct_vals = tuple(c[...] for c in refs[nr + npar:nr + npar + nc])
        out_refs = refs[nr + npar + nc + n_add + n_after:]

        def g(*dv):
            full = list(vals)
            for k, v in zip(diff_pos, dv):
                full[k] = v
            return tuple(fn(*full))

        _, vjp = jax.vjp(g, *[vals[k] for k in diff_pos])
        grads = vjp(ct_vals)
        for n, (k, o, gr) in enumerate(zip(diff_pos, out_refs, grads)):
            if k < nr:
                o[...] = (gr + refs[nr + npar + nc][...] if (n == 0 and n_add) else gr).astype(o.dtype)
            else:
                first = (i == 0) | (i == n_lat_tiles) if args[k].grouped else (i == 0)

                @pl.when(first)
                def _():
                    o[0] = gr

                @pl.when(jnp.logical_not(first))
                def _():
                    o[0] += gr

    def grp(i):
        return jnp.minimum(i // n_lat_tiles, 1)

    out_specs, out_shape = [], []
    for k in diff_pos:
        a = args[k]
        if k < nr:
            out_specs.append(pl.BlockSpec((tm, a.width), lambda i: (i, 0)))
            dtype = row_grad_dtype[len(out_shape)] if isinstance(row_grad_dtype, tuple) else row_grad_dtype
            out_shape.append(jax.ShapeDtypeStruct((n_rows, a.width), dtype))
        else:
            blk = (1,) + a.arr.shape[1:]
            out_specs.append(pl.BlockSpec(blk, (lambda i: (grp(i), 0, 0)) if a.grouped else (lambda i: (0, 0, 0))))
            out_shape.append(jax.ShapeDtypeStruct(a.arr.shape, F32))
    extra = list(cts) + ([add_to_first] if n_add else [])
    ct_specs = [pl.BlockSpec((tm, c.shape[1]), lambda i: (i, 0)) for c in extra]
    if n_after:
        extra.append(after)
        ct_specs.append(pl.BlockSpec(memory_space=pl.ANY))
    return pl.pallas_call(
        body, name=name, grid=(n_rows // tm,),
        in_specs=_row_specs(rows, pars, tm, n_lat_tiles) + ct_specs,
        out_specs=out_specs, out_shape=out_shape,
        compiler_params=_params(("arbitrary",)),
    )(*[r.arr for r in rows], *[p.arr for p in pars], *extra)


class BView(NamedTuple):
    n: int
    k: int
    tn: int
    tk: int
    index_map: object
    lead: int = 1
    part_maps: tuple = ()


MATMUL_VMEM_BUDGET = 40 * 1024 * 1024


def _matmul_tiles(m, n, k, a_bytes, b_bytes, o_bytes):
    tms = [c for c in (1152, 1024, 768, 512, 256, 128) if m % c == 0] or [m]
    tns = [c for c in (2048, 1408, 1280, 1024, 768, 512, 256, 128) if n % c == 0] or [n]
    tks = [k] + [c for c in (2816, 2304, 2048, 1408, 1024, 512, 256, 128) if k % c == 0 and c < k]
    for tk in tks:
        fits = [(tm * tn, tm, tn) for tm in tms for tn in tns
                if 2 * (tm * tk * a_bytes + tk * tn * b_bytes + tm * tn * o_bytes) + 2 * tm * tn * 4 <= MATMUL_VMEM_BUDGET]
        if fits and (max(fits)[0] >= min(512 * 512, tms[0] * tns[0]) or tk == tks[-1]):
            _, tm, tn = max(fits)
            return tm, tn, tk
    raise ValueError(f"no matmul tiling for {(m, n, k)}")


class OView(NamedTuple):
    shape: tuple
    index_map: object
    tn: int = None
    into: object = None


def matmul(name, a, b, *, ta=False, tb=False, add=None, out_dtype=F32, view=None, o_view=None, after=None):
    m = a.shape[1] if ta else a.shape[0]
    o_bytes = jnp.dtype(out_dtype).itemsize * (1 if add is None else 2)
    if view is None:
        k = a.shape[0] if ta else a.shape[1]
        n = b.shape[0] if tb else b.shape[1]
        assert (b.shape[1] if tb else b.shape[0]) == k, (a.shape, b.shape, ta, tb)
        if o_view is not None and o_view.tn is not None:
            tn = o_view.tn
            tm, _, tk = _matmul_tiles(m, tn, k, a.dtype.itemsize, b.dtype.itemsize, o_bytes)
        else:
            tm, tn, tk = _matmul_tiles(m, n, k, a.dtype.itemsize, b.dtype.itemsize, o_bytes)
    else:
        n, k, tn, tk = view.n, view.k, view.tn, view.tk
        b_maps = view.part_maps or (view.index_map,)
        tm, _, whole = _matmul_tiles(m, tn, tk * len(b_maps), a.dtype.itemsize, b.dtype.itemsize, o_bytes)
        assert whole == tk * len(b_maps) and not (ta and len(b_maps) > 1), (name, tm, whole)
    parts = 1 if view is None else len(b_maps)
    k_step = tk * parts
    nk = k // k_step
    dims = (((0 if ta else 1,), (1 if tb else 0,)), ((), ()))

    def body(a_ref, *rest):
        b_refs, rest = rest[:parts], rest[parts:]
        if parts == 1:
            prod = lax.dot_general(a_ref[...].astype(BF16), b_refs[0][...].astype(BF16), dims, preferred_element_type=F32)
        else:
            prod = sum(lax.dot_general(a_ref[:, p * tk:(p + 1) * tk].astype(BF16), b_refs[p][...].astype(BF16), dims,
                                       preferred_element_type=F32) for p in range(parts))
        if nk == 1:
            o_ref = rest[-1]
            o_ref[...] = (prod if add is None else prod + rest[0][...]).astype(o_ref.dtype)
            return
        o_ref, acc = rest[-2:]
        kk = pl.program_id(2)

        @pl.when(kk == 0)
        def _():
            acc[...] = prod

        @pl.when(kk != 0)
        def _():
            acc[...] += prod

        @pl.when(kk == nk - 1)
        def _():
            r = acc[...]
            if add is not None:
                r = r + rest[0][...]
            o_ref[...] = r.astype(o_ref.dtype)

    if ta:
        a_spec = pl.BlockSpec((k_step, tm), lambda i, j, kk: (kk, i))
    else:
        a_spec = pl.BlockSpec((tm, k_step), lambda i, j, kk: (i, kk))
    b_tile = (tn, tk) if tb else (tk, tn)
    if view is not None:
        b_specs = [pl.BlockSpec((None,) * view.lead + b_tile, index_map) for index_map in b_maps]
    elif tb:
        b_specs = [pl.BlockSpec(b_tile, lambda i, j, kk: (j, kk))]
    else:
        b_specs = [pl.BlockSpec(b_tile, lambda i, j, kk: (kk, j))]
    o_spec = pl.BlockSpec((tm, tn), lambda i, j, kk: (i, j))
    ins = [a] + [b] * parts + ([add] if add is not None else [])
    in_specs = [a_spec] + b_specs + ([o_spec] if add is not None else [])
    out_shape, aliases = jax.ShapeDtypeStruct((m, n), out_dtype), {}
    if o_view is not None:
        assert add is None
        o_spec = pl.BlockSpec((None, tm, tn), o_view.index_map)
        out_shape = jax.ShapeDtypeStruct(o_view.shape, out_dtype)
        if o_view.into is not None:
            aliases = {len(ins): 0}
            ins.append(o_view.into)
            in_specs.append(pl.BlockSpec(memory_space=pl.ANY))
    if after is not None:
        ins.append(after)
        in_specs.append(pl.BlockSpec(memory_space=pl.ANY))
    return pl.pallas_call(
        body, name=name, grid=(m // tm, n // tn, nk),
        in_specs=in_specs, out_specs=o_spec, out_shape=out_shape, input_output_aliases=aliases,
        scratch_shapes=[pltpu.VMEM((tm, tn), F32)] if nk > 1 else [],
        compiler_params=_params(("parallel", "parallel", "arbitrary")),
    )(*ins)


def normmod_tile(x, g, shift, scale):
    return (rms(x) * g * (1.0 + scale) + shift,)


def resid_tile(x, y, gate):
    return (x + gate * y,)


def resid_norm_tile(x, y, gate, g, shift, scale):
    x1 = x + gate * y
    return x1, rms(x1) * g * (1.0 + scale) + shift


def gated_tile(y, gate):
    return (gate * y,)


def prep_tile(z_qk, z_rq, z_rk, z_gq, zg, cos, sin, qg, kg, gate_up, gate_b):
    out = []
    for h, t in enumerate(_heads(z_qk, ATT_Q_HEADS + ATT_KV_HEADS)):
        out.append(rope(rms(t) * (qg if h < ATT_Q_HEADS else kg), cos, sin))
    gq = z_gq * (GLA_DK ** -0.5)
    rq = [rope(t, cos, sin) for t in _heads(z_rq, RET_HEADS)]
    rk = [rope(t * (HEAD_DIM ** -0.5), cos, sin) for t in _heads(z_rk, RET_HEADS)]
    la = [log_sigmoid(bdot(zg, gate_up[d * LANES:(d + 1) * LANES]) + gate_b[d:d + 1]) * (1.0 / GLA_TAU) for d in range(2)]
    return (jnp.concatenate(out + [gq] + rq + rk + la, axis=1),)


def post_tile(o_att, o_ret_f, o_ret_b, o_gla_f, o_gla_b, rg, gr, ret_g, gla_g):
    ret = jnp.concatenate([rms(t) * ret_g for t in _heads(o_ret_f + o_ret_b, RET_HEADS)], axis=1) * silu(rg)
    gla = jnp.concatenate([rms(t) * gla_g for t in _heads(o_gla_f + o_gla_b, GLA_HEADS)], axis=1) * silu(gr)
    return (jnp.concatenate([o_att, ret, gla], axis=1),)


def _convglu_tile(n_lat, a, v, cw, cb):
    t = a.shape[0]
    row = lax.broadcasted_iota(jnp.int32, (t, 1), 0)
    has_prev = ((row != 0) & (row != n_lat)).astype(F32)
    has_next = ((row != n_lat - 1) & (row != t - 1)).astype(F32)
    conv = roll(a, 1, 0) * has_prev * cw[0:1] + a * cw[1:2] + roll(a, -1, 0) * has_next * cw[2:3] + cb
    return silu(conv) * v


def convglu(name, u, cw, cb, n_lat):
    t, f2 = u.shape
    f, tc = f2 // 2, FFN_COL_TILE
    nb = f // tc

    def body(a_ref, v_ref, cw_ref, cb_ref, o_ref):
        o_ref[...] = _convglu_tile(n_lat, a_ref[...].astype(F32), v_ref[...].astype(F32), cw_ref[...], cb_ref[...]).astype(o_ref.dtype)

    return pl.pallas_call(
        body, name=name, grid=(nb,),
        in_specs=[pl.BlockSpec((t, tc), lambda j: (0, j)), pl.BlockSpec((t, tc), lambda j: (0, nb + j)),
                  pl.BlockSpec((3, tc), lambda j: (0, j)), pl.BlockSpec((1, tc), lambda j: (0, j))],
        out_specs=pl.BlockSpec((t, tc), lambda j: (0, j)),
        out_shape=jax.ShapeDtypeStruct((t, f), BF16),
        compiler_params=_params(("parallel",)),
    )(u, u, cw, cb)


def convglu_bwd(name, u, cw, cb, dg, n_lat):
    t, f2 = u.shape
    f, tc = f2 // 2, FFN_COL_TILE
    nb = f // tc

    def body(a_ref, v_ref, cw_ref, cb_ref, dg_ref, da_ref, dv_ref, dcw_ref, dcb_ref):
        _, vjp = jax.vjp(functools.partial(_convglu_tile, n_lat), a_ref[...].astype(F32), v_ref[...].astype(F32),
                         cw_ref[...], cb_ref[...])
        da, dv, dcw_ref[...], dcb_ref[...] = vjp(dg_ref[...])
        da_ref[...], dv_ref[...] = da.astype(BF16), dv.astype(BF16)

    col = pl.BlockSpec((t, tc), lambda j: (0, j))
    return pl.pallas_call(
        body, name=name, grid=(nb,),
        in_specs=[col, pl.BlockSpec((t, tc), lambda j: (0, nb + j)), pl.BlockSpec((3, tc), lambda j: (0, j)),
                  pl.BlockSpec((1, tc), lambda j: (0, j)), col],
        out_specs=[col, col, pl.BlockSpec((3, tc), lambda j: (0, j)), pl.BlockSpec((1, tc), lambda j: (0, j))],
        out_shape=[jax.ShapeDtypeStruct((t, f), BF16), jax.ShapeDtypeStruct((t, f), BF16),
                   jax.ShapeDtypeStruct((3, f), F32), jax.ShapeDtypeStruct((1, f), F32)],
        compiler_params=_params(("parallel",)),
    )(u, u, cw, cb, dg)


def final_loss(x, target, g, n_lat):
    tm = ROW_TILE
    d = x.shape[1]

    def body(x_ref, t_ref, g_ref, loss_ref, dx_ref, dg_ref):
        i = pl.program_id(0)
        tgt = t_ref[...]

        def f(xv, gv):
            e = rms(xv) * gv - tgt
            s = jnp.sum(jnp.sum(e * e, axis=1, keepdims=True), axis=0, keepdims=True)
            return s * (0.5 / d)

        val, vjp = jax.vjp(f, x_ref[...], g_ref[...])
        dx, dgv = vjp(jnp.ones((1, 1), F32))
        dx_ref[...] = dx

        @pl.when(i == 0)
        def _():
            dg_ref[...] = dgv
            loss_ref[...] = jnp.broadcast_to(val, loss_ref.shape)

        @pl.when(i != 0)
        def _():
            dg_ref[...] += dgv
            loss_ref[...] += jnp.broadcast_to(val, loss_ref.shape)

    return pl.pallas_call(
        body, name="final_loss", grid=(n_lat // tm,),
        in_specs=[pl.BlockSpec((tm, d), lambda i: (i, 0)), pl.BlockSpec((tm, d), lambda i: (i, 0)),
                  pl.BlockSpec((1, d), lambda i: (0, 0))],
        out_specs=[pl.BlockSpec((1, LANES), lambda i: (0, 0)), pl.BlockSpec((tm, d), lambda i: (i, 0)),
                   pl.BlockSpec((1, d), lambda i: (0, 0))],
        out_shape=[jax.ShapeDtypeStruct((1, LANES), F32), jax.ShapeDtypeStruct((n_lat, d), F32),
                   jax.ShapeDtypeStruct((1, d), F32)],
        compiler_params=_params(("arbitrary",)),
    )(x, target, g)


ATT_SCALE = HEAD_DIM ** -0.5
_AK_BLK = P_AK // HEAD_DIM
_AV_BLK = Z_AV // HEAD_DIM


def _att_specs(t, tq):
    gw = ATT_GROUP * HEAD_DIM
    q_spec = pl.BlockSpec((tq, gw), lambda kv, i: (i, kv))
    k_spec = pl.BlockSpec((t, HEAD_DIM), lambda kv, i: (0, _AK_BLK + kv))
    v_spec = pl.BlockSpec((t, HEAD_DIM), lambda kv, i: (0, _AV_BLK + kv))
    row_spec = pl.BlockSpec((ATT_GROUP, tq, 1), lambda kv, i: (kv, i, 0))
    return q_spec, k_spec, v_spec, row_spec


def _att_mask(i, t, tq, n_lat):
    col = lax.broadcasted_iota(jnp.int32, (1, t), 1)
    return jnp.where((i >= n_lat // tq) & (col < n_lat), -jnp.inf, 0.0).astype(F32)


def attn_fwd(p, z, n_lat):
    t = p.shape[0]
    tq = ROW_TILE

    def body(q_ref, k_ref, v_ref, o_ref, lse_ref):
        mask = _att_mask(pl.program_id(1), t, tq, n_lat)
        k, v = k_ref[...].astype(BF16), v_ref[...].astype(BF16)
        for g in range(ATT_GROUP):
            cols = slice(g * HEAD_DIM, (g + 1) * HEAD_DIM)
            s = _dg(q_ref[:, cols], k, _NT) * ATT_SCALE + mask
            m = jnp.max(s, axis=1, keepdims=True)
            pr = jnp.exp(s - m)
            l = jnp.sum(pr, axis=1, keepdims=True)
            o_ref[:, cols] = _dg(pr, v, _NN) / l
            lse_ref[g] = m + jnp.log(l)

    q_spec, k_spec, v_spec, row_spec = _att_specs(t, tq)
    return pl.pallas_call(
        body, name="attn_fwd", grid=(ATT_KV_HEADS, t // tq),
        in_specs=[q_spec, k_spec, v_spec], out_specs=[q_spec, row_spec],
        out_shape=[jax.ShapeDtypeStruct((t, ATT_Q_HEADS * HEAD_DIM), F32),
                   jax.ShapeDtypeStruct((ATT_Q_HEADS, t, 1), F32)],
        compiler_params=_params(("parallel", "parallel")),
    )(p, p, z)


def attn_bwd(p, z, o, lse, do, n_lat):
    t = p.shape[0]
    tq = ROW_TILE

    def body(q_ref, k_ref, v_ref, o_ref, do_ref, lse_ref, dq_ref, dk_ref, dv_ref):
        i = pl.program_id(1)

        @pl.when(i == 0)
        def _():
            dk_ref[...] = jnp.zeros_like(dk_ref)
            dv_ref[...] = jnp.zeros_like(dv_ref)

        mask = _att_mask(i, t, tq, n_lat)
        k, v = k_ref[...].astype(BF16), v_ref[...].astype(BF16)
        dk, dv = dk_ref[...], dv_ref[...]
        for g in range(ATT_GROUP):
            cols = slice(g * HEAD_DIM, (g + 1) * HEAD_DIM)
            q, do_g = q_ref[:, cols].astype(BF16), do_ref[:, cols]
            pr = jnp.exp(_dg(q, k, _NT) * ATT_SCALE + mask - lse_ref[g])
            delta = jnp.sum(o_ref[:, cols] * do_g, axis=1, keepdims=True)
            ds = pr * (_dg(do_g, v, _NT) - delta) * ATT_SCALE
            dq_ref[:, cols] = _dg(ds, k, _NN)
            dk = dk + _dg(ds, q, _TN)
            dv = dv + _dg(pr, do_g, _TN)
        dk_ref[...], dv_ref[...] = dk, dv

    q_spec, k_spec, v_spec, row_spec = _att_specs(t, tq)
    kv_out = pl.BlockSpec((t, HEAD_DIM), lambda kv, i: (0, kv))
    return pl.pallas_call(
        body, name="attn_bwd", grid=(ATT_KV_HEADS, t // tq),
        in_specs=[q_spec, k_spec, v_spec, q_spec, q_spec, row_spec],
        out_specs=[q_spec, kv_out, kv_out],
        out_shape=[jax.ShapeDtypeStruct((t, ATT_Q_HEADS * HEAD_DIM), F32),
                   jax.ShapeDtypeStruct((t, ATT_KV_HEADS * HEAD_DIM), F32),
                   jax.ShapeDtypeStruct((t, ATT_KV_HEADS * HEAD_DIM), F32)],
        compiler_params=_params(("parallel", "arbitrary")),
    )(p, p, z, o, do, lse)


_RQ_BLK = P_RQ // HEAD_DIM
_RK_BLK = P_RK // HEAD_DIM
_RV_BLK = Z_RV // HEAD_DIM


def _scan_chunk(direction, step, n_chunks, n_lat_chunks):
    return jnp.where(direction == 0, (step + n_lat_chunks) % n_chunks, n_chunks - 1 - step)


def _ret_geometry(direction):
    c = RET_CHUNK
    i = lax.broadcasted_iota(jnp.int32, (c, c), 0)
    j = lax.broadcasted_iota(jnp.int32, (c, c), 1)
    rel = jnp.where(direction == 0, i - j, j - i).astype(F32)
    r = lax.broadcasted_iota(jnp.int32, (c, 1), 0)
    pos = jnp.where(direction == 0, r, c - 1 - r).astype(F32)
    return rel, pos


def ret_chunk(q, k, v, s, lg, rel, pos):
    c = RET_CHUNK
    causal = rel >= 0
    d_in = jnp.where(causal, jnp.exp(lg * jnp.where(causal, rel, 0.0)), 0.0)
    q_dec = jnp.exp(lg * (pos + 1.0))
    k_dec = jnp.exp(lg * (c - 1.0 - pos))
    c_dec = jnp.exp(lg * c)
    att = bdot_nt(q, k) * d_in
    o = bdot(att, v) + bdot(q * q_dec, s)
    s_new = c_dec * s + bdot_tn(k * k_dec, v)
    return o, s_new


def ret_fwd(p, z, lg, n_lat):
    t = p.shape[0]
    c = RET_CHUNK
    nc, nlc = t // c, n_lat // c

    def body(q_ref, k_ref, v_ref, lg_ref, o_ref, ssave_ref, s_s):
        d, n = pl.program_id(0), pl.program_id(1)

        @pl.when(n == 0)
        def _():
            s_s[...] = jnp.zeros_like(s_s)

        rel, pos = _ret_geometry(d)
        for h in range(RET_HEADS):
            cols = slice(h * HEAD_DIM, (h + 1) * HEAD_DIM)
            ssave_ref[0, h, 0] = s_s[h]
            o, s_new = ret_chunk(q_ref[:, cols], k_ref[:, cols], v_ref[:, cols].astype(F32), s_s[h], lg_ref[0, h], rel, pos)
            o_ref[:, cols] = o
            s_s[h] = s_new

    w = RET_HEADS * HEAD_DIM

    def blk(base):
        return pl.BlockSpec((c, w), lambda d, n: (_scan_chunk(d, n, nc, nlc), base // RET_HEADS))

    return pl.pallas_call(
        body, name="ret_fwd", grid=(2, nc),
        in_specs=[blk(_RQ_BLK), blk(_RK_BLK), blk(_RV_BLK), pl.BlockSpec((1, RET_HEADS, 1, 1), lambda d, n: (d, 0, 0, 0))],
        out_specs=[pl.BlockSpec((c, w), lambda d, n: (_scan_chunk(d, n, nc, nlc), d)),
                   pl.BlockSpec((1, RET_HEADS, 1, HEAD_DIM, HEAD_DIM), lambda d, n: (d, 0, n, 0, 0))],
        out_shape=[jax.ShapeDtypeStruct((t, 2 * w), F32),
                   jax.ShapeDtypeStruct((2, RET_HEADS, nc, HEAD_DIM, HEAD_DIM), F32)],
        scratch_shapes=[pltpu.VMEM((RET_HEADS, HEAD_DIM, HEAD_DIM), F32)],
        compiler_params=_params(("parallel", "arbitrary")),
    )(p, p, z, lg)


def ret_bwd(p, z, lg, states, do, n_lat):
    t = p.shape[0]
    c = RET_CHUNK
    nc, nlc = t // c, n_lat // c

    def body(q_ref, k_ref, v_ref, lg_ref, s_ref, do_ref, dq_ref, dk_ref, dv_ref, dlg_ref, ds_s):
        d, n = pl.program_id(0), pl.program_id(1)

        @pl.when(n == 0)
        def _():
            ds_s[...] = jnp.zeros_like(ds_s)
            dlg_ref[...] = jnp.zeros_like(dlg_ref)

        rel, pos = _ret_geometry(d)
        f = functools.partial(ret_chunk, rel=rel, pos=pos)
        for h in range(RET_HEADS):
            cols = slice(h * HEAD_DIM, (h + 1) * HEAD_DIM)
            _, vjp = jax.vjp(f, q_ref[:, cols], k_ref[:, cols], v_ref[:, cols].astype(F32), s_ref[0, h, 0], lg_ref[0, h])
            dq, dk, dv, ds, dlg = vjp((do_ref[:, cols], ds_s[h]))
            dq_ref[:, cols], dk_ref[:, cols], dv_ref[:, cols] = dq, dk, dv
            ds_s[h] = ds
            dlg_ref[0, h] += dlg

    def chunk_of(d, n):
        return _scan_chunk(d, nc - 1 - n, nc, nlc)

    w = RET_HEADS * HEAD_DIM

    def blk(base):
        return pl.BlockSpec((c, w), lambda d, n: (chunk_of(d, n), base // RET_HEADS))

    out_blk = pl.BlockSpec((c, w), lambda d, n: (chunk_of(d, n), d))
    lg_blk = pl.BlockSpec((1, RET_HEADS, 1, 1), lambda d, n: (d, 0, 0, 0))
    grad_shape = jax.ShapeDtypeStruct((t, 2 * w), F32)
    return pl.pallas_call(
        body, name="ret_bwd", grid=(2, nc),
        in_specs=[blk(_RQ_BLK), blk(_RK_BLK), blk(_RV_BLK), lg_blk,
                  pl.BlockSpec((1, RET_HEADS, 1, HEAD_DIM, HEAD_DIM), lambda d, n: (d, 0, nc - 1 - n, 0, 0)),
                  pl.BlockSpec((c, w), lambda d, n: (chunk_of(d, n), 0))],
        out_specs=[out_blk, out_blk, out_blk, lg_blk],
        out_shape=[grad_shape, grad_shape, grad_shape, jax.ShapeDtypeStruct((2, RET_HEADS, 1, 1), F32)],
        scratch_shapes=[pltpu.VMEM((RET_HEADS, HEAD_DIM, HEAD_DIM), F32)],
        compiler_params=_params(("parallel", "arbitrary")),
    )(p, p, z, lg, states, do)


_GQ_BLK = P_GQ // (GLA_HEADS * GLA_DK)
_GK_BLK = Z_GK // (GLA_HEADS * GLA_DK)
_GV_BLK = Z_GV // (GLA_HEADS * GLA_DV)
_LA_BLK = P_LA // (GLA_HEADS * GLA_DK)


def _gla_mask(direction):
    c = GLA_CHUNK
    i = lax.broadcasted_iota(jnp.int32, (c, c), 0)
    j = lax.broadcasted_iota(jnp.int32, (c, c), 1)
    return (jnp.where(direction == 0, i - j, j - i) >= 0).astype(F32)


def gla_chunk(q, k, v, la, st, mask):
    b = mask_cumsum(mask, la)
    btot = jnp.sum(la, axis=0, keepdims=True)
    half = 0.5 * btot
    qt, kt = q * jnp.exp(b - half), k * jnp.exp(half - b)
    qs, ke = q * jnp.exp(b), k * jnp.exp(btot - b)
    outs, upd = [], []
    for h in range(GLA_HEADS):
        ks = slice(h * GLA_DK, (h + 1) * GLA_DK)
        vh = v[:, h * GLA_DV:(h + 1) * GLA_DV]
        att = bdot_nt(qt[:, ks], kt[:, ks]) * mask
        outs.append(bdot(att, vh) + bdot_nt(qs[:, ks], st[:, ks]))
        upd.append(bdot_tn(vh, ke[:, ks]))
    st_new = st * jnp.exp(btot) + jnp.concatenate(upd, axis=1)
    return jnp.concatenate(outs, axis=1), st_new


def gla_fwd(p, z, n_lat):
    t = p.shape[0]
    c = GLA_CHUNK
    nc, nlc = t // c, n_lat // c
    kw, vw = GLA_HEADS * GLA_DK, GLA_HEADS * GLA_DV

    def body(q_ref, k_ref, v_ref, la_ref, o_ref, ssave_ref, s_s):
        d, n = pl.program_id(0), pl.program_id(1)

        @pl.when(n == 0)
        def _():
            s_s[...] = jnp.zeros_like(s_s)

        ssave_ref[0, 0] = s_s[...]
        o, s_new = gla_chunk(q_ref[...], k_ref[...].astype(F32), v_ref[...].astype(F32), la_ref[...], s_s[...], _gla_mask(d))
        o_ref[...] = o
        s_s[...] = s_new

    def chunk_of(d, n):
        return _scan_chunk(d, n, nc, nlc)

    return pl.pallas_call(
        body, name="gla_fwd", grid=(2, nc),
        in_specs=[pl.BlockSpec((c, kw), lambda d, n: (chunk_of(d, n), _GQ_BLK)),
                  pl.BlockSpec((c, kw), lambda d, n: (chunk_of(d, n), _GK_BLK)),
                  pl.BlockSpec((c, vw), lambda d, n: (chunk_of(d, n), _GV_BLK)),
                  pl.BlockSpec((c, kw), lambda d, n: (chunk_of(d, n), _LA_BLK + d))],
        out_specs=[pl.BlockSpec((c, vw), lambda d, n: (chunk_of(d, n), d)),
                   pl.BlockSpec((1, 1, GLA_DV, kw), lambda d, n: (d, n, 0, 0))],
        out_shape=[jax.ShapeDtypeStruct((t, 2 * vw), F32), jax.ShapeDtypeStruct((2, nc, GLA_DV, kw), F32)],
        scratch_shapes=[pltpu.VMEM((GLA_DV, kw), F32)],
        compiler_params=_params(("parallel", "arbitrary")),
    )(p, z, z, p)


def gla_bwd(p, z, states, do, n_lat):
    t = p.shape[0]
    c = GLA_CHUNK
    nc, nlc = t // c, n_lat // c
    kw, vw = GLA_HEADS * GLA_DK, GLA_HEADS * GLA_DV

    def body(q_ref, k_ref, v_ref, la_ref, s_ref, do_ref, dq_ref, dk_ref, dv_ref, dla_ref, ds_s):
        d, n = pl.program_id(0), pl.program_id(1)

        @pl.when(n == 0)
        def _():
            ds_s[...] = jnp.zeros_like(ds_s)

        f = functools.partial(gla_chunk, mask=_gla_mask(d))
        _, vjp = jax.vjp(f, q_ref[...], k_ref[...].astype(F32), v_ref[...].astype(F32), la_ref[...], s_ref[0, 0])
        dq_ref[...], dk_ref[...], dv_ref[...], dla_ref[...], ds_s[...] = vjp((do_ref[...], ds_s[...]))

    def chunk_of(d, n):
        return _scan_chunk(d, nc - 1 - n, nc, nlc)

    k_out = pl.BlockSpec((c, kw), lambda d, n: (chunk_of(d, n), d))
    return pl.pallas_call(
        body, name="gla_bwd", grid=(2, nc),
        in_specs=[pl.BlockSpec((c, kw), lambda d, n: (chunk_of(d, n), _GQ_BLK)),
                  pl.BlockSpec((c, kw), lambda d, n: (chunk_of(d, n), _GK_BLK)),
                  pl.BlockSpec((c, vw), lambda d, n: (chunk_of(d, n), _GV_BLK)),
                  pl.BlockSpec((c, kw), lambda d, n: (chunk_of(d, n), _LA_BLK + d)),
                  pl.BlockSpec((1, 1, GLA_DV, kw), lambda d, n: (d, nc - 1 - n, 0, 0)),
                  pl.BlockSpec((c, vw), lambda d, n: (chunk_of(d, n), 0))],
        out_specs=[k_out, k_out, pl.BlockSpec((c, vw), lambda d, n: (chunk_of(d, n), d)), k_out],
        out_shape=[jax.ShapeDtypeStruct((t, 2 * kw), F32), jax.ShapeDtypeStruct((t, 2 * kw), F32),
                   jax.ShapeDtypeStruct((t, 2 * vw), F32), jax.ShapeDtypeStruct((t, 2 * kw), F32)],
        scratch_shapes=[pltpu.VMEM((GLA_DV, kw), F32)],
        compiler_params=_params(("parallel", "arbitrary")),
    )(p, z, z, p, states, do)


def _adam_tile(w, g, m, v):
    m = ADAM_B1 * m + (1.0 - ADAM_B1) * g
    v = ADAM_B2 * v + (1.0 - ADAM_B2) * (g * g)
    m_hat = m / (1.0 - ADAM_B1 ** ADAM_STEP)
    v_hat = v / (1.0 - ADAM_B2 ** ADAM_STEP)
    delta = -ADAM_LR * (m_hat / (jnp.sqrt(v_hat) + ADAM_EPS) + ADAM_WD * w)
    return delta, m, v


def adamw(name, w, g, m, v, after=None):
    shape = w.shape
    cols = shape[-1] if w.ndim > 1 and shape[-1] >= LANES else int(np.prod(shape))
    rows = int(np.prod(shape)) // cols
    tr = rows
    for cand in (512, 256, 128, 64, 32, 16, 8):
        if rows % cand == 0 and cand * cols * 4 <= (1 << 21):
            tr = cand
            break
    flat = [a.reshape(rows, cols) for a in (w, g, m, v)]

    behind = [] if after is None else [after]

    def body(w_ref, g_ref, m_ref, v_ref, *rest):
        d_ref, mo_ref, vo_ref = rest[len(behind):]
        d_ref[...], mo_ref[...], vo_ref[...] = _adam_tile(w_ref[...], g_ref[...], m_ref[...], v_ref[...])

    spec = pl.BlockSpec((tr, cols), lambda i: (i, 0))
    outs = pl.pallas_call(
        body, name=name, grid=(rows // tr,),
        in_specs=[spec] * 4 + [pl.BlockSpec(memory_space=pl.ANY)] * len(behind), out_specs=[spec] * 3,
        out_shape=[jax.ShapeDtypeStruct((rows, cols), F32)] * 3,
        compiler_params=_params(("parallel",)),
    )(*flat, *behind)
    return tuple(o.reshape(shape) for o in outs)


def adamw_layers(name, w, grads, m, v):
    depth, rows, cols = w.shape
    tr = _rows_tile(rows, cols)
    nb = rows // tr

    def body(w_ref, m_ref, v_ref, *rest):
        g_refs, (g_ref, d_ref, mo_ref, vo_ref) = rest[:depth], rest[depth:]
        l = pl.program_id(0)
        for k in range(depth):
            @pl.when(l == k)
            def _():
                g = g_refs[k][...]
                g_ref[...] = g
                d_ref[...], mo_ref[...], vo_ref[...] = _adam_tile(w_ref[...], g, m_ref[...], v_ref[...])

    def layer_grad(k):
        return pl.BlockSpec((tr, cols), lambda l, i: (jnp.where(l < k, 0, jnp.where(l == k, i, nb - 1)), 0))

    spec = pl.BlockSpec((tr, cols), lambda l, i: (l * nb + i, 0))
    flat = [a.reshape(depth * rows, cols) for a in (w, m, v)]
    g_all, delta, new_m, new_v = pl.pallas_call(
        body, name=name, grid=(depth, nb),
        in_specs=[spec] * 3 + [layer_grad(k) for k in range(depth)], out_specs=[spec] * 4,
        out_shape=[jax.ShapeDtypeStruct((depth * rows, cols), F32)] * 4,
        compiler_params=_params(("arbitrary", "arbitrary")),
    )(*flat, *grads)
    return tuple(a.reshape(w.shape) for a in (delta, new_m, new_v)), g_all.reshape(w.shape)


MESH = pl.DeviceIdType.MESH
_HBM = pl.BlockSpec(memory_space=pltpu.HBM)
N_CHIPS = 4
N_DEV = 8


def _place():
    x, y, c = lax.axis_index("x"), lax.axis_index("y"), lax.axis_index("c")
    chips = [(1 - x, y), (x, 1 - y), (1 - x, 1 - y)]
    return x, y, c, chips


def _remote(src, dst, send_sem, recv_sem, to):
    return pltpu.make_async_remote_copy(src_ref=src, dst_ref=dst, send_sem=send_sem, recv_sem=recv_sem,
                                        device_id=to, device_id_type=MESH)


def all_gather_small(name, v):
    m_per, n = v.shape

    def body(x_ref, out_ref, send_sems, recv_sems, local_sem):
        x, y, c, chips = _place()
        me, sibling = (x, y, c), (x, y, 1 - c)

        def rows(px, py, pc):
            return out_ref.at[pl.ds((4 * px + 2 * py + pc) * m_per, m_per), :]

        def copy(k, block, to, src=None):
            return _remote(rows(*block) if src is None else src, rows(*block), send_sems.at[k], recv_sems.at[k], to)

        mine = pltpu.make_async_copy(x_ref, rows(*me), local_sem)
        mine.start()
        first = [copy(0, me, sibling, src=x_ref)]
        first += [copy(1 + j, me, (*chip, c), src=x_ref) for j, chip in enumerate(chips)]
        for cp in first:
            cp.start()
        passed = [copy(4 + j, (*chip, c), sibling) for j, chip in enumerate(chips)]
        for j, chip in enumerate(chips):
            copy(1 + j, (*chip, c), me).wait_recv()
            passed[j].start()
        copy(0, sibling, me).wait_recv()
        for j, chip in enumerate(chips):
            copy(4 + j, (*chip, 1 - c), me).wait_recv()
        for cp in first + passed:
            cp.wait_send()
        mine.wait()

    return pl.pallas_call(
        body, name=name,
        out_shape=jax.ShapeDtypeStruct((N_DEV * m_per, n), v.dtype),
        in_specs=[pl.BlockSpec(memory_space=pltpu.VMEM)],
        out_specs=pl.BlockSpec(memory_space=pltpu.VMEM),
        scratch_shapes=[pltpu.SemaphoreType.DMA((7,)), pltpu.SemaphoreType.DMA((7,)), pltpu.SemaphoreType.DMA],
        compiler_params=pltpu.CompilerParams(vmem_limit_bytes=VMEM_LIMIT),
    )(v)


_SEM = pl.BlockSpec(memory_space=pltpu.SEMAPHORE)
_SPLIT_COPY = pltpu.CompilerParams(has_side_effects=pltpu.SideEffectType.DATAFLOW_SIDE_EFFECTING)


class CopyPlan(NamedTuple):
    copies: object
    n: int
    in_place: bool = False


def _gather_copies(x_ref, land_ref, x, y, c, chips):
    half = x_ref.shape[0] // 2
    rows = pl.ds(c * half, half)
    return [(x_ref.at[rows, :], land_ref.at[2 * x + y, rows, :], (*chip, c), land_ref.at[2 * chip[0] + chip[1], rows, :])
            for chip in chips]


def _pass_copies(land_ref, _, x, y, c, chips):
    half = land_ref.shape[1] // 2
    mine, other = pl.ds(c * half, half), pl.ds((1 - c) * half, half)
    return [(land_ref.at[2 * chip[0] + chip[1], mine, :], land_ref.at[2 * chip[0] + chip[1], mine, :], (x, y, 1 - c),
             land_ref.at[2 * chip[0] + chip[1], other, :]) for chip in chips]


def _sibling_half_copies(p_ref, land_ref, x, y, c, chips):
    half = p_ref.shape[1] // 2
    return [(p_ref.at[:, pl.ds((1 - c) * half, half), :], land_ref, (x, y, 1 - c), land_ref)]


def _scatter_copies(s_ref, land_ref, x, y, c, chips):
    return [(s_ref.at[2 * chip[0] + chip[1]], land_ref.at[j], (*chip, c), land_ref.at[j]) for j, chip in enumerate(chips)]


def _join_copies(buf_ref, _, x, y, c, chips):
    half = buf_ref.shape[0] // 2
    mine = buf_ref.at[pl.ds(c * half, half), :]
    return [(mine, mine, (x, y, 1 - c), buf_ref.at[pl.ds((1 - c) * half, half), :])]


GATHER = CopyPlan(_gather_copies, 3)
PASS_ON = CopyPlan(_pass_copies, 3, in_place=True)
SIBLING_HALF = CopyPlan(_sibling_half_copies, 1)
SCATTER = CopyPlan(_scatter_copies, 3)
JOIN = CopyPlan(_join_copies, 1, in_place=True)


def split_start(name, plan, srcs, land_shapes=None, after=None):
    nt = len(srcs)
    arrays = [pltpu.with_memory_space_constraint(s, pltpu.HBM) for s in srcs]
    if not plan.in_place:
        arrays += [pltpu.with_memory_space_constraint(lax.empty(shape, s.dtype), pltpu.HBM) for shape, s in zip(land_shapes, srcs)]
    na = len(arrays)
    behind = [] if after is None else [after]
    n_in = na + len(behind)

    def body(*refs):
        x_refs = refs[:nt]
        land_refs = x_refs if plan.in_place else refs[nt:na]
        send, recv = refs[n_in:n_in + nt], refs[n_in + nt:n_in + 2 * nt]
        x, y, c, chips = _place()
        for t in range(nt):
            for j, (src, dst, to, _) in enumerate(plan.copies(x_refs[t], land_refs[t], x, y, c, chips)):
                _remote(src, dst, send[t].at[j], recv[t].at[j], to).start()
        refs[-1][...] = jnp.zeros_like(refs[-1])

    outs = pl.pallas_call(
        body, name=name,
        out_shape=tuple([pltpu.SemaphoreType.DMA((plan.n,))] * (2 * nt) + [pltpu.HBM(a.shape, a.dtype) for a in arrays]
                        + [jax.ShapeDtypeStruct((8, LANES), F32)]),
        in_specs=[_HBM] * na + [pl.BlockSpec(memory_space=pl.ANY)] * len(behind),
        out_specs=tuple([_SEM] * (2 * nt) + [_HBM] * na + [pl.BlockSpec(memory_space=pltpu.VMEM)]),
        input_output_aliases={i: 2 * nt + i for i in range(na)},
        compiler_params=_SPLIT_COPY,
    )(*arrays, *behind)
    groups = [(outs[t], outs[nt + t]) + tuple(outs[2 * nt + t + k * nt] for k in range(na // nt)) for t in range(nt)]
    return groups, outs[-1]


def split_wait(name, plan, group, after):
    send, recv, *arrays = group
    na = len(arrays)

    def body(*refs):
        x_ref, land_ref = refs[0], refs[na - 1]
        send_sem, recv_sem = refs[na], refs[na + 1]
        x, y, c, chips = _place()
        for j, (s, _, to, arrival) in enumerate(plan.copies(x_ref, land_ref, x, y, c, chips)):
            cp = _remote(s, arrival, send_sem.at[j], recv_sem.at[j], to)
            cp.wait_send()
            cp.wait_recv()

    return pl.pallas_call(
        body, name=name,
        out_shape=tuple(pltpu.HBM(a.shape, a.dtype) for a in arrays),
        in_specs=tuple([_HBM] * na + [_SEM, _SEM, pl.BlockSpec(memory_space=pl.ANY)]), out_specs=tuple([_HBM] * na),
        input_output_aliases={i: i for i in range(na)}, compiler_params=_SPLIT_COPY,
    )(*arrays, send, recv, after)


def _rows_tile(rows, cols):
    for cand in (512, 256, 128, 64, 32, 16):
        if rows % cand == 0 and cand * cols * 4 <= (1 << 21):
            return cand
    return rows


def add_sibling_half(name, pieces, from_sibling, core):
    n, h, cols = from_sibling.shape
    tr = _rows_tile(h, cols // 4)
    nb = h // tr

    def body(c_ref, a_ref, b_ref, o_ref):
        o_ref[...] = (a_ref[...].astype(F32) + b_ref[...].astype(F32)).astype(o_ref.dtype)

    blk = pl.BlockSpec((1, tr, cols), lambda q, i, c_ref: (q, i, 0))
    return pl.pallas_call(
        body, name=name,
        grid_spec=pltpu.PrefetchScalarGridSpec(
            num_scalar_prefetch=1, grid=(n, nb),
            in_specs=[pl.BlockSpec((1, tr, cols), lambda q, i, c_ref: (q, c_ref[0] * nb + i, 0)), blk], out_specs=blk),
        out_shape=jax.ShapeDtypeStruct((n, h, cols), BF16),
        compiler_params=_params(("parallel", "parallel")),
    )(core.reshape(1).astype(jnp.int32), pieces, from_sibling)


def add_chip_sums(name, chip_sums, from_chips, chip, core):
    _, h, cols = chip_sums.shape
    tr = _rows_tile(h, cols // 2)
    nb = h // tr

    def body(s_ref, own_ref, r0_ref, r1_ref, r2_ref, o_ref):
        acc = own_ref[0].astype(F32) + r0_ref[0].astype(F32)
        o_ref[...] = acc + r1_ref[0].astype(F32) + r2_ref[0].astype(F32)

    def got(j):
        return pl.BlockSpec((1, tr, cols), lambda i, s_ref: (j, i, 0))

    return pl.pallas_call(
        body, name=name,
        grid_spec=pltpu.PrefetchScalarGridSpec(
            num_scalar_prefetch=1, grid=(nb,),
            in_specs=[pl.BlockSpec((1, tr, cols), lambda i, s_ref: (s_ref[0], i, 0)), got(0), got(1), got(2)],
            out_specs=pl.BlockSpec((tr, cols), lambda i, s_ref: (s_ref[1] * nb + i, 0))),
        out_shape=jax.ShapeDtypeStruct((2 * h, cols), F32),
        compiler_params=_params(("parallel",)),
    )(jnp.stack([chip, core]).astype(jnp.int32), chip_sums, from_chips, from_chips, from_chips)


def sum_device_blocks(name, g):
    n = g.shape[1]

    def body(g_ref, o_ref):
        acc = g_ref[0:8, :]
        for d in range(1, N_DEV):
            acc = acc + g_ref[8 * d:8 * (d + 1), :]
        o_ref[...] = acc

    return pl.pallas_call(body, name=name, out_shape=jax.ShapeDtypeStruct((8, n), F32),
                          compiler_params=pltpu.CompilerParams(vmem_limit_bytes=VMEM_LIMIT))(g)


class LayerWeights(NamedTuple):
    norm1_g: jax.Array
    q_g: jax.Array
    k_g: jax.Array
    lg: jax.Array
    ret_g: jax.Array
    gate_up: jax.Array
    gate_b: jax.Array
    gla_g: jax.Array
    norm2_g: jax.Array
    conv_w: jax.Array
    conv_b: jax.Array


def _mod(mods, k):
    return mods[:, k:k + 1, :]


def out_view(l, tb):
    rows = D_MODEL // N_CHIPS
    if tb:
        return BView(n=D_MODEL, k=D_MODEL, tn=rows, tk=D_MODEL, index_map=lambda i, j, kk: (j, l, kk))
    chips = tuple(functools.partial(lambda i, j, kk, q: (q, l, j), q=q) for q in range(N_CHIPS))
    return BView(n=D_MODEL, k=D_MODEL, tn=1024, tk=rows, index_map=None, part_maps=chips)


def down_view(l, f, tb):
    rows = f // N_CHIPS
    if tb:
        return BView(n=f, k=D_MODEL, tn=rows, tk=D_MODEL, index_map=lambda i, j, kk: (j, l, kk))
    chips = tuple(functools.partial(lambda i, j, kk, q: (q, l, j), q=q) for q in range(N_CHIPS))
    return BView(n=D_MODEL, k=f, tn=512, tk=rows, index_map=None, part_maps=chips)


def up_view(l, f, part=None):
    cols = 2 * f // N_CHIPS
    tc = _pick(cols, (1408, 1024, 512, 256))
    nbc = cols // tc
    if part is None:
        return BView(n=2 * f, k=D_MODEL, tn=tc, tk=D_MODEL, index_map=lambda i, j, kk: (j // nbc, l, j % nbc))
    nnb = D_MODEL // 512
    tiles = tuple(functools.partial(lambda i, j, kk, p: (2 * part + p // nbc, l * nnb + j, p % nbc), p=p) for p in range(2 * nbc))
    return BView(n=D_MODEL, k=f, tn=512, tk=tc, index_map=None, part_maps=tiles)


def up_grad_view(f, part, into):
    cols = f // 2
    tn = _pick(cols, (1408, 1024, 512, 256))
    nbc = cols // tn
    return OView((N_CHIPS, D_MODEL, cols), lambda i, j, kk: (2 * part + j // nbc, i, j % nbc), tn, into)


def ada_view(l, n_ada, tb):
    if tb:
        return BView(n=D_MODEL, k=n_ada, tn=1024, tk=n_ada, index_map=lambda i, j, kk: (l, j, 0))
    return BView(n=n_ada, k=D_MODEL, tn=1024, tk=D_MODEL, index_map=lambda i, j, kk: (l, 0, j))


def _prep_args(z, zg, cos, sin, w):
    rows = [Row(z, Z_AV, 0), Row(z, 512, Z_RQ // 512), Row(z, 512, Z_RK // 512), Row(z, 256, Z_GQ // 256),
            Row(zg, LANES, 0), Row(cos, HEAD_DIM, 0, False), Row(sin, HEAD_DIM, 0, False)]
    return rows, [Par(w.q_g), Par(w.k_g), Par(w.gate_up), Par(w.gate_b)]


def _post_args(o_att, o_ret, o_gla, z, w):
    rows = [Row(o_att, 1024), Row(o_ret, 512, 0), Row(o_ret, 512, 1, False), Row(o_gla, 512, 0), Row(o_gla, 512, 1, False),
            Row(z, 512, Z_RG // 512), Row(z, 512, Z_GR // 512)]
    return rows, [Par(w.ret_g), Par(w.gla_g)]


def layer_fwd(l, xs, mods, w, fetch, cos, sin, n_lat, n_out):
    t, d = xs.shape
    tag = f"l{l}_"
    nm1 = [Par(w.norm1_g), Par(_mod(mods, 0), True), Par(_mod(mods, 1), True)]
    (h,) = row_map(tag + "norm1", normmod_tile, [Row(xs, d)], nm1, [(d, BF16)], t, n_lat)
    (w_main, w_gate), started = fetch("w_in", h)
    z = matmul(tag + "in_proj", h, w_main, after=started, out_dtype=BF16)
    zg = matmul(tag + "gate_proj", h, w_gate)
    rows, pars = _prep_args(z, zg, cos, sin, w)
    (p,) = row_map(tag + "prep", prep_tile, rows, pars, [(P_W, F32)], t, n_lat)
    o_att, lse = attn_fwd(p, z, n_lat)
    o_ret, s_ret = ret_fwd(p, z, w.lg, n_lat)
    o_gla, s_gla = gla_fwd(p, z, n_lat)
    rows, pars = _post_args(o_att, o_ret, o_gla, z, w)
    (m,) = row_map(tag + "post", post_tile, rows, pars, [(d, BF16)], t, n_lat)
    m = m[:n_out]
    g_out, started = fetch("w_out", m)
    y = matmul(tag + "out_proj", m, g_out, view=out_view(0, False), after=started)
    rn = [Par(_mod(mods, 2), True), Par(w.norm2_g), Par(_mod(mods, 3), True), Par(_mod(mods, 4), True)]
    x1, h2 = row_map(tag + "resid1_norm2", resid_norm_tile, [Row(xs, d), Row(y, d)], rn, [(d, F32), (d, BF16)], n_out, n_lat)
    f = w.conv_b.shape[1]
    g_up, started = fetch("w_up", h2)
    u = matmul(tag + "up_proj", h2, g_up, view=up_view(0, f), after=started, out_dtype=BF16)
    g = convglu(tag + "convglu", u, w.conv_w, w.conv_b, n_lat)
    g_down, started = fetch("w_down", g)
    yd = matmul(tag + "down_proj", g, g_down, view=down_view(0, f, False), after=started)
    (x2,) = row_map(tag + "resid2", resid_tile, [Row(x1, d), Row(yd, d)], [Par(_mod(mods, 5), True)], [(d, F32)], n_out, n_lat)
    saved = dict(xs=xs, h=h, z=z, zg=zg, p=p, o_att=o_att, lse=lse, o_ret=o_ret, s_ret=s_ret, o_gla=o_gla, s_gla=s_gla,
                 m=m, y=y, x1=x1, h2=h2, u=u, g=g, yd=yd, w_main=w_main, w_gate=w_gate, g_out=g_out, g_up=g_up, g_down=g_down)
    return x2, saved


def _sum_dirs(a):
    w = a.shape[1] // 2
    return a[:, :w] + a[:, w:]


def layer_bwd(l, dx2, s, mods, w, cos, sin, n_lat, grad_ready):
    (t, d), n_out = s["xs"].shape, dx2.shape[0]
    tag = f"l{l}_b_"

    def all_rows(a):
        return a if n_out == t else jnp.pad(a, ((0, t - n_out), (0, 0)))

    dyd, dgate5 = row_vjp(tag + "resid2", gated_tile, [Row(s["yd"], d)], [Par(_mod(mods, 5), True)], [dx2], n_out, n_lat,
                          row_grad_dtype=BF16)
    f = w.conv_b.shape[1]
    dg = matmul(tag + "down_dx", dyd, s["g_down"], tb=True, view=down_view(0, f, True))
    dw_down = matmul(tag + "down_dw", s["g"], dyd, ta=True, out_dtype=BF16)
    da, dv, dcw, dcb = convglu_bwd(tag + "convglu", s["u"], w.conv_w, w.conv_b, dg, n_lat)
    dh2 = matmul(tag + "up_dx_gate", da, s["g_up"], tb=True, view=up_view(0, f, 0))
    dh2 = matmul(tag + "up_dx_value", dv, s["g_up"], tb=True, view=up_view(0, f, 1), add=dh2)
    dw_up = matmul(tag + "up_dw_gate", s["h2"], da, ta=True, out_dtype=BF16, o_view=up_grad_view(f, 0, None))
    dw_up = matmul(tag + "up_dw_value", s["h2"], dv, ta=True, out_dtype=BF16, o_view=up_grad_view(f, 1, dw_up))
    started = grad_ready("ffn", dict(w_up=dw_up, w_down=dw_down))
    rn = [Par(_mod(mods, 2), True), Par(w.norm2_g), Par(_mod(mods, 3), True), Par(_mod(mods, 4), True)]
    dx1, dy, dgate2, dg2, dshift3, dscale4 = row_vjp(
        tag + "resid1_norm2", resid_norm_tile, [Row(s["xs"], d), Row(s["y"], d)], rn, [dx2, dh2], n_out, n_lat,
        row_grad_dtype=(F32, BF16), after=started)
    if n_out < t:
        dgate5, dshift3, dscale4, dgate2 = [g.at[1].set(0.0) for g in (dgate5, dshift3, dscale4, dgate2)]
    dm = matmul(tag + "out_dx", dy, s["g_out"], tb=True, view=out_view(0, True))
    dw_out = matmul(tag + "out_dw", s["m"], dy, ta=True, out_dtype=BF16)
    rows, pars = _post_args(s["o_att"], s["o_ret"], s["o_gla"], s["z"], w)
    started = grad_ready("w_out", dict(w_out=dw_out))
    do_att, do_ret, do_gla, d_rg, d_gr, d_ret_g, d_gla_g = row_vjp(tag + "post", post_tile, rows, pars, [dm], n_out, n_lat, after=started)
    do_att, do_ret, do_gla, d_rg, d_gr, dx1 = [all_rows(a) for a in (do_att, do_ret, do_gla, d_rg, d_gr, dx1)]
    dq_a, dk_a, dv_a = attn_bwd(s["p"], s["z"], s["o_att"], s["lse"], do_att, n_lat)
    dq_r, dk_r, dv_r, dlg = ret_bwd(s["p"], s["z"], w.lg, s["s_ret"], do_ret, n_lat)
    dq_g, dk_g, dv_g, dla = gla_bwd(s["p"], s["z"], s["s_gla"], do_gla, n_lat)
    dp = jnp.concatenate([dq_a, dk_a, _sum_dirs(dq_g), _sum_dirs(dq_r), _sum_dirs(dk_r), dla], axis=1)
    rows, pars = _prep_args(s["z"], s["zg"], cos, sin, w)
    d_zqk, d_zrq, d_zrk, d_zgq, dzg, d_qg, d_kg, d_up, d_gb = row_vjp(tag + "prep", prep_tile, rows, pars, [dp], t, n_lat)
    dz = jnp.concatenate([d_zqk, dv_a, d_zrq, d_zrk, _sum_dirs(dv_r), d_rg, d_zgq, _sum_dirs(dk_g), _sum_dirs(dv_g), d_gr], axis=1)
    dz, dzg = dz.astype(BF16), dzg.astype(BF16)
    dh_gate = matmul(tag + "gate_dx", dzg, s["w_gate"], tb=True)
    dh = matmul(tag + "in_dx", dz, s["w_main"], tb=True, add=dh_gate)
    dw_main = matmul(tag + "in_dw", s["h"], dz, ta=True, out_dtype=BF16)
    dw_gate = matmul(tag + "gate_dw", s["h"], dzg, ta=True, out_dtype=BF16)
    started = grad_ready("w_in", dict(w_main=dw_main, w_gate=dw_gate))
    nm1 = [Par(w.norm1_g), Par(_mod(mods, 0), True), Par(_mod(mods, 1), True)]
    dx, dg1, dshift0, dscale1 = row_vjp(tag + "norm1", normmod_tile, [Row(s["xs"], d)], nm1, [dh], t, n_lat,
                                        add_to_first=dx1, after=started)
    dmods = jnp.concatenate([dshift0, dscale1, dgate2, dshift3, dscale4, dgate5], axis=1)
    grads = dict(w_main=dw_main, w_gate=dw_gate, w_out=dw_out, w_up=dw_up, w_down=dw_down, norm1_g=dg1, q_g=d_qg, k_g=d_kg,
                 lg=dlg, ret_g=d_ret_g, gate_up=d_up, gate_b=d_gb, gla_g=d_gla_g, norm2_g=dg2, conv_w=dcw, conv_b=dcb)
    return dx, dmods, grads


def rope_tables(n_lat, n_ctx):
    rows = n_lat // GRID_W
    row = jnp.repeat(jnp.arange(rows, dtype=F32), GRID_W)
    col = jnp.tile(jnp.arange(GRID_W, dtype=F32), rows)
    n_freq = HEAD_DIM // 4
    inv_freq = ROPE_THETA ** (-jnp.arange(n_freq, dtype=F32) / n_freq)
    ang = jnp.concatenate([row[:, None] * inv_freq, col[:, None] * inv_freq], axis=-1)
    cos, sin = jnp.cos(ang), jnp.sin(ang)
    cos = jnp.concatenate([jnp.concatenate([cos, cos], axis=1), jnp.ones((n_ctx, HEAD_DIM), F32)], axis=0)
    sin = jnp.concatenate([jnp.concatenate([-sin, sin], axis=1), jnp.zeros((n_ctx, HEAD_DIM), F32)], axis=0)
    return cos, sin


def local_step(xs, target, mods, weights, fetch, final_g, n_lat, grad_ready):
    t, d = xs.shape
    cos, sin = rope_tables(n_lat, t - n_lat)
    saved = []
    h = xs
    for l, w in enumerate(weights):
        n_out = t if l + 1 < len(weights) else n_lat
        h, s = layer_fwd(l, h, mods[l], w, functools.partial(fetch, l), cos, sin, n_lat, n_out)
        saved.append(s)
    loss, dx, dgf = final_loss(h, target, final_g, n_lat)
    dmods, grads = [None] * len(weights), [None] * len(weights)
    for l in reversed(range(len(weights))):
        dx, dmods[l], grads[l] = layer_bwd(l, dx, saved[l], mods[l], weights[l], cos, sin, n_lat, functools.partial(grad_ready, l))
    return loss, dx, dmods, grads, dgf


WEIGHT_NAMES = ("c_ctx", "ada_w", "ada_b", "norm1_g", "w_in", "q_norm_g", "k_norm_g", "ret_log_decay", "ret_norm_g",
                "gla_gate_up", "gla_gate_b", "gla_norm_g", "w_out", "norm2_g", "w_up", "conv_w", "conv_b", "w_down", "final_norm_g")
PACK_QUANTUM = 8 * LANES


def _pack(arrays):
    flat = jnp.concatenate([a.reshape(-1).astype(F32) for a in arrays])
    n = -(-flat.shape[0] // PACK_QUANTUM) * PACK_QUANTUM
    return jnp.pad(flat, (0, n - flat.shape[0])).reshape(8, n // 8)


def _unpack(flat2d, shapes):
    out, at = [], 0
    for s in shapes:
        size = int(np.prod(s))
        out.append(flat2d[:, at:at + size].reshape((flat2d.shape[0],) + tuple(s)))
        at += size
    return out


def _per_device(gathered):
    return gathered.reshape(N_DEV, -1)


def _from_chips(per_device, axis):
    chips = per_device[0::2]
    moved = jnp.moveaxis(chips, 0, axis)
    shape = moved.shape
    return moved.reshape(shape[:axis] + (shape[axis] * shape[axis + 1],) + shape[axis + 2:])


def kernel(x, c, ctx, c_ctx, ada_w, ada_b, norm1_g, w_in, q_norm_g, k_norm_g, ret_log_decay, ret_norm_g, gla_gate_up, gla_gate_b, gla_norm_g, w_out, norm2_g, w_up, conv_w, conv_b, w_down, final_norm_g, loss_target, m_c_ctx, m_ada_w, m_ada_b, m_norm1_g, m_w_in, m_q_norm_g, m_k_norm_g, m_ret_log_decay, m_ret_norm_g, m_gla_gate_up, m_gla_gate_b, m_gla_norm_g, m_w_out, m_norm2_g, m_w_up, m_conv_w, m_conv_b, m_w_down, m_final_norm_g, v_c_ctx, v_ada_w, v_ada_b, v_norm1_g, v_w_in, v_q_norm_g, v_k_norm_g, v_ret_log_decay, v_ret_norm_g, v_gla_gate_up, v_gla_gate_b, v_gla_norm_g, v_w_out, v_norm2_g, v_w_up, v_conv_w, v_conv_b, v_w_down, v_final_norm_g):
    weights = dict(zip(WEIGHT_NAMES, (c_ctx, ada_w, ada_b, norm1_g, w_in, q_norm_g, k_norm_g, ret_log_decay, ret_norm_g,
                                      gla_gate_up, gla_gate_b, gla_norm_g, w_out, norm2_g, w_up, conv_w, conv_b, w_down, final_norm_g)))
    mom_m = dict(zip(WEIGHT_NAMES, (m_c_ctx, m_ada_w, m_ada_b, m_norm1_g, m_w_in, m_q_norm_g, m_k_norm_g, m_ret_log_decay, m_ret_norm_g,
                                    m_gla_gate_up, m_gla_gate_b, m_gla_norm_g, m_w_out, m_norm2_g, m_w_up, m_conv_w, m_conv_b, m_w_down, m_final_norm_g)))
    mom_v = dict(zip(WEIGHT_NAMES, (v_c_ctx, v_ada_w, v_ada_b, v_norm1_g, v_w_in, v_q_norm_g, v_k_norm_g, v_ret_log_decay, v_ret_norm_g,
                                    v_gla_gate_up, v_gla_gate_b, v_gla_norm_g, v_w_out, v_norm2_g, v_w_up, v_conv_w, v_conv_b, v_w_down, v_final_norm_g)))
    depth, d = norm1_g.shape
    assert d == D_MODEL and x.shape[0] == 1
    n_lat, n_ctx, f = x.shape[1], ctx.shape[1], conv_b.shape[1]
    assert n_lat % ROW_TILE == 0 and n_ctx % ROW_TILE == 0 and f % FFN_COL_TILE == 0 and f % N_CHIPS == 0
    n_in = w_in.shape[2]
    n_ada = ada_w.shape[2]
    xi, yi, ci = lax.axis_index("x"), lax.axis_index("y"), lax.axis_index("c")
    chip = 2 * xi + yi
    dev = 2 * chip + ci

    big = ("w_in", "w_out", "w_up", "w_down")
    order = [(l, name) for l in range(depth) for name in big]
    passing = {}

    def pass_on(k, after):
        tag = "{1}{0}".format(*order[k])
        own, land = split_wait("gather_wait_" + tag, GATHER, in_flight[k], after)
        (moving,), started = split_start("gather_pass_" + tag, PASS_ON, [land])
        passing[k] = (own, moving)
        return started

    def fetch(l, name, after):
        k = order.index((l, name))
        if k == 0:
            pass_on(0, after)
        own, moving = passing.pop(k)
        (land,) = split_wait(f"gather_pass_wait_{name}{l}", PASS_ON, moving, after)
        started = pass_on(k + 1, after) if k + 1 < len(order) else None
        land = lax.dynamic_update_slice_in_dim(land, own[None], chip, axis=0)
        if name != "w_in":
            return land, started
        last = N_MAIN - (N_CHIPS - 1) * n_in
        w_main = jnp.concatenate([land[q] for q in range(N_CHIPS - 1)] + [land[N_CHIPS - 1][:, :last]], axis=1)
        return (w_main, jnp.pad(land[N_CHIPS - 1][:, last:], ((0, 0), (0, LANES - N_GATE)))), started

    small_shapes = [c.shape[1:], conv_w.shape, gla_gate_up.shape, gla_gate_b.shape]
    got = _per_device(all_gather_small("gather_small", _pack([c, conv_w, gla_gate_up, gla_gate_b])))
    c_all, conv_w_sh, gate_up_sh, gate_b_sh = _unpack(got, small_shapes)
    conv_w_full = _from_chips(conv_w_sh, 2)
    gate_up_full = _from_chips(gate_up_sh, 3)
    gate_b_full = _from_chips(gate_b_sh, 2)

    act = jnp.zeros((16, d), F32).at[0:N_DEV].set(jax.nn.silu(c_all)).at[N_DEV].set(jax.nn.silu(c_ctx))
    mod_sh = jnp.stack([matmul(f"ada_fwd{l}", act, ada_w, view=ada_view(l, n_ada, False)) for l in range(depth)])
    got = _per_device(all_gather_small("gather_mods", _pack([mod_sh])))
    (mod_sh_all,) = _unpack(got, [mod_sh.shape])
    mod_full = _from_chips(mod_sh_all, 2) + ada_b[:, None, :]
    mod_mine = lax.dynamic_index_in_dim(mod_full, dev, axis=1, keepdims=False)
    mods = [jnp.stack([mod_mine[l].reshape(N_MOD, d), mod_full[l, N_DEV].reshape(N_MOD, d)]) for l in range(depth)]
    in_flight, token = [], None
    batches = [order[:1], order[1:len(big)]] + [order[l * len(big):(l + 1) * len(big)] for l in range(1, depth)]
    for b, batch in enumerate(batches):
        behind = 0.0 if token is None else token[0, 0]
        shards = [(weights[name][l] + behind).astype(BF16) for l, name in batch]
        started, token = split_start(f"gather_start{b}", GATHER, shards, [(N_CHIPS,) + s.shape for s in shards],
                                     after=mod_full if b == 0 else None)
        in_flight += started

    layer_w = []
    for l in range(depth):
        up = jnp.zeros((2, LANES, GLA_HEADS * GLA_DK), F32)
        up = up.at[0, 0:GLA_RANK].set(gate_up_full[l, 0]).at[1, GLA_RANK:2 * GLA_RANK].set(gate_up_full[l, 1])
        layer_w.append(LayerWeights(
            norm1_g=norm1_g[l].reshape(1, 1, d), q_g=q_norm_g[l].reshape(1, 1, HEAD_DIM), k_g=k_norm_g[l].reshape(1, 1, HEAD_DIM),
            lg=ret_log_decay[l].reshape(2, RET_HEADS, 1, 1), ret_g=ret_norm_g[l].reshape(1, 1, HEAD_DIM),
            gate_up=up.reshape(1, 2 * LANES, -1), gate_b=gate_b_full[l].reshape(1, 2, -1), gla_g=gla_norm_g[l].reshape(1, 1, HEAD_DIM),
            norm2_g=norm2_g[l].reshape(1, 1, d), conv_w=conv_w_full[l], conv_b=conv_b[l].reshape(1, f)))

    def pieces_of(name, g):
        if name == "w_in":
            shards = [g["w_main"][:, q * n_in:(q + 1) * n_in] for q in range(N_CHIPS - 1)]
            tail = jnp.concatenate([g["w_main"][:, (N_CHIPS - 1) * n_in:], g["w_gate"][:, :N_GATE]], axis=1)
            return jnp.stack(shards + [tail])
        if name == "w_up":
            return g["w_up"]
        return g[name].reshape(N_CHIPS, -1, d)

    groups = {"ffn": ("w_up", "w_down"), "w_out": ("w_out",), "w_in": ("w_in",)}
    reducing = {}
    to_sibling = []

    def sibling_arrived(after):
        started = None
        while to_sibling:
            l, group, in_flight_halves = to_sibling.pop(0)
            sums = []
            for name, halves in zip(groups[group], in_flight_halves):
                pieces, from_sibling = split_wait(f"rs_sibling_wait_{name}{l}", SIBLING_HALF, halves, after)
                sums.append(add_sibling_half(f"rs_add_sibling_{name}{l}", pieces, from_sibling, ci))
            in_flight_sums, token = split_start(f"rs_start_{group}{l}", SCATTER, sums, [(3,) + s.shape[1:] for s in sums])
            reducing.update({(l, name): grp for name, grp in zip(groups[group], in_flight_sums)})
            started = token if started is None else started + token
        return started

    def grad_ready(l, group, g):
        pieces = [pieces_of(name, g) for name in groups[group]]
        before = None if (l, group) == (0, "w_in") else sibling_arrived(pieces[0])
        in_flight_halves, started = split_start(f"rs_sibling_{group}{l}", SIBLING_HALF, pieces,
                                                [(N_CHIPS, pc.shape[1] // 2, pc.shape[2]) for pc in pieces])
        to_sibling.append((l, group, in_flight_halves))
        return started if before is None else started + before

    xs = jnp.concatenate([x[0], ctx[0]], axis=0) + token[0, 0]
    loss, dx, dmods, grads, dgf = local_step(xs, loss_target[0], mods, layer_w, fetch, final_norm_g.reshape(1, d), n_lat, grad_ready)

    def gate_up_grad(g):
        return jnp.stack([g[0, 0:GLA_RANK], g[0, LANES + GLA_RANK:LANES + 2 * GLA_RANK]])

    per_layer = [[dmods[l][0], dmods[l][1], grads[l]["norm1_g"], grads[l]["norm2_g"], grads[l]["q_g"], grads[l]["k_g"],
                  grads[l]["ret_g"], grads[l]["gla_g"], grads[l]["lg"], gate_up_grad(grads[l]["gate_up"]), grads[l]["gate_b"],
                  grads[l]["conv_w"], grads[l]["conv_b"]] for l in range(depth)]
    layer_shapes = [(N_MOD * d,), (N_MOD * d,), (d,), (d,), (HEAD_DIM,), (HEAD_DIM,), (HEAD_DIM,), (HEAD_DIM,), (2, RET_HEADS),
                    (2, GLA_RANK, GLA_HEADS * GLA_DK), (2, GLA_HEADS * GLA_DK), (3, f), (f,)]
    packed = _pack([a for lay in per_layer for a in lay] + [dgf, loss[0, 0:1]])
    gathered = all_gather_small("gather_small_grads", packed)
    every = _unpack(_per_device(gathered), layer_shapes * depth + [(d,), (1,)])
    total = _unpack(sum_device_blocks("sum_small_grads", gathered).reshape(1, -1), layer_shapes * depth + [(d,), (1,)])
    nl = len(layer_shapes)

    def tot(l, k):
        return total[l * nl + k][0]

    out = {"norm1_g": jnp.stack([tot(l, 2) for l in range(depth)]), "norm2_g": jnp.stack([tot(l, 3) for l in range(depth)]),
           "q_norm_g": jnp.stack([tot(l, 4) for l in range(depth)]), "k_norm_g": jnp.stack([tot(l, 5) for l in range(depth)]),
           "ret_norm_g": jnp.stack([tot(l, 6) for l in range(depth)]), "gla_norm_g": jnp.stack([tot(l, 7) for l in range(depth)]),
           "ret_log_decay": jnp.stack([tot(l, 8) for l in range(depth)]),
           "gla_gate_up": lax.dynamic_slice_in_dim(jnp.stack([tot(l, 9) for l in range(depth)]), chip * gla_gate_up.shape[3], gla_gate_up.shape[3], axis=3),
           "gla_gate_b": lax.dynamic_slice_in_dim(jnp.stack([tot(l, 10) for l in range(depth)]), chip * gla_gate_b.shape[2], gla_gate_b.shape[2], axis=2),
           "conv_w": lax.dynamic_slice_in_dim(jnp.stack([tot(l, 11) for l in range(depth)]), chip * conv_w.shape[2], conv_w.shape[2], axis=2),
           "conv_b": jnp.stack([tot(l, 12) for l in range(depth)]),
           "final_norm_g": total[depth * nl][0],
           "ada_b": jnp.stack([tot(l, 0) + tot(l, 1) for l in range(depth)])}
    loss_total = total[depth * nl + 1][0, 0]

    dmod_all = jnp.zeros((depth, 16, N_MOD * d), F32)
    for l in range(depth):
        dmod_all = dmod_all.at[l, 0:N_DEV].set(every[l * nl][:, :]).at[l, N_DEV].set(tot(l, 1))
    dmod_cols = lax.dynamic_slice_in_dim(dmod_all, chip * n_ada, n_ada, axis=2)
    for l in range(depth):
        slab = OView((depth, d, n_ada), functools.partial(lambda i, j, kk, l: (l, i, j), l=l), None, out.get("ada_w"))
        out["ada_w"] = matmul(f"ada_dw{l}", act, dmod_cols[l], ta=True, o_view=slab)
    dact = matmul("ada_dx0", dmod_cols[0], ada_w, tb=True, view=ada_view(0, n_ada, True))
    for l in range(1, depth):
        dact = matmul(f"ada_dx{l}", dmod_cols[l], ada_w, tb=True, view=ada_view(l, n_ada, True), add=dact)
    got = _per_device(all_gather_small("gather_dcctx", _pack([dact[N_DEV]])))
    last_started = sibling_arrived(got)
    got = got[0::2, :d]
    dsilu = got[0] + got[1] + got[2] + got[3]
    sig = jax.nn.sigmoid(c_ctx)
    out["c_ctx"] = dsilu * (sig + c_ctx * sig * (1.0 - sig))

    deltas, new_m, new_v = {}, {}, {}

    def update(name):
        out[name] = out[name].reshape(weights[name].shape)
        deltas[name], new_m[name], new_v[name] = adamw("adamw_" + name, weights[name], out[name], mom_m[name], mom_v[name],
                                                       after=last_started)

    for name in WEIGHT_NAMES:
        if name not in big:
            update(name)
    behind = new_v["ada_w"]
    joining = []

    def joined(after):
        name, in_flight_halves = joining.pop()
        per_layer = [split_wait(f"rs_join_wait_{name}{l}", JOIN, grp, after)[0] for l, grp in enumerate(in_flight_halves)]
        (deltas[name], new_m[name], new_v[name]), out[name] = adamw_layers(
            "adamw_" + name, weights[name], per_layer, mom_m[name], mom_v[name])
        return new_v[name]

    for name in ("w_down", "w_up", "w_out", "w_in"):
        halves = []
        for l in range(depth):
            sums, got = split_wait(f"rs_wait_{name}{l}", SCATTER, reducing[(l, name)], behind)
            halves.append(add_chip_sums(f"rs_add_chips_{name}{l}", sums, got, chip, ci))
        in_flight_halves, _ = split_start("rs_join_" + name, JOIN, halves)
        if joining:
            behind = joined(behind)
        joining.append((name, in_flight_halves))
    joined(behind)
    grad_x = dx[:n_lat].reshape(x.shape)
    return (loss_total, grad_x, *[out[n] for n in WEIGHT_NAMES], *[deltas[n] for n in WEIGHT_NAMES],
            *[new_m[n] for n in WEIGHT_NAMES], *[new_v[n] for n in WEIGHT_NAMES])
```
